```python
import math
import jax, jax.numpy as jnp
from jax import lax
import numpy as np

D_MODEL = 1024
BATCH = 8
SEQ = 4096
DEPTH = 2
DEC_BATCH = 32
DEC_SEQ = 8
PAST_LEN = 16384
PAGE_SIZE = 128

BRANCH_WIDTH = 512
N_BRANCH = 3
M_HEADS = 4
M_DH = BRANCH_WIDTH // M_HEADS
M_WIDTH = M_HEADS * M_DH
M_CHUNK = 64
NSA_HEADS = 8
NSA_DH = BRANCH_WIDTH // NSA_HEADS
NSA_WIDTH = NSA_HEADS * NSA_DH
NSA_GROUPS = 2
NSA_HPG = NSA_HEADS // NSA_GROUPS
NSA_KV = NSA_GROUPS * NSA_DH
CMP_LEN = 32
CMP_STRIDE = 16
SEL_LEN = 64
SEL_TOPN = 16
WINDOW = 512
Q_BLOCK = 64
MEM_LEN = 256
X_HEADS = 4
X_DH = BRANCH_WIDTH // X_HEADS
X_WIDTH = X_HEADS * X_DH
D_FF = 2816
N_EXPERTS = 8
TOP_K = 2
D_FF_EXPERT = 3584
N_DENSE = (DEPTH + 1) // 2
N_MOE = DEPTH // 2
ALPHA = (2.0 * DEPTH) ** 0.25
BETA = (8.0 * DEPTH) ** -0.25
LN_EPS = 1e-5

IN_SPLITS = (M_WIDTH, M_WIDTH, M_WIDTH, M_HEADS, M_HEADS, M_WIDTH,
             NSA_WIDTH, 6 * NSA_KV, 3 * NSA_HEADS,
             X_WIDTH,
             N_BRANCH * D_MODEL)
N_IN = sum(IN_SPLITS)

kernel_name = 'hybrid_mlstm_nsa_memxattn_decode_step'


def split_cols(z, sizes):
    return jnp.split(z, np.cumsum(sizes)[:-1].tolist(), axis=-1)


def alibi_slopes(n):
    return jnp.asarray([2.0 ** (-8.0 * (h + 1) / n) for h in range(n)], jnp.float32)


def layer_norm(x, g, b):
    xf = x.astype(jnp.float32)
    mu = xf.mean(-1, keepdims=True)
    var = jnp.square(xf - mu).mean(-1, keepdims=True)
    return ((xf - mu) * lax.rsqrt(var + LN_EPS) * g.astype(jnp.float32) + b.astype(jnp.float32)).astype(x.dtype)


def post_norm(x, f, g, b):
    return layer_norm(ALPHA * x + f, g, b)


def masked_softmax(s, mask, axis=-1):
    s = jnp.where(mask, s, -jnp.inf)
    mx = jnp.max(s, axis=axis, keepdims=True)
    mx = jnp.where(jnp.isfinite(mx), mx, 0.0)
    p = jnp.where(mask, jnp.exp(s - mx), 0.0)
    return p / jnp.maximum(p.sum(axis=axis, keepdims=True), 1e-30)


def mlstm_chunk(carry, inp):
    C, n, m = carry
    q, k, v, ig, lf = inp
    L = q.shape[2]
    b = jnp.cumsum(lf, axis=-1)
    causal = jnp.tril(jnp.ones((L, L), bool))
    dmat = jnp.where(causal, b[..., :, None] - b[..., None, :] + ig[..., None, :], -jnp.inf)
    inter = b + m[..., None]
    m_t = jnp.maximum(inter, dmat.max(-1))
    s = jnp.einsum('bhtd,bhsd->bhts', q, k) * jnp.exp(dmat - m_t[..., None])
    sc_in = jnp.exp(inter - m_t)
    num = sc_in[..., None] * jnp.einsum('bhvk,bhtk->bhtv', C, q) + jnp.einsum('bhts,bhsv->bhtv', s, v)
    den = sc_in * jnp.einsum('bhk,bhtk->bht', n, q) + s.sum(-1)
    h = num / jnp.maximum(jnp.abs(den), jnp.exp(-m_t))[..., None]
    b_last = b[..., -1]
    dec = b_last[..., None] - b + ig
    m_new = jnp.maximum(b_last + m, dec.max(-1))
    ws = jnp.exp(dec - m_new[..., None])
    sc = jnp.exp(b_last + m - m_new)
    C_new = sc[..., None, None] * C + jnp.einsum('bhs,bhsv,bhsk->bhvk', ws, v, k)
    n_new = sc[..., None] * n + jnp.einsum('bhs,bhsk->bhk', ws, k)
    return (C_new, n_new, m_new), h


def mlstm_branch(q, k, v, ig, fg, og, norm_g, C, n, m):
    B, T, _ = q.shape
    L = math.gcd(T, M_CHUNK)
    nc = T // L

    def heads(z):
        return z.astype(jnp.float32).reshape(B, nc, L, M_HEADS, M_DH).transpose(1, 0, 3, 2, 4)

    def gates(z):
        return z.astype(jnp.float32).reshape(B, nc, L, M_HEADS).transpose(1, 0, 3, 2)

    carry = (C.astype(jnp.float32), n.astype(jnp.float32), m.astype(jnp.float32))
    xs = (heads(q), heads(k) * M_DH ** -0.5, heads(v), gates(ig), jax.nn.log_sigmoid(gates(fg)))
    (C, n, m), h = lax.scan(mlstm_chunk, carry, xs)
    h = h.transpose(1, 0, 3, 2, 4).reshape(B, T, M_HEADS, M_DH)
    h = h * jax.nn.sigmoid(og.astype(jnp.float32)).reshape(B, T, M_HEADS, M_DH)
    mu = h.mean(-1, keepdims=True)
    var = jnp.square(h - mu).mean(-1, keepdims=True)
    h = ((h - mu) * lax.rsqrt(var + LN_EPS)).reshape(B, T, M_WIDTH) * norm_g.astype(jnp.float32)
    return h, (C, n, m)


def nsa_branch(q, g, kv_all, win_src, cmp_w, cmp_b, pos0):
    B, T, _ = q.shape
    G, P, dh = NSA_GROUPS, NSA_HPG, NSA_DH
    Tk = kv_all.shape[1]
    Tp = -(-Tk // SEL_LEN) * SEL_LEN
    kv = jnp.pad(kv_all, ((0, 0), (0, Tp - Tk), (0, 0), (0, 0), (0, 0))).astype(jnp.float32)
    slope = alibi_slopes(NSA_HEADS).reshape(G, P)
    R = CMP_LEN // CMP_STRIDE
    n_sub = Tp // CMP_STRIDE
    n_cmp = n_sub - R + 1
    sub = kv[:, :, :2].reshape(B, n_sub, CMP_STRIDE, 2, G, dh)
    cw = cmp_w.astype(jnp.float32).reshape(2, R, CMP_STRIDE, dh, dh)
    kvc = cmp_b.astype(jnp.float32)[:, None, :]
    for r in range(R):
        kvc = kvc + jnp.einsum('bnjcgd,cjde->bncge', sub[:, r:r + n_cmp], cw[:, r])
    k_cmp, v_cmp = kvc[:, :, 0], kvc[:, :, 1]
    c_start = jnp.arange(n_cmp) * CMP_STRIDE
    c_end = c_start + CMP_LEN - 1
    n_sel = Tp // SEL_LEN
    n_top = min(SEL_TOPN, n_sel)
    sel = kv[:, :, 2:].reshape(B, n_sel, SEL_LEN, 2, G, dh).transpose(3, 0, 4, 1, 2, 5)
    k_sel, v_sel = sel[0], sel[1]
    s_start = jnp.arange(n_sel) * SEL_LEN
    overlap = ((c_start[:, None] < s_start[None, :] + SEL_LEN)
               & (c_start[:, None] + CMP_LEN > s_start[None, :])).astype(jnp.float32)
    blk = jnp.arange(n_sel)
    bi = jnp.arange(B)[:, None, None, None]
    gi = jnp.arange(G)[None, :, None, None]
    Qb = math.gcd(T, Q_BLOCK)
    nqb = T // Qb
    W_src = win_src.shape[1] - T
    win = win_src.astype(jnp.float32)
    qh = (q.astype(jnp.float32) * dh ** -0.5).reshape(B, nqb, Qb, G, P, dh).transpose(1, 0, 2, 3, 4, 5)
    gh = jax.nn.sigmoid(g.astype(jnp.float32)).reshape(B, nqb, Qb, 3, G, P).transpose(1, 0, 2, 3, 4, 5)

    def block(args):
        qb, gb, i = args
        start = i * Qb
        tpos = pos0 + start + jnp.arange(Qb)
        dc = tpos[:, None] - c_end[None, :]
        sc = jnp.einsum('bqgpd,bngd->bgpqn', qb, k_cmp) - slope[:, :, None, None] * dc.astype(jnp.float32)
        pc = masked_softmax(sc, dc >= 0)
        o_c = jnp.einsum('bgpqn,bngd->bqgpd', pc, v_cmp)
        imp = jnp.einsum('bgpqn,nj->bgqj', pc, overlap)
        cur = tpos // SEL_LEN
        forced = (blk[None, :] == 0) | (blk[None, :] == cur[:, None]) | (blk[None, :] == cur[:, None] - 1)
        imp = jnp.where(forced, jnp.inf, jnp.where(blk[None, :] <= cur[:, None], imp, -jnp.inf))
        _, idx = lax.top_k(imp, n_top)
        ks = k_sel[bi, gi, idx]
        vs = v_sel[bi, gi, idx]
        spos = idx[..., None] * SEL_LEN + jnp.arange(SEL_LEN)
        ds = tpos[:, None, None] - spos
        ss = jnp.einsum('bqgpd,bgqksd->bgpqks', qb, ks) - slope[:, :, None, None, None] * ds[:, :, None].astype(jnp.float32)
        ps = masked_softmax(ss, (ds >= 0)[:, :, None], axis=(-2, -1))
        o_s = jnp.einsum('bgpqks,bgqksd->bqgpd', ps, vs)
        wblk = lax.dynamic_slice_in_dim(win, start, W_src + Qb, axis=1)
        wpos = pos0 - W_src + start + jnp.arange(W_src + Qb)
        dw = tpos[:, None] - wpos[None, :]
        mw = (dw >= 0) & (dw < WINDOW) & (wpos[None, :] >= 0)
        sw = jnp.einsum('bqgpd,bkgd->bgpqk', qb, wblk[:, :, 0]) - slope[:, :, None, None] * dw.astype(jnp.float32)
        pw = masked_softmax(sw, mw)
        o_w = jnp.einsum('bgpqk,bkgd->bqgpd', pw, wblk[:, :, 1])
        return gb[:, :, 0, ..., None] * o_c + gb[:, :, 1, ..., None] * o_s + gb[:, :, 2, ..., None] * o_w

    out = lax.map(block, (qh, gh, jnp.arange(nqb)))
    return out.transpose(1, 0, 2, 3, 4, 5).reshape(B, T, NSA_WIDTH)


def mem_attend(q, mem_k, mem_v):
    B, T, _ = q.shape
    qh = q.astype(jnp.float32).reshape(B, T, X_HEADS, X_DH) * X_DH ** -0.5
    p = jax.nn.softmax(jnp.einsum('bthd,bmhd->bhtm', qh, mem_k.astype(jnp.float32)), axis=-1)
    return jnp.einsum('bhtm,bmhd->bthd', p, mem_v.astype(jnp.float32)).reshape(B, T, X_WIDTH)


def token_mixer(x, w_in, b_in, norm_g, cmp_w, cmp_b, w_branch, w_out, mlstm_state,
                nsa_past, win_past, mem_k, mem_v, pos0):
    B, T, _ = x.shape
    z = x @ w_in + b_in
    mq, mk, mv, mi, mf, mo, nq, nkv, ng, xq, mg = split_cols(z, IN_SPLITS)
    h_m, mlstm_state = mlstm_branch(mq, mk, mv, mi, mf, mo, norm_g, *mlstm_state)
    kv_new = nkv.reshape(B, T, 6, NSA_GROUPS, NSA_DH)
    nsa_rows, win_rows = kv_new[:, :, :4], kv_new[:, :, 4:]
    if nsa_past is None:
        kv_all = nsa_rows
        win_src = jnp.pad(win_rows, ((0, 0), (WINDOW, 0), (0, 0), (0, 0), (0, 0)))
        w_keep = min(WINDOW, T)
    else:
        kv_all = jnp.concatenate([nsa_past.astype(x.dtype), nsa_rows], axis=1)
        win_src = jnp.concatenate([win_past.astype(x.dtype), win_rows], axis=1)
        w_keep = win_past.shape[1]
    h_n = nsa_branch(nq, ng, kv_all, win_src, cmp_w, cmp_b, pos0)
    h_x = mem_attend(xq, mem_k, mem_v)
    gates = jax.nn.sigmoid(mg.astype(jnp.float32)).reshape(B, T, N_BRANCH, D_MODEL)
    up = jnp.einsum('btcw,cwd->btcd', jnp.stack([h_m, h_n, h_x], axis=2), w_branch.astype(jnp.float32))
    merged = jnp.sum(gates * up, axis=2).astype(x.dtype)
    return merged @ w_out, mlstm_state, nsa_rows, win_src[:, win_src.shape[1] - w_keep:]


def swiglu(x, w_up, w_down):
    g, u = jnp.split(x @ w_up, 2, axis=-1)
    return (jax.nn.silu(g) * u) @ w_down


def moe_ffn(x, w_router, b_router, w_up, w_down):
    shp = x.shape
    xf = x.reshape(-1, shp[-1])
    logits = (xf @ w_router + b_router).astype(jnp.float32)
    top_v, top_i = lax.top_k(logits, TOP_K)
    probs = jax.nn.softmax(top_v, axis=-1)
    combine = jnp.einsum('nk,nke->ne', probs, jax.nn.one_hot(top_i, N_EXPERTS, dtype=jnp.float32))
    y = jnp.zeros(xf.shape, jnp.float32)
    for e in range(N_EXPERTS):
        y = y + combine[:, e:e + 1] * swiglu(xf, w_up[e], w_down[e])
    return y.reshape(shp).astype(x.dtype)


def channel_mixer(x, l, ffn_w_up, ffn_w_down, moe_w_router, moe_b_router, moe_w_up, moe_w_down):
    if l % 2 == 0:
        return swiglu(x, ffn_w_up[l // 2], ffn_w_down[l // 2])
    return moe_ffn(x, moe_w_router[l // 2], moe_b_router[l // 2], moe_w_up[l // 2], moe_w_down[l // 2])


def setup_inputs(seed: int = 0) -> dict:
    key = jax.random.key(seed)
    ks = jax.random.split(key, 32)
    f32 = jnp.float32

    def nrm(k, shp, s):
        return jax.random.normal(k, shp, f32) * s

    n_pages = PAST_LEN // PAGE_SIZE
    n_pool = (5 * DEC_BATCH * n_pages) // 4
    win_buf = min(WINDOW, PAST_LEN)
    page_table = jax.random.permutation(ks[0], n_pool)[:DEC_BATCH * n_pages].reshape(DEC_BATCH, n_pages).astype(jnp.int32)
    f_off = 3 * M_WIDTH + M_HEADS
    b_in = nrm(ks[1], (DEPTH, N_IN), 0.02).at[:, f_off:f_off + M_HEADS].add(jnp.linspace(3.0, 6.0, M_HEADS))
    return {
        'x_prompt': nrm(ks[2], (BATCH, SEQ, D_MODEL), 1.0),
        'x_sample': nrm(ks[3], (DEC_BATCH, DEC_SEQ, D_MODEL), 1.0),
        'mem_prompt': nrm(ks[4], (BATCH, MEM_LEN, D_MODEL), 1.0),
        'cache_nsa_kv': nrm(ks[5], (n_pool, PAGE_SIZE, DEPTH, 4, NSA_GROUPS, NSA_DH), 1.0),
        'cache_win_kv': nrm(ks[6], (DEC_BATCH, win_buf, DEPTH, 2, NSA_GROUPS, NSA_DH), 1.0),
        'state_mlstm_C': nrm(ks[7], (DEC_BATCH, DEPTH, M_HEADS, M_DH, M_DH), 0.5),
        'state_mlstm_n': nrm(ks[8], (DEC_BATCH, DEPTH, M_HEADS, M_DH), 0.5),
        'state_mlstm_m': nrm(ks[9], (DEC_BATCH, DEPTH, M_HEADS), 0.5) + 1.0,
        'cache_mem_kv': nrm(ks[10], (DEC_BATCH, MEM_LEN, DEPTH, 2, X_HEADS, X_DH), 1.0),
        'page_table': page_table,
        'w_in': nrm(ks[11], (DEPTH, D_MODEL, N_IN), D_MODEL ** -0.5),
        'b_in': b_in,
        'mlstm_norm_g': 1.0 + nrm(ks[12], (DEPTH, M_WIDTH), 0.02),
        'cmp_w': nrm(ks[13], (DEPTH, 2, CMP_LEN, NSA_DH, NSA_DH), (CMP_LEN * NSA_DH) ** -0.5),
        'cmp_b': nrm(ks[14], (DEPTH, 2, NSA_DH), 0.02),
        'w_mem_kv': nrm(ks[15], (DEPTH, D_MODEL, 2 * X_WIDTH), D_MODEL ** -0.5),
        'w_branch': nrm(ks[16], (DEPTH, N_BRANCH, BRANCH_WIDTH, D_MODEL), BRANCH_WIDTH ** -0.5),
        'w_out': nrm(ks[17], (DEPTH, D_MODEL, D_MODEL), BETA * D_MODEL ** -0.5),
        'ln1_g': 1.0 + nrm(ks[18], (DEPTH, D_MODEL), 0.02),
        'ln1_b': nrm(ks[19], (DEPTH, D_MODEL), 0.02),
        'ln2_g': 1.0 + nrm(ks[20], (DEPTH, D_MODEL), 0.02),
        'ln2_b': nrm(ks[21], (DEPTH, D_MODEL), 0.02),
        'ffn_w_up': nrm(ks[22], (N_DENSE, D_MODEL, 2 * D_FF), D_MODEL ** -0.5),
        'ffn_w_down': nrm(ks[23], (N_DENSE, D_FF, D_MODEL), BETA * D_FF ** -0.5),
        'moe_w_router': nrm(ks[24], (N_MOE, D_MODEL, N_EXPERTS), D_MODEL ** -0.5),
        'moe_b_router': nrm(ks[25], (N_MOE, N_EXPERTS), 0.01),
        'moe_w_up': nrm(ks[26], (N_MOE, N_EXPERTS, D_MODEL, 2 * D_FF_EXPERT), D_MODEL ** -0.5),
        'moe_w_down': nrm(ks[27], (N_MOE, N_EXPERTS, D_FF_EXPERT, D_MODEL), BETA * D_FF_EXPERT ** -0.5),
    }


def reference(x_prompt, x_sample, mem_prompt, cache_nsa_kv, cache_win_kv, state_mlstm_C, state_mlstm_n,
              state_mlstm_m, cache_mem_kv, page_table, w_in, b_in, mlstm_norm_g, cmp_w, cmp_b, w_mem_kv,
              w_branch, w_out, ln1_g, ln1_b, ln2_g, ln2_b, ffn_w_up, ffn_w_down, moe_w_router, moe_b_router,
              moe_w_up, moe_w_down):
    B, T, _ = x_prompt.shape
    DB = x_sample.shape[0]
    past_len = page_table.shape[1] * cache_nsa_kv.shape[1]
    xp, xs = x_prompt, x_sample
    nsa_p, nsa_s, win_p, win_s = [], [], [], []
    Cp, np_, mp, Cs, ns, ms, memkv_p = [], [], [], [], [], [], []
    for l in range(DEPTH):
        lw = (w_in[l], b_in[l], mlstm_norm_g[l], cmp_w[l], cmp_b[l], w_branch[l], w_out[l])
        mkv = (mem_prompt @ w_mem_kv[l]).reshape(B, MEM_LEN, 2, X_HEADS, X_DH)
        st0 = (jnp.zeros((B, M_HEADS, M_DH, M_DH), jnp.float32), jnp.zeros((B, M_HEADS, M_DH), jnp.float32),
               jnp.zeros((B, M_HEADS), jnp.float32))
        mix, st, rows, wst = token_mixer(xp, *lw, st0, None, None, mkv[:, :, 0], mkv[:, :, 1], 0)
        xp = post_norm(xp, mix, ln1_g[l], ln1_b[l])
        xp = post_norm(xp, channel_mixer(xp, l, ffn_w_up, ffn_w_down, moe_w_router, moe_b_router, moe_w_up, moe_w_down),
                       ln2_g[l], ln2_b[l])
        nsa_p.append(rows); win_p.append(wst); Cp.append(st[0]); np_.append(st[1]); mp.append(st[2]); memkv_p.append(mkv)
        past = cache_nsa_kv[page_table, :, l].reshape(DB, past_len, 4, NSA_GROUPS, NSA_DH)
        sts = (state_mlstm_C[:, l], state_mlstm_n[:, l], state_mlstm_m[:, l])
        mix, st, rows, wst = token_mixer(xs, *lw, sts, past, cache_win_kv[:, :, l],
                                         cache_mem_kv[:, :, l, 0], cache_mem_kv[:, :, l, 1], past_len)
        xs = post_norm(xs, mix, ln1_g[l], ln1_b[l])
        xs = post_norm(xs, channel_mixer(xs, l, ffn_w_up, ffn_w_down, moe_w_router, moe_b_router, moe_w_up, moe_w_down),
                       ln2_g[l], ln2_b[l])
        nsa_s.append(rows); win_s.append(wst); Cs.append(st[0]); ns.append(st[1]); ms.append(st[2])
    return (xp, xs,
            jnp.stack(nsa_p, axis=2), jnp.stack(nsa_s, axis=2),
            jnp.stack(win_p, axis=2), jnp.stack(win_s, axis=2),
            jnp.stack(Cp, axis=1), jnp.stack(np_, axis=1), jnp.stack(mp, axis=1),
            jnp.stack(Cs, axis=1), jnp.stack(ns, axis=1), jnp.stack(ms, axis=1),
            jnp.stack(memkv_p, axis=2))
```

```python
import functools
import math

import numpy as np
import jax
import jax.numpy as jnp
from jax import lax
from jax.experimental import pallas as pl
from jax.experimental.pallas import tpu as pltpu

D_MODEL = 1024
DEPTH = 2
BRANCH_WIDTH = 512
N_BRANCH = 3
M_HEADS = 4
M_DH = BRANCH_WIDTH // M_HEADS
M_WIDTH = M_HEADS * M_DH
NSA_HEADS = 8
NSA_DH = BRANCH_WIDTH // NSA_HEADS
NSA_WIDTH = NSA_HEADS * NSA_DH
NSA_GROUPS = 2
NSA_HPG = NSA_HEADS // NSA_GROUPS
NSA_KV = NSA_GROUPS * NSA_DH
CMP_LEN = 32
CMP_STRIDE = 16
SEL_LEN = 64
SEL_TOPN = 16
WINDOW = 512
MEM_LEN = 256
X_HEADS = 4
X_DH = BRANCH_WIDTH // X_HEADS
X_WIDTH = X_HEADS * X_DH
D_FF = 2816
N_EXPERTS = 8
TOP_K = 2
D_FF_EXPERT = 3584
ALPHA = (2.0 * DEPTH) ** 0.25
LN_EPS = 1e-5
IN_SPLITS = (M_WIDTH, M_WIDTH, M_WIDTH, M_HEADS, M_HEADS, M_WIDTH,
             NSA_WIDTH, 6 * NSA_KV, 3 * NSA_HEADS, X_WIDTH, N_BRANCH * D_MODEL)
NSA_SLOPES = tuple(2.0 ** (-8.0 * (h + 1) / NSA_HEADS) for h in range(NSA_HEADS))

LANES = 128
VMEM_LIMIT = 56 * 1024 * 1024
PAGES_PER_STEP = 16
GATE_COLS = LANES
NG_OFF = 2 * M_HEADS

F32 = jnp.float32
BF16 = jnp.bfloat16
NEG_INF = float("-inf")
HIGHEST = lax.Precision.HIGHEST


def _dot(a, b, precision=None):
    return jnp.dot(a, b, preferred_element_type=F32, precision=precision)


def _dot_nt(a, b):
    return lax.dot_general(a, b, (((1,), (1,)), ((), ())), preferred_element_type=F32)


def _dot_tn(a, b):
    return lax.dot_general(a, b, (((0,), (0,)), ((), ())), preferred_element_type=F32)


def _params(sem):
    return pltpu.CompilerParams(dimension_semantics=sem, vmem_limit_bytes=VMEM_LIMIT)


def _masked_softmax(s, mask):
    s = jnp.where(mask, s, NEG_INF)
    mx = jnp.max(s, axis=-1, keepdims=True)
    mx = jnp.where(mx > NEG_INF, mx, 0.0)
    p = jnp.where(mask, jnp.exp(s - mx), 0.0)
    return p / jnp.maximum(jnp.sum(p, axis=-1, keepdims=True), 1e-30)


def _layer_norm(xf, g, b):
    mu = jnp.mean(xf, axis=-1, keepdims=True)
    var = jnp.mean(jnp.square(xf - mu), axis=-1, keepdims=True)
    return (xf - mu) * lax.rsqrt(var + LN_EPS) * g + b


def _in_proj_kernel(x_ref, wa_ref, wb_ref, wc_ref, wd_ref, ba_ref, bb_ref, bc_ref, bd_ref,
                    oa_ref, ob_ref, oc_ref, ocb_ref, od_ref):
    x = x_ref[...].astype(BF16)
    oa_ref[...] = (_dot(x, wa_ref[...]) + ba_ref[...]).astype(BF16)
    ob_ref[...] = (_dot(x, wb_ref[...]) + bb_ref[...]).astype(BF16)
    c = _dot(x, wc_ref[...]) + bc_ref[...]
    oc_ref[...] = c
    ocb_ref[...] = c.astype(BF16)
    od_ref[...] = _dot(x, wd_ref[...]) + bd_ref[...]


def _split_in_proj(w_in_l, b_in_l):
    offs = np.cumsum((0,) + IN_SPLITS)
    w = [w_in_l[:, offs[i]:offs[i + 1]] for i in range(len(IN_SPLITS))]
    b = [b_in_l[offs[i]:offs[i + 1]] for i in range(len(IN_SPLITS))]
    mq, mk, mv, mi, mf, mo, nq, nkv, ng, xq, mg = range(11)
    pad = GATE_COLS - 2 * M_HEADS - 3 * NSA_HEADS

    def cat(ids, zpad=0):
        ww = jnp.concatenate([w[i] for i in ids], axis=1)
        bb = jnp.concatenate([b[i] for i in ids])
        if zpad:
            ww = jnp.pad(ww, ((0, 0), (0, zpad)))
            bb = jnp.pad(bb, (0, zpad))
        return ww.astype(BF16), bb.reshape(1, -1).astype(F32)

    return dict(a=cat([mq, mk, mv, mo]), b=cat([nq, xq]), c=cat([nkv]), d=cat([mi, mf, ng], pad), mg=cat([mg]))


def in_proj(x2d, wp, tm):
    n = x2d.shape[0]
    (wa, ba), (wb, bb), (wc, bc), (wd, bd) = wp["a"], wp["b"], wp["c"], wp["d"]
    full = lambda arr: pl.BlockSpec(arr.shape, lambda i: (0, 0))
    row = lambda w: pl.BlockSpec((tm, w), lambda i: (i, 0))
    return pl.pallas_call(
        _in_proj_kernel,
        grid=(n // tm,),
        in_specs=[row(D_MODEL), full(wa), full(wb), full(wc), full(wd), full(ba), full(bb), full(bc), full(bd)],
        out_specs=[row(wa.shape[1]), row(wb.shape[1]), row(wc.shape[1]), row(wc.shape[1]), row(wd.shape[1])],
        out_shape=[jax.ShapeDtypeStruct((n, wa.shape[1]), BF16), jax.ShapeDtypeStruct((n, wb.shape[1]), BF16),
                   jax.ShapeDtypeStruct((n, wc.shape[1]), F32), jax.ShapeDtypeStruct((n, wc.shape[1]), BF16),
                   jax.ShapeDtypeStruct((n, wd.shape[1]), F32)],
        compiler_params=_params(("parallel",)),
        name="in_proj",
    )(x2d, wa, wb, wc, wd, ba, bb, bc, bd)


def _proj_kernel(x_ref, w_ref, o_ref):
    o_ref[...] = _dot(x_ref[...].astype(BF16), w_ref[...])


def proj(x2d, w_bf16, tm):
    n, k = x2d.shape
    m = w_bf16.shape[1]
    return pl.pallas_call(
        _proj_kernel,
        grid=(n // tm,),
        in_specs=[pl.BlockSpec((tm, k), lambda i: (i, 0)), pl.BlockSpec((k, m), lambda i: (0, 0))],
        out_specs=pl.BlockSpec((tm, m), lambda i: (i, 0)),
        out_shape=jax.ShapeDtypeStruct((n, m), F32),
        compiler_params=_params(("parallel",)),
        name="mem_proj",
    )(x2d, w_bf16)


def _log_sigmoid(x):
    return jnp.minimum(x, 0.0) - jnp.log1p(jnp.exp(-jnp.abs(x)))


def _mlstm_kernel(q_ref, k_ref, v_ref, og_ref, gc_ref, gr_ref, c0_ref, n0_ref, m0_ref, ng_ref,
                  h_ref, ct_ref, nt_ref, mt_ref, c_s, n_s, m_s, *, L, t_valid):
    ci = pl.program_id(1)

    @pl.when(ci == 0)
    def _():
        c_s[...] = c0_ref[0]
        n_s[...] = n0_ref[0]
        m_s[...] = m0_ref[0]

    row = lax.broadcasted_iota(jnp.int32, (L, L), 0)
    col = lax.broadcasted_iota(jnp.int32, (L, L), 1)
    causal = row >= col
    tri = causal.astype(F32)
    tri_t = (row <= col).astype(F32)
    gc = gc_ref[0]
    gr = gr_ref[0]
    rvalid = lax.broadcasted_iota(jnp.int32, (L, GATE_COLS), 0) < t_valid
    cvalid = lax.broadcasted_iota(jnp.int32, (2 * M_HEADS, L), 1) < t_valid
    lf_c = jnp.where(rvalid, _log_sigmoid(gc), 0.0)
    lf_r = jnp.where(cvalid, _log_sigmoid(gr), 0.0)
    ig_c = jnp.where(rvalid, gc, NEG_INF)
    ig_r = jnp.where(cvalid, gr, NEG_INF)
    b_c = _dot(tri, lf_c, HIGHEST)
    b_r = _dot(lf_r, tri_t, HIGHEST)
    scale = M_DH ** -0.5

    for h in range(M_HEADS):
        sl = slice(h * M_DH, (h + 1) * M_DH)
        q = q_ref[0, :, sl]
        k = k_ref[0, :, sl]
        v = v_ref[0, :, sl]
        bc = b_c[:, M_HEADS + h:M_HEADS + h + 1]
        br = b_r[M_HEADS + h:M_HEADS + h + 1, :]
        igc = ig_c[:, h:h + 1]
        igr = ig_r[h:h + 1, :]
        m_prev = m_s[h, 0:1, 0:1]
        c_prev = c_s[h]
        n_prev = n_s[h:h + 1, :]

        dmat = jnp.where(causal, bc - br + igr, NEG_INF)
        inter = bc + m_prev
        m_t = jnp.maximum(inter, jnp.max(dmat, axis=-1, keepdims=True))
        s = _dot_nt(q, k) * scale * jnp.exp(dmat - m_t)
        sc_in = jnp.exp(inter - m_t)
        qf = q.astype(F32)
        num = sc_in * _dot_nt(q, c_prev.astype(BF16)) + _dot(s.astype(BF16), v)
        den = sc_in * jnp.sum(qf * n_prev, axis=-1, keepdims=True) + jnp.sum(s, axis=-1, keepdims=True)
        hh = num / jnp.maximum(jnp.abs(den), jnp.exp(-m_t))

        b_last = bc[L - 1:L, :]
        dec_c = b_last - bc + igc
        dec_r = b_last - br + igr
        m_new = jnp.maximum(b_last + m_prev, jnp.max(dec_r, axis=-1, keepdims=True))
        ws_c = jnp.exp(dec_c - m_new) * scale
        sc = jnp.exp(b_last + m_prev - m_new)
        vf = v.astype(F32)
        kf = k.astype(F32)
        c_new = sc * c_prev + _dot_tn((vf * ws_c).astype(BF16), k)
        n_new = sc * n_prev + jnp.sum(kf * ws_c, axis=0, keepdims=True)
        c_s[h] = c_new
        n_s[h:h + 1, :] = n_new
        m_s[h] = jnp.broadcast_to(m_new, m_s.shape[1:])

        og = og_ref[0, :, sl].astype(F32)
        hh = hh * jax.nn.sigmoid(og)
        mu = jnp.mean(hh, axis=-1, keepdims=True)
        var = jnp.mean(jnp.square(hh - mu), axis=-1, keepdims=True)
        hn = (hh - mu) * lax.rsqrt(var + LN_EPS) * ng_ref[:, sl]
        h_ref[0, :, sl] = hn.astype(h_ref.dtype)

    ct_ref[0] = c_s[...]
    nt_ref[0] = n_s[...]
    mt_ref[0] = m_s[...]


def mlstm(za, zd, c0, n0, m0, norm_g, *, L, t_valid):
    B, T, _ = za.shape
    nc = T // L
    gr = jnp.swapaxes(zd[:, :, :2 * M_HEADS], 1, 2)
    m0b = jnp.broadcast_to(m0.astype(F32)[:, :, None, None], (B, M_HEADS, 8, LANES))
    colspec = lambda j: pl.BlockSpec((1, L, M_WIDTH), lambda b, c, j=j: (b, c, j))
    st = lambda shp: pl.BlockSpec((1,) + shp, lambda b, c: (b,) + (0,) * len(shp))
    kern = functools.partial(_mlstm_kernel, L=L, t_valid=t_valid)
    h, ct, nt, mt = pl.pallas_call(
        kern,
        grid=(B, nc),
        in_specs=[colspec(0), colspec(1), colspec(2), colspec(3),
                  pl.BlockSpec((1, L, GATE_COLS), lambda b, c: (b, c, 0)),
                  pl.BlockSpec((1, 2 * M_HEADS, L), lambda b, c: (b, 0, c)),
                  st((M_HEADS, M_DH, M_DH)), st((M_HEADS, M_DH)), st((M_HEADS, 8, LANES)),
                  pl.BlockSpec((1, M_WIDTH), lambda b, c: (0, 0))],
        out_specs=[pl.BlockSpec((1, L, M_WIDTH), lambda b, c: (b, c, 0)),
                   st((M_HEADS, M_DH, M_DH)), st((M_HEADS, M_DH)), st((M_HEADS, 8, LANES))],
        out_shape=[jax.ShapeDtypeStruct((B, T, M_WIDTH), BF16),
                   jax.ShapeDtypeStruct((B, M_HEADS, M_DH, M_DH), F32),
                   jax.ShapeDtypeStruct((B, M_HEADS, M_DH), F32),
                   jax.ShapeDtypeStruct((B, M_HEADS, 8, LANES), F32)],
        scratch_shapes=[pltpu.VMEM((M_HEADS, M_DH, M_DH), F32), pltpu.VMEM((M_HEADS, M_DH), F32),
                        pltpu.VMEM((M_HEADS, 8, LANES), F32)],
        compiler_params=_params(("parallel", "arbitrary")),
        name="mlstm",
    )(za, za, za, za, zd, gr, c0.astype(F32), n0.astype(F32), m0b, norm_g.reshape(1, M_WIDTH).astype(F32))
    return h, ct, nt, mt[:, :, 0, 0]


def _mem_attn_kernel(q_ref, k_ref, v_ref, o_ref):
    scale = X_DH ** -0.5
    for h in range(X_HEADS):
        sl = slice(h * X_DH, (h + 1) * X_DH)
        s = _dot_nt(q_ref[0, :, sl], k_ref[0, :, sl].astype(BF16)) * scale
        mx = jnp.max(s, axis=-1, keepdims=True)
        p = jnp.exp(s - mx)
        p = p / jnp.sum(p, axis=-1, keepdims=True)
        o_ref[0, :, sl] = _dot(p.astype(BF16), v_ref[0, :, sl].astype(BF16)).astype(o_ref.dtype)


def mem_attn(zb, kv, k_blk, v_blk, tq):
    B, T, _ = zb.shape
    return pl.pallas_call(
        _mem_attn_kernel,
        grid=(B, T // tq),
        in_specs=[pl.BlockSpec((1, tq, X_WIDTH), lambda b, i: (b, i, 1)),
                  pl.BlockSpec((1, MEM_LEN, X_WIDTH), lambda b, i: (b, 0, k_blk)),
                  pl.BlockSpec((1, MEM_LEN, X_WIDTH), lambda b, i: (b, 0, v_blk))],
        out_specs=pl.BlockSpec((1, tq, X_WIDTH), lambda b, i: (b, i, 0)),
        out_shape=jax.ShapeDtypeStruct((B, T, X_WIDTH), BF16),
        compiler_params=_params(("parallel", "parallel")),
        name="mem_attn",
    )(zb, kv, kv)


def _group_padded_queries(q, nq):
    lane = lax.broadcasted_iota(jnp.int32, (nq, LANES), 1)
    parts = []
    for hd in range(NSA_HEADS):
        j, hi, g = hd // 2, hd % 2, hd // NSA_HPG
        x = q[:, j * LANES:(j + 1) * LANES].astype(F32)
        if hi != g:
            x = pltpu.roll(x, NSA_DH, axis=1)
        keep = (lane >= NSA_DH) if g == 1 else (lane < NSA_DH)
        parts.append(jnp.where(keep, x * (NSA_DH ** -0.5), 0.0))
    return jnp.concatenate(parts, axis=0).astype(BF16)


def _assemble_heads(outs, nq):
    lane = lax.broadcasted_iota(jnp.int32, (nq, LANES), 1)
    cols = []
    for j in range(NSA_HEADS // 2):
        g = (2 * j) // NSA_HPG
        a, b = outs[2 * j], outs[2 * j + 1]
        if g == 0:
            b = pltpu.roll(b, NSA_DH, axis=1)
        else:
            a = pltpu.roll(a, NSA_DH, axis=1)
        cols.append(jnp.where(lane < NSA_DH, a, b))
    return cols


def _overlap_matrix(n_rows, n_cols, row_off):
    r = lax.broadcasted_iota(jnp.int32, (n_rows, n_cols), 0) - row_off
    j = lax.broadcasted_iota(jnp.int32, (n_rows, n_cols), 1)
    c_start = r * CMP_STRIDE
    s_start = j * SEL_LEN
    return ((c_start < s_start + SEL_LEN) & (c_start + CMP_LEN > s_start)).astype(F32)


def _select_blocks(imp, tpos, n_sel, n_top):
    nq, W = imp.shape
    blk = lax.broadcasted_iota(jnp.int32, (nq, W), 1)
    cur = tpos // SEL_LEN
    forced = (blk == 0) | (blk == cur) | (blk == cur - 1)
    v = jnp.where(forced, jnp.inf, jnp.where(blk <= cur, imp, NEG_INF))
    v = jnp.where(blk < n_sel, v, NEG_INF)
    sel = jnp.zeros((nq, W), F32)
    for _ in range(n_top):
        mx = jnp.max(v, axis=-1, keepdims=True)
        first = jnp.min(jnp.where(v == mx, blk, W), axis=-1, keepdims=True)
        pick = blk == first
        sel = jnp.where(pick, 1.0, sel)
        v = jnp.where(pick, NEG_INF, v)
    return sel


def _online_update(s, m_old, l_old):
    m_new = jnp.maximum(m_old, jnp.max(s, axis=-1, keepdims=True))
    alpha = jnp.exp(m_old - m_new)
    p = jnp.exp(s - m_new)
    return p, m_new, alpha, alpha * l_old + jnp.sum(p, axis=-1, keepdims=True)


def _cmp_tokens_kernel(xk_ref, xv_ref, bd_ref, b_ref, kc_ref, vc_ref, *, nsub):
    for c, (x_ref, o_ref) in enumerate(((xk_ref, kc_ref), (xv_ref, vc_ref))):
        a0 = jnp.zeros((nsub, LANES), F32)
        a1 = jnp.zeros((nsub, LANES), F32)
        for j in range(CMP_STRIDE):
            xj = x_ref[0, :, j, :].astype(BF16)
            a0 = a0 + _dot(xj, bd_ref[c, 0, j])
            a1 = a1 + _dot(xj, bd_ref[c, 1, j])
        tok = a0 + pltpu.roll(a1, nsub - 1, axis=0) + b_ref[c]
        o_ref[0] = tok.astype(o_ref.dtype)


def _cmp_blockdiag(cmp_w_l, copies):
    R = CMP_LEN // CMP_STRIDE
    w = cmp_w_l.astype(F32).reshape(2, R, CMP_STRIDE, NSA_DH, NSA_DH)
    eye = jnp.eye(copies, dtype=F32)
    bd = jnp.einsum("ab,crjde->crjadbe", eye, w).reshape(2, R, CMP_STRIDE, copies * NSA_DH, copies * NSA_DH)
    return bd.astype(BF16)


def cmp_tokens(zc, cmp_w_l, cmp_b_l):
    B, T, W = zc.shape
    nsub = T // CMP_STRIDE
    x4 = zc.reshape(B, nsub, CMP_STRIDE, W)
    bd = _cmp_blockdiag(cmp_w_l, NSA_GROUPS)
    bias = jnp.tile(cmp_b_l.astype(F32), (1, NSA_GROUPS)).reshape(2, 1, LANES)
    spec = lambda j: pl.BlockSpec((1, nsub, CMP_STRIDE, LANES), lambda b, j=j: (b, 0, 0, j))
    return pl.pallas_call(
        functools.partial(_cmp_tokens_kernel, nsub=nsub),
        grid=(B,),
        in_specs=[spec(0), spec(1), pl.BlockSpec(bd.shape, lambda b: (0,) * 5), pl.BlockSpec(bias.shape, lambda b: (0, 0, 0))],
        out_specs=[pl.BlockSpec((1, nsub, LANES), lambda b: (b, 0, 0))] * 2,
        out_shape=[jax.ShapeDtypeStruct((B, nsub, LANES), BF16)] * 2,
        compiler_params=_params(("parallel",)),
        name="nsa_cmp_tokens",
    )(x4, x4, bd, bias)


def _nsa_prompt_kernel(q_ref, g_ref, kc_ref, vc_ref, ks_ref, vs_ref, kw_ref, vw_ref, o_ref, m_s, l_s, acc_s,
                       *, tq, T, CH):
    qi = pl.program_id(1)
    start = qi * tq
    n_cmp_rows = kc_ref.shape[1]
    n_sel = T // SEL_LEN
    n_top = min(SEL_TOPN, n_sel)
    qp = _group_padded_queries(q_ref[0], tq)
    tpos = start + lax.broadcasted_iota(jnp.int32, (tq, 1), 0)
    rows = lambda a, hd: a[hd * tq:(hd + 1) * tq]

    sc_all = _dot_nt(qp, kc_ref[0])
    n_idx = lax.broadcasted_iota(jnp.int32, (tq, n_cmp_rows), 1)
    dc = tpos - (n_idx * CMP_STRIDE + CMP_LEN - 1)
    mask_c = dc >= 0
    dcf = dc.astype(F32)
    pcs = [_masked_softmax(rows(sc_all, hd) - NSA_SLOPES[hd] * dcf, mask_c) for hd in range(NSA_HEADS)]
    o_c = _dot(jnp.concatenate(pcs, axis=0).astype(BF16), vc_ref[0])
    ov = _overlap_matrix(n_cmp_rows, LANES, 0)
    sels = []
    for g in range(NSA_GROUPS):
        pg = pcs[g * NSA_HPG]
        for p in range(1, NSA_HPG):
            pg = pg + pcs[g * NSA_HPG + p]
        imp = _dot(pg, ov, HIGHEST)
        sels.append(_select_blocks(imp, tpos, n_sel, n_top).astype(BF16))

    m_s[...] = jnp.full(m_s.shape, NEG_INF, F32)
    l_s[...] = jnp.zeros(l_s.shape, F32)
    acc_s[...] = jnp.zeros(acc_s.shape, F32)
    n_chunks = (start + tq + CH - 1) // CH

    def chunk(c, carry):
        base = pl.multiple_of(c * CH, CH)
        kch = ks_ref[0, pl.ds(base, CH), :]
        vch = vs_ref[0, pl.ds(base, CH), :]
        s_all = _dot_nt(qp, kch)
        kpos = base + lax.broadcasted_iota(jnp.int32, (tq, CH), 1)
        ds = tpos - kpos
        causal = ds >= 0
        dsf = ds.astype(F32)
        jj = lax.broadcasted_iota(jnp.int32, (LANES, CH), 0)
        kk = base + lax.broadcasted_iota(jnp.int32, (LANES, CH), 1)
        expand = jnp.where(kk // SEL_LEN == jj, 1.0, 0.0).astype(BF16)
        ps = []
        for g in range(NSA_GROUPS):
            mk = (_dot(sels[g], expand) > 0.5) & causal
            for p in range(NSA_HPG):
                hd = g * NSA_HPG + p
                r = slice(hd * tq, (hd + 1) * tq)
                s = jnp.where(mk, rows(s_all, hd) - NSA_SLOPES[hd] * dsf, NEG_INF)
                pr, m_new, alpha, l_new = _online_update(s, m_s[r], l_s[r])
                m_s[r] = m_new
                l_s[r] = l_new
                acc_s[r] = alpha * acc_s[r]
                ps.append(pr)
        acc_s[...] = acc_s[...] + _dot(jnp.concatenate(ps, axis=0).astype(BF16), vch)
        return carry

    lax.fori_loop(0, n_chunks, chunk, 0)
    o_s = acc_s[...] / jnp.maximum(l_s[...], 1e-30)

    wk = WINDOW + tq
    base_w = pl.multiple_of(start, tq)
    sw_all = _dot_nt(qp, kw_ref[0, pl.ds(base_w, wk), :])
    jw = lax.broadcasted_iota(jnp.int32, (tq, wk), 1)
    iw = lax.broadcasted_iota(jnp.int32, (tq, wk), 0)
    dw = iw - jw + WINDOW
    mask_w = (dw >= 0) & (dw < WINDOW) & (start - WINDOW + jw >= 0)
    dwf = dw.astype(F32)
    pws = [_masked_softmax(rows(sw_all, hd) - NSA_SLOPES[hd] * dwf, mask_w) for hd in range(NSA_HEADS)]
    o_w = _dot(jnp.concatenate(pws, axis=0).astype(BF16), vw_ref[0, pl.ds(base_w, wk), :])

    gate = jax.nn.sigmoid(g_ref[0])
    outs = []
    for hd in range(NSA_HEADS):
        gcol = lambda br: gate[:, NG_OFF + br * NSA_HEADS + hd:NG_OFF + br * NSA_HEADS + hd + 1]
        outs.append(gcol(0) * rows(o_c, hd) + gcol(1) * rows(o_s, hd) + gcol(2) * rows(o_w, hd))
    for j, colv in enumerate(_assemble_heads(outs, tq)):
        o_ref[0, :, j * LANES:(j + 1) * LANES] = colv.astype(o_ref.dtype)


def nsa_prompt(zb, zd, zcb, kc, vc, tq, CH):
    B, T, _ = zb.shape
    kw = jnp.pad(zcb[:, :, 4 * NSA_KV:5 * NSA_KV], ((0, 0), (WINDOW, 0), (0, 0)))
    vw = jnp.pad(zcb[:, :, 5 * NSA_KV:6 * NSA_KV], ((0, 0), (WINDOW, 0), (0, 0)))
    nsub = kc.shape[1]
    per_b = lambda rows_, j: pl.BlockSpec((1, rows_, LANES), lambda b, i, j=j: (b, 0, j))
    kern = functools.partial(_nsa_prompt_kernel, tq=tq, T=T, CH=CH)
    return pl.pallas_call(
        kern,
        grid=(B, T // tq),
        in_specs=[pl.BlockSpec((1, tq, NSA_WIDTH), lambda b, i: (b, i, 0)),
                  pl.BlockSpec((1, tq, GATE_COLS), lambda b, i: (b, i, 0)),
                  per_b(nsub, 0), per_b(nsub, 0), per_b(T, 2), per_b(T, 3),
                  per_b(T + WINDOW, 0), per_b(T + WINDOW, 0)],
        out_specs=pl.BlockSpec((1, tq, NSA_WIDTH), lambda b, i: (b, i, 0)),
        out_shape=jax.ShapeDtypeStruct((B, T, NSA_WIDTH), BF16),
        scratch_shapes=[pltpu.VMEM((NSA_HEADS * tq, 1), F32), pltpu.VMEM((NSA_HEADS * tq, 1), F32),
                        pltpu.VMEM((NSA_HEADS * tq, LANES), F32)],
        compiler_params=_params(("parallel", "arbitrary")),
        name="nsa_prompt",
    )(zb, zd, kc, vc, zcb, zcb, kw, vw)


def _page_specs(shape, col_blk, n_pages):
    nd = len(shape)

    def mk(i):
        def imap(b, s, pt):
            return (pt[b * n_pages + s * PAGES_PER_STEP + i],) + (0,) * (nd - 2) + (col_blk,)
        return pl.BlockSpec(shape, imap)
    return [mk(i) for i in range(PAGES_PER_STEP)]


def _nsa_dec_cmp_kernel(pt_ref, *refs, T, past_len, n_rows):
    pages = refs[:PAGES_PER_STEP]
    bd_ref, b_ref, q_ref, oc_ref, sel_ref, tok_s, pend_s = refs[PAGES_PER_STEP:]
    s = pl.program_id(1)
    n_steps = pl.num_programs(1)
    R = PAGES_PER_STEP * (refs[0].shape[1])

    @pl.when(s == 0)
    def _():
        pend_s[...] = jnp.zeros(pend_s.shape, F32)

    a0 = jnp.zeros((R, 2 * LANES), F32)
    a1 = jnp.zeros((R, 2 * LANES), F32)
    for j in range(CMP_STRIDE):
        xj = jnp.concatenate([pg[0, :, j, :] for pg in pages], axis=0).astype(BF16)
        a0 = a0 + _dot(xj, bd_ref[0, j])
        a1 = a1 + _dot(xj, bd_ref[1, j])
    a0 = a0 + b_ref[...]
    rid = lax.broadcasted_iota(jnp.int32, (R, 2 * LANES), 0)
    tok = jnp.where(rid == 0, pend_s[...], pltpu.roll(a0, 1, axis=0)) + a1
    tok_s[pl.ds(pl.multiple_of(s * R, R), R), :] = tok
    pend_s[...] = a0[R - 1:R, :]

    @pl.when(s == n_steps - 1)
    def _():
        qp = _group_padded_queries(q_ref[0], T)
        kc = tok_s[:, 0:LANES].astype(BF16)
        vc = tok_s[:, LANES:2 * LANES].astype(BF16)
        sc_all = _dot_nt(qp, kc)
        tpos = past_len + lax.broadcasted_iota(jnp.int32, (T, 1), 0)
        r = lax.broadcasted_iota(jnp.int32, (T, n_rows), 1)
        dc = tpos - ((r - 1) * CMP_STRIDE + CMP_LEN - 1)
        mask_c = (dc >= 0) & (r >= 1)
        dcf = dc.astype(F32)
        pcs = [_masked_softmax(sc_all[hd * T:(hd + 1) * T] - NSA_SLOPES[hd] * dcf, mask_c) for hd in range(NSA_HEADS)]
        oc_ref[0] = _dot(jnp.concatenate(pcs, axis=0).astype(BF16), vc)
        W = sel_ref.shape[2]
        n_sel = -(-(past_len + T) // SEL_LEN)
        ov = _overlap_matrix(n_rows, W, 1)
        for g in range(NSA_GROUPS):
            pg = pcs[g * NSA_HPG]
            for p in range(1, NSA_HPG):
                pg = pg + pcs[g * NSA_HPG + p]
            imp = _dot(pg, ov, HIGHEST)
            sel_ref[0, g * T:(g + 1) * T, :] = _select_blocks(imp, tpos, n_sel, min(SEL_TOPN, n_sel))


def _nsa_dec_sel_kernel(pt_ref, *refs, T, past_len, page_rows):
    pages = refs[:PAGES_PER_STEP]
    q_ref, sel_ref, sel_last_ref, ex_ref, kn_ref, vn_ref, os_ref, m_s, l_s, acc_s = refs[PAGES_PER_STEP:]
    s = pl.program_id(1)
    n_steps = pl.num_programs(1)
    CH = PAGES_PER_STEP * page_rows

    @pl.when(s == 0)
    def _():
        m_s[...] = jnp.full(m_s.shape, NEG_INF, F32)
        l_s[...] = jnp.zeros(l_s.shape, F32)
        acc_s[...] = jnp.zeros(acc_s.shape, F32)

    qp = _group_padded_queries(q_ref[0], T)
    tpos = past_len + lax.broadcasted_iota(jnp.int32, (T, 1), 0)

    def update(s_all, mask_of_group, kpos, pv):
        ds = tpos - kpos
        dsf = ds.astype(F32)
        ps = []
        for g in range(NSA_GROUPS):
            mk = mask_of_group(g) & (ds >= 0)
            for p in range(NSA_HPG):
                hd = g * NSA_HPG + p
                r = slice(hd * T, (hd + 1) * T)
                sc = jnp.where(mk, s_all[r] - NSA_SLOPES[hd] * dsf, NEG_INF)
                pr, m_new, alpha, l_new = _online_update(sc, m_s[r], l_s[r])
                m_s[r] = m_new
                l_s[r] = l_new
                acc_s[r] = alpha * acc_s[r]
                ps.append(pr)
        acc_s[...] = acc_s[...] + pv(jnp.concatenate(ps, axis=0).astype(BF16))

    selk = _dot(sel_ref[0, 0].astype(BF16), ex_ref[...])
    s_all = jnp.concatenate([_dot_nt(qp, pg[0, :, 0:LANES].astype(BF16)) for pg in pages], axis=1)
    kpos = s * CH + lax.broadcasted_iota(jnp.int32, (T, CH), 1)

    def pv_pages(pmat):
        out = jnp.zeros((NSA_HEADS * T, LANES), F32)
        for i, pg in enumerate(pages):
            out = out + _dot(pmat[:, i * page_rows:(i + 1) * page_rows], pg[0, :, LANES:2 * LANES].astype(BF16))
        return out

    update(s_all, lambda g: selk[g * T:(g + 1) * T] > 0.5, kpos, pv_pages)

    @pl.when(s == n_steps - 1)
    def _():
        nk = kn_ref.shape[1]
        sn = _dot_nt(qp, kn_ref[0])
        lane = lax.broadcasted_iota(jnp.int32, (T, nk), 1)
        sl = sel_last_ref[0, 0]
        update(sn, lambda g: (sl[g * T:(g + 1) * T, 0:1] > 0.5) & (lane < T), past_len + lane,
               lambda pmat: _dot(pmat, vn_ref[0]))
        os_ref[0] = acc_s[...] / jnp.maximum(l_s[...], 1e-30)


def _nsa_dec_win_kernel(q_ref, g_ref, oc_ref, os_ref, kwp_ref, vwp_ref, kwn_ref, vwn_ref, o_ref, *, T, past_len):
    qp = _group_padded_queries(q_ref[0], T)
    w_src = kwp_ref.shape[1]
    nk = kwn_ref.shape[1]
    sw_all = jnp.concatenate([_dot_nt(qp, kwp_ref[0].astype(BF16)), _dot_nt(qp, kwn_ref[0])], axis=1)
    j = lax.broadcasted_iota(jnp.int32, (T, w_src + nk), 1)
    tpos = past_len + lax.broadcasted_iota(jnp.int32, (T, 1), 0)
    wpos = past_len - w_src + j
    dw = tpos - wpos
    mask_w = (dw >= 0) & (dw < WINDOW) & (wpos >= 0) & (j < w_src + T)
    dwf = dw.astype(F32)
    pws = [_masked_softmax(sw_all[hd * T:(hd + 1) * T] - NSA_SLOPES[hd] * dwf, mask_w) for hd in range(NSA_HEADS)]
    pw = jnp.concatenate(pws, axis=0).astype(BF16)
    o_w = _dot(pw[:, :w_src], vwp_ref[0].astype(BF16)) + _dot(pw[:, w_src:], vwn_ref[0])
    gate = jax.nn.sigmoid(g_ref[0])
    o_c = oc_ref[0]
    o_s = os_ref[0]
    outs = []
    for hd in range(NSA_HEADS):
        r = slice(hd * T, (hd + 1) * T)
        gcol = lambda br: gate[:, NG_OFF + br * NSA_HEADS + hd:NG_OFF + br * NSA_HEADS + hd + 1]
        outs.append(gcol(0) * o_c[r] + gcol(1) * o_s[r] + gcol(2) * o_w[r])
    for jj, colv in enumerate(_assemble_heads(outs, T)):
        o_ref[0, :, jj * LANES:(jj + 1) * LANES] = colv.astype(o_ref.dtype)


def nsa_decode(zb, zd, zcb, cache_nsa_kv, cache_win_kv, page_table, cmp_w_l, cmp_b_l, layer):
    DB, T, _ = zb.shape
    n_pool, page_rows = cache_nsa_kv.shape[:2]
    n_pages = page_table.shape[1]
    past_len = n_pages * page_rows
    n_steps = n_pages // PAGES_PER_STEP
    sub_per_page = page_rows // CMP_STRIDE
    n_rows = n_pages * sub_per_page
    assert (n_rows - 1) * CMP_STRIDE + CMP_LEN - 1 > past_len + T - 1
    assert T <= CMP_STRIDE and n_pages % PAGES_PER_STEP == 0
    row_w = DEPTH * 4 * NSA_KV
    pt = page_table.reshape(-1).astype(jnp.int32)
    half = 2 * NSA_KV
    cache4 = cache_nsa_kv.reshape(n_pool, sub_per_page, CMP_STRIDE, row_w)
    cache3 = cache_nsa_kv.reshape(n_pool, page_rows, row_w)

    bd = _cmp_blockdiag(cmp_w_l, NSA_GROUPS)
    z = jnp.zeros_like(bd[0])
    bd2 = jnp.concatenate([jnp.concatenate([bd[0], z], axis=-1), jnp.concatenate([z, bd[1]], axis=-1)], axis=-2)
    bias = jnp.concatenate([jnp.tile(cmp_b_l[0].astype(F32), NSA_GROUPS), jnp.tile(cmp_b_l[1].astype(F32), NSA_GROUPS)]).reshape(1, half)

    n_sel = -(-(past_len + T) // SEL_LEN)
    blocks_per_step = PAGES_PER_STEP * page_rows // SEL_LEN
    sel_used = (n_steps + 1) * blocks_per_step
    sel_w = -(-sel_used // LANES) * LANES
    assert sel_used >= n_sel and blocks_per_step <= LANES
    q_spec = pl.BlockSpec((1, T, NSA_WIDTH), lambda b, s, pt: (b, 0, 0))

    o_c, sel = pl.pallas_call(
        functools.partial(_nsa_dec_cmp_kernel, T=T, past_len=past_len, n_rows=n_rows),
        grid_spec=pltpu.PrefetchScalarGridSpec(
            num_scalar_prefetch=1, grid=(DB, n_steps),
            in_specs=_page_specs((1, sub_per_page, CMP_STRIDE, half), 2 * layer, n_pages)
            + [pl.BlockSpec(bd2.shape, lambda b, s, pt: (0, 0, 0, 0)), pl.BlockSpec(bias.shape, lambda b, s, pt: (0, 0)), q_spec],
            out_specs=[pl.BlockSpec((1, NSA_HEADS * T, LANES), lambda b, s, pt: (b, 0, 0)),
                       pl.BlockSpec((1, NSA_GROUPS * T, sel_w), lambda b, s, pt: (b, 0, 0))],
            scratch_shapes=[pltpu.VMEM((n_rows, half), F32), pltpu.VMEM((1, half), F32)]),
        out_shape=[jax.ShapeDtypeStruct((DB, NSA_HEADS * T, LANES), F32),
                   jax.ShapeDtypeStruct((DB, NSA_GROUPS * T, sel_w), F32)],
        compiler_params=_params(("parallel", "arbitrary")),
        name="nsa_dec_cmp",
    )(pt, *([cache4] * PAGES_PER_STEP), bd2, bias, zb)

    sel_steps = sel[:, :, :sel_used].reshape(DB, NSA_GROUPS * T, n_steps + 1, blocks_per_step).transpose(0, 2, 1, 3)
    sel_steps = jnp.pad(sel_steps, ((0, 0), (0, 0), (0, 0), (0, LANES - blocks_per_step)))
    kk = np.arange(PAGES_PER_STEP * page_rows) // SEL_LEN
    expand = jnp.asarray(kk[None, :] == np.arange(LANES)[:, None], BF16)
    pad_rows = LANES - T
    k_new = jnp.pad(zcb[:, :, 2 * NSA_KV:3 * NSA_KV], ((0, 0), (0, pad_rows), (0, 0)))
    v_new = jnp.pad(zcb[:, :, 3 * NSA_KV:4 * NSA_KV], ((0, 0), (0, pad_rows), (0, 0)))
    new_spec = pl.BlockSpec((1, LANES, LANES), lambda b, s, pt: (b, 0, 0))
    sel_blk = (1, 1, NSA_GROUPS * T, LANES)

    o_s = pl.pallas_call(
        functools.partial(_nsa_dec_sel_kernel, T=T, past_len=past_len, page_rows=page_rows),
        grid_spec=pltpu.PrefetchScalarGridSpec(
            num_scalar_prefetch=1, grid=(DB, n_steps),
            in_specs=_page_specs((1, page_rows, half), 2 * layer + 1, n_pages)
            + [q_spec, pl.BlockSpec(sel_blk, lambda b, s, pt: (b, s, 0, 0)),
               pl.BlockSpec(sel_blk, lambda b, s, pt: (b, n_steps, 0, 0)),
               pl.BlockSpec(expand.shape, lambda b, s, pt: (0, 0)), new_spec, new_spec],
            out_specs=pl.BlockSpec((1, NSA_HEADS * T, LANES), lambda b, s, pt: (b, 0, 0)),
            scratch_shapes=[pltpu.VMEM((NSA_HEADS * T, 1), F32), pltpu.VMEM((NSA_HEADS * T, 1), F32),
                            pltpu.VMEM((NSA_HEADS * T, LANES), F32)]),
        out_shape=jax.ShapeDtypeStruct((DB, NSA_HEADS * T, LANES), F32),
        compiler_params=_params(("parallel", "arbitrary")),
        name="nsa_dec_sel",
    )(pt, *([cache3] * PAGES_PER_STEP), zb, sel_steps, sel_steps, expand, k_new, v_new)

    w_src = cache_win_kv.shape[1]
    win3 = cache_win_kv.reshape(DB, w_src, DEPTH * 2 * NSA_KV)
    kw_new = jnp.pad(zcb[:, :, 4 * NSA_KV:5 * NSA_KV], ((0, 0), (0, pad_rows), (0, 0)))
    vw_new = jnp.pad(zcb[:, :, 5 * NSA_KV:6 * NSA_KV], ((0, 0), (0, pad_rows), (0, 0)))
    b3 = lambda shp, j=0: pl.BlockSpec(shp, lambda b, j=j: (b, 0, j))
    return pl.pallas_call(
        functools.partial(_nsa_dec_win_kernel, T=T, past_len=past_len),
        grid=(DB,),
        in_specs=[b3((1, T, NSA_WIDTH)), b3((1, T, GATE_COLS)), b3((1, NSA_HEADS * T, LANES)), b3((1, NSA_HEADS * T, LANES)),
                  b3((1, w_src, LANES), 2 * layer), b3((1, w_src, LANES), 2 * layer + 1),
                  b3((1, LANES, LANES)), b3((1, LANES, LANES))],
        out_specs=b3((1, T, NSA_WIDTH)),
        out_shape=jax.ShapeDtypeStruct((DB, T, NSA_WIDTH), BF16),
        compiler_params=_params(("parallel",)),
        name="nsa_dec_win",
    )(zb, zd, o_c, o_s, win3, win3, kw_new, vw_new)


def _merge_kernel(x_ref, hm_ref, hn_ref, hx_ref, wg_ref, bg_ref, wbr_ref, wo_ref, g_ref, b_ref, o_ref):
    x = x_ref[...]
    xb = x.astype(BF16)
    merged = None
    for c, h_ref in enumerate((hm_ref, hn_ref, hx_ref)):
        sl = slice(c * D_MODEL, (c + 1) * D_MODEL)
        gate = jax.nn.sigmoid(_dot(xb, wg_ref[:, sl]) + bg_ref[:, sl])
        term = gate * _dot(h_ref[...], wbr_ref[c])
        merged = term if merged is None else merged + term
    mix = _dot(merged.astype(BF16), wo_ref[...])
    o_ref[...] = _layer_norm(ALPHA * x + mix, g_ref[...], b_ref[...])


def merge(x2d, hm, hn, hx, wg, bg, wbr, wo, g, b, tm):
    n = x2d.shape[0]
    row = lambda w: pl.BlockSpec((tm, w), lambda i: (i, 0))
    full = lambda a: pl.BlockSpec(a.shape, lambda i: (0,) * a.ndim)
    return pl.pallas_call(
        _merge_kernel,
        grid=(n // tm,),
        in_specs=[row(D_MODEL), row(BRANCH_WIDTH), row(BRANCH_WIDTH), row(BRANCH_WIDTH),
                  full(wg), full(bg), full(wbr), full(wo), full(g), full(b)],
        out_specs=row(D_MODEL),
        out_shape=jax.ShapeDtypeStruct((n, D_MODEL), F32),
        compiler_params=_params(("parallel",)),
        name="merge",
    )(x2d, hm, hn, hx, wg, bg, wbr, wo, g, b)


def _ffn_kernel(x_ref, cw_ref, wg_ref, wu_ref, wd_ref, g_ref, b_ref, o_ref, acc_s, *, routed):
    e = pl.program_id(1)
    f = pl.program_id(2)

    @pl.when((e == 0) & (f == 0))
    def _():
        acc_s[...] = jnp.zeros(acc_s.shape, F32)

    xb = x_ref[...].astype(BF16)
    gt = _dot(xb, wg_ref[0])
    up = _dot(xb, wu_ref[0])
    hcur = jax.nn.silu(gt) * up
    if routed:
        cw = cw_ref[...]
        lane = lax.broadcasted_iota(jnp.int32, cw.shape, 1)
        hcur = hcur * jnp.sum(jnp.where(lane == e, cw, 0.0), axis=-1, keepdims=True)
    acc_s[...] += _dot(hcur.astype(BF16), wd_ref[0])

    @pl.when((e == pl.num_programs(1) - 1) & (f == pl.num_programs(2) - 1))
    def _():
        o_ref[...] = _layer_norm(ALPHA * x_ref[...] + acc_s[...], g_ref[...], b_ref[...])


def ffn(x2d, cw, w_up, w_down, g, b, tm, tf, routed):
    n = x2d.shape[0]
    E, _, F2 = w_up.shape
    nf = (F2 // 2) // tf
    return pl.pallas_call(
        functools.partial(_ffn_kernel, routed=routed),
        grid=(n // tm, E, nf),
        in_specs=[pl.BlockSpec((tm, D_MODEL), lambda i, e, f: (i, 0)),
                  pl.BlockSpec((tm, cw.shape[1]), lambda i, e, f: (i, 0)),
                  pl.BlockSpec((1, D_MODEL, tf), lambda i, e, f: (e, 0, f)),
                  pl.BlockSpec((1, D_MODEL, tf), lambda i, e, f: (e, 0, nf + f)),
                  pl.BlockSpec((1, tf, D_MODEL), lambda i, e, f: (e, f, 0)),
                  pl.BlockSpec((1, D_MODEL), lambda i, e, f: (0, 0)),
                  pl.BlockSpec((1, D_MODEL), lambda i, e, f: (0, 0))],
        out_specs=pl.BlockSpec((tm, D_MODEL), lambda i, e, f: (i, 0)),
        out_shape=jax.ShapeDtypeStruct((n, D_MODEL), F32),
        scratch_shapes=[pltpu.VMEM((tm, D_MODEL), F32)],
        compiler_params=_params(("parallel", "arbitrary", "arbitrary")),
        name="moe_ffn" if routed else "dense_ffn",
    )(x2d, cw, w_up, w_up, w_down, g, b)


def _router_kernel(x_ref, w_ref, b_ref, o_ref):
    logits = _dot(x_ref[...], w_ref[...], HIGHEST) + b_ref[...]
    lane = lax.broadcasted_iota(jnp.int32, logits.shape, 1)
    W = logits.shape[1]
    logits = jnp.where(lane < N_EXPERTS, logits, NEG_INF)
    m1 = jnp.max(logits, axis=-1, keepdims=True)
    i1 = jnp.min(jnp.where(logits == m1, lane, W), axis=-1, keepdims=True)
    rest = jnp.where(lane == i1, NEG_INF, logits)
    m2 = jnp.max(rest, axis=-1, keepdims=True)
    i2 = jnp.min(jnp.where(rest == m2, lane, W), axis=-1, keepdims=True)
    e2 = jnp.exp(m2 - m1)
    den = 1.0 + e2
    o_ref[...] = jnp.where(lane == i1, 1.0 / den, 0.0) + jnp.where(lane == i2, e2 / den, 0.0)


def router(x2d, w_router, b_router, tm):
    n = x2d.shape[0]
    w = jnp.pad(w_router.astype(F32), ((0, 0), (0, LANES - N_EXPERTS)))
    bb = jnp.pad(b_router.astype(F32), (0, LANES - N_EXPERTS)).reshape(1, LANES)
    return pl.pallas_call(
        _router_kernel,
        grid=(n // tm,),
        in_specs=[pl.BlockSpec((tm, D_MODEL), lambda i: (i, 0)), pl.BlockSpec(w.shape, lambda i: (0, 0)),
                  pl.BlockSpec(bb.shape, lambda i: (0, 0))],
        out_specs=pl.BlockSpec((tm, LANES), lambda i: (i, 0)),
        out_shape=jax.ShapeDtypeStruct((n, LANES), F32),
        compiler_params=_params(("parallel",)),
        name="router",
    )(x2d, w, bb)


def _row_tile(n, pref):
    return pref if n % pref == 0 else n


def _layer(x, lw, l, *, mem_kv, mem_blk, mstate, decode):
    B, T, _ = x.shape
    n = B * T
    x2d = x.reshape(n, D_MODEL)
    tm = _row_tile(n, 512)
    za, zb, zc, zcb, zd = in_proj(x2d, lw["in"], tm)
    za, zb, zc, zcb, zd = (a.reshape(B, T, -1) for a in (za, zb, zc, zcb, zd))

    if T % 256 == 0:
        h_m, ct, nt, mt = mlstm(za, zd, *mstate, lw["norm_g"], L=256, t_valid=256)
    else:
        Lp = LANES
        padt = lambda a: jnp.pad(a, ((0, 0), (0, Lp - T), (0, 0)))
        h_m, ct, nt, mt = mlstm(padt(za), padt(zd), *mstate, lw["norm_g"], L=Lp, t_valid=T)
        h_m = h_m[:, :T]

    if decode is None:
        kc, vc = cmp_tokens(zc, lw["cmp_w"], lw["cmp_b"])
        h_n = nsa_prompt(zb, zd, zcb, kc, vc, tq=min(128, T), CH=min(512, T))
    else:
        h_n = nsa_decode(zb, zd, zcb, decode["nsa"], decode["win"], decode["pt"], lw["cmp_w"], lw["cmp_b"], l)

    h_x = mem_attn(zb, mem_kv, mem_blk[0], mem_blk[1], tq=_row_tile(T, 512))

    wg, bg = lw["in"]["mg"]
    flat = lambda a: a.reshape(n, -1)
    x1 = merge(x2d, flat(h_m), flat(h_n), flat(h_x), wg, bg, lw["w_branch"], lw["w_out"], lw["ln1_g"], lw["ln1_b"], tm)

    tmf = _row_tile(n, 1024)
    if l % 2 == 0:
        cw = jnp.zeros((n, LANES), F32)
        x2 = ffn(x1, cw, lw["ffn_up"], lw["ffn_down"], lw["ln2_g"], lw["ln2_b"], tmf, 256, routed=False)
    else:
        cw = router(x1, lw["w_router"], lw["b_router"], tmf)
        x2 = ffn(x1, cw, lw["ffn_up"], lw["ffn_down"], lw["ln2_g"], lw["ln2_b"], tmf, 512, routed=True)
    return x2.reshape(B, T, D_MODEL), zc, (ct, nt, mt)


def kernel(x_prompt, x_sample, mem_prompt, cache_nsa_kv, cache_win_kv, state_mlstm_C, state_mlstm_n,
           state_mlstm_m, cache_mem_kv, page_table, w_in, b_in, mlstm_norm_g, cmp_w, cmp_b, w_mem_kv,
           w_branch, w_out, ln1_g, ln1_b, ln2_g, ln2_b, ffn_w_up, ffn_w_down, moe_w_router, moe_b_router,
           moe_w_up, moe_w_down):
    B, T, _ = x_prompt.shape
    DB, TS, _ = x_sample.shape
    w_src = cache_win_kv.shape[1]
    xp, xs = x_prompt, x_sample
    nsa_p, nsa_s, win_p, win_s = [], [], [], []
    Cp, np_, mp, Cs, ns, ms, memkv_p = [], [], [], [], [], [], []
    mem_s = cache_mem_kv.reshape(DB, MEM_LEN, DEPTH * 2 * X_WIDTH)
    row1 = lambda a: a.reshape(1, -1).astype(F32)
    for l in range(DEPTH):
        lw = dict(norm_g=mlstm_norm_g[l], cmp_w=cmp_w[l], cmp_b=cmp_b[l], w_branch=w_branch[l].astype(BF16),
                  w_out=w_out[l].astype(BF16), ln1_g=row1(ln1_g[l]), ln1_b=row1(ln1_b[l]),
                  ln2_g=row1(ln2_g[l]), ln2_b=row1(ln2_b[l]))
        lw["in"] = _split_in_proj(w_in[l], b_in[l])
        if l % 2 == 0:
            lw["ffn_up"] = ffn_w_up[l // 2][None].astype(BF16)
            lw["ffn_down"] = ffn_w_down[l // 2][None].astype(BF16)
        else:
            lw["ffn_up"] = moe_w_up[l // 2].astype(BF16)
            lw["ffn_down"] = moe_w_down[l // 2].astype(BF16)
            lw["w_router"] = moe_w_router[l // 2]
            lw["b_router"] = moe_b_router[l // 2]

        mkv = proj(mem_prompt.reshape(B * MEM_LEN, D_MODEL), w_mem_kv[l].astype(BF16), _row_tile(B * MEM_LEN, 512))
        mkv = mkv.reshape(B, MEM_LEN, 2 * X_WIDTH)
        st0 = (jnp.zeros((B, M_HEADS, M_DH, M_DH), F32), jnp.zeros((B, M_HEADS, M_DH), F32), jnp.zeros((B, M_HEADS), F32))
        xp, zc, st = _layer(xp, lw, l, mem_kv=mkv, mem_blk=(0, 1), mstate=st0, decode=None)
        kvn = zc.reshape(B, T, 6, NSA_GROUPS, NSA_DH)
        nsa_p.append(kvn[:, :, :4])
        w_keep = min(WINDOW, T)
        win_p.append(kvn[:, T - w_keep:, 4:])
        Cp.append(st[0]); np_.append(st[1]); mp.append(st[2])
        memkv_p.append(mkv.reshape(B, MEM_LEN, 2, X_HEADS, X_DH))

        sts = (state_mlstm_C[:, l], state_mlstm_n[:, l], state_mlstm_m[:, l])
        dec = dict(nsa=cache_nsa_kv, win=cache_win_kv, pt=page_table)
        xs, zc, st = _layer(xs, lw, l, mem_kv=mem_s, mem_blk=(2 * l, 2 * l + 1), mstate=sts, decode=dec)
        kvn = zc.reshape(DB, TS, 6, NSA_GROUPS, NSA_DH)
        nsa_s.append(kvn[:, :, :4])
        win_s.append(jnp.concatenate([cache_win_kv[:, :, l].astype(F32), kvn[:, :, 4:]], axis=1)[:, TS:])
        Cs.append(st[0]); ns.append(st[1]); ms.append(st[2])
    return (xp, xs,
            jnp.stack(nsa_p, axis=2), jnp.stack(nsa_s, axis=2),
            jnp.stack(win_p, axis=2), jnp.stack(win_s, axis=2),
            jnp.stack(Cp, axis=1), jnp.stack(np_, axis=1), jnp.stack(mp, axis=1),
            jnp.stack(Cs, axis=1), jnp.stack(ns, axis=1), jnp.stack(ms, axis=1),
            jnp.stack(memkv_p, axis=2))
```

```python
import functools

import numpy as np
import jax
import jax.numpy as jnp
from jax import lax
from jax.experimental import pallas as pl
from jax.experimental.pallas import tpu as pltpu

D_MODEL = 1024
DEPTH = 2
BRANCH_WIDTH = 512
N_BRANCH = 3
M_HEADS = 4
M_DH = BRANCH_WIDTH // M_HEADS
M_WIDTH = M_HEADS * M_DH
NSA_HEADS = 8
NSA_DH = BRANCH_WIDTH // NSA_HEADS
NSA_WIDTH = NSA_HEADS * NSA_DH
NSA_GROUPS = 2
NSA_HPG = NSA_HEADS // NSA_GROUPS
NSA_KV = NSA_GROUPS * NSA_DH
CMP_LEN = 32
CMP_STRIDE = 16
SEL_LEN = 64
SEL_TOPN = 16
WINDOW = 512
MEM_LEN = 256
X_HEADS = 4
X_DH = BRANCH_WIDTH // X_HEADS
X_WIDTH = X_HEADS * X_DH
D_FF = 2816
N_EXPERTS = 8
TOP_K = 2
D_FF_EXPERT = 3584
ALPHA = (2.0 * DEPTH) ** 0.25
LN_EPS = 1e-5
IN_SPLITS = (M_WIDTH, M_WIDTH, M_WIDTH, M_HEADS, M_HEADS, M_WIDTH,
             NSA_WIDTH, 6 * NSA_KV, 3 * NSA_HEADS, X_WIDTH, N_BRANCH * D_MODEL)
NSA_SLOPES = tuple(2.0 ** (-8.0 * (h + 1) / NSA_HEADS) for h in range(NSA_HEADS))

LANES = 128
SUBLANES = 8
VMEM_LIMIT = 56 * 1024 * 1024
PAGES_PER_STEP = 16
GATE_COLS = LANES
NG_OFF = 2 * M_HEADS
NSA_QPAD = NSA_HEADS * LANES
MOE_BLOCK = 1024
MOE_CAP = 384
ROUTE_ROWS = 16

F32 = jnp.float32
BF16 = jnp.bfloat16
NEG_INF = float("-inf")
HIGHEST = lax.Precision.HIGHEST


def _dot(a, b, precision=None):
    return jnp.dot(a, b, preferred_element_type=F32, precision=precision)


def _dot_nt(a, b, precision=None):
    return lax.dot_general(a, b, (((1,), (1,)), ((), ())), preferred_element_type=F32, precision=precision)


def _dot_tn(a, b):
    return lax.dot_general(a, b, (((0,), (0,)), ((), ())), preferred_element_type=F32)


def _params(sem):
    return pltpu.CompilerParams(dimension_semantics=sem, vmem_limit_bytes=VMEM_LIMIT)


def _masked_softmax(s, mask):
    s = jnp.where(mask, s, NEG_INF)
    mx = jnp.max(s, axis=-1, keepdims=True)
    mx = jnp.where(mx > NEG_INF, mx, 0.0)
    p = jnp.where(mask, jnp.exp(s - mx), 0.0)
    return p / jnp.maximum(jnp.sum(p, axis=-1, keepdims=True), 1e-30)


def _layer_norm(xf, g, b):
    mu = jnp.mean(xf, axis=-1, keepdims=True)
    var = jnp.mean(jnp.square(xf - mu), axis=-1, keepdims=True)
    return (xf - mu) * lax.rsqrt(var + LN_EPS) * g + b


def _row_tile(n, pref):
    return pref if n % pref == 0 else n


def _in_proj_kernel(x_ref, wa_ref, wb_ref, wc_ref, wd_ref, ba_ref, bb_ref, bc_ref, bd_ref,
                    oa_ref, ob_ref, oc_ref, ocb_ref, od_ref):
    x = x_ref[...].astype(BF16)
    oa_ref[...] = (_dot(x, wa_ref[...]) + ba_ref[...]).astype(BF16)
    ob_ref[...] = (_dot(x, wb_ref[...]) + bb_ref[...]).astype(BF16)
    c = _dot(x, wc_ref[...]) + bc_ref[...]
    oc_ref[...] = c
    ocb_ref[...] = c.astype(BF16)
    od_ref[...] = _dot(x, wd_ref[...]) + bd_ref[...]


def _group_select():
    return jax.nn.one_hot(np.arange(NSA_HEADS) // NSA_HPG, NSA_GROUPS, dtype=F32)


def _pad_branch_rows(w_nsa):
    return jnp.einsum("hem,hg->hgem", w_nsa.reshape(NSA_HEADS, NSA_DH, -1), _group_select()).reshape(NSA_QPAD, -1)


def _split_in_proj(w_in_l, b_in_l):
    offs = np.cumsum((0,) + IN_SPLITS)
    w = [w_in_l[:, offs[i]:offs[i + 1]] for i in range(len(IN_SPLITS))]
    b = [b_in_l[offs[i]:offs[i + 1]] for i in range(len(IN_SPLITS))]
    mq, mk, mv, mi, mf, mo, nq, nkv, ng, xq, mg = range(11)
    pad = GATE_COLS - 2 * M_HEADS - 3 * NSA_HEADS

    def cat(ids, zpad=0):
        ww = jnp.concatenate([w[i] for i in ids], axis=1)
        bb = jnp.concatenate([b[i] for i in ids])
        if zpad:
            ww = jnp.pad(ww, ((0, 0), (0, zpad)))
            bb = jnp.pad(bb, (0, zpad))
        return ww.astype(BF16), bb.reshape(1, -1).astype(F32)

    scale = NSA_DH ** -0.5
    w[nq] = jnp.einsum("dhe,hg->dhge", w[nq].reshape(D_MODEL, NSA_HEADS, NSA_DH) * scale, _group_select()).reshape(D_MODEL, NSA_QPAD)
    b[nq] = jnp.einsum("he,hg->hge", b[nq].reshape(NSA_HEADS, NSA_DH) * scale, _group_select()).reshape(NSA_QPAD)
    return dict(a=cat([mq, mk, mv, mo]), b=cat([nq, xq]), c=cat([nkv]), d=cat([mi, mf, ng], pad), mg=cat([mg]))


def in_proj(x2d, wp, tm):
    n = x2d.shape[0]
    (wa, ba), (wb, bb), (wc, bc), (wd, bd) = wp["a"], wp["b"], wp["c"], wp["d"]
    full = lambda arr: pl.BlockSpec(arr.shape, lambda i: (0, 0))
    row = lambda w: pl.BlockSpec((tm, w), lambda i: (i, 0))
    return pl.pallas_call(
        _in_proj_kernel,
        grid=(n // tm,),
        in_specs=[row(D_MODEL), full(wa), full(wb), full(wc), full(wd), full(ba), full(bb), full(bc), full(bd)],
        out_specs=[row(wa.shape[1]), row(wb.shape[1]), row(wc.shape[1]), row(wc.shape[1]), row(wd.shape[1])],
        out_shape=[jax.ShapeDtypeStruct((n, wa.shape[1]), BF16), jax.ShapeDtypeStruct((n, wb.shape[1]), BF16),
                   jax.ShapeDtypeStruct((n, wc.shape[1]), F32), jax.ShapeDtypeStruct((n, wc.shape[1]), BF16),
                   jax.ShapeDtypeStruct((n, wd.shape[1]), F32)],
        compiler_params=_params(("parallel",)),
        name="in_proj",
    )(x2d, wa, wb, wc, wd, ba, bb, bc, bd)


def _proj_kernel(x_ref, w_ref, o_ref):
    o_ref[...] = _dot(x_ref[...].astype(BF16), w_ref[...])


def proj(x2d, w_bf16, tm):
    n, k = x2d.shape
    m = w_bf16.shape[1]
    return pl.pallas_call(
        _proj_kernel,
        grid=(n // tm,),
        in_specs=[pl.BlockSpec((tm, k), lambda i: (i, 0)), pl.BlockSpec((k, m), lambda i: (0, 0))],
        out_specs=pl.BlockSpec((tm, m), lambda i: (i, 0)),
        out_shape=jax.ShapeDtypeStruct((n, m), F32),
        compiler_params=_params(("parallel",)),
        name="mem_proj",
    )(x2d, w_bf16)


def _log_sigmoid(x):
    return jnp.minimum(x, 0.0) - jnp.log1p(jnp.exp(-jnp.abs(x)))


def _mlstm_kernel(q_ref, k_ref, v_ref, og_ref, gc_ref, gr_ref, c0_ref, n0_ref, m0_ref, ng_ref,
                  h_ref, ct_ref, nt_ref, mt_ref, c_s, n_s, m_s, *, L, t_valid):
    ci = pl.program_id(1)

    @pl.when(ci == 0)
    def _():
        c_s[...] = c0_ref[0]
        n_s[...] = n0_ref[0]
        m_s[...] = m0_ref[0]

    row = lax.broadcasted_iota(jnp.int32, (L, L), 0)
    col = lax.broadcasted_iota(jnp.int32, (L, L), 1)
    causal = row >= col
    tri = causal.astype(F32)
    tri_t = (row <= col).astype(F32)
    gc = gc_ref[0]
    gr = gr_ref[0]
    rvalid = lax.broadcasted_iota(jnp.int32, (L, GATE_COLS), 0) < t_valid
    cvalid = lax.broadcasted_iota(jnp.int32, (2 * M_HEADS, L), 1) < t_valid
    lf_c = jnp.where(rvalid, _log_sigmoid(gc), 0.0)
    lf_r = jnp.where(cvalid, _log_sigmoid(gr), 0.0)
    ig_c = jnp.where(rvalid, gc, NEG_INF)
    ig_r = jnp.where(cvalid, gr, NEG_INF)
    b_c = _dot(tri, lf_c, HIGHEST)
    b_r = _dot(lf_r, tri_t, HIGHEST)
    scale = M_DH ** -0.5

    for h in range(M_HEADS):
        sl = slice(h * M_DH, (h + 1) * M_DH)
        q = q_ref[0, :, sl]
        k = k_ref[0, :, sl]
        v = v_ref[0, :, sl]
        bc = b_c[:, M_HEADS + h:M_HEADS + h + 1]
        br = b_r[M_HEADS + h:M_HEADS + h + 1, :]
        igc = ig_c[:, h:h + 1]
        igr = ig_r[h:h + 1, :]
        m_prev = m_s[h, 0:1, 0:1]
        c_prev = c_s[h]
        n_prev = n_s[h:h + 1, :]

        dmat = jnp.where(causal, bc - br + igr, NEG_INF)
        inter = bc + m_prev
        m_t = jnp.maximum(inter, jnp.max(dmat, axis=-1, keepdims=True))
        s = _dot_nt(q, k) * scale * jnp.exp(dmat - m_t)
        sc_in = jnp.exp(inter - m_t)
        qf = q.astype(F32)
        num = sc_in * _dot_nt(q, c_prev.astype(BF16)) + _dot(s.astype(BF16), v)
        den = sc_in * jnp.sum(qf * n_prev, axis=-1, keepdims=True) + jnp.sum(s, axis=-1, keepdims=True)
        hh = num / jnp.maximum(jnp.abs(den), jnp.exp(-m_t))

        b_last = bc[L - 1:L, :]
        dec_c = b_last - bc + igc
        dec_r = b_last - br + igr
        m_new = jnp.maximum(b_last + m_prev, jnp.max(dec_r, axis=-1, keepdims=True))
        ws_c = jnp.exp(dec_c - m_new) * scale
        sc = jnp.exp(b_last + m_prev - m_new)
        vf = v.astype(F32)
        kf = k.astype(F32)
        c_new = sc * c_prev + _dot_tn((vf * ws_c).astype(BF16), k)
        n_new = sc * n_prev + jnp.sum(kf * ws_c, axis=0, keepdims=True)
        c_s[h] = c_new
        n_s[h:h + 1, :] = n_new
        m_s[h] = jnp.broadcast_to(m_new, m_s.shape[1:])

        og = og_ref[0, :, sl].astype(F32)
        hh = hh * jax.nn.sigmoid(og)
        mu = jnp.mean(hh, axis=-1, keepdims=True)
        var = jnp.mean(jnp.square(hh - mu), axis=-1, keepdims=True)
        hn = (hh - mu) * lax.rsqrt(var + LN_EPS) * ng_ref[:, sl]
        h_ref[0, :, sl] = hn.astype(h_ref.dtype)

    ct_ref[0] = c_s[...]
    nt_ref[0] = n_s[...]
    mt_ref[0] = m_s[...]


def mlstm(za, zd, c0, n0, m0, norm_g, *, L, t_valid):
    B, T, _ = za.shape
    nc = T // L
    gr = jnp.swapaxes(zd[:, :, :2 * M_HEADS], 1, 2)
    m0b = jnp.broadcast_to(m0.astype(F32)[:, :, None, None], (B, M_HEADS, SUBLANES, LANES))
    colspec = lambda j: pl.BlockSpec((1, L, M_WIDTH), lambda b, c, j=j: (b, c, j))
    st = lambda shp: pl.BlockSpec((1,) + shp, lambda b, c: (b,) + (0,) * len(shp))
    kern = functools.partial(_mlstm_kernel, L=L, t_valid=t_valid)
    h, ct, nt, mt = pl.pallas_call(
        kern,
        grid=(B, nc),
        in_specs=[colspec(0), colspec(1), colspec(2), colspec(3),
                  pl.BlockSpec((1, L, GATE_COLS), lambda b, c: (b, c, 0)),
                  pl.BlockSpec((1, 2 * M_HEADS, L), lambda b, c: (b, 0, c)),
                  st((M_HEADS, M_DH, M_DH)), st((M_HEADS, M_DH)), st((M_HEADS, SUBLANES, LANES)),
                  pl.BlockSpec((1, M_WIDTH), lambda b, c: (0, 0))],
        out_specs=[pl.BlockSpec((1, L, M_WIDTH), lambda b, c: (b, c, 0)),
                   st((M_HEADS, M_DH, M_DH)), st((M_HEADS, M_DH)), st((M_HEADS, SUBLANES, LANES))],
        out_shape=[jax.ShapeDtypeStruct((B, T, M_WIDTH), BF16),
                   jax.ShapeDtypeStruct((B, M_HEADS, M_DH, M_DH), F32),
                   jax.ShapeDtypeStruct((B, M_HEADS, M_DH), F32),
                   jax.ShapeDtypeStruct((B, M_HEADS, SUBLANES, LANES), F32)],
        scratch_shapes=[pltpu.VMEM((M_HEADS, M_DH, M_DH), F32), pltpu.VMEM((M_HEADS, M_DH), F32),
                        pltpu.VMEM((M_HEADS, SUBLANES, LANES), F32)],
        compiler_params=_params(("parallel", "arbitrary")),
        name="mlstm",
    )(za, za, za, za, zd, gr, c0.astype(F32), n0.astype(F32), m0b, norm_g.reshape(1, M_WIDTH).astype(F32))
    return h, ct, nt, mt[:, :, 0, 0]


def _mem_attn_kernel(q_ref, k_ref, v_ref, o_ref):
    scale = X_DH ** -0.5
    for h in range(X_HEADS):
        sl = slice(h * X_DH, (h + 1) * X_DH)
        s = _dot_nt(q_ref[0, :, sl], k_ref[0, :, sl].astype(BF16)) * scale
        mx = jnp.max(s, axis=-1, keepdims=True)
        p = jnp.exp(s - mx)
        p = p / jnp.sum(p, axis=-1, keepdims=True)
        o_ref[0, :, sl] = _dot(p.astype(BF16), v_ref[0, :, sl].astype(BF16)).astype(o_ref.dtype)


def mem_attn(zb, kv, k_blk, v_blk, tq):
    B, T, _ = zb.shape
    return pl.pallas_call(
        _mem_attn_kernel,
        grid=(B, T // tq),
        in_specs=[pl.BlockSpec((1, tq, X_WIDTH), lambda b, i: (b, i, NSA_QPAD // X_WIDTH)),
                  pl.BlockSpec((1, MEM_LEN, X_WIDTH), lambda b, i: (b, 0, k_blk)),
                  pl.BlockSpec((1, MEM_LEN, X_WIDTH), lambda b, i: (b, 0, v_blk))],
        out_specs=pl.BlockSpec((1, tq, X_WIDTH), lambda b, i: (b, i, 0)),
        out_shape=jax.ShapeDtypeStruct((B, T, X_WIDTH), BF16),
        compiler_params=_params(("parallel", "parallel")),
        name="mem_attn",
    )(zb, kv, kv)


def _head_rows(q_ref, nq):
    parts = [q_ref[0, :, hd * LANES:(hd + 1) * LANES] for hd in range(NSA_HEADS)]
    if nq % 16:
        return jnp.concatenate([p.astype(F32) for p in parts], axis=0).astype(BF16)
    return jnp.concatenate(parts, axis=0)


def _store_heads(o_ref, gate, o_c, o_s, o_w, nq):
    lane = lax.broadcasted_iota(jnp.int32, (nq, LANES), 1)
    for hd in range(NSA_HEADS):
        r = slice(hd * nq, (hd + 1) * nq)
        gcol = lambda br: gate[:, NG_OFF + br * NSA_HEADS + hd:NG_OFF + br * NSA_HEADS + hd + 1]
        val = gcol(0) * o_c[r] + gcol(1) * o_s[r] + gcol(2) * o_w[r]
        keep = (lane >= NSA_DH) if hd // NSA_HPG == 1 else (lane < NSA_DH)
        o_ref[0, :, hd * LANES:(hd + 1) * LANES] = jnp.where(keep, val, 0.0).astype(o_ref.dtype)


def _overlap(tok, blk):
    c_start = tok * CMP_STRIDE
    s_start = blk * SEL_LEN
    return ((c_start < s_start + SEL_LEN) & (c_start + CMP_LEN > s_start)).astype(F32)


def _select_blocks(imp, tpos, n_sel, n_top):
    nq, W = imp.shape
    blk = lax.broadcasted_iota(jnp.int32, (nq, W), 1)
    cur = tpos // SEL_LEN
    forced = (blk == 0) | (blk == cur) | (blk == cur - 1)
    v = jnp.where(forced, jnp.inf, jnp.where(blk <= cur, imp, NEG_INF))
    v = jnp.where(blk < n_sel, v, NEG_INF)
    sel = jnp.zeros((nq, W), F32)
    for _ in range(n_top):
        mx = jnp.max(v, axis=-1, keepdims=True)
        first = jnp.min(jnp.where(v == mx, blk, W), axis=-1, keepdims=True)
        pick = blk == first
        sel = jnp.where(pick, 1.0, sel)
        v = jnp.where(pick, NEG_INF, v)
    return sel


def _select_blocks_t(imp_t, tpos_row, n_sel, n_top):
    n_blk, nq = imp_t.shape
    blk = lax.broadcasted_iota(jnp.int32, (n_blk, nq), 0)
    cur = tpos_row // SEL_LEN
    forced = (blk == 0) | (blk == cur) | (blk == cur - 1)
    v = jnp.where(forced, jnp.inf, jnp.where(blk <= cur, imp_t, NEG_INF))
    v = jnp.where(blk < n_sel, v, NEG_INF)
    ahead = jnp.zeros((n_blk, nq), F32)
    for j in range(n_sel):
        vj = v[j:j + 1, :]
        ahead = ahead + jnp.where(vj > v, 1.0, jnp.where(vj == v, jnp.where(blk > j, 1.0, 0.0), 0.0))
    return jnp.where((ahead < n_top) & (blk < n_sel), 1.0, 0.0)


def _online_update(s, m_old, l_old):
    m_new = jnp.maximum(m_old, jnp.max(s, axis=-1, keepdims=True))
    alpha = jnp.exp(m_old - m_new)
    p = jnp.exp(s - m_new)
    return p, m_new, alpha, alpha * l_old + jnp.sum(p, axis=-1, keepdims=True)


def _cmp_tokens_kernel(xk_ref, xv_ref, bd_ref, b_ref, kc_ref, vc_ref, *, nsub):
    for c, (x_ref, o_ref) in enumerate(((xk_ref, kc_ref), (xv_ref, vc_ref))):
        a0 = jnp.zeros((nsub, LANES), F32)
        a1 = jnp.zeros((nsub, LANES), F32)
        for j in range(CMP_STRIDE):
            xj = x_ref[0, :, j, :].astype(BF16)
            a0 = a0 + _dot(xj, bd_ref[c, 0, j])
            a1 = a1 + _dot(xj, bd_ref[c, 1, j])
        tok = a0 + pltpu.roll(a1, nsub - 1, axis=0) + b_ref[c]
        o_ref[0] = tok.astype(o_ref.dtype)


def _cmp_blockdiag(cmp_w_l):
    R = CMP_LEN // CMP_STRIDE
    w = cmp_w_l.astype(F32).reshape(2, R, CMP_STRIDE, NSA_DH, NSA_DH)
    eye = jnp.eye(NSA_GROUPS, dtype=F32)
    return jnp.einsum("ab,crjde->crjadbe", eye, w).reshape(2, R, CMP_STRIDE, LANES, LANES).astype(BF16)


def _cmp_bias(cmp_b_l):
    return jnp.tile(cmp_b_l.astype(F32), (1, NSA_GROUPS)).reshape(2, 1, LANES)


def cmp_tokens(zc, cmp_w_l, cmp_b_l):
    B, T, W = zc.shape
    nsub = T // CMP_STRIDE
    x4 = zc.reshape(B, nsub, CMP_STRIDE, W)
    bd = _cmp_blockdiag(cmp_w_l)
    bias = _cmp_bias(cmp_b_l)
    spec = lambda j: pl.BlockSpec((1, nsub, CMP_STRIDE, LANES), lambda b, j=j: (b, 0, 0, j))
    return pl.pallas_call(
        functools.partial(_cmp_tokens_kernel, nsub=nsub),
        grid=(B,),
        in_specs=[spec(0), spec(1), pl.BlockSpec(bd.shape, lambda b: (0,) * 5), pl.BlockSpec(bias.shape, lambda b: (0, 0, 0))],
        out_specs=[pl.BlockSpec((1, nsub, LANES), lambda b: (b, 0, 0))] * 2,
        out_shape=[jax.ShapeDtypeStruct((B, nsub, LANES), BF16)] * 2,
        compiler_params=_params(("parallel",)),
        name="nsa_cmp_tokens",
    )(x4, x4, bd, bias)


def _nsa_prompt_kernel(q_ref, g_ref, kc_ref, vc_ref, ks_ref, vs_ref, kw_ref, vw_ref, o_ref, m_s, l_s, acc_s,
                       *, tq, T, CH):
    qi = pl.program_id(1)
    start = qi * tq
    ref_pos = start + tq
    n_cmp_rows = kc_ref.shape[1]
    n_sel = T // SEL_LEN
    n_top = min(SEL_TOPN, n_sel)
    n_blk = -(-n_sel // SUBLANES) * SUBLANES
    qp = _head_rows(q_ref, tq)
    tpos = start + lax.broadcasted_iota(jnp.int32, (tq, 1), 0)
    tpos_row = start + lax.broadcasted_iota(jnp.int32, (1, tq), 1)
    rows = lambda a, hd: a[hd * tq:(hd + 1) * tq]

    sc_all = _dot_nt(qp, kc_ref[0])
    c_end = lax.broadcasted_iota(jnp.int32, (1, n_cmp_rows), 1) * CMP_STRIDE + CMP_LEN - 1
    mask_c = tpos >= c_end
    rel_c = (c_end - ref_pos).astype(F32)
    pcs = [_masked_softmax(rows(sc_all, hd) + NSA_SLOPES[hd] * rel_c, mask_c) for hd in range(NSA_HEADS)]
    o_c = _dot(jnp.concatenate(pcs, axis=0).astype(BF16), vc_ref[0])
    ov_t = _overlap(lax.broadcasted_iota(jnp.int32, (n_blk, n_cmp_rows), 1),
                    lax.broadcasted_iota(jnp.int32, (n_blk, n_cmp_rows), 0))
    sels = []
    for g in range(NSA_GROUPS):
        pg = pcs[g * NSA_HPG]
        for p in range(1, NSA_HPG):
            pg = pg + pcs[g * NSA_HPG + p]
        sel_t = _select_blocks_t(_dot_nt(ov_t, pg, HIGHEST), tpos_row, n_sel, n_top)
        sel_t = jnp.concatenate([sel_t, jnp.zeros((LANES - n_blk, tq), F32)], axis=0)
        sels.append(sel_t.T.astype(BF16))

    m_s[...] = jnp.full(m_s.shape, NEG_INF, F32)
    l_s[...] = jnp.zeros(l_s.shape, F32)
    acc_s[...] = jnp.zeros(acc_s.shape, F32)

    def chunk(c, diag):
        base = pl.multiple_of(c * CH, CH)
        kch = ks_ref[0, pl.ds(base, CH), :]
        vch = vs_ref[0, pl.ds(base, CH), :]
        s_all = _dot_nt(qp, kch)
        kpos = base + lax.broadcasted_iota(jnp.int32, (1, CH), 1)
        rel = (kpos - ref_pos).astype(F32)
        jj = lax.broadcasted_iota(jnp.int32, (LANES, CH), 0)
        kk = base + lax.broadcasted_iota(jnp.int32, (LANES, CH), 1)
        expand = jnp.where(kk // SEL_LEN == jj, 1.0, 0.0).astype(BF16)
        ps = []
        for g in range(NSA_GROUPS):
            mk = _dot(sels[g], expand) > 0.5
            if diag:
                mk = mk & (tpos >= kpos)
            for p in range(NSA_HPG):
                hd = g * NSA_HPG + p
                r = slice(hd * tq, (hd + 1) * tq)
                s = jnp.where(mk, rows(s_all, hd) + NSA_SLOPES[hd] * rel, NEG_INF)
                pr, m_new, alpha, l_new = _online_update(s, m_s[r], l_s[r])
                m_s[r] = m_new
                l_s[r] = l_new
                acc_s[r] = alpha * acc_s[r]
                ps.append(pr)
        acc_s[...] = acc_s[...] + _dot(jnp.concatenate(ps, axis=0).astype(BF16), vch)

    def full_chunk(c, carry):
        chunk(c, False)
        return carry

    n_full = start // CH
    lax.fori_loop(0, n_full, full_chunk, 0)
    chunk(n_full, True)
    o_s = acc_s[...] / jnp.maximum(l_s[...], 1e-30)

    wk = WINDOW + tq
    base_w = pl.multiple_of(start, tq)
    sw_all = _dot_nt(qp, kw_ref[0, pl.ds(base_w, wk), :])
    jw = lax.broadcasted_iota(jnp.int32, (1, wk), 1)
    iw = lax.broadcasted_iota(jnp.int32, (tq, 1), 0)
    dw = iw - jw + WINDOW
    mask_w = (dw >= 0) & (dw < WINDOW) & (start - WINDOW + jw >= 0)
    rel_w = (jw - wk).astype(F32)
    pws = [_masked_softmax(rows(sw_all, hd) + NSA_SLOPES[hd] * rel_w, mask_w) for hd in range(NSA_HEADS)]
    o_w = _dot(jnp.concatenate(pws, axis=0).astype(BF16), vw_ref[0, pl.ds(base_w, wk), :])

    _store_heads(o_ref, jax.nn.sigmoid(g_ref[0]), o_c, o_s, o_w, tq)


def nsa_prompt(zb, zd, zcb, kc, vc, tq, CH):
    B, T, _ = zb.shape
    assert T % CH == 0 and CH % tq == 0
    kw = jnp.pad(zcb[:, :, 4 * NSA_KV:5 * NSA_KV], ((0, 0), (WINDOW, 0), (0, 0)))
    vw = jnp.pad(zcb[:, :, 5 * NSA_KV:6 * NSA_KV], ((0, 0), (WINDOW, 0), (0, 0)))
    nsub = kc.shape[1]
    per_b = lambda rows_, j: pl.BlockSpec((1, rows_, LANES), lambda b, i, j=j: (b, 0, j))
    kern = functools.partial(_nsa_prompt_kernel, tq=tq, T=T, CH=CH)
    return pl.pallas_call(
        kern,
        grid=(B, T // tq),
        in_specs=[pl.BlockSpec((1, tq, NSA_QPAD), lambda b, i: (b, i, 0)),
                  pl.BlockSpec((1, tq, GATE_COLS), lambda b, i: (b, i, 0)),
                  per_b(nsub, 0), per_b(nsub, 0), per_b(T, 2), per_b(T, 3),
                  per_b(T + WINDOW, 0), per_b(T + WINDOW, 0)],
        out_specs=pl.BlockSpec((1, tq, NSA_QPAD), lambda b, i: (b, i, 0)),
        out_shape=jax.ShapeDtypeStruct((B, T, NSA_QPAD), BF16),
        scratch_shapes=[pltpu.VMEM((NSA_HEADS * tq, 1), F32), pltpu.VMEM((NSA_HEADS * tq, 1), F32),
                        pltpu.VMEM((NSA_HEADS * tq, LANES), F32)],
        compiler_params=_params(("parallel", "arbitrary")),
        name="nsa_prompt",
    )(zb, zd, kc, vc, zcb, zcb, kw, vw)


def _page_specs(shape, slot_blk, layer, n_pages):
    def mk(i):
        def imap(b, s, pt):
            return (pt[b * n_pages + s * PAGES_PER_STEP + i], layer, slot_blk, 0, 0)
        return pl.BlockSpec(shape, imap)
    return [mk(i) for i in range(PAGES_PER_STEP)]


def _nsa_dec_cmp_kernel(pt_ref, *refs, T, past_len, n_rows):
    pages = refs[:PAGES_PER_STEP]
    perm_ref, bd_ref, b_ref, q_ref, oc_ref, sel_ref, tok_s, pend_s = refs[PAGES_PER_STEP:]
    s = pl.program_id(1)
    n_steps = pl.num_programs(1)
    page_rows = pages[0].shape[-1]
    sub = page_rows // CMP_STRIDE
    R = PAGES_PER_STEP * sub

    @pl.when(s == 0)
    def _():
        pend_s[...] = jnp.zeros(pend_s.shape, F32)

    rid = lax.broadcasted_iota(jnp.int32, (R, LANES), 0)
    for c in range(2):
        xs = [_dot_nt(perm_ref[...], pg[0, 0, c].astype(BF16)) for pg in pages]
        a0 = jnp.zeros((R, LANES), F32)
        a1 = jnp.zeros((R, LANES), F32)
        for j in range(CMP_STRIDE):
            xj = jnp.concatenate([x[j * sub:(j + 1) * sub] for x in xs], axis=0).astype(BF16)
            a0 = a0 + _dot(xj, bd_ref[c, 0, j])
            a1 = a1 + _dot(xj, bd_ref[c, 1, j])
        a0 = a0 + b_ref[c]
        tok = jnp.where(rid == 0, pend_s[c], pltpu.roll(a0, 1, axis=0)) + a1
        tok_s[c, pl.ds(pl.multiple_of(s * R, R), R), :] = tok
        pend_s[c] = a0[R - 1:R, :]

    @pl.when(s == n_steps - 1)
    def _():
        qp = _head_rows(q_ref, T)
        kc = tok_s[0].astype(BF16)
        vc = tok_s[1].astype(BF16)
        sc_all = _dot_nt(qp, kc)
        tpos = past_len + lax.broadcasted_iota(jnp.int32, (T, 1), 0)
        r = lax.broadcasted_iota(jnp.int32, (T, n_rows), 1)
        dc = tpos - ((r - 1) * CMP_STRIDE + CMP_LEN - 1)
        mask_c = (dc >= 0) & (r >= 1)
        dcf = dc.astype(F32)
        pcs = [_masked_softmax(sc_all[hd * T:(hd + 1) * T] - NSA_SLOPES[hd] * dcf, mask_c) for hd in range(NSA_HEADS)]
        oc_ref[0] = _dot(jnp.concatenate(pcs, axis=0).astype(BF16), vc)
        W = sel_ref.shape[2]
        n_sel = -(-(past_len + T) // SEL_LEN)
        ov = _overlap(lax.broadcasted_iota(jnp.int32, (n_rows, W), 0) - 1, lax.broadcasted_iota(jnp.int32, (n_rows, W), 1))
        for g in range(NSA_GROUPS):
            pg = pcs[g * NSA_HPG]
            for p in range(1, NSA_HPG):
                pg = pg + pcs[g * NSA_HPG + p]
            imp = _dot(pg, ov, HIGHEST)
            sel_ref[0, g * T:(g + 1) * T, :] = _select_blocks(imp, tpos, n_sel, min(SEL_TOPN, n_sel))


def _nsa_dec_sel_kernel(pt_ref, *refs, T, past_len):
    pages = refs[:PAGES_PER_STEP]
    q_ref, sel_ref, sel_last_ref, ex_ref, kn_ref, vn_ref, os_ref, m_s, l_s, acc_s = refs[PAGES_PER_STEP:]
    s = pl.program_id(1)
    n_steps = pl.num_programs(1)
    page_rows = pages[0].shape[-1]
    CH = PAGES_PER_STEP * page_rows

    @pl.when(s == 0)
    def _():
        m_s[...] = jnp.full(m_s.shape, NEG_INF, F32)
        l_s[...] = jnp.zeros(l_s.shape, F32)
        acc_s[...] = jnp.zeros(acc_s.shape, F32)

    qp = _head_rows(q_ref, T)
    tpos = past_len + lax.broadcasted_iota(jnp.int32, (T, 1), 0)

    def update(s_all, mask_of_group, kpos, pv):
        ds = tpos - kpos
        dsf = ds.astype(F32)
        ps = []
        for g in range(NSA_GROUPS):
            mk = mask_of_group(g) & (ds >= 0)
            for p in range(NSA_HPG):
                hd = g * NSA_HPG + p
                r = slice(hd * T, (hd + 1) * T)
                sc = jnp.where(mk, s_all[r] - NSA_SLOPES[hd] * dsf, NEG_INF)
                pr, m_new, alpha, l_new = _online_update(sc, m_s[r], l_s[r])
                m_s[r] = m_new
                l_s[r] = l_new
                acc_s[r] = alpha * acc_s[r]
                ps.append(pr)
        acc_s[...] = acc_s[...] + pv(jnp.concatenate(ps, axis=0).astype(BF16))

    selk = _dot(sel_ref[0, 0].astype(BF16), ex_ref[...])
    s_all = jnp.concatenate([_dot(qp, pg[0, 0, 0].astype(BF16)) for pg in pages], axis=1)
    kpos = s * CH + lax.broadcasted_iota(jnp.int32, (T, CH), 1)

    def pv_pages(pmat):
        out = jnp.zeros((NSA_HEADS * T, LANES), F32)
        for i, pg in enumerate(pages):
            out = out + _dot_nt(pmat[:, i * page_rows:(i + 1) * page_rows], pg[0, 0, 1].astype(BF16))
        return out

    update(s_all, lambda g: selk[g * T:(g + 1) * T] > 0.5, kpos, pv_pages)

    @pl.when(s == n_steps - 1)
    def _():
        nk = kn_ref.shape[1]
        sn = _dot_nt(qp, kn_ref[0])
        lane = lax.broadcasted_iota(jnp.int32, (T, nk), 1)
        sl = sel_last_ref[0, 0]
        update(sn, lambda g: (sl[g * T:(g + 1) * T, 0:1] > 0.5) & (lane < T), past_len + lane,
               lambda pmat: _dot(pmat, vn_ref[0]))
        os_ref[0] = acc_s[...] / jnp.maximum(l_s[...], 1e-30)


def _nsa_dec_win_kernel(q_ref, g_ref, oc_ref, os_ref, wp_ref, kwn_ref, vwn_ref, o_ref, *, T, past_len):
    qp = _head_rows(q_ref, T)
    w_src = wp_ref.shape[-1]
    nk = kwn_ref.shape[1]
    sw_all = jnp.concatenate([_dot(qp, wp_ref[0, 0, 0].astype(BF16)), _dot_nt(qp, kwn_ref[0])], axis=1)
    j = lax.broadcasted_iota(jnp.int32, (T, w_src + nk), 1)
    tpos = past_len + lax.broadcasted_iota(jnp.int32, (T, 1), 0)
    wpos = past_len - w_src + j
    dw = tpos - wpos
    mask_w = (dw >= 0) & (dw < WINDOW) & (wpos >= 0) & (j < w_src + T)
    dwf = dw.astype(F32)
    pws = [_masked_softmax(sw_all[hd * T:(hd + 1) * T] - NSA_SLOPES[hd] * dwf, mask_w) for hd in range(NSA_HEADS)]
    pw = jnp.concatenate(pws, axis=0).astype(BF16)
    o_w = _dot_nt(pw[:, :w_src], wp_ref[0, 0, 1].astype(BF16)) + _dot(pw[:, w_src:], vwn_ref[0])
    _store_heads(o_ref, jax.nn.sigmoid(g_ref[0]), oc_ref[0], os_ref[0], o_w, T)


def nsa_decode(zb, zd, zcb, cache_nsa_kv, cache_win_kv, page_table, cmp_w_l, cmp_b_l, layer):
    DB, T, _ = zb.shape
    n_pool, page_rows = cache_nsa_kv.shape[:2]
    n_pages = page_table.shape[1]
    past_len = n_pages * page_rows
    n_steps = n_pages // PAGES_PER_STEP
    sub_per_page = page_rows // CMP_STRIDE
    n_rows = n_pages * sub_per_page
    assert (n_rows - 1) * CMP_STRIDE + CMP_LEN - 1 > past_len + T - 1
    assert T <= CMP_STRIDE and n_pages % PAGES_PER_STEP == 0 and page_rows == LANES
    pt = page_table.reshape(-1).astype(jnp.int32)
    cache_t = jnp.transpose(cache_nsa_kv, (0, 2, 3, 4, 5, 1)).reshape(n_pool, DEPTH, 4, NSA_KV, page_rows)
    page_blk = (1, 1, 2, NSA_KV, page_rows)

    rr = np.arange(page_rows)
    perm = jnp.asarray(rr[None, :] == ((rr % sub_per_page) * CMP_STRIDE + rr // sub_per_page)[:, None], BF16)
    bd = _cmp_blockdiag(cmp_w_l)
    bias = _cmp_bias(cmp_b_l)

    n_sel = -(-(past_len + T) // SEL_LEN)
    blocks_per_step = PAGES_PER_STEP * page_rows // SEL_LEN
    sel_used = (n_steps + 1) * blocks_per_step
    sel_w = -(-sel_used // LANES) * LANES
    assert sel_used >= n_sel and blocks_per_step <= LANES
    q_spec = pl.BlockSpec((1, T, NSA_QPAD), lambda b, s, pt: (b, 0, 0))
    const = lambda a: pl.BlockSpec(a.shape, lambda b, s, pt: (0,) * a.ndim)

    o_c, sel = pl.pallas_call(
        functools.partial(_nsa_dec_cmp_kernel, T=T, past_len=past_len, n_rows=n_rows),
        grid_spec=pltpu.PrefetchScalarGridSpec(
            num_scalar_prefetch=1, grid=(DB, n_steps),
            in_specs=_page_specs(page_blk, 0, layer, n_pages) + [const(perm), const(bd), const(bias), q_spec],
            out_specs=[pl.BlockSpec((1, NSA_HEADS * T, LANES), lambda b, s, pt: (b, 0, 0)),
                       pl.BlockSpec((1, NSA_GROUPS * T, sel_w), lambda b, s, pt: (b, 0, 0))],
            scratch_shapes=[pltpu.VMEM((2, n_rows, LANES), F32), pltpu.VMEM((2, 1, LANES), F32)]),
        out_shape=[jax.ShapeDtypeStruct((DB, NSA_HEADS * T, LANES), F32),
                   jax.ShapeDtypeStruct((DB, NSA_GROUPS * T, sel_w), F32)],
        compiler_params=_params(("parallel", "arbitrary")),
        name="nsa_dec_cmp",
    )(pt, *([cache_t] * PAGES_PER_STEP), perm, bd, bias, zb)

    sel_steps = sel[:, :, :sel_used].reshape(DB, NSA_GROUPS * T, n_steps + 1, blocks_per_step).transpose(0, 2, 1, 3)
    sel_steps = jnp.pad(sel_steps, ((0, 0), (0, 0), (0, 0), (0, LANES - blocks_per_step)))
    kk = np.arange(PAGES_PER_STEP * page_rows) // SEL_LEN
    expand = jnp.asarray(kk[None, :] == np.arange(LANES)[:, None], BF16)
    pad_rows = LANES - T
    new_rows = lambda slot: jnp.pad(zcb[:, :, slot * NSA_KV:(slot + 1) * NSA_KV], ((0, 0), (0, pad_rows), (0, 0)))
    new_spec = pl.BlockSpec((1, LANES, LANES), lambda b, s, pt: (b, 0, 0))
    sel_blk = (1, 1, NSA_GROUPS * T, LANES)

    o_s = pl.pallas_call(
        functools.partial(_nsa_dec_sel_kernel, T=T, past_len=past_len),
        grid_spec=pltpu.PrefetchScalarGridSpec(
            num_scalar_prefetch=1, grid=(DB, n_steps),
            in_specs=_page_specs(page_blk, 1, layer, n_pages)
            + [q_spec, pl.BlockSpec(sel_blk, lambda b, s, pt: (b, s, 0, 0)),
               pl.BlockSpec(sel_blk, lambda b, s, pt: (b, n_steps, 0, 0)), const(expand), new_spec, new_spec],
            out_specs=pl.BlockSpec((1, NSA_HEADS * T, LANES), lambda b, s, pt: (b, 0, 0)),
            scratch_shapes=[pltpu.VMEM((NSA_HEADS * T, 1), F32), pltpu.VMEM((NSA_HEADS * T, 1), F32),
                            pltpu.VMEM((NSA_HEADS * T, LANES), F32)]),
        out_shape=jax.ShapeDtypeStruct((DB, NSA_HEADS * T, LANES), F32),
        compiler_params=_params(("parallel", "arbitrary")),
        name="nsa_dec_sel",
    )(pt, *([cache_t] * PAGES_PER_STEP), zb, sel_steps, sel_steps, expand, new_rows(2), new_rows(3))

    w_src = cache_win_kv.shape[1]
    win_t = jnp.transpose(cache_win_kv, (0, 2, 3, 4, 5, 1)).reshape(DB, DEPTH, 2, NSA_KV, w_src)
    b3 = lambda shp: pl.BlockSpec(shp, lambda b: (b, 0, 0))
    return pl.pallas_call(
        functools.partial(_nsa_dec_win_kernel, T=T, past_len=past_len),
        grid=(DB,),
        in_specs=[b3((1, T, NSA_QPAD)), b3((1, T, GATE_COLS)), b3((1, NSA_HEADS * T, LANES)), b3((1, NSA_HEADS * T, LANES)),
                  pl.BlockSpec((1, 1, 2, NSA_KV, w_src), lambda b: (b, layer, 0, 0, 0)),
                  b3((1, LANES, LANES)), b3((1, LANES, LANES))],
        out_specs=b3((1, T, NSA_QPAD)),
        out_shape=jax.ShapeDtypeStruct((DB, T, NSA_QPAD), BF16),
        compiler_params=_params(("parallel",)),
        name="nsa_dec_win",
    )(zb, zd, o_c, o_s, win_t, new_rows(4), new_rows(5))


def _merge_kernel(x_ref, hm_ref, hn_ref, hx_ref, wg_ref, bg_ref, wbm_ref, wbn_ref, wbx_ref, wo_ref, g_ref, b_ref, o_ref):
    x = x_ref[...]
    xb = x.astype(BF16)
    merged = None
    for c, (h_ref, w_ref) in enumerate(((hm_ref, wbm_ref), (hn_ref, wbn_ref), (hx_ref, wbx_ref))):
        sl = slice(c * D_MODEL, (c + 1) * D_MODEL)
        gate = jax.nn.sigmoid(_dot(xb, wg_ref[:, sl]) + bg_ref[:, sl])
        term = gate * _dot(h_ref[...], w_ref[...])
        merged = term if merged is None else merged + term
    mix = _dot(merged.astype(BF16), wo_ref[...])
    o_ref[...] = _layer_norm(ALPHA * x + mix, g_ref[...], b_ref[...])


def merge(x2d, hm, hn, hx, wg, bg, wbr, wo, g, b, tm):
    n = x2d.shape[0]
    row = lambda a: pl.BlockSpec((tm, a.shape[1]), lambda i: (i, 0))
    full = lambda a: pl.BlockSpec(a.shape, lambda i: (0,) * a.ndim)
    return pl.pallas_call(
        _merge_kernel,
        grid=(n // tm,),
        in_specs=[row(x2d), row(hm), row(hn), row(hx), full(wg), full(bg), full(wbr[0]), full(wbr[1]), full(wbr[2]),
                  full(wo), full(g), full(b)],
        out_specs=row(x2d),
        out_shape=jax.ShapeDtypeStruct((n, D_MODEL), F32),
        compiler_params=_params(("parallel",)),
        name="merge",
    )(x2d, hm, hn, hx, wg, bg, wbr[0], wbr[1], wbr[2], wo, g, b)


def _ffn_kernel(x_ref, wg_ref, wu_ref, wd_ref, g_ref, b_ref, o_ref, acc_s):
    f = pl.program_id(1)

    @pl.when(f == 0)
    def _():
        acc_s[...] = jnp.zeros(acc_s.shape, F32)

    xb = x_ref[...].astype(BF16)
    hcur = jax.nn.silu(_dot(xb, wg_ref[...])) * _dot(xb, wu_ref[...])
    acc_s[...] += _dot(hcur.astype(BF16), wd_ref[...])

    @pl.when(f == pl.num_programs(1) - 1)
    def _():
        o_ref[...] = _layer_norm(ALPHA * x_ref[...] + acc_s[...], g_ref[...], b_ref[...])


def ffn(x2d, w_up, w_down, g, b, tm, tf):
    n = x2d.shape[0]
    nf = (w_up.shape[1] // 2) // tf
    return pl.pallas_call(
        _ffn_kernel,
        grid=(n // tm, nf),
        in_specs=[pl.BlockSpec((tm, D_MODEL), lambda i, f: (i, 0)),
                  pl.BlockSpec((D_MODEL, tf), lambda i, f: (0, f)),
                  pl.BlockSpec((D_MODEL, tf), lambda i, f: (0, nf + f)),
                  pl.BlockSpec((tf, D_MODEL), lambda i, f: (f, 0)),
                  pl.BlockSpec((1, D_MODEL), lambda i, f: (0, 0)),
                  pl.BlockSpec((1, D_MODEL), lambda i, f: (0, 0))],
        out_specs=pl.BlockSpec((tm, D_MODEL), lambda i, f: (i, 0)),
        out_shape=jax.ShapeDtypeStruct((n, D_MODEL), F32),
        scratch_shapes=[pltpu.VMEM((tm, D_MODEL), F32)],
        compiler_params=_params(("parallel", "arbitrary")),
        name="dense_ffn",
    )(x2d, w_up, w_up, w_down, g, b)


def _router_kernel(x_ref, w_ref, b_ref, o_ref):
    logits = _dot(x_ref[...], w_ref[...], HIGHEST) + b_ref[...]
    lane = lax.broadcasted_iota(jnp.int32, logits.shape, 1)
    W = logits.shape[1]
    logits = jnp.where(lane < N_EXPERTS, logits, NEG_INF)
    m1 = jnp.max(logits, axis=-1, keepdims=True)
    i1 = jnp.min(jnp.where(logits == m1, lane, W), axis=-1, keepdims=True)
    rest = jnp.where(lane == i1, NEG_INF, logits)
    m2 = jnp.max(rest, axis=-1, keepdims=True)
    i2 = jnp.min(jnp.where(rest == m2, lane, W), axis=-1, keepdims=True)
    e2 = jnp.exp(m2 - m1)
    den = 1.0 + e2
    o_ref[...] = jnp.where(lane == i1, 1.0 / den, 0.0) + jnp.where(lane == i2, e2 / den, 0.0)


def router(x2d, w_router, b_router, tm):
    n = x2d.shape[0]
    w = jnp.pad(w_router.astype(F32), ((0, 0), (0, LANES - N_EXPERTS)))
    bb = jnp.pad(b_router.astype(F32), (0, LANES - N_EXPERTS)).reshape(1, LANES)
    return pl.pallas_call(
        _router_kernel,
        grid=(n // tm,),
        in_specs=[pl.BlockSpec((tm, D_MODEL), lambda i: (i, 0)), pl.BlockSpec(w.shape, lambda i: (0, 0)),
                  pl.BlockSpec(bb.shape, lambda i: (0, 0))],
        out_specs=pl.BlockSpec((tm, LANES), lambda i: (i, 0)),
        out_shape=jax.ShapeDtypeStruct((n, LANES), F32),
        compiler_params=_params(("parallel",)),
        name="router",
    )(x2d, w, bb)


def _moe_kernel(np_ref, x_ref, cw_ref, cwt_ref, wg_ref, wu_ref, wd_ref, g_ref, b_ref, o_ref,
                xb_s, y_s, pos_s, post_s, xg_s, ws_s, acc_s, *, blk, cap):
    j = pl.program_id(0)
    e = pl.program_id(1)
    f = pl.program_id(2)
    n_e = pl.num_programs(1)
    n_f = pl.num_programs(2)
    n_pass = np_ref[j * n_e + e]
    RC = min(256, blk)

    @pl.when((e == 0) & (f == 0))
    def _():
        xb_s[...] = x_ref[...].astype(BF16)
        y_s[...] = jnp.zeros(y_s.shape, F32)
        cw = cw_ref[...]
        cwt = cwt_ref[...]
        routed = jnp.where(cw != 0.0, 1.0, 0.0).astype(BF16)
        routed_t = jnp.where(cwt != 0.0, 1.0, 0.0).astype(BF16)
        for rc in range(blk // RC):
            rows_i = rc * RC + lax.broadcasted_iota(jnp.int32, (RC, blk), 0)
            cols_i = lax.broadcasted_iota(jnp.int32, (RC, blk), 1)
            before = jnp.where(cols_i < rows_i, 1.0, 0.0).astype(BF16)
            cnt = _dot(before, routed)
            pos_s[rc * RC:(rc + 1) * RC, :] = jnp.where(cw[rc * RC:(rc + 1) * RC] != 0.0, cnt, -1.0)
            rows_j = lax.broadcasted_iota(jnp.int32, (blk, RC), 0)
            cols_j = rc * RC + lax.broadcasted_iota(jnp.int32, (blk, RC), 1)
            before_t = jnp.where(rows_j < cols_j, 1.0, 0.0).astype(BF16)
            cnt_t = _dot(routed_t, before_t)
            post_s[:, rc * RC:(rc + 1) * RC] = jnp.where(cwt[:, rc * RC:(rc + 1) * RC] != 0.0, cnt_t, -1.0)

    @pl.when(f == 0)
    def _():
        prow = post_s[pl.ds(e, 1), :]
        wrow = cwt_ref[pl.ds(e, 1), :]

        def gather(u, carry):
            slot = (u * cap + lax.broadcasted_iota(jnp.int32, (cap, 1), 0)).astype(F32)
            hit = prow == slot
            xg_s[u] = _dot(jnp.where(hit, 1.0, 0.0).astype(BF16), xb_s[...]).astype(BF16)
            ws_s[u] = jnp.sum(jnp.where(hit, wrow, 0.0), axis=-1, keepdims=True)
            acc_s[u] = jnp.zeros(acc_s.shape[1:], F32)
            return carry

        lax.fori_loop(0, n_pass, gather, 0)

    def expert(u, carry):
        xg = xg_s[u]
        hcur = jax.nn.silu(_dot(xg, wg_ref[0])) * _dot(xg, wu_ref[0]) * ws_s[u]
        acc_s[u] += _dot(hcur.astype(BF16), wd_ref[0])
        return carry

    lax.fori_loop(0, n_pass, expert, 0)

    @pl.when(f == n_f - 1)
    def _():
        lane = lax.broadcasted_iota(jnp.int32, pos_s.shape, 1)
        pcol = jnp.sum(jnp.where(lane == e, pos_s[...], 0.0), axis=-1, keepdims=True)

        def scatter(u, carry):
            slot = (u * cap + lax.broadcasted_iota(jnp.int32, (1, cap), 1)).astype(F32)
            hit = jnp.where(pcol == slot, 1.0, 0.0).astype(BF16)
            y_s[...] += _dot(hit, acc_s[u].astype(BF16))
            return carry

        lax.fori_loop(0, n_pass, scatter, 0)

    @pl.when((e == n_e - 1) & (f == n_f - 1))
    def _():
        o_ref[...] = _layer_norm(ALPHA * x_ref[...] + y_s[...], g_ref[...], b_ref[...])


def moe(x2d, cw, w_up, w_down, g, b, tf):
    n = x2d.shape[0]
    E, _, F2 = w_up.shape
    nf = (F2 // 2) // tf
    blk = _row_tile(n, MOE_BLOCK)
    cap = min(MOE_CAP, blk)
    max_pass = -(-blk // cap)
    nblk = n // blk
    cwt = jnp.swapaxes(cw[:, :ROUTE_ROWS], 0, 1)
    counts = jnp.sum((cw[:, :E] != 0.0).reshape(nblk, blk, E), axis=1)
    n_pass = ((counts + cap - 1) // cap).astype(jnp.int32).reshape(-1)
    return pl.pallas_call(
        functools.partial(_moe_kernel, blk=blk, cap=cap),
        grid_spec=pltpu.PrefetchScalarGridSpec(
            num_scalar_prefetch=1, grid=(nblk, E, nf),
            in_specs=[pl.BlockSpec((blk, D_MODEL), lambda j, e, f, npr: (j, 0)),
                      pl.BlockSpec((blk, LANES), lambda j, e, f, npr: (j, 0)),
                      pl.BlockSpec((ROUTE_ROWS, blk), lambda j, e, f, npr: (0, j)),
                      pl.BlockSpec((1, D_MODEL, tf), lambda j, e, f, npr: (e, 0, f)),
                      pl.BlockSpec((1, D_MODEL, tf), lambda j, e, f, npr: (e, 0, nf + f)),
                      pl.BlockSpec((1, tf, D_MODEL), lambda j, e, f, npr: (e, f, 0)),
                      pl.BlockSpec((1, D_MODEL), lambda j, e, f, npr: (0, 0)),
                      pl.BlockSpec((1, D_MODEL), lambda j, e, f, npr: (0, 0))],
            out_specs=pl.BlockSpec((blk, D_MODEL), lambda j, e, f, npr: (j, 0)),
            scratch_shapes=[pltpu.VMEM((blk, D_MODEL), BF16), pltpu.VMEM((blk, D_MODEL), F32),
                            pltpu.VMEM((blk, LANES), F32), pltpu.VMEM((ROUTE_ROWS, blk), F32),
                            pltpu.VMEM((max_pass, cap, D_MODEL), BF16), pltpu.VMEM((max_pass, cap, 1), F32),
                            pltpu.VMEM((max_pass, cap, D_MODEL), F32)]),
        out_shape=jax.ShapeDtypeStruct((n, D_MODEL), F32),
        compiler_params=_params(("parallel", "arbitrary", "arbitrary")),
        name="moe_ffn",
    )(n_pass, x2d, cw, cwt, w_up, w_up, w_down, g, b)


def _layer(x, lw, l, *, mem_kv, mem_blk, mstate, decode):
    B, T, _ = x.shape
    n = B * T
    x2d = x.reshape(n, D_MODEL)
    tm = _row_tile(n, 512)
    za, zb, zc, zcb, zd = in_proj(x2d, lw["in"], tm)
    za, zb, zc, zcb, zd = (a.reshape(B, T, -1) for a in (za, zb, zc, zcb, zd))

    if T % 256 == 0:
        h_m, ct, nt, mt = mlstm(za, zd, *mstate, lw["norm_g"], L=256, t_valid=256)
    else:
        padt = lambda a: jnp.pad(a, ((0, 0), (0, LANES - T), (0, 0)))
        h_m, ct, nt, mt = mlstm(padt(za), padt(zd), *mstate, lw["norm_g"], L=LANES, t_valid=T)
        h_m = h_m[:, :T]

    if decode is None:
        kc, vc = cmp_tokens(zc, lw["cmp_w"], lw["cmp_b"])
        h_n = nsa_prompt(zb, zd, zcb, kc, vc, tq=min(128, T), CH=min(512, T))
    else:
        h_n = nsa_decode(zb, zd, zcb, decode["nsa"], decode["win"], decode["pt"], lw["cmp_w"], lw["cmp_b"], l)

    h_x = mem_attn(zb, mem_kv, mem_blk[0], mem_blk[1], tq=_row_tile(T, 512))

    wg, bg = lw["in"]["mg"]
    flat = lambda a: a.reshape(n, -1)
    x1 = merge(x2d, flat(h_m), flat(h_n), flat(h_x), wg, bg, lw["w_branch"], lw["w_out"], lw["ln1_g"], lw["ln1_b"], tm)

    if l % 2 == 0:
        x2 = ffn(x1, lw["ffn_up"], lw["ffn_down"], lw["ln2_g"], lw["ln2_b"], _row_tile(n, 1024), 256)
    else:
        cw = router(x1, lw["w_router"], lw["b_router"], _row_tile(n, 1024))
        x2 = moe(x1, cw, lw["ffn_up"], lw["ffn_down"], lw["ln2_g"], lw["ln2_b"], 512)
    return x2.reshape(B, T, D_MODEL), zc, (ct, nt, mt)


def kernel(x_prompt, x_sample, mem_prompt, cache_nsa_kv, cache_win_kv, state_mlstm_C, state_mlstm_n,
           state_mlstm_m, cache_mem_kv, page_table, w_in, b_in, mlstm_norm_g, cmp_w, cmp_b, w_mem_kv,
           w_branch, w_out, ln1_g, ln1_b, ln2_g, ln2_b, ffn_w_up, ffn_w_down, moe_w_router, moe_b_router,
           moe_w_up, moe_w_down):
    B, T, _ = x_prompt.shape
    DB, TS, _ = x_sample.shape
    xp, xs = x_prompt, x_sample
    nsa_p, nsa_s, win_p, win_s = [], [], [], []
    Cp, np_, mp, Cs, ns, ms, memkv_p = [], [], [], [], [], [], []
    mem_s = cache_mem_kv.reshape(DB, MEM_LEN, DEPTH * 2 * X_WIDTH)
    row1 = lambda a: a.reshape(1, -1).astype(F32)
    for l in range(DEPTH):
        wbr = (w_branch[l, 0].astype(BF16), _pad_branch_rows(w_branch[l, 1]).astype(BF16), w_branch[l, 2].astype(BF16))
        lw = dict(norm_g=mlstm_norm_g[l], cmp_w=cmp_w[l], cmp_b=cmp_b[l], w_branch=wbr,
                  w_out=w_out[l].astype(BF16), ln1_g=row1(ln1_g[l]), ln1_b=row1(ln1_b[l]),
                  ln2_g=row1(ln2_g[l]), ln2_b=row1(ln2_b[l]))
        lw["in"] = _split_in_proj(w_in[l], b_in[l])
        if l % 2 == 0:
            lw["ffn_up"] = ffn_w_up[l // 2].astype(BF16)
            lw["ffn_down"] = ffn_w_down[l // 2].astype(BF16)
        else:
            lw["ffn_up"] = moe_w_up[l // 2].astype(BF16)
            lw["ffn_down"] = moe_w_down[l // 2].astype(BF16)
            lw["w_router"] = moe_w_router[l // 2]
            lw["b_router"] = moe_b_router[l // 2]

        mkv = proj(mem_prompt.reshape(B * MEM_LEN, D_MODEL), w_mem_kv[l].astype(BF16), _row_tile(B * MEM_LEN, 512))
        mkv = mkv.reshape(B, MEM_LEN, 2 * X_WIDTH)
        st0 = (jnp.zeros((B, M_HEADS, M_DH, M_DH), F32), jnp.zeros((B, M_HEADS, M_DH), F32), jnp.zeros((B, M_HEADS), F32))
        xp, zc, st = _layer(xp, lw, l, mem_kv=mkv, mem_blk=(0, 1), mstate=st0, decode=None)
        kvn = zc.reshape(B, T, 6, NSA_GROUPS, NSA_DH)
        nsa_p.append(kvn[:, :, :4])
        win_p.append(kvn[:, T - min(WINDOW, T):, 4:])
        Cp.append(st[0]); np_.append(st[1]); mp.append(st[2])
        memkv_p.append(mkv.reshape(B, MEM_LEN, 2, X_HEADS, X_DH))

        sts = (state_mlstm_C[:, l], state_mlstm_n[:, l], state_mlstm_m[:, l])
        dec = dict(nsa=cache_nsa_kv, win=cache_win_kv, pt=page_table)
        xs, zc, st = _layer(xs, lw, l, mem_kv=mem_s, mem_blk=(2 * l, 2 * l + 1), mstate=sts, decode=dec)
        kvn = zc.reshape(DB, TS, 6, NSA_GROUPS, NSA_DH)
        nsa_s.append(kvn[:, :, :4])
        win_s.append(jnp.concatenate([cache_win_kv[:, :, l].astype(F32), kvn[:, :, 4:]], axis=1)[:, TS:])
        Cs.append(st[0]); ns.append(st[1]); ms.append(st[2])
    return (xp, xs,
            jnp.stack(nsa_p, axis=2), jnp.stack(nsa_s, axis=2),
            jnp.stack(win_p, axis=2), jnp.stack(win_s, axis=2),
            jnp.stack(Cp, axis=1), jnp.stack(np_, axis=1), jnp.stack(mp, axis=1),
            jnp.stack(Cs, axis=1), jnp.stack(ns, axis=1), jnp.stack(ms, axis=1),
            jnp.stack(memkv_p, axis=2))
```

```python
import functools

import numpy as np
import jax
import jax.numpy as jnp
from jax import lax
from jax.experimental import pallas as pl
from jax.experimental.pallas import tpu as pltpu

D_MODEL = 1024
DEPTH = 2
BRANCH_WIDTH = 512
N_BRANCH = 3
M_HEADS = 4
M_DH = BRANCH_WIDTH // M_HEADS
M_WIDTH = M_HEADS * M_DH
NSA_HEADS = 8
NSA_DH = BRANCH_WIDTH // NSA_HEADS
NSA_WIDTH = NSA_HEADS * NSA_DH
NSA_GROUPS = 2
NSA_HPG = NSA_HEADS // NSA_GROUPS
NSA_KV = NSA_GROUPS * NSA_DH
CMP_LEN = 32
CMP_STRIDE = 16
SEL_LEN = 64
SEL_TOPN = 16
WINDOW = 512
MEM_LEN = 256
X_HEADS = 4
X_DH = BRANCH_WIDTH // X_HEADS
X_WIDTH = X_HEADS * X_DH
D_FF = 2816
N_EXPERTS = 8
TOP_K = 2
D_FF_EXPERT = 3584
ALPHA = (2.0 * DEPTH) ** 0.25
LN_EPS = 1e-5
IN_SPLITS = (M_WIDTH, M_WIDTH, M_WIDTH, M_HEADS, M_HEADS, M_WIDTH,
             NSA_WIDTH, 6 * NSA_KV, 3 * NSA_HEADS, X_WIDTH, N_BRANCH * D_MODEL)
LOG2E = 1.4426950408889634
NSA_SLOPES = tuple(LOG2E * 2.0 ** (-8.0 * (h + 1) / NSA_HEADS) for h in range(NSA_HEADS))
MASK_BIAS = -(2.0 ** 30)
SEL_FEAT = 64
SLOPE_PARTS = 3

LANES = 128
SUBLANES = 8
VMEM_LIMIT = 56 * 1024 * 1024
PAGES_PER_STEP = 16
GATE_COLS = LANES
NG_OFF = 2 * M_HEADS
NSA_QPAD = NSA_HEADS * LANES
MOE_BLOCK = 1024
MOE_CAP = 320
NO_SLOT = -2
ROUTE_ROWS = 16

F32 = jnp.float32
BF16 = jnp.bfloat16
NEG_INF = float("-inf")
HIGHEST = lax.Precision.HIGHEST


def _dot(a, b, precision=None):
    return jnp.dot(a, b, preferred_element_type=F32, precision=precision)


def _dot_nt(a, b, precision=None):
    return lax.dot_general(a, b, (((1,), (1,)), ((), ())), preferred_element_type=F32, precision=precision)


def _dot_tn(a, b):
    return lax.dot_general(a, b, (((0,), (0,)), ((), ())), preferred_element_type=F32)


def _params(sem):
    return pltpu.CompilerParams(dimension_semantics=sem, vmem_limit_bytes=VMEM_LIMIT)


def _masked_softmax2(s, mask):
    s = jnp.where(mask, s, NEG_INF)
    mx = jnp.max(s, axis=-1, keepdims=True)
    mx = jnp.where(mx > NEG_INF, mx, 0.0)
    p = jnp.where(mask, jnp.exp2(s - mx), 0.0)
    return p / jnp.maximum(jnp.sum(p, axis=-1, keepdims=True), 1e-30)


def _layer_norm(xf, g, b):
    mu = jnp.mean(xf, axis=-1, keepdims=True)
    var = jnp.mean(jnp.square(xf - mu), axis=-1, keepdims=True)
    return (xf - mu) * lax.rsqrt(var + LN_EPS) * g + b


def _row_tile(n, pref):
    return pref if n % pref == 0 else n


def _in_proj_kernel(x_ref, wa_ref, wb_ref, wc_ref, wd_ref, ba_ref, bb_ref, bc_ref, bd_ref,
                    oa_ref, ob_ref, oc_ref, ocb_ref, od_ref):
    x = x_ref[...].astype(BF16)
    oa_ref[...] = (_dot(x, wa_ref[...]) + ba_ref[...]).astype(BF16)
    ob_ref[...] = (_dot(x, wb_ref[...]) + bb_ref[...]).astype(BF16)
    c = _dot(x, wc_ref[...]) + bc_ref[...]
    oc_ref[...] = c
    ocb_ref[...] = c.astype(BF16)
    od_ref[...] = _dot(x, wd_ref[...]) + bd_ref[...]


def _group_select():
    return jax.nn.one_hot(np.arange(NSA_HEADS) // NSA_HPG, NSA_GROUPS, dtype=F32)


def _pad_branch_rows(w_nsa):
    return jnp.einsum("hem,hg->hgem", w_nsa.reshape(NSA_HEADS, NSA_DH, -1), _group_select()).reshape(NSA_QPAD, -1)


def _split_in_proj(w_in_l, b_in_l):
    offs = np.cumsum((0,) + IN_SPLITS)
    w = [w_in_l[:, offs[i]:offs[i + 1]] for i in range(len(IN_SPLITS))]
    b = [b_in_l[offs[i]:offs[i + 1]] for i in range(len(IN_SPLITS))]
    mq, mk, mv, mi, mf, mo, nq, nkv, ng, xq, mg = range(11)
    pad = GATE_COLS - 2 * M_HEADS - 3 * NSA_HEADS

    def cat(ids, zpad=0):
        ww = jnp.concatenate([w[i] for i in ids], axis=1)
        bb = jnp.concatenate([b[i] for i in ids])
        if zpad:
            ww = jnp.pad(ww, ((0, 0), (0, zpad)))
            bb = jnp.pad(bb, (0, zpad))
        return ww.astype(BF16), bb.reshape(1, -1).astype(F32)

    scale = NSA_DH ** -0.5 * LOG2E
    w[nq] = jnp.einsum("dhe,hg->dhge", w[nq].reshape(D_MODEL, NSA_HEADS, NSA_DH) * scale, _group_select()).reshape(D_MODEL, NSA_QPAD)
    b[nq] = jnp.einsum("he,hg->hge", b[nq].reshape(NSA_HEADS, NSA_DH) * scale, _group_select()).reshape(NSA_QPAD)
    return dict(a=cat([mq, mk, mv, mo]), b=cat([nq, xq]), c=cat([nkv]), d=cat([mi, mf, ng], pad), mg=cat([mg]))


def in_proj(x2d, wp, tm):
    n = x2d.shape[0]
    (wa, ba), (wb, bb), (wc, bc), (wd, bd) = wp["a"], wp["b"], wp["c"], wp["d"]
    full = lambda arr: pl.BlockSpec(arr.shape, lambda i: (0, 0))
    row = lambda w: pl.BlockSpec((tm, w), lambda i: (i, 0))
    return pl.pallas_call(
        _in_proj_kernel,
        grid=(n // tm,),
        in_specs=[row(D_MODEL), full(wa), full(wb), full(wc), full(wd), full(ba), full(bb), full(bc), full(bd)],
        out_specs=[row(wa.shape[1]), row(wb.shape[1]), row(wc.shape[1]), row(wc.shape[1]), row(wd.shape[1])],
        out_shape=[jax.ShapeDtypeStruct((n, wa.shape[1]), BF16), jax.ShapeDtypeStruct((n, wb.shape[1]), BF16),
                   jax.ShapeDtypeStruct((n, wc.shape[1]), F32), jax.ShapeDtypeStruct((n, wc.shape[1]), BF16),
                   jax.ShapeDtypeStruct((n, wd.shape[1]), F32)],
        compiler_params=_params(("parallel",)),
        name="in_proj",
    )(x2d, wa, wb, wc, wd, ba, bb, bc, bd)


def _proj_kernel(x_ref, w_ref, o_ref):
    o_ref[...] = _dot(x_ref[...].astype(BF16), w_ref[...])


def proj(x2d, w_bf16, tm):
    n, k = x2d.shape
    m = w_bf16.shape[1]
    return pl.pallas_call(
        _proj_kernel,
        grid=(n // tm,),
        in_specs=[pl.BlockSpec((tm, k), lambda i: (i, 0)), pl.BlockSpec((k, m), lambda i: (0, 0))],
        out_specs=pl.BlockSpec((tm, m), lambda i: (i, 0)),
        out_shape=jax.ShapeDtypeStruct((n, m), F32),
        compiler_params=_params(("parallel",)),
        name="mem_proj",
    )(x2d, w_bf16)


def _log_sigmoid(x):
    return jnp.minimum(x, 0.0) - jnp.log1p(jnp.exp(-jnp.abs(x)))


def _mlstm_kernel(q_ref, k_ref, v_ref, og_ref, gc_ref, gr_ref, c0_ref, n0_ref, m0_ref, ng_ref,
                  h_ref, ct_ref, nt_ref, mt_ref, c_s, n_s, m_s, *, L, t_valid):
    ci = pl.program_id(1)

    @pl.when(ci == 0)
    def _():
        c_s[...] = c0_ref[0]
        n_s[...] = n0_ref[0]
        m_s[...] = m0_ref[0]

    row = lax.broadcasted_iota(jnp.int32, (L, L), 0)
    col = lax.broadcasted_iota(jnp.int32, (L, L), 1)
    causal = row >= col
    tri = causal.astype(F32)
    tri_t = (row <= col).astype(F32)
    gc = gc_ref[0]
    gr = gr_ref[0]
    rvalid = lax.broadcasted_iota(jnp.int32, (L, GATE_COLS), 0) < t_valid
    cvalid = lax.broadcasted_iota(jnp.int32, (2 * M_HEADS, L), 1) < t_valid
    lf_c = jnp.where(rvalid, _log_sigmoid(gc), 0.0)
    lf_r = jnp.where(cvalid, _log_sigmoid(gr), 0.0)
    ig_c = jnp.where(rvalid, gc, NEG_INF)
    ig_r = jnp.where(cvalid, gr, NEG_INF)
    b_c = _dot(tri, lf_c, HIGHEST)
    b_r = _dot(lf_r, tri_t, HIGHEST)
    scale = M_DH ** -0.5

    for h in range(M_HEADS):
        sl = slice(h * M_DH, (h + 1) * M_DH)
        q = q_ref[0, :, sl]
        k = k_ref[0, :, sl]
        v = v_ref[0, :, sl]
        bc = b_c[:, M_HEADS + h:M_HEADS + h + 1]
        br = b_r[M_HEADS + h:M_HEADS + h + 1, :]
        igc = ig_c[:, h:h + 1]
        igr = ig_r[h:h + 1, :]
        m_prev = m_s[h, 0:1, 0:1]
        c_prev = c_s[h]
        n_prev = n_s[h:h + 1, :]

        dmat = jnp.where(causal, bc - br + igr, NEG_INF)
        inter = bc + m_prev
        m_t = jnp.maximum(inter, jnp.max(dmat, axis=-1, keepdims=True))
        s = _dot_nt(q, k) * scale * jnp.exp(dmat - m_t)
        sc_in = jnp.exp(inter - m_t)
        qf = q.astype(F32)
        num = sc_in * _dot_nt(q, c_prev.astype(BF16)) + _dot(s.astype(BF16), v)
        den = sc_in * jnp.sum(qf * n_prev, axis=-1, keepdims=True) + jnp.sum(s, axis=-1, keepdims=True)
        hh = num / jnp.maximum(jnp.abs(den), jnp.exp(-m_t))

        b_last = bc[L - 1:L, :]
        dec_c = b_last - bc + igc
        dec_r = b_last - br + igr
        m_new = jnp.maximum(b_last + m_prev, jnp.max(dec_r, axis=-1, keepdims=True))
        ws_c = jnp.exp(dec_c - m_new) * scale
        sc = jnp.exp(b_last + m_prev - m_new)
        vf = v.astype(F32)
        kf = k.astype(F32)
        c_new = sc * c_prev + _dot_tn((vf * ws_c).astype(BF16), k)
        n_new = sc * n_prev + jnp.sum(kf * ws_c, axis=0, keepdims=True)
        c_s[h] = c_new
        n_s[h:h + 1, :] = n_new
        m_s[h] = jnp.broadcast_to(m_new, m_s.shape[1:])

        og = og_ref[0, :, sl].astype(F32)
        hh = hh * jax.nn.sigmoid(og)
        mu = jnp.mean(hh, axis=-1, keepdims=True)
        var = jnp.mean(jnp.square(hh - mu), axis=-1, keepdims=True)
        hn = (hh - mu) * lax.rsqrt(var + LN_EPS) * ng_ref[:, sl]
        h_ref[0, :, sl] = hn.astype(h_ref.dtype)

    ct_ref[0] = c_s[...]
    nt_ref[0] = n_s[...]
    mt_ref[0] = m_s[...]


def mlstm(za, zd, c0, n0, m0, norm_g, *, L, t_valid):
    B, T, _ = za.shape
    nc = T // L
    gr = jnp.swapaxes(zd[:, :, :2 * M_HEADS], 1, 2)
    m0b = jnp.broadcast_to(m0.astype(F32)[:, :, None, None], (B, M_HEADS, SUBLANES, LANES))
    colspec = lambda j: pl.BlockSpec((1, L, M_WIDTH), lambda b, c, j=j: (b, c, j))
    st = lambda shp: pl.BlockSpec((1,) + shp, lambda b, c: (b,) + (0,) * len(shp))
    kern = functools.partial(_mlstm_kernel, L=L, t_valid=t_valid)
    h, ct, nt, mt = pl.pallas_call(
        kern,
        grid=(B, nc),
        in_specs=[colspec(0), colspec(1), colspec(2), colspec(3),
                  pl.BlockSpec((1, L, GATE_COLS), lambda b, c: (b, c, 0)),
                  pl.BlockSpec((1, 2 * M_HEADS, L), lambda b, c: (b, 0, c)),
                  st((M_HEADS, M_DH, M_DH)), st((M_HEADS, M_DH)), st((M_HEADS, SUBLANES, LANES)),
                  pl.BlockSpec((1, M_WIDTH), lambda b, c: (0, 0))],
        out_specs=[pl.BlockSpec((1, L, M_WIDTH), lambda b, c: (b, c, 0)),
                   st((M_HEADS, M_DH, M_DH)), st((M_HEADS, M_DH)), st((M_HEADS, SUBLANES, LANES))],
        out_shape=[jax.ShapeDtypeStruct((B, T, M_WIDTH), BF16),
                   jax.ShapeDtypeStruct((B, M_HEADS, M_DH, M_DH), F32),
                   jax.ShapeDtypeStruct((B, M_HEADS, M_DH), F32),
                   jax.ShapeDtypeStruct((B, M_HEADS, SUBLANES, LANES), F32)],
        scratch_shapes=[pltpu.VMEM((M_HEADS, M_DH, M_DH), F32), pltpu.VMEM((M_HEADS, M_DH), F32),
                        pltpu.VMEM((M_HEADS, SUBLANES, LANES), F32)],
        compiler_params=_params(("parallel", "arbitrary")),
        name="mlstm",
    )(za, za, za, za, zd, gr, c0.astype(F32), n0.astype(F32), m0b, norm_g.reshape(1, M_WIDTH).astype(F32))
    return h, ct, nt, mt[:, :, 0, 0]


def _mem_attn_kernel(q_ref, k_ref, v_ref, o_ref):
    scale = X_DH ** -0.5
    for h in range(X_HEADS):
        sl = slice(h * X_DH, (h + 1) * X_DH)
        s = _dot_nt(q_ref[0, :, sl], k_ref[0, :, sl].astype(BF16)) * scale
        mx = jnp.max(s, axis=-1, keepdims=True)
        p = jnp.exp(s - mx)
        p = p / jnp.sum(p, axis=-1, keepdims=True)
        o_ref[0, :, sl] = _dot(p.astype(BF16), v_ref[0, :, sl].astype(BF16)).astype(o_ref.dtype)


def mem_attn(zb, kv, k_blk, v_blk, tq):
    B, T, _ = zb.shape
    return pl.pallas_call(
        _mem_attn_kernel,
        grid=(B, T // tq),
        in_specs=[pl.BlockSpec((1, tq, X_WIDTH), lambda b, i: (b, i, NSA_QPAD // X_WIDTH)),
                  pl.BlockSpec((1, MEM_LEN, X_WIDTH), lambda b, i: (b, 0, k_blk)),
                  pl.BlockSpec((1, MEM_LEN, X_WIDTH), lambda b, i: (b, 0, v_blk))],
        out_specs=pl.BlockSpec((1, tq, X_WIDTH), lambda b, i: (b, i, 0)),
        out_shape=jax.ShapeDtypeStruct((B, T, X_WIDTH), BF16),
        compiler_params=_params(("parallel", "parallel")),
        name="mem_attn",
    )(zb, kv, kv)


def _head_rows(q_ref, nq):
    parts = [q_ref[0, :, hd * LANES:(hd + 1) * LANES] for hd in range(NSA_HEADS)]
    if nq % 16:
        return jnp.concatenate([p.astype(F32) for p in parts], axis=0).astype(BF16)
    return jnp.concatenate(parts, axis=0)


def _store_heads(o_ref, gate, o_c, o_s, o_w, nq):
    lane = lax.broadcasted_iota(jnp.int32, (nq, LANES), 1)
    for hd in range(NSA_HEADS):
        r = slice(hd * nq, (hd + 1) * nq)
        gcol = lambda br: gate[:, NG_OFF + br * NSA_HEADS + hd:NG_OFF + br * NSA_HEADS + hd + 1]
        val = gcol(0) * o_c[r] + gcol(1) * o_s[r] + gcol(2) * o_w[r]
        keep = (lane >= NSA_DH) if hd // NSA_HPG == 1 else (lane < NSA_DH)
        o_ref[0, :, hd * LANES:(hd + 1) * LANES] = jnp.where(keep, val, 0.0).astype(o_ref.dtype)


def _overlap(tok, blk):
    c_start = tok * CMP_STRIDE
    s_start = blk * SEL_LEN
    return ((c_start < s_start + SEL_LEN) & (c_start + CMP_LEN > s_start)).astype(F32)


def _select_blocks(imp, tpos, n_sel, n_top):
    nq, W = imp.shape
    blk = lax.broadcasted_iota(jnp.int32, (nq, W), 1)
    cur = tpos // SEL_LEN
    forced = (blk == 0) | (blk == cur) | (blk == cur - 1)
    v = jnp.where(forced, jnp.inf, jnp.where(blk <= cur, imp, NEG_INF))
    v = jnp.where(blk < n_sel, v, NEG_INF)
    ahead = jnp.zeros((nq, W), F32)
    for j in range(n_sel):
        vj = v[:, j:j + 1]
        ahead = ahead + jnp.where(vj > v, 1.0, jnp.where(vj == v, jnp.where(blk > j, 1.0, 0.0), 0.0))
    return jnp.where((ahead < n_top) & (blk < n_sel), 1.0, 0.0)


def _select_blocks_t(imp_t, tpos_row, n_sel, n_top):
    n_blk, nq = imp_t.shape
    blk = lax.broadcasted_iota(jnp.int32, (n_blk, nq), 0)
    cur = tpos_row // SEL_LEN
    forced = (blk == 0) | (blk == cur) | (blk == cur - 1)
    v = jnp.where(forced, jnp.inf, jnp.where(blk <= cur, imp_t, NEG_INF))
    v = jnp.where(blk < n_sel, v, NEG_INF)
    n_rg = n_blk // SUBLANES
    vg = [v[rg * SUBLANES:(rg + 1) * SUBLANES] for rg in range(n_rg)]
    bg = [rg * SUBLANES + lax.broadcasted_iota(jnp.int32, (SUBLANES, nq), 0) for rg in range(n_rg)]
    ahead = [jnp.zeros((SUBLANES, nq), F32) for _ in range(n_rg)]
    for j in range(n_sel):
        vj = v[j:j + 1, :]
        for rg in range(n_rg):
            if rg * SUBLANES > j:
                inc = jnp.where(vj >= vg[rg], 1.0, 0.0)
            elif (rg + 1) * SUBLANES - 1 <= j:
                inc = jnp.where(vj > vg[rg], 1.0, 0.0)
            else:
                inc = jnp.where(vj > vg[rg], 1.0, jnp.where(vj == vg[rg], jnp.where(bg[rg] > j, 1.0, 0.0), 0.0))
            ahead[rg] = ahead[rg] + inc
    ahead = jnp.concatenate(ahead, axis=0)
    return jnp.where((ahead < n_top) & (blk < n_sel), 1.0, 0.0)


def _online_update(s, m_old, l_old):
    m_new = jnp.maximum(m_old, jnp.max(s, axis=-1, keepdims=True))
    alpha = jnp.exp2(m_old - m_new)
    p = jnp.exp2(s - m_new)
    return p, m_new, alpha, alpha * l_old + jnp.sum(p, axis=-1, keepdims=True)


def _cmp_tokens_kernel(xk_ref, xv_ref, bd_ref, b_ref, kc_ref, vc_ref, *, nsub):
    for c, (x_ref, o_ref) in enumerate(((xk_ref, kc_ref), (xv_ref, vc_ref))):
        a0 = jnp.zeros((nsub, LANES), F32)
        a1 = jnp.zeros((nsub, LANES), F32)
        for j in range(CMP_STRIDE):
            xj = x_ref[0, :, j, :].astype(BF16)
            a0 = a0 + _dot(xj, bd_ref[c, 0, j])
            a1 = a1 + _dot(xj, bd_ref[c, 1, j])
        tok = a0 + pltpu.roll(a1, nsub - 1, axis=0) + b_ref[c]
        o_ref[0] = tok.astype(o_ref.dtype)


def _cmp_blockdiag(cmp_w_l):
    R = CMP_LEN // CMP_STRIDE
    w = cmp_w_l.astype(F32).reshape(2, R, CMP_STRIDE, NSA_DH, NSA_DH)
    eye = jnp.eye(NSA_GROUPS, dtype=F32)
    return jnp.einsum("ab,crjde->crjadbe", eye, w).reshape(2, R, CMP_STRIDE, LANES, LANES).astype(BF16)


def _cmp_bias(cmp_b_l):
    return jnp.tile(cmp_b_l.astype(F32), (1, NSA_GROUPS)).reshape(2, 1, LANES)


def cmp_tokens(zc, cmp_w_l, cmp_b_l):
    B, T, W = zc.shape
    nsub = T // CMP_STRIDE
    x4 = zc.reshape(B, nsub, CMP_STRIDE, W)
    bd = _cmp_blockdiag(cmp_w_l)
    bias = _cmp_bias(cmp_b_l)
    spec = lambda j: pl.BlockSpec((1, nsub, CMP_STRIDE, LANES), lambda b, j=j: (b, 0, 0, j))
    return pl.pallas_call(
        functools.partial(_cmp_tokens_kernel, nsub=nsub),
        grid=(B,),
        in_specs=[spec(0), spec(1), pl.BlockSpec(bd.shape, lambda b: (0,) * 5), pl.BlockSpec(bias.shape, lambda b: (0, 0, 0))],
        out_specs=[pl.BlockSpec((1, nsub, LANES), lambda b: (b, 0, 0))] * 2,
        out_shape=[jax.ShapeDtypeStruct((B, nsub, LANES), BF16)] * 2,
        compiler_params=_params(("parallel",)),
        name="nsa_cmp_tokens",
    )(x4, x4, bd, bias)


def _nsa_prompt_kernel(q_ref, g_ref, kc_ref, vc_ref, ks_ref, vst_ref, kw_ref, vw_ref, qf_ref, kf_ref, o_ref,
                       *, tq, T, CH):
    qi = pl.program_id(1)
    start = qi * tq
    ref_pos = start + tq
    n_cmp_rows = kc_ref.shape[1]
    n_sel = T // SEL_LEN
    n_top = min(SEL_TOPN, n_sel)
    n_blk = -(-n_sel // SUBLANES) * SUBLANES
    qp = _head_rows(q_ref, tq)
    tpos = start + lax.broadcasted_iota(jnp.int32, (tq, 1), 0)
    tpos_row = start + lax.broadcasted_iota(jnp.int32, (1, tq), 1)
    rows = lambda a, hd: a[hd * tq:(hd + 1) * tq]

    sc_all = _dot_nt(qp, kc_ref[0])
    c_end = lax.broadcasted_iota(jnp.int32, (1, n_cmp_rows), 1) * CMP_STRIDE + CMP_LEN - 1
    mask_c = tpos >= c_end
    rel_c = (c_end - ref_pos).astype(F32)
    pcs = [_masked_softmax2(rows(sc_all, hd) + NSA_SLOPES[hd] * rel_c, mask_c) for hd in range(NSA_HEADS)]
    o_c = _dot(jnp.concatenate(pcs, axis=0).astype(BF16), vc_ref[0])
    ov_t = _overlap(lax.broadcasted_iota(jnp.int32, (n_blk, n_cmp_rows), 1),
                    lax.broadcasted_iota(jnp.int32, (n_blk, n_cmp_rows), 0))
    lane = lax.broadcasted_iota(jnp.int32, (tq, LANES), 1)
    feats = []
    for g in range(NSA_GROUPS):
        pg = pcs[g * NSA_HPG]
        for p in range(1, NSA_HPG):
            pg = pg + pcs[g * NSA_HPG + p]
        sel_t = _select_blocks_t(_dot_nt(ov_t, pg, HIGHEST), tpos_row, n_sel, n_top)
        sel = jnp.concatenate([sel_t, jnp.zeros((LANES - n_blk, tq), F32)], axis=0).T
        unsel = jnp.where(lane < SEL_FEAT, (1.0 - sel) * MASK_BIAS, 0.0)
        feats += [(unsel + qf_ref[g * NSA_HPG + p:g * NSA_HPG + p + 1, :]).astype(BF16) for p in range(NSA_HPG)]
    qaug = jnp.concatenate([qp, jnp.concatenate(feats, axis=0)], axis=1)

    nq_all = NSA_HEADS * tq
    tpos_all = jnp.concatenate([tpos_row] * NSA_HEADS, axis=1)

    def scores(c):
        base = pl.multiple_of(c * CH, CH)
        kaug = jnp.concatenate([ks_ref[0, pl.ds(base, CH), :], kf_ref[pl.ds(base, CH), :]], axis=1)
        return _dot_nt(kaug, qaug)

    def absorb(c, s_t, m_old, l_old, acc, diag):
        if diag:
            kpos = c * CH + lax.broadcasted_iota(jnp.int32, (CH, 1), 0)
            s_t = jnp.where(kpos <= tpos_all, s_t, NEG_INF)
        m_new = jnp.maximum(m_old, jnp.max(s_t, axis=0, keepdims=True))
        alpha = jnp.exp2(m_old - m_new)
        p_t = jnp.exp2(s_t - m_new)
        l_new = alpha * l_old + jnp.sum(p_t, axis=0, keepdims=True)
        return m_new, l_new, alpha * acc + _dot(vst_ref[0, c], p_t.astype(BF16))

    def full_chunk(c, carry):
        s_t, m_old, l_old, acc = carry
        s_next = scores(c + 1)
        return (s_next,) + absorb(c, s_t, m_old, l_old, acc, False)

    n_full = start // CH
    init = (scores(0), jnp.full((1, nq_all), NEG_INF, F32), jnp.zeros((1, nq_all), F32), jnp.zeros((NSA_KV, nq_all), F32))
    s_t, m_old, l_old, acc = lax.fori_loop(0, n_full, full_chunk, init)
    _, l_fin, acc = absorb(n_full, s_t, m_old, l_old, acc, True)
    o_st = acc / jnp.maximum(l_fin, 1e-30)
    o_s = jnp.concatenate([o_st[:, hd * tq:(hd + 1) * tq].T for hd in range(NSA_HEADS)], axis=0)

    wk = WINDOW + tq
    base_w = pl.multiple_of(start, tq)
    sw_all = _dot_nt(qp, kw_ref[0, pl.ds(base_w, wk), :])
    jw = lax.broadcasted_iota(jnp.int32, (1, wk), 1)
    iw = lax.broadcasted_iota(jnp.int32, (tq, 1), 0)
    dw = iw - jw + WINDOW
    mask_w = (dw >= 0) & (dw < WINDOW) & (start - WINDOW + jw >= 0)
    rel_w = (jw - wk).astype(F32)
    pws = [_masked_softmax2(rows(sw_all, hd) + NSA_SLOPES[hd] * rel_w, mask_w) for hd in range(NSA_HEADS)]
    o_w = _dot(jnp.concatenate(pws, axis=0).astype(BF16), vw_ref[0, pl.ds(base_w, wk), :])

    _store_heads(o_ref, jax.nn.sigmoid(g_ref[0]), o_c, o_s, o_w, tq)


def _bf16_parts(x, n):
    parts = []
    for _ in range(n):
        p = float(np.asarray(x, np.float32).astype(jnp.bfloat16).astype(np.float32))
        parts.append(p)
        x = x - p
    return parts


def _selected_features(T):
    qf = np.zeros((NSA_HEADS, LANES), np.float32)
    kf = np.zeros((T, LANES), np.float32)
    pos = np.arange(T)
    kf[pos, pos // SEL_LEN] = 1.0
    for hd in range(NSA_HEADS):
        for p, s_p in enumerate(_bf16_parts(NSA_SLOPES[hd], SLOPE_PARTS)):
            qf[hd, SEL_FEAT + 2 * p] = s_p * SEL_LEN
            qf[hd, SEL_FEAT + 2 * p + 1] = s_p
    for p in range(SLOPE_PARTS):
        kf[:, SEL_FEAT + 2 * p] = pos // SEL_LEN
        kf[:, SEL_FEAT + 2 * p + 1] = pos % SEL_LEN
    return jnp.asarray(qf), jnp.asarray(kf, BF16)


def nsa_prompt(zb, zd, zcb, kc, vc, tq, CH):
    B, T, _ = zb.shape
    assert T % CH == 0 and CH % tq == 0 and T // SEL_LEN <= SEL_FEAT
    qf, kf = _selected_features(T)
    vs_t = jnp.swapaxes(zcb[:, :, 3 * NSA_KV:4 * NSA_KV].reshape(B, T // CH, CH, NSA_KV), 2, 3)
    kw = jnp.pad(zcb[:, :, 4 * NSA_KV:5 * NSA_KV], ((0, 0), (WINDOW, 0), (0, 0)))
    vw = jnp.pad(zcb[:, :, 5 * NSA_KV:6 * NSA_KV], ((0, 0), (WINDOW, 0), (0, 0)))
    nsub = kc.shape[1]
    per_b = lambda rows_, j: pl.BlockSpec((1, rows_, LANES), lambda b, i, j=j: (b, 0, j))
    kern = functools.partial(_nsa_prompt_kernel, tq=tq, T=T, CH=CH)
    return pl.pallas_call(
        kern,
        grid=(B, T // tq),
        in_specs=[pl.BlockSpec((1, tq, NSA_QPAD), lambda b, i: (b, i, 0)),
                  pl.BlockSpec((1, tq, GATE_COLS), lambda b, i: (b, i, 0)),
                  per_b(nsub, 0), per_b(nsub, 0), per_b(T, 2),
                  pl.BlockSpec((1, T // CH, NSA_KV, CH), lambda b, i: (b, 0, 0, 0)),
                  per_b(T + WINDOW, 0), per_b(T + WINDOW, 0),
                  pl.BlockSpec(qf.shape, lambda b, i: (0, 0)), pl.BlockSpec(kf.shape, lambda b, i: (0, 0))],
        out_specs=pl.BlockSpec((1, tq, NSA_QPAD), lambda b, i: (b, i, 0)),
        out_shape=jax.ShapeDtypeStruct((B, T, NSA_QPAD), BF16),
        compiler_params=_params(("parallel", "arbitrary")),
        name="nsa_prompt",
    )(zb, zd, kc, vc, zcb, vs_t, kw, vw, qf, kf)


def _page_specs(shape, slot_blk, layer, n_pages):
    def mk(i):
        def imap(b, s, pt):
            return (pt[b * n_pages + s * PAGES_PER_STEP + i], layer, slot_blk, 0, 0)
        return pl.BlockSpec(shape, imap)
    return [mk(i) for i in range(PAGES_PER_STEP)]


def _nsa_dec_cmp_kernel(pt_ref, *refs, T, past_len, n_rows):
    pages = refs[:PAGES_PER_STEP]
    perm_ref, bd_ref, b_ref, q_ref, oc_ref, sel_ref, tok_s, pend_s = refs[PAGES_PER_STEP:]
    s = pl.program_id(1)
    n_steps = pl.num_programs(1)
    page_rows = pages[0].shape[-1]
    sub = page_rows // CMP_STRIDE
    R = PAGES_PER_STEP * sub

    @pl.when(s == 0)
    def _():
        pend_s[...] = jnp.zeros(pend_s.shape, F32)

    rid = lax.broadcasted_iota(jnp.int32, (R, LANES), 0)
    for c in range(2):
        xs = [_dot_nt(perm_ref[...], pg[0, 0, c].astype(BF16)) for pg in pages]
        a0 = jnp.zeros((R, LANES), F32)
        a1 = jnp.zeros((R, LANES), F32)
        for j in range(CMP_STRIDE):
            xj = jnp.concatenate([x[j * sub:(j + 1) * sub] for x in xs], axis=0).astype(BF16)
            a0 = a0 + _dot(xj, bd_ref[c, 0, j])
            a1 = a1 + _dot(xj, bd_ref[c, 1, j])
        a0 = a0 + b_ref[c]
        tok = jnp.where(rid == 0, pend_s[c], pltpu.roll(a0, 1, axis=0)) + a1
        tok_s[c, pl.ds(pl.multiple_of(s * R, R), R), :] = tok
        pend_s[c] = a0[R - 1:R, :]

    @pl.when(s == n_steps - 1)
    def _():
        qp = _head_rows(q_ref, T)
        kc = tok_s[0].astype(BF16)
        vc = tok_s[1].astype(BF16)
        sc_all = _dot_nt(qp, kc)
        tpos = past_len + lax.broadcasted_iota(jnp.int32, (T, 1), 0)
        r = lax.broadcasted_iota(jnp.int32, (T, n_rows), 1)
        dc = tpos - ((r - 1) * CMP_STRIDE + CMP_LEN - 1)
        mask_c = (dc >= 0) & (r >= 1)
        dcf = dc.astype(F32)
        pcs = [_masked_softmax2(sc_all[hd * T:(hd + 1) * T] - NSA_SLOPES[hd] * dcf, mask_c) for hd in range(NSA_HEADS)]
        oc_ref[0] = _dot(jnp.concatenate(pcs, axis=0).astype(BF16), vc)
        W = sel_ref.shape[2]
        n_sel = -(-(past_len + T) // SEL_LEN)
        ov = _overlap(lax.broadcasted_iota(jnp.int32, (n_rows, W), 0) - 1, lax.broadcasted_iota(jnp.int32, (n_rows, W), 1))
        for g in range(NSA_GROUPS):
            pg = pcs[g * NSA_HPG]
            for p in range(1, NSA_HPG):
                pg = pg + pcs[g * NSA_HPG + p]
            imp = _dot(pg, ov, HIGHEST)
            sel_ref[0, g * T:(g + 1) * T, :] = _select_blocks(imp, tpos, n_sel, min(SEL_TOPN, n_sel))


def _nsa_dec_sel_kernel(pt_ref, *refs, T, past_len):
    pages = refs[:PAGES_PER_STEP]
    q_ref, sel_ref, sel_last_ref, ex_ref, kn_ref, vn_ref, os_ref, m_s, l_s, acc_s = refs[PAGES_PER_STEP:]
    s = pl.program_id(1)
    n_steps = pl.num_programs(1)
    page_rows = pages[0].shape[-1]
    CH = PAGES_PER_STEP * page_rows

    @pl.when(s == 0)
    def _():
        m_s[...] = jnp.full(m_s.shape, NEG_INF, F32)
        l_s[...] = jnp.zeros(l_s.shape, F32)
        acc_s[...] = jnp.zeros(acc_s.shape, F32)

    qp = _head_rows(q_ref, T)
    tpos = past_len + lax.broadcasted_iota(jnp.int32, (T, 1), 0)

    def update(s_all, mask_of_group, kpos, pv):
        ds = tpos - kpos
        dsf = ds.astype(F32)
        ps = []
        for g in range(NSA_GROUPS):
            mk = mask_of_group(g) & (ds >= 0)
            for p in range(NSA_HPG):
                hd = g * NSA_HPG + p
                r = slice(hd * T, (hd + 1) * T)
                sc = jnp.where(mk, s_all[r] - NSA_SLOPES[hd] * dsf, NEG_INF)
                pr, m_new, alpha, l_new = _online_update(sc, m_s[r], l_s[r])
                m_s[r] = m_new
                l_s[r] = l_new
                acc_s[r] = alpha * acc_s[r]
                ps.append(pr)
        acc_s[...] = acc_s[...] + pv(jnp.concatenate(ps, axis=0).astype(BF16))

    selk = _dot(sel_ref[0, 0].astype(BF16), ex_ref[...])
    s_all = jnp.concatenate([_dot(qp, pg[0, 0, 0].astype(BF16)) for pg in pages], axis=1)
    kpos = s * CH + lax.broadcasted_iota(jnp.int32, (T, CH), 1)

    def pv_pages(pmat):
        out = jnp.zeros((NSA_HEADS * T, LANES), F32)
        for i, pg in enumerate(pages):
            out = out + _dot_nt(pmat[:, i * page_rows:(i + 1) * page_rows], pg[0, 0, 1].astype(BF16))
        return out

    update(s_all, lambda g: selk[g * T:(g + 1) * T] > 0.5, kpos, pv_pages)

    @pl.when(s == n_steps - 1)
    def _():
        nk = kn_ref.shape[1]
        sn = _dot_nt(qp, kn_ref[0])
        lane = lax.broadcasted_iota(jnp.int32, (T, nk), 1)
        sl = sel_last_ref[0, 0]
        update(sn, lambda g: (sl[g * T:(g + 1) * T, 0:1] > 0.5) & (lane < T), past_len + lane,
               lambda pmat: _dot(pmat, vn_ref[0]))
        os_ref[0] = acc_s[...] / jnp.maximum(l_s[...], 1e-30)


def _nsa_dec_win_kernel(q_ref, g_ref, oc_ref, os_ref, wp_ref, kwn_ref, vwn_ref, o_ref, *, T, past_len):
    qp = _head_rows(q_ref, T)
    w_src = wp_ref.shape[-1]
    nk = kwn_ref.shape[1]
    sw_all = jnp.concatenate([_dot(qp, wp_ref[0, 0, 0].astype(BF16)), _dot_nt(qp, kwn_ref[0])], axis=1)
    j = lax.broadcasted_iota(jnp.int32, (T, w_src + nk), 1)
    tpos = past_len + lax.broadcasted_iota(jnp.int32, (T, 1), 0)
    wpos = past_len - w_src + j
    dw = tpos - wpos
    mask_w = (dw >= 0) & (dw < WINDOW) & (wpos >= 0) & (j < w_src + T)
    dwf = dw.astype(F32)
    pws = [_masked_softmax2(sw_all[hd * T:(hd + 1) * T] - NSA_SLOPES[hd] * dwf, mask_w) for hd in range(NSA_HEADS)]
    pw = jnp.concatenate(pws, axis=0).astype(BF16)
    o_w = _dot_nt(pw[:, :w_src], wp_ref[0, 0, 1].astype(BF16)) + _dot(pw[:, w_src:], vwn_ref[0])
    _store_heads(o_ref, jax.nn.sigmoid(g_ref[0]), oc_ref[0], os_ref[0], o_w, T)


def nsa_decode(zb, zd, zcb, cache_nsa_kv, cache_win_kv, page_table, cmp_w_l, cmp_b_l, layer):
    DB, T, _ = zb.shape
    n_pool, page_rows = cache_nsa_kv.shape[:2]
    n_pages = page_table.shape[1]
    past_len = n_pages * page_rows
    n_steps = n_pages // PAGES_PER_STEP
    sub_per_page = page_rows // CMP_STRIDE
    n_rows = n_pages * sub_per_page
    assert (n_rows - 1) * CMP_STRIDE + CMP_LEN - 1 > past_len + T - 1
    assert T <= CMP_STRIDE and n_pages % PAGES_PER_STEP == 0 and page_rows == LANES
    pt = page_table.reshape(-1).astype(jnp.int32)
    cache_t = jnp.transpose(cache_nsa_kv, (0, 2, 3, 4, 5, 1)).reshape(n_pool, DEPTH, 4, NSA_KV, page_rows)
    page_blk = (1, 1, 2, NSA_KV, page_rows)

    rr = np.arange(page_rows)
    perm = jnp.asarray(rr[None, :] == ((rr % sub_per_page) * CMP_STRIDE + rr // sub_per_page)[:, None], BF16)
    bd = _cmp_blockdiag(cmp_w_l)
    bias = _cmp_bias(cmp_b_l)

    n_sel = -(-(past_len + T) // SEL_LEN)
    blocks_per_step = PAGES_PER_STEP * page_rows // SEL_LEN
    sel_used = (n_steps + 1) * blocks_per_step
    sel_w = -(-sel_used // LANES) * LANES
    assert sel_used >= n_sel and blocks_per_step <= LANES
    q_spec = pl.BlockSpec((1, T, NSA_QPAD), lambda b, s, pt: (b, 0, 0))
    const = lambda a: pl.BlockSpec(a.shape, lambda b, s, pt: (0,) * a.ndim)

    o_c, sel = pl.pallas_call(
        functools.partial(_nsa_dec_cmp_kernel, T=T, past_len=past_len, n_rows=n_rows),
        grid_spec=pltpu.PrefetchScalarGridSpec(
            num_scalar_prefetch=1, grid=(DB, n_steps),
            in_specs=_page_specs(page_blk, 0, layer, n_pages) + [const(perm), const(bd), const(bias), q_spec],
            out_specs=[pl.BlockSpec((1, NSA_HEADS * T, LANES), lambda b, s, pt: (b, 0, 0)),
                       pl.BlockSpec((1, NSA_GROUPS * T, sel_w), lambda b, s, pt: (b, 0, 0))],
            scratch_shapes=[pltpu.VMEM((2, n_rows, LANES), F32), pltpu.VMEM((2, 1, LANES), F32)]),
        out_shape=[jax.ShapeDtypeStruct((DB, NSA_HEADS * T, LANES), F32),
                   jax.ShapeDtypeStruct((DB, NSA_GROUPS * T, sel_w), F32)],
        compiler_params=_params(("parallel", "arbitrary")),
        name="nsa_dec_cmp",
    )(pt, *([cache_t] * PAGES_PER_STEP), perm, bd, bias, zb)

    sel_steps = sel[:, :, :sel_used].reshape(DB, NSA_GROUPS * T, n_steps + 1, blocks_per_step).transpose(0, 2, 1, 3)
    sel_steps = jnp.pad(sel_steps, ((0, 0), (0, 0), (0, 0), (0, LANES - blocks_per_step)))
    kk = np.arange(PAGES_PER_STEP * page_rows) // SEL_LEN
    expand = jnp.asarray(kk[None, :] == np.arange(LANES)[:, None], BF16)
    pad_rows = LANES - T
    new_rows = lambda slot: jnp.pad(zcb[:, :, slot * NSA_KV:(slot + 1) * NSA_KV], ((0, 0), (0, pad_rows), (0, 0)))
    new_spec = pl.BlockSpec((1, LANES, LANES), lambda b, s, pt: (b, 0, 0))
    sel_blk = (1, 1, NSA_GROUPS * T, LANES)

    o_s = pl.pallas_call(
        functools.partial(_nsa_dec_sel_kernel, T=T, past_len=past_len),
        grid_spec=pltpu.PrefetchScalarGridSpec(
            num_scalar_prefetch=1, grid=(DB, n_steps),
            in_specs=_page_specs(page_blk, 1, layer, n_pages)
            + [q_spec, pl.BlockSpec(sel_blk, lambda b, s, pt: (b, s, 0, 0)),
               pl.BlockSpec(sel_blk, lambda b, s, pt: (b, n_steps, 0, 0)), const(expand), new_spec, new_spec],
            out_specs=pl.BlockSpec((1, NSA_HEADS * T, LANES), lambda b, s, pt: (b, 0, 0)),
            scratch_shapes=[pltpu.VMEM((NSA_HEADS * T, 1), F32), pltpu.VMEM((NSA_HEADS * T, 1), F32),
                            pltpu.VMEM((NSA_HEADS * T, LANES), F32)]),
        out_shape=jax.ShapeDtypeStruct((DB, NSA_HEADS * T, LANES), F32),
        compiler_params=_params(("parallel", "arbitrary")),
        name="nsa_dec_sel",
    )(pt, *([cache_t] * PAGES_PER_STEP), zb, sel_steps, sel_steps, expand, new_rows(2), new_rows(3))

    w_src = cache_win_kv.shape[1]
    win_t = jnp.transpose(cache_win_kv, (0, 2, 3, 4, 5, 1)).reshape(DB, DEPTH, 2, NSA_KV, w_src)
    b3 = lambda shp: pl.BlockSpec(shp, lambda b: (b, 0, 0))
    return pl.pallas_call(
        functools.partial(_nsa_dec_win_kernel, T=T, past_len=past_len),
        grid=(DB,),
        in_specs=[b3((1, T, NSA_QPAD)), b3((1, T, GATE_COLS)), b3((1, NSA_HEADS * T, LANES)), b3((1, NSA_HEADS * T, LANES)),
                  pl.BlockSpec((1, 1, 2, NSA_KV, w_src), lambda b: (b, layer, 0, 0, 0)),
                  b3((1, LANES, LANES)), b3((1, LANES, LANES))],
        out_specs=b3((1, T, NSA_QPAD)),
        out_shape=jax.ShapeDtypeStruct((DB, T, NSA_QPAD), BF16),
        compiler_params=_params(("parallel",)),
        name="nsa_dec_win",
    )(zb, zd, o_c, o_s, win_t, new_rows(4), new_rows(5))


def _merge_kernel(x_ref, hm_ref, hn_ref, hx_ref, wg_ref, bg_ref, wbm_ref, wbn_ref, wbx_ref, wo_ref, g_ref, b_ref, o_ref):
    x = x_ref[...]
    xb = x.astype(BF16)
    merged = None
    for c, (h_ref, w_ref) in enumerate(((hm_ref, wbm_ref), (hn_ref, wbn_ref), (hx_ref, wbx_ref))):
        sl = slice(c * D_MODEL, (c + 1) * D_MODEL)
        gate = jax.nn.sigmoid(_dot(xb, wg_ref[:, sl]) + bg_ref[:, sl])
        term = gate * _dot(h_ref[...], w_ref[...])
        merged = term if merged is None else merged + term
    mix = _dot(merged.astype(BF16), wo_ref[...])
    o_ref[...] = _layer_norm(ALPHA * x + mix, g_ref[...], b_ref[...])


def merge(x2d, hm, hn, hx, wg, bg, wbr, wo, g, b, tm):
    n = x2d.shape[0]
    row = lambda a: pl.BlockSpec((tm, a.shape[1]), lambda i: (i, 0))
    full = lambda a: pl.BlockSpec(a.shape, lambda i: (0,) * a.ndim)
    return pl.pallas_call(
        _merge_kernel,
        grid=(n // tm,),
        in_specs=[row(x2d), row(hm), row(hn), row(hx), full(wg), full(bg), full(wbr[0]), full(wbr[1]), full(wbr[2]),
                  full(wo), full(g), full(b)],
        out_specs=row(x2d),
        out_shape=jax.ShapeDtypeStruct((n, D_MODEL), F32),
        compiler_params=_params(("parallel",)),
        name="merge",
    )(x2d, hm, hn, hx, wg, bg, wbr[0], wbr[1], wbr[2], wo, g, b)


def _ffn_kernel(x_ref, wg_ref, wu_ref, wd_ref, g_ref, b_ref, o_ref, acc_s):
    f = pl.program_id(1)

    @pl.when(f == 0)
    def _():
        acc_s[...] = jnp.zeros(acc_s.shape, F32)

    xb = x_ref[...].astype(BF16)
    hcur = jax.nn.silu(_dot(xb, wg_ref[...])) * _dot(xb, wu_ref[...])
    acc_s[...] += _dot(hcur.astype(BF16), wd_ref[...])

    @pl.when(f == pl.num_programs(1) - 1)
    def _():
        o_ref[...] = _layer_norm(ALPHA * x_ref[...] + acc_s[...], g_ref[...], b_ref[...])


def ffn(x2d, w_up, w_down, g, b, tm, tf):
    n = x2d.shape[0]
    nf = (w_up.shape[1] // 2) // tf
    return pl.pallas_call(
        _ffn_kernel,
        grid=(n // tm, nf),
        in_specs=[pl.BlockSpec((tm, D_MODEL), lambda i, f: (i, 0)),
                  pl.BlockSpec((D_MODEL, tf), lambda i, f: (0, f)),
                  pl.BlockSpec((D_MODEL, tf), lambda i, f: (0, nf + f)),
                  pl.BlockSpec((tf, D_MODEL), lambda i, f: (f, 0)),
                  pl.BlockSpec((1, D_MODEL), lambda i, f: (0, 0)),
                  pl.BlockSpec((1, D_MODEL), lambda i, f: (0, 0))],
        out_specs=pl.BlockSpec((tm, D_MODEL), lambda i, f: (i, 0)),
        out_shape=jax.ShapeDtypeStruct((n, D_MODEL), F32),
        scratch_shapes=[pltpu.VMEM((tm, D_MODEL), F32)],
        compiler_params=_params(("parallel", "arbitrary")),
        name="dense_ffn",
    )(x2d, w_up, w_up, w_down, g, b)


def _router_kernel(x_ref, w_ref, b_ref, o_ref):
    logits = _dot(x_ref[...], w_ref[...], HIGHEST) + b_ref[...]
    lane = lax.broadcasted_iota(jnp.int32, logits.shape, 1)
    W = logits.shape[1]
    logits = jnp.where(lane < N_EXPERTS, logits, NEG_INF)
    m1 = jnp.max(logits, axis=-1, keepdims=True)
    i1 = jnp.min(jnp.where(logits == m1, lane, W), axis=-1, keepdims=True)
    rest = jnp.where(lane == i1, NEG_INF, logits)
    m2 = jnp.max(rest, axis=-1, keepdims=True)
    i2 = jnp.min(jnp.where(rest == m2, lane, W), axis=-1, keepdims=True)
    e2 = jnp.exp(m2 - m1)
    den = 1.0 + e2
    o_ref[...] = jnp.where(lane == i1, 1.0 / den, 0.0) + jnp.where(lane == i2, e2 / den, 0.0)


def router(x2d, w_router, b_router, tm):
    n = x2d.shape[0]
    w = jnp.pad(w_router.astype(F32), ((0, 0), (0, LANES - N_EXPERTS)))
    bb = jnp.pad(b_router.astype(F32), (0, LANES - N_EXPERTS)).reshape(1, LANES)
    return pl.pallas_call(
        _router_kernel,
        grid=(n // tm,),
        in_specs=[pl.BlockSpec((tm, D_MODEL), lambda i: (i, 0)), pl.BlockSpec(w.shape, lambda i: (0, 0)),
                  pl.BlockSpec(bb.shape, lambda i: (0, 0))],
        out_specs=pl.BlockSpec((tm, LANES), lambda i: (i, 0)),
        out_shape=jax.ShapeDtypeStruct((n, LANES), F32),
        compiler_params=_params(("parallel",)),
        name="router",
    )(x2d, w, bb)


def _moe_kernel(np_ref, x_ref, cw_ref, cwt_ref, wg_ref, wu_ref, wd_ref, g_ref, b_ref, o_ref,
                xb_s, y_s, pos_s, post_s, xg_s, ws_s, acc_s, *, blk, cap, cap_pad):
    j = pl.program_id(0)
    e = pl.program_id(1)
    f = pl.program_id(2)
    n_e = pl.num_programs(1)
    n_f = pl.num_programs(2)
    n_pass = np_ref[j * n_e + e]
    RC = min(256, blk)

    @pl.when((e == 0) & (f == 0))
    def _():
        xb_s[...] = x_ref[...].astype(BF16)
        y_s[...] = jnp.zeros(y_s.shape, F32)
        cw = cw_ref[...]
        cwt = cwt_ref[...]
        routed = jnp.where(cw != 0.0, 1.0, 0.0).astype(BF16)
        routed_t = jnp.where(cwt != 0.0, 1.0, 0.0).astype(BF16)
        for rc in range(blk // RC):
            rows_i = rc * RC + lax.broadcasted_iota(jnp.int32, (RC, blk), 0)
            cols_i = lax.broadcasted_iota(jnp.int32, (RC, blk), 1)
            before = jnp.where(cols_i < rows_i, 1.0, 0.0).astype(BF16)
            cnt = _dot(before, routed)
            pos_s[rc * RC:(rc + 1) * RC, :] = jnp.where(cw[rc * RC:(rc + 1) * RC] != 0.0, cnt, -1.0)
            rows_j = lax.broadcasted_iota(jnp.int32, (blk, RC), 0)
            cols_j = rc * RC + lax.broadcasted_iota(jnp.int32, (blk, RC), 1)
            before_t = jnp.where(rows_j < cols_j, 1.0, 0.0).astype(BF16)
            cnt_t = _dot(routed_t, before_t)
            post_s[:, rc * RC:(rc + 1) * RC] = jnp.where(cwt[:, rc * RC:(rc + 1) * RC] != 0.0, cnt_t, -1.0)

    @pl.when(f == 0)
    def _():
        prow = post_s[pl.ds(e, 1), :]
        wrow = cwt_ref[pl.ds(e, 1), :]

        def gather(u, carry):
            rr = lax.broadcasted_iota(jnp.int32, (cap_pad, 1), 0)
            slot = jnp.where(rr < cap, u * cap + rr, NO_SLOT).astype(F32)
            hit = prow == slot
            xg_s[u] = _dot(jnp.where(hit, 1.0, 0.0).astype(BF16), xb_s[...]).astype(BF16)
            ws_s[u] = jnp.sum(jnp.where(hit, wrow, 0.0), axis=-1, keepdims=True)
            acc_s[u] = jnp.zeros(acc_s.shape[1:], F32)
            return carry

        lax.fori_loop(0, n_pass, gather, 0)

    def expert(u, carry):
        xg = xg_s[u, 0:cap, :]
        hcur = jax.nn.silu(_dot(xg, wg_ref[0])) * _dot(xg, wu_ref[0]) * ws_s[u, 0:cap, :]
        acc_s[u, 0:cap, :] += _dot(hcur.astype(BF16), wd_ref[0])
        return carry

    lax.fori_loop(0, n_pass, expert, 0)

    @pl.when(f == n_f - 1)
    def _():
        lane = lax.broadcasted_iota(jnp.int32, pos_s.shape, 1)
        pcol = jnp.sum(jnp.where(lane == e, pos_s[...], 0.0), axis=-1, keepdims=True)

        def scatter(u, carry):
            cc = lax.broadcasted_iota(jnp.int32, (1, cap_pad), 1)
            slot = jnp.where(cc < cap, u * cap + cc, NO_SLOT).astype(F32)
            hit = jnp.where(pcol == slot, 1.0, 0.0).astype(BF16)
            y_s[...] += _dot(hit, acc_s[u].astype(BF16))
            return carry

        lax.fori_loop(0, n_pass, scatter, 0)

    @pl.when((e == n_e - 1) & (f == n_f - 1))
    def _():
        o_ref[...] = _layer_norm(ALPHA * x_ref[...] + y_s[...], g_ref[...], b_ref[...])


def moe(x2d, cw, w_up, w_down, g, b, tf):
    n = x2d.shape[0]
    E, _, F2 = w_up.shape
    nf = (F2 // 2) // tf
    blk = _row_tile(n, MOE_BLOCK)
    cap = min(MOE_CAP, blk)
    cap_pad = -(-cap // LANES) * LANES
    max_pass = -(-blk // cap)
    nblk = n // blk
    cwt = jnp.swapaxes(cw[:, :ROUTE_ROWS], 0, 1)
    counts = jnp.sum((cw[:, :E] != 0.0).reshape(nblk, blk, E), axis=1)
    n_pass = ((counts + cap - 1) // cap).astype(jnp.int32).reshape(-1)
    return pl.pallas_call(
        functools.partial(_moe_kernel, blk=blk, cap=cap, cap_pad=cap_pad),
        grid_spec=pltpu.PrefetchScalarGridSpec(
            num_scalar_prefetch=1, grid=(nblk, E, nf),
            in_specs=[pl.BlockSpec((blk, D_MODEL), lambda j, e, f, npr: (j, 0)),
                      pl.BlockSpec((blk, LANES), lambda j, e, f, npr: (j, 0)),
                      pl.BlockSpec((ROUTE_ROWS, blk), lambda j, e, f, npr: (0, j)),
                      pl.BlockSpec((1, D_MODEL, tf), lambda j, e, f, npr: (e, 0, f)),
                      pl.BlockSpec((1, D_MODEL, tf), lambda j, e, f, npr: (e, 0, nf + f)),
                      pl.BlockSpec((1, tf, D_MODEL), lambda j, e, f, npr: (e, f, 0)),
                      pl.BlockSpec((1, D_MODEL), lambda j, e, f, npr: (0, 0)),
                      pl.BlockSpec((1, D_MODEL), lambda j, e, f, npr: (0, 0))],
            out_specs=pl.BlockSpec((blk, D_MODEL), lambda j, e, f, npr: (j, 0)),
            scratch_shapes=[pltpu.VMEM((blk, D_MODEL), BF16), pltpu.VMEM((blk, D_MODEL), F32),
                            pltpu.VMEM((blk, LANES), F32), pltpu.VMEM((ROUTE_ROWS, blk), F32),
                            pltpu.VMEM((max_pass, cap_pad, D_MODEL), BF16), pltpu.VMEM((max_pass, cap_pad, 1), F32),
                            pltpu.VMEM((max_pass, cap_pad, D_MODEL), F32)]),
        out_shape=jax.ShapeDtypeStruct((n, D_MODEL), F32),
        compiler_params=_params(("parallel", "arbitrary", "arbitrary")),
        name="moe_ffn",
    )(n_pass, x2d, cw, cwt, w_up, w_up, w_down, g, b)


def _layer(x, lw, l, *, mem_kv, mem_blk, mstate, decode):
    B, T, _ = x.shape
    n = B * T
    x2d = x.reshape(n, D_MODEL)
    tm = _row_tile(n, 512)
    za, zb, zc, zcb, zd = in_proj(x2d, lw["in"], tm)
    za, zb, zc, zcb, zd = (a.reshape(B, T, -1) for a in (za, zb, zc, zcb, zd))

    if T % 256 == 0:
        h_m, ct, nt, mt = mlstm(za, zd, *mstate, lw["norm_g"], L=256, t_valid=256)
    else:
        padt = lambda a: jnp.pad(a, ((0, 0), (0, LANES - T), (0, 0)))
        h_m, ct, nt, mt = mlstm(padt(za), padt(zd), *mstate, lw["norm_g"], L=LANES, t_valid=T)
        h_m = h_m[:, :T]

    if decode is None:
        kc, vc = cmp_tokens(zc, lw["cmp_w"], lw["cmp_b"])
        h_n = nsa_prompt(zb, zd, zcb, kc, vc, tq=min(256, T), CH=min(512, T))
    else:
        h_n = nsa_decode(zb, zd, zcb, decode["nsa"], decode["win"], decode["pt"], lw["cmp_w"], lw["cmp_b"], l)

    h_x = mem_attn(zb, mem_kv, mem_blk[0], mem_blk[1], tq=_row_tile(T, 512))

    wg, bg = lw["in"]["mg"]
    flat = lambda a: a.reshape(n, -1)
    x1 = merge(x2d, flat(h_m), flat(h_n), flat(h_x), wg, bg, lw["w_branch"], lw["w_out"], lw["ln1_g"], lw["ln1_b"], tm)

    if l % 2 == 0:
        x2 = ffn(x1, lw["ffn_up"], lw["ffn_down"], lw["ln2_g"], lw["ln2_b"], _row_tile(n, 1024), 256)
    else:
        cw = router(x1, lw["w_router"], lw["b_router"], _row_tile(n, 1024))
        x2 = moe(x1, cw, lw["ffn_up"], lw["ffn_down"], lw["ln2_g"], lw["ln2_b"], 896)
    return x2.reshape(B, T, D_MODEL), zc, (ct, nt, mt)


def kernel(x_prompt, x_sample, mem_prompt, cache_nsa_kv, cache_win_kv, state_mlstm_C, state_mlstm_n,
           state_mlstm_m, cache_mem_kv, page_table, w_in, b_in, mlstm_norm_g, cmp_w, cmp_b, w_mem_kv,
           w_branch, w_out, ln1_g, ln1_b, ln2_g, ln2_b, ffn_w_up, ffn_w_down, moe_w_router, moe_b_router,
           moe_w_up, moe_w_down):
    B, T, _ = x_prompt.shape
    DB, TS, _ = x_sample.shape
    xp, xs = x_prompt, x_sample
    nsa_p, nsa_s, win_p, win_s = [], [], [], []
    Cp, np_, mp, Cs, ns, ms, memkv_p = [], [], [], [], [], [], []
    mem_s = cache_mem_kv.reshape(DB, MEM_LEN, DEPTH * 2 * X_WIDTH)
    row1 = lambda a: a.reshape(1, -1).astype(F32)
    for l in range(DEPTH):
        wbr = (w_branch[l, 0].astype(BF16), _pad_branch_rows(w_branch[l, 1]).astype(BF16), w_branch[l, 2].astype(BF16))
        lw = dict(norm_g=mlstm_norm_g[l], cmp_w=cmp_w[l], cmp_b=cmp_b[l], w_branch=wbr,
                  w_out=w_out[l].astype(BF16), ln1_g=row1(ln1_g[l]), ln1_b=row1(ln1_b[l]),
                  ln2_g=row1(ln2_g[l]), ln2_b=row1(ln2_b[l]))
        lw["in"] = _split_in_proj(w_in[l], b_in[l])
        if l % 2 == 0:
            lw["ffn_up"] = ffn_w_up[l // 2].astype(BF16)
            lw["ffn_down"] = ffn_w_down[l // 2].astype(BF16)
        else:
            lw["ffn_up"] = moe_w_up[l // 2].astype(BF16)
            lw["ffn_down"] = moe_w_down[l // 2].astype(BF16)
            lw["w_router"] = moe_w_router[l // 2]
            lw["b_router"] = moe_b_router[l // 2]

        mkv = proj(mem_prompt.reshape(B * MEM_LEN, D_MODEL), w_mem_kv[l].astype(BF16), _row_tile(B * MEM_LEN, 512))
        mkv = mkv.reshape(B, MEM_LEN, 2 * X_WIDTH)
        st0 = (jnp.zeros((B, M_HEADS, M_DH, M_DH), F32), jnp.zeros((B, M_HEADS, M_DH), F32), jnp.zeros((B, M_HEADS), F32))
        xp, zc, st = _layer(xp, lw, l, mem_kv=mkv, mem_blk=(0, 1), mstate=st0, decode=None)
        kvn = zc.reshape(B, T, 6, NSA_GROUPS, NSA_DH)
        nsa_p.append(kvn[:, :, :4])
        win_p.append(kvn[:, T - min(WINDOW, T):, 4:])
        Cp.append(st[0]); np_.append(st[1]); mp.append(st[2])
        memkv_p.append(mkv.reshape(B, MEM_LEN, 2, X_HEADS, X_DH))

        sts = (state_mlstm_C[:, l], state_mlstm_n[:, l], state_mlstm_m[:, l])
        dec = dict(nsa=cache_nsa_kv, win=cache_win_kv, pt=page_table)
        xs, zc, st = _layer(xs, lw, l, mem_kv=mem_s, mem_blk=(2 * l, 2 * l + 1), mstate=sts, decode=dec)
        kvn = zc.reshape(DB, TS, 6, NSA_GROUPS, NSA_DH)
        nsa_s.append(kvn[:, :, :4])
        win_s.append(jnp.concatenate([cache_win_kv[:, :, l].astype(F32), kvn[:, :, 4:]], axis=1)[:, TS:])
        Cs.append(st[0]); ns.append(st[1]); ms.append(st[2])
    return (xp, xs,
            jnp.stack(nsa_p, axis=2), jnp.stack(nsa_s, axis=2),
            jnp.stack(win_p, axis=2), jnp.stack(win_s, axis=2),
            jnp.stack(Cp, axis=1), jnp.stack(np_, axis=1), jnp.stack(mp, axis=1),
            jnp.stack(Cs, axis=1), jnp.stack(ns, axis=1), jnp.stack(ms, axis=1),
            jnp.stack(memkv_p, axis=2))
```

```python
import functools

import numpy as np
import jax
import jax.numpy as jnp
from jax import lax
from jax.experimental import pallas as pl
from jax.experimental.pallas import tpu as pltpu

D_MODEL = 1024
DEPTH = 2
BRANCH_WIDTH = 512
N_BRANCH = 3
M_HEADS = 4
M_DH = BRANCH_WIDTH // M_HEADS
M_WIDTH = M_HEADS * M_DH
NSA_HEADS = 8
NSA_DH = BRANCH_WIDTH // NSA_HEADS
NSA_WIDTH = NSA_HEADS * NSA_DH
NSA_GROUPS = 2
NSA_HPG = NSA_HEADS // NSA_GROUPS
NSA_KV = NSA_GROUPS * NSA_DH
CMP_LEN = 32
CMP_STRIDE = 16
SEL_LEN = 64
SEL_TOPN = 16
WINDOW = 512
MEM_LEN = 256
X_HEADS = 4
X_DH = BRANCH_WIDTH // X_HEADS
X_WIDTH = X_HEADS * X_DH
D_FF = 2816
N_EXPERTS = 8
TOP_K = 2
D_FF_EXPERT = 3584
ALPHA = (2.0 * DEPTH) ** 0.25
LN_EPS = 1e-5
IN_SPLITS = (M_WIDTH, M_WIDTH, M_WIDTH, M_HEADS, M_HEADS, M_WIDTH,
             NSA_WIDTH, 6 * NSA_KV, 3 * NSA_HEADS, X_WIDTH, N_BRANCH * D_MODEL)
LOG2E = 1.4426950408889634
NSA_SLOPES = tuple(LOG2E * 2.0 ** (-8.0 * (h + 1) / NSA_HEADS) for h in range(NSA_HEADS))
MASK_BIAS = -(2.0 ** 30)
SEL_FEAT = 64
SEL_QUARTERS = 4
SLOPE_PARTS = 3

LANES = 128
SUBLANES = 8
VMEM_LIMIT = 56 * 1024 * 1024
PAGES_PER_STEP = 32
MLSTM_CHUNK = 256
GATE_COLS = LANES
NG_OFF = 2 * M_HEADS
NSA_QPAD = NSA_HEADS * LANES
MOE_BLOCK = 1024
MOE_CAP = 320
NO_SLOT = -2
ROUTE_ROWS = 16

F32 = jnp.float32
BF16 = jnp.bfloat16
NEG_INF = float("-inf")
HIGHEST = lax.Precision.HIGHEST


def _dot(a, b, precision=None):
    return jnp.dot(a, b, preferred_element_type=F32, precision=precision)


def _dot_nt(a, b, precision=None):
    return lax.dot_general(a, b, (((1,), (1,)), ((), ())), preferred_element_type=F32, precision=precision)


def _dot_tn(a, b):
    return lax.dot_general(a, b, (((0,), (0,)), ((), ())), preferred_element_type=F32)


def _params(sem):
    return pltpu.CompilerParams(dimension_semantics=sem, vmem_limit_bytes=VMEM_LIMIT)


def _masked_softmax2(s, mask):
    s = jnp.where(mask, s, NEG_INF)
    mx = jnp.max(s, axis=-1, keepdims=True)
    mx = jnp.where(mx > NEG_INF, mx, 0.0)
    p = jnp.where(mask, jnp.exp2(s - mx), 0.0)
    return p / jnp.maximum(jnp.sum(p, axis=-1, keepdims=True), 1e-30)


def _layer_norm(xf, g, b):
    mu = jnp.mean(xf, axis=-1, keepdims=True)
    var = jnp.mean(jnp.square(xf - mu), axis=-1, keepdims=True)
    return (xf - mu) * lax.rsqrt(var + LN_EPS) * g + b


def _row_tile(n, pref):
    return pref if n % pref == 0 else n


def _in_proj_kernel(x_ref, wa_ref, wb_ref, wc_ref, wd_ref, ba_ref, bb_ref, bc_ref, bd_ref,
                    oa_ref, ob_ref, oc_ref, ocb_ref, od_ref):
    x = x_ref[...].astype(BF16)
    oa_ref[...] = (_dot(x, wa_ref[...]) + ba_ref[...]).astype(BF16)
    ob_ref[...] = (_dot(x, wb_ref[...]) + bb_ref[...]).astype(BF16)
    c = _dot(x, wc_ref[...]) + bc_ref[...]
    oc_ref[...] = c
    ocb_ref[...] = c.astype(BF16)
    od_ref[...] = _dot(x, wd_ref[...]) + bd_ref[...]


def _group_select():
    return jax.nn.one_hot(np.arange(NSA_HEADS) // NSA_HPG, NSA_GROUPS, dtype=F32)


def _pad_branch_rows(w_nsa):
    return jnp.einsum("hem,hg->hgem", w_nsa.reshape(NSA_HEADS, NSA_DH, -1), _group_select()).reshape(NSA_QPAD, -1)


def _split_in_proj(w_in_l, b_in_l):
    offs = np.cumsum((0,) + IN_SPLITS)
    w = [w_in_l[:, offs[i]:offs[i + 1]] for i in range(len(IN_SPLITS))]
    b = [b_in_l[offs[i]:offs[i + 1]] for i in range(len(IN_SPLITS))]
    mq, mk, mv, mi, mf, mo, nq, nkv, ng, xq, mg = range(11)
    pad = GATE_COLS - 2 * M_HEADS - 3 * NSA_HEADS

    def cat(ids, zpad=0):
        ww = jnp.concatenate([w[i] for i in ids], axis=1)
        bb = jnp.concatenate([b[i] for i in ids])
        if zpad:
            ww = jnp.pad(ww, ((0, 0), (0, zpad)))
            bb = jnp.pad(bb, (0, zpad))
        return ww.astype(BF16), bb.reshape(1, -1).astype(F32)

    scale = NSA_DH ** -0.5 * LOG2E
    w[nq] = jnp.einsum("dhe,hg->dhge", w[nq].reshape(D_MODEL, NSA_HEADS, NSA_DH) * scale, _group_select()).reshape(D_MODEL, NSA_QPAD)
    b[nq] = jnp.einsum("he,hg->hge", b[nq].reshape(NSA_HEADS, NSA_DH) * scale, _group_select()).reshape(NSA_QPAD)
    return dict(a=cat([mq, mk, mv, mo]), b=cat([nq, xq]), c=cat([nkv]), d=cat([mi, mf, ng], pad), mg=cat([mg]))


def in_proj(x2d, wp, tm):
    n = x2d.shape[0]
    (wa, ba), (wb, bb), (wc, bc), (wd, bd) = wp["a"], wp["b"], wp["c"], wp["d"]
    full = lambda arr: pl.BlockSpec(arr.shape, lambda i: (0, 0))
    row = lambda w: pl.BlockSpec((tm, w), lambda i: (i, 0))
    return pl.pallas_call(
        _in_proj_kernel,
        grid=(n // tm,),
        in_specs=[row(D_MODEL), full(wa), full(wb), full(wc), full(wd), full(ba), full(bb), full(bc), full(bd)],
        out_specs=[row(wa.shape[1]), row(wb.shape[1]), row(wc.shape[1]), row(wc.shape[1]), row(wd.shape[1])],
        out_shape=[jax.ShapeDtypeStruct((n, wa.shape[1]), BF16), jax.ShapeDtypeStruct((n, wb.shape[1]), BF16),
                   jax.ShapeDtypeStruct((n, wc.shape[1]), F32), jax.ShapeDtypeStruct((n, wc.shape[1]), BF16),
                   jax.ShapeDtypeStruct((n, wd.shape[1]), F32)],
        compiler_params=_params(("parallel",)),
        name="in_proj",
    )(x2d, wa, wb, wc, wd, ba, bb, bc, bd)


def _proj_kernel(x_ref, w_ref, o_ref):
    o_ref[...] = _dot(x_ref[...].astype(BF16), w_ref[...])


def proj(x2d, w_bf16, tm):
    n, k = x2d.shape
    m = w_bf16.shape[1]
    return pl.pallas_call(
        _proj_kernel,
        grid=(n // tm,),
        in_specs=[pl.BlockSpec((tm, k), lambda i: (i, 0)), pl.BlockSpec((k, m), lambda i: (0, 0))],
        out_specs=pl.BlockSpec((tm, m), lambda i: (i, 0)),
        out_shape=jax.ShapeDtypeStruct((n, m), F32),
        compiler_params=_params(("parallel",)),
        name="mem_proj",
    )(x2d, w_bf16)


def _log_sigmoid(x):
    return jnp.minimum(x, 0.0) - jnp.log1p(jnp.exp(-jnp.abs(x)))


def _mlstm_kernel(q_ref, k_ref, v_ref, og_ref, gc_ref, gr_ref, c0_ref, n0_ref, m0_ref, ng_ref,
                  h_ref, ct_ref, nt_ref, mt_ref, c_s, n_s, m_s, *, L, t_valid):
    ci = pl.program_id(1)

    @pl.when(ci == 0)
    def _():
        c_s[...] = c0_ref[0]
        n_s[...] = n0_ref[0]
        m_s[...] = m0_ref[0]

    row = lax.broadcasted_iota(jnp.int32, (L, L), 0)
    col = lax.broadcasted_iota(jnp.int32, (L, L), 1)
    causal = row >= col
    tri = causal.astype(F32)
    tri_t = (row <= col).astype(F32)
    gc = gc_ref[0]
    gr = gr_ref[0]
    rvalid = lax.broadcasted_iota(jnp.int32, (L, GATE_COLS), 0) < t_valid
    cvalid = lax.broadcasted_iota(jnp.int32, (2 * M_HEADS, L), 1) < t_valid
    lf_c = jnp.where(rvalid, _log_sigmoid(gc), 0.0)
    lf_r = jnp.where(cvalid, _log_sigmoid(gr), 0.0)
    ig_c = jnp.where(rvalid, gc, NEG_INF)
    ig_r = jnp.where(cvalid, gr, NEG_INF)
    b_c = _dot(tri, lf_c, HIGHEST)
    b_r = _dot(lf_r, tri_t, HIGHEST)
    scale = M_DH ** -0.5

    for h in range(M_HEADS):
        sl = slice(h * M_DH, (h + 1) * M_DH)
        q = q_ref[0, :, sl]
        k = k_ref[0, :, sl]
        v = v_ref[0, :, sl]
        bc = b_c[:, M_HEADS + h:M_HEADS + h + 1]
        br = b_r[M_HEADS + h:M_HEADS + h + 1, :]
        igc = ig_c[:, h:h + 1]
        igr = ig_r[h:h + 1, :]
        m_prev = m_s[h, 0:1, 0:1]
        c_prev = c_s[h]
        n_prev = n_s[h:h + 1, :]

        dmat = jnp.where(causal, bc - br + igr, NEG_INF)
        inter = bc + m_prev
        m_t = jnp.maximum(inter, jnp.max(dmat, axis=-1, keepdims=True))
        s = _dot_nt(q, k) * scale * jnp.exp(dmat - m_t)
        sc_in = jnp.exp(inter - m_t)
        qf = q.astype(F32)
        num = sc_in * _dot_nt(q, c_prev.astype(BF16)) + _dot(s.astype(BF16), v)
        den = sc_in * jnp.sum(qf * n_prev, axis=-1, keepdims=True) + jnp.sum(s, axis=-1, keepdims=True)
        hh = num / jnp.maximum(jnp.abs(den), jnp.exp(-m_t))

        b_last = bc[L - 1:L, :]
        dec_c = b_last - bc + igc
        dec_r = b_last - br + igr
        m_new = jnp.maximum(b_last + m_prev, jnp.max(dec_r, axis=-1, keepdims=True))
        ws_c = jnp.exp(dec_c - m_new) * scale
        sc = jnp.exp(b_last + m_prev - m_new)
        vf = v.astype(F32)
        kf = k.astype(F32)
        c_new = sc * c_prev + _dot_tn((vf * ws_c).astype(BF16), k)
        n_new = sc * n_prev + jnp.sum(kf * ws_c, axis=0, keepdims=True)
        c_s[h] = c_new
        n_s[h:h + 1, :] = n_new
        m_s[h] = jnp.broadcast_to(m_new, m_s.shape[1:])

        og = og_ref[0, :, sl].astype(F32)
        hh = hh * jax.nn.sigmoid(og)
        mu = jnp.mean(hh, axis=-1, keepdims=True)
        var = jnp.mean(jnp.square(hh - mu), axis=-1, keepdims=True)
        hn = (hh - mu) * lax.rsqrt(var + LN_EPS) * ng_ref[:, sl]
        h_ref[0, :, sl] = hn.astype(h_ref.dtype)

    ct_ref[0] = c_s[...]
    nt_ref[0] = n_s[...]
    mt_ref[0] = m_s[...]


def mlstm(za, zd, c0, n0, m0, norm_g, *, L, t_valid):
    B, T, _ = za.shape
    nc = T // L
    gr = jnp.swapaxes(zd[:, :, :2 * M_HEADS], 1, 2)
    m0b = jnp.broadcast_to(m0.astype(F32)[:, :, None, None], (B, M_HEADS, SUBLANES, LANES))
    colspec = lambda j: pl.BlockSpec((1, L, M_WIDTH), lambda b, c, j=j: (b, c, j))
    st = lambda shp: pl.BlockSpec((1,) + shp, lambda b, c: (b,) + (0,) * len(shp))
    kern = functools.partial(_mlstm_kernel, L=L, t_valid=t_valid)
    h, ct, nt, mt = pl.pallas_call(
        kern,
        grid=(B, nc),
        in_specs=[colspec(0), colspec(1), colspec(2), colspec(3),
                  pl.BlockSpec((1, L, GATE_COLS), lambda b, c: (b, c, 0)),
                  pl.BlockSpec((1, 2 * M_HEADS, L), lambda b, c: (b, 0, c)),
                  st((M_HEADS, M_DH, M_DH)), st((M_HEADS, M_DH)), st((M_HEADS, SUBLANES, LANES)),
                  pl.BlockSpec((1, M_WIDTH), lambda b, c: (0, 0))],
        out_specs=[pl.BlockSpec((1, L, M_WIDTH), lambda b, c: (b, c, 0)),
                   st((M_HEADS, M_DH, M_DH)), st((M_HEADS, M_DH)), st((M_HEADS, SUBLANES, LANES))],
        out_shape=[jax.ShapeDtypeStruct((B, T, M_WIDTH), BF16),
                   jax.ShapeDtypeStruct((B, M_HEADS, M_DH, M_DH), F32),
                   jax.ShapeDtypeStruct((B, M_HEADS, M_DH), F32),
                   jax.ShapeDtypeStruct((B, M_HEADS, SUBLANES, LANES), F32)],
        scratch_shapes=[pltpu.VMEM((M_HEADS, M_DH, M_DH), F32), pltpu.VMEM((M_HEADS, M_DH), F32),
                        pltpu.VMEM((M_HEADS, SUBLANES, LANES), F32)],
        compiler_params=_params(("parallel", "arbitrary")),
        name="mlstm",
    )(za, za, za, za, zd, gr, c0.astype(F32), n0.astype(F32), m0b, norm_g.reshape(1, M_WIDTH).astype(F32))
    return h, ct, nt, mt[:, :, 0, 0]


def _mem_attn_kernel(q_ref, k_ref, v_ref, o_ref):
    scale = X_DH ** -0.5
    for h in range(X_HEADS):
        sl = slice(h * X_DH, (h + 1) * X_DH)
        s = _dot_nt(q_ref[0, :, sl], k_ref[0, :, sl].astype(BF16)) * scale
        mx = jnp.max(s, axis=-1, keepdims=True)
        p = jnp.exp(s - mx)
        p = p / jnp.sum(p, axis=-1, keepdims=True)
        o_ref[0, :, sl] = _dot(p.astype(BF16), v_ref[0, :, sl].astype(BF16)).astype(o_ref.dtype)


def mem_attn(zb, kv, k_blk, v_blk, tq):
    B, T, _ = zb.shape
    return pl.pallas_call(
        _mem_attn_kernel,
        grid=(B, T // tq),
        in_specs=[pl.BlockSpec((1, tq, X_WIDTH), lambda b, i: (b, i, NSA_QPAD // X_WIDTH)),
                  pl.BlockSpec((1, MEM_LEN, X_WIDTH), lambda b, i: (b, 0, k_blk)),
                  pl.BlockSpec((1, MEM_LEN, X_WIDTH), lambda b, i: (b, 0, v_blk))],
        out_specs=pl.BlockSpec((1, tq, X_WIDTH), lambda b, i: (b, i, 0)),
        out_shape=jax.ShapeDtypeStruct((B, T, X_WIDTH), BF16),
        compiler_params=_params(("parallel", "parallel")),
        name="mem_attn",
    )(zb, kv, kv)


def _head_rows(q_ref, nq):
    parts = [q_ref[0, :, hd * LANES:(hd + 1) * LANES] for hd in range(NSA_HEADS)]
    if nq % 16:
        return jnp.concatenate([p.astype(F32) for p in parts], axis=0).astype(BF16)
    return jnp.concatenate(parts, axis=0)


def _store_heads(o_ref, gate, o_c, o_s, o_w, nq):
    lane = lax.broadcasted_iota(jnp.int32, (nq, LANES), 1)
    for hd in range(NSA_HEADS):
        r = slice(hd * nq, (hd + 1) * nq)
        gcol = lambda br: gate[:, NG_OFF + br * NSA_HEADS + hd:NG_OFF + br * NSA_HEADS + hd + 1]
        val = gcol(0) * o_c[r] + gcol(1) * o_s[r] + gcol(2) * o_w[r]
        keep = (lane >= NSA_DH) if hd // NSA_HPG == 1 else (lane < NSA_DH)
        o_ref[0, :, hd * LANES:(hd + 1) * LANES] = jnp.where(keep, val, 0.0).astype(o_ref.dtype)


def _overlap(tok, blk):
    c_start = tok * CMP_STRIDE
    s_start = blk * SEL_LEN
    return ((c_start < s_start + SEL_LEN) & (c_start + CMP_LEN > s_start)).astype(F32)


def _select_blocks(imp, tpos, n_sel, n_top):
    nq, W = imp.shape
    blk = lax.broadcasted_iota(jnp.int32, (nq, W), 1)
    cur = tpos // SEL_LEN
    forced = (blk == 0) | (blk == cur) | (blk == cur - 1)
    v = jnp.where(forced, jnp.inf, jnp.where(blk <= cur, imp, NEG_INF))
    v = jnp.where(blk < n_sel, v, NEG_INF)
    ahead = jnp.zeros((nq, W), F32)
    for j in range(n_sel):
        vj = v[:, j:j + 1]
        ahead = ahead + jnp.where(vj > v, 1.0, jnp.where(vj == v, jnp.where(blk > j, 1.0, 0.0), 0.0))
    return jnp.where((ahead < n_top) & (blk < n_sel), 1.0, 0.0)


def _select_blocks_t(imp_t, tpos_row, n_sel, n_top):
    n_blk, nq = imp_t.shape
    blk = lax.broadcasted_iota(jnp.int32, (n_blk, nq), 0)
    cur = tpos_row // SEL_LEN
    forced = (blk == 0) | (blk == cur) | (blk == cur - 1)
    v = jnp.where(forced, jnp.inf, jnp.where(blk <= cur, imp_t, NEG_INF))
    v = jnp.where(blk < n_sel, v, NEG_INF)
    n_rg = n_blk // SUBLANES
    vg = [v[rg * SUBLANES:(rg + 1) * SUBLANES] for rg in range(n_rg)]
    bg = [rg * SUBLANES + lax.broadcasted_iota(jnp.int32, (SUBLANES, nq), 0) for rg in range(n_rg)]
    ahead = [jnp.zeros((SUBLANES, nq), F32) for _ in range(n_rg)]
    for j in range(n_sel):
        vj = v[j:j + 1, :]
        for rg in range(n_rg):
            if rg * SUBLANES > j:
                inc = jnp.where(vj >= vg[rg], 1.0, 0.0)
            elif (rg + 1) * SUBLANES - 1 <= j:
                inc = jnp.where(vj > vg[rg], 1.0, 0.0)
            else:
                inc = jnp.where(vj > vg[rg], 1.0, jnp.where(vj == vg[rg], jnp.where(bg[rg] > j, 1.0, 0.0), 0.0))
            ahead[rg] = ahead[rg] + inc
    ahead = jnp.concatenate(ahead, axis=0)
    return jnp.where((ahead < n_top) & (blk < n_sel), 1.0, 0.0)


def _online_update(s, m_old, l_old):
    m_new = jnp.maximum(m_old, jnp.max(s, axis=-1, keepdims=True))
    alpha = jnp.exp2(m_old - m_new)
    p = jnp.exp2(s - m_new)
    return p, m_new, alpha, alpha * l_old + jnp.sum(p, axis=-1, keepdims=True)


def _cmp_tokens_kernel(xk_ref, xv_ref, bd_ref, b_ref, kc_ref, vc_ref, *, nsub):
    for c, (x_ref, o_ref) in enumerate(((xk_ref, kc_ref), (xv_ref, vc_ref))):
        a0 = jnp.zeros((nsub, LANES), F32)
        a1 = jnp.zeros((nsub, LANES), F32)
        for j in range(CMP_STRIDE):
            xj = x_ref[0, :, j, :].astype(BF16)
            a0 = a0 + _dot(xj, bd_ref[c, 0, j])
            a1 = a1 + _dot(xj, bd_ref[c, 1, j])
        tok = a0 + pltpu.roll(a1, nsub - 1, axis=0) + b_ref[c]
        o_ref[0] = tok.astype(o_ref.dtype)


def _cmp_blockdiag(cmp_w_l):
    R = CMP_LEN // CMP_STRIDE
    w = cmp_w_l.astype(F32).reshape(2, R, CMP_STRIDE, NSA_DH, NSA_DH)
    eye = jnp.eye(NSA_GROUPS, dtype=F32)
    return jnp.einsum("ab,crjde->crjadbe", eye, w).reshape(2, R, CMP_STRIDE, LANES, LANES).astype(BF16)


def _cmp_bias(cmp_b_l):
    return jnp.tile(cmp_b_l.astype(F32), (1, NSA_GROUPS)).reshape(2, 1, LANES)


def cmp_tokens(zc, cmp_w_l, cmp_b_l):
    B, T, W = zc.shape
    nsub = T // CMP_STRIDE
    x4 = zc.reshape(B, nsub, CMP_STRIDE, W)
    bd = _cmp_blockdiag(cmp_w_l)
    bias = _cmp_bias(cmp_b_l)
    spec = lambda j: pl.BlockSpec((1, nsub, CMP_STRIDE, LANES), lambda b, j=j: (b, 0, 0, j))
    return pl.pallas_call(
        functools.partial(_cmp_tokens_kernel, nsub=nsub),
        grid=(B,),
        in_specs=[spec(0), spec(1), pl.BlockSpec(bd.shape, lambda b: (0,) * 5), pl.BlockSpec(bias.shape, lambda b: (0, 0, 0))],
        out_specs=[pl.BlockSpec((1, nsub, LANES), lambda b: (b, 0, 0))] * 2,
        out_shape=[jax.ShapeDtypeStruct((B, nsub, LANES), BF16)] * 2,
        compiler_params=_params(("parallel",)),
        name="nsa_cmp_tokens",
    )(x4, x4, bd, bias)


def _nsa_prompt_kernel(q_ref, g_ref, kc_ref, vc_ref, ka_ref, vst_ref, kw_ref, vw_ref, qf_ref, o_ref,
                       qaug_s, s_buf, p_buf, m_s, l_s, a_s, acc_s, *, tq, T, CH):
    qi = pl.program_id(1)
    start = qi * tq
    ref_pos = start + tq
    n_cmp_rows = kc_ref.shape[1]
    n_sel = T // SEL_LEN
    n_top = min(SEL_TOPN, n_sel)
    n_blk = -(-n_sel // SUBLANES) * SUBLANES
    qp = _head_rows(q_ref, tq)
    tpos = start + lax.broadcasted_iota(jnp.int32, (tq, 1), 0)
    tpos_row = start + lax.broadcasted_iota(jnp.int32, (1, tq), 1)
    rows = lambda a, hd: a[hd * tq:(hd + 1) * tq]

    sc_all = _dot_nt(qp, kc_ref[0])
    c_end = lax.broadcasted_iota(jnp.int32, (1, n_cmp_rows), 1) * CMP_STRIDE + CMP_LEN - 1
    mask_c = tpos >= c_end
    rel_c = (c_end - ref_pos).astype(F32)
    pcs = [_masked_softmax2(rows(sc_all, hd) + NSA_SLOPES[hd] * rel_c, mask_c) for hd in range(NSA_HEADS)]
    o_c = _dot(jnp.concatenate(pcs, axis=0).astype(BF16), vc_ref[0])
    ov_t = _overlap(lax.broadcasted_iota(jnp.int32, (n_blk, n_cmp_rows), 1),
                    lax.broadcasted_iota(jnp.int32, (n_blk, n_cmp_rows), 0))
    lane = lax.broadcasted_iota(jnp.int32, (tq, LANES), 1)
    feats = []
    for g in range(NSA_GROUPS):
        pg = pcs[g * NSA_HPG]
        for p in range(1, NSA_HPG):
            pg = pg + pcs[g * NSA_HPG + p]
        sel_t = _select_blocks_t(_dot_nt(ov_t, pg, HIGHEST), tpos_row, n_sel, n_top)
        sel = jnp.concatenate([sel_t, jnp.zeros((LANES - n_blk, tq), F32)], axis=0).T
        unsel = jnp.where(lane < SEL_FEAT, (1.0 - sel) * MASK_BIAS, 0.0)
        feats += [(unsel + qf_ref[g * NSA_HPG + p:g * NSA_HPG + p + 1, :]).astype(BF16) for p in range(NSA_HPG)]
    qaug_s[...] = jnp.concatenate([qp, jnp.concatenate(feats, axis=0)], axis=1)

    nq_all = NSA_HEADS * tq
    qw = nq_all // SEL_QUARTERS
    tpos_all = jnp.concatenate([tpos_row] * NSA_HEADS, axis=1)
    m_s[...] = jnp.full(m_s.shape, NEG_INF, F32)
    l_s[...] = jnp.zeros(l_s.shape, F32)
    acc_s[...] = jnp.zeros(acc_s.shape, F32)

    def issue(c, qi):
        base = pl.multiple_of(c * CH, CH)
        s_buf[qi] = _dot_nt(ka_ref[0, pl.ds(base, CH), :], qaug_s[qi * qw:(qi + 1) * qw, :])

    def absorb(c, qi, diag):
        for ct in range(qw // LANES):
            cols = slice(qi * qw + ct * LANES, qi * qw + (ct + 1) * LANES)
            s_t = s_buf[qi, :, ct * LANES:(ct + 1) * LANES]
            if diag:
                kpos = c * CH + lax.broadcasted_iota(jnp.int32, (CH, 1), 0)
                s_t = jnp.where(kpos <= tpos_all[:, cols], s_t, NEG_INF)
            m_old = m_s[:, cols]
            m_new = jnp.maximum(m_old, jnp.max(s_t, axis=0, keepdims=True))
            alpha = jnp.exp2(m_old - m_new)
            p_t = jnp.exp2(s_t - m_new)
            m_s[:, cols] = m_new
            a_s[:, cols] = alpha
            l_s[:, cols] = alpha * l_s[:, cols] + jnp.sum(p_t, axis=0, keepdims=True)
            p_buf[qi, :, ct * LANES:(ct + 1) * LANES] = p_t.astype(BF16)
        cols = slice(qi * qw, (qi + 1) * qw)
        acc_s[:, cols] = a_s[:, cols] * acc_s[:, cols] + _dot(vst_ref[0, c], p_buf[qi])

    def full_chunk(c, carry):
        for qi in range(SEL_QUARTERS):
            absorb(c, qi, False)
            issue(c + 1, qi)
        return carry

    n_full = start // CH
    for qi in range(SEL_QUARTERS):
        issue(0, qi)
    lax.fori_loop(0, n_full, full_chunk, 0)
    for qi in range(SEL_QUARTERS):
        absorb(n_full, qi, True)
    o_st = acc_s[...] / jnp.maximum(l_s[...], 1e-30)
    o_s = jnp.concatenate([o_st[:, hd * tq:(hd + 1) * tq].T for hd in range(NSA_HEADS)], axis=0)

    wk = WINDOW + tq
    base_w = pl.multiple_of(start, tq)
    sw_all = _dot_nt(qp, kw_ref[0, pl.ds(base_w, wk), :])
    jw = lax.broadcasted_iota(jnp.int32, (1, wk), 1)
    iw = lax.broadcasted_iota(jnp.int32, (tq, 1), 0)
    dw = iw - jw + WINDOW
    mask_w = (dw >= 0) & (dw < WINDOW) & (start - WINDOW + jw >= 0)
    rel_w = (jw - wk).astype(F32)
    pws = [_masked_softmax2(rows(sw_all, hd) + NSA_SLOPES[hd] * rel_w, mask_w) for hd in range(NSA_HEADS)]
    o_w = _dot(jnp.concatenate(pws, axis=0).astype(BF16), vw_ref[0, pl.ds(base_w, wk), :])

    _store_heads(o_ref, jax.nn.sigmoid(g_ref[0]), o_c, o_s, o_w, tq)


def _bf16_parts(x, n):
    parts = []
    for _ in range(n):
        p = float(np.asarray(x, np.float32).astype(jnp.bfloat16).astype(np.float32))
        parts.append(p)
        x = x - p
    return parts


def _selected_features(T):
    qf = np.zeros((NSA_HEADS, LANES), np.float32)
    kf = np.zeros((T, LANES), np.float32)
    pos = np.arange(T)
    kf[pos, pos // SEL_LEN] = 1.0
    for hd in range(NSA_HEADS):
        for p, s_p in enumerate(_bf16_parts(NSA_SLOPES[hd], SLOPE_PARTS)):
            qf[hd, SEL_FEAT + 2 * p] = s_p * SEL_LEN
            qf[hd, SEL_FEAT + 2 * p + 1] = s_p
    for p in range(SLOPE_PARTS):
        kf[:, SEL_FEAT + 2 * p] = pos // SEL_LEN
        kf[:, SEL_FEAT + 2 * p + 1] = pos % SEL_LEN
    return jnp.asarray(qf), jnp.asarray(kf, BF16)


def nsa_prompt(zb, zd, zcb, kc, vc, tq, CH):
    B, T, _ = zb.shape
    assert T % CH == 0 and CH % tq == 0 and T // SEL_LEN <= SEL_FEAT
    qf, kf = _selected_features(T)
    k_aug = jnp.concatenate([zcb[:, :, 2 * NSA_KV:3 * NSA_KV], jnp.broadcast_to(kf[None], (B, T, LANES))], axis=2)
    vs_t = jnp.swapaxes(zcb[:, :, 3 * NSA_KV:4 * NSA_KV].reshape(B, T // CH, CH, NSA_KV), 2, 3)
    kw = jnp.pad(zcb[:, :, 4 * NSA_KV:5 * NSA_KV], ((0, 0), (WINDOW, 0), (0, 0)))
    vw = jnp.pad(zcb[:, :, 5 * NSA_KV:6 * NSA_KV], ((0, 0), (WINDOW, 0), (0, 0)))
    nsub = kc.shape[1]
    per_b = lambda rows_, j: pl.BlockSpec((1, rows_, LANES), lambda b, i, j=j: (b, 0, j))
    kern = functools.partial(_nsa_prompt_kernel, tq=tq, T=T, CH=CH)
    nq_all = NSA_HEADS * tq
    assert (nq_all // SEL_QUARTERS) % LANES == 0
    return pl.pallas_call(
        kern,
        grid=(B, T // tq),
        in_specs=[pl.BlockSpec((1, tq, NSA_QPAD), lambda b, i: (b, i, 0)),
                  pl.BlockSpec((1, tq, GATE_COLS), lambda b, i: (b, i, 0)),
                  per_b(nsub, 0), per_b(nsub, 0),
                  pl.BlockSpec((1, T, 2 * LANES), lambda b, i: (b, 0, 0)),
                  pl.BlockSpec((1, T // CH, NSA_KV, CH), lambda b, i: (b, 0, 0, 0)),
                  per_b(T + WINDOW, 0), per_b(T + WINDOW, 0),
                  pl.BlockSpec(qf.shape, lambda b, i: (0, 0))],
        out_specs=pl.BlockSpec((1, tq, NSA_QPAD), lambda b, i: (b, i, 0)),
        out_shape=jax.ShapeDtypeStruct((B, T, NSA_QPAD), BF16),
        scratch_shapes=[pltpu.VMEM((nq_all, 2 * LANES), BF16),
                        pltpu.VMEM((SEL_QUARTERS, CH, nq_all // SEL_QUARTERS), F32),
                        pltpu.VMEM((SEL_QUARTERS, CH, nq_all // SEL_QUARTERS), BF16),
                        pltpu.VMEM((1, nq_all), F32), pltpu.VMEM((1, nq_all), F32), pltpu.VMEM((1, nq_all), F32),
                        pltpu.VMEM((NSA_KV, nq_all), F32)],
        compiler_params=_params(("parallel", "arbitrary")),
        name="nsa_prompt",
    )(zb, zd, kc, vc, k_aug, vs_t, kw, vw, qf)


def _page_specs(shape, slot_blk, layer, n_pages):
    def mk(i):
        def imap(b, s, pt):
            return (pt[b * n_pages + s * PAGES_PER_STEP + i], layer, slot_blk, 0, 0)
        return pl.BlockSpec(shape, imap)
    return [mk(i) for i in range(PAGES_PER_STEP)]


def _nsa_dec_cmp_kernel(pt_ref, *refs, T, past_len, n_rows):
    pages = refs[:PAGES_PER_STEP]
    perm_ref, bd_ref, b_ref, q_ref, oc_ref, sel_ref, tok_s, pend_s = refs[PAGES_PER_STEP:]
    s = pl.program_id(1)
    n_steps = pl.num_programs(1)
    page_rows = pages[0].shape[-1]
    sub = page_rows // CMP_STRIDE
    R = PAGES_PER_STEP * sub

    @pl.when(s == 0)
    def _():
        pend_s[...] = jnp.zeros(pend_s.shape, F32)

    rid = lax.broadcasted_iota(jnp.int32, (R, LANES), 0)
    for c in range(2):
        xs = [_dot_nt(perm_ref[...], pg[0, 0, c].astype(BF16)) for pg in pages]
        a0 = jnp.zeros((R, LANES), F32)
        a1 = jnp.zeros((R, LANES), F32)
        for j in range(CMP_STRIDE):
            xj = jnp.concatenate([x[j * sub:(j + 1) * sub] for x in xs], axis=0).astype(BF16)
            a0 = a0 + _dot(xj, bd_ref[c, 0, j])
            a1 = a1 + _dot(xj, bd_ref[c, 1, j])
        a0 = a0 + b_ref[c]
        tok = jnp.where(rid == 0, pend_s[c], pltpu.roll(a0, 1, axis=0)) + a1
        tok_s[c, pl.ds(pl.multiple_of(s * R, R), R), :] = tok
        pend_s[c] = a0[R - 1:R, :]

    @pl.when(s == n_steps - 1)
    def _():
        qp = _head_rows(q_ref, T)
        kc = tok_s[0].astype(BF16)
        vc = tok_s[1].astype(BF16)
        sc_all = _dot_nt(qp, kc)
        tpos = past_len + lax.broadcasted_iota(jnp.int32, (T, 1), 0)
        r = lax.broadcasted_iota(jnp.int32, (T, n_rows), 1)
        dc = tpos - ((r - 1) * CMP_STRIDE + CMP_LEN - 1)
        mask_c = (dc >= 0) & (r >= 1)
        dcf = dc.astype(F32)
        pcs = [_masked_softmax2(sc_all[hd * T:(hd + 1) * T] - NSA_SLOPES[hd] * dcf, mask_c) for hd in range(NSA_HEADS)]
        oc_ref[0] = _dot(jnp.concatenate(pcs, axis=0).astype(BF16), vc)
        W = sel_ref.shape[2]
        n_sel = -(-(past_len + T) // SEL_LEN)
        ov = _overlap(lax.broadcasted_iota(jnp.int32, (n_rows, W), 0) - 1, lax.broadcasted_iota(jnp.int32, (n_rows, W), 1))
        for g in range(NSA_GROUPS):
            pg = pcs[g * NSA_HPG]
            for p in range(1, NSA_HPG):
                pg = pg + pcs[g * NSA_HPG + p]
            imp = _dot(pg, ov, HIGHEST)
            sel_ref[0, g * T:(g + 1) * T, :] = _select_blocks(imp, tpos, n_sel, min(SEL_TOPN, n_sel))


def _nsa_dec_sel_kernel(pt_ref, *refs, T, past_len):
    pages = refs[:PAGES_PER_STEP]
    q_ref, sel_ref, sel_last_ref, ex_ref, kn_ref, vn_ref, os_ref, m_s, l_s, acc_s = refs[PAGES_PER_STEP:]
    s = pl.program_id(1)
    n_steps = pl.num_programs(1)
    page_rows = pages[0].shape[-1]
    CH = PAGES_PER_STEP * page_rows

    @pl.when(s == 0)
    def _():
        m_s[...] = jnp.full(m_s.shape, NEG_INF, F32)
        l_s[...] = jnp.zeros(l_s.shape, F32)
        acc_s[...] = jnp.zeros(acc_s.shape, F32)

    qp = _head_rows(q_ref, T)
    tpos = past_len + lax.broadcasted_iota(jnp.int32, (T, 1), 0)

    def update(s_all, mask_of_group, kpos, pv):
        ds = tpos - kpos
        dsf = ds.astype(F32)
        ps = []
        for g in range(NSA_GROUPS):
            mk = mask_of_group(g) & (ds >= 0)
            for p in range(NSA_HPG):
                hd = g * NSA_HPG + p
                r = slice(hd * T, (hd + 1) * T)
                sc = jnp.where(mk, s_all[r] - NSA_SLOPES[hd] * dsf, NEG_INF)
                pr, m_new, alpha, l_new = _online_update(sc, m_s[r], l_s[r])
                m_s[r] = m_new
                l_s[r] = l_new
                acc_s[r] = alpha * acc_s[r]
                ps.append(pr)
        acc_s[...] = acc_s[...] + pv(jnp.concatenate(ps, axis=0).astype(BF16))

    selk = _dot(sel_ref[0, 0].astype(BF16), ex_ref[...])
    s_all = jnp.concatenate([_dot(qp, pg[0, 0, 0].astype(BF16)) for pg in pages], axis=1)
    kpos = s * CH + lax.broadcasted_iota(jnp.int32, (T, CH), 1)

    def pv_pages(pmat):
        out = jnp.zeros((NSA_HEADS * T, LANES), F32)
        for i, pg in enumerate(pages):
            out = out + _dot_nt(pmat[:, i * page_rows:(i + 1) * page_rows], pg[0, 0, 1].astype(BF16))
        return out

    update(s_all, lambda g: selk[g * T:(g + 1) * T] > 0.5, kpos, pv_pages)

    @pl.when(s == n_steps - 1)
    def _():
        nk = kn_ref.shape[1]
        sn = _dot_nt(qp, kn_ref[0])
        lane = lax.broadcasted_iota(jnp.int32, (T, nk), 1)
        sl = sel_last_ref[0, 0]
        update(sn, lambda g: (sl[g * T:(g + 1) * T, 0:1] > 0.5) & (lane < T), past_len + lane,
               lambda pmat: _dot(pmat, vn_ref[0]))
        os_ref[0] = acc_s[...] / jnp.maximum(l_s[...], 1e-30)


def _nsa_dec_win_kernel(q_ref, g_ref, oc_ref, os_ref, wp_ref, kwn_ref, vwn_ref, o_ref, *, T, past_len):
    qp = _head_rows(q_ref, T)
    w_src = wp_ref.shape[-1]
    nk = kwn_ref.shape[1]
    sw_all = jnp.concatenate([_dot(qp, wp_ref[0, 0, 0].astype(BF16)), _dot_nt(qp, kwn_ref[0])], axis=1)
    j = lax.broadcasted_iota(jnp.int32, (T, w_src + nk), 1)
    tpos = past_len + lax.broadcasted_iota(jnp.int32, (T, 1), 0)
    wpos = past_len - w_src + j
    dw = tpos - wpos
    mask_w = (dw >= 0) & (dw < WINDOW) & (wpos >= 0) & (j < w_src + T)
    dwf = dw.astype(F32)
    pws = [_masked_softmax2(sw_all[hd * T:(hd + 1) * T] - NSA_SLOPES[hd] * dwf, mask_w) for hd in range(NSA_HEADS)]
    pw = jnp.concatenate(pws, axis=0).astype(BF16)
    o_w = _dot_nt(pw[:, :w_src], wp_ref[0, 0, 1].astype(BF16)) + _dot(pw[:, w_src:], vwn_ref[0])
    _store_heads(o_ref, jax.nn.sigmoid(g_ref[0]), oc_ref[0], os_ref[0], o_w, T)


def nsa_decode(zb, zd, zcb, cache_nsa_kv, cache_win_kv, page_table, cmp_w_l, cmp_b_l, layer):
    DB, T, _ = zb.shape
    n_pool, page_rows = cache_nsa_kv.shape[:2]
    n_pages = page_table.shape[1]
    past_len = n_pages * page_rows
    n_steps = n_pages // PAGES_PER_STEP
    sub_per_page = page_rows // CMP_STRIDE
    n_rows = n_pages * sub_per_page
    assert (n_rows - 1) * CMP_STRIDE + CMP_LEN - 1 > past_len + T - 1
    assert T <= CMP_STRIDE and n_pages % PAGES_PER_STEP == 0 and page_rows == LANES
    pt = page_table.reshape(-1).astype(jnp.int32)
    cache_t = jnp.transpose(cache_nsa_kv, (0, 2, 3, 4, 5, 1)).reshape(n_pool, DEPTH, 4, NSA_KV, page_rows)
    page_blk = (1, 1, 2, NSA_KV, page_rows)

    rr = np.arange(page_rows)
    perm = jnp.asarray(rr[None, :] == ((rr % sub_per_page) * CMP_STRIDE + rr // sub_per_page)[:, None], BF16)
    bd = _cmp_blockdiag(cmp_w_l)
    bias = _cmp_bias(cmp_b_l)

    n_sel = -(-(past_len + T) // SEL_LEN)
    blocks_per_step = PAGES_PER_STEP * page_rows // SEL_LEN
    sel_used = (n_steps + 1) * blocks_per_step
    sel_w = -(-sel_used // LANES) * LANES
    assert sel_used >= n_sel and blocks_per_step <= LANES
    q_spec = pl.BlockSpec((1, T, NSA_QPAD), lambda b, s, pt: (b, 0, 0))
    const = lambda a: pl.BlockSpec(a.shape, lambda b, s, pt: (0,) * a.ndim)

    o_c, sel = pl.pallas_call(
        functools.partial(_nsa_dec_cmp_kernel, T=T, past_len=past_len, n_rows=n_rows),
        grid_spec=pltpu.PrefetchScalarGridSpec(
            num_scalar_prefetch=1, grid=(DB, n_steps),
            in_specs=_page_specs(page_blk, 0, layer, n_pages) + [const(perm), const(bd), const(bias), q_spec],
            out_specs=[pl.BlockSpec((1, NSA_HEADS * T, LANES), lambda b, s, pt: (b, 0, 0)),
                       pl.BlockSpec((1, NSA_GROUPS * T, sel_w), lambda b, s, pt: (b, 0, 0))],
            scratch_shapes=[pltpu.VMEM((2, n_rows, LANES), F32), pltpu.VMEM((2, 1, LANES), F32)]),
        out_shape=[jax.ShapeDtypeStruct((DB, NSA_HEADS * T, LANES), F32),
                   jax.ShapeDtypeStruct((DB, NSA_GROUPS * T, sel_w), F32)],
        compiler_params=_params(("parallel", "arbitrary")),
        name="nsa_dec_cmp",
    )(pt, *([cache_t] * PAGES_PER_STEP), perm, bd, bias, zb)

    sel_steps = sel[:, :, :sel_used].reshape(DB, NSA_GROUPS * T, n_steps + 1, blocks_per_step).transpose(0, 2, 1, 3)
    sel_steps = jnp.pad(sel_steps, ((0, 0), (0, 0), (0, 0), (0, LANES - blocks_per_step)))
    kk = np.arange(PAGES_PER_STEP * page_rows) // SEL_LEN
    expand = jnp.asarray(kk[None, :] == np.arange(LANES)[:, None], BF16)
    pad_rows = LANES - T
    new_rows = lambda slot: jnp.pad(zcb[:, :, slot * NSA_KV:(slot + 1) * NSA_KV], ((0, 0), (0, pad_rows), (0, 0)))
    new_spec = pl.BlockSpec((1, LANES, LANES), lambda b, s, pt: (b, 0, 0))
    sel_blk = (1, 1, NSA_GROUPS * T, LANES)

    o_s = pl.pallas_call(
        functools.partial(_nsa_dec_sel_kernel, T=T, past_len=past_len),
        grid_spec=pltpu.PrefetchScalarGridSpec(
            num_scalar_prefetch=1, grid=(DB, n_steps),
            in_specs=_page_specs(page_blk, 1, layer, n_pages)
            + [q_spec, pl.BlockSpec(sel_blk, lambda b, s, pt: (b, s, 0, 0)),
               pl.BlockSpec(sel_blk, lambda b, s, pt: (b, n_steps, 0, 0)), const(expand), new_spec, new_spec],
            out_specs=pl.BlockSpec((1, NSA_HEADS * T, LANES), lambda b, s, pt: (b, 0, 0)),
            scratch_shapes=[pltpu.VMEM((NSA_HEADS * T, 1), F32), pltpu.VMEM((NSA_HEADS * T, 1), F32),
                            pltpu.VMEM((NSA_HEADS * T, LANES), F32)]),
        out_shape=jax.ShapeDtypeStruct((DB, NSA_HEADS * T, LANES), F32),
        compiler_params=_params(("parallel", "arbitrary")),
        name="nsa_dec_sel",
    )(pt, *([cache_t] * PAGES_PER_STEP), zb, sel_steps, sel_steps, expand, new_rows(2), new_rows(3))

    w_src = cache_win_kv.shape[1]
    win_t = jnp.transpose(cache_win_kv, (0, 2, 3, 4, 5, 1)).reshape(DB, DEPTH, 2, NSA_KV, w_src)
    b3 = lambda shp: pl.BlockSpec(shp, lambda b: (b, 0, 0))
    return pl.pallas_call(
        functools.partial(_nsa_dec_win_kernel, T=T, past_len=past_len),
        grid=(DB,),
        in_specs=[b3((1, T, NSA_QPAD)), b3((1, T, GATE_COLS)), b3((1, NSA_HEADS * T, LANES)), b3((1, NSA_HEADS * T, LANES)),
                  pl.BlockSpec((1, 1, 2, NSA_KV, w_src), lambda b: (b, layer, 0, 0, 0)),
                  b3((1, LANES, LANES)), b3((1, LANES, LANES))],
        out_specs=b3((1, T, NSA_QPAD)),
        out_shape=jax.ShapeDtypeStruct((DB, T, NSA_QPAD), BF16),
        compiler_params=_params(("parallel",)),
        name="nsa_dec_win",
    )(zb, zd, o_c, o_s, win_t, new_rows(4), new_rows(5))


def _merge_kernel(x_ref, hm_ref, hn_ref, hx_ref, wg_ref, bg_ref, wbm_ref, wbn_ref, wbx_ref, wo_ref, g_ref, b_ref, o_ref):
    x = x_ref[...]
    xb = x.astype(BF16)
    merged = None
    for c, (h_ref, w_ref) in enumerate(((hm_ref, wbm_ref), (hn_ref, wbn_ref), (hx_ref, wbx_ref))):
        sl = slice(c * D_MODEL, (c + 1) * D_MODEL)
        gate = jax.nn.sigmoid(_dot(xb, wg_ref[:, sl]) + bg_ref[:, sl])
        term = gate * _dot(h_ref[...], w_ref[...])
        merged = term if merged is None else merged + term
    mix = _dot(merged.astype(BF16), wo_ref[...])
    o_ref[...] = _layer_norm(ALPHA * x + mix, g_ref[...], b_ref[...])


def merge(x2d, hm, hn, hx, wg, bg, wbr, wo, g, b, tm):
    n = x2d.shape[0]
    row = lambda a: pl.BlockSpec((tm, a.shape[1]), lambda i: (i, 0))
    full = lambda a: pl.BlockSpec(a.shape, lambda i: (0,) * a.ndim)
    return pl.pallas_call(
        _merge_kernel,
        grid=(n // tm,),
        in_specs=[row(x2d), row(hm), row(hn), row(hx), full(wg), full(bg), full(wbr[0]), full(wbr[1]), full(wbr[2]),
                  full(wo), full(g), full(b)],
        out_specs=row(x2d),
        out_shape=jax.ShapeDtypeStruct((n, D_MODEL), F32),
        compiler_params=_params(("parallel",)),
        name="merge",
    )(x2d, hm, hn, hx, wg, bg, wbr[0], wbr[1], wbr[2], wo, g, b)


def _ffn_kernel(x_ref, wg_ref, wu_ref, wd_ref, g_ref, b_ref, o_ref, acc_s):
    f = pl.program_id(1)

    @pl.when(f == 0)
    def _():
        acc_s[...] = jnp.zeros(acc_s.shape, F32)

    xb = x_ref[...].astype(BF16)
    hcur = jax.nn.silu(_dot(xb, wg_ref[...])) * _dot(xb, wu_ref[...])
    acc_s[...] += _dot(hcur.astype(BF16), wd_ref[...])

    @pl.when(f == pl.num_programs(1) - 1)
    def _():
        o_ref[...] = _layer_norm(ALPHA * x_ref[...] + acc_s[...], g_ref[...], b_ref[...])


def ffn(x2d, w_up, w_down, g, b, tm, tf):
    n = x2d.shape[0]
    nf = (w_up.shape[1] // 2) // tf
    return pl.pallas_call(
        _ffn_kernel,
        grid=(n // tm, nf),
        in_specs=[pl.BlockSpec((tm, D_MODEL), lambda i, f: (i, 0)),
                  pl.BlockSpec((D_MODEL, tf), lambda i, f: (0, f)),
                  pl.BlockSpec((D_MODEL, tf), lambda i, f: (0, nf + f)),
                  pl.BlockSpec((tf, D_MODEL), lambda i, f: (f, 0)),
                  pl.BlockSpec((1, D_MODEL), lambda i, f: (0, 0)),
                  pl.BlockSpec((1, D_MODEL), lambda i, f: (0, 0))],
        out_specs=pl.BlockSpec((tm, D_MODEL), lambda i, f: (i, 0)),
        out_shape=jax.ShapeDtypeStruct((n, D_MODEL), F32),
        scratch_shapes=[pltpu.VMEM((tm, D_MODEL), F32)],
        compiler_params=_params(("parallel", "arbitrary")),
        name="dense_ffn",
    )(x2d, w_up, w_up, w_down, g, b)


def _router_kernel(x_ref, w_ref, b_ref, o_ref):
    logits = _dot(x_ref[...], w_ref[...], HIGHEST) + b_ref[...]
    lane = lax.broadcasted_iota(jnp.int32, logits.shape, 1)
    W = logits.shape[1]
    logits = jnp.where(lane < N_EXPERTS, logits, NEG_INF)
    m1 = jnp.max(logits, axis=-1, keepdims=True)
    i1 = jnp.min(jnp.where(logits == m1, lane, W), axis=-1, keepdims=True)
    rest = jnp.where(lane == i1, NEG_INF, logits)
    m2 = jnp.max(rest, axis=-1, keepdims=True)
    i2 = jnp.min(jnp.where(rest == m2, lane, W), axis=-1, keepdims=True)
    e2 = jnp.exp(m2 - m1)
    den = 1.0 + e2
    o_ref[...] = jnp.where(lane == i1, 1.0 / den, 0.0) + jnp.where(lane == i2, e2 / den, 0.0)


def router(x2d, w_router, b_router, tm):
    n = x2d.shape[0]
    w = jnp.pad(w_router.astype(F32), ((0, 0), (0, LANES - N_EXPERTS)))
    bb = jnp.pad(b_router.astype(F32), (0, LANES - N_EXPERTS)).reshape(1, LANES)
    return pl.pallas_call(
        _router_kernel,
        grid=(n // tm,),
        in_specs=[pl.BlockSpec((tm, D_MODEL), lambda i: (i, 0)), pl.BlockSpec(w.shape, lambda i: (0, 0)),
                  pl.BlockSpec(bb.shape, lambda i: (0, 0))],
        out_specs=pl.BlockSpec((tm, LANES), lambda i: (i, 0)),
        out_shape=jax.ShapeDtypeStruct((n, LANES), F32),
        compiler_params=_params(("parallel",)),
        name="router",
    )(x2d, w, bb)


def _moe_kernel(np_ref, x_ref, cw_ref, cwt_ref, wg_ref, wu_ref, wd_ref, g_ref, b_ref, o_ref,
                xb_s, y_s, pos_s, post_s, xg_s, ws_s, acc_s, *, blk, cap, cap_pad):
    j = pl.program_id(0)
    e = pl.program_id(1)
    f = pl.program_id(2)
    n_e = pl.num_programs(1)
    n_f = pl.num_programs(2)
    n_pass = np_ref[j * n_e + e]
    RC = min(256, blk)

    @pl.when((e == 0) & (f == 0))
    def _():
        xb_s[...] = x_ref[...].astype(BF16)
        y_s[...] = jnp.zeros(y_s.shape, F32)
        cw = cw_ref[...]
        cwt = cwt_ref[...]
        routed = jnp.where(cw != 0.0, 1.0, 0.0).astype(BF16)
        routed_t = jnp.where(cwt != 0.0, 1.0, 0.0).astype(BF16)
        for rc in range(blk // RC):
            rows_i = rc * RC + lax.broadcasted_iota(jnp.int32, (RC, blk), 0)
            cols_i = lax.broadcasted_iota(jnp.int32, (RC, blk), 1)
            before = jnp.where(cols_i < rows_i, 1.0, 0.0).astype(BF16)
            cnt = _dot(before, routed)
            pos_s[rc * RC:(rc + 1) * RC, :] = jnp.where(cw[rc * RC:(rc + 1) * RC] != 0.0, cnt, -1.0)
            rows_j = lax.broadcasted_iota(jnp.int32, (blk, RC), 0)
            cols_j = rc * RC + lax.broadcasted_iota(jnp.int32, (blk, RC), 1)
            before_t = jnp.where(rows_j < cols_j, 1.0, 0.0).astype(BF16)
            cnt_t = _dot(routed_t, before_t)
            post_s[:, rc * RC:(rc + 1) * RC] = jnp.where(cwt[:, rc * RC:(rc + 1) * RC] != 0.0, cnt_t, -1.0)

    @pl.when(f == 0)
    def _():
        prow = post_s[pl.ds(e, 1), :]
        wrow = cwt_ref[pl.ds(e, 1), :]

        def gather(u, carry):
            rr = lax.broadcasted_iota(jnp.int32, (cap_pad, 1), 0)
            slot = jnp.where(rr < cap, u * cap + rr, NO_SLOT).astype(F32)
            hit = prow == slot
            xg_s[u] = _dot(jnp.where(hit, 1.0, 0.0).astype(BF16), xb_s[...]).astype(BF16)
            ws_s[u] = jnp.sum(jnp.where(hit, wrow, 0.0), axis=-1, keepdims=True)
            acc_s[u] = jnp.zeros(acc_s.shape[1:], F32)
            return carry

        lax.fori_loop(0, n_pass, gather, 0)

    def expert(u, carry):
        xg = xg_s[u, 0:cap, :]
        hcur = jax.nn.silu(_dot(xg, wg_ref[0])) * _dot(xg, wu_ref[0]) * ws_s[u, 0:cap, :]
        acc_s[u, 0:cap, :] += _dot(hcur.astype(BF16), wd_ref[0])
        return carry

    lax.fori_loop(0, n_pass, expert, 0)

    @pl.when(f == n_f - 1)
    def _():
        lane = lax.broadcasted_iota(jnp.int32, pos_s.shape, 1)
        pcol = jnp.sum(jnp.where(lane == e, pos_s[...], 0.0), axis=-1, keepdims=True)

        def scatter(u, carry):
            cc = lax.broadcasted_iota(jnp.int32, (1, cap_pad), 1)
            slot = jnp.where(cc < cap, u * cap + cc, NO_SLOT).astype(F32)
            hit = jnp.where(pcol == slot, 1.0, 0.0).astype(BF16)
            y_s[...] += _dot(hit, acc_s[u].astype(BF16))
            return carry

        lax.fori_loop(0, n_pass, scatter, 0)

    @pl.when((e == n_e - 1) & (f == n_f - 1))
    def _():
        o_ref[...] = _layer_norm(ALPHA * x_ref[...] + y_s[...], g_ref[...], b_ref[...])


def moe(x2d, cw, w_up, w_down, g, b, tf):
    n = x2d.shape[0]
    E, _, F2 = w_up.shape
    nf = (F2 // 2) // tf
    blk = _row_tile(n, MOE_BLOCK)
    cap = min(MOE_CAP, blk)
    cap_pad = -(-cap // LANES) * LANES
    max_pass = -(-blk // cap)
    nblk = n // blk
    cwt = jnp.swapaxes(cw[:, :ROUTE_ROWS], 0, 1)
    counts = jnp.sum((cw[:, :E] != 0.0).reshape(nblk, blk, E), axis=1)
    n_pass = ((counts + cap - 1) // cap).astype(jnp.int32).reshape(-1)
    return pl.pallas_call(
        functools.partial(_moe_kernel, blk=blk, cap=cap, cap_pad=cap_pad),
        grid_spec=pltpu.PrefetchScalarGridSpec(
            num_scalar_prefetch=1, grid=(nblk, E, nf),
            in_specs=[pl.BlockSpec((blk, D_MODEL), lambda j, e, f, npr: (j, 0)),
                      pl.BlockSpec((blk, LANES), lambda j, e, f, npr: (j, 0)),
                      pl.BlockSpec((ROUTE_ROWS, blk), lambda j, e, f, npr: (0, j)),
                      pl.BlockSpec((1, D_MODEL, tf), lambda j, e, f, npr: (e, 0, f)),
                      pl.BlockSpec((1, D_MODEL, tf), lambda j, e, f, npr: (e, 0, nf + f)),
                      pl.BlockSpec((1, tf, D_MODEL), lambda j, e, f, npr: (e, f, 0)),
                      pl.BlockSpec((1, D_MODEL), lambda j, e, f, npr: (0, 0)),
                      pl.BlockSpec((1, D_MODEL), lambda j, e, f, npr: (0, 0))],
            out_specs=pl.BlockSpec((blk, D_MODEL), lambda j, e, f, npr: (j, 0)),
            scratch_shapes=[pltpu.VMEM((blk, D_MODEL), BF16), pltpu.VMEM((blk, D_MODEL), F32),
                            pltpu.VMEM((blk, LANES), F32), pltpu.VMEM((ROUTE_ROWS, blk), F32),
                            pltpu.VMEM((max_pass, cap_pad, D_MODEL), BF16), pltpu.VMEM((max_pass, cap_pad, 1), F32),
                            pltpu.VMEM((max_pass, cap_pad, D_MODEL), F32)]),
        out_shape=jax.ShapeDtypeStruct((n, D_MODEL), F32),
        compiler_params=_params(("parallel", "arbitrary", "arbitrary")),
        name="moe_ffn",
    )(n_pass, x2d, cw, cwt, w_up, w_up, w_down, g, b)


def _layer(x, lw, l, *, mem_kv, mem_blk, mstate, decode):
    B, T, _ = x.shape
    n = B * T
    x2d = x.reshape(n, D_MODEL)
    tm = _row_tile(n, 512)
    za, zb, zc, zcb, zd = in_proj(x2d, lw["in"], tm)
    za, zb, zc, zcb, zd = (a.reshape(B, T, -1) for a in (za, zb, zc, zcb, zd))

    if T % MLSTM_CHUNK == 0:
        h_m, ct, nt, mt = mlstm(za, zd, *mstate, lw["norm_g"], L=MLSTM_CHUNK, t_valid=MLSTM_CHUNK)
    else:
        padt = lambda a: jnp.pad(a, ((0, 0), (0, LANES - T), (0, 0)))
        h_m, ct, nt, mt = mlstm(padt(za), padt(zd), *mstate, lw["norm_g"], L=LANES, t_valid=T)
        h_m = h_m[:, :T]

    if decode is None:
        kc, vc = cmp_tokens(zc, lw["cmp_w"], lw["cmp_b"])
        h_n = nsa_prompt(zb, zd, zcb, kc, vc, tq=min(256, T), CH=min(512, T))
    else:
        h_n = nsa_decode(zb, zd, zcb, decode["nsa"], decode["win"], decode["pt"], lw["cmp_w"], lw["cmp_b"], l)

    h_x = mem_attn(zb, mem_kv, mem_blk[0], mem_blk[1], tq=_row_tile(T, 512))

    wg, bg = lw["in"]["mg"]
    flat = lambda a: a.reshape(n, -1)
    x1 = merge(x2d, flat(h_m), flat(h_n), flat(h_x), wg, bg, lw["w_branch"], lw["w_out"], lw["ln1_g"], lw["ln1_b"], tm)

    if l % 2 == 0:
        x2 = ffn(x1, lw["ffn_up"], lw["ffn_down"], lw["ln2_g"], lw["ln2_b"], _row_tile(n, 1024), 256)
    else:
        cw = router(x1, lw["w_router"], lw["b_router"], _row_tile(n, 1024))
        x2 = moe(x1, cw, lw["ffn_up"], lw["ffn_down"], lw["ln2_g"], lw["ln2_b"], 896)
    return x2.reshape(B, T, D_MODEL), zc, (ct, nt, mt)


def kernel(x_prompt, x_sample, mem_prompt, cache_nsa_kv, cache_win_kv, state_mlstm_C, state_mlstm_n,
           state_mlstm_m, cache_mem_kv, page_table, w_in, b_in, mlstm_norm_g, cmp_w, cmp_b, w_mem_kv,
           w_branch, w_out, ln1_g, ln1_b, ln2_g, ln2_b, ffn_w_up, ffn_w_down, moe_w_router, moe_b_router,
           moe_w_up, moe_w_down):
    B, T, _ = x_prompt.shape
    DB, TS, _ = x_sample.shape
    xp, xs = x_prompt, x_sample
    nsa_p, nsa_s, win_p, win_s = [], [], [], []
    Cp, np_, mp, Cs, ns, ms, memkv_p = [], [], [], [], [], [], []
    mem_s = cache_mem_kv.reshape(DB, MEM_LEN, DEPTH * 2 * X_WIDTH)
    row1 = lambda a: a.reshape(1, -1).astype(F32)
    for l in range(DEPTH):
        wbr = (w_branch[l, 0].astype(BF16), _pad_branch_rows(w_branch[l, 1]).astype(BF16), w_branch[l, 2].astype(BF16))
        lw = dict(norm_g=mlstm_norm_g[l], cmp_w=cmp_w[l], cmp_b=cmp_b[l], w_branch=wbr,
                  w_out=w_out[l].astype(BF16), ln1_g=row1(ln1_g[l]), ln1_b=row1(ln1_b[l]),
                  ln2_g=row1(ln2_g[l]), ln2_b=row1(ln2_b[l]))
        lw["in"] = _split_in_proj(w_in[l], b_in[l])
        if l % 2 == 0:
            lw["ffn_up"] = ffn_w_up[l // 2].astype(BF16)
            lw["ffn_down"] = ffn_w_down[l // 2].astype(BF16)
        else:
            lw["ffn_up"] = moe_w_up[l // 2].astype(BF16)
            lw["ffn_down"] = moe_w_down[l // 2].astype(BF16)
            lw["w_router"] = moe_w_router[l // 2]
            lw["b_router"] = moe_b_router[l // 2]

        mkv = proj(mem_prompt.reshape(B * MEM_LEN, D_MODEL), w_mem_kv[l].astype(BF16), _row_tile(B * MEM_LEN, 512))
        mkv = mkv.reshape(B, MEM_LEN, 2 * X_WIDTH)
        st0 = (jnp.zeros((B, M_HEADS, M_DH, M_DH), F32), jnp.zeros((B, M_HEADS, M_DH), F32), jnp.zeros((B, M_HEADS), F32))
        xp, zc, st = _layer(xp, lw, l, mem_kv=mkv, mem_blk=(0, 1), mstate=st0, decode=None)
        kvn = zc.reshape(B, T, 6, NSA_GROUPS, NSA_DH)
        nsa_p.append(kvn[:, :, :4])
        win_p.append(kvn[:, T - min(WINDOW, T):, 4:])
        Cp.append(st[0]); np_.append(st[1]); mp.append(st[2])
        memkv_p.append(mkv.reshape(B, MEM_LEN, 2, X_HEADS, X_DH))

        sts = (state_mlstm_C[:, l], state_mlstm_n[:, l], state_mlstm_m[:, l])
        dec = dict(nsa=cache_nsa_kv, win=cache_win_kv, pt=page_table)
        xs, zc, st = _layer(xs, lw, l, mem_kv=mem_s, mem_blk=(2 * l, 2 * l + 1), mstate=sts, decode=dec)
        kvn = zc.reshape(DB, TS, 6, NSA_GROUPS, NSA_DH)
        nsa_s.append(kvn[:, :, :4])
        win_s.append(jnp.concatenate([cache_win_kv[:, :, l].astype(F32), kvn[:, :, 4:]], axis=1)[:, TS:])
        Cs.append(st[0]); ns.append(st[1]); ms.append(st[2])
    return (xp, xs,
            jnp.stack(nsa_p, axis=2), jnp.stack(nsa_s, axis=2),
            jnp.stack(win_p, axis=2), jnp.stack(win_s, axis=2),
            jnp.stack(Cp, axis=1), jnp.stack(np_, axis=1), jnp.stack(mp, axis=1),
            jnp.stack(Cs, axis=1), jnp.stack(ns, axis=1), jnp.stack(ms, axis=1),
            jnp.stack(memkv_p, axis=2))
```

```python
import functools

import numpy as np
import jax
import jax.numpy as jnp
from jax import lax
from jax.experimental import pallas as pl
from jax.experimental.pallas import tpu as pltpu

D_MODEL = 1024
DEPTH = 2
BRANCH_WIDTH = 512
N_BRANCH = 3
M_HEADS = 4
M_DH = BRANCH_WIDTH // M_HEADS
M_WIDTH = M_HEADS * M_DH
NSA_HEADS = 8
NSA_DH = BRANCH_WIDTH // NSA_HEADS
NSA_WIDTH = NSA_HEADS * NSA_DH
NSA_GROUPS = 2
NSA_HPG = NSA_HEADS // NSA_GROUPS
NSA_KV = NSA_GROUPS * NSA_DH
CMP_LEN = 32
CMP_STRIDE = 16
SEL_LEN = 64
SEL_TOPN = 16
WINDOW = 512
MEM_LEN = 256
X_HEADS = 4
X_DH = BRANCH_WIDTH // X_HEADS
X_WIDTH = X_HEADS * X_DH
D_FF = 2816
N_EXPERTS = 8
TOP_K = 2
D_FF_EXPERT = 3584
ALPHA = (2.0 * DEPTH) ** 0.25
LN_EPS = 1e-5
IN_SPLITS = (M_WIDTH, M_WIDTH, M_WIDTH, M_HEADS, M_HEADS, M_WIDTH,
             NSA_WIDTH, 6 * NSA_KV, 3 * NSA_HEADS, X_WIDTH, N_BRANCH * D_MODEL)
LOG2E = 1.4426950408889634
NSA_SLOPES = tuple(LOG2E * 2.0 ** (-8.0 * (h + 1) / NSA_HEADS) for h in range(NSA_HEADS))
MASK_BIAS = -(2.0 ** 30)
SEL_FEAT = 64
SEL_QUARTERS = 4
SLOPE_PARTS = 3

LANES = 128
SUBLANES = 8
VMEM_LIMIT = 56 * 1024 * 1024
PAGES_PER_STEP = 32
MLSTM_CHUNK = 256
GATE_COLS = LANES
NG_OFF = 2 * M_HEADS
NSA_QPAD = NSA_HEADS * LANES
MOE_BLOCK = 1024
MOE_CAP = 320
NO_SLOT = -2
ROUTE_ROWS = 16

F32 = jnp.float32
BF16 = jnp.bfloat16
NEG_INF = float("-inf")
HIGHEST = lax.Precision.HIGHEST


def _dot(a, b, precision=None):
    return jnp.dot(a, b, preferred_element_type=F32, precision=precision)


def _dot_nt(a, b, precision=None):
    return lax.dot_general(a, b, (((1,), (1,)), ((), ())), preferred_element_type=F32, precision=precision)


def _dot_tn(a, b):
    return lax.dot_general(a, b, (((0,), (0,)), ((), ())), preferred_element_type=F32)


def _params(sem):
    return pltpu.CompilerParams(dimension_semantics=sem, vmem_limit_bytes=VMEM_LIMIT)


def _masked_softmax2(s, mask):
    s = jnp.where(mask, s, NEG_INF)
    mx = jnp.max(s, axis=-1, keepdims=True)
    mx = jnp.where(mx > NEG_INF, mx, 0.0)
    p = jnp.where(mask, jnp.exp2(s - mx), 0.0)
    return p / jnp.maximum(jnp.sum(p, axis=-1, keepdims=True), 1e-30)


def _layer_norm(xf, g, b):
    mu = jnp.mean(xf, axis=-1, keepdims=True)
    var = jnp.mean(jnp.square(xf - mu), axis=-1, keepdims=True)
    return (xf - mu) * lax.rsqrt(var + LN_EPS) * g + b


def _row_tile(n, pref):
    return pref if n % pref == 0 else n


def _in_proj_kernel(x_ref, wa_ref, wb_ref, wc_ref, wd_ref, ba_ref, bb_ref, bc_ref, bd_ref,
                    oa_ref, ob_ref, oc_ref, ocb_ref, od_ref):
    x = x_ref[...].astype(BF16)
    oa_ref[...] = (_dot(x, wa_ref[...]) + ba_ref[...]).astype(BF16)
    ob_ref[...] = (_dot(x, wb_ref[...]) + bb_ref[...]).astype(BF16)
    c = _dot(x, wc_ref[...]) + bc_ref[...]
    oc_ref[...] = c
    ocb_ref[...] = c.astype(BF16)
    od_ref[...] = _dot(x, wd_ref[...]) + bd_ref[...]


def _group_select():
    return jax.nn.one_hot(np.arange(NSA_HEADS) // NSA_HPG, NSA_GROUPS, dtype=F32)


def _pad_branch_rows(w_nsa):
    return jnp.einsum("hem,hg->hgem", w_nsa.reshape(NSA_HEADS, NSA_DH, -1), _group_select()).reshape(NSA_QPAD, -1)


def _split_in_proj(w_in_l, b_in_l):
    offs = np.cumsum((0,) + IN_SPLITS)
    w = [w_in_l[:, offs[i]:offs[i + 1]] for i in range(len(IN_SPLITS))]
    b = [b_in_l[offs[i]:offs[i + 1]] for i in range(len(IN_SPLITS))]
    mq, mk, mv, mi, mf, mo, nq, nkv, ng, xq, mg = range(11)
    pad = GATE_COLS - 2 * M_HEADS - 3 * NSA_HEADS

    def cat(ids, zpad=0):
        ww = jnp.concatenate([w[i] for i in ids], axis=1)
        bb = jnp.concatenate([b[i] for i in ids])
        if zpad:
            ww = jnp.pad(ww, ((0, 0), (0, zpad)))
            bb = jnp.pad(bb, (0, zpad))
        return ww.astype(BF16), bb.reshape(1, -1).astype(F32)

    scale = NSA_DH ** -0.5 * LOG2E
    w[nq] = jnp.einsum("dhe,hg->dhge", w[nq].reshape(D_MODEL, NSA_HEADS, NSA_DH) * scale, _group_select()).reshape(D_MODEL, NSA_QPAD)
    b[nq] = jnp.einsum("he,hg->hge", b[nq].reshape(NSA_HEADS, NSA_DH) * scale, _group_select()).reshape(NSA_QPAD)
    return dict(a=cat([mq, mk, mv, mo]), b=cat([nq, xq]), c=cat([nkv]), d=cat([mi, mf, ng], pad), mg=cat([mg]))


def in_proj(x2d, wp, tm):
    n = x2d.shape[0]
    (wa, ba), (wb, bb), (wc, bc), (wd, bd) = wp["a"], wp["b"], wp["c"], wp["d"]
    full = lambda arr: pl.BlockSpec(arr.shape, lambda i: (0, 0))
    row = lambda w: pl.BlockSpec((tm, w), lambda i: (i, 0))
    return pl.pallas_call(
        _in_proj_kernel,
        grid=(n // tm,),
        in_specs=[row(D_MODEL), full(wa), full(wb), full(wc), full(wd), full(ba), full(bb), full(bc), full(bd)],
        out_specs=[row(wa.shape[1]), row(wb.shape[1]), row(wc.shape[1]), row(wc.shape[1]), row(wd.shape[1])],
        out_shape=[jax.ShapeDtypeStruct((n, wa.shape[1]), BF16), jax.ShapeDtypeStruct((n, wb.shape[1]), BF16),
                   jax.ShapeDtypeStruct((n, wc.shape[1]), F32), jax.ShapeDtypeStruct((n, wc.shape[1]), BF16),
                   jax.ShapeDtypeStruct((n, wd.shape[1]), F32)],
        compiler_params=_params(("parallel",)),
        name="in_proj",
    )(x2d, wa, wb, wc, wd, ba, bb, bc, bd)


def _proj_kernel(x_ref, w_ref, o_ref):
    o_ref[...] = _dot(x_ref[...].astype(BF16), w_ref[...])


def proj(x2d, w_bf16, tm):
    n, k = x2d.shape
    m = w_bf16.shape[1]
    return pl.pallas_call(
        _proj_kernel,
        grid=(n // tm,),
        in_specs=[pl.BlockSpec((tm, k), lambda i: (i, 0)), pl.BlockSpec((k, m), lambda i: (0, 0))],
        out_specs=pl.BlockSpec((tm, m), lambda i: (i, 0)),
        out_shape=jax.ShapeDtypeStruct((n, m), F32),
        compiler_params=_params(("parallel",)),
        name="mem_proj",
    )(x2d, w_bf16)


def _log_sigmoid(x):
    return jnp.minimum(x, 0.0) - jnp.log1p(jnp.exp(-jnp.abs(x)))


def _mlstm_kernel(q_ref, k_ref, v_ref, og_ref, gc_ref, gr_ref, c0_ref, n0_ref, m0_ref, ng_ref,
                  h_ref, ct_ref, nt_ref, mt_ref, c_s, n_s, m_s, *, L, t_valid):
    ci = pl.program_id(1)

    @pl.when(ci == 0)
    def _():
        c_s[...] = c0_ref[0]
        n_s[...] = n0_ref[0]
        m_s[...] = m0_ref[0]

    row = lax.broadcasted_iota(jnp.int32, (L, L), 0)
    col = lax.broadcasted_iota(jnp.int32, (L, L), 1)
    causal = row >= col
    tri = causal.astype(F32)
    tri_t = (row <= col).astype(F32)
    gc = gc_ref[0]
    gr = gr_ref[0]
    rvalid = lax.broadcasted_iota(jnp.int32, (L, GATE_COLS), 0) < t_valid
    cvalid = lax.broadcasted_iota(jnp.int32, (2 * M_HEADS, L), 1) < t_valid
    lf_c = jnp.where(rvalid, _log_sigmoid(gc), 0.0)
    lf_r = jnp.where(cvalid, _log_sigmoid(gr), 0.0)
    ig_c = jnp.where(rvalid, gc, NEG_INF)
    ig_r = jnp.where(cvalid, gr, NEG_INF)
    b_c = _dot(tri, lf_c, HIGHEST)
    b_r = _dot(lf_r, tri_t, HIGHEST)
    scale = M_DH ** -0.5

    for h in range(M_HEADS):
        sl = slice(h * M_DH, (h + 1) * M_DH)
        q = q_ref[0, :, sl]
        k = k_ref[0, :, sl]
        v = v_ref[0, :, sl]
        bc = b_c[:, M_HEADS + h:M_HEADS + h + 1]
        br = b_r[M_HEADS + h:M_HEADS + h + 1, :]
        igc = ig_c[:, h:h + 1]
        igr = ig_r[h:h + 1, :]
        m_prev = m_s[h, 0:1, 0:1]
        c_prev = c_s[h]
        n_prev = n_s[h:h + 1, :]

        dmat = jnp.where(causal, bc - br + igr, NEG_INF)
        inter = bc + m_prev
        m_t = jnp.maximum(inter, jnp.max(dmat, axis=-1, keepdims=True))
        s = _dot_nt(q, k) * scale * jnp.exp(dmat - m_t)
        sc_in = jnp.exp(inter - m_t)
        qf = q.astype(F32)
        num = sc_in * _dot_nt(q, c_prev.astype(BF16)) + _dot(s.astype(BF16), v)
        den = sc_in * jnp.sum(qf * n_prev, axis=-1, keepdims=True) + jnp.sum(s, axis=-1, keepdims=True)
        hh = num / jnp.maximum(jnp.abs(den), jnp.exp(-m_t))

        b_last = bc[L - 1:L, :]
        dec_c = b_last - bc + igc
        dec_r = b_last - br + igr
        m_new = jnp.maximum(b_last + m_prev, jnp.max(dec_r, axis=-1, keepdims=True))
        ws_c = jnp.exp(dec_c - m_new) * scale
        sc = jnp.exp(b_last + m_prev - m_new)
        vf = v.astype(F32)
        kf = k.astype(F32)
        c_new = sc * c_prev + _dot_tn((vf * ws_c).astype(BF16), k)
        n_new = sc * n_prev + jnp.sum(kf * ws_c, axis=0, keepdims=True)
        c_s[h] = c_new
        n_s[h:h + 1, :] = n_new
        m_s[h] = jnp.broadcast_to(m_new, m_s.shape[1:])

        og = og_ref[0, :, sl].astype(F32)
        hh = hh * jax.nn.sigmoid(og)
        mu = jnp.mean(hh, axis=-1, keepdims=True)
        var = jnp.mean(jnp.square(hh - mu), axis=-1, keepdims=True)
        hn = (hh - mu) * lax.rsqrt(var + LN_EPS) * ng_ref[:, sl]
        h_ref[0, :, sl] = hn.astype(h_ref.dtype)

    ct_ref[0] = c_s[...]
    nt_ref[0] = n_s[...]
    mt_ref[0] = m_s[...]


def mlstm(za, zd, c0, n0, m0, norm_g, *, L, t_valid):
    B, T, _ = za.shape
    nc = T // L
    gr = jnp.swapaxes(zd[:, :, :2 * M_HEADS], 1, 2)
    m0b = jnp.broadcast_to(m0.astype(F32)[:, :, None, None], (B, M_HEADS, SUBLANES, LANES))
    colspec = lambda j: pl.BlockSpec((1, L, M_WIDTH), lambda b, c, j=j: (b, c, j))
    st = lambda shp: pl.BlockSpec((1,) + shp, lambda b, c: (b,) + (0,) * len(shp))
    kern = functools.partial(_mlstm_kernel, L=L, t_valid=t_valid)
    h, ct, nt, mt = pl.pallas_call(
        kern,
        grid=(B, nc),
        in_specs=[colspec(0), colspec(1), colspec(2), colspec(3),
                  pl.BlockSpec((1, L, GATE_COLS), lambda b, c: (b, c, 0)),
                  pl.BlockSpec((1, 2 * M_HEADS, L), lambda b, c: (b, 0, c)),
                  st((M_HEADS, M_DH, M_DH)), st((M_HEADS, M_DH)), st((M_HEADS, SUBLANES, LANES)),
                  pl.BlockSpec((1, M_WIDTH), lambda b, c: (0, 0))],
        out_specs=[pl.BlockSpec((1, L, M_WIDTH), lambda b, c: (b, c, 0)),
                   st((M_HEADS, M_DH, M_DH)), st((M_HEADS, M_DH)), st((M_HEADS, SUBLANES, LANES))],
        out_shape=[jax.ShapeDtypeStruct((B, T, M_WIDTH), BF16),
                   jax.ShapeDtypeStruct((B, M_HEADS, M_DH, M_DH), F32),
                   jax.ShapeDtypeStruct((B, M_HEADS, M_DH), F32),
                   jax.ShapeDtypeStruct((B, M_HEADS, SUBLANES, LANES), F32)],
        scratch_shapes=[pltpu.VMEM((M_HEADS, M_DH, M_DH), F32), pltpu.VMEM((M_HEADS, M_DH), F32),
                        pltpu.VMEM((M_HEADS, SUBLANES, LANES), F32)],
        compiler_params=_params(("parallel", "arbitrary")),
        name="mlstm",
    )(za, za, za, za, zd, gr, c0.astype(F32), n0.astype(F32), m0b, norm_g.reshape(1, M_WIDTH).astype(F32))
    return h, ct, nt, mt[:, :, 0, 0]


def _mem_attn_kernel(q_ref, k_ref, v_ref, o_ref):
    scale = X_DH ** -0.5
    for h in range(X_HEADS):
        sl = slice(h * X_DH, (h + 1) * X_DH)
        s = _dot_nt(q_ref[0, :, sl], k_ref[0, :, sl].astype(BF16)) * scale
        mx = jnp.max(s, axis=-1, keepdims=True)
        p = jnp.exp(s - mx)
        p = p / jnp.sum(p, axis=-1, keepdims=True)
        o_ref[0, :, sl] = _dot(p.astype(BF16), v_ref[0, :, sl].astype(BF16)).astype(o_ref.dtype)


def mem_attn(zb, kv, k_blk, v_blk, tq):
    B, T, _ = zb.shape
    return pl.pallas_call(
        _mem_attn_kernel,
        grid=(B, T // tq),
        in_specs=[pl.BlockSpec((1, tq, X_WIDTH), lambda b, i: (b, i, NSA_QPAD // X_WIDTH)),
                  pl.BlockSpec((1, MEM_LEN, X_WIDTH), lambda b, i: (b, 0, k_blk)),
                  pl.BlockSpec((1, MEM_LEN, X_WIDTH), lambda b, i: (b, 0, v_blk))],
        out_specs=pl.BlockSpec((1, tq, X_WIDTH), lambda b, i: (b, i, 0)),
        out_shape=jax.ShapeDtypeStruct((B, T, X_WIDTH), BF16),
        compiler_params=_params(("parallel", "parallel")),
        name="mem_attn",
    )(zb, kv, kv)


def _head_rows(q_ref, nq):
    parts = [q_ref[0, :, hd * LANES:(hd + 1) * LANES] for hd in range(NSA_HEADS)]
    if nq % 16:
        return jnp.concatenate([p.astype(F32) for p in parts], axis=0).astype(BF16)
    return jnp.concatenate(parts, axis=0)


def _store_heads(o_ref, gate, o_c, o_s, o_w, nq):
    lane = lax.broadcasted_iota(jnp.int32, (nq, LANES), 1)
    for hd in range(NSA_HEADS):
        r = slice(hd * nq, (hd + 1) * nq)
        gcol = lambda br: gate[:, NG_OFF + br * NSA_HEADS + hd:NG_OFF + br * NSA_HEADS + hd + 1]
        val = gcol(0) * o_c[r] + gcol(1) * o_s[r] + gcol(2) * o_w[r]
        keep = (lane >= NSA_DH) if hd // NSA_HPG == 1 else (lane < NSA_DH)
        o_ref[0, :, hd * LANES:(hd + 1) * LANES] = jnp.where(keep, val, 0.0).astype(o_ref.dtype)


def _overlap(tok, blk):
    c_start = tok * CMP_STRIDE
    s_start = blk * SEL_LEN
    return ((c_start < s_start + SEL_LEN) & (c_start + CMP_LEN > s_start)).astype(F32)


def _select_blocks(imp, tpos, n_sel, n_top):
    nq, W = imp.shape
    blk = lax.broadcasted_iota(jnp.int32, (nq, W), 1)
    cur = tpos // SEL_LEN
    forced = (blk == 0) | (blk == cur) | (blk == cur - 1)
    v = jnp.where(forced, jnp.inf, jnp.where(blk <= cur, imp, NEG_INF))
    v = jnp.where(blk < n_sel, v, NEG_INF)
    ahead = jnp.zeros((nq, W), F32)
    for j in range(n_sel):
        vj = v[:, j:j + 1]
        ahead = ahead + jnp.where(vj > v, 1.0, jnp.where(vj == v, jnp.where(blk > j, 1.0, 0.0), 0.0))
    return jnp.where((ahead < n_top) & (blk < n_sel), 1.0, 0.0)


def _select_blocks_t(imp_t, tpos_row, n_sel, n_top):
    n_blk, nq = imp_t.shape
    blk = lax.broadcasted_iota(jnp.int32, (n_blk, nq), 0)
    cur = tpos_row // SEL_LEN
    forced = (blk == 0) | (blk == cur) | (blk == cur - 1)
    v = jnp.where(forced, jnp.inf, jnp.where(blk <= cur, imp_t, NEG_INF))
    v = jnp.where(blk < n_sel, v, NEG_INF)
    n_rg = n_blk // SUBLANES
    vg = [v[rg * SUBLANES:(rg + 1) * SUBLANES] for rg in range(n_rg)]
    bg = [rg * SUBLANES + lax.broadcasted_iota(jnp.int32, (SUBLANES, nq), 0) for rg in range(n_rg)]
    ahead = [jnp.zeros((SUBLANES, nq), F32) for _ in range(n_rg)]
    for j in range(n_sel):
        vj = v[j:j + 1, :]
        for rg in range(n_rg):
            if rg * SUBLANES > j:
                inc = jnp.where(vj >= vg[rg], 1.0, 0.0)
            elif (rg + 1) * SUBLANES - 1 <= j:
                inc = jnp.where(vj > vg[rg], 1.0, 0.0)
            else:
                inc = jnp.where(vj > vg[rg], 1.0, jnp.where(vj == vg[rg], jnp.where(bg[rg] > j, 1.0, 0.0), 0.0))
            ahead[rg] = ahead[rg] + inc
    ahead = jnp.concatenate(ahead, axis=0)
    return jnp.where((ahead < n_top) & (blk < n_sel), 1.0, 0.0)


def _online_update(s, m_old, l_old):
    m_new = jnp.maximum(m_old, jnp.max(s, axis=-1, keepdims=True))
    alpha = jnp.exp2(m_old - m_new)
    p = jnp.exp2(s - m_new)
    return p, m_new, alpha, alpha * l_old + jnp.sum(p, axis=-1, keepdims=True)


def _cmp_tokens_kernel(xk_ref, xv_ref, bd_ref, b_ref, kc_ref, vc_ref, *, nsub):
    for c, (x_ref, o_ref) in enumerate(((xk_ref, kc_ref), (xv_ref, vc_ref))):
        a0 = jnp.zeros((nsub, LANES), F32)
        a1 = jnp.zeros((nsub, LANES), F32)
        for j in range(CMP_STRIDE):
            xj = x_ref[0, :, j, :].astype(BF16)
            a0 = a0 + _dot(xj, bd_ref[c, 0, j])
            a1 = a1 + _dot(xj, bd_ref[c, 1, j])
        tok = a0 + pltpu.roll(a1, nsub - 1, axis=0) + b_ref[c]
        o_ref[0] = tok.astype(o_ref.dtype)


def _cmp_blockdiag(cmp_w_l):
    R = CMP_LEN // CMP_STRIDE
    w = cmp_w_l.astype(F32).reshape(2, R, CMP_STRIDE, NSA_DH, NSA_DH)
    eye = jnp.eye(NSA_GROUPS, dtype=F32)
    return jnp.einsum("ab,crjde->crjadbe", eye, w).reshape(2, R, CMP_STRIDE, LANES, LANES).astype(BF16)


def _cmp_bias(cmp_b_l):
    return jnp.tile(cmp_b_l.astype(F32), (1, NSA_GROUPS)).reshape(2, 1, LANES)


def cmp_tokens(zc, cmp_w_l, cmp_b_l):
    B, T, W = zc.shape
    nsub = T // CMP_STRIDE
    x4 = zc.reshape(B, nsub, CMP_STRIDE, W)
    bd = _cmp_blockdiag(cmp_w_l)
    bias = _cmp_bias(cmp_b_l)
    spec = lambda j: pl.BlockSpec((1, nsub, CMP_STRIDE, LANES), lambda b, j=j: (b, 0, 0, j))
    return pl.pallas_call(
        functools.partial(_cmp_tokens_kernel, nsub=nsub),
        grid=(B,),
        in_specs=[spec(0), spec(1), pl.BlockSpec(bd.shape, lambda b: (0,) * 5), pl.BlockSpec(bias.shape, lambda b: (0, 0, 0))],
        out_specs=[pl.BlockSpec((1, nsub, LANES), lambda b: (b, 0, 0))] * 2,
        out_shape=[jax.ShapeDtypeStruct((B, nsub, LANES), BF16)] * 2,
        compiler_params=_params(("parallel",)),
        name="nsa_cmp_tokens",
    )(x4, x4, bd, bias)


def _nsa_prompt_kernel(q_ref, g_ref, kca_ref, vc_ref, ka_ref, vst_ref, kwa_ref, vw_ref, qf_ref, o_ref,
                       qaug_s, s_buf, p_buf, w_buf, pw_buf, pc_s, m_s, l_s, a_s, acc_s, oc_s, ow_s, *, tq, T, CH):
    start = pl.program_id(1) * tq
    n_cmp_rows = kca_ref.shape[1]
    n_sel = T // SEL_LEN
    n_top = min(SEL_TOPN, n_sel)
    n_blk = -(-n_sel // SUBLANES) * SUBLANES
    nq_all = NSA_HEADS * tq
    qw = nq_all // SEL_QUARTERS
    tpos_row = start + lax.broadcasted_iota(jnp.int32, (1, tq), 1)
    tpos_all = jnp.concatenate([tpos_row] * NSA_HEADS, axis=1)
    qcols = lambda qi: slice(qi * qw, (qi + 1) * qw)
    tiles = [(qi, ct, slice(qi * qw + ct * LANES, qi * qw + (ct + 1) * LANES), slice(ct * LANES, (ct + 1) * LANES))
             for qi in range(SEL_QUARTERS) for ct in range(qw // LANES)]

    qaug_s[:, 0:LANES] = _head_rows(q_ref, tq)
    for hd in range(NSA_HEADS):
        qaug_s[hd * tq:(hd + 1) * tq, LANES:2 * LANES] = jnp.broadcast_to(qf_ref[hd:hd + 1, :], (tq, LANES)).astype(BF16)

    def softmax_tile(s_t):
        mx = jnp.max(s_t, axis=0, keepdims=True)
        mx = jnp.where(mx > NEG_INF, mx, 0.0)
        p_t = jnp.exp2(s_t - mx)
        return p_t, jnp.maximum(jnp.sum(p_t, axis=0, keepdims=True), 1e-30)

    c_end = lax.broadcasted_iota(jnp.int32, (n_cmp_rows, 1), 0) * CMP_STRIDE + CMP_LEN - 1
    for qi in range(SEL_QUARTERS):
        s_buf[qi, 0:n_cmp_rows, :] = _dot_nt(kca_ref[0], qaug_s[qcols(qi), :])
    for qi, ct, cols, tc in tiles:
        p_t, den = softmax_tile(jnp.where(c_end <= tpos_all[:, cols], s_buf[qi, 0:n_cmp_rows, tc], NEG_INF))
        pc_s[:, cols] = p_t / den
    oc_s[...] = _dot_tn(vc_ref[0], pc_s[...].astype(BF16))
    ov_t = _overlap(lax.broadcasted_iota(jnp.int32, (n_blk, n_cmp_rows), 1),
                    lax.broadcasted_iota(jnp.int32, (n_blk, n_cmp_rows), 0))
    lane = lax.broadcasted_iota(jnp.int32, (tq, LANES), 1)
    feat_sel = [None] * NSA_HEADS
    for g in range(NSA_GROUPS):
        pg = pc_s[:, g * NSA_HPG * tq:(g * NSA_HPG + 1) * tq]
        for p in range(1, NSA_HPG):
            pg = pg + pc_s[:, (g * NSA_HPG + p) * tq:(g * NSA_HPG + p + 1) * tq]
        sel_t = _select_blocks_t(_dot(ov_t, pg, HIGHEST), tpos_row, n_sel, n_top)
        sel = jnp.concatenate([sel_t, jnp.zeros((LANES - n_blk, tq), F32)], axis=0).T
        unsel = jnp.where(lane < SEL_FEAT, (1.0 - sel) * MASK_BIAS, 0.0)
        for p in range(NSA_HPG):
            hd = g * NSA_HPG + p
            feat_sel[hd] = (unsel + qf_ref[hd:hd + 1, :]).astype(BF16)

    wk = WINDOW + tq
    base_w = pl.multiple_of(start, tq)
    wpos = start - WINDOW + lax.broadcasted_iota(jnp.int32, (wk, 1), 0)
    for qi in range(SEL_QUARTERS):
        w_buf[qi] = _dot_nt(kwa_ref[0, pl.ds(base_w, wk), :], qaug_s[qcols(qi), :])
    for qi, ct, cols, tc in tiles:
        tp = tpos_all[:, cols]
        first = jnp.maximum(tp - (WINDOW - 1), 0)
        s_t = jnp.where(wpos >= first, jnp.where(wpos <= tp, w_buf[qi, :, tc], NEG_INF), NEG_INF)
        p_t, den = softmax_tile(s_t)
        pw_buf[qi, :, tc] = p_t.astype(BF16)
        a_s[:, cols] = den
    for qi in range(SEL_QUARTERS):
        ow_s[:, qcols(qi)] = _dot_tn(vw_ref[0, pl.ds(base_w, wk), :], pw_buf[qi]) / a_s[:, qcols(qi)]

    for hd in range(NSA_HEADS):
        qaug_s[hd * tq:(hd + 1) * tq, LANES:2 * LANES] = feat_sel[hd]

    m_s[...] = jnp.full(m_s.shape, NEG_INF, F32)
    l_s[...] = jnp.zeros(l_s.shape, F32)
    acc_s[...] = jnp.zeros(acc_s.shape, F32)

    def issue(c, qi):
        base = pl.multiple_of(c * CH, CH)
        s_buf[qi] = _dot_nt(ka_ref[0, pl.ds(base, CH), :], qaug_s[qi * qw:(qi + 1) * qw, :])

    def absorb(c, qi, diag):
        for ct in range(qw // LANES):
            cols = slice(qi * qw + ct * LANES, qi * qw + (ct + 1) * LANES)
            s_t = s_buf[qi, :, ct * LANES:(ct + 1) * LANES]
            if diag:
                kpos = c * CH + lax.broadcasted_iota(jnp.int32, (CH, 1), 0)
                s_t = jnp.where(kpos <= tpos_all[:, cols], s_t, NEG_INF)
            m_old = m_s[:, cols]
            m_new = jnp.maximum(m_old, jnp.max(s_t, axis=0, keepdims=True))
            alpha = jnp.exp2(m_old - m_new)
            p_t = jnp.exp2(s_t - m_new)
            m_s[:, cols] = m_new
            a_s[:, cols] = alpha
            l_s[:, cols] = alpha * l_s[:, cols] + jnp.sum(p_t, axis=0, keepdims=True)
            p_buf[qi, :, ct * LANES:(ct + 1) * LANES] = p_t.astype(BF16)
        cols = slice(qi * qw, (qi + 1) * qw)
        acc_s[:, cols] = a_s[:, cols] * acc_s[:, cols] + _dot(vst_ref[0, c], p_buf[qi])

    def full_chunk(c, carry):
        for qi in range(SEL_QUARTERS):
            absorb(c, qi, False)
            issue(c + 1, qi)
        return carry

    n_full = start // CH
    for qi in range(SEL_QUARTERS):
        issue(0, qi)
    lax.fori_loop(0, n_full, full_chunk, 0)
    for qi in range(SEL_QUARTERS):
        absorb(n_full, qi, True)

    gate_t = jax.nn.sigmoid(g_ref[0]).T
    for hd in range(NSA_HEADS):
        cols = slice(hd * tq, (hd + 1) * tq)
        grow = lambda br: gate_t[NG_OFF + br * NSA_HEADS + hd:NG_OFF + br * NSA_HEADS + hd + 1, :]
        o_s = acc_s[:, cols] / jnp.maximum(l_s[:, cols], 1e-30)
        val = (grow(0) * oc_s[:, cols] + grow(1) * o_s + grow(2) * ow_s[:, cols]).T
        keep = (lane >= NSA_DH) if hd // NSA_HPG == 1 else (lane < NSA_DH)
        o_ref[0, :, hd * LANES:(hd + 1) * LANES] = jnp.where(keep, val, 0.0).astype(o_ref.dtype)


def _bf16_parts(x, n):
    parts = []
    for _ in range(n):
        p = float(np.asarray(x, np.float32).astype(jnp.bfloat16).astype(np.float32))
        parts.append(p)
        x = x - p
    return parts


def _slope_features():
    qf = np.zeros((NSA_HEADS, LANES), np.float32)
    for hd in range(NSA_HEADS):
        for p, s_p in enumerate(_bf16_parts(NSA_SLOPES[hd], SLOPE_PARTS)):
            qf[hd, SEL_FEAT + 2 * p] = s_p * SEL_LEN
            qf[hd, SEL_FEAT + 2 * p + 1] = s_p
    return jnp.asarray(qf)


def _key_features(pos, block_onehot):
    pos = np.asarray(pos)
    ok = pos >= 0
    kf = np.zeros((pos.shape[0], LANES), np.float32)
    if block_onehot:
        kf[np.arange(pos.shape[0])[ok], pos[ok] // SEL_LEN] = 1.0
    for p in range(SLOPE_PARTS):
        kf[ok, SEL_FEAT + 2 * p] = pos[ok] // SEL_LEN
        kf[ok, SEL_FEAT + 2 * p + 1] = pos[ok] % SEL_LEN
    return jnp.asarray(kf, BF16)


def nsa_prompt(zb, zd, zcb, kc, vc, tq, CH):
    B, T, _ = zb.shape
    nsub = kc.shape[1]
    assert T % CH == 0 and CH % tq == 0 and T // SEL_LEN <= SEL_FEAT and nsub <= CH
    qf = _slope_features()
    with_feats = lambda k, feats: jnp.concatenate([k, jnp.broadcast_to(feats[None], (B,) + feats.shape)], axis=2)
    k_aug = with_feats(zcb[:, :, 2 * NSA_KV:3 * NSA_KV], _key_features(np.arange(T), True))
    kc_aug = with_feats(kc, _key_features(np.arange(nsub) * CMP_STRIDE + CMP_LEN - 1, False))
    kw = jnp.pad(zcb[:, :, 4 * NSA_KV:5 * NSA_KV], ((0, 0), (WINDOW, 0), (0, 0)))
    kw_aug = with_feats(kw, _key_features(np.arange(T + WINDOW) - WINDOW, False))
    vs_t = jnp.swapaxes(zcb[:, :, 3 * NSA_KV:4 * NSA_KV].reshape(B, T // CH, CH, NSA_KV), 2, 3)
    vw = jnp.pad(zcb[:, :, 5 * NSA_KV:6 * NSA_KV], ((0, 0), (WINDOW, 0), (0, 0)))
    per_b = lambda rows_, w: pl.BlockSpec((1, rows_, w), lambda b, i: (b, 0, 0))
    kern = functools.partial(_nsa_prompt_kernel, tq=tq, T=T, CH=CH)
    nq_all = NSA_HEADS * tq
    qw = nq_all // SEL_QUARTERS
    wk = WINDOW + tq
    assert qw % LANES == 0
    return pl.pallas_call(
        kern,
        grid=(B, T // tq),
        in_specs=[pl.BlockSpec((1, tq, NSA_QPAD), lambda b, i: (b, i, 0)),
                  pl.BlockSpec((1, tq, GATE_COLS), lambda b, i: (b, i, 0)),
                  per_b(nsub, 2 * LANES), per_b(nsub, LANES),
                  per_b(T, 2 * LANES),
                  pl.BlockSpec((1, T // CH, NSA_KV, CH), lambda b, i: (b, 0, 0, 0)),
                  per_b(T + WINDOW, 2 * LANES), per_b(T + WINDOW, LANES),
                  pl.BlockSpec(qf.shape, lambda b, i: (0, 0))],
        out_specs=pl.BlockSpec((1, tq, NSA_QPAD), lambda b, i: (b, i, 0)),
        out_shape=jax.ShapeDtypeStruct((B, T, NSA_QPAD), BF16),
        scratch_shapes=[pltpu.VMEM((nq_all, 2 * LANES), BF16),
                        pltpu.VMEM((SEL_QUARTERS, CH, qw), F32), pltpu.VMEM((SEL_QUARTERS, CH, qw), BF16),
                        pltpu.VMEM((SEL_QUARTERS, wk, qw), F32), pltpu.VMEM((SEL_QUARTERS, wk, qw), BF16),
                        pltpu.VMEM((nsub, nq_all), F32),
                        pltpu.VMEM((1, nq_all), F32), pltpu.VMEM((1, nq_all), F32), pltpu.VMEM((1, nq_all), F32),
                        pltpu.VMEM((NSA_KV, nq_all), F32), pltpu.VMEM((NSA_KV, nq_all), F32),
                        pltpu.VMEM((NSA_KV, nq_all), F32)],
        compiler_params=_params(("parallel", "arbitrary")),
        name="nsa_prompt",
    )(zb, zd, kc_aug, vc, k_aug, vs_t, kw_aug, vw, qf)


def _page_specs(shape, slot_blk, layer, n_pages):
    def mk(i):
        def imap(b, s, pt):
            return (pt[b * n_pages + s * PAGES_PER_STEP + i], layer, slot_blk, 0, 0)
        return pl.BlockSpec(shape, imap)
    return [mk(i) for i in range(PAGES_PER_STEP)]


def _nsa_dec_cmp_kernel(pt_ref, *refs, T, past_len, n_rows):
    pages = refs[:PAGES_PER_STEP]
    perm_ref, bd_ref, b_ref, q_ref, oc_ref, sel_ref, tok_s, pend_s = refs[PAGES_PER_STEP:]
    s = pl.program_id(1)
    n_steps = pl.num_programs(1)
    page_rows = pages[0].shape[-1]
    sub = page_rows // CMP_STRIDE
    R = PAGES_PER_STEP * sub

    @pl.when(s == 0)
    def _():
        pend_s[...] = jnp.zeros(pend_s.shape, F32)

    rid = lax.broadcasted_iota(jnp.int32, (R, LANES), 0)
    for c in range(2):
        xs = [_dot_nt(perm_ref[...], pg[0, 0, c].astype(BF16)) for pg in pages]
        a0 = jnp.zeros((R, LANES), F32)
        a1 = jnp.zeros((R, LANES), F32)
        for j in range(CMP_STRIDE):
            xj = jnp.concatenate([x[j * sub:(j + 1) * sub] for x in xs], axis=0).astype(BF16)
            a0 = a0 + _dot(xj, bd_ref[c, 0, j])
            a1 = a1 + _dot(xj, bd_ref[c, 1, j])
        a0 = a0 + b_ref[c]
        tok = jnp.where(rid == 0, pend_s[c], pltpu.roll(a0, 1, axis=0)) + a1
        tok_s[c, pl.ds(pl.multiple_of(s * R, R), R), :] = tok
        pend_s[c] = a0[R - 1:R, :]

    @pl.when(s == n_steps - 1)
    def _():
        qp = _head_rows(q_ref, T)
        kc = tok_s[0].astype(BF16)
        vc = tok_s[1].astype(BF16)
        sc_all = _dot_nt(qp, kc)
        tpos = past_len + lax.broadcasted_iota(jnp.int32, (T, 1), 0)
        r = lax.broadcasted_iota(jnp.int32, (T, n_rows), 1)
        dc = tpos - ((r - 1) * CMP_STRIDE + CMP_LEN - 1)
        mask_c = (dc >= 0) & (r >= 1)
        dcf = dc.astype(F32)
        pcs = [_masked_softmax2(sc_all[hd * T:(hd + 1) * T] - NSA_SLOPES[hd] * dcf, mask_c) for hd in range(NSA_HEADS)]
        oc_ref[0] = _dot(jnp.concatenate(pcs, axis=0).astype(BF16), vc)
        W = sel_ref.shape[2]
        n_sel = -(-(past_len + T) // SEL_LEN)
        ov = _overlap(lax.broadcasted_iota(jnp.int32, (n_rows, W), 0) - 1, lax.broadcasted_iota(jnp.int32, (n_rows, W), 1))
        for g in range(NSA_GROUPS):
            pg = pcs[g * NSA_HPG]
            for p in range(1, NSA_HPG):
                pg = pg + pcs[g * NSA_HPG + p]
            imp = _dot(pg, ov, HIGHEST)
            sel_ref[0, g * T:(g + 1) * T, :] = _select_blocks(imp, tpos, n_sel, min(SEL_TOPN, n_sel))


def _nsa_dec_sel_kernel(pt_ref, *refs, T, past_len):
    pages = refs[:PAGES_PER_STEP]
    q_ref, sel_ref, sel_last_ref, ex_ref, kn_ref, vn_ref, os_ref, m_s, l_s, acc_s = refs[PAGES_PER_STEP:]
    s = pl.program_id(1)
    n_steps = pl.num_programs(1)
    page_rows = pages[0].shape[-1]
    CH = PAGES_PER_STEP * page_rows

    @pl.when(s == 0)
    def _():
        m_s[...] = jnp.full(m_s.shape, NEG_INF, F32)
        l_s[...] = jnp.zeros(l_s.shape, F32)
        acc_s[...] = jnp.zeros(acc_s.shape, F32)

    qp = _head_rows(q_ref, T)
    tpos = past_len + lax.broadcasted_iota(jnp.int32, (T, 1), 0)

    def update(s_all, mask_of_group, kpos, pv):
        ds = tpos - kpos
        dsf = ds.astype(F32)
        ps = []
        for g in range(NSA_GROUPS):
            mk = mask_of_group(g) & (ds >= 0)
            for p in range(NSA_HPG):
                hd = g * NSA_HPG + p
                r = slice(hd * T, (hd + 1) * T)
                sc = jnp.where(mk, s_all[r] - NSA_SLOPES[hd] * dsf, NEG_INF)
                pr, m_new, alpha, l_new = _online_update(sc, m_s[r], l_s[r])
                m_s[r] = m_new
                l_s[r] = l_new
                acc_s[r] = alpha * acc_s[r]
                ps.append(pr)
        acc_s[...] = acc_s[...] + pv(jnp.concatenate(ps, axis=0).astype(BF16))

    selk = _dot(sel_ref[0, 0].astype(BF16), ex_ref[...])
    s_all = jnp.concatenate([_dot(qp, pg[0, 0, 0].astype(BF16)) for pg in pages], axis=1)
    kpos = s * CH + lax.broadcasted_iota(jnp.int32, (T, CH), 1)

    def pv_pages(pmat):
        out = jnp.zeros((NSA_HEADS * T, LANES), F32)
        for i, pg in enumerate(pages):
            out = out + _dot_nt(pmat[:, i * page_rows:(i + 1) * page_rows], pg[0, 0, 1].astype(BF16))
        return out

    update(s_all, lambda g: selk[g * T:(g + 1) * T] > 0.5, kpos, pv_pages)

    @pl.when(s == n_steps - 1)
    def _():
        nk = kn_ref.shape[1]
        sn = _dot_nt(qp, kn_ref[0])
        lane = lax.broadcasted_iota(jnp.int32, (T, nk), 1)
        sl = sel_last_ref[0, 0]
        update(sn, lambda g: (sl[g * T:(g + 1) * T, 0:1] > 0.5) & (lane < T), past_len + lane,
               lambda pmat: _dot(pmat, vn_ref[0]))
        os_ref[0] = acc_s[...] / jnp.maximum(l_s[...], 1e-30)


def _nsa_dec_win_kernel(q_ref, g_ref, oc_ref, os_ref, wp_ref, kwn_ref, vwn_ref, o_ref, *, T, past_len):
    qp = _head_rows(q_ref, T)
    w_src = wp_ref.shape[-1]
    nk = kwn_ref.shape[1]
    sw_all = jnp.concatenate([_dot(qp, wp_ref[0, 0, 0].astype(BF16)), _dot_nt(qp, kwn_ref[0])], axis=1)
    j = lax.broadcasted_iota(jnp.int32, (T, w_src + nk), 1)
    tpos = past_len + lax.broadcasted_iota(jnp.int32, (T, 1), 0)
    wpos = past_len - w_src + j
    dw = tpos - wpos
    mask_w = (dw >= 0) & (dw < WINDOW) & (wpos >= 0) & (j < w_src + T)
    dwf = dw.astype(F32)
    pws = [_masked_softmax2(sw_all[hd * T:(hd + 1) * T] - NSA_SLOPES[hd] * dwf, mask_w) for hd in range(NSA_HEADS)]
    pw = jnp.concatenate(pws, axis=0).astype(BF16)
    o_w = _dot_nt(pw[:, :w_src], wp_ref[0, 0, 1].astype(BF16)) + _dot(pw[:, w_src:], vwn_ref[0])
    _store_heads(o_ref, jax.nn.sigmoid(g_ref[0]), oc_ref[0], os_ref[0], o_w, T)


def nsa_decode(zb, zd, zcb, cache_nsa_kv, cache_win_kv, page_table, cmp_w_l, cmp_b_l, layer):
    DB, T, _ = zb.shape
    n_pool, page_rows = cache_nsa_kv.shape[:2]
    n_pages = page_table.shape[1]
    past_len = n_pages * page_rows
    n_steps = n_pages // PAGES_PER_STEP
    sub_per_page = page_rows // CMP_STRIDE
    n_rows = n_pages * sub_per_page
    assert (n_rows - 1) * CMP_STRIDE + CMP_LEN - 1 > past_len + T - 1
    assert T <= CMP_STRIDE and n_pages % PAGES_PER_STEP == 0 and page_rows == LANES
    pt = page_table.reshape(-1).astype(jnp.int32)
    cache_t = jnp.transpose(cache_nsa_kv, (0, 2, 3, 4, 5, 1)).reshape(n_pool, DEPTH, 4, NSA_KV, page_rows)
    page_blk = (1, 1, 2, NSA_KV, page_rows)

    rr = np.arange(page_rows)
    perm = jnp.asarray(rr[None, :] == ((rr % sub_per_page) * CMP_STRIDE + rr // sub_per_page)[:, None], BF16)
    bd = _cmp_blockdiag(cmp_w_l)
    bias = _cmp_bias(cmp_b_l)

    n_sel = -(-(past_len + T) // SEL_LEN)
    blocks_per_step = PAGES_PER_STEP * page_rows // SEL_LEN
    sel_used = (n_steps + 1) * blocks_per_step
    sel_w = -(-sel_used // LANES) * LANES
    assert sel_used >= n_sel and blocks_per_step <= LANES
    q_spec = pl.BlockSpec((1, T, NSA_QPAD), lambda b, s, pt: (b, 0, 0))
    const = lambda a: pl.BlockSpec(a.shape, lambda b, s, pt: (0,) * a.ndim)

    o_c, sel = pl.pallas_call(
        functools.partial(_nsa_dec_cmp_kernel, T=T, past_len=past_len, n_rows=n_rows),
        grid_spec=pltpu.PrefetchScalarGridSpec(
            num_scalar_prefetch=1, grid=(DB, n_steps),
            in_specs=_page_specs(page_blk, 0, layer, n_pages) + [const(perm), const(bd), const(bias), q_spec],
            out_specs=[pl.BlockSpec((1, NSA_HEADS * T, LANES), lambda b, s, pt: (b, 0, 0)),
                       pl.BlockSpec((1, NSA_GROUPS * T, sel_w), lambda b, s, pt: (b, 0, 0))],
            scratch_shapes=[pltpu.VMEM((2, n_rows, LANES), F32), pltpu.VMEM((2, 1, LANES), F32)]),
        out_shape=[jax.ShapeDtypeStruct((DB, NSA_HEADS * T, LANES), F32),
                   jax.ShapeDtypeStruct((DB, NSA_GROUPS * T, sel_w), F32)],
        compiler_params=_params(("parallel", "arbitrary")),
        name="nsa_dec_cmp",
    )(pt, *([cache_t] * PAGES_PER_STEP), perm, bd, bias, zb)

    sel_steps = sel[:, :, :sel_used].reshape(DB, NSA_GROUPS * T, n_steps + 1, blocks_per_step).transpose(0, 2, 1, 3)
    sel_steps = jnp.pad(sel_steps, ((0, 0), (0, 0), (0, 0), (0, LANES - blocks_per_step)))
    kk = np.arange(PAGES_PER_STEP * page_rows) // SEL_LEN
    expand = jnp.asarray(kk[None, :] == np.arange(LANES)[:, None], BF16)
    pad_rows = LANES - T
    new_rows = lambda slot: jnp.pad(zcb[:, :, slot * NSA_KV:(slot + 1) * NSA_KV], ((0, 0), (0, pad_rows), (0, 0)))
    new_spec = pl.BlockSpec((1, LANES, LANES), lambda b, s, pt: (b, 0, 0))
    sel_blk = (1, 1, NSA_GROUPS * T, LANES)

    o_s = pl.pallas_call(
        functools.partial(_nsa_dec_sel_kernel, T=T, past_len=past_len),
        grid_spec=pltpu.PrefetchScalarGridSpec(
            num_scalar_prefetch=1, grid=(DB, n_steps),
            in_specs=_page_specs(page_blk, 1, layer, n_pages)
            + [q_spec, pl.BlockSpec(sel_blk, lambda b, s, pt: (b, s, 0, 0)),
               pl.BlockSpec(sel_blk, lambda b, s, pt: (b, n_steps, 0, 0)), const(expand), new_spec, new_spec],
            out_specs=pl.BlockSpec((1, NSA_HEADS * T, LANES), lambda b, s, pt: (b, 0, 0)),
            scratch_shapes=[pltpu.VMEM((NSA_HEADS * T, 1), F32), pltpu.VMEM((NSA_HEADS * T, 1), F32),
                            pltpu.VMEM((NSA_HEADS * T, LANES), F32)]),
        out_shape=jax.ShapeDtypeStruct((DB, NSA_HEADS * T, LANES), F32),
        compiler_params=_params(("parallel", "arbitrary")),
        name="nsa_dec_sel",
    )(pt, *([cache_t] * PAGES_PER_STEP), zb, sel_steps, sel_steps, expand, new_rows(2), new_rows(3))

    w_src = cache_win_kv.shape[1]
    win_t = jnp.transpose(cache_win_kv, (0, 2, 3, 4, 5, 1)).reshape(DB, DEPTH, 2, NSA_KV, w_src)
    b3 = lambda shp: pl.BlockSpec(shp, lambda b: (b, 0, 0))
    return pl.pallas_call(
        functools.partial(_nsa_dec_win_kernel, T=T, past_len=past_len),
        grid=(DB,),
        in_specs=[b3((1, T, NSA_QPAD)), b3((1, T, GATE_COLS)), b3((1, NSA_HEADS * T, LANES)), b3((1, NSA_HEADS * T, LANES)),
                  pl.BlockSpec((1, 1, 2, NSA_KV, w_src), lambda b: (b, layer, 0, 0, 0)),
                  b3((1, LANES, LANES)), b3((1, LANES, LANES))],
        out_specs=b3((1, T, NSA_QPAD)),
        out_shape=jax.ShapeDtypeStruct((DB, T, NSA_QPAD), BF16),
        compiler_params=_params(("parallel",)),
        name="nsa_dec_win",
    )(zb, zd, o_c, o_s, win_t, new_rows(4), new_rows(5))


def _merge_kernel(x_ref, hm_ref, hn_ref, hx_ref, wg_ref, bg_ref, wbm_ref, wbn_ref, wbx_ref, wo_ref, g_ref, b_ref, o_ref):
    x = x_ref[...]
    xb = x.astype(BF16)
    merged = None
    for c, (h_ref, w_ref) in enumerate(((hm_ref, wbm_ref), (hn_ref, wbn_ref), (hx_ref, wbx_ref))):
        sl = slice(c * D_MODEL, (c + 1) * D_MODEL)
        gate = jax.nn.sigmoid(_dot(xb, wg_ref[:, sl]) + bg_ref[:, sl])
        term = gate * _dot(h_ref[...], w_ref[...])
        merged = term if merged is None else merged + term
    mix = _dot(merged.astype(BF16), wo_ref[...])
    o_ref[...] = _layer_norm(ALPHA * x + mix, g_ref[...], b_ref[...])


def merge(x2d, hm, hn, hx, wg, bg, wbr, wo, g, b, tm):
    n = x2d.shape[0]
    row = lambda a: pl.BlockSpec((tm, a.shape[1]), lambda i: (i, 0))
    full = lambda a: pl.BlockSpec(a.shape, lambda i: (0,) * a.ndim)
    return pl.pallas_call(
        _merge_kernel,
        grid=(n // tm,),
        in_specs=[row(x2d), row(hm), row(hn), row(hx), full(wg), full(bg), full(wbr[0]), full(wbr[1]), full(wbr[2]),
                  full(wo), full(g), full(b)],
        out_specs=row(x2d),
        out_shape=jax.ShapeDtypeStruct((n, D_MODEL), F32),
        compiler_params=_params(("parallel",)),
        name="merge",
    )(x2d, hm, hn, hx, wg, bg, wbr[0], wbr[1], wbr[2], wo, g, b)


def _ffn_kernel(x_ref, wg_ref, wu_ref, wd_ref, g_ref, b_ref, o_ref, acc_s):
    f = pl.program_id(1)

    @pl.when(f == 0)
    def _():
        acc_s[...] = jnp.zeros(acc_s.shape, F32)

    xb = x_ref[...].astype(BF16)
    hcur = jax.nn.silu(_dot(xb, wg_ref[...])) * _dot(xb, wu_ref[...])
    acc_s[...] += _dot(hcur.astype(BF16), wd_ref[...])

    @pl.when(f == pl.num_programs(1) - 1)
    def _():
        o_ref[...] = _layer_norm(ALPHA * x_ref[...] + acc_s[...], g_ref[...], b_ref[...])


def ffn(x2d, w_up, w_down, g, b, tm, tf):
    n = x2d.shape[0]
    nf = (w_up.shape[1] // 2) // tf
    return pl.pallas_call(
        _ffn_kernel,
        grid=(n // tm, nf),
        in_specs=[pl.BlockSpec((tm, D_MODEL), lambda i, f: (i, 0)),
                  pl.BlockSpec((D_MODEL, tf), lambda i, f: (0, f)),
                  pl.BlockSpec((D_MODEL, tf), lambda i, f: (0, nf + f)),
                  pl.BlockSpec((tf, D_MODEL), lambda i, f: (f, 0)),
                  pl.BlockSpec((1, D_MODEL), lambda i, f: (0, 0)),
                  pl.BlockSpec((1, D_MODEL), lambda i, f: (0, 0))],
        out_specs=pl.BlockSpec((tm, D_MODEL), lambda i, f: (i, 0)),
        out_shape=jax.ShapeDtypeStruct((n, D_MODEL), F32),
        scratch_shapes=[pltpu.VMEM((tm, D_MODEL), F32)],
        compiler_params=_params(("parallel", "arbitrary")),
        name="dense_ffn",
    )(x2d, w_up, w_up, w_down, g, b)


def _router_kernel(x_ref, w_ref, b_ref, o_ref):
    logits = _dot(x_ref[...], w_ref[...], HIGHEST) + b_ref[...]
    lane = lax.broadcasted_iota(jnp.int32, logits.shape, 1)
    W = logits.shape[1]
    logits = jnp.where(lane < N_EXPERTS, logits, NEG_INF)
    m1 = jnp.max(logits, axis=-1, keepdims=True)
    i1 = jnp.min(jnp.where(logits == m1, lane, W), axis=-1, keepdims=True)
    rest = jnp.where(lane == i1, NEG_INF, logits)
    m2 = jnp.max(rest, axis=-1, keepdims=True)
    i2 = jnp.min(jnp.where(rest == m2, lane, W), axis=-1, keepdims=True)
    e2 = jnp.exp(m2 - m1)
    den = 1.0 + e2
    o_ref[...] = jnp.where(lane == i1, 1.0 / den, 0.0) + jnp.where(lane == i2, e2 / den, 0.0)


def router(x2d, w_router, b_router, tm):
    n = x2d.shape[0]
    w = jnp.pad(w_router.astype(F32), ((0, 0), (0, LANES - N_EXPERTS)))
    bb = jnp.pad(b_router.astype(F32), (0, LANES - N_EXPERTS)).reshape(1, LANES)
    return pl.pallas_call(
        _router_kernel,
        grid=(n // tm,),
        in_specs=[pl.BlockSpec((tm, D_MODEL), lambda i: (i, 0)), pl.BlockSpec(w.shape, lambda i: (0, 0)),
                  pl.BlockSpec(bb.shape, lambda i: (0, 0))],
        out_specs=pl.BlockSpec((tm, LANES), lambda i: (i, 0)),
        out_shape=jax.ShapeDtypeStruct((n, LANES), F32),
        compiler_params=_params(("parallel",)),
        name="router",
    )(x2d, w, bb)


def _moe_kernel(np_ref, x_ref, cw_ref, cwt_ref, wg_ref, wu_ref, wd_ref, g_ref, b_ref, o_ref,
                xb_s, y_s, pos_s, post_s, xg_s, ws_s, acc_s, *, blk, cap, cap_pad):
    j = pl.program_id(0)
    e = pl.program_id(1)
    f = pl.program_id(2)
    n_e = pl.num_programs(1)
    n_f = pl.num_programs(2)
    n_pass = np_ref[j * n_e + e]
    RC = min(256, blk)

    @pl.when((e == 0) & (f == 0))
    def _():
        xb_s[...] = x_ref[...].astype(BF16)
        y_s[...] = jnp.zeros(y_s.shape, F32)
        cw = cw_ref[...]
        cwt = cwt_ref[...]
        routed = jnp.where(cw != 0.0, 1.0, 0.0).astype(BF16)
        routed_t = jnp.where(cwt != 0.0, 1.0, 0.0).astype(BF16)
        for rc in range(blk // RC):
            rows_i = rc * RC + lax.broadcasted_iota(jnp.int32, (RC, blk), 0)
            cols_i = lax.broadcasted_iota(jnp.int32, (RC, blk), 1)
            before = jnp.where(cols_i < rows_i, 1.0, 0.0).astype(BF16)
            cnt = _dot(before, routed)
            pos_s[rc * RC:(rc + 1) * RC, :] = jnp.where(cw[rc * RC:(rc + 1) * RC] != 0.0, cnt, -1.0)
            rows_j = lax.broadcasted_iota(jnp.int32, (blk, RC), 0)
            cols_j = rc * RC + lax.broadcasted_iota(jnp.int32, (blk, RC), 1)
            before_t = jnp.where(rows_j < cols_j, 1.0, 0.0).astype(BF16)
            cnt_t = _dot(routed_t, before_t)
            post_s[:, rc * RC:(rc + 1) * RC] = jnp.where(cwt[:, rc * RC:(rc + 1) * RC] != 0.0, cnt_t, -1.0)

    @pl.when(f == 0)
    def _():
        prow = post_s[pl.ds(e, 1), :]
        wrow = cwt_ref[pl.ds(e, 1), :]

        def gather(u, carry):
            rr = lax.broadcasted_iota(jnp.int32, (cap_pad, 1), 0)
            slot = jnp.where(rr < cap, u * cap + rr, NO_SLOT).astype(F32)
            hit = prow == slot
            xg_s[u] = _dot(jnp.where(hit, 1.0, 0.0).astype(BF16), xb_s[...]).astype(BF16)
            ws_s[u] = jnp.sum(jnp.where(hit, wrow, 0.0), axis=-1, keepdims=True)
            acc_s[u] = jnp.zeros(acc_s.shape[1:], F32)
            return carry

        lax.fori_loop(0, n_pass, gather, 0)

    def expert(u, carry):
        xg = xg_s[u, 0:cap, :]
        hcur = jax.nn.silu(_dot(xg, wg_ref[0])) * _dot(xg, wu_ref[0]) * ws_s[u, 0:cap, :]
        acc_s[u, 0:cap, :] += _dot(hcur.astype(BF16), wd_ref[0])
        return carry

    lax.fori_loop(0, n_pass, expert, 0)

    @pl.when(f == n_f - 1)
    def _():
        lane = lax.broadcasted_iota(jnp.int32, pos_s.shape, 1)
        pcol = jnp.sum(jnp.where(lane == e, pos_s[...], 0.0), axis=-1, keepdims=True)

        def scatter(u, carry):
            cc = lax.broadcasted_iota(jnp.int32, (1, cap_pad), 1)
            slot = jnp.where(cc < cap, u * cap + cc, NO_SLOT).astype(F32)
            hit = jnp.where(pcol == slot, 1.0, 0.0).astype(BF16)
            y_s[...] += _dot(hit, acc_s[u].astype(BF16))
            return carry

        lax.fori_loop(0, n_pass, scatter, 0)

    @pl.when((e == n_e - 1) & (f == n_f - 1))
    def _():
        o_ref[...] = _layer_norm(ALPHA * x_ref[...] + y_s[...], g_ref[...], b_ref[...])


def moe(x2d, cw, w_up, w_down, g, b, tf):
    n = x2d.shape[0]
    E, _, F2 = w_up.shape
    nf = (F2 // 2) // tf
    blk = _row_tile(n, MOE_BLOCK)
    cap = min(MOE_CAP, blk)
    cap_pad = -(-cap // LANES) * LANES
    max_pass = -(-blk // cap)
    nblk = n // blk
    cwt = jnp.swapaxes(cw[:, :ROUTE_ROWS], 0, 1)
    counts = jnp.sum((cw[:, :E] != 0.0).reshape(nblk, blk, E), axis=1)
    n_pass = ((counts + cap - 1) // cap).astype(jnp.int32).reshape(-1)
    return pl.pallas_call(
        functools.partial(_moe_kernel, blk=blk, cap=cap, cap_pad=cap_pad),
        grid_spec=pltpu.PrefetchScalarGridSpec(
            num_scalar_prefetch=1, grid=(nblk, E, nf),
            in_specs=[pl.BlockSpec((blk, D_MODEL), lambda j, e, f, npr: (j, 0)),
                      pl.BlockSpec((blk, LANES), lambda j, e, f, npr: (j, 0)),
                      pl.BlockSpec((ROUTE_ROWS, blk), lambda j, e, f, npr: (0, j)),
                      pl.BlockSpec((1, D_MODEL, tf), lambda j, e, f, npr: (e, 0, f)),
                      pl.BlockSpec((1, D_MODEL, tf), lambda j, e, f, npr: (e, 0, nf + f)),
                      pl.BlockSpec((1, tf, D_MODEL), lambda j, e, f, npr: (e, f, 0)),
                      pl.BlockSpec((1, D_MODEL), lambda j, e, f, npr: (0, 0)),
                      pl.BlockSpec((1, D_MODEL), lambda j, e, f, npr: (0, 0))],
            out_specs=pl.BlockSpec((blk, D_MODEL), lambda j, e, f, npr: (j, 0)),
            scratch_shapes=[pltpu.VMEM((blk, D_MODEL), BF16), pltpu.VMEM((blk, D_MODEL), F32),
                            pltpu.VMEM((blk, LANES), F32), pltpu.VMEM((ROUTE_ROWS, blk), F32),
                            pltpu.VMEM((max_pass, cap_pad, D_MODEL), BF16), pltpu.VMEM((max_pass, cap_pad, 1), F32),
                            pltpu.VMEM((max_pass, cap_pad, D_MODEL), F32)]),
        out_shape=jax.ShapeDtypeStruct((n, D_MODEL), F32),
        compiler_params=_params(("parallel", "arbitrary", "arbitrary")),
        name="moe_ffn",
    )(n_pass, x2d, cw, cwt, w_up, w_up, w_down, g, b)


def _layer(x, lw, l, *, mem_kv, mem_blk, mstate, decode):
    B, T, _ = x.shape
    n = B * T
    x2d = x.reshape(n, D_MODEL)
    tm = _row_tile(n, 512)
    za, zb, zc, zcb, zd = in_proj(x2d, lw["in"], tm)
    za, zb, zc, zcb, zd = (a.reshape(B, T, -1) for a in (za, zb, zc, zcb, zd))

    if T % MLSTM_CHUNK == 0:
        h_m, ct, nt, mt = mlstm(za, zd, *mstate, lw["norm_g"], L=MLSTM_CHUNK, t_valid=MLSTM_CHUNK)
    else:
        padt = lambda a: jnp.pad(a, ((0, 0), (0, LANES - T), (0, 0)))
        h_m, ct, nt, mt = mlstm(padt(za), padt(zd), *mstate, lw["norm_g"], L=LANES, t_valid=T)
        h_m = h_m[:, :T]

    if decode is None:
        kc, vc = cmp_tokens(zc, lw["cmp_w"], lw["cmp_b"])
        h_n = nsa_prompt(zb, zd, zcb, kc, vc, tq=min(256, T), CH=min(512, T))
    else:
        h_n = nsa_decode(zb, zd, zcb, decode["nsa"], decode["win"], decode["pt"], lw["cmp_w"], lw["cmp_b"], l)

    h_x = mem_attn(zb, mem_kv, mem_blk[0], mem_blk[1], tq=_row_tile(T, 512))

    wg, bg = lw["in"]["mg"]
    flat = lambda a: a.reshape(n, -1)
    x1 = merge(x2d, flat(h_m), flat(h_n), flat(h_x), wg, bg, lw["w_branch"], lw["w_out"], lw["ln1_g"], lw["ln1_b"], tm)

    if l % 2 == 0:
        x2 = ffn(x1, lw["ffn_up"], lw["ffn_down"], lw["ln2_g"], lw["ln2_b"], _row_tile(n, 1024), 256)
    else:
        cw = router(x1, lw["w_router"], lw["b_router"], _row_tile(n, 1024))
        x2 = moe(x1, cw, lw["ffn_up"], lw["ffn_down"], lw["ln2_g"], lw["ln2_b"], 896)
    return x2.reshape(B, T, D_MODEL), zc, (ct, nt, mt)


def kernel(x_prompt, x_sample, mem_prompt, cache_nsa_kv, cache_win_kv, state_mlstm_C, state_mlstm_n,
           state_mlstm_m, cache_mem_kv, page_table, w_in, b_in, mlstm_norm_g, cmp_w, cmp_b, w_mem_kv,
           w_branch, w_out, ln1_g, ln1_b, ln2_g, ln2_b, ffn_w_up, ffn_w_down, moe_w_router, moe_b_router,
           moe_w_up, moe_w_down):
    B, T, _ = x_prompt.shape
    DB, TS, _ = x_sample.shape
    xp, xs = x_prompt, x_sample
    nsa_p, nsa_s, win_p, win_s = [], [], [], []
    Cp, np_, mp, Cs, ns, ms, memkv_p = [], [], [], [], [], [], []
    mem_s = cache_mem_kv.reshape(DB, MEM_LEN, DEPTH * 2 * X_WIDTH)
    row1 = lambda a: a.reshape(1, -1).astype(F32)
    for l in range(DEPTH):
        wbr = (w_branch[l, 0].astype(BF16), _pad_branch_rows(w_branch[l, 1]).astype(BF16), w_branch[l, 2].astype(BF16))
        lw = dict(norm_g=mlstm_norm_g[l], cmp_w=cmp_w[l], cmp_b=cmp_b[l], w_branch=wbr,
                  w_out=w_out[l].astype(BF16), ln1_g=row1(ln1_g[l]), ln1_b=row1(ln1_b[l]),
                  ln2_g=row1(ln2_g[l]), ln2_b=row1(ln2_b[l]))
        lw["in"] = _split_in_proj(w_in[l], b_in[l])
        if l % 2 == 0:
            lw["ffn_up"] = ffn_w_up[l // 2].astype(BF16)
            lw["ffn_down"] = ffn_w_down[l // 2].astype(BF16)
        else:
            lw["ffn_up"] = moe_w_up[l // 2].astype(BF16)
            lw["ffn_down"] = moe_w_down[l // 2].astype(BF16)
            lw["w_router"] = moe_w_router[l // 2]
            lw["b_router"] = moe_b_router[l // 2]

        mkv = proj(mem_prompt.reshape(B * MEM_LEN, D_MODEL), w_mem_kv[l].astype(BF16), _row_tile(B * MEM_LEN, 512))
        mkv = mkv.reshape(B, MEM_LEN, 2 * X_WIDTH)
        st0 = (jnp.zeros((B, M_HEADS, M_DH, M_DH), F32), jnp.zeros((B, M_HEADS, M_DH), F32), jnp.zeros((B, M_HEADS), F32))
        xp, zc, st = _layer(xp, lw, l, mem_kv=mkv, mem_blk=(0, 1), mstate=st0, decode=None)
        kvn = zc.reshape(B, T, 6, NSA_GROUPS, NSA_DH)
        nsa_p.append(kvn[:, :, :4])
        win_p.append(kvn[:, T - min(WINDOW, T):, 4:])
        Cp.append(st[0]); np_.append(st[1]); mp.append(st[2])
        memkv_p.append(mkv.reshape(B, MEM_LEN, 2, X_HEADS, X_DH))

        sts = (state_mlstm_C[:, l], state_mlstm_n[:, l], state_mlstm_m[:, l])
        dec = dict(nsa=cache_nsa_kv, win=cache_win_kv, pt=page_table)
        xs, zc, st = _layer(xs, lw, l, mem_kv=mem_s, mem_blk=(2 * l, 2 * l + 1), mstate=sts, decode=dec)
        kvn = zc.reshape(DB, TS, 6, NSA_GROUPS, NSA_DH)
        nsa_s.append(kvn[:, :, :4])
        win_s.append(jnp.concatenate([cache_win_kv[:, :, l].astype(F32), kvn[:, :, 4:]], axis=1)[:, TS:])
        Cs.append(st[0]); ns.append(st[1]); ms.append(st[2])
    return (xp, xs,
            jnp.stack(nsa_p, axis=2), jnp.stack(nsa_s, axis=2),
            jnp.stack(win_p, axis=2), jnp.stack(win_s, axis=2),
            jnp.stack(Cp, axis=1), jnp.stack(np_, axis=1), jnp.stack(mp, axis=1),
            jnp.stack(Cs, axis=1), jnp.stack(ns, axis=1), jnp.stack(ms, axis=1),
            jnp.stack(memkv_p, axis=2))
```

```python
import functools

import numpy as np
import jax
import jax.numpy as jnp
from jax import lax
from jax.experimental import pallas as pl
from jax.experimental.pallas import tpu as pltpu

D_MODEL = 1024
DEPTH = 2
BRANCH_WIDTH = 512
N_BRANCH = 3
M_HEADS = 4
M_DH = BRANCH_WIDTH // M_HEADS
M_WIDTH = M_HEADS * M_DH
NSA_HEADS = 8
NSA_DH = BRANCH_WIDTH // NSA_HEADS
NSA_WIDTH = NSA_HEADS * NSA_DH
NSA_GROUPS = 2
NSA_HPG = NSA_HEADS // NSA_GROUPS
NSA_KV = NSA_GROUPS * NSA_DH
CMP_LEN = 32
CMP_STRIDE = 16
SEL_LEN = 64
SEL_TOPN = 16
WINDOW = 512
MEM_LEN = 256
X_HEADS = 4
X_DH = BRANCH_WIDTH // X_HEADS
X_WIDTH = X_HEADS * X_DH
D_FF = 2816
N_EXPERTS = 8
TOP_K = 2
D_FF_EXPERT = 3584
ALPHA = (2.0 * DEPTH) ** 0.25
LN_EPS = 1e-5
IN_SPLITS = (M_WIDTH, M_WIDTH, M_WIDTH, M_HEADS, M_HEADS, M_WIDTH,
             NSA_WIDTH, 6 * NSA_KV, 3 * NSA_HEADS, X_WIDTH, N_BRANCH * D_MODEL)
LOG2E = 1.4426950408889634
NSA_SLOPES = tuple(LOG2E * 2.0 ** (-8.0 * (h + 1) / NSA_HEADS) for h in range(NSA_HEADS))
MASK_BIAS = -(2.0 ** 30)
SEL_FEAT = 64
SEL_QUARTERS = 4
SLOPE_PARTS = 3

LANES = 128
SUBLANES = 8
VMEM_LIMIT = 56 * 1024 * 1024
PAGES_PER_STEP = 32
MLSTM_SEQS = 1
MLSTM_CHUNK = 256
GATE_COLS = LANES
NG_OFF = 2 * M_HEADS
NSA_QPAD = NSA_HEADS * LANES
MOE_BLOCK = 1024
MOE_CAP = 288
NO_SLOT = -2
ROUTE_ROWS = 16

F32 = jnp.float32
BF16 = jnp.bfloat16
NEG_INF = float("-inf")
HIGHEST = lax.Precision.HIGHEST


def _dot(a, b, precision=None):
    return jnp.dot(a, b, preferred_element_type=F32, precision=precision)


def _dot_nt(a, b, precision=None):
    return lax.dot_general(a, b, (((1,), (1,)), ((), ())), preferred_element_type=F32, precision=precision)


def _dot_tn(a, b):
    return lax.dot_general(a, b, (((0,), (0,)), ((), ())), preferred_element_type=F32)


def _params(sem):
    return pltpu.CompilerParams(dimension_semantics=sem, vmem_limit_bytes=VMEM_LIMIT)


def _masked_softmax2(s, mask):
    s = jnp.where(mask, s, NEG_INF)
    mx = jnp.max(s, axis=-1, keepdims=True)
    mx = jnp.where(mx > NEG_INF, mx, 0.0)
    p = jnp.where(mask, jnp.exp2(s - mx), 0.0)
    return p / jnp.maximum(jnp.sum(p, axis=-1, keepdims=True), 1e-30)


def _layer_norm(xf, g, b):
    mu = jnp.mean(xf, axis=-1, keepdims=True)
    var = jnp.mean(jnp.square(xf - mu), axis=-1, keepdims=True)
    return (xf - mu) * lax.rsqrt(var + LN_EPS) * g + b


def _row_tile(n, pref):
    return pref if n % pref == 0 else n


def _in_proj_kernel(x_ref, wa_ref, wb_ref, wc_ref, wd_ref, ba_ref, bb_ref, bc_ref, bd_ref,
                    oa_ref, ob_ref, oc_ref, ocb_ref, od_ref):
    x = x_ref[...].astype(BF16)
    oa_ref[...] = (_dot(x, wa_ref[...]) + ba_ref[...]).astype(BF16)
    ob_ref[...] = (_dot(x, wb_ref[...]) + bb_ref[...]).astype(BF16)
    c = _dot(x, wc_ref[...]) + bc_ref[...]
    oc_ref[...] = c
    ocb_ref[...] = c.astype(BF16)
    od_ref[...] = _dot(x, wd_ref[...]) + bd_ref[...]


def _group_select():
    return jax.nn.one_hot(np.arange(NSA_HEADS) // NSA_HPG, NSA_GROUPS, dtype=F32)


def _pad_branch_rows(w_nsa):
    return jnp.einsum("hem,hg->hgem", w_nsa.reshape(NSA_HEADS, NSA_DH, -1), _group_select()).reshape(NSA_QPAD, -1)


def _split_in_proj(w_in_l, b_in_l):
    offs = np.cumsum((0,) + IN_SPLITS)
    w = [w_in_l[:, offs[i]:offs[i + 1]] for i in range(len(IN_SPLITS))]
    b = [b_in_l[offs[i]:offs[i + 1]] for i in range(len(IN_SPLITS))]
    mq, mk, mv, mi, mf, mo, nq, nkv, ng, xq, mg = range(11)
    pad = GATE_COLS - 2 * M_HEADS - 3 * NSA_HEADS

    def cat(ids, zpad=0):
        ww = jnp.concatenate([w[i] for i in ids], axis=1)
        bb = jnp.concatenate([b[i] for i in ids])
        if zpad:
            ww = jnp.pad(ww, ((0, 0), (0, zpad)))
            bb = jnp.pad(bb, (0, zpad))
        return ww.astype(BF16), bb.reshape(1, -1).astype(F32)

    scale = NSA_DH ** -0.5 * LOG2E
    w[nq] = jnp.einsum("dhe,hg->dhge", w[nq].reshape(D_MODEL, NSA_HEADS, NSA_DH) * scale, _group_select()).reshape(D_MODEL, NSA_QPAD)
    b[nq] = jnp.einsum("he,hg->hge", b[nq].reshape(NSA_HEADS, NSA_DH) * scale, _group_select()).reshape(NSA_QPAD)
    return dict(a=cat([mq, mk, mv, mo]), b=cat([nq, xq]), c=cat([nkv]), d=cat([mi, mf, ng], pad), mg=cat([mg]))


def in_proj(x2d, wp, tm):
    n = x2d.shape[0]
    (wa, ba), (wb, bb), (wc, bc), (wd, bd) = wp["a"], wp["b"], wp["c"], wp["d"]
    full = lambda arr: pl.BlockSpec(arr.shape, lambda i: (0, 0), pipeline_mode=pl.Buffered(1))
    row = lambda w: pl.BlockSpec((tm, w), lambda i: (i, 0))
    return pl.pallas_call(
        _in_proj_kernel,
        grid=(n // tm,),
        in_specs=[row(D_MODEL), full(wa), full(wb), full(wc), full(wd), full(ba), full(bb), full(bc), full(bd)],
        out_specs=[row(wa.shape[1]), row(wb.shape[1]), row(wc.shape[1]), row(wc.shape[1]), row(wd.shape[1])],
        out_shape=[jax.ShapeDtypeStruct((n, wa.shape[1]), BF16), jax.ShapeDtypeStruct((n, wb.shape[1]), BF16),
                   jax.ShapeDtypeStruct((n, wc.shape[1]), F32), jax.ShapeDtypeStruct((n, wc.shape[1]), BF16),
                   jax.ShapeDtypeStruct((n, wd.shape[1]), F32)],
        compiler_params=_params(("parallel",)),
        name="in_proj",
    )(x2d, wa, wb, wc, wd, ba, bb, bc, bd)


def _proj_kernel(x_ref, w_ref, o_ref):
    o_ref[...] = _dot(x_ref[...].astype(BF16), w_ref[...])


def proj(x2d, w_bf16, tm):
    n, k = x2d.shape
    m = w_bf16.shape[1]
    return pl.pallas_call(
        _proj_kernel,
        grid=(n // tm,),
        in_specs=[pl.BlockSpec((tm, k), lambda i: (i, 0)), pl.BlockSpec((k, m), lambda i: (0, 0))],
        out_specs=pl.BlockSpec((tm, m), lambda i: (i, 0)),
        out_shape=jax.ShapeDtypeStruct((n, m), F32),
        compiler_params=_params(("parallel",)),
        name="mem_proj",
    )(x2d, w_bf16)


def _log_sigmoid(x):
    return jnp.minimum(x, 0.0) - jnp.log1p(jnp.exp(-jnp.abs(x)))


def _mlstm_kernel(q_ref, k_ref, v_ref, og_ref, gc_ref, gr_ref, c0_ref, n0_ref, m0_ref, ng_ref,
                  h_ref, ct_ref, nt_ref, mt_ref, c_s, n_s, m_s, *, L, t_valid, nb):
    ci = pl.program_id(1)

    @pl.when(ci == 0)
    def _():
        c_s[...] = c0_ref[...]
        n_s[...] = n0_ref[...]
        m_s[...] = m0_ref[...]

    row = lax.broadcasted_iota(jnp.int32, (L, L), 0)
    col = lax.broadcasted_iota(jnp.int32, (L, L), 1)
    causal = row >= col
    tri = causal.astype(F32)
    tri_t = (row <= col).astype(F32)
    rvalid = lax.broadcasted_iota(jnp.int32, (L, GATE_COLS), 0) < t_valid
    cvalid = lax.broadcasted_iota(jnp.int32, (2 * M_HEADS, L), 1) < t_valid
    scale = M_DH ** -0.5

    for bb, h in [(bb, h) for bb in range(nb) for h in range(M_HEADS)]:
        if h == 0:
            gc = gc_ref[bb]
            gr = gr_ref[bb]
            lf_c = jnp.where(rvalid, _log_sigmoid(gc), 0.0)
            lf_r = jnp.where(cvalid, _log_sigmoid(gr), 0.0)
            ig_c = jnp.where(rvalid, gc, NEG_INF)
            ig_r = jnp.where(cvalid, gr, NEG_INF)
            b_c = _dot(tri, lf_c, HIGHEST)
            b_r = _dot(lf_r, tri_t, HIGHEST)
        sl = slice(h * M_DH, (h + 1) * M_DH)
        q = q_ref[bb, :, sl]
        k = k_ref[bb, :, sl]
        v = v_ref[bb, :, sl]
        bc = b_c[:, M_HEADS + h:M_HEADS + h + 1]
        br = b_r[M_HEADS + h:M_HEADS + h + 1, :]
        igc = ig_c[:, h:h + 1]
        igr = ig_r[h:h + 1, :]
        m_prev = m_s[bb, h, 0:1, 0:1]
        c_prev = c_s[bb, h]
        n_prev = n_s[bb, h:h + 1, :]

        dmat = jnp.where(causal, bc - br + igr, NEG_INF)
        inter = bc + m_prev
        m_t = jnp.maximum(inter, jnp.max(dmat, axis=-1, keepdims=True))
        s = _dot_nt(q, k) * scale * jnp.exp(dmat - m_t)
        sc_in = jnp.exp(inter - m_t)
        qf = q.astype(F32)
        num = sc_in * _dot_nt(q, c_prev.astype(BF16)) + _dot(s.astype(BF16), v)
        den = sc_in * jnp.sum(qf * n_prev, axis=-1, keepdims=True) + jnp.sum(s, axis=-1, keepdims=True)
        hh = num / jnp.maximum(jnp.abs(den), jnp.exp(-m_t))

        b_last = bc[L - 1:L, :]
        dec_c = b_last - bc + igc
        dec_r = b_last - br + igr
        m_new = jnp.maximum(b_last + m_prev, jnp.max(dec_r, axis=-1, keepdims=True))
        ws_c = jnp.exp(dec_c - m_new) * scale
        sc = jnp.exp(b_last + m_prev - m_new)
        vf = v.astype(F32)
        kf = k.astype(F32)
        c_new = sc * c_prev + _dot_tn((vf * ws_c).astype(BF16), k)
        n_new = sc * n_prev + jnp.sum(kf * ws_c, axis=0, keepdims=True)
        c_s[bb, h] = c_new
        n_s[bb, h:h + 1, :] = n_new
        m_s[bb, h] = jnp.broadcast_to(m_new, m_s.shape[2:])

        og = og_ref[bb, :, sl].astype(F32)
        hh = hh * jax.nn.sigmoid(og)
        mu = jnp.mean(hh, axis=-1, keepdims=True)
        var = jnp.mean(jnp.square(hh - mu), axis=-1, keepdims=True)
        hn = (hh - mu) * lax.rsqrt(var + LN_EPS) * ng_ref[:, sl]
        h_ref[bb, :, sl] = hn.astype(h_ref.dtype)

    ct_ref[...] = c_s[...]
    nt_ref[...] = n_s[...]
    mt_ref[...] = m_s[...]


def mlstm(za, zd, c0, n0, m0, norm_g, *, L, t_valid):
    B, T, _ = za.shape
    nc = T // L
    gr = jnp.swapaxes(zd[:, :, :2 * M_HEADS], 1, 2)
    m0b = jnp.broadcast_to(m0.astype(F32)[:, :, None, None], (B, M_HEADS, SUBLANES, LANES))
    nb = MLSTM_SEQS if B % MLSTM_SEQS == 0 else 1
    colspec = lambda j: pl.BlockSpec((nb, L, M_WIDTH), lambda b, c, j=j: (b, c, j))
    st = lambda shp: pl.BlockSpec((nb,) + shp, lambda b, c: (b,) + (0,) * len(shp))
    kern = functools.partial(_mlstm_kernel, L=L, t_valid=t_valid, nb=nb)
    h, ct, nt, mt = pl.pallas_call(
        kern,
        grid=(B // nb, nc),
        in_specs=[colspec(0), colspec(1), colspec(2), colspec(3),
                  pl.BlockSpec((nb, L, GATE_COLS), lambda b, c: (b, c, 0)),
                  pl.BlockSpec((nb, 2 * M_HEADS, L), lambda b, c: (b, 0, c)),
                  st((M_HEADS, M_DH, M_DH)), st((M_HEADS, M_DH)), st((M_HEADS, SUBLANES, LANES)),
                  pl.BlockSpec((1, M_WIDTH), lambda b, c: (0, 0))],
        out_specs=[pl.BlockSpec((nb, L, M_WIDTH), lambda b, c: (b, c, 0)),
                   st((M_HEADS, M_DH, M_DH)), st((M_HEADS, M_DH)), st((M_HEADS, SUBLANES, LANES))],
        out_shape=[jax.ShapeDtypeStruct((B, T, M_WIDTH), BF16),
                   jax.ShapeDtypeStruct((B, M_HEADS, M_DH, M_DH), F32),
                   jax.ShapeDtypeStruct((B, M_HEADS, M_DH), F32),
                   jax.ShapeDtypeStruct((B, M_HEADS, SUBLANES, LANES), F32)],
        scratch_shapes=[pltpu.VMEM((nb, M_HEADS, M_DH, M_DH), F32), pltpu.VMEM((nb, M_HEADS, M_DH), F32),
                        pltpu.VMEM((nb, M_HEADS, SUBLANES, LANES), F32)],
        compiler_params=_params(("parallel", "arbitrary")),
        name="mlstm",
    )(za, za, za, za, zd, gr, c0.astype(F32), n0.astype(F32), m0b, norm_g.reshape(1, M_WIDTH).astype(F32))
    return h, ct, nt, mt[:, :, 0, 0]


def _mem_attn_kernel(q_ref, k_ref, v_ref, o_ref):
    scale = X_DH ** -0.5
    for h in range(X_HEADS):
        sl = slice(h * X_DH, (h + 1) * X_DH)
        s = _dot_nt(q_ref[0, :, sl], k_ref[:, h, :].astype(BF16)) * scale
        mx = jnp.max(s, axis=-1, keepdims=True)
        p = jnp.exp(s - mx)
        p = p / jnp.sum(p, axis=-1, keepdims=True)
        o_ref[0, :, sl] = _dot(p.astype(BF16), v_ref[:, h, :].astype(BF16)).astype(o_ref.dtype)


def mem_attn(zb, kv, layer, tq):
    B, T, _ = zb.shape
    kv_blk = (None, MEM_LEN, None, None, X_HEADS, X_DH)
    return pl.pallas_call(
        _mem_attn_kernel,
        grid=(B, T // tq),
        in_specs=[pl.BlockSpec((1, tq, X_WIDTH), lambda b, i: (b, i, NSA_QPAD // X_WIDTH)),
                  pl.BlockSpec(kv_blk, lambda b, i: (b, 0, layer, 0, 0, 0)),
                  pl.BlockSpec(kv_blk, lambda b, i: (b, 0, layer, 1, 0, 0))],
        out_specs=pl.BlockSpec((1, tq, X_WIDTH), lambda b, i: (b, i, 0)),
        out_shape=jax.ShapeDtypeStruct((B, T, X_WIDTH), BF16),
        compiler_params=_params(("parallel", "parallel")),
        name="mem_attn",
    )(zb, kv, kv)


def _head_rows(q_ref, nq):
    parts = [q_ref[0, :, hd * LANES:(hd + 1) * LANES] for hd in range(NSA_HEADS)]
    if nq % 16:
        return jnp.concatenate([p.astype(F32) for p in parts], axis=0).astype(BF16)
    return jnp.concatenate(parts, axis=0)


def _store_heads(o_ref, gate, o_c, o_s, o_w, nq):
    lane = lax.broadcasted_iota(jnp.int32, (nq, LANES), 1)
    for hd in range(NSA_HEADS):
        r = slice(hd * nq, (hd + 1) * nq)
        gcol = lambda br: gate[:, NG_OFF + br * NSA_HEADS + hd:NG_OFF + br * NSA_HEADS + hd + 1]
        val = gcol(0) * o_c[r] + gcol(1) * o_s[r] + gcol(2) * o_w[r]
        keep = (lane >= NSA_DH) if hd // NSA_HPG == 1 else (lane < NSA_DH)
        o_ref[0, :, hd * LANES:(hd + 1) * LANES] = jnp.where(keep, val, 0.0).astype(o_ref.dtype)


def _overlap(tok, blk):
    c_start = tok * CMP_STRIDE
    s_start = blk * SEL_LEN
    return ((c_start < s_start + SEL_LEN) & (c_start + CMP_LEN > s_start)).astype(F32)


def _select_blocks(imp, tpos, n_sel, n_top):
    nq, W = imp.shape
    blk = lax.broadcasted_iota(jnp.int32, (nq, W), 1)
    cur = tpos // SEL_LEN
    forced = (blk == 0) | (blk == cur) | (blk == cur - 1)
    v = jnp.where(forced, jnp.inf, jnp.where(blk <= cur, imp, NEG_INF))
    v = jnp.where(blk < n_sel, v, NEG_INF)
    ahead = jnp.zeros((nq, W), F32)
    for j in range(n_sel):
        vj = v[:, j:j + 1]
        ahead = ahead + jnp.where(vj > v, 1.0, jnp.where(vj == v, jnp.where(blk > j, 1.0, 0.0), 0.0))
    return jnp.where((ahead < n_top) & (blk < n_sel), 1.0, 0.0)


def _select_blocks_t(imp_t, tpos_row, n_sel, n_top):
    n_blk, nq = imp_t.shape
    blk = lax.broadcasted_iota(jnp.int32, (n_blk, nq), 0)
    cur = tpos_row // SEL_LEN
    forced = (blk == 0) | (blk == cur) | (blk == cur - 1)
    v = jnp.where(forced, jnp.inf, jnp.where(blk <= cur, imp_t, NEG_INF))
    v = jnp.where(blk < n_sel, v, NEG_INF)
    n_rg = n_blk // SUBLANES
    vg = [v[rg * SUBLANES:(rg + 1) * SUBLANES] for rg in range(n_rg)]
    bg = [rg * SUBLANES + lax.broadcasted_iota(jnp.int32, (SUBLANES, nq), 0) for rg in range(n_rg)]
    ahead = [jnp.zeros((SUBLANES, nq), F32) for _ in range(n_rg)]
    for j in range(n_sel):
        vj = v[j:j + 1, :]
        for rg in range(n_rg):
            if rg * SUBLANES > j:
                inc = jnp.where(vj >= vg[rg], 1.0, 0.0)
            elif (rg + 1) * SUBLANES - 1 <= j:
                inc = jnp.where(vj > vg[rg], 1.0, 0.0)
            else:
                inc = jnp.where(vj > vg[rg], 1.0, jnp.where(vj == vg[rg], jnp.where(bg[rg] > j, 1.0, 0.0), 0.0))
            ahead[rg] = ahead[rg] + inc
    ahead = jnp.concatenate(ahead, axis=0)
    return jnp.where((ahead < n_top) & (blk < n_sel), 1.0, 0.0)


def _online_update(s, m_old, l_old):
    m_new = jnp.maximum(m_old, jnp.max(s, axis=-1, keepdims=True))
    alpha = jnp.exp2(m_old - m_new)
    p = jnp.exp2(s - m_new)
    return p, m_new, alpha, alpha * l_old + jnp.sum(p, axis=-1, keepdims=True)


def _cmp_tokens_kernel(xk_ref, xv_ref, bd_ref, b_ref, kc_ref, vc_ref, *, nsub):
    for c, (x_ref, o_ref) in enumerate(((xk_ref, kc_ref), (xv_ref, vc_ref))):
        a0 = jnp.zeros((nsub, LANES), F32)
        a1 = jnp.zeros((nsub, LANES), F32)
        for j in range(CMP_STRIDE):
            xj = x_ref[0, :, j, :].astype(BF16)
            a0 = a0 + _dot(xj, bd_ref[c, 0, j])
            a1 = a1 + _dot(xj, bd_ref[c, 1, j])
        tok = a0 + pltpu.roll(a1, nsub - 1, axis=0) + b_ref[c]
        o_ref[0] = tok.astype(o_ref.dtype)


def _cmp_blockdiag(cmp_w_l):
    R = CMP_LEN // CMP_STRIDE
    w = cmp_w_l.astype(F32).reshape(2, R, CMP_STRIDE, NSA_DH, NSA_DH)
    eye = jnp.eye(NSA_GROUPS, dtype=F32)
    return jnp.einsum("ab,crjde->crjadbe", eye, w).reshape(2, R, CMP_STRIDE, LANES, LANES).astype(BF16)


def _cmp_bias(cmp_b_l):
    return jnp.tile(cmp_b_l.astype(F32), (1, NSA_GROUPS)).reshape(2, 1, LANES)


def cmp_tokens(zc, cmp_w_l, cmp_b_l):
    B, T, W = zc.shape
    nsub = T // CMP_STRIDE
    x4 = zc.reshape(B, nsub, CMP_STRIDE, W)
    bd = _cmp_blockdiag(cmp_w_l)
    bias = _cmp_bias(cmp_b_l)
    spec = lambda j: pl.BlockSpec((1, nsub, CMP_STRIDE, LANES), lambda b, j=j: (b, 0, 0, j))
    return pl.pallas_call(
        functools.partial(_cmp_tokens_kernel, nsub=nsub),
        grid=(B,),
        in_specs=[spec(0), spec(1), pl.BlockSpec(bd.shape, lambda b: (0,) * 5), pl.BlockSpec(bias.shape, lambda b: (0, 0, 0))],
        out_specs=[pl.BlockSpec((1, nsub, LANES), lambda b: (b, 0, 0))] * 2,
        out_shape=[jax.ShapeDtypeStruct((B, nsub, LANES), BF16)] * 2,
        compiler_params=_params(("parallel",)),
        name="nsa_cmp_tokens",
    )(x4, x4, bd, bias)


def _nsa_prompt_kernel(q_ref, g_ref, kca_ref, vc_ref, ka_ref, vst_ref, kwa_ref, vw_ref, qf_ref, o_ref,
                       qaug_s, s_buf, p_buf, w_buf, pw_buf, pc_s, m_s, l_s, a_s, acc_s, oc_s, ow_s, *, tq, T, CH):
    start = pl.program_id(1) * tq
    n_cmp_rows = kca_ref.shape[1]
    n_sel = T // SEL_LEN
    n_top = min(SEL_TOPN, n_sel)
    n_blk = -(-n_sel // SUBLANES) * SUBLANES
    nq_all = NSA_HEADS * tq
    qw = nq_all // SEL_QUARTERS
    tpos_row = start + lax.broadcasted_iota(jnp.int32, (1, tq), 1)
    tpos_all = jnp.concatenate([tpos_row] * NSA_HEADS, axis=1)
    qcols = lambda qi: slice(qi * qw, (qi + 1) * qw)
    tiles = [(qi, ct, slice(qi * qw + ct * LANES, qi * qw + (ct + 1) * LANES), slice(ct * LANES, (ct + 1) * LANES))
             for qi in range(SEL_QUARTERS) for ct in range(qw // LANES)]

    qaug_s[:, 0:LANES] = _head_rows(q_ref, tq)
    for hd in range(NSA_HEADS):
        qaug_s[hd * tq:(hd + 1) * tq, LANES:2 * LANES] = jnp.broadcast_to(qf_ref[hd:hd + 1, :], (tq, LANES)).astype(BF16)

    def softmax_tile(s_t):
        mx = jnp.max(s_t, axis=0, keepdims=True)
        mx = jnp.where(mx > NEG_INF, mx, 0.0)
        p_t = jnp.exp2(s_t - mx)
        return p_t, jnp.maximum(jnp.sum(p_t, axis=0, keepdims=True), 1e-30)

    c_end = lax.broadcasted_iota(jnp.int32, (n_cmp_rows, 1), 0) * CMP_STRIDE + CMP_LEN - 1
    for qi in range(SEL_QUARTERS):
        s_buf[qi, 0:n_cmp_rows, :] = _dot_nt(kca_ref[0], qaug_s[qcols(qi), :])
    for qi, ct, cols, tc in tiles:
        p_t, den = softmax_tile(jnp.where(c_end <= tpos_all[:, cols], s_buf[qi, 0:n_cmp_rows, tc], NEG_INF))
        pc_s[:, cols] = p_t / den
    oc_s[...] = _dot_tn(vc_ref[0], pc_s[...].astype(BF16))
    ov_t = _overlap(lax.broadcasted_iota(jnp.int32, (n_blk, n_cmp_rows), 1),
                    lax.broadcasted_iota(jnp.int32, (n_blk, n_cmp_rows), 0))
    lane = lax.broadcasted_iota(jnp.int32, (tq, LANES), 1)
    feat_sel = [None] * NSA_HEADS
    for g in range(NSA_GROUPS):
        pg = pc_s[:, g * NSA_HPG * tq:(g * NSA_HPG + 1) * tq]
        for p in range(1, NSA_HPG):
            pg = pg + pc_s[:, (g * NSA_HPG + p) * tq:(g * NSA_HPG + p + 1) * tq]
        sel_t = _select_blocks_t(_dot(ov_t, pg, HIGHEST), tpos_row, n_sel, n_top)
        sel = jnp.concatenate([sel_t, jnp.zeros((LANES - n_blk, tq), F32)], axis=0).T
        unsel = jnp.where(lane < SEL_FEAT, (1.0 - sel) * MASK_BIAS, 0.0)
        for p in range(NSA_HPG):
            hd = g * NSA_HPG + p
            feat_sel[hd] = (unsel + qf_ref[hd:hd + 1, :]).astype(BF16)

    wk = WINDOW + tq
    base_w = pl.multiple_of(start, tq)
    wpos = start - WINDOW + lax.broadcasted_iota(jnp.int32, (wk, 1), 0)
    for qi in range(SEL_QUARTERS):
        w_buf[qi] = _dot_nt(kwa_ref[0, pl.ds(base_w, wk), :], qaug_s[qcols(qi), :])
    for qi, ct, cols, tc in tiles:
        tp = tpos_all[:, cols]
        first = jnp.maximum(tp - (WINDOW - 1), 0)
        s_t = jnp.where(wpos >= first, jnp.where(wpos <= tp, w_buf[qi, :, tc], NEG_INF), NEG_INF)
        p_t, den = softmax_tile(s_t)
        pw_buf[qi, :, tc] = p_t.astype(BF16)
        a_s[:, cols] = den
    for qi in range(SEL_QUARTERS):
        ow_s[:, qcols(qi)] = _dot_tn(vw_ref[0, pl.ds(base_w, wk), :], pw_buf[qi]) / a_s[:, qcols(qi)]

    for hd in range(NSA_HEADS):
        qaug_s[hd * tq:(hd + 1) * tq, LANES:2 * LANES] = feat_sel[hd]

    m_s[...] = jnp.full(m_s.shape, NEG_INF, F32)
    l_s[...] = jnp.zeros(l_s.shape, F32)
    acc_s[...] = jnp.zeros(acc_s.shape, F32)

    def issue(c, qi):
        base = pl.multiple_of(c * CH, CH)
        s_buf[qi] = _dot_nt(ka_ref[0, pl.ds(base, CH), :], qaug_s[qi * qw:(qi + 1) * qw, :])

    def absorb(c, qi, diag):
        for ct in range(qw // LANES):
            cols = slice(qi * qw + ct * LANES, qi * qw + (ct + 1) * LANES)
            s_t = s_buf[qi, :, ct * LANES:(ct + 1) * LANES]
            if diag:
                kpos = c * CH + lax.broadcasted_iota(jnp.int32, (CH, 1), 0)
                s_t = jnp.where(kpos <= tpos_all[:, cols], s_t, NEG_INF)
            m_old = m_s[:, cols]
            m_new = jnp.maximum(m_old, jnp.max(s_t, axis=0, keepdims=True))
            alpha = jnp.exp2(m_old - m_new)
            p_t = jnp.exp2(s_t - m_new)
            m_s[:, cols] = m_new
            a_s[:, cols] = alpha
            l_s[:, cols] = alpha * l_s[:, cols] + jnp.sum(p_t, axis=0, keepdims=True)
            p_buf[qi, :, ct * LANES:(ct + 1) * LANES] = p_t.astype(BF16)
        cols = slice(qi * qw, (qi + 1) * qw)
        acc_s[:, cols] = a_s[:, cols] * acc_s[:, cols] + _dot(vst_ref[0, c], p_buf[qi])

    def full_chunk(c, carry):
        for qi in range(SEL_QUARTERS):
            absorb(c, qi, False)
            issue(c + 1, qi)
        return carry

    n_full = start // CH
    for qi in range(SEL_QUARTERS):
        issue(0, qi)
    lax.fori_loop(0, n_full, full_chunk, 0)
    for qi in range(SEL_QUARTERS):
        absorb(n_full, qi, True)

    gate_t = jax.nn.sigmoid(g_ref[0]).T
    for hd in range(NSA_HEADS):
        cols = slice(hd * tq, (hd + 1) * tq)
        grow = lambda br: gate_t[NG_OFF + br * NSA_HEADS + hd:NG_OFF + br * NSA_HEADS + hd + 1, :]
        o_s = acc_s[:, cols] / jnp.maximum(l_s[:, cols], 1e-30)
        val = (grow(0) * oc_s[:, cols] + grow(1) * o_s + grow(2) * ow_s[:, cols]).T
        keep = (lane >= NSA_DH) if hd // NSA_HPG == 1 else (lane < NSA_DH)
        o_ref[0, :, hd * LANES:(hd + 1) * LANES] = jnp.where(keep, val, 0.0).astype(o_ref.dtype)


def _bf16_parts(x, n):
    parts = []
    for _ in range(n):
        p = float(np.asarray(x, np.float32).astype(jnp.bfloat16).astype(np.float32))
        parts.append(p)
        x = x - p
    return parts


def _slope_features():
    qf = np.zeros((NSA_HEADS, LANES), np.float32)
    for hd in range(NSA_HEADS):
        for p, s_p in enumerate(_bf16_parts(NSA_SLOPES[hd], SLOPE_PARTS)):
            qf[hd, SEL_FEAT + 2 * p] = s_p * SEL_LEN
            qf[hd, SEL_FEAT + 2 * p + 1] = s_p
    return jnp.asarray(qf)


def _key_features(pos, block_onehot):
    pos = np.asarray(pos)
    ok = pos >= 0
    kf = np.zeros((pos.shape[0], LANES), np.float32)
    if block_onehot:
        kf[np.arange(pos.shape[0])[ok], pos[ok] // SEL_LEN] = 1.0
    for p in range(SLOPE_PARTS):
        kf[ok, SEL_FEAT + 2 * p] = pos[ok] // SEL_LEN
        kf[ok, SEL_FEAT + 2 * p + 1] = pos[ok] % SEL_LEN
    return jnp.asarray(kf, BF16)


def nsa_prompt(zb, zd, zcb, kc, vc, tq, CH):
    B, T, _ = zb.shape
    nsub = kc.shape[1]
    assert T % CH == 0 and CH % tq == 0 and T // SEL_LEN <= SEL_FEAT and nsub <= CH
    qf = _slope_features()
    with_feats = lambda k, feats: jnp.concatenate([k, jnp.broadcast_to(feats[None], (B,) + feats.shape)], axis=2)
    k_aug = with_feats(zcb[:, :, 2 * NSA_KV:3 * NSA_KV], _key_features(np.arange(T), True))
    kc_aug = with_feats(kc, _key_features(np.arange(nsub) * CMP_STRIDE + CMP_LEN - 1, False))
    kw = jnp.pad(zcb[:, :, 4 * NSA_KV:5 * NSA_KV], ((0, 0), (WINDOW, 0), (0, 0)))
    kw_aug = with_feats(kw, _key_features(np.arange(T + WINDOW) - WINDOW, False))
    vs_t = jnp.swapaxes(zcb[:, :, 3 * NSA_KV:4 * NSA_KV].reshape(B, T // CH, CH, NSA_KV), 2, 3)
    vw = jnp.pad(zcb[:, :, 5 * NSA_KV:6 * NSA_KV], ((0, 0), (WINDOW, 0), (0, 0)))
    per_b = lambda rows_, w: pl.BlockSpec((1, rows_, w), lambda b, i: (b, 0, 0))
    kern = functools.partial(_nsa_prompt_kernel, tq=tq, T=T, CH=CH)
    nq_all = NSA_HEADS * tq
    qw = nq_all // SEL_QUARTERS
    wk = WINDOW + tq
    assert qw % LANES == 0
    return pl.pallas_call(
        kern,
        grid=(B, T // tq),
        in_specs=[pl.BlockSpec((1, tq, NSA_QPAD), lambda b, i: (b, i, 0)),
                  pl.BlockSpec((1, tq, GATE_COLS), lambda b, i: (b, i, 0)),
                  per_b(nsub, 2 * LANES), per_b(nsub, LANES),
                  per_b(T, 2 * LANES),
                  pl.BlockSpec((1, T // CH, NSA_KV, CH), lambda b, i: (b, 0, 0, 0)),
                  per_b(T + WINDOW, 2 * LANES), per_b(T + WINDOW, LANES),
                  pl.BlockSpec(qf.shape, lambda b, i: (0, 0))],
        out_specs=pl.BlockSpec((1, tq, NSA_QPAD), lambda b, i: (b, i, 0)),
        out_shape=jax.ShapeDtypeStruct((B, T, NSA_QPAD), BF16),
        scratch_shapes=[pltpu.VMEM((nq_all, 2 * LANES), BF16),
                        pltpu.VMEM((SEL_QUARTERS, CH, qw), F32), pltpu.VMEM((SEL_QUARTERS, CH, qw), BF16),
                        pltpu.VMEM((SEL_QUARTERS, wk, qw), F32), pltpu.VMEM((SEL_QUARTERS, wk, qw), BF16),
                        pltpu.VMEM((nsub, nq_all), F32),
                        pltpu.VMEM((1, nq_all), F32), pltpu.VMEM((1, nq_all), F32), pltpu.VMEM((1, nq_all), F32),
                        pltpu.VMEM((NSA_KV, nq_all), F32), pltpu.VMEM((NSA_KV, nq_all), F32),
                        pltpu.VMEM((NSA_KV, nq_all), F32)],
        compiler_params=_params(("parallel", "arbitrary")),
        name="nsa_prompt",
    )(zb, zd, kc_aug, vc, k_aug, vs_t, kw_aug, vw, qf)


def _page_specs(shape, slot_blk, layer, n_pages):
    def mk(i):
        def imap(b, s, pt):
            return (pt[b * n_pages + s * PAGES_PER_STEP + i], layer, slot_blk, 0, 0)
        return pl.BlockSpec(shape, imap)
    return [mk(i) for i in range(PAGES_PER_STEP)]


def _nsa_dec_cmp_kernel(pt_ref, *refs, T, past_len, n_rows):
    pages = refs[:PAGES_PER_STEP]
    perm_ref, bd_ref, b_ref, q_ref, oc_ref, sel_ref, tok_s, pend_s = refs[PAGES_PER_STEP:]
    s = pl.program_id(1)
    n_steps = pl.num_programs(1)
    page_rows = pages[0].shape[-1]
    sub = page_rows // CMP_STRIDE
    R = PAGES_PER_STEP * sub

    @pl.when(s == 0)
    def _():
        pend_s[...] = jnp.zeros(pend_s.shape, F32)

    rid = lax.broadcasted_iota(jnp.int32, (R, LANES), 0)
    for c in range(2):
        xs = [_dot_nt(perm_ref[...], pg[0, 0, c].astype(BF16)) for pg in pages]
        a0 = jnp.zeros((R, LANES), F32)
        a1 = jnp.zeros((R, LANES), F32)
        for j in range(CMP_STRIDE):
            xj = jnp.concatenate([x[j * sub:(j + 1) * sub] for x in xs], axis=0).astype(BF16)
            a0 = a0 + _dot(xj, bd_ref[c, 0, j])
            a1 = a1 + _dot(xj, bd_ref[c, 1, j])
        a0 = a0 + b_ref[c]
        tok = jnp.where(rid == 0, pend_s[c], pltpu.roll(a0, 1, axis=0)) + a1
        tok_s[c, pl.ds(pl.multiple_of(s * R, R), R), :] = tok
        pend_s[c] = a0[R - 1:R, :]

    @pl.when(s == n_steps - 1)
    def _():
        qp = _head_rows(q_ref, T)
        kc = tok_s[0].astype(BF16)
        vc = tok_s[1].astype(BF16)
        sc_all = _dot_nt(qp, kc)
        tpos = past_len + lax.broadcasted_iota(jnp.int32, (T, 1), 0)
        r = lax.broadcasted_iota(jnp.int32, (T, n_rows), 1)
        dc = tpos - ((r - 1) * CMP_STRIDE + CMP_LEN - 1)
        mask_c = (dc >= 0) & (r >= 1)
        dcf = dc.astype(F32)
        pcs = [_masked_softmax2(sc_all[hd * T:(hd + 1) * T] - NSA_SLOPES[hd] * dcf, mask_c) for hd in range(NSA_HEADS)]
        oc_ref[0] = _dot(jnp.concatenate(pcs, axis=0).astype(BF16), vc)
        W = sel_ref.shape[2]
        n_sel = -(-(past_len + T) // SEL_LEN)
        ov = _overlap(lax.broadcasted_iota(jnp.int32, (n_rows, W), 0) - 1, lax.broadcasted_iota(jnp.int32, (n_rows, W), 1))
        for g in range(NSA_GROUPS):
            pg = pcs[g * NSA_HPG]
            for p in range(1, NSA_HPG):
                pg = pg + pcs[g * NSA_HPG + p]
            imp = _dot(pg, ov, HIGHEST)
            sel_ref[0, g * T:(g + 1) * T, :] = _select_blocks(imp, tpos, n_sel, min(SEL_TOPN, n_sel))


def _nsa_dec_sel_kernel(pt_ref, *refs, T, past_len):
    pages = refs[:PAGES_PER_STEP]
    q_ref, sel_ref, sel_last_ref, ex_ref, kn_ref, vn_ref, os_ref, m_s, l_s, acc_s = refs[PAGES_PER_STEP:]
    s = pl.program_id(1)
    n_steps = pl.num_programs(1)
    page_rows = pages[0].shape[-1]
    CH = PAGES_PER_STEP * page_rows

    @pl.when(s == 0)
    def _():
        m_s[...] = jnp.full(m_s.shape, NEG_INF, F32)
        l_s[...] = jnp.zeros(l_s.shape, F32)
        acc_s[...] = jnp.zeros(acc_s.shape, F32)

    qp = _head_rows(q_ref, T)
    tpos = past_len + lax.broadcasted_iota(jnp.int32, (T, 1), 0)

    def update(s_all, mask_of_group, kpos, pv):
        ds = tpos - kpos
        dsf = ds.astype(F32)
        ps = []
        for g in range(NSA_GROUPS):
            mk = mask_of_group(g) & (ds >= 0)
            for p in range(NSA_HPG):
                hd = g * NSA_HPG + p
                r = slice(hd * T, (hd + 1) * T)
                sc = jnp.where(mk, s_all[r] - NSA_SLOPES[hd] * dsf, NEG_INF)
                pr, m_new, alpha, l_new = _online_update(sc, m_s[r], l_s[r])
                m_s[r] = m_new
                l_s[r] = l_new
                acc_s[r] = alpha * acc_s[r]
                ps.append(pr)
        acc_s[...] = acc_s[...] + pv(jnp.concatenate(ps, axis=0).astype(BF16))

    selk = _dot(sel_ref[0, 0].astype(BF16), ex_ref[...])
    s_all = jnp.concatenate([_dot(qp, pg[0, 0, 0].astype(BF16)) for pg in pages], axis=1)
    kpos = s * CH + lax.broadcasted_iota(jnp.int32, (T, CH), 1)

    def pv_pages(pmat):
        out = jnp.zeros((NSA_HEADS * T, LANES), F32)
        for i, pg in enumerate(pages):
            out = out + _dot_nt(pmat[:, i * page_rows:(i + 1) * page_rows], pg[0, 0, 1].astype(BF16))
        return out

    update(s_all, lambda g: selk[g * T:(g + 1) * T] > 0.5, kpos, pv_pages)

    @pl.when(s == n_steps - 1)
    def _():
        nk = kn_ref.shape[1]
        sn = _dot_nt(qp, kn_ref[0])
        lane = lax.broadcasted_iota(jnp.int32, (T, nk), 1)
        sl = sel_last_ref[0, 0]
        update(sn, lambda g: (sl[g * T:(g + 1) * T, 0:1] > 0.5) & (lane < T), past_len + lane,
               lambda pmat: _dot(pmat, vn_ref[0]))
        os_ref[0] = acc_s[...] / jnp.maximum(l_s[...], 1e-30)


def _nsa_dec_win_kernel(q_ref, g_ref, oc_ref, os_ref, wp_ref, kwn_ref, vwn_ref, o_ref, *, T, past_len):
    qp = _head_rows(q_ref, T)
    w_src = wp_ref.shape[-1]
    nk = kwn_ref.shape[1]
    sw_all = jnp.concatenate([_dot(qp, wp_ref[0, 0, 0].astype(BF16)), _dot_nt(qp, kwn_ref[0])], axis=1)
    j = lax.broadcasted_iota(jnp.int32, (T, w_src + nk), 1)
    tpos = past_len + lax.broadcasted_iota(jnp.int32, (T, 1), 0)
    wpos = past_len - w_src + j
    dw = tpos - wpos
    mask_w = (dw >= 0) & (dw < WINDOW) & (wpos >= 0) & (j < w_src + T)
    dwf = dw.astype(F32)
    pws = [_masked_softmax2(sw_all[hd * T:(hd + 1) * T] - NSA_SLOPES[hd] * dwf, mask_w) for hd in range(NSA_HEADS)]
    pw = jnp.concatenate(pws, axis=0).astype(BF16)
    o_w = _dot_nt(pw[:, :w_src], wp_ref[0, 0, 1].astype(BF16)) + _dot(pw[:, w_src:], vwn_ref[0])
    _store_heads(o_ref, jax.nn.sigmoid(g_ref[0]), oc_ref[0], os_ref[0], o_w, T)


def nsa_decode(zb, zd, zcb, cache_nsa_kv, cache_win_kv, page_table, cmp_w_l, cmp_b_l, layer):
    DB, T, _ = zb.shape
    n_pool, page_rows = cache_nsa_kv.shape[:2]
    n_pages = page_table.shape[1]
    past_len = n_pages * page_rows
    n_steps = n_pages // PAGES_PER_STEP
    sub_per_page = page_rows // CMP_STRIDE
    n_rows = n_pages * sub_per_page
    assert (n_rows - 1) * CMP_STRIDE + CMP_LEN - 1 > past_len + T - 1
    assert T <= CMP_STRIDE and n_pages % PAGES_PER_STEP == 0 and page_rows == LANES
    pt = page_table.reshape(-1).astype(jnp.int32)
    cache_t = jnp.transpose(cache_nsa_kv, (0, 2, 3, 4, 5, 1)).reshape(n_pool, DEPTH, 4, NSA_KV, page_rows)
    page_blk = (1, 1, 2, NSA_KV, page_rows)

    rr = np.arange(page_rows)
    perm = jnp.asarray(rr[None, :] == ((rr % sub_per_page) * CMP_STRIDE + rr // sub_per_page)[:, None], BF16)
    bd = _cmp_blockdiag(cmp_w_l)
    bias = _cmp_bias(cmp_b_l)

    n_sel = -(-(past_len + T) // SEL_LEN)
    blocks_per_step = PAGES_PER_STEP * page_rows // SEL_LEN
    sel_used = (n_steps + 1) * blocks_per_step
    sel_w = -(-sel_used // LANES) * LANES
    assert sel_used >= n_sel and blocks_per_step <= LANES
    q_spec = pl.BlockSpec((1, T, NSA_QPAD), lambda b, s, pt: (b, 0, 0))
    const = lambda a: pl.BlockSpec(a.shape, lambda b, s, pt: (0,) * a.ndim)

    o_c, sel = pl.pallas_call(
        functools.partial(_nsa_dec_cmp_kernel, T=T, past_len=past_len, n_rows=n_rows),
        grid_spec=pltpu.PrefetchScalarGridSpec(
            num_scalar_prefetch=1, grid=(DB, n_steps),
            in_specs=_page_specs(page_blk, 0, layer, n_pages) + [const(perm), const(bd), const(bias), q_spec],
            out_specs=[pl.BlockSpec((1, NSA_HEADS * T, LANES), lambda b, s, pt: (b, 0, 0)),
                       pl.BlockSpec((1, NSA_GROUPS * T, sel_w), lambda b, s, pt: (b, 0, 0))],
            scratch_shapes=[pltpu.VMEM((2, n_rows, LANES), F32), pltpu.VMEM((2, 1, LANES), F32)]),
        out_shape=[jax.ShapeDtypeStruct((DB, NSA_HEADS * T, LANES), F32),
                   jax.ShapeDtypeStruct((DB, NSA_GROUPS * T, sel_w), F32)],
        compiler_params=_params(("parallel", "arbitrary")),
        name="nsa_dec_cmp",
    )(pt, *([cache_t] * PAGES_PER_STEP), perm, bd, bias, zb)

    sel_steps = sel[:, :, :sel_used].reshape(DB, NSA_GROUPS * T, n_steps + 1, blocks_per_step).transpose(0, 2, 1, 3)
    sel_steps = jnp.pad(sel_steps, ((0, 0), (0, 0), (0, 0), (0, LANES - blocks_per_step)))
    kk = np.arange(PAGES_PER_STEP * page_rows) // SEL_LEN
    expand = jnp.asarray(kk[None, :] == np.arange(LANES)[:, None], BF16)
    pad_rows = LANES - T
    new_rows = lambda slot: jnp.pad(zcb[:, :, slot * NSA_KV:(slot + 1) * NSA_KV], ((0, 0), (0, pad_rows), (0, 0)))
    new_spec = pl.BlockSpec((1, LANES, LANES), lambda b, s, pt: (b, 0, 0))
    sel_blk = (1, 1, NSA_GROUPS * T, LANES)

    o_s = pl.pallas_call(
        functools.partial(_nsa_dec_sel_kernel, T=T, past_len=past_len),
        grid_spec=pltpu.PrefetchScalarGridSpec(
            num_scalar_prefetch=1, grid=(DB, n_steps),
            in_specs=_page_specs(page_blk, 1, layer, n_pages)
            + [q_spec, pl.BlockSpec(sel_blk, lambda b, s, pt: (b, s, 0, 0)),
               pl.BlockSpec(sel_blk, lambda b, s, pt: (b, n_steps, 0, 0)), const(expand), new_spec, new_spec],
            out_specs=pl.BlockSpec((1, NSA_HEADS * T, LANES), lambda b, s, pt: (b, 0, 0)),
            scratch_shapes=[pltpu.VMEM((NSA_HEADS * T, 1), F32), pltpu.VMEM((NSA_HEADS * T, 1), F32),
                            pltpu.VMEM((NSA_HEADS * T, LANES), F32)]),
        out_shape=jax.ShapeDtypeStruct((DB, NSA_HEADS * T, LANES), F32),
        compiler_params=_params(("parallel", "arbitrary")),
        name="nsa_dec_sel",
    )(pt, *([cache_t] * PAGES_PER_STEP), zb, sel_steps, sel_steps, expand, new_rows(2), new_rows(3))

    w_src = cache_win_kv.shape[1]
    win_t = jnp.transpose(cache_win_kv, (0, 2, 3, 4, 5, 1)).reshape(DB, DEPTH, 2, NSA_KV, w_src)
    b3 = lambda shp: pl.BlockSpec(shp, lambda b: (b, 0, 0))
    return pl.pallas_call(
        functools.partial(_nsa_dec_win_kernel, T=T, past_len=past_len),
        grid=(DB,),
        in_specs=[b3((1, T, NSA_QPAD)), b3((1, T, GATE_COLS)), b3((1, NSA_HEADS * T, LANES)), b3((1, NSA_HEADS * T, LANES)),
                  pl.BlockSpec((1, 1, 2, NSA_KV, w_src), lambda b: (b, layer, 0, 0, 0)),
                  b3((1, LANES, LANES)), b3((1, LANES, LANES))],
        out_specs=b3((1, T, NSA_QPAD)),
        out_shape=jax.ShapeDtypeStruct((DB, T, NSA_QPAD), BF16),
        compiler_params=_params(("parallel",)),
        name="nsa_dec_win",
    )(zb, zd, o_c, o_s, win_t, new_rows(4), new_rows(5))


def _merge_kernel(x_ref, hm_ref, hn_ref, hx_ref, wg_ref, bg_ref, wbm_ref, wbn_ref, wbx_ref, wo_ref, g_ref, b_ref, o_ref):
    x = x_ref[...]
    xb = x.astype(BF16)
    merged = None
    for c, (h_ref, w_ref) in enumerate(((hm_ref, wbm_ref), (hn_ref, wbn_ref), (hx_ref, wbx_ref))):
        sl = slice(c * D_MODEL, (c + 1) * D_MODEL)
        gate = jax.nn.sigmoid(_dot(xb, wg_ref[:, sl]) + bg_ref[:, sl])
        term = gate * _dot(h_ref[...], w_ref[...])
        merged = term if merged is None else merged + term
    mix = _dot(merged.astype(BF16), wo_ref[...])
    o_ref[...] = _layer_norm(ALPHA * x + mix, g_ref[...], b_ref[...])


def merge(x2d, hm, hn, hx, wg, bg, wbr, wo, g, b, tm):
    n = x2d.shape[0]
    row = lambda a: pl.BlockSpec((tm, a.shape[1]), lambda i: (i, 0))
    full = lambda a: pl.BlockSpec(a.shape, lambda i: (0,) * a.ndim, pipeline_mode=pl.Buffered(1))
    return pl.pallas_call(
        _merge_kernel,
        grid=(n // tm,),
        in_specs=[row(x2d), row(hm), row(hn), row(hx), full(wg), full(bg), full(wbr[0]), full(wbr[1]), full(wbr[2]),
                  full(wo), full(g), full(b)],
        out_specs=row(x2d),
        out_shape=jax.ShapeDtypeStruct((n, D_MODEL), F32),
        compiler_params=_params(("parallel",)),
        name="merge",
    )(x2d, hm, hn, hx, wg, bg, wbr[0], wbr[1], wbr[2], wo, g, b)


def _ffn_kernel(x_ref, wg_ref, wu_ref, wd_ref, g_ref, b_ref, o_ref, acc_s):
    f = pl.program_id(1)

    @pl.when(f == 0)
    def _():
        acc_s[...] = jnp.zeros(acc_s.shape, F32)

    xb = x_ref[...].astype(BF16)
    hcur = jax.nn.silu(_dot(xb, wg_ref[...])) * _dot(xb, wu_ref[...])
    acc_s[...] += _dot(hcur.astype(BF16), wd_ref[...])

    @pl.when(f == pl.num_programs(1) - 1)
    def _():
        o_ref[...] = _layer_norm(ALPHA * x_ref[...] + acc_s[...], g_ref[...], b_ref[...])


def ffn(x2d, w_up, w_down, g, b, tm, tf):
    n = x2d.shape[0]
    nf = (w_up.shape[1] // 2) // tf
    return pl.pallas_call(
        _ffn_kernel,
        grid=(n // tm, nf),
        in_specs=[pl.BlockSpec((tm, D_MODEL), lambda i, f: (i, 0)),
                  pl.BlockSpec((D_MODEL, tf), lambda i, f: (0, f)),
                  pl.BlockSpec((D_MODEL, tf), lambda i, f: (0, nf + f)),
                  pl.BlockSpec((tf, D_MODEL), lambda i, f: (f, 0)),
                  pl.BlockSpec((1, D_MODEL), lambda i, f: (0, 0)),
                  pl.BlockSpec((1, D_MODEL), lambda i, f: (0, 0))],
        out_specs=pl.BlockSpec((tm, D_MODEL), lambda i, f: (i, 0)),
        out_shape=jax.ShapeDtypeStruct((n, D_MODEL), F32),
        scratch_shapes=[pltpu.VMEM((tm, D_MODEL), F32)],
        compiler_params=_params(("parallel", "arbitrary")),
        name="dense_ffn",
    )(x2d, w_up, w_up, w_down, g, b)


def _router_kernel(x_ref, w_ref, b_ref, o_ref):
    logits = _dot(x_ref[...], w_ref[...], HIGHEST) + b_ref[...]
    lane = lax.broadcasted_iota(jnp.int32, logits.shape, 1)
    W = logits.shape[1]
    logits = jnp.where(lane < N_EXPERTS, logits, NEG_INF)
    m1 = jnp.max(logits, axis=-1, keepdims=True)
    i1 = jnp.min(jnp.where(logits == m1, lane, W), axis=-1, keepdims=True)
    rest = jnp.where(lane == i1, NEG_INF, logits)
    m2 = jnp.max(rest, axis=-1, keepdims=True)
    i2 = jnp.min(jnp.where(rest == m2, lane, W), axis=-1, keepdims=True)
    e2 = jnp.exp(m2 - m1)
    den = 1.0 + e2
    o_ref[...] = jnp.where(lane == i1, 1.0 / den, 0.0) + jnp.where(lane == i2, e2 / den, 0.0)


def router(x2d, w_router, b_router, tm):
    n = x2d.shape[0]
    w = jnp.pad(w_router.astype(F32), ((0, 0), (0, LANES - N_EXPERTS)))
    bb = jnp.pad(b_router.astype(F32), (0, LANES - N_EXPERTS)).reshape(1, LANES)
    return pl.pallas_call(
        _router_kernel,
        grid=(n // tm,),
        in_specs=[pl.BlockSpec((tm, D_MODEL), lambda i: (i, 0)), pl.BlockSpec(w.shape, lambda i: (0, 0)),
                  pl.BlockSpec(bb.shape, lambda i: (0, 0))],
        out_specs=pl.BlockSpec((tm, LANES), lambda i: (i, 0)),
        out_shape=jax.ShapeDtypeStruct((n, LANES), F32),
        compiler_params=_params(("parallel",)),
        name="router",
    )(x2d, w, bb)


def _moe_kernel(np_ref, x_ref, cw_ref, cwt_ref, wg_ref, wu_ref, wd_ref, g_ref, b_ref, o_ref,
                xb_s, y_s, pos_s, post_s, xg_s, ws_s, acc_s, *, blk, cap, cap_pad):
    j = pl.program_id(0)
    e = pl.program_id(1)
    f = pl.program_id(2)
    n_e = pl.num_programs(1)
    n_f = pl.num_programs(2)
    n_pass = np_ref[j * n_e + e]
    RC = min(256, blk)

    @pl.when((e == 0) & (f == 0))
    def _():
        xb_s[...] = x_ref[...].astype(BF16)
        y_s[...] = jnp.zeros(y_s.shape, F32)
        cw = cw_ref[...]
        cwt = cwt_ref[...]
        routed = jnp.where(cw != 0.0, 1.0, 0.0).astype(BF16)
        routed_t = jnp.where(cwt != 0.0, 1.0, 0.0).astype(BF16)
        for rc in range(blk // RC):
            rows_i = rc * RC + lax.broadcasted_iota(jnp.int32, (RC, blk), 0)
            cols_i = lax.broadcasted_iota(jnp.int32, (RC, blk), 1)
            before = jnp.where(cols_i < rows_i, 1.0, 0.0).astype(BF16)
            cnt = _dot(before, routed)
            pos_s[rc * RC:(rc + 1) * RC, :] = jnp.where(cw[rc * RC:(rc + 1) * RC] != 0.0, cnt, -1.0)
            rows_j = lax.broadcasted_iota(jnp.int32, (blk, RC), 0)
            cols_j = rc * RC + lax.broadcasted_iota(jnp.int32, (blk, RC), 1)
            before_t = jnp.where(rows_j < cols_j, 1.0, 0.0).astype(BF16)
            cnt_t = _dot(routed_t, before_t)
            post_s[:, rc * RC:(rc + 1) * RC] = jnp.where(cwt[:, rc * RC:(rc + 1) * RC] != 0.0, cnt_t, -1.0)

    @pl.when(f == 0)
    def _():
        prow = post_s[pl.ds(e, 1), :]
        wrow = cwt_ref[pl.ds(e, 1), :]

        def gather(u, carry):
            rr = lax.broadcasted_iota(jnp.int32, (cap_pad, 1), 0)
            slot = jnp.where(rr < cap, u * cap + rr, NO_SLOT).astype(F32)
            hit = prow == slot
            xg_s[u] = _dot(jnp.where(hit, 1.0, 0.0).astype(BF16), xb_s[...]).astype(BF16)
            ws_s[u] = jnp.sum(jnp.where(hit, wrow, 0.0), axis=-1, keepdims=True)
            acc_s[u] = jnp.zeros(acc_s.shape[1:], F32)
            return carry

        lax.fori_loop(0, n_pass, gather, 0)

    def expert(u, carry):
        xg = xg_s[u, 0:cap, :]
        hcur = jax.nn.silu(_dot(xg, wg_ref[0])) * _dot(xg, wu_ref[0]) * ws_s[u, 0:cap, :]
        acc_s[u, 0:cap, :] += _dot(hcur.astype(BF16), wd_ref[0])
        return carry

    lax.fori_loop(0, n_pass, expert, 0)

    @pl.when(f == n_f - 1)
    def _():
        lane = lax.broadcasted_iota(jnp.int32, pos_s.shape, 1)
        pcol = jnp.sum(jnp.where(lane == e, pos_s[...], 0.0), axis=-1, keepdims=True)

        def scatter(u, carry):
            cc = lax.broadcasted_iota(jnp.int32, (1, cap_pad), 1)
            slot = jnp.where(cc < cap, u * cap + cc, NO_SLOT).astype(F32)
            hit = jnp.where(pcol == slot, 1.0, 0.0).astype(BF16)
            y_s[...] += _dot(hit, acc_s[u].astype(BF16))
            return carry

        lax.fori_loop(0, n_pass, scatter, 0)

    @pl.when((e == n_e - 1) & (f == n_f - 1))
    def _():
        o_ref[...] = _layer_norm(ALPHA * x_ref[...] + y_s[...], g_ref[...], b_ref[...])


def moe(x2d, cw, w_up, w_down, g, b, tf):
    n = x2d.shape[0]
    E, _, F2 = w_up.shape
    nf = (F2 // 2) // tf
    blk = _row_tile(n, MOE_BLOCK)
    cap = min(MOE_CAP, blk)
    cap_pad = -(-cap // LANES) * LANES
    max_pass = -(-blk // cap)
    nblk = n // blk
    cwt = jnp.swapaxes(cw[:, :ROUTE_ROWS], 0, 1)
    counts = jnp.sum((cw[:, :E] != 0.0).reshape(nblk, blk, E), axis=1)
    n_pass = ((counts + cap - 1) // cap).astype(jnp.int32).reshape(-1)
    return pl.pallas_call(
        functools.partial(_moe_kernel, blk=blk, cap=cap, cap_pad=cap_pad),
        grid_spec=pltpu.PrefetchScalarGridSpec(
            num_scalar_prefetch=1, grid=(nblk, E, nf),
            in_specs=[pl.BlockSpec((blk, D_MODEL), lambda j, e, f, npr: (j, 0)),
                      pl.BlockSpec((blk, LANES), lambda j, e, f, npr: (j, 0)),
                      pl.BlockSpec((ROUTE_ROWS, blk), lambda j, e, f, npr: (0, j)),
                      pl.BlockSpec((1, D_MODEL, tf), lambda j, e, f, npr: (e, 0, f)),
                      pl.BlockSpec((1, D_MODEL, tf), lambda j, e, f, npr: (e, 0, nf + f)),
                      pl.BlockSpec((1, tf, D_MODEL), lambda j, e, f, npr: (e, f, 0)),
                      pl.BlockSpec((1, D_MODEL), lambda j, e, f, npr: (0, 0)),
                      pl.BlockSpec((1, D_MODEL), lambda j, e, f, npr: (0, 0))],
            out_specs=pl.BlockSpec((blk, D_MODEL), lambda j, e, f, npr: (j, 0)),
            scratch_shapes=[pltpu.VMEM((blk, D_MODEL), BF16), pltpu.VMEM((blk, D_MODEL), F32),
                            pltpu.VMEM((blk, LANES), F32), pltpu.VMEM((ROUTE_ROWS, blk), F32),
                            pltpu.VMEM((max_pass, cap_pad, D_MODEL), BF16), pltpu.VMEM((max_pass, cap_pad, 1), F32),
                            pltpu.VMEM((max_pass, cap_pad, D_MODEL), F32)]),
        out_shape=jax.ShapeDtypeStruct((n, D_MODEL), F32),
        compiler_params=_params(("parallel", "arbitrary", "arbitrary")),
        name="moe_ffn",
    )(n_pass, x2d, cw, cwt, w_up, w_up, w_down, g, b)


def _layer(x, lw, l, *, mem_kv, mem_layer, mstate, decode):
    B, T, _ = x.shape
    n = B * T
    x2d = x.reshape(n, D_MODEL)
    tm = _row_tile(n, 1024)
    za, zb, zc, zcb, zd = in_proj(x2d, lw["in"], tm)
    za, zb, zc, zcb, zd = (a.reshape(B, T, -1) for a in (za, zb, zc, zcb, zd))

    if T % MLSTM_CHUNK == 0:
        h_m, ct, nt, mt = mlstm(za, zd, *mstate, lw["norm_g"], L=MLSTM_CHUNK, t_valid=MLSTM_CHUNK)
    else:
        padt = lambda a: jnp.pad(a, ((0, 0), (0, LANES - T), (0, 0)))
        h_m, ct, nt, mt = mlstm(padt(za), padt(zd), *mstate, lw["norm_g"], L=LANES, t_valid=T)
        h_m = h_m[:, :T]

    if decode is None:
        kc, vc = cmp_tokens(zc, lw["cmp_w"], lw["cmp_b"])
        h_n = nsa_prompt(zb, zd, zcb, kc, vc, tq=min(256, T), CH=min(512, T))
    else:
        h_n = nsa_decode(zb, zd, zcb, decode["nsa"], decode["win"], decode["pt"], lw["cmp_w"], lw["cmp_b"], l)

    h_x = mem_attn(zb, mem_kv, mem_layer, tq=_row_tile(T, 512))

    wg, bg = lw["in"]["mg"]
    flat = lambda a: a.reshape(n, -1)
    x1 = merge(x2d, flat(h_m), flat(h_n), flat(h_x), wg, bg, lw["w_branch"], lw["w_out"], lw["ln1_g"], lw["ln1_b"], tm)

    if l % 2 == 0:
        x2 = ffn(x1, lw["ffn_up"], lw["ffn_down"], lw["ln2_g"], lw["ln2_b"], _row_tile(n, 512), D_FF // 2)
    else:
        cw = router(x1, lw["w_router"], lw["b_router"], _row_tile(n, 1024))
        x2 = moe(x1, cw, lw["ffn_up"], lw["ffn_down"], lw["ln2_g"], lw["ln2_b"], 896)
    return x2.reshape(B, T, D_MODEL), zc, (ct, nt, mt)


def kernel(x_prompt, x_sample, mem_prompt, cache_nsa_kv, cache_win_kv, state_mlstm_C, state_mlstm_n,
           state_mlstm_m, cache_mem_kv, page_table, w_in, b_in, mlstm_norm_g, cmp_w, cmp_b, w_mem_kv,
           w_branch, w_out, ln1_g, ln1_b, ln2_g, ln2_b, ffn_w_up, ffn_w_down, moe_w_router, moe_b_router,
           moe_w_up, moe_w_down):
    B, T, _ = x_prompt.shape
    DB, TS, _ = x_sample.shape
    xp, xs = x_prompt, x_sample
    nsa_p, nsa_s, win_p, win_s = [], [], [], []
    Cp, np_, mp, Cs, ns, ms, memkv_p = [], [], [], [], [], [], []
    row1 = lambda a: a.reshape(1, -1).astype(F32)
    for l in range(DEPTH):
        wbr = (w_branch[l, 0].astype(BF16), _pad_branch_rows(w_branch[l, 1]).astype(BF16), w_branch[l, 2].astype(BF16))
        lw = dict(norm_g=mlstm_norm_g[l], cmp_w=cmp_w[l], cmp_b=cmp_b[l], w_branch=wbr,
                  w_out=w_out[l].astype(BF16), ln1_g=row1(ln1_g[l]), ln1_b=row1(ln1_b[l]),
                  ln2_g=row1(ln2_g[l]), ln2_b=row1(ln2_b[l]))
        lw["in"] = _split_in_proj(w_in[l], b_in[l])
        if l % 2 == 0:
            lw["ffn_up"] = ffn_w_up[l // 2].astype(BF16)
            lw["ffn_down"] = ffn_w_down[l // 2].astype(BF16)
        else:
            lw["ffn_up"] = moe_w_up[l // 2].astype(BF16)
            lw["ffn_down"] = moe_w_down[l // 2].astype(BF16)
            lw["w_router"] = moe_w_router[l // 2]
            lw["b_router"] = moe_b_router[l // 2]

        mkv = proj(mem_prompt.reshape(B * MEM_LEN, D_MODEL), w_mem_kv[l].astype(BF16), _row_tile(B * MEM_LEN, 512))
        mkv = mkv.reshape(B, MEM_LEN, 2 * X_WIDTH)
        st0 = (jnp.zeros((B, M_HEADS, M_DH, M_DH), F32), jnp.zeros((B, M_HEADS, M_DH), F32), jnp.zeros((B, M_HEADS), F32))
        xp, zc, st = _layer(xp, lw, l, mem_kv=mkv.reshape(B, MEM_LEN, 1, 2, X_HEADS, X_DH), mem_layer=0, mstate=st0,
                            decode=None)
        kvn = zc.reshape(B, T, 6, NSA_GROUPS, NSA_DH)
        nsa_p.append(kvn[:, :, :4])
        win_p.append(kvn[:, T - min(WINDOW, T):, 4:])
        Cp.append(st[0]); np_.append(st[1]); mp.append(st[2])
        memkv_p.append(mkv.reshape(B, MEM_LEN, 2, X_HEADS, X_DH))

        sts = (state_mlstm_C[:, l], state_mlstm_n[:, l], state_mlstm_m[:, l])
        dec = dict(nsa=cache_nsa_kv, win=cache_win_kv, pt=page_table)
        xs, zc, st = _layer(xs, lw, l, mem_kv=cache_mem_kv, mem_layer=l, mstate=sts, decode=dec)
        kvn = zc.reshape(DB, TS, 6, NSA_GROUPS, NSA_DH)
        nsa_s.append(kvn[:, :, :4])
        win_s.append(jnp.concatenate([cache_win_kv[:, :, l].astype(F32), kvn[:, :, 4:]], axis=1)[:, TS:])
        Cs.append(st[0]); ns.append(st[1]); ms.append(st[2])
    return (xp, xs,
            jnp.stack(nsa_p, axis=2), jnp.stack(nsa_s, axis=2),
            jnp.stack(win_p, axis=2), jnp.stack(win_s, axis=2),
            jnp.stack(Cp, axis=1), jnp.stack(np_, axis=1), jnp.stack(mp, axis=1),
            jnp.stack(Cs, axis=1), jnp.stack(ns, axis=1), jnp.stack(ms, axis=1),
            jnp.stack(memkv_p, axis=2))
```

```python
import functools

import numpy as np
import jax
import jax.numpy as jnp
from jax import lax
from jax.experimental import pallas as pl
from jax.experimental.pallas import tpu as pltpu

D_MODEL = 1024
DEPTH = 2
BRANCH_WIDTH = 512
N_BRANCH = 3
M_HEADS = 4
M_DH = BRANCH_WIDTH // M_HEADS
M_WIDTH = M_HEADS * M_DH
NSA_HEADS = 8
NSA_DH = BRANCH_WIDTH // NSA_HEADS
NSA_WIDTH = NSA_HEADS * NSA_DH
NSA_GROUPS = 2
NSA_HPG = NSA_HEADS // NSA_GROUPS
NSA_KV = NSA_GROUPS * NSA_DH
CMP_LEN = 32
CMP_STRIDE = 16
SEL_LEN = 64
SEL_TOPN = 16
WINDOW = 512
MEM_LEN = 256
X_HEADS = 4
X_DH = BRANCH_WIDTH // X_HEADS
X_WIDTH = X_HEADS * X_DH
D_FF = 2816
N_EXPERTS = 8
TOP_K = 2
D_FF_EXPERT = 3584
ALPHA = (2.0 * DEPTH) ** 0.25
LN_EPS = 1e-5
IN_SPLITS = (M_WIDTH, M_WIDTH, M_WIDTH, M_HEADS, M_HEADS, M_WIDTH,
             NSA_WIDTH, 6 * NSA_KV, 3 * NSA_HEADS, X_WIDTH, N_BRANCH * D_MODEL)
LOG2E = 1.4426950408889634
NSA_SLOPES = tuple(LOG2E * 2.0 ** (-8.0 * (h + 1) / NSA_HEADS) for h in range(NSA_HEADS))
MASK_BIAS = -(2.0 ** 30)
SEL_FEAT = 64
SEL_QUARTERS = 4
SLOPE_PARTS = 3

LANES = 128
SUBLANES = 8
VMEM_LIMIT = 56 * 1024 * 1024
PAGES_PER_STEP = 32
MLSTM_SEQS = 1
MLSTM_CHUNK = 256
GATE_COLS = LANES
NG_OFF = 2 * M_HEADS
NSA_QPAD = NSA_HEADS * LANES
MOE_BLOCK = 1024
MOE_CAP = 288
NO_SLOT = -2
ROUTE_ROWS = 16

F32 = jnp.float32
BF16 = jnp.bfloat16
NEG_INF = float("-inf")
HIGHEST = lax.Precision.HIGHEST


def _dot(a, b, precision=None):
    return jnp.dot(a, b, preferred_element_type=F32, precision=precision)


def _dot_nt(a, b, precision=None):
    return lax.dot_general(a, b, (((1,), (1,)), ((), ())), preferred_element_type=F32, precision=precision)


def _dot_tn(a, b):
    return lax.dot_general(a, b, (((0,), (0,)), ((), ())), preferred_element_type=F32)


def _params(sem):
    return pltpu.CompilerParams(dimension_semantics=sem, vmem_limit_bytes=VMEM_LIMIT)


def _masked_softmax2(s, mask):
    s = jnp.where(mask, s, NEG_INF)
    mx = jnp.max(s, axis=-1, keepdims=True)
    mx = jnp.where(mx > NEG_INF, mx, 0.0)
    p = jnp.where(mask, jnp.exp2(s - mx), 0.0)
    return p / jnp.maximum(jnp.sum(p, axis=-1, keepdims=True), 1e-30)


def _layer_norm(xf, g, b):
    mu = jnp.mean(xf, axis=-1, keepdims=True)
    var = jnp.mean(jnp.square(xf - mu), axis=-1, keepdims=True)
    return (xf - mu) * lax.rsqrt(var + LN_EPS) * g + b


def _row_tile(n, pref):
    return pref if n % pref == 0 else n


def _in_proj_kernel(x_ref, wa_ref, wb_ref, wc_ref, wd_ref, ba_ref, bb_ref, bc_ref, bd_ref,
                    oa_ref, ob_ref, oc_ref, ocb_ref, od_ref):
    x = x_ref[...].astype(BF16)
    oa_ref[...] = (_dot(x, wa_ref[...]) + ba_ref[...]).astype(BF16)
    ob_ref[...] = (_dot(x, wb_ref[...]) + bb_ref[...]).astype(BF16)
    c = _dot(x, wc_ref[...]) + bc_ref[...]
    oc_ref[...] = c
    ocb_ref[...] = c.astype(BF16)
    od_ref[...] = _dot(x, wd_ref[...]) + bd_ref[...]


def _group_select():
    return jax.nn.one_hot(np.arange(NSA_HEADS) // NSA_HPG, NSA_GROUPS, dtype=F32)


def _pad_branch_rows(w_nsa):
    return jnp.einsum("hem,hg->hgem", w_nsa.reshape(NSA_HEADS, NSA_DH, -1), _group_select()).reshape(NSA_QPAD, -1)


def _split_in_proj(w_in_l, b_in_l):
    offs = np.cumsum((0,) + IN_SPLITS)
    w = [w_in_l[:, offs[i]:offs[i + 1]] for i in range(len(IN_SPLITS))]
    b = [b_in_l[offs[i]:offs[i + 1]] for i in range(len(IN_SPLITS))]
    mq, mk, mv, mi, mf, mo, nq, nkv, ng, xq, mg = range(11)
    pad = GATE_COLS - 2 * M_HEADS - 3 * NSA_HEADS

    def cat(ids, zpad=0):
        ww = jnp.concatenate([w[i] for i in ids], axis=1)
        bb = jnp.concatenate([b[i] for i in ids])
        if zpad:
            ww = jnp.pad(ww, ((0, 0), (0, zpad)))
            bb = jnp.pad(bb, (0, zpad))
        return ww.astype(BF16), bb.reshape(1, -1).astype(F32)

    scale = NSA_DH ** -0.5 * LOG2E
    w[nq] = jnp.einsum("dhe,hg->dhge", w[nq].reshape(D_MODEL, NSA_HEADS, NSA_DH) * scale, _group_select()).reshape(D_MODEL, NSA_QPAD)
    b[nq] = jnp.einsum("he,hg->hge", b[nq].reshape(NSA_HEADS, NSA_DH) * scale, _group_select()).reshape(NSA_QPAD)
    return dict(a=cat([mq, mk, mv, mo]), b=cat([nq, xq]), c=cat([nkv]), d=cat([mi, mf, ng], pad), mg=cat([mg]))


def in_proj(x2d, wp, tm):
    n = x2d.shape[0]
    (wa, ba), (wb, bb), (wc, bc), (wd, bd) = wp["a"], wp["b"], wp["c"], wp["d"]
    full = lambda arr: pl.BlockSpec(arr.shape, lambda i: (0, 0), pipeline_mode=pl.Buffered(1))
    row = lambda w: pl.BlockSpec((tm, w), lambda i: (i, 0))
    return pl.pallas_call(
        _in_proj_kernel,
        grid=(n // tm,),
        in_specs=[row(D_MODEL), full(wa), full(wb), full(wc), full(wd), full(ba), full(bb), full(bc), full(bd)],
        out_specs=[row(wa.shape[1]), row(wb.shape[1]), row(wc.shape[1]), row(wc.shape[1]), row(wd.shape[1])],
        out_shape=[jax.ShapeDtypeStruct((n, wa.shape[1]), BF16), jax.ShapeDtypeStruct((n, wb.shape[1]), BF16),
                   jax.ShapeDtypeStruct((n, wc.shape[1]), F32), jax.ShapeDtypeStruct((n, wc.shape[1]), BF16),
                   jax.ShapeDtypeStruct((n, wd.shape[1]), F32)],
        compiler_params=_params(("parallel",)),
        name="in_proj",
    )(x2d, wa, wb, wc, wd, ba, bb, bc, bd)


def _proj_kernel(x_ref, w_ref, o_ref):
    o_ref[...] = _dot(x_ref[...].astype(BF16), w_ref[...])


def proj(x2d, w_bf16, tm):
    n, k = x2d.shape
    m = w_bf16.shape[1]
    return pl.pallas_call(
        _proj_kernel,
        grid=(n // tm,),
        in_specs=[pl.BlockSpec((tm, k), lambda i: (i, 0)), pl.BlockSpec((k, m), lambda i: (0, 0))],
        out_specs=pl.BlockSpec((tm, m), lambda i: (i, 0)),
        out_shape=jax.ShapeDtypeStruct((n, m), F32),
        compiler_params=_params(("parallel",)),
        name="mem_proj",
    )(x2d, w_bf16)


def _log_sigmoid(x):
    return jnp.minimum(x, 0.0) - jnp.log1p(jnp.exp(-jnp.abs(x)))


def _mlstm_kernel(q_ref, k_ref, v_ref, og_ref, gc_ref, gr_ref, c0_ref, n0_ref, m0_ref, ng_ref,
                  h_ref, ct_ref, nt_ref, mt_ref, c_s, n_s, m_s, *, L, t_valid, nb):
    ci = pl.program_id(1)

    @pl.when(ci == 0)
    def _():
        c_s[...] = c0_ref[...]
        n_s[...] = n0_ref[...]
        m_s[...] = m0_ref[...]

    row = lax.broadcasted_iota(jnp.int32, (L, L), 0)
    col = lax.broadcasted_iota(jnp.int32, (L, L), 1)
    causal = row >= col
    tri = causal.astype(F32)
    tri_t = (row <= col).astype(F32)
    rvalid = lax.broadcasted_iota(jnp.int32, (L, GATE_COLS), 0) < t_valid
    cvalid = lax.broadcasted_iota(jnp.int32, (2 * M_HEADS, L), 1) < t_valid
    scale = M_DH ** -0.5

    for bb, h in [(bb, h) for bb in range(nb) for h in range(M_HEADS)]:
        if h == 0:
            gc = gc_ref[bb]
            gr = gr_ref[bb]
            lf_c = jnp.where(rvalid, _log_sigmoid(gc), 0.0)
            lf_r = jnp.where(cvalid, _log_sigmoid(gr), 0.0)
            ig_c = jnp.where(rvalid, gc, NEG_INF)
            ig_r = jnp.where(cvalid, gr, NEG_INF)
            b_c = _dot(tri, lf_c, HIGHEST)
            b_r = _dot(lf_r, tri_t, HIGHEST)
        sl = slice(h * M_DH, (h + 1) * M_DH)
        q = q_ref[bb, :, sl]
        k = k_ref[bb, :, sl]
        v = v_ref[bb, :, sl]
        bc = b_c[:, M_HEADS + h:M_HEADS + h + 1]
        br = b_r[M_HEADS + h:M_HEADS + h + 1, :]
        igc = ig_c[:, h:h + 1]
        igr = ig_r[h:h + 1, :]
        m_prev = m_s[bb, h, 0:1, 0:1]
        c_prev = c_s[bb, h]
        n_prev = n_s[bb, h:h + 1, :]

        dmat = jnp.where(causal, bc - br + igr, NEG_INF)
        inter = bc + m_prev
        m_t = jnp.maximum(inter, jnp.max(dmat, axis=-1, keepdims=True))
        s = _dot_nt(q, k) * scale * jnp.exp(dmat - m_t)
        sc_in = jnp.exp(inter - m_t)
        qf = q.astype(F32)
        num = sc_in * _dot_nt(q, c_prev.astype(BF16)) + _dot(s.astype(BF16), v)
        den = sc_in * jnp.sum(qf * n_prev, axis=-1, keepdims=True) + jnp.sum(s, axis=-1, keepdims=True)
        hh = num / jnp.maximum(jnp.abs(den), jnp.exp(-m_t))

        b_last = bc[L - 1:L, :]
        dec_c = b_last - bc + igc
        dec_r = b_last - br + igr
        m_new = jnp.maximum(b_last + m_prev, jnp.max(dec_r, axis=-1, keepdims=True))
        ws_c = jnp.exp(dec_c - m_new) * scale
        sc = jnp.exp(b_last + m_prev - m_new)
        vf = v.astype(F32)
        kf = k.astype(F32)
        c_new = sc * c_prev + _dot_tn((vf * ws_c).astype(BF16), k)
        n_new = sc * n_prev + jnp.sum(kf * ws_c, axis=0, keepdims=True)
        c_s[bb, h] = c_new
        n_s[bb, h:h + 1, :] = n_new
        m_s[bb, h] = jnp.broadcast_to(m_new, m_s.shape[2:])

        og = og_ref[bb, :, sl].astype(F32)
        hh = hh * jax.nn.sigmoid(og)
        mu = jnp.mean(hh, axis=-1, keepdims=True)
        var = jnp.mean(jnp.square(hh - mu), axis=-1, keepdims=True)
        hn = (hh - mu) * lax.rsqrt(var + LN_EPS) * ng_ref[:, sl]
        h_ref[bb, :, sl] = hn.astype(h_ref.dtype)

    ct_ref[...] = c_s[...]
    nt_ref[...] = n_s[...]
    mt_ref[...] = m_s[...]


def mlstm(za, zd, c0, n0, m0, norm_g, *, L, t_valid):
    B, T, _ = za.shape
    nc = T // L
    gr = jnp.swapaxes(zd[:, :, :2 * M_HEADS], 1, 2)
    m0b = jnp.broadcast_to(m0.astype(F32)[:, :, None, None], (B, M_HEADS, SUBLANES, LANES))
    nb = MLSTM_SEQS if B % MLSTM_SEQS == 0 else 1
    colspec = lambda j: pl.BlockSpec((nb, L, M_WIDTH), lambda b, c, j=j: (b, c, j))
    st = lambda shp: pl.BlockSpec((nb,) + shp, lambda b, c: (b,) + (0,) * len(shp))
    kern = functools.partial(_mlstm_kernel, L=L, t_valid=t_valid, nb=nb)
    h, ct, nt, mt = pl.pallas_call(
        kern,
        grid=(B // nb, nc),
        in_specs=[colspec(0), colspec(1), colspec(2), colspec(3),
                  pl.BlockSpec((nb, L, GATE_COLS), lambda b, c: (b, c, 0)),
                  pl.BlockSpec((nb, 2 * M_HEADS, L), lambda b, c: (b, 0, c)),
                  st((M_HEADS, M_DH, M_DH)), st((M_HEADS, M_DH)), st((M_HEADS, SUBLANES, LANES)),
                  pl.BlockSpec((1, M_WIDTH), lambda b, c: (0, 0))],
        out_specs=[pl.BlockSpec((nb, L, M_WIDTH), lambda b, c: (b, c, 0)),
                   st((M_HEADS, M_DH, M_DH)), st((M_HEADS, M_DH)), st((M_HEADS, SUBLANES, LANES))],
        out_shape=[jax.ShapeDtypeStruct((B, T, M_WIDTH), BF16),
                   jax.ShapeDtypeStruct((B, M_HEADS, M_DH, M_DH), F32),
                   jax.ShapeDtypeStruct((B, M_HEADS, M_DH), F32),
                   jax.ShapeDtypeStruct((B, M_HEADS, SUBLANES, LANES), F32)],
        scratch_shapes=[pltpu.VMEM((nb, M_HEADS, M_DH, M_DH), F32), pltpu.VMEM((nb, M_HEADS, M_DH), F32),
                        pltpu.VMEM((nb, M_HEADS, SUBLANES, LANES), F32)],
        compiler_params=_params(("parallel", "arbitrary")),
        name="mlstm",
    )(za, za, za, za, zd, gr, c0.astype(F32), n0.astype(F32), m0b, norm_g.reshape(1, M_WIDTH).astype(F32))
    return h, ct, nt, mt[:, :, 0, 0]


def _mem_attn_kernel(q_ref, k_ref, v_ref, o_ref):
    scale = X_DH ** -0.5
    for h in range(X_HEADS):
        sl = slice(h * X_DH, (h + 1) * X_DH)
        s = _dot_nt(q_ref[0, :, sl], k_ref[:, h, :].astype(BF16)) * scale
        mx = jnp.max(s, axis=-1, keepdims=True)
        p = jnp.exp(s - mx)
        p = p / jnp.sum(p, axis=-1, keepdims=True)
        o_ref[0, :, sl] = _dot(p.astype(BF16), v_ref[:, h, :].astype(BF16)).astype(o_ref.dtype)


def mem_attn(zb, kv, layer, tq):
    B, T, _ = zb.shape
    kv_blk = (None, MEM_LEN, None, None, X_HEADS, X_DH)
    return pl.pallas_call(
        _mem_attn_kernel,
        grid=(B, T // tq),
        in_specs=[pl.BlockSpec((1, tq, X_WIDTH), lambda b, i: (b, i, NSA_QPAD // X_WIDTH)),
                  pl.BlockSpec(kv_blk, lambda b, i: (b, 0, layer, 0, 0, 0)),
                  pl.BlockSpec(kv_blk, lambda b, i: (b, 0, layer, 1, 0, 0))],
        out_specs=pl.BlockSpec((1, tq, X_WIDTH), lambda b, i: (b, i, 0)),
        out_shape=jax.ShapeDtypeStruct((B, T, X_WIDTH), BF16),
        compiler_params=_params(("parallel", "parallel")),
        name="mem_attn",
    )(zb, kv, kv)


def _head_rows(q_ref, nq):
    parts = [q_ref[0, :, hd * LANES:(hd + 1) * LANES] for hd in range(NSA_HEADS)]
    if nq % 16:
        return jnp.concatenate([p.astype(F32) for p in parts], axis=0).astype(BF16)
    return jnp.concatenate(parts, axis=0)


def _store_heads(o_ref, gate, o_c, o_s, o_w, nq):
    lane = lax.broadcasted_iota(jnp.int32, (nq, LANES), 1)
    for hd in range(NSA_HEADS):
        r = slice(hd * nq, (hd + 1) * nq)
        gcol = lambda br: gate[:, NG_OFF + br * NSA_HEADS + hd:NG_OFF + br * NSA_HEADS + hd + 1]
        val = gcol(0) * o_c[r] + gcol(1) * o_s[r] + gcol(2) * o_w[r]
        keep = (lane >= NSA_DH) if hd // NSA_HPG == 1 else (lane < NSA_DH)
        o_ref[0, :, hd * LANES:(hd + 1) * LANES] = jnp.where(keep, val, 0.0).astype(o_ref.dtype)


def _overlap(tok, blk):
    c_start = tok * CMP_STRIDE
    s_start = blk * SEL_LEN
    return ((c_start < s_start + SEL_LEN) & (c_start + CMP_LEN > s_start)).astype(F32)


def _select_blocks(imp, tpos, n_sel, n_top):
    nq, W = imp.shape
    blk = lax.broadcasted_iota(jnp.int32, (nq, W), 1)
    cur = tpos // SEL_LEN
    forced = (blk == 0) | (blk == cur) | (blk == cur - 1)
    v = jnp.where(forced, jnp.inf, jnp.where(blk <= cur, imp, NEG_INF))
    v = jnp.where(blk < n_sel, v, NEG_INF)
    ahead = jnp.zeros((nq, W), F32)
    for j in range(n_sel):
        vj = v[:, j:j + 1]
        ahead = ahead + jnp.where(vj > v, 1.0, jnp.where(vj == v, jnp.where(blk > j, 1.0, 0.0), 0.0))
    return jnp.where((ahead < n_top) & (blk < n_sel), 1.0, 0.0)


def _select_blocks_t(imp_t, tpos_row, n_sel, n_top):
    n_blk, nq = imp_t.shape
    blk = lax.broadcasted_iota(jnp.int32, (n_blk, nq), 0)
    cur = tpos_row // SEL_LEN
    forced = (blk == 0) | (blk == cur) | (blk == cur - 1)
    v = jnp.where(forced, jnp.inf, jnp.where(blk <= cur, imp_t, NEG_INF))
    v = jnp.where(blk < n_sel, v, NEG_INF)
    n_rg = n_blk // SUBLANES
    vg = [v[rg * SUBLANES:(rg + 1) * SUBLANES] for rg in range(n_rg)]
    bg = [rg * SUBLANES + lax.broadcasted_iota(jnp.int32, (SUBLANES, nq), 0) for rg in range(n_rg)]
    ahead = [jnp.zeros((SUBLANES, nq), F32) for _ in range(n_rg)]
    for j in range(n_sel):
        vj = v[j:j + 1, :]
        for rg in range(n_rg):
            if rg * SUBLANES > j:
                inc = jnp.where(vj >= vg[rg], 1.0, 0.0)
            elif (rg + 1) * SUBLANES - 1 <= j:
                inc = jnp.where(vj > vg[rg], 1.0, 0.0)
            else:
                inc = jnp.where(vj > vg[rg], 1.0, jnp.where(vj == vg[rg], jnp.where(bg[rg] > j, 1.0, 0.0), 0.0))
            ahead[rg] = ahead[rg] + inc
    ahead = jnp.concatenate(ahead, axis=0)
    return jnp.where((ahead < n_top) & (blk < n_sel), 1.0, 0.0)


def _online_update(s, m_old, l_old):
    m_new = jnp.maximum(m_old, jnp.max(s, axis=-1, keepdims=True))
    alpha = jnp.exp2(m_old - m_new)
    p = jnp.exp2(s - m_new)
    return p, m_new, alpha, alpha * l_old + jnp.sum(p, axis=-1, keepdims=True)


def _cmp_tokens_kernel(xk_ref, xv_ref, bd_ref, b_ref, kc_ref, vc_ref, *, nsub):
    for c, (x_ref, o_ref) in enumerate(((xk_ref, kc_ref), (xv_ref, vc_ref))):
        a0 = jnp.zeros((nsub, LANES), F32)
        a1 = jnp.zeros((nsub, LANES), F32)
        for j in range(CMP_STRIDE):
            xj = x_ref[0, :, j, :].astype(BF16)
            a0 = a0 + _dot(xj, bd_ref[c, 0, j])
            a1 = a1 + _dot(xj, bd_ref[c, 1, j])
        tok = a0 + pltpu.roll(a1, nsub - 1, axis=0) + b_ref[c]
        o_ref[0] = tok.astype(o_ref.dtype)


def _cmp_blockdiag(cmp_w_l):
    R = CMP_LEN // CMP_STRIDE
    w = cmp_w_l.astype(F32).reshape(2, R, CMP_STRIDE, NSA_DH, NSA_DH)
    eye = jnp.eye(NSA_GROUPS, dtype=F32)
    return jnp.einsum("ab,crjde->crjadbe", eye, w).reshape(2, R, CMP_STRIDE, LANES, LANES).astype(BF16)


def _cmp_bias(cmp_b_l):
    return jnp.tile(cmp_b_l.astype(F32), (1, NSA_GROUPS)).reshape(2, 1, LANES)


def cmp_tokens(zc, cmp_w_l, cmp_b_l):
    B, T, W = zc.shape
    nsub = T // CMP_STRIDE
    x4 = zc.reshape(B, nsub, CMP_STRIDE, W)
    bd = _cmp_blockdiag(cmp_w_l)
    bias = _cmp_bias(cmp_b_l)
    spec = lambda j: pl.BlockSpec((1, nsub, CMP_STRIDE, LANES), lambda b, j=j: (b, 0, 0, j))
    return pl.pallas_call(
        functools.partial(_cmp_tokens_kernel, nsub=nsub),
        grid=(B,),
        in_specs=[spec(0), spec(1), pl.BlockSpec(bd.shape, lambda b: (0,) * 5), pl.BlockSpec(bias.shape, lambda b: (0, 0, 0))],
        out_specs=[pl.BlockSpec((1, nsub, LANES), lambda b: (b, 0, 0))] * 2,
        out_shape=[jax.ShapeDtypeStruct((B, nsub, LANES), BF16)] * 2,
        compiler_params=_params(("parallel",)),
        name="nsa_cmp_tokens",
    )(x4, x4, bd, bias)


def _nsa_prompt_kernel(q_ref, g_ref, kca_ref, vc_ref, ka_ref, vst_ref, kwa_ref, vw_ref, qf_ref, o_ref,
                       qaug_s, s_buf, p_buf, w_buf, pw_buf, pc_s, m_s, l_s, a_s, acc_s, oc_s, ow_s, *, tq, T, CH):
    start = pl.program_id(1) * tq
    n_cmp_rows = kca_ref.shape[1]
    n_sel = T // SEL_LEN
    n_top = min(SEL_TOPN, n_sel)
    n_blk = -(-n_sel // SUBLANES) * SUBLANES
    nq_all = NSA_HEADS * tq
    qw = nq_all // SEL_QUARTERS
    tpos_row = start + lax.broadcasted_iota(jnp.int32, (1, tq), 1)
    tpos_all = jnp.concatenate([tpos_row] * NSA_HEADS, axis=1)
    qcols = lambda qi: slice(qi * qw, (qi + 1) * qw)
    tiles = [(qi, ct, slice(qi * qw + ct * LANES, qi * qw + (ct + 1) * LANES), slice(ct * LANES, (ct + 1) * LANES))
             for qi in range(SEL_QUARTERS) for ct in range(qw // LANES)]

    qaug_s[:, 0:LANES] = _head_rows(q_ref, tq)
    for hd in range(NSA_HEADS):
        qaug_s[hd * tq:(hd + 1) * tq, LANES:2 * LANES] = jnp.broadcast_to(qf_ref[hd:hd + 1, :], (tq, LANES)).astype(BF16)

    def softmax_tile(s_t):
        mx = jnp.max(s_t, axis=0, keepdims=True)
        mx = jnp.where(mx > NEG_INF, mx, 0.0)
        p_t = jnp.exp2(s_t - mx)
        return p_t, jnp.maximum(jnp.sum(p_t, axis=0, keepdims=True), 1e-30)

    c_end = lax.broadcasted_iota(jnp.int32, (n_cmp_rows, 1), 0) * CMP_STRIDE + CMP_LEN - 1
    for qi in range(SEL_QUARTERS):
        s_buf[qi, 0:n_cmp_rows, :] = _dot_nt(kca_ref[0], qaug_s[qcols(qi), :])
    for qi, ct, cols, tc in tiles:
        p_t, den = softmax_tile(jnp.where(c_end <= tpos_all[:, cols], s_buf[qi, 0:n_cmp_rows, tc], NEG_INF))
        pc_s[:, cols] = p_t / den
    oc_s[...] = _dot_tn(vc_ref[0], pc_s[...].astype(BF16))
    ov_t = _overlap(lax.broadcasted_iota(jnp.int32, (n_blk, n_cmp_rows), 1),
                    lax.broadcasted_iota(jnp.int32, (n_blk, n_cmp_rows), 0))
    lane = lax.broadcasted_iota(jnp.int32, (tq, LANES), 1)
    feat_sel = [None] * NSA_HEADS
    for g in range(NSA_GROUPS):
        pg = pc_s[:, g * NSA_HPG * tq:(g * NSA_HPG + 1) * tq]
        for p in range(1, NSA_HPG):
            pg = pg + pc_s[:, (g * NSA_HPG + p) * tq:(g * NSA_HPG + p + 1) * tq]
        sel_t = _select_blocks_t(_dot(ov_t, pg, HIGHEST), tpos_row, n_sel, n_top)
        sel = jnp.concatenate([sel_t, jnp.zeros((LANES - n_blk, tq), F32)], axis=0).T
        unsel = jnp.where(lane < SEL_FEAT, (1.0 - sel) * MASK_BIAS, 0.0)
        for p in range(NSA_HPG):
            hd = g * NSA_HPG + p
            feat_sel[hd] = (unsel + qf_ref[hd:hd + 1, :]).astype(BF16)

    wk = WINDOW + tq
    base_w = pl.multiple_of(start, tq)
    wpos = start - WINDOW + lax.broadcasted_iota(jnp.int32, (wk, 1), 0)
    for qi in range(SEL_QUARTERS):
        w_buf[qi] = _dot_nt(kwa_ref[0, pl.ds(base_w, wk), :], qaug_s[qcols(qi), :])
    for qi, ct, cols, tc in tiles:
        tp = tpos_all[:, cols]
        first = jnp.maximum(tp - (WINDOW - 1), 0)
        s_t = jnp.where(wpos >= first, jnp.where(wpos <= tp, w_buf[qi, :, tc], NEG_INF), NEG_INF)
        p_t, den = softmax_tile(s_t)
        pw_buf[qi, :, tc] = p_t.astype(BF16)
        a_s[:, cols] = den
    for qi in range(SEL_QUARTERS):
        ow_s[:, qcols(qi)] = _dot_tn(vw_ref[0, pl.ds(base_w, wk), :], pw_buf[qi]) / a_s[:, qcols(qi)]

    for hd in range(NSA_HEADS):
        qaug_s[hd * tq:(hd + 1) * tq, LANES:2 * LANES] = feat_sel[hd]

    m_s[...] = jnp.full(m_s.shape, NEG_INF, F32)
    l_s[...] = jnp.zeros(l_s.shape, F32)
    acc_s[...] = jnp.zeros(acc_s.shape, F32)

    def issue(c, qi):
        base = pl.multiple_of(c * CH, CH)
        s_buf[qi] = _dot_nt(ka_ref[0, pl.ds(base, CH), :], qaug_s[qi * qw:(qi + 1) * qw, :])

    def absorb(c, qi, diag):
        for ct in range(qw // LANES):
            cols = slice(qi * qw + ct * LANES, qi * qw + (ct + 1) * LANES)
            s_t = s_buf[qi, :, ct * LANES:(ct + 1) * LANES]
            if diag:
                kpos = c * CH + lax.broadcasted_iota(jnp.int32, (CH, 1), 0)
                s_t = jnp.where(kpos <= tpos_all[:, cols], s_t, NEG_INF)
            m_old = m_s[:, cols]
            m_new = jnp.maximum(m_old, jnp.max(s_t, axis=0, keepdims=True))
            alpha = jnp.exp2(m_old - m_new)
            p_t = jnp.exp2(s_t - m_new)
            m_s[:, cols] = m_new
            a_s[:, cols] = alpha
            l_s[:, cols] = alpha * l_s[:, cols] + jnp.sum(p_t, axis=0, keepdims=True)
            p_buf[qi, :, ct * LANES:(ct + 1) * LANES] = p_t.astype(BF16)
        cols = slice(qi * qw, (qi + 1) * qw)
        acc_s[:, cols] = a_s[:, cols] * acc_s[:, cols] + _dot(vst_ref[0, c], p_buf[qi])

    def full_chunk(c, carry):
        for qi in range(SEL_QUARTERS):
            absorb(c, qi, False)
            issue(c + 1, qi)
        return carry

    n_full = start // CH
    for qi in range(SEL_QUARTERS):
        issue(0, qi)
    lax.fori_loop(0, n_full, full_chunk, 0)
    for qi in range(SEL_QUARTERS):
        absorb(n_full, qi, True)

    gate_t = jax.nn.sigmoid(g_ref[0]).T
    for hd in range(NSA_HEADS):
        cols = slice(hd * tq, (hd + 1) * tq)
        grow = lambda br: gate_t[NG_OFF + br * NSA_HEADS + hd:NG_OFF + br * NSA_HEADS + hd + 1, :]
        o_s = acc_s[:, cols] / jnp.maximum(l_s[:, cols], 1e-30)
        val = (grow(0) * oc_s[:, cols] + grow(1) * o_s + grow(2) * ow_s[:, cols]).T
        keep = (lane >= NSA_DH) if hd // NSA_HPG == 1 else (lane < NSA_DH)
        o_ref[0, :, hd * LANES:(hd + 1) * LANES] = jnp.where(keep, val, 0.0).astype(o_ref.dtype)


def _bf16_parts(x, n):
    parts = []
    for _ in range(n):
        p = float(np.asarray(x, np.float32).astype(jnp.bfloat16).astype(np.float32))
        parts.append(p)
        x = x - p
    return parts


def _slope_features():
    qf = np.zeros((NSA_HEADS, LANES), np.float32)
    for hd in range(NSA_HEADS):
        for p, s_p in enumerate(_bf16_parts(NSA_SLOPES[hd], SLOPE_PARTS)):
            qf[hd, SEL_FEAT + 2 * p] = s_p * SEL_LEN
            qf[hd, SEL_FEAT + 2 * p + 1] = s_p
    return jnp.asarray(qf)


def _key_features(pos, block_onehot):
    pos = np.asarray(pos)
    ok = pos >= 0
    kf = np.zeros((pos.shape[0], LANES), np.float32)
    if block_onehot:
        kf[np.arange(pos.shape[0])[ok], pos[ok] // SEL_LEN] = 1.0
    for p in range(SLOPE_PARTS):
        kf[ok, SEL_FEAT + 2 * p] = pos[ok] // SEL_LEN
        kf[ok, SEL_FEAT + 2 * p + 1] = pos[ok] % SEL_LEN
    return jnp.asarray(kf, BF16)


def nsa_prompt(zb, zd, zcb, kc, vc, tq, CH):
    B, T, _ = zb.shape
    nsub = kc.shape[1]
    assert T % CH == 0 and CH % tq == 0 and T // SEL_LEN <= SEL_FEAT and nsub <= CH
    qf = _slope_features()
    with_feats = lambda k, feats: jnp.concatenate([k, jnp.broadcast_to(feats[None], (B,) + feats.shape)], axis=2)
    k_aug = with_feats(zcb[:, :, 2 * NSA_KV:3 * NSA_KV], _key_features(np.arange(T), True))
    kc_aug = with_feats(kc, _key_features(np.arange(nsub) * CMP_STRIDE + CMP_LEN - 1, False))
    kw = jnp.pad(zcb[:, :, 4 * NSA_KV:5 * NSA_KV], ((0, 0), (WINDOW, 0), (0, 0)))
    kw_aug = with_feats(kw, _key_features(np.arange(T + WINDOW) - WINDOW, False))
    vs_t = jnp.swapaxes(zcb[:, :, 3 * NSA_KV:4 * NSA_KV].reshape(B, T // CH, CH, NSA_KV), 2, 3)
    vw = jnp.pad(zcb[:, :, 5 * NSA_KV:6 * NSA_KV], ((0, 0), (WINDOW, 0), (0, 0)))
    per_b = lambda rows_, w: pl.BlockSpec((1, rows_, w), lambda b, i: (b, 0, 0))
    kern = functools.partial(_nsa_prompt_kernel, tq=tq, T=T, CH=CH)
    nq_all = NSA_HEADS * tq
    qw = nq_all // SEL_QUARTERS
    wk = WINDOW + tq
    assert qw % LANES == 0
    return pl.pallas_call(
        kern,
        grid=(B, T // tq),
        in_specs=[pl.BlockSpec((1, tq, NSA_QPAD), lambda b, i: (b, i, 0)),
                  pl.BlockSpec((1, tq, GATE_COLS), lambda b, i: (b, i, 0)),
                  per_b(nsub, 2 * LANES), per_b(nsub, LANES),
                  per_b(T, 2 * LANES),
                  pl.BlockSpec((1, T // CH, NSA_KV, CH), lambda b, i: (b, 0, 0, 0)),
                  per_b(T + WINDOW, 2 * LANES), per_b(T + WINDOW, LANES),
                  pl.BlockSpec(qf.shape, lambda b, i: (0, 0))],
        out_specs=pl.BlockSpec((1, tq, NSA_QPAD), lambda b, i: (b, i, 0)),
        out_shape=jax.ShapeDtypeStruct((B, T, NSA_QPAD), BF16),
        scratch_shapes=[pltpu.VMEM((nq_all, 2 * LANES), BF16),
                        pltpu.VMEM((SEL_QUARTERS, CH, qw), F32), pltpu.VMEM((SEL_QUARTERS, CH, qw), BF16),
                        pltpu.VMEM((SEL_QUARTERS, wk, qw), F32), pltpu.VMEM((SEL_QUARTERS, wk, qw), BF16),
                        pltpu.VMEM((nsub, nq_all), F32),
                        pltpu.VMEM((1, nq_all), F32), pltpu.VMEM((1, nq_all), F32), pltpu.VMEM((1, nq_all), F32),
                        pltpu.VMEM((NSA_KV, nq_all), F32), pltpu.VMEM((NSA_KV, nq_all), F32),
                        pltpu.VMEM((NSA_KV, nq_all), F32)],
        compiler_params=_params(("parallel", "arbitrary")),
        name="nsa_prompt",
    )(zb, zd, kc_aug, vc, k_aug, vs_t, kw_aug, vw, qf)


def _page_specs(shape, slot_blk, layer, n_pages):
    def mk(i):
        def imap(b, s, pt):
            return (pt[b * n_pages + s * PAGES_PER_STEP + i], layer, slot_blk, 0, 0)
        return pl.BlockSpec(shape, imap)
    return [mk(i) for i in range(PAGES_PER_STEP)]


def _nsa_dec_cmp_kernel(pt_ref, *refs, T, past_len, n_rows):
    pages = refs[:PAGES_PER_STEP]
    perm_ref, bd_ref, b_ref, q_ref, oc_ref, sel_ref, tok_s, pend_s = refs[PAGES_PER_STEP:]
    s = pl.program_id(1)
    n_steps = pl.num_programs(1)
    page_rows = pages[0].shape[-1]
    sub = page_rows // CMP_STRIDE
    R = PAGES_PER_STEP * sub

    @pl.when(s == 0)
    def _():
        pend_s[...] = jnp.zeros(pend_s.shape, F32)

    rid = lax.broadcasted_iota(jnp.int32, (R, LANES), 0)
    for c in range(2):
        xs = [_dot_nt(perm_ref[...], pg[0, 0, c].astype(BF16)) for pg in pages]
        a0 = jnp.zeros((R, LANES), F32)
        a1 = jnp.zeros((R, LANES), F32)
        for j in range(CMP_STRIDE):
            xj = jnp.concatenate([x[j * sub:(j + 1) * sub] for x in xs], axis=0).astype(BF16)
            a0 = a0 + _dot(xj, bd_ref[c, 0, j])
            a1 = a1 + _dot(xj, bd_ref[c, 1, j])
        a0 = a0 + b_ref[c]
        tok = jnp.where(rid == 0, pend_s[c], pltpu.roll(a0, 1, axis=0)) + a1
        tok_s[c, pl.ds(pl.multiple_of(s * R, R), R), :] = tok
        pend_s[c] = a0[R - 1:R, :]

    @pl.when(s == n_steps - 1)
    def _():
        qp = _head_rows(q_ref, T)
        kc = tok_s[0].astype(BF16)
        vc = tok_s[1].astype(BF16)
        sc_all = _dot_nt(qp, kc)
        tpos = past_len + lax.broadcasted_iota(jnp.int32, (T, 1), 0)
        r = lax.broadcasted_iota(jnp.int32, (T, n_rows), 1)
        dc = tpos - ((r - 1) * CMP_STRIDE + CMP_LEN - 1)
        mask_c = (dc >= 0) & (r >= 1)
        dcf = dc.astype(F32)
        pcs = [_masked_softmax2(sc_all[hd * T:(hd + 1) * T] - NSA_SLOPES[hd] * dcf, mask_c) for hd in range(NSA_HEADS)]
        oc_ref[0] = _dot(jnp.concatenate(pcs, axis=0).astype(BF16), vc)
        W = sel_ref.shape[2]
        n_sel = -(-(past_len + T) // SEL_LEN)
        ov = _overlap(lax.broadcasted_iota(jnp.int32, (n_rows, W), 0) - 1, lax.broadcasted_iota(jnp.int32, (n_rows, W), 1))
        for g in range(NSA_GROUPS):
            pg = pcs[g * NSA_HPG]
            for p in range(1, NSA_HPG):
                pg = pg + pcs[g * NSA_HPG + p]
            imp = _dot(pg, ov, HIGHEST)
            sel_ref[0, g * T:(g + 1) * T, :] = _select_blocks(imp, tpos, n_sel, min(SEL_TOPN, n_sel))


def _nsa_dec_sel_kernel(pt_ref, *refs, T, past_len):
    pages = refs[:PAGES_PER_STEP]
    q_ref, sel_ref, sel_last_ref, ex_ref, kn_ref, vn_ref, os_ref, m_s, l_s, acc_s = refs[PAGES_PER_STEP:]
    s = pl.program_id(1)
    n_steps = pl.num_programs(1)
    page_rows = pages[0].shape[-1]
    CH = PAGES_PER_STEP * page_rows

    @pl.when(s == 0)
    def _():
        m_s[...] = jnp.full(m_s.shape, NEG_INF, F32)
        l_s[...] = jnp.zeros(l_s.shape, F32)
        acc_s[...] = jnp.zeros(acc_s.shape, F32)

    qp = _head_rows(q_ref, T)
    tpos = past_len + lax.broadcasted_iota(jnp.int32, (T, 1), 0)

    def update(s_all, mask_of_group, kpos, pv):
        ds = tpos - kpos
        dsf = ds.astype(F32)
        ps = []
        for g in range(NSA_GROUPS):
            mk = mask_of_group(g) & (ds >= 0)
            for p in range(NSA_HPG):
                hd = g * NSA_HPG + p
                r = slice(hd * T, (hd + 1) * T)
                sc = jnp.where(mk, s_all[r] - NSA_SLOPES[hd] * dsf, NEG_INF)
                pr, m_new, alpha, l_new = _online_update(sc, m_s[r], l_s[r])
                m_s[r] = m_new
                l_s[r] = l_new
                acc_s[r] = alpha * acc_s[r]
                ps.append(pr)
        acc_s[...] = acc_s[...] + pv(jnp.concatenate(ps, axis=0).astype(BF16))

    selk = _dot(sel_ref[0, 0].astype(BF16), ex_ref[...])
    s_all = jnp.concatenate([_dot(qp, pg[0, 0, 0].astype(BF16)) for pg in pages], axis=1)
    kpos = s * CH + lax.broadcasted_iota(jnp.int32, (T, CH), 1)

    def pv_pages(pmat):
        out = jnp.zeros((NSA_HEADS * T, LANES), F32)
        for i, pg in enumerate(pages):
            out = out + _dot_nt(pmat[:, i * page_rows:(i + 1) * page_rows], pg[0, 0, 1].astype(BF16))
        return out

    update(s_all, lambda g: selk[g * T:(g + 1) * T] > 0.5, kpos, pv_pages)

    @pl.when(s == n_steps - 1)
    def _():
        nk = kn_ref.shape[1]
        sn = _dot_nt(qp, kn_ref[0])
        lane = lax.broadcasted_iota(jnp.int32, (T, nk), 1)
        sl = sel_last_ref[0, 0]
        update(sn, lambda g: (sl[g * T:(g + 1) * T, 0:1] > 0.5) & (lane < T), past_len + lane,
               lambda pmat: _dot(pmat, vn_ref[0]))
        os_ref[0] = acc_s[...] / jnp.maximum(l_s[...], 1e-30)


def _nsa_dec_win_kernel(q_ref, g_ref, oc_ref, os_ref, wp_ref, kwn_ref, vwn_ref, o_ref, *, T, past_len):
    qp = _head_rows(q_ref, T)
    w_src = wp_ref.shape[-1]
    nk = kwn_ref.shape[1]
    sw_all = jnp.concatenate([_dot(qp, wp_ref[0, 0, 0].astype(BF16)), _dot_nt(qp, kwn_ref[0])], axis=1)
    j = lax.broadcasted_iota(jnp.int32, (T, w_src + nk), 1)
    tpos = past_len + lax.broadcasted_iota(jnp.int32, (T, 1), 0)
    wpos = past_len - w_src + j
    dw = tpos - wpos
    mask_w = (dw >= 0) & (dw < WINDOW) & (wpos >= 0) & (j < w_src + T)
    dwf = dw.astype(F32)
    pws = [_masked_softmax2(sw_all[hd * T:(hd + 1) * T] - NSA_SLOPES[hd] * dwf, mask_w) for hd in range(NSA_HEADS)]
    pw = jnp.concatenate(pws, axis=0).astype(BF16)
    o_w = _dot_nt(pw[:, :w_src], wp_ref[0, 0, 1].astype(BF16)) + _dot(pw[:, w_src:], vwn_ref[0])
    _store_heads(o_ref, jax.nn.sigmoid(g_ref[0]), oc_ref[0], os_ref[0], o_w, T)


def nsa_decode(zb, zd, zcb, cache_nsa_kv, cache_win_kv, page_table, cmp_w_l, cmp_b_l, layer):
    DB, T, _ = zb.shape
    n_pool, page_rows = cache_nsa_kv.shape[:2]
    n_pages = page_table.shape[1]
    past_len = n_pages * page_rows
    n_steps = n_pages // PAGES_PER_STEP
    sub_per_page = page_rows // CMP_STRIDE
    n_rows = n_pages * sub_per_page
    assert (n_rows - 1) * CMP_STRIDE + CMP_LEN - 1 > past_len + T - 1
    assert T <= CMP_STRIDE and n_pages % PAGES_PER_STEP == 0 and page_rows == LANES
    pt = page_table.reshape(-1).astype(jnp.int32)
    cache_t = jnp.transpose(cache_nsa_kv, (0, 2, 3, 4, 5, 1)).reshape(n_pool, DEPTH, 4, NSA_KV, page_rows)
    page_blk = (1, 1, 2, NSA_KV, page_rows)

    rr = np.arange(page_rows)
    perm = jnp.asarray(rr[None, :] == ((rr % sub_per_page) * CMP_STRIDE + rr // sub_per_page)[:, None], BF16)
    bd = _cmp_blockdiag(cmp_w_l)
    bias = _cmp_bias(cmp_b_l)

    n_sel = -(-(past_len + T) // SEL_LEN)
    blocks_per_step = PAGES_PER_STEP * page_rows // SEL_LEN
    sel_used = (n_steps + 1) * blocks_per_step
    sel_w = -(-sel_used // LANES) * LANES
    assert sel_used >= n_sel and blocks_per_step <= LANES
    q_spec = pl.BlockSpec((1, T, NSA_QPAD), lambda b, s, pt: (b, 0, 0))
    const = lambda a: pl.BlockSpec(a.shape, lambda b, s, pt: (0,) * a.ndim)

    o_c, sel = pl.pallas_call(
        functools.partial(_nsa_dec_cmp_kernel, T=T, past_len=past_len, n_rows=n_rows),
        grid_spec=pltpu.PrefetchScalarGridSpec(
            num_scalar_prefetch=1, grid=(DB, n_steps),
            in_specs=_page_specs(page_blk, 0, layer, n_pages) + [const(perm), const(bd), const(bias), q_spec],
            out_specs=[pl.BlockSpec((1, NSA_HEADS * T, LANES), lambda b, s, pt: (b, 0, 0)),
                       pl.BlockSpec((1, NSA_GROUPS * T, sel_w), lambda b, s, pt: (b, 0, 0))],
            scratch_shapes=[pltpu.VMEM((2, n_rows, LANES), F32), pltpu.VMEM((2, 1, LANES), F32)]),
        out_shape=[jax.ShapeDtypeStruct((DB, NSA_HEADS * T, LANES), F32),
                   jax.ShapeDtypeStruct((DB, NSA_GROUPS * T, sel_w), F32)],
        compiler_params=_params(("parallel", "arbitrary")),
        name="nsa_dec_cmp",
    )(pt, *([cache_t] * PAGES_PER_STEP), perm, bd, bias, zb)

    sel_steps = sel[:, :, :sel_used].reshape(DB, NSA_GROUPS * T, n_steps + 1, blocks_per_step).transpose(0, 2, 1, 3)
    sel_steps = jnp.pad(sel_steps, ((0, 0), (0, 0), (0, 0), (0, LANES - blocks_per_step)))
    kk = np.arange(PAGES_PER_STEP * page_rows) // SEL_LEN
    expand = jnp.asarray(kk[None, :] == np.arange(LANES)[:, None], BF16)
    pad_rows = LANES - T
    new_rows = lambda slot: jnp.pad(zcb[:, :, slot * NSA_KV:(slot + 1) * NSA_KV], ((0, 0), (0, pad_rows), (0, 0)))
    new_spec = pl.BlockSpec((1, LANES, LANES), lambda b, s, pt: (b, 0, 0))
    sel_blk = (1, 1, NSA_GROUPS * T, LANES)

    o_s = pl.pallas_call(
        functools.partial(_nsa_dec_sel_kernel, T=T, past_len=past_len),
        grid_spec=pltpu.PrefetchScalarGridSpec(
            num_scalar_prefetch=1, grid=(DB, n_steps),
            in_specs=_page_specs(page_blk, 1, layer, n_pages)
            + [q_spec, pl.BlockSpec(sel_blk, lambda b, s, pt: (b, s, 0, 0)),
               pl.BlockSpec(sel_blk, lambda b, s, pt: (b, n_steps, 0, 0)), const(expand), new_spec, new_spec],
            out_specs=pl.BlockSpec((1, NSA_HEADS * T, LANES), lambda b, s, pt: (b, 0, 0)),
            scratch_shapes=[pltpu.VMEM((NSA_HEADS * T, 1), F32), pltpu.VMEM((NSA_HEADS * T, 1), F32),
                            pltpu.VMEM((NSA_HEADS * T, LANES), F32)]),
        out_shape=jax.ShapeDtypeStruct((DB, NSA_HEADS * T, LANES), F32),
        compiler_params=_params(("parallel", "arbitrary")),
        name="nsa_dec_sel",
    )(pt, *([cache_t] * PAGES_PER_STEP), zb, sel_steps, sel_steps, expand, new_rows(2), new_rows(3))

    w_src = cache_win_kv.shape[1]
    win_t = jnp.transpose(cache_win_kv, (0, 2, 3, 4, 5, 1)).reshape(DB, DEPTH, 2, NSA_KV, w_src)
    b3 = lambda shp: pl.BlockSpec(shp, lambda b: (b, 0, 0))
    return pl.pallas_call(
        functools.partial(_nsa_dec_win_kernel, T=T, past_len=past_len),
        grid=(DB,),
        in_specs=[b3((1, T, NSA_QPAD)), b3((1, T, GATE_COLS)), b3((1, NSA_HEADS * T, LANES)), b3((1, NSA_HEADS * T, LANES)),
                  pl.BlockSpec((1, 1, 2, NSA_KV, w_src), lambda b: (b, layer, 0, 0, 0)),
                  b3((1, LANES, LANES)), b3((1, LANES, LANES))],
        out_specs=b3((1, T, NSA_QPAD)),
        out_shape=jax.ShapeDtypeStruct((DB, T, NSA_QPAD), BF16),
        compiler_params=_params(("parallel",)),
        name="nsa_dec_win",
    )(zb, zd, o_c, o_s, win_t, new_rows(4), new_rows(5))


def _route_top2(logits):
    lane = lax.broadcasted_iota(jnp.int32, logits.shape, 1)
    W = logits.shape[1]
    logits = jnp.where(lane < N_EXPERTS, logits, NEG_INF)
    m1 = jnp.max(logits, axis=-1, keepdims=True)
    i1 = jnp.min(jnp.where(logits == m1, lane, W), axis=-1, keepdims=True)
    rest = jnp.where(lane == i1, NEG_INF, logits)
    m2 = jnp.max(rest, axis=-1, keepdims=True)
    i2 = jnp.min(jnp.where(rest == m2, lane, W), axis=-1, keepdims=True)
    e2 = jnp.exp(m2 - m1)
    den = 1.0 + e2
    return jnp.where(lane == i1, 1.0 / den, 0.0) + jnp.where(lane == i2, e2 / den, 0.0)


def _router_kernel(x_ref, w_ref, b_ref, o_ref):
    o_ref[...] = _route_top2(_dot(x_ref[...], w_ref[...], HIGHEST) + b_ref[...])


def router(x2d, w_router, b_router, tm):
    n = x2d.shape[0]
    w = jnp.pad(w_router.astype(F32), ((0, 0), (0, LANES - N_EXPERTS)))
    bb = jnp.pad(b_router.astype(F32), (0, LANES - N_EXPERTS)).reshape(1, LANES)
    return pl.pallas_call(
        _router_kernel,
        grid=(n // tm,),
        in_specs=[pl.BlockSpec((tm, D_MODEL), lambda i: (i, 0)), pl.BlockSpec(w.shape, lambda i: (0, 0)),
                  pl.BlockSpec(bb.shape, lambda i: (0, 0))],
        out_specs=pl.BlockSpec((tm, LANES), lambda i: (i, 0)),
        out_shape=jax.ShapeDtypeStruct((n, LANES), F32),
        compiler_params=_params(("parallel",)),
        name="router",
    )(x2d, w, bb)


def _merge_kernel(x_ref, hm_ref, hn_ref, hx_ref, wg_ref, bg_ref, wbm_ref, wbn_ref, wbx_ref, wo_ref, g_ref, b_ref,
                  o_ref):
    x = x_ref[...]
    xb = x.astype(BF16)
    merged = None
    for c, (h_ref, w_ref) in enumerate(((hm_ref, wbm_ref), (hn_ref, wbn_ref), (hx_ref, wbx_ref))):
        sl = slice(c * D_MODEL, (c + 1) * D_MODEL)
        gate = jax.nn.sigmoid(_dot(xb, wg_ref[:, sl]) + bg_ref[:, sl])
        term = gate * _dot(h_ref[...], w_ref[...])
        merged = term if merged is None else merged + term
    mix = _dot(merged.astype(BF16), wo_ref[...])
    o_ref[...] = _layer_norm(ALPHA * x + mix, g_ref[...], b_ref[...])


def merge(x2d, hm, hn, hx, wg, bg, wbr, wo, g, b, tm):
    n = x2d.shape[0]
    row = lambda w: pl.BlockSpec((tm, w), lambda i: (i, 0))
    full = lambda a: pl.BlockSpec(a.shape, lambda i: (0,) * a.ndim, pipeline_mode=pl.Buffered(1))
    args = [x2d, hm, hn, hx, wg, bg, wbr[0], wbr[1], wbr[2], wo, g, b]
    return pl.pallas_call(
        _merge_kernel,
        grid=(n // tm,),
        in_specs=[row(a.shape[1]) for a in args[:4]] + [full(a) for a in args[4:]],
        out_specs=row(D_MODEL),
        out_shape=jax.ShapeDtypeStruct((n, D_MODEL), F32),
        compiler_params=_params(("parallel",)),
        name="merge",
    )(*args)


def _ffn_kernel(x_ref, wg_ref, wu_ref, wd_ref, g_ref, b_ref, o_ref, acc_s):
    f = pl.program_id(1)

    @pl.when(f == 0)
    def _():
        acc_s[...] = jnp.zeros(acc_s.shape, F32)

    xb = x_ref[...].astype(BF16)
    hcur = jax.nn.silu(_dot(xb, wg_ref[...])) * _dot(xb, wu_ref[...])
    acc_s[...] += _dot(hcur.astype(BF16), wd_ref[...])

    @pl.when(f == pl.num_programs(1) - 1)
    def _():
        o_ref[...] = _layer_norm(ALPHA * x_ref[...] + acc_s[...], g_ref[...], b_ref[...])


def ffn(x2d, w_up, w_down, g, b, tm, tf):
    n = x2d.shape[0]
    nf = (w_up.shape[1] // 2) // tf
    return pl.pallas_call(
        _ffn_kernel,
        grid=(n // tm, nf),
        in_specs=[pl.BlockSpec((tm, D_MODEL), lambda i, f: (i, 0)),
                  pl.BlockSpec((D_MODEL, tf), lambda i, f: (0, f)),
                  pl.BlockSpec((D_MODEL, tf), lambda i, f: (0, nf + f)),
                  pl.BlockSpec((tf, D_MODEL), lambda i, f: (f, 0)),
                  pl.BlockSpec((1, D_MODEL), lambda i, f: (0, 0)),
                  pl.BlockSpec((1, D_MODEL), lambda i, f: (0, 0))],
        out_specs=pl.BlockSpec((tm, D_MODEL), lambda i, f: (i, 0)),
        out_shape=jax.ShapeDtypeStruct((n, D_MODEL), F32),
        scratch_shapes=[pltpu.VMEM((tm, D_MODEL), F32)],
        compiler_params=_params(("parallel", "arbitrary")),
        name="dense_ffn",
    )(x2d, w_up, w_up, w_down, g, b)


def _moe_kernel(np_ref, x_ref, cw_ref, cwt_ref, wg_ref, wu_ref, wd_ref, g_ref, b_ref, o_ref,
                xb_s, y_s, pos_s, post_s, xg_s, ws_s, acc_s, *, blk, cap, cap_pad):
    j = pl.program_id(0)
    e = pl.program_id(1)
    f = pl.program_id(2)
    n_e = pl.num_programs(1)
    n_f = pl.num_programs(2)
    n_pass = np_ref[j * n_e + e]
    RC = min(256, blk)

    @pl.when((e == 0) & (f == 0))
    def _():
        xb_s[...] = x_ref[...].astype(BF16)
        y_s[...] = jnp.zeros(y_s.shape, F32)
        cw = cw_ref[...]
        cwt = cwt_ref[...]
        routed = jnp.where(cw != 0.0, 1.0, 0.0).astype(BF16)
        routed_t = jnp.where(cwt != 0.0, 1.0, 0.0).astype(BF16)
        for rc in range(blk // RC):
            rows_i = rc * RC + lax.broadcasted_iota(jnp.int32, (RC, blk), 0)
            cols_i = lax.broadcasted_iota(jnp.int32, (RC, blk), 1)
            before = jnp.where(cols_i < rows_i, 1.0, 0.0).astype(BF16)
            cnt = _dot(before, routed)
            pos_s[rc * RC:(rc + 1) * RC, :] = jnp.where(cw[rc * RC:(rc + 1) * RC] != 0.0, cnt, -1.0)
            rows_j = lax.broadcasted_iota(jnp.int32, (blk, RC), 0)
            cols_j = rc * RC + lax.broadcasted_iota(jnp.int32, (blk, RC), 1)
            before_t = jnp.where(rows_j < cols_j, 1.0, 0.0).astype(BF16)
            cnt_t = _dot(routed_t, before_t)
            post_s[:, rc * RC:(rc + 1) * RC] = jnp.where(cwt[:, rc * RC:(rc + 1) * RC] != 0.0, cnt_t, -1.0)

    @pl.when(f == 0)
    def _():
        prow = post_s[pl.ds(e, 1), :]
        wrow = cwt_ref[pl.ds(e, 1), :]

        def gather(u, carry):
            rr = lax.broadcasted_iota(jnp.int32, (cap_pad, 1), 0)
            slot = jnp.where(rr < cap, u * cap + rr, NO_SLOT).astype(F32)
            hit = prow == slot
            xg_s[u] = _dot(jnp.where(hit, 1.0, 0.0).astype(BF16), xb_s[...]).astype(BF16)
            ws_s[u] = jnp.sum(jnp.where(hit, wrow, 0.0), axis=-1, keepdims=True)
            acc_s[u] = jnp.zeros(acc_s.shape[1:], F32)
            return carry

        lax.fori_loop(0, n_pass, gather, 0)

    def expert(u, carry):
        xg = xg_s[u, 0:cap, :]
        hcur = jax.nn.silu(_dot(xg, wg_ref[0])) * _dot(xg, wu_ref[0]) * ws_s[u, 0:cap, :]
        acc_s[u, 0:cap, :] += _dot(hcur.astype(BF16), wd_ref[0])
        return carry

    lax.fori_loop(0, n_pass, expert, 0)

    @pl.when(f == n_f - 1)
    def _():
        lane = lax.broadcasted_iota(jnp.int32, pos_s.shape, 1)
        pcol = jnp.sum(jnp.where(lane == e, pos_s[...], 0.0), axis=-1, keepdims=True)

        def scatter(u, carry):
            cc = lax.broadcasted_iota(jnp.int32, (1, cap_pad), 1)
            slot = jnp.where(cc < cap, u * cap + cc, NO_SLOT).astype(F32)
            hit = jnp.where(pcol == slot, 1.0, 0.0).astype(BF16)
            y_s[...] += _dot(hit, acc_s[u].astype(BF16))
            return carry

        lax.fori_loop(0, n_pass, scatter, 0)

    @pl.when((e == n_e - 1) & (f == n_f - 1))
    def _():
        o_ref[...] = _layer_norm(ALPHA * x_ref[...] + y_s[...], g_ref[...], b_ref[...])


def moe(x2d, cw, w_up, w_down, g, b, tf):
    n = x2d.shape[0]
    E, _, F2 = w_up.shape
    nf = (F2 // 2) // tf
    blk = _row_tile(n, MOE_BLOCK)
    cap = min(MOE_CAP, blk)
    cap_pad = -(-cap // LANES) * LANES
    max_pass = -(-blk // cap)
    nblk = n // blk
    cwt = jnp.swapaxes(cw[:, :ROUTE_ROWS], 0, 1)
    counts = jnp.sum((cw[:, :E] != 0.0).reshape(nblk, blk, E), axis=1)
    n_pass = ((counts + cap - 1) // cap).astype(jnp.int32).reshape(-1)
    return pl.pallas_call(
        functools.partial(_moe_kernel, blk=blk, cap=cap, cap_pad=cap_pad),
        grid_spec=pltpu.PrefetchScalarGridSpec(
            num_scalar_prefetch=1, grid=(nblk, E, nf),
            in_specs=[pl.BlockSpec((blk, D_MODEL), lambda j, e, f, npr: (j, 0), pipeline_mode=pl.Buffered(1)),
                      pl.BlockSpec((blk, LANES), lambda j, e, f, npr: (j, 0)),
                      pl.BlockSpec((ROUTE_ROWS, blk), lambda j, e, f, npr: (0, j)),
                      pl.BlockSpec((1, D_MODEL, tf), lambda j, e, f, npr: (e, 0, f)),
                      pl.BlockSpec((1, D_MODEL, tf), lambda j, e, f, npr: (e, 0, nf + f)),
                      pl.BlockSpec((1, tf, D_MODEL), lambda j, e, f, npr: (e, f, 0)),
                      pl.BlockSpec((1, D_MODEL), lambda j, e, f, npr: (0, 0)),
                      pl.BlockSpec((1, D_MODEL), lambda j, e, f, npr: (0, 0))],
            out_specs=pl.BlockSpec((blk, D_MODEL), lambda j, e, f, npr: (j, 0)),
            scratch_shapes=[pltpu.VMEM((blk, D_MODEL), BF16), pltpu.VMEM((blk, D_MODEL), F32),
                            pltpu.VMEM((blk, LANES), F32), pltpu.VMEM((ROUTE_ROWS, blk), F32),
                            pltpu.VMEM((max_pass, cap_pad, D_MODEL), BF16), pltpu.VMEM((max_pass, cap_pad, 1), F32),
                            pltpu.VMEM((max_pass, cap_pad, D_MODEL), F32)]),
        out_shape=jax.ShapeDtypeStruct((n, D_MODEL), F32),
        compiler_params=_params(("parallel", "arbitrary", "arbitrary")),
        name="moe_ffn",
    )(n_pass, x2d, cw, cwt, w_up, w_up, w_down, g, b)


def _layer(x, lw, l, *, mem_kv, mem_layer, mstate, decode):
    B, T, _ = x.shape
    n = B * T
    x2d = x.reshape(n, D_MODEL)
    tm = _row_tile(n, 1024)
    za, zb, zc, zcb, zd = in_proj(x2d, lw["in"], tm)
    za, zb, zc, zcb, zd = (a.reshape(B, T, -1) for a in (za, zb, zc, zcb, zd))

    if T % MLSTM_CHUNK == 0:
        h_m, ct, nt, mt = mlstm(za, zd, *mstate, lw["norm_g"], L=MLSTM_CHUNK, t_valid=MLSTM_CHUNK)
    else:
        padt = lambda a: jnp.pad(a, ((0, 0), (0, LANES - T), (0, 0)))
        h_m, ct, nt, mt = mlstm(padt(za), padt(zd), *mstate, lw["norm_g"], L=LANES, t_valid=T)
        h_m = h_m[:, :T]

    if decode is None:
        kc, vc = cmp_tokens(zc, lw["cmp_w"], lw["cmp_b"])
        h_n = nsa_prompt(zb, zd, zcb, kc, vc, tq=min(256, T), CH=min(512, T))
    else:
        h_n = nsa_decode(zb, zd, zcb, decode["nsa"], decode["win"], decode["pt"], lw["cmp_w"], lw["cmp_b"], l)

    h_x = mem_attn(zb, mem_kv, mem_layer, tq=_row_tile(T, 512))

    wg, bg = lw["in"]["mg"]
    flat = lambda a: a.reshape(n, -1)
    x1 = merge(x2d, flat(h_m), flat(h_n), flat(h_x), wg, bg, lw["w_branch"], lw["w_out"], lw["ln1_g"], lw["ln1_b"], tm)

    if l % 2 == 0:
        x2 = ffn(x1, lw["ffn_up"], lw["ffn_down"], lw["ln2_g"], lw["ln2_b"], _row_tile(n, 512), D_FF // 2)
    else:
        cw = router(x1, lw["w_router"], lw["b_router"], _row_tile(n, 1024))
        x2 = moe(x1, cw, lw["ffn_up"], lw["ffn_down"], lw["ln2_g"], lw["ln2_b"], D_FF_EXPERT // 2)
    return x2.reshape(B, T, D_MODEL), zc, (ct, nt, mt)


def kernel(x_prompt, x_sample, mem_prompt, cache_nsa_kv, cache_win_kv, state_mlstm_C, state_mlstm_n,
           state_mlstm_m, cache_mem_kv, page_table, w_in, b_in, mlstm_norm_g, cmp_w, cmp_b, w_mem_kv,
           w_branch, w_out, ln1_g, ln1_b, ln2_g, ln2_b, ffn_w_up, ffn_w_down, moe_w_router, moe_b_router,
           moe_w_up, moe_w_down):
    B, T, _ = x_prompt.shape
    DB, TS, _ = x_sample.shape
    xp, xs = x_prompt, x_sample
    nsa_p, nsa_s, win_p, win_s = [], [], [], []
    Cp, np_, mp, Cs, ns, ms, memkv_p = [], [], [], [], [], [], []
    row1 = lambda a: a.reshape(1, -1).astype(F32)
    for l in range(DEPTH):
        wbr = (w_branch[l, 0].astype(BF16), _pad_branch_rows(w_branch[l, 1]).astype(BF16), w_branch[l, 2].astype(BF16))
        lw = dict(norm_g=mlstm_norm_g[l], cmp_w=cmp_w[l], cmp_b=cmp_b[l], w_branch=wbr,
                  w_out=w_out[l].astype(BF16), ln1_g=row1(ln1_g[l]), ln1_b=row1(ln1_b[l]),
                  ln2_g=row1(ln2_g[l]), ln2_b=row1(ln2_b[l]))
        lw["in"] = _split_in_proj(w_in[l], b_in[l])
        if l % 2 == 0:
            lw["ffn_up"] = ffn_w_up[l // 2].astype(BF16)
            lw["ffn_down"] = ffn_w_down[l // 2].astype(BF16)
        else:
            lw["ffn_up"] = moe_w_up[l // 2].astype(BF16)
            lw["ffn_down"] = moe_w_down[l // 2].astype(BF16)
            lw["w_router"] = moe_w_router[l // 2]
            lw["b_router"] = moe_b_router[l // 2]

        mkv = proj(mem_prompt.reshape(B * MEM_LEN, D_MODEL), w_mem_kv[l].astype(BF16), _row_tile(B * MEM_LEN, 512))
        mkv = mkv.reshape(B, MEM_LEN, 2 * X_WIDTH)
        st0 = (jnp.zeros((B, M_HEADS, M_DH, M_DH), F32), jnp.zeros((B, M_HEADS, M_DH), F32), jnp.zeros((B, M_HEADS), F32))
        xp, zc, st = _layer(xp, lw, l, mem_kv=mkv.reshape(B, MEM_LEN, 1, 2, X_HEADS, X_DH), mem_layer=0, mstate=st0,
                            decode=None)
        kvn = zc.reshape(B, T, 6, NSA_GROUPS, NSA_DH)
        nsa_p.append(kvn[:, :, :4])
        win_p.append(kvn[:, T - min(WINDOW, T):, 4:])
        Cp.append(st[0]); np_.append(st[1]); mp.append(st[2])
        memkv_p.append(mkv.reshape(B, MEM_LEN, 2, X_HEADS, X_DH))

        sts = (state_mlstm_C[:, l], state_mlstm_n[:, l], state_mlstm_m[:, l])
        dec = dict(nsa=cache_nsa_kv, win=cache_win_kv, pt=page_table)
        xs, zc, st = _layer(xs, lw, l, mem_kv=cache_mem_kv, mem_layer=l, mstate=sts, decode=dec)
        kvn = zc.reshape(DB, TS, 6, NSA_GROUPS, NSA_DH)
        nsa_s.append(kvn[:, :, :4])
        win_s.append(jnp.concatenate([cache_win_kv[:, :, l].astype(F32), kvn[:, :, 4:]], axis=1)[:, TS:])
        Cs.append(st[0]); ns.append(st[1]); ms.append(st[2])
    return (xp, xs,
            jnp.stack(nsa_p, axis=2), jnp.stack(nsa_s, axis=2),
            jnp.stack(win_p, axis=2), jnp.stack(win_s, axis=2),
            jnp.stack(Cp, axis=1), jnp.stack(np_, axis=1), jnp.stack(mp, axis=1),
            jnp.stack(Cs, axis=1), jnp.stack(ns, axis=1), jnp.stack(ms, axis=1),
            jnp.stack(memkv_p, axis=2))
```

```python
import functools

import numpy as np
import jax
import jax.numpy as jnp
from jax import lax
from jax.experimental import pallas as pl
from jax.experimental.pallas import tpu as pltpu

D_MODEL = 1024
DEPTH = 2
BRANCH_WIDTH = 512
N_BRANCH = 3
M_HEADS = 4
M_DH = BRANCH_WIDTH // M_HEADS
M_WIDTH = M_HEADS * M_DH
NSA_HEADS = 8
NSA_DH = BRANCH_WIDTH // NSA_HEADS
NSA_WIDTH = NSA_HEADS * NSA_DH
NSA_GROUPS = 2
NSA_HPG = NSA_HEADS // NSA_GROUPS
NSA_KV = NSA_GROUPS * NSA_DH
CMP_LEN = 32
CMP_STRIDE = 16
SEL_LEN = 64
SEL_TOPN = 16
WINDOW = 512
MEM_LEN = 256
X_HEADS = 4
X_DH = BRANCH_WIDTH // X_HEADS
X_WIDTH = X_HEADS * X_DH
D_FF = 2816
N_EXPERTS = 8
TOP_K = 2
D_FF_EXPERT = 3584
ALPHA = (2.0 * DEPTH) ** 0.25
LN_EPS = 1e-5
IN_SPLITS = (M_WIDTH, M_WIDTH, M_WIDTH, M_HEADS, M_HEADS, M_WIDTH,
             NSA_WIDTH, 6 * NSA_KV, 3 * NSA_HEADS, X_WIDTH, N_BRANCH * D_MODEL)
LOG2E = 1.4426950408889634
NSA_SLOPES = tuple(LOG2E * 2.0 ** (-8.0 * (h + 1) / NSA_HEADS) for h in range(NSA_HEADS))
MASK_BIAS = -(2.0 ** 30)
SEL_FEAT = 64
SEL_QUARTERS = 4
SLOPE_PARTS = 3

LANES = 128
SUBLANES = 8
VMEM_LIMIT = 56 * 1024 * 1024
PAGES_PER_STEP = 64
MLSTM_SEQS = 1
MLSTM_CHUNK = 256
GATE_COLS = LANES
NG_OFF = 2 * M_HEADS
NSA_QPAD = NSA_HEADS * LANES
MOE_BLOCK = 1024
MOE_CAP = 288
NO_SLOT = -2
ROUTE_ROWS = 16

F32 = jnp.float32
BF16 = jnp.bfloat16
NEG_INF = float("-inf")
HIGHEST = lax.Precision.HIGHEST


def _dot(a, b, precision=None):
    return jnp.dot(a, b, preferred_element_type=F32, precision=precision)


def _dot_nt(a, b, precision=None):
    return lax.dot_general(a, b, (((1,), (1,)), ((), ())), preferred_element_type=F32, precision=precision)


def _dot_tn(a, b):
    return lax.dot_general(a, b, (((0,), (0,)), ((), ())), preferred_element_type=F32)


def _params(sem):
    return pltpu.CompilerParams(dimension_semantics=sem, vmem_limit_bytes=VMEM_LIMIT)


def _masked_softmax2(s, mask):
    s = jnp.where(mask, s, NEG_INF)
    mx = jnp.max(s, axis=-1, keepdims=True)
    mx = jnp.where(mx > NEG_INF, mx, 0.0)
    p = jnp.where(mask, jnp.exp2(s - mx), 0.0)
    return p / jnp.maximum(jnp.sum(p, axis=-1, keepdims=True), 1e-30)


def _layer_norm(xf, g, b):
    mu = jnp.mean(xf, axis=-1, keepdims=True)
    var = jnp.mean(jnp.square(xf - mu), axis=-1, keepdims=True)
    return (xf - mu) * lax.rsqrt(var + LN_EPS) * g + b


def _row_tile(n, pref):
    return pref if n % pref == 0 else n


def _in_proj_kernel(x_ref, wa_ref, wb_ref, wc_ref, wd_ref, ba_ref, bb_ref, bc_ref, bd_ref,
                    oa_ref, ob_ref, oc_ref, ocb_ref, od_ref):
    x = x_ref[...].astype(BF16)
    oa_ref[...] = (_dot(x, wa_ref[...]) + ba_ref[...]).astype(BF16)
    ob_ref[...] = (_dot(x, wb_ref[...]) + bb_ref[...]).astype(BF16)
    c = _dot(x, wc_ref[...]) + bc_ref[...]
    oc_ref[...] = c
    ocb_ref[...] = c.astype(BF16)
    od_ref[...] = _dot(x, wd_ref[...]) + bd_ref[...]


def _group_select():
    return jax.nn.one_hot(np.arange(NSA_HEADS) // NSA_HPG, NSA_GROUPS, dtype=F32)


def _pad_branch_rows(w_nsa):
    return jnp.einsum("hem,hg->hgem", w_nsa.reshape(NSA_HEADS, NSA_DH, -1), _group_select()).reshape(NSA_QPAD, -1)


def _split_in_proj(w_in_l, b_in_l):
    offs = np.cumsum((0,) + IN_SPLITS)
    w = [w_in_l[:, offs[i]:offs[i + 1]] for i in range(len(IN_SPLITS))]
    b = [b_in_l[offs[i]:offs[i + 1]] for i in range(len(IN_SPLITS))]
    mq, mk, mv, mi, mf, mo, nq, nkv, ng, xq, mg = range(11)
    pad = GATE_COLS - 2 * M_HEADS - 3 * NSA_HEADS

    def cat(ids, zpad=0):
        ww = jnp.concatenate([w[i] for i in ids], axis=1)
        bb = jnp.concatenate([b[i] for i in ids])
        if zpad:
            ww = jnp.pad(ww, ((0, 0), (0, zpad)))
            bb = jnp.pad(bb, (0, zpad))
        return ww.astype(BF16), bb.reshape(1, -1).astype(F32)

    scale = NSA_DH ** -0.5 * LOG2E
    w[nq] = jnp.einsum("dhe,hg->dhge", w[nq].reshape(D_MODEL, NSA_HEADS, NSA_DH) * scale, _group_select()).reshape(D_MODEL, NSA_QPAD)
    b[nq] = jnp.einsum("he,hg->hge", b[nq].reshape(NSA_HEADS, NSA_DH) * scale, _group_select()).reshape(NSA_QPAD)
    return dict(a=cat([mq, mk, mv, mo]), b=cat([nq, xq]), c=cat([nkv]), d=cat([mi, mf, ng], pad), mg=cat([mg]))


def in_proj(x2d, wp, tm):
    n = x2d.shape[0]
    (wa, ba), (wb, bb), (wc, bc), (wd, bd) = wp["a"], wp["b"], wp["c"], wp["d"]
    full = lambda arr: pl.BlockSpec(arr.shape, lambda i: (0, 0), pipeline_mode=pl.Buffered(1))
    row = lambda w: pl.BlockSpec((tm, w), lambda i: (i, 0))
    return pl.pallas_call(
        _in_proj_kernel,
        grid=(n // tm,),
        in_specs=[row(D_MODEL), full(wa), full(wb), full(wc), full(wd), full(ba), full(bb), full(bc), full(bd)],
        out_specs=[row(wa.shape[1]), row(wb.shape[1]), row(wc.shape[1]), row(wc.shape[1]), row(wd.shape[1])],
        out_shape=[jax.ShapeDtypeStruct((n, wa.shape[1]), BF16), jax.ShapeDtypeStruct((n, wb.shape[1]), BF16),
                   jax.ShapeDtypeStruct((n, wc.shape[1]), F32), jax.ShapeDtypeStruct((n, wc.shape[1]), BF16),
                   jax.ShapeDtypeStruct((n, wd.shape[1]), F32)],
        compiler_params=_params(("parallel",)),
        name="in_proj",
    )(x2d, wa, wb, wc, wd, ba, bb, bc, bd)


def _proj_kernel(x_ref, w_ref, o_ref):
    o_ref[...] = _dot(x_ref[...].astype(BF16), w_ref[...])


def proj(x2d, w_bf16, tm):
    n, k = x2d.shape
    m = w_bf16.shape[1]
    return pl.pallas_call(
        _proj_kernel,
        grid=(n // tm,),
        in_specs=[pl.BlockSpec((tm, k), lambda i: (i, 0)), pl.BlockSpec((k, m), lambda i: (0, 0))],
        out_specs=pl.BlockSpec((tm, m), lambda i: (i, 0)),
        out_shape=jax.ShapeDtypeStruct((n, m), F32),
        compiler_params=_params(("parallel",)),
        name="mem_proj",
    )(x2d, w_bf16)


def _log_sigmoid(x):
    return jnp.minimum(x, 0.0) - jnp.log1p(jnp.exp(-jnp.abs(x)))


def _mlstm_kernel(q_ref, k_ref, v_ref, og_ref, gc_ref, gr_ref, c0_ref, n0_ref, m0_ref, ng_ref,
                  h_ref, ct_ref, nt_ref, mt_ref, c_s, n_s, m_s, *, L, t_valid, nb):
    ci = pl.program_id(1)

    @pl.when(ci == 0)
    def _():
        c_s[...] = c0_ref[...]
        n_s[...] = n0_ref[...]
        m_s[...] = m0_ref[...]

    row = lax.broadcasted_iota(jnp.int32, (L, L), 0)
    col = lax.broadcasted_iota(jnp.int32, (L, L), 1)
    causal = row >= col
    tri = causal.astype(F32)
    tri_t = (row <= col).astype(F32)
    rvalid = lax.broadcasted_iota(jnp.int32, (L, GATE_COLS), 0) < t_valid
    cvalid = lax.broadcasted_iota(jnp.int32, (2 * M_HEADS, L), 1) < t_valid
    scale = M_DH ** -0.5

    for bb, h in [(bb, h) for bb in range(nb) for h in range(M_HEADS)]:
        if h == 0:
            gc = gc_ref[bb]
            gr = gr_ref[bb]
            lf_c = jnp.where(rvalid, _log_sigmoid(gc), 0.0)
            lf_r = jnp.where(cvalid, _log_sigmoid(gr), 0.0)
            ig_c = jnp.where(rvalid, gc, NEG_INF)
            ig_r = jnp.where(cvalid, gr, NEG_INF)
            b_c = _dot(tri, lf_c, HIGHEST)
            b_r = _dot(lf_r, tri_t, HIGHEST)
        sl = slice(h * M_DH, (h + 1) * M_DH)
        q = q_ref[bb, :, sl]
        k = k_ref[bb, :, sl]
        v = v_ref[bb, :, sl]
        bc = b_c[:, M_HEADS + h:M_HEADS + h + 1]
        br = b_r[M_HEADS + h:M_HEADS + h + 1, :]
        igc = ig_c[:, h:h + 1]
        igr = ig_r[h:h + 1, :]
        m_prev = m_s[bb, h, 0:1, 0:1]
        c_prev = c_s[bb, h]
        n_prev = n_s[bb, h:h + 1, :]

        dmat = jnp.where(causal, bc - br + igr, NEG_INF)
        inter = bc + m_prev
        m_t = jnp.maximum(inter, jnp.max(dmat, axis=-1, keepdims=True))
        s = _dot_nt(q, k) * scale * jnp.exp(dmat - m_t)
        sc_in = jnp.exp(inter - m_t)
        qf = q.astype(F32)
        num = sc_in * _dot_nt(q, c_prev.astype(BF16)) + _dot(s.astype(BF16), v)
        den = sc_in * jnp.sum(qf * n_prev, axis=-1, keepdims=True) + jnp.sum(s, axis=-1, keepdims=True)
        hh = num / jnp.maximum(jnp.abs(den), jnp.exp(-m_t))

        b_last = bc[L - 1:L, :]
        dec_c = b_last - bc + igc
        dec_r = b_last - br + igr
        m_new = jnp.maximum(b_last + m_prev, jnp.max(dec_r, axis=-1, keepdims=True))
        ws_c = jnp.exp(dec_c - m_new) * scale
        sc = jnp.exp(b_last + m_prev - m_new)
        vf = v.astype(F32)
        kf = k.astype(F32)
        c_new = sc * c_prev + _dot_tn((vf * ws_c).astype(BF16), k)
        n_new = sc * n_prev + jnp.sum(kf * ws_c, axis=0, keepdims=True)
        c_s[bb, h] = c_new
        n_s[bb, h:h + 1, :] = n_new
        m_s[bb, h] = jnp.broadcast_to(m_new, m_s.shape[2:])

        og = og_ref[bb, :, sl].astype(F32)
        hh = hh * jax.nn.sigmoid(og)
        mu = jnp.mean(hh, axis=-1, keepdims=True)
        var = jnp.mean(jnp.square(hh - mu), axis=-1, keepdims=True)
        hn = (hh - mu) * lax.rsqrt(var + LN_EPS) * ng_ref[:, sl]
        h_ref[bb, :, sl] = hn.astype(h_ref.dtype)

    ct_ref[...] = c_s[...]
    nt_ref[...] = n_s[...]
    mt_ref[...] = m_s[...]


def mlstm(za, zd, c0, n0, m0, norm_g, *, L, t_valid):
    B, T, _ = za.shape
    nc = T // L
    gr = jnp.swapaxes(zd[:, :, :2 * M_HEADS], 1, 2)
    m0b = jnp.broadcast_to(m0.astype(F32)[:, :, None, None], (B, M_HEADS, SUBLANES, LANES))
    nb = MLSTM_SEQS if B % MLSTM_SEQS == 0 else 1
    colspec = lambda j: pl.BlockSpec((nb, L, M_WIDTH), lambda b, c, j=j: (b, c, j))
    st = lambda shp: pl.BlockSpec((nb,) + shp, lambda b, c: (b,) + (0,) * len(shp))
    kern = functools.partial(_mlstm_kernel, L=L, t_valid=t_valid, nb=nb)
    h, ct, nt, mt = pl.pallas_call(
        kern,
        grid=(B // nb, nc),
        in_specs=[colspec(0), colspec(1), colspec(2), colspec(3),
                  pl.BlockSpec((nb, L, GATE_COLS), lambda b, c: (b, c, 0)),
                  pl.BlockSpec((nb, 2 * M_HEADS, L), lambda b, c: (b, 0, c)),
                  st((M_HEADS, M_DH, M_DH)), st((M_HEADS, M_DH)), st((M_HEADS, SUBLANES, LANES)),
                  pl.BlockSpec((1, M_WIDTH), lambda b, c: (0, 0))],
        out_specs=[pl.BlockSpec((nb, L, M_WIDTH), lambda b, c: (b, c, 0)),
                   st((M_HEADS, M_DH, M_DH)), st((M_HEADS, M_DH)), st((M_HEADS, SUBLANES, LANES))],
        out_shape=[jax.ShapeDtypeStruct((B, T, M_WIDTH), BF16),
                   jax.ShapeDtypeStruct((B, M_HEADS, M_DH, M_DH), F32),
                   jax.ShapeDtypeStruct((B, M_HEADS, M_DH), F32),
                   jax.ShapeDtypeStruct((B, M_HEADS, SUBLANES, LANES), F32)],
        scratch_shapes=[pltpu.VMEM((nb, M_HEADS, M_DH, M_DH), F32), pltpu.VMEM((nb, M_HEADS, M_DH), F32),
                        pltpu.VMEM((nb, M_HEADS, SUBLANES, LANES), F32)],
        compiler_params=_params(("parallel", "arbitrary")),
        name="mlstm",
    )(za, za, za, za, zd, gr, c0.astype(F32), n0.astype(F32), m0b, norm_g.reshape(1, M_WIDTH).astype(F32))
    return h, ct, nt, mt[:, :, 0, 0]


def _mem_attn_kernel(q_ref, k_ref, v_ref, o_ref):
    scale = X_DH ** -0.5
    for h in range(X_HEADS):
        sl = slice(h * X_DH, (h + 1) * X_DH)
        s = _dot_nt(q_ref[0, :, sl], k_ref[:, h, :].astype(BF16)) * scale
        mx = jnp.max(s, axis=-1, keepdims=True)
        p = jnp.exp(s - mx)
        p = p / jnp.sum(p, axis=-1, keepdims=True)
        o_ref[0, :, sl] = _dot(p.astype(BF16), v_ref[:, h, :].astype(BF16)).astype(o_ref.dtype)


def mem_attn(zb, kv, layer, tq):
    B, T, _ = zb.shape
    kv_blk = (None, MEM_LEN, None, None, X_HEADS, X_DH)
    return pl.pallas_call(
        _mem_attn_kernel,
        grid=(B, T // tq),
        in_specs=[pl.BlockSpec((1, tq, X_WIDTH), lambda b, i: (b, i, NSA_QPAD // X_WIDTH)),
                  pl.BlockSpec(kv_blk, lambda b, i: (b, 0, layer, 0, 0, 0)),
                  pl.BlockSpec(kv_blk, lambda b, i: (b, 0, layer, 1, 0, 0))],
        out_specs=pl.BlockSpec((1, tq, X_WIDTH), lambda b, i: (b, i, 0)),
        out_shape=jax.ShapeDtypeStruct((B, T, X_WIDTH), BF16),
        compiler_params=_params(("parallel", "parallel")),
        name="mem_attn",
    )(zb, kv, kv)


def _head_rows(q_ref, nq):
    parts = [q_ref[0, :, hd * LANES:(hd + 1) * LANES] for hd in range(NSA_HEADS)]
    if nq % 16:
        return jnp.concatenate([p.astype(F32) for p in parts], axis=0).astype(BF16)
    return jnp.concatenate(parts, axis=0)


def _store_heads(o_ref, gate, o_c, o_s, o_w, nq):
    lane = lax.broadcasted_iota(jnp.int32, (nq, LANES), 1)
    for hd in range(NSA_HEADS):
        r = slice(hd * nq, (hd + 1) * nq)
        gcol = lambda br: gate[:, NG_OFF + br * NSA_HEADS + hd:NG_OFF + br * NSA_HEADS + hd + 1]
        val = gcol(0) * o_c[r] + gcol(1) * o_s[r] + gcol(2) * o_w[r]
        keep = (lane >= NSA_DH) if hd // NSA_HPG == 1 else (lane < NSA_DH)
        o_ref[0, :, hd * LANES:(hd + 1) * LANES] = jnp.where(keep, val, 0.0).astype(o_ref.dtype)


def _overlap(tok, blk):
    c_start = tok * CMP_STRIDE
    s_start = blk * SEL_LEN
    return ((c_start < s_start + SEL_LEN) & (c_start + CMP_LEN > s_start)).astype(F32)


def _select_blocks(imp, tpos, n_sel, n_top):
    nq, W = imp.shape
    blk = lax.broadcasted_iota(jnp.int32, (nq, W), 1)
    cur = tpos // SEL_LEN
    forced = (blk == 0) | (blk == cur) | (blk == cur - 1)
    v = jnp.where(forced, jnp.inf, jnp.where(blk <= cur, imp, NEG_INF))
    v = jnp.where(blk < n_sel, v, NEG_INF)
    ahead = jnp.zeros((nq, W), F32)
    for j in range(n_sel):
        vj = v[:, j:j + 1]
        ahead = ahead + jnp.where(vj > v, 1.0, jnp.where(vj == v, jnp.where(blk > j, 1.0, 0.0), 0.0))
    return jnp.where((ahead < n_top) & (blk < n_sel), 1.0, 0.0)


def _select_blocks_t(imp_t, tpos_row, n_sel, n_top):
    n_blk, nq = imp_t.shape
    blk = lax.broadcasted_iota(jnp.int32, (n_blk, nq), 0)
    cur = tpos_row // SEL_LEN
    forced = (blk == 0) | (blk == cur) | (blk == cur - 1)
    v = jnp.where(forced, jnp.inf, jnp.where(blk <= cur, imp_t, NEG_INF))
    v = jnp.where(blk < n_sel, v, NEG_INF)
    n_rg = n_blk // SUBLANES
    vg = [v[rg * SUBLANES:(rg + 1) * SUBLANES] for rg in range(n_rg)]
    bg = [rg * SUBLANES + lax.broadcasted_iota(jnp.int32, (SUBLANES, nq), 0) for rg in range(n_rg)]
    ahead = [jnp.zeros((SUBLANES, nq), F32) for _ in range(n_rg)]
    for j in range(n_sel):
        vj = v[j:j + 1, :]
        for rg in range(n_rg):
            if rg * SUBLANES > j:
                inc = jnp.where(vj >= vg[rg], 1.0, 0.0)
            elif (rg + 1) * SUBLANES - 1 <= j:
                inc = jnp.where(vj > vg[rg], 1.0, 0.0)
            else:
                inc = jnp.where(vj > vg[rg], 1.0, jnp.where(vj == vg[rg], jnp.where(bg[rg] > j, 1.0, 0.0), 0.0))
            ahead[rg] = ahead[rg] + inc
    ahead = jnp.concatenate(ahead, axis=0)
    return jnp.where((ahead < n_top) & (blk < n_sel), 1.0, 0.0)


def _online_update(s, m_old, l_old):
    m_new = jnp.maximum(m_old, jnp.max(s, axis=-1, keepdims=True))
    alpha = jnp.exp2(m_old - m_new)
    p = jnp.exp2(s - m_new)
    return p, m_new, alpha, alpha * l_old + jnp.sum(p, axis=-1, keepdims=True)


def _cmp_tokens_kernel(xk_ref, xv_ref, bd_ref, b_ref, kc_ref, vc_ref, *, nsub):
    for c, (x_ref, o_ref) in enumerate(((xk_ref, kc_ref), (xv_ref, vc_ref))):
        a0 = jnp.zeros((nsub, LANES), F32)
        a1 = jnp.zeros((nsub, LANES), F32)
        for j in range(CMP_STRIDE):
            xj = x_ref[0, :, j, :].astype(BF16)
            a0 = a0 + _dot(xj, bd_ref[c, 0, j])
            a1 = a1 + _dot(xj, bd_ref[c, 1, j])
        tok = a0 + pltpu.roll(a1, nsub - 1, axis=0) + b_ref[c]
        o_ref[0] = tok.astype(o_ref.dtype)


def _cmp_blockdiag(cmp_w_l):
    R = CMP_LEN // CMP_STRIDE
    w = cmp_w_l.astype(F32).reshape(2, R, CMP_STRIDE, NSA_DH, NSA_DH)
    eye = jnp.eye(NSA_GROUPS, dtype=F32)
    return jnp.einsum("ab,crjde->crjadbe", eye, w).reshape(2, R, CMP_STRIDE, LANES, LANES).astype(BF16)


def _cmp_bias(cmp_b_l):
    return jnp.tile(cmp_b_l.astype(F32), (1, NSA_GROUPS)).reshape(2, 1, LANES)


def cmp_tokens(zc, cmp_w_l, cmp_b_l):
    B, T, W = zc.shape
    nsub = T // CMP_STRIDE
    x4 = zc.reshape(B, nsub, CMP_STRIDE, W)
    bd = _cmp_blockdiag(cmp_w_l)
    bias = _cmp_bias(cmp_b_l)
    spec = lambda j: pl.BlockSpec((1, nsub, CMP_STRIDE, LANES), lambda b, j=j: (b, 0, 0, j))
    return pl.pallas_call(
        functools.partial(_cmp_tokens_kernel, nsub=nsub),
        grid=(B,),
        in_specs=[spec(0), spec(1), pl.BlockSpec(bd.shape, lambda b: (0,) * 5), pl.BlockSpec(bias.shape, lambda b: (0, 0, 0))],
        out_specs=[pl.BlockSpec((1, nsub, LANES), lambda b: (b, 0, 0))] * 2,
        out_shape=[jax.ShapeDtypeStruct((B, nsub, LANES), BF16)] * 2,
        compiler_params=_params(("parallel",)),
        name="nsa_cmp_tokens",
    )(x4, x4, bd, bias)


def _nsa_prompt_kernel(q_ref, g_ref, kca_ref, vc_ref, ka_ref, vst_ref, kwa_ref, vw_ref, qf_ref, o_ref,
                       qaug_s, s_buf, p_buf, w_buf, pw_buf, pc_s, m_s, l_s, a_s, acc_s, oc_s, ow_s, *, tq, T, CH):
    start = pl.program_id(1) * tq
    n_cmp_rows = kca_ref.shape[1]
    n_sel = T // SEL_LEN
    n_top = min(SEL_TOPN, n_sel)
    n_blk = -(-n_sel // SUBLANES) * SUBLANES
    nq_all = NSA_HEADS * tq
    qw = nq_all // SEL_QUARTERS
    tpos_row = start + lax.broadcasted_iota(jnp.int32, (1, tq), 1)
    tpos_all = jnp.concatenate([tpos_row] * NSA_HEADS, axis=1)
    qcols = lambda qi: slice(qi * qw, (qi + 1) * qw)
    tiles = [(qi, ct, slice(qi * qw + ct * LANES, qi * qw + (ct + 1) * LANES), slice(ct * LANES, (ct + 1) * LANES))
             for qi in range(SEL_QUARTERS) for ct in range(qw // LANES)]

    qaug_s[:, 0:LANES] = _head_rows(q_ref, tq)
    for hd in range(NSA_HEADS):
        qaug_s[hd * tq:(hd + 1) * tq, LANES:2 * LANES] = jnp.broadcast_to(qf_ref[hd:hd + 1, :], (tq, LANES)).astype(BF16)

    def softmax_tile(s_t):
        mx = jnp.max(s_t, axis=0, keepdims=True)
        mx = jnp.where(mx > NEG_INF, mx, 0.0)
        p_t = jnp.exp2(s_t - mx)
        return p_t, jnp.maximum(jnp.sum(p_t, axis=0, keepdims=True), 1e-30)

    c_end = lax.broadcasted_iota(jnp.int32, (n_cmp_rows, 1), 0) * CMP_STRIDE + CMP_LEN - 1
    for qi in range(SEL_QUARTERS):
        s_buf[qi, 0:n_cmp_rows, :] = _dot_nt(kca_ref[0], qaug_s[qcols(qi), :])
    for qi, ct, cols, tc in tiles:
        p_t, den = softmax_tile(jnp.where(c_end <= tpos_all[:, cols], s_buf[qi, 0:n_cmp_rows, tc], NEG_INF))
        pc_s[:, cols] = p_t / den
    oc_s[...] = _dot_tn(vc_ref[0], pc_s[...].astype(BF16))
    ov_t = _overlap(lax.broadcasted_iota(jnp.int32, (n_blk, n_cmp_rows), 1),
                    lax.broadcasted_iota(jnp.int32, (n_blk, n_cmp_rows), 0))
    lane = lax.broadcasted_iota(jnp.int32, (tq, LANES), 1)
    feat_sel = [None] * NSA_HEADS
    for g in range(NSA_GROUPS):
        pg = pc_s[:, g * NSA_HPG * tq:(g * NSA_HPG + 1) * tq]
        for p in range(1, NSA_HPG):
            pg = pg + pc_s[:, (g * NSA_HPG + p) * tq:(g * NSA_HPG + p + 1) * tq]
        sel_t = _select_blocks_t(_dot(ov_t, pg, HIGHEST), tpos_row, n_sel, n_top)
        sel = jnp.concatenate([sel_t, jnp.zeros((LANES - n_blk, tq), F32)], axis=0).T
        unsel = jnp.where(lane < SEL_FEAT, (1.0 - sel) * MASK_BIAS, 0.0)
        for p in range(NSA_HPG):
            hd = g * NSA_HPG + p
            feat_sel[hd] = (unsel + qf_ref[hd:hd + 1, :]).astype(BF16)

    wk = WINDOW + tq
    base_w = pl.multiple_of(start, tq)
    wpos = start - WINDOW + lax.broadcasted_iota(jnp.int32, (wk, 1), 0)
    for qi in range(SEL_QUARTERS):
        w_buf[qi] = _dot_nt(kwa_ref[0, pl.ds(base_w, wk), :], qaug_s[qcols(qi), :])
    for qi, ct, cols, tc in tiles:
        tp = tpos_all[:, cols]
        first = jnp.maximum(tp - (WINDOW - 1), 0)
        s_t = jnp.where(wpos >= first, jnp.where(wpos <= tp, w_buf[qi, :, tc], NEG_INF), NEG_INF)
        p_t, den = softmax_tile(s_t)
        pw_buf[qi, :, tc] = p_t.astype(BF16)
        a_s[:, cols] = den
    for qi in range(SEL_QUARTERS):
        ow_s[:, qcols(qi)] = _dot_tn(vw_ref[0, pl.ds(base_w, wk), :], pw_buf[qi]) / a_s[:, qcols(qi)]

    for hd in range(NSA_HEADS):
        qaug_s[hd * tq:(hd + 1) * tq, LANES:2 * LANES] = feat_sel[hd]

    m_s[...] = jnp.full(m_s.shape, NEG_INF, F32)
    l_s[...] = jnp.zeros(l_s.shape, F32)
    acc_s[...] = jnp.zeros(acc_s.shape, F32)

    def issue(c, qi):
        base = pl.multiple_of(c * CH, CH)
        s_buf[qi] = _dot_nt(ka_ref[0, pl.ds(base, CH), :], qaug_s[qi * qw:(qi + 1) * qw, :])

    def absorb(c, qi, diag):
        for ct in range(qw // LANES):
            cols = slice(qi * qw + ct * LANES, qi * qw + (ct + 1) * LANES)
            s_t = s_buf[qi, :, ct * LANES:(ct + 1) * LANES]
            if diag:
                kpos = c * CH + lax.broadcasted_iota(jnp.int32, (CH, 1), 0)
                s_t = jnp.where(kpos <= tpos_all[:, cols], s_t, NEG_INF)
            m_old = m_s[:, cols]
            m_new = jnp.maximum(m_old, jnp.max(s_t, axis=0, keepdims=True))
            alpha = jnp.exp2(m_old - m_new)
            p_t = jnp.exp2(s_t - m_new)
            m_s[:, cols] = m_new
            a_s[:, cols] = alpha
            l_s[:, cols] = alpha * l_s[:, cols] + jnp.sum(p_t, axis=0, keepdims=True)
            p_buf[qi, :, ct * LANES:(ct + 1) * LANES] = p_t.astype(BF16)
        cols = slice(qi * qw, (qi + 1) * qw)
        acc_s[:, cols] = a_s[:, cols] * acc_s[:, cols] + _dot(vst_ref[0, c], p_buf[qi])

    def full_chunk(c, carry):
        for qi in range(SEL_QUARTERS):
            absorb(c, qi, False)
            issue(c + 1, qi)
        return carry

    n_full = start // CH
    for qi in range(SEL_QUARTERS):
        issue(0, qi)
    lax.fori_loop(0, n_full, full_chunk, 0)
    for qi in range(SEL_QUARTERS):
        absorb(n_full, qi, True)

    gate_t = jax.nn.sigmoid(g_ref[0]).T
    for hd in range(NSA_HEADS):
        cols = slice(hd * tq, (hd + 1) * tq)
        grow = lambda br: gate_t[NG_OFF + br * NSA_HEADS + hd:NG_OFF + br * NSA_HEADS + hd + 1, :]
        o_s = acc_s[:, cols] / jnp.maximum(l_s[:, cols], 1e-30)
        val = (grow(0) * oc_s[:, cols] + grow(1) * o_s + grow(2) * ow_s[:, cols]).T
        keep = (lane >= NSA_DH) if hd // NSA_HPG == 1 else (lane < NSA_DH)
        o_ref[0, :, hd * LANES:(hd + 1) * LANES] = jnp.where(keep, val, 0.0).astype(o_ref.dtype)


def _bf16_parts(x, n):
    parts = []
    for _ in range(n):
        p = float(np.asarray(x, np.float32).astype(jnp.bfloat16).astype(np.float32))
        parts.append(p)
        x = x - p
    return parts


def _slope_features():
    qf = np.zeros((NSA_HEADS, LANES), np.float32)
    for hd in range(NSA_HEADS):
        for p, s_p in enumerate(_bf16_parts(NSA_SLOPES[hd], SLOPE_PARTS)):
            qf[hd, SEL_FEAT + 2 * p] = s_p * SEL_LEN
            qf[hd, SEL_FEAT + 2 * p + 1] = s_p
    return jnp.asarray(qf)


def _key_features(pos, block_onehot):
    pos = np.asarray(pos)
    ok = pos >= 0
    kf = np.zeros((pos.shape[0], LANES), np.float32)
    if block_onehot:
        kf[np.arange(pos.shape[0])[ok], pos[ok] // SEL_LEN] = 1.0
    for p in range(SLOPE_PARTS):
        kf[ok, SEL_FEAT + 2 * p] = pos[ok] // SEL_LEN
        kf[ok, SEL_FEAT + 2 * p + 1] = pos[ok] % SEL_LEN
    return jnp.asarray(kf, BF16)


def nsa_prompt(zb, zd, zcb, kc, vc, tq, CH):
    B, T, _ = zb.shape
    nsub = kc.shape[1]
    assert T % CH == 0 and CH % tq == 0 and T // SEL_LEN <= SEL_FEAT and nsub <= CH
    qf = _slope_features()
    with_feats = lambda k, feats: jnp.concatenate([k, jnp.broadcast_to(feats[None], (B,) + feats.shape)], axis=2)
    k_aug = with_feats(zcb[:, :, 2 * NSA_KV:3 * NSA_KV], _key_features(np.arange(T), True))
    kc_aug = with_feats(kc, _key_features(np.arange(nsub) * CMP_STRIDE + CMP_LEN - 1, False))
    kw = jnp.pad(zcb[:, :, 4 * NSA_KV:5 * NSA_KV], ((0, 0), (WINDOW, 0), (0, 0)))
    kw_aug = with_feats(kw, _key_features(np.arange(T + WINDOW) - WINDOW, False))
    vs_t = jnp.swapaxes(zcb[:, :, 3 * NSA_KV:4 * NSA_KV].reshape(B, T // CH, CH, NSA_KV), 2, 3)
    vw = jnp.pad(zcb[:, :, 5 * NSA_KV:6 * NSA_KV], ((0, 0), (WINDOW, 0), (0, 0)))
    per_b = lambda rows_, w: pl.BlockSpec((1, rows_, w), lambda b, i: (b, 0, 0))
    kern = functools.partial(_nsa_prompt_kernel, tq=tq, T=T, CH=CH)
    nq_all = NSA_HEADS * tq
    qw = nq_all // SEL_QUARTERS
    wk = WINDOW + tq
    assert qw % LANES == 0
    return pl.pallas_call(
        kern,
        grid=(B, T // tq),
        in_specs=[pl.BlockSpec((1, tq, NSA_QPAD), lambda b, i: (b, i, 0)),
                  pl.BlockSpec((1, tq, GATE_COLS), lambda b, i: (b, i, 0)),
                  per_b(nsub, 2 * LANES), per_b(nsub, LANES),
                  per_b(T, 2 * LANES),
                  pl.BlockSpec((1, T // CH, NSA_KV, CH), lambda b, i: (b, 0, 0, 0)),
                  per_b(T + WINDOW, 2 * LANES), per_b(T + WINDOW, LANES),
                  pl.BlockSpec(qf.shape, lambda b, i: (0, 0))],
        out_specs=pl.BlockSpec((1, tq, NSA_QPAD), lambda b, i: (b, i, 0)),
        out_shape=jax.ShapeDtypeStruct((B, T, NSA_QPAD), BF16),
        scratch_shapes=[pltpu.VMEM((nq_all, 2 * LANES), BF16),
                        pltpu.VMEM((SEL_QUARTERS, CH, qw), F32), pltpu.VMEM((SEL_QUARTERS, CH, qw), BF16),
                        pltpu.VMEM((SEL_QUARTERS, wk, qw), F32), pltpu.VMEM((SEL_QUARTERS, wk, qw), BF16),
                        pltpu.VMEM((nsub, nq_all), F32),
                        pltpu.VMEM((1, nq_all), F32), pltpu.VMEM((1, nq_all), F32), pltpu.VMEM((1, nq_all), F32),
                        pltpu.VMEM((NSA_KV, nq_all), F32), pltpu.VMEM((NSA_KV, nq_all), F32),
                        pltpu.VMEM((NSA_KV, nq_all), F32)],
        compiler_params=_params(("parallel", "arbitrary")),
        name="nsa_prompt",
    )(zb, zd, kc_aug, vc, k_aug, vs_t, kw_aug, vw, qf)


def _page_specs(shape, slot_blk, layer, n_pages):
    def mk(i):
        def imap(b, s, pt):
            return (pt[b * n_pages + s * PAGES_PER_STEP + i], layer, slot_blk, 0, 0)
        return pl.BlockSpec(shape, imap)
    return [mk(i) for i in range(PAGES_PER_STEP)]


def _nsa_dec_cmp_kernel(pt_ref, *refs, T, past_len, n_rows):
    pages = refs[:PAGES_PER_STEP]
    perm_ref, bd_ref, b_ref, q_ref, oc_ref, sel_ref, tok_s, pend_s = refs[PAGES_PER_STEP:]
    s = pl.program_id(1)
    n_steps = pl.num_programs(1)
    page_rows = pages[0].shape[-1]
    sub = page_rows // CMP_STRIDE
    R = PAGES_PER_STEP * sub

    @pl.when(s == 0)
    def _():
        pend_s[...] = jnp.zeros(pend_s.shape, F32)

    rid = lax.broadcasted_iota(jnp.int32, (R, LANES), 0)
    for c in range(2):
        xs = [_dot_nt(perm_ref[...], pg[0, 0, c].astype(BF16)) for pg in pages]
        a0 = jnp.zeros((R, LANES), F32)
        a1 = jnp.zeros((R, LANES), F32)
        for j in range(CMP_STRIDE):
            xj = jnp.concatenate([x[j * sub:(j + 1) * sub] for x in xs], axis=0).astype(BF16)
            a0 = a0 + _dot(xj, bd_ref[c, 0, j])
            a1 = a1 + _dot(xj, bd_ref[c, 1, j])
        a0 = a0 + b_ref[c]
        tok = jnp.where(rid == 0, pend_s[c], pltpu.roll(a0, 1, axis=0)) + a1
        tok_s[c, pl.ds(pl.multiple_of(s * R, R), R), :] = tok
        pend_s[c] = a0[R - 1:R, :]

    @pl.when(s == n_steps - 1)
    def _():
        qp = _head_rows(q_ref, T)
        kc = tok_s[0].astype(BF16)
        vc = tok_s[1].astype(BF16)
        sc_all = _dot_nt(qp, kc)
        tpos = past_len + lax.broadcasted_iota(jnp.int32, (T, 1), 0)
        r = lax.broadcasted_iota(jnp.int32, (T, n_rows), 1)
        dc = tpos - ((r - 1) * CMP_STRIDE + CMP_LEN - 1)
        mask_c = (dc >= 0) & (r >= 1)
        dcf = dc.astype(F32)
        pcs = [_masked_softmax2(sc_all[hd * T:(hd + 1) * T] - NSA_SLOPES[hd] * dcf, mask_c) for hd in range(NSA_HEADS)]
        oc_ref[0] = _dot(jnp.concatenate(pcs, axis=0).astype(BF16), vc)
        W = sel_ref.shape[2]
        n_sel = -(-(past_len + T) // SEL_LEN)
        ov = _overlap(lax.broadcasted_iota(jnp.int32, (n_rows, W), 0) - 1, lax.broadcasted_iota(jnp.int32, (n_rows, W), 1))
        for g in range(NSA_GROUPS):
            pg = pcs[g * NSA_HPG]
            for p in range(1, NSA_HPG):
                pg = pg + pcs[g * NSA_HPG + p]
            imp = _dot(pg, ov, HIGHEST)
            sel_ref[0, g * T:(g + 1) * T, :] = _select_blocks(imp, tpos, n_sel, min(SEL_TOPN, n_sel))


def _nsa_dec_sel_kernel(pt_ref, *refs, T, past_len):
    pages = refs[:PAGES_PER_STEP]
    q_ref, sel_ref, sel_last_ref, ex_ref, kn_ref, vn_ref, os_ref, m_s, l_s, acc_s = refs[PAGES_PER_STEP:]
    s = pl.program_id(1)
    n_steps = pl.num_programs(1)
    page_rows = pages[0].shape[-1]
    CH = PAGES_PER_STEP * page_rows

    @pl.when(s == 0)
    def _():
        m_s[...] = jnp.full(m_s.shape, NEG_INF, F32)
        l_s[...] = jnp.zeros(l_s.shape, F32)
        acc_s[...] = jnp.zeros(acc_s.shape, F32)

    qp = _head_rows(q_ref, T)
    tpos = past_len + lax.broadcasted_iota(jnp.int32, (T, 1), 0)

    def update(s_all, mask_of_group, kpos, pv):
        ds = tpos - kpos
        dsf = ds.astype(F32)
        ps = []
        for g in range(NSA_GROUPS):
            mk = mask_of_group(g) & (ds >= 0)
            for p in range(NSA_HPG):
                hd = g * NSA_HPG + p
                r = slice(hd * T, (hd + 1) * T)
                sc = jnp.where(mk, s_all[r] - NSA_SLOPES[hd] * dsf, NEG_INF)
                pr, m_new, alpha, l_new = _online_update(sc, m_s[r], l_s[r])
                m_s[r] = m_new
                l_s[r] = l_new
                acc_s[r] = alpha * acc_s[r]
                ps.append(pr)
        acc_s[...] = acc_s[...] + pv(jnp.concatenate(ps, axis=0).astype(BF16))

    selk = _dot(sel_ref[0, 0].astype(BF16), ex_ref[...])
    s_all = jnp.concatenate([_dot(qp, pg[0, 0, 0].astype(BF16)) for pg in pages], axis=1)
    kpos = s * CH + lax.broadcasted_iota(jnp.int32, (T, CH), 1)

    def pv_pages(pmat):
        out = jnp.zeros((NSA_HEADS * T, LANES), F32)
        for i, pg in enumerate(pages):
            out = out + _dot_nt(pmat[:, i * page_rows:(i + 1) * page_rows], pg[0, 0, 1].astype(BF16))
        return out

    update(s_all, lambda g: selk[g * T:(g + 1) * T] > 0.5, kpos, pv_pages)

    @pl.when(s == n_steps - 1)
    def _():
        nk = kn_ref.shape[1]
        sn = _dot_nt(qp, kn_ref[0])
        lane = lax.broadcasted_iota(jnp.int32, (T, nk), 1)
        sl = sel_last_ref[0, 0]
        update(sn, lambda g: (sl[g * T:(g + 1) * T, 0:1] > 0.5) & (lane < T), past_len + lane,
               lambda pmat: _dot(pmat, vn_ref[0]))
        os_ref[0] = acc_s[...] / jnp.maximum(l_s[...], 1e-30)


def _nsa_dec_win_kernel(q_ref, g_ref, oc_ref, os_ref, wp_ref, kwn_ref, vwn_ref, o_ref, *, T, past_len):
    qp = _head_rows(q_ref, T)
    w_src = wp_ref.shape[-1]
    nk = kwn_ref.shape[1]
    sw_all = jnp.concatenate([_dot(qp, wp_ref[0, 0, 0].astype(BF16)), _dot_nt(qp, kwn_ref[0])], axis=1)
    j = lax.broadcasted_iota(jnp.int32, (T, w_src + nk), 1)
    tpos = past_len + lax.broadcasted_iota(jnp.int32, (T, 1), 0)
    wpos = past_len - w_src + j
    dw = tpos - wpos
    mask_w = (dw >= 0) & (dw < WINDOW) & (wpos >= 0) & (j < w_src + T)
    dwf = dw.astype(F32)
    pws = [_masked_softmax2(sw_all[hd * T:(hd + 1) * T] - NSA_SLOPES[hd] * dwf, mask_w) for hd in range(NSA_HEADS)]
    pw = jnp.concatenate(pws, axis=0).astype(BF16)
    o_w = _dot_nt(pw[:, :w_src], wp_ref[0, 0, 1].astype(BF16)) + _dot(pw[:, w_src:], vwn_ref[0])
    _store_heads(o_ref, jax.nn.sigmoid(g_ref[0]), oc_ref[0], os_ref[0], o_w, T)


def nsa_decode(zb, zd, zcb, cache_nsa_kv, cache_win_kv, page_table, cmp_w_l, cmp_b_l, layer):
    DB, T, _ = zb.shape
    n_pool, page_rows = cache_nsa_kv.shape[:2]
    n_pages = page_table.shape[1]
    past_len = n_pages * page_rows
    n_steps = n_pages // PAGES_PER_STEP
    sub_per_page = page_rows // CMP_STRIDE
    n_rows = n_pages * sub_per_page
    assert (n_rows - 1) * CMP_STRIDE + CMP_LEN - 1 > past_len + T - 1
    assert T <= CMP_STRIDE and n_pages % PAGES_PER_STEP == 0 and page_rows == LANES
    pt = page_table.reshape(-1).astype(jnp.int32)
    cache_t = jnp.transpose(cache_nsa_kv, (0, 2, 3, 4, 5, 1)).reshape(n_pool, DEPTH, 4, NSA_KV, page_rows)
    page_blk = (1, 1, 2, NSA_KV, page_rows)

    rr = np.arange(page_rows)
    perm = jnp.asarray(rr[None, :] == ((rr % sub_per_page) * CMP_STRIDE + rr // sub_per_page)[:, None], BF16)
    bd = _cmp_blockdiag(cmp_w_l)
    bias = _cmp_bias(cmp_b_l)

    n_sel = -(-(past_len + T) // SEL_LEN)
    blocks_per_step = PAGES_PER_STEP * page_rows // SEL_LEN
    sel_used = (n_steps + 1) * blocks_per_step
    sel_w = -(-sel_used // LANES) * LANES
    assert sel_used >= n_sel and blocks_per_step <= LANES
    q_spec = pl.BlockSpec((1, T, NSA_QPAD), lambda b, s, pt: (b, 0, 0))
    const = lambda a: pl.BlockSpec(a.shape, lambda b, s, pt: (0,) * a.ndim)

    o_c, sel = pl.pallas_call(
        functools.partial(_nsa_dec_cmp_kernel, T=T, past_len=past_len, n_rows=n_rows),
        grid_spec=pltpu.PrefetchScalarGridSpec(
            num_scalar_prefetch=1, grid=(DB, n_steps),
            in_specs=_page_specs(page_blk, 0, layer, n_pages) + [const(perm), const(bd), const(bias), q_spec],
            out_specs=[pl.BlockSpec((1, NSA_HEADS * T, LANES), lambda b, s, pt: (b, 0, 0)),
                       pl.BlockSpec((1, NSA_GROUPS * T, sel_w), lambda b, s, pt: (b, 0, 0))],
            scratch_shapes=[pltpu.VMEM((2, n_rows, LANES), F32), pltpu.VMEM((2, 1, LANES), F32)]),
        out_shape=[jax.ShapeDtypeStruct((DB, NSA_HEADS * T, LANES), F32),
                   jax.ShapeDtypeStruct((DB, NSA_GROUPS * T, sel_w), F32)],
        compiler_params=_params(("parallel", "arbitrary")),
        name="nsa_dec_cmp",
    )(pt, *([cache_t] * PAGES_PER_STEP), perm, bd, bias, zb)

    sel_steps = sel[:, :, :sel_used].reshape(DB, NSA_GROUPS * T, n_steps + 1, blocks_per_step).transpose(0, 2, 1, 3)
    sel_steps = jnp.pad(sel_steps, ((0, 0), (0, 0), (0, 0), (0, LANES - blocks_per_step)))
    kk = np.arange(PAGES_PER_STEP * page_rows) // SEL_LEN
    expand = jnp.asarray(kk[None, :] == np.arange(LANES)[:, None], BF16)
    pad_rows = LANES - T
    new_rows = lambda slot: jnp.pad(zcb[:, :, slot * NSA_KV:(slot + 1) * NSA_KV], ((0, 0), (0, pad_rows), (0, 0)))
    new_spec = pl.BlockSpec((1, LANES, LANES), lambda b, s, pt: (b, 0, 0))
    sel_blk = (1, 1, NSA_GROUPS * T, LANES)

    o_s = pl.pallas_call(
        functools.partial(_nsa_dec_sel_kernel, T=T, past_len=past_len),
        grid_spec=pltpu.PrefetchScalarGridSpec(
            num_scalar_prefetch=1, grid=(DB, n_steps),
            in_specs=_page_specs(page_blk, 1, layer, n_pages)
            + [q_spec, pl.BlockSpec(sel_blk, lambda b, s, pt: (b, s, 0, 0)),
               pl.BlockSpec(sel_blk, lambda b, s, pt: (b, n_steps, 0, 0)), const(expand), new_spec, new_spec],
            out_specs=pl.BlockSpec((1, NSA_HEADS * T, LANES), lambda b, s, pt: (b, 0, 0)),
            scratch_shapes=[pltpu.VMEM((NSA_HEADS * T, 1), F32), pltpu.VMEM((NSA_HEADS * T, 1), F32),
                            pltpu.VMEM((NSA_HEADS * T, LANES), F32)]),
        out_shape=jax.ShapeDtypeStruct((DB, NSA_HEADS * T, LANES), F32),
        compiler_params=_params(("parallel", "arbitrary")),
        name="nsa_dec_sel",
    )(pt, *([cache_t] * PAGES_PER_STEP), zb, sel_steps, sel_steps, expand, new_rows(2), new_rows(3))

    w_src = cache_win_kv.shape[1]
    win_t = jnp.transpose(cache_win_kv, (0, 2, 3, 4, 5, 1)).reshape(DB, DEPTH, 2, NSA_KV, w_src)
    b3 = lambda shp: pl.BlockSpec(shp, lambda b: (b, 0, 0))
    return pl.pallas_call(
        functools.partial(_nsa_dec_win_kernel, T=T, past_len=past_len),
        grid=(DB,),
        in_specs=[b3((1, T, NSA_QPAD)), b3((1, T, GATE_COLS)), b3((1, NSA_HEADS * T, LANES)), b3((1, NSA_HEADS * T, LANES)),
                  pl.BlockSpec((1, 1, 2, NSA_KV, w_src), lambda b: (b, layer, 0, 0, 0)),
                  b3((1, LANES, LANES)), b3((1, LANES, LANES))],
        out_specs=b3((1, T, NSA_QPAD)),
        out_shape=jax.ShapeDtypeStruct((DB, T, NSA_QPAD), BF16),
        compiler_params=_params(("parallel",)),
        name="nsa_dec_win",
    )(zb, zd, o_c, o_s, win_t, new_rows(4), new_rows(5))


def _route_top2(logits):
    lane = lax.broadcasted_iota(jnp.int32, logits.shape, 1)
    W = logits.shape[1]
    logits = jnp.where(lane < N_EXPERTS, logits, NEG_INF)
    m1 = jnp.max(logits, axis=-1, keepdims=True)
    i1 = jnp.min(jnp.where(logits == m1, lane, W), axis=-1, keepdims=True)
    rest = jnp.where(lane == i1, NEG_INF, logits)
    m2 = jnp.max(rest, axis=-1, keepdims=True)
    i2 = jnp.min(jnp.where(rest == m2, lane, W), axis=-1, keepdims=True)
    e2 = jnp.exp(m2 - m1)
    den = 1.0 + e2
    return jnp.where(lane == i1, 1.0 / den, 0.0) + jnp.where(lane == i2, e2 / den, 0.0)


def _router_kernel(x_ref, w_ref, b_ref, o_ref):
    x = x_ref[...]
    w = w_ref[...]
    x_hi = x.astype(BF16)
    x_lo = (x - x_hi.astype(F32)).astype(BF16)
    w_hi = w.astype(BF16)
    w_lo = (w - w_hi.astype(F32)).astype(BF16)
    logits = _dot(x_hi, w_hi) + _dot(x_hi, w_lo) + _dot(x_lo, w_hi)
    o_ref[...] = _route_top2(logits + b_ref[...])


def router(x2d, w_router, b_router, tm):
    n = x2d.shape[0]
    w = jnp.pad(w_router.astype(F32), ((0, 0), (0, LANES - N_EXPERTS)))
    bb = jnp.pad(b_router.astype(F32), (0, LANES - N_EXPERTS)).reshape(1, LANES)
    return pl.pallas_call(
        _router_kernel,
        grid=(n // tm,),
        in_specs=[pl.BlockSpec((tm, D_MODEL), lambda i: (i, 0)), pl.BlockSpec(w.shape, lambda i: (0, 0)),
                  pl.BlockSpec(bb.shape, lambda i: (0, 0))],
        out_specs=pl.BlockSpec((tm, LANES), lambda i: (i, 0)),
        out_shape=jax.ShapeDtypeStruct((n, LANES), F32),
        compiler_params=_params(("parallel",)),
        name="router",
    )(x2d, w, bb)


def _merge_kernel(x_ref, hm_ref, hn_ref, hx_ref, wg_ref, bg_ref, wbm_ref, wbn_ref, wbx_ref, wo_ref, g_ref, b_ref,
                  o_ref):
    x = x_ref[...]
    xb = x.astype(BF16)
    merged = None
    for c, (h_ref, w_ref) in enumerate(((hm_ref, wbm_ref), (hn_ref, wbn_ref), (hx_ref, wbx_ref))):
        sl = slice(c * D_MODEL, (c + 1) * D_MODEL)
        gate = jax.nn.sigmoid(_dot(xb, wg_ref[:, sl]) + bg_ref[:, sl])
        term = gate * _dot(h_ref[...], w_ref[...])
        merged = term if merged is None else merged + term
    mix = _dot(merged.astype(BF16), wo_ref[...])
    o_ref[...] = _layer_norm(ALPHA * x + mix, g_ref[...], b_ref[...])


def merge(x2d, hm, hn, hx, wg, bg, wbr, wo, g, b, tm):
    n = x2d.shape[0]
    row = lambda w: pl.BlockSpec((tm, w), lambda i: (i, 0))
    full = lambda a: pl.BlockSpec(a.shape, lambda i: (0,) * a.ndim, pipeline_mode=pl.Buffered(1))
    args = [x2d, hm, hn, hx, wg, bg, wbr[0], wbr[1], wbr[2], wo, g, b]
    return pl.pallas_call(
        _merge_kernel,
        grid=(n // tm,),
        in_specs=[row(a.shape[1]) for a in args[:4]] + [full(a) for a in args[4:]],
        out_specs=row(D_MODEL),
        out_shape=jax.ShapeDtypeStruct((n, D_MODEL), F32),
        compiler_params=_params(("parallel",)),
        name="merge",
    )(*args)


def _ffn_kernel(x_ref, wg_ref, wu_ref, wd_ref, g_ref, b_ref, o_ref, acc_s):
    f = pl.program_id(1)

    @pl.when(f == 0)
    def _():
        acc_s[...] = jnp.zeros(acc_s.shape, F32)

    xb = x_ref[...].astype(BF16)
    hcur = jax.nn.silu(_dot(xb, wg_ref[...])) * _dot(xb, wu_ref[...])
    acc_s[...] += _dot(hcur.astype(BF16), wd_ref[...])

    @pl.when(f == pl.num_programs(1) - 1)
    def _():
        o_ref[...] = _layer_norm(ALPHA * x_ref[...] + acc_s[...], g_ref[...], b_ref[...])


def ffn(x2d, w_up, w_down, g, b, tm, tf):
    n = x2d.shape[0]
    nf = (w_up.shape[1] // 2) // tf
    wmode = dict(pipeline_mode=pl.Buffered(1)) if nf == 1 else {}
    return pl.pallas_call(
        _ffn_kernel,
        grid=(n // tm, nf),
        in_specs=[pl.BlockSpec((tm, D_MODEL), lambda i, f: (i, 0)),
                  pl.BlockSpec((D_MODEL, tf), lambda i, f: (0, f), **wmode),
                  pl.BlockSpec((D_MODEL, tf), lambda i, f: (0, nf + f), **wmode),
                  pl.BlockSpec((tf, D_MODEL), lambda i, f: (f, 0), **wmode),
                  pl.BlockSpec((1, D_MODEL), lambda i, f: (0, 0)),
                  pl.BlockSpec((1, D_MODEL), lambda i, f: (0, 0))],
        out_specs=pl.BlockSpec((tm, D_MODEL), lambda i, f: (i, 0)),
        out_shape=jax.ShapeDtypeStruct((n, D_MODEL), F32),
        scratch_shapes=[pltpu.VMEM((tm, D_MODEL), F32)],
        compiler_params=_params(("parallel", "arbitrary")),
        name="dense_ffn",
    )(x2d, w_up, w_up, w_down, g, b)


def _moe_kernel(np_ref, x_ref, cw_ref, cwt_ref, wg_ref, wu_ref, wd_ref, g_ref, b_ref, o_ref,
                xb_s, y_s, pos_s, post_s, xg_s, ws_s, acc_s, *, blk, cap, cap_pad):
    j = pl.program_id(0)
    e = pl.program_id(1)
    f = pl.program_id(2)
    n_e = pl.num_programs(1)
    n_f = pl.num_programs(2)
    n_pass = np_ref[j * n_e + e]
    RC = min(256, blk)

    @pl.when((e == 0) & (f == 0))
    def _():
        xb_s[...] = x_ref[...].astype(BF16)
        y_s[...] = jnp.zeros(y_s.shape, F32)
        cw = cw_ref[...]
        cwt = cwt_ref[...]
        routed = jnp.where(cw != 0.0, 1.0, 0.0).astype(BF16)
        routed_t = jnp.where(cwt != 0.0, 1.0, 0.0).astype(BF16)
        for rc in range(blk // RC):
            rows_i = rc * RC + lax.broadcasted_iota(jnp.int32, (RC, blk), 0)
            cols_i = lax.broadcasted_iota(jnp.int32, (RC, blk), 1)
            before = jnp.where(cols_i < rows_i, 1.0, 0.0).astype(BF16)
            cnt = _dot(before, routed)
            pos_s[rc * RC:(rc + 1) * RC, :] = jnp.where(cw[rc * RC:(rc + 1) * RC] != 0.0, cnt, -1.0)
            rows_j = lax.broadcasted_iota(jnp.int32, (blk, RC), 0)
            cols_j = rc * RC + lax.broadcasted_iota(jnp.int32, (blk, RC), 1)
            before_t = jnp.where(rows_j < cols_j, 1.0, 0.0).astype(BF16)
            cnt_t = _dot(routed_t, before_t)
            post_s[:, rc * RC:(rc + 1) * RC] = jnp.where(cwt[:, rc * RC:(rc + 1) * RC] != 0.0, cnt_t, -1.0)

    @pl.when(f == 0)
    def _():
        prow = post_s[pl.ds(e, 1), :]
        wrow = cwt_ref[pl.ds(e, 1), :]

        def gather(u, carry):
            slot = (u * cap + lax.broadcasted_iota(jnp.int32, (cap, 1), 0)).astype(F32)
            hit = prow == slot
            xg_s[u, 0:cap, :] = _dot(jnp.where(hit, 1.0, 0.0).astype(BF16), xb_s[...]).astype(BF16)
            ws_s[u, 0:cap, :] = jnp.sum(jnp.where(hit, wrow, 0.0), axis=-1, keepdims=True)
            acc_s[u] = jnp.zeros(acc_s.shape[1:], F32)
            return carry

        lax.fori_loop(0, n_pass, gather, 0)

    def expert(u, carry):
        xg = xg_s[u, 0:cap, :]
        hcur = jax.nn.silu(_dot(xg, wg_ref[0])) * _dot(xg, wu_ref[0]) * ws_s[u, 0:cap, :]
        acc_s[u, 0:cap, :] += _dot(hcur.astype(BF16), wd_ref[0])
        return carry

    lax.fori_loop(0, n_pass, expert, 0)

    @pl.when(f == n_f - 1)
    def _():
        lane = lax.broadcasted_iota(jnp.int32, pos_s.shape, 1)
        pcol = jnp.sum(jnp.where(lane == e, pos_s[...], 0.0), axis=-1, keepdims=True)

        def scatter(u, carry):
            cc = lax.broadcasted_iota(jnp.int32, (1, cap_pad), 1)
            slot = jnp.where(cc < cap, u * cap + cc, NO_SLOT).astype(F32)
            hit = jnp.where(pcol == slot, 1.0, 0.0).astype(BF16)
            y_s[...] += _dot(hit, acc_s[u].astype(BF16))
            return carry

        lax.fori_loop(0, n_pass, scatter, 0)

    @pl.when((e == n_e - 1) & (f == n_f - 1))
    def _():
        o_ref[...] = _layer_norm(ALPHA * x_ref[...] + y_s[...], g_ref[...], b_ref[...])


def moe(x2d, cw, w_up, w_down, g, b, tf):
    n = x2d.shape[0]
    E, _, F2 = w_up.shape
    nf = (F2 // 2) // tf
    blk = _row_tile(n, MOE_BLOCK)
    cap = min(MOE_CAP, blk)
    cap_pad = -(-cap // LANES) * LANES
    max_pass = -(-blk // cap)
    nblk = n // blk
    cwt = jnp.swapaxes(cw[:, :ROUTE_ROWS], 0, 1)
    counts = jnp.sum((cw[:, :E] != 0.0).reshape(nblk, blk, E), axis=1)
    n_pass = ((counts + cap - 1) // cap).astype(jnp.int32).reshape(-1)
    return pl.pallas_call(
        functools.partial(_moe_kernel, blk=blk, cap=cap, cap_pad=cap_pad),
        grid_spec=pltpu.PrefetchScalarGridSpec(
            num_scalar_prefetch=1, grid=(nblk, E, nf),
            in_specs=[pl.BlockSpec((blk, D_MODEL), lambda j, e, f, npr: (j, 0), pipeline_mode=pl.Buffered(1)),
                      pl.BlockSpec((blk, LANES), lambda j, e, f, npr: (j, 0)),
                      pl.BlockSpec((ROUTE_ROWS, blk), lambda j, e, f, npr: (0, j)),
                      pl.BlockSpec((1, D_MODEL, tf), lambda j, e, f, npr: (e, 0, f)),
                      pl.BlockSpec((1, D_MODEL, tf), lambda j, e, f, npr: (e, 0, nf + f)),
                      pl.BlockSpec((1, tf, D_MODEL), lambda j, e, f, npr: (e, f, 0)),
                      pl.BlockSpec((1, D_MODEL), lambda j, e, f, npr: (0, 0)),
                      pl.BlockSpec((1, D_MODEL), lambda j, e, f, npr: (0, 0))],
            out_specs=pl.BlockSpec((blk, D_MODEL), lambda j, e, f, npr: (j, 0)),
            scratch_shapes=[pltpu.VMEM((blk, D_MODEL), BF16), pltpu.VMEM((blk, D_MODEL), F32),
                            pltpu.VMEM((blk, LANES), F32), pltpu.VMEM((ROUTE_ROWS, blk), F32),
                            pltpu.VMEM((max_pass, cap_pad, D_MODEL), BF16), pltpu.VMEM((max_pass, cap_pad, 1), F32),
                            pltpu.VMEM((max_pass, cap_pad, D_MODEL), F32)]),
        out_shape=jax.ShapeDtypeStruct((n, D_MODEL), F32),
        compiler_params=_params(("parallel", "arbitrary", "arbitrary")),
        name="moe_ffn",
    )(n_pass, x2d, cw, cwt, w_up, w_up, w_down, g, b)


def _layer(x, lw, l, *, mem_kv, mem_layer, mstate, decode):
    B, T, _ = x.shape
    n = B * T
    x2d = x.reshape(n, D_MODEL)
    tm = _row_tile(n, 1024)
    za, zb, zc, zcb, zd = in_proj(x2d, lw["in"], tm)
    za, zb, zc, zcb, zd = (a.reshape(B, T, -1) for a in (za, zb, zc, zcb, zd))

    if T % MLSTM_CHUNK == 0:
        h_m, ct, nt, mt = mlstm(za, zd, *mstate, lw["norm_g"], L=MLSTM_CHUNK, t_valid=MLSTM_CHUNK)
    else:
        padt = lambda a: jnp.pad(a, ((0, 0), (0, LANES - T), (0, 0)))
        h_m, ct, nt, mt = mlstm(padt(za), padt(zd), *mstate, lw["norm_g"], L=LANES, t_valid=T)
        h_m = h_m[:, :T]

    if decode is None:
        kc, vc = cmp_tokens(zc, lw["cmp_w"], lw["cmp_b"])
        h_n = nsa_prompt(zb, zd, zcb, kc, vc, tq=min(256, T), CH=min(512, T))
    else:
        h_n = nsa_decode(zb, zd, zcb, decode["nsa"], decode["win"], decode["pt"], lw["cmp_w"], lw["cmp_b"], l)

    h_x = mem_attn(zb, mem_kv, mem_layer, tq=_row_tile(T, 512))

    wg, bg = lw["in"]["mg"]
    flat = lambda a: a.reshape(n, -1)
    x1 = merge(x2d, flat(h_m), flat(h_n), flat(h_x), wg, bg, lw["w_branch"], lw["w_out"], lw["ln1_g"], lw["ln1_b"], tm)

    if l % 2 == 0:
        x2 = ffn(x1, lw["ffn_up"], lw["ffn_down"], lw["ln2_g"], lw["ln2_b"], _row_tile(n, 512), D_FF)
    else:
        cw = router(x1, lw["w_router"], lw["b_router"], _row_tile(n, 1024))
        x2 = moe(x1, cw, lw["ffn_up"], lw["ffn_down"], lw["ln2_g"], lw["ln2_b"], D_FF_EXPERT // 2)
    return x2.reshape(B, T, D_MODEL), zc, (ct, nt, mt)


def kernel(x_prompt, x_sample, mem_prompt, cache_nsa_kv, cache_win_kv, state_mlstm_C, state_mlstm_n,
           state_mlstm_m, cache_mem_kv, page_table, w_in, b_in, mlstm_norm_g, cmp_w, cmp_b, w_mem_kv,
           w_branch, w_out, ln1_g, ln1_b, ln2_g, ln2_b, ffn_w_up, ffn_w_down, moe_w_router, moe_b_router,
           moe_w_up, moe_w_down):
    B, T, _ = x_prompt.shape
    DB, TS, _ = x_sample.shape
    xp, xs = x_prompt, x_sample
    nsa_p, nsa_s, win_p, win_s = [], [], [], []
    Cp, np_, mp, Cs, ns, ms, memkv_p = [], [], [], [], [], [], []
    row1 = lambda a: a.reshape(1, -1).astype(F32)
    for l in range(DEPTH):
        wbr = (w_branch[l, 0].astype(BF16), _pad_branch_rows(w_branch[l, 1]).astype(BF16), w_branch[l, 2].astype(BF16))
        lw = dict(norm_g=mlstm_norm_g[l], cmp_w=cmp_w[l], cmp_b=cmp_b[l], w_branch=wbr,
                  w_out=w_out[l].astype(BF16), ln1_g=row1(ln1_g[l]), ln1_b=row1(ln1_b[l]),
                  ln2_g=row1(ln2_g[l]), ln2_b=row1(ln2_b[l]))
        lw["in"] = _split_in_proj(w_in[l], b_in[l])
        if l % 2 == 0:
            lw["ffn_up"] = ffn_w_up[l // 2].astype(BF16)
            lw["ffn_down"] = ffn_w_down[l // 2].astype(BF16)
        else:
            lw["ffn_up"] = moe_w_up[l // 2].astype(BF16)
            lw["ffn_down"] = moe_w_down[l // 2].astype(BF16)
            lw["w_router"] = moe_w_router[l // 2]
            lw["b_router"] = moe_b_router[l // 2]

        mkv = proj(mem_prompt.reshape(B * MEM_LEN, D_MODEL), w_mem_kv[l].astype(BF16), _row_tile(B * MEM_LEN, 512))
        mkv = mkv.reshape(B, MEM_LEN, 2 * X_WIDTH)
        st0 = (jnp.zeros((B, M_HEADS, M_DH, M_DH), F32), jnp.zeros((B, M_HEADS, M_DH), F32), jnp.zeros((B, M_HEADS), F32))
        xp, zc, st = _layer(xp, lw, l, mem_kv=mkv.reshape(B, MEM_LEN, 1, 2, X_HEADS, X_DH), mem_layer=0, mstate=st0,
                            decode=None)
        kvn = zc.reshape(B, T, 6, NSA_GROUPS, NSA_DH)
        nsa_p.append(kvn[:, :, :4])
        win_p.append(kvn[:, T - min(WINDOW, T):, 4:])
        Cp.append(st[0]); np_.append(st[1]); mp.append(st[2])
        memkv_p.append(mkv.reshape(B, MEM_LEN, 2, X_HEADS, X_DH))

        sts = (state_mlstm_C[:, l], state_mlstm_n[:, l], state_mlstm_m[:, l])
        dec = dict(nsa=cache_nsa_kv, win=cache_win_kv, pt=page_table)
        xs, zc, st = _layer(xs, lw, l, mem_kv=cache_mem_kv, mem_layer=l, mstate=sts, decode=dec)
        kvn = zc.reshape(DB, TS, 6, NSA_GROUPS, NSA_DH)
        nsa_s.append(kvn[:, :, :4])
        win_s.append(jnp.concatenate([cache_win_kv[:, :, l].astype(F32), kvn[:, :, 4:]], axis=1)[:, TS:])
        Cs.append(st[0]); ns.append(st[1]); ms.append(st[2])
    return (xp, xs,
            jnp.stack(nsa_p, axis=2), jnp.stack(nsa_s, axis=2),
            jnp.stack(win_p, axis=2), jnp.stack(win_s, axis=2),
            jnp.stack(Cp, axis=1), jnp.stack(np_, axis=1), jnp.stack(mp, axis=1),
            jnp.stack(Cs, axis=1), jnp.stack(ns, axis=1), jnp.stack(ms, axis=1),
            jnp.stack(memkv_p, axis=2))
```

```python
import functools

import numpy as np
import jax
import jax.numpy as jnp
from jax import lax
from jax.experimental import pallas as pl
from jax.experimental.pallas import tpu as pltpu

D_MODEL = 1024
DEPTH = 2
BRANCH_WIDTH = 512
N_BRANCH = 3
M_HEADS = 4
M_DH = BRANCH_WIDTH // M_HEADS
M_WIDTH = M_HEADS * M_DH
NSA_HEADS = 8
NSA_DH = BRANCH_WIDTH // NSA_HEADS
NSA_WIDTH = NSA_HEADS * NSA_DH
NSA_GROUPS = 2
NSA_HPG = NSA_HEADS // NSA_GROUPS
NSA_KV = NSA_GROUPS * NSA_DH
CMP_LEN = 32
CMP_STRIDE = 16
SEL_LEN = 64
SEL_TOPN = 16
WINDOW = 512
MEM_LEN = 256
X_HEADS = 4
X_DH = BRANCH_WIDTH // X_HEADS
X_WIDTH = X_HEADS * X_DH
D_FF = 2816
N_EXPERTS = 8
TOP_K = 2
D_FF_EXPERT = 3584
ALPHA = (2.0 * DEPTH) ** 0.25
LN_EPS = 1e-5
IN_SPLITS = (M_WIDTH, M_WIDTH, M_WIDTH, M_HEADS, M_HEADS, M_WIDTH,
             NSA_WIDTH, 6 * NSA_KV, 3 * NSA_HEADS, X_WIDTH, N_BRANCH * D_MODEL)
LOG2E = 1.4426950408889634
NSA_SLOPES = tuple(LOG2E * 2.0 ** (-8.0 * (h + 1) / NSA_HEADS) for h in range(NSA_HEADS))
MASK_BIAS = -(2.0 ** 30)
SEL_FEAT = 64
SEL_QUARTERS = 4
SLOPE_PARTS = 3

LANES = 128
SUBLANES = 8
VMEM_LIMIT = 56 * 1024 * 1024
PAGES_PER_STEP = 64
MLSTM_SEQS = 1
MLSTM_CHUNK = 256
GATE_COLS = LANES
NG_OFF = 2 * M_HEADS
NSA_QPAD = NSA_HEADS * LANES
MOE_BLOCK = 1024
MOE_CAP = 288
NO_SLOT = -2
ROUTE_ROWS = 16

F32 = jnp.float32
BF16 = jnp.bfloat16
NEG_INF = float("-inf")
HIGHEST = lax.Precision.HIGHEST


def _dot(a, b, precision=None):
    return jnp.dot(a, b, preferred_element_type=F32, precision=precision)


def _dot_nt(a, b, precision=None):
    return lax.dot_general(a, b, (((1,), (1,)), ((), ())), preferred_element_type=F32, precision=precision)


def _dot_tn(a, b):
    return lax.dot_general(a, b, (((0,), (0,)), ((), ())), preferred_element_type=F32)


def _params(sem):
    return pltpu.CompilerParams(dimension_semantics=sem, vmem_limit_bytes=VMEM_LIMIT)


def _masked_softmax2(s, mask):
    s = jnp.where(mask, s, NEG_INF)
    mx = jnp.max(s, axis=-1, keepdims=True)
    mx = jnp.where(mx > NEG_INF, mx, 0.0)
    p = jnp.where(mask, jnp.exp2(s - mx), 0.0)
    return p / jnp.maximum(jnp.sum(p, axis=-1, keepdims=True), 1e-30)


def _layer_norm(xf, g, b):
    mu = jnp.mean(xf, axis=-1, keepdims=True)
    var = jnp.mean(jnp.square(xf - mu), axis=-1, keepdims=True)
    return (xf - mu) * lax.rsqrt(var + LN_EPS) * g + b


def _row_tile(n, pref):
    return pref if n % pref == 0 else n


def _in_proj_kernel(x_ref, wa_ref, wb_ref, wc_ref, wd_ref, ba_ref, bb_ref, bc_ref, bd_ref,
                    oa_ref, ob_ref, oc_ref, ocb_ref, od_ref):
    x = x_ref[...].astype(BF16)
    oa_ref[...] = (_dot(x, wa_ref[...]) + ba_ref[...]).astype(BF16)
    ob_ref[...] = (_dot(x, wb_ref[...]) + bb_ref[...]).astype(BF16)
    c = _dot(x, wc_ref[...]) + bc_ref[...]
    oc_ref[...] = c
    ocb_ref[...] = c.astype(BF16)
    od_ref[...] = _dot(x, wd_ref[...]) + bd_ref[...]


def _group_select():
    return jax.nn.one_hot(np.arange(NSA_HEADS) // NSA_HPG, NSA_GROUPS, dtype=F32)


def _pad_branch_rows(w_nsa):
    return jnp.einsum("hem,hg->hgem", w_nsa.reshape(NSA_HEADS, NSA_DH, -1), _group_select()).reshape(NSA_QPAD, -1)


def _split_in_proj(w_in_l, b_in_l):
    offs = np.cumsum((0,) + IN_SPLITS)
    w = [w_in_l[:, offs[i]:offs[i + 1]] for i in range(len(IN_SPLITS))]
    b = [b_in_l[offs[i]:offs[i + 1]] for i in range(len(IN_SPLITS))]
    mq, mk, mv, mi, mf, mo, nq, nkv, ng, xq, mg = range(11)
    pad = GATE_COLS - 2 * M_HEADS - 3 * NSA_HEADS

    def cat(ids, zpad=0):
        ww = jnp.concatenate([w[i] for i in ids], axis=1)
        bb = jnp.concatenate([b[i] for i in ids])
        if zpad:
            ww = jnp.pad(ww, ((0, 0), (0, zpad)))
            bb = jnp.pad(bb, (0, zpad))
        return ww.astype(BF16), bb.reshape(1, -1).astype(F32)

    scale = NSA_DH ** -0.5 * LOG2E
    w[nq] = jnp.einsum("dhe,hg->dhge", w[nq].reshape(D_MODEL, NSA_HEADS, NSA_DH) * scale, _group_select()).reshape(D_MODEL, NSA_QPAD)
    b[nq] = jnp.einsum("he,hg->hge", b[nq].reshape(NSA_HEADS, NSA_DH) * scale, _group_select()).reshape(NSA_QPAD)
    return dict(a=cat([mq, mk, mv, mo]), b=cat([nq, xq]), c=cat([nkv]), d=cat([mi, mf, ng], pad), mg=cat([mg]))


def in_proj(x2d, wp, tm):
    n = x2d.shape[0]
    (wa, ba), (wb, bb), (wc, bc), (wd, bd) = wp["a"], wp["b"], wp["c"], wp["d"]
    full = lambda arr: pl.BlockSpec(arr.shape, lambda i: (0, 0), pipeline_mode=pl.Buffered(1))
    row = lambda w: pl.BlockSpec((tm, w), lambda i: (i, 0))
    return pl.pallas_call(
        _in_proj_kernel,
        grid=(n // tm,),
        in_specs=[row(D_MODEL), full(wa), full(wb), full(wc), full(wd), full(ba), full(bb), full(bc), full(bd)],
        out_specs=[row(wa.shape[1]), row(wb.shape[1]), row(wc.shape[1]), row(wc.shape[1]), row(wd.shape[1])],
        out_shape=[jax.ShapeDtypeStruct((n, wa.shape[1]), BF16), jax.ShapeDtypeStruct((n, wb.shape[1]), BF16),
                   jax.ShapeDtypeStruct((n, wc.shape[1]), F32), jax.ShapeDtypeStruct((n, wc.shape[1]), BF16),
                   jax.ShapeDtypeStruct((n, wd.shape[1]), F32)],
        compiler_params=_params(("parallel",)),
        name="in_proj",
    )(x2d, wa, wb, wc, wd, ba, bb, bc, bd)


def _proj_kernel(x_ref, w_ref, o_ref):
    o_ref[...] = _dot(x_ref[...].astype(BF16), w_ref[...])


def proj(x2d, w_bf16, tm):
    n, k = x2d.shape
    m = w_bf16.shape[1]
    return pl.pallas_call(
        _proj_kernel,
        grid=(n // tm,),
        in_specs=[pl.BlockSpec((tm, k), lambda i: (i, 0)), pl.BlockSpec((k, m), lambda i: (0, 0))],
        out_specs=pl.BlockSpec((tm, m), lambda i: (i, 0)),
        out_shape=jax.ShapeDtypeStruct((n, m), F32),
        compiler_params=_params(("parallel",)),
        name="mem_proj",
    )(x2d, w_bf16)


def _log_sigmoid(x):
    return jnp.minimum(x, 0.0) - jnp.log1p(jnp.exp(-jnp.abs(x)))


def _mlstm_kernel(q_ref, k_ref, v_ref, og_ref, gc_ref, gr_ref, c0_ref, n0_ref, m0_ref, ng_ref,
                  h_ref, ct_ref, nt_ref, mt_ref, c_s, n_s, m_s, *, L, t_valid, nb):
    ci = pl.program_id(1)

    @pl.when(ci == 0)
    def _():
        c_s[...] = c0_ref[...]
        n_s[...] = n0_ref[...]
        m_s[...] = m0_ref[...]

    row = lax.broadcasted_iota(jnp.int32, (L, L), 0)
    col = lax.broadcasted_iota(jnp.int32, (L, L), 1)
    causal = row >= col
    tri = causal.astype(F32)
    tri_t = (row <= col).astype(F32)
    rvalid = lax.broadcasted_iota(jnp.int32, (L, GATE_COLS), 0) < t_valid
    cvalid = lax.broadcasted_iota(jnp.int32, (2 * M_HEADS, L), 1) < t_valid
    scale = M_DH ** -0.5

    for bb, h in [(bb, h) for bb in range(nb) for h in range(M_HEADS)]:
        if h == 0:
            gc = gc_ref[bb]
            gr = gr_ref[bb]
            lf_c = jnp.where(rvalid, _log_sigmoid(gc), 0.0)
            lf_r = jnp.where(cvalid, _log_sigmoid(gr), 0.0)
            ig_c = jnp.where(rvalid, gc, NEG_INF)
            ig_r = jnp.where(cvalid, gr, NEG_INF)
            b_c = _dot(tri, lf_c, HIGHEST)
            b_r = _dot(lf_r, tri_t, HIGHEST)
        sl = slice(h * M_DH, (h + 1) * M_DH)
        q = q_ref[bb, :, sl]
        k = k_ref[bb, :, sl]
        v = v_ref[bb, :, sl]
        bc = b_c[:, M_HEADS + h:M_HEADS + h + 1]
        br = b_r[M_HEADS + h:M_HEADS + h + 1, :]
        igc = ig_c[:, h:h + 1]
        igr = ig_r[h:h + 1, :]
        m_prev = m_s[bb, h, 0:1, 0:1]
        c_prev = c_s[bb, h]
        n_prev = n_s[bb, h:h + 1, :]

        dmat = jnp.where(causal, bc - br + igr, NEG_INF)
        inter = bc + m_prev
        m_t = jnp.maximum(inter, jnp.max(dmat, axis=-1, keepdims=True))
        s = _dot_nt(q, k) * scale * jnp.exp(dmat - m_t)
        sc_in = jnp.exp(inter - m_t)
        qf = q.astype(F32)
        num = sc_in * _dot_nt(q, c_prev.astype(BF16)) + _dot(s.astype(BF16), v)
        den = sc_in * jnp.sum(qf * n_prev, axis=-1, keepdims=True) + jnp.sum(s, axis=-1, keepdims=True)
        hh = num / jnp.maximum(jnp.abs(den), jnp.exp(-m_t))

        b_last = bc[L - 1:L, :]
        dec_c = b_last - bc + igc
        dec_r = b_last - br + igr
        m_new = jnp.maximum(b_last + m_prev, jnp.max(dec_r, axis=-1, keepdims=True))
        ws_c = jnp.exp(dec_c - m_new) * scale
        sc = jnp.exp(b_last + m_prev - m_new)
        vf = v.astype(F32)
        kf = k.astype(F32)
        c_new = sc * c_prev + _dot_tn((vf * ws_c).astype(BF16), k)
        n_new = sc * n_prev + jnp.sum(kf * ws_c, axis=0, keepdims=True)
        c_s[bb, h] = c_new
        n_s[bb, h:h + 1, :] = n_new
        m_s[bb, h] = jnp.broadcast_to(m_new, m_s.shape[2:])

        og = og_ref[bb, :, sl].astype(F32)
        hh = hh * jax.nn.sigmoid(og)
        mu = jnp.mean(hh, axis=-1, keepdims=True)
        var = jnp.mean(jnp.square(hh - mu), axis=-1, keepdims=True)
        hn = (hh - mu) * lax.rsqrt(var + LN_EPS) * ng_ref[:, sl]
        h_ref[bb, :, sl] = hn.astype(h_ref.dtype)

    ct_ref[...] = c_s[...]
    nt_ref[...] = n_s[...]
    mt_ref[...] = m_s[...]


def mlstm(za, zd, c0, n0, m0, norm_g, *, L, t_valid):
    B, T, _ = za.shape
    nc = T // L
    gr = jnp.swapaxes(zd[:, :, :2 * M_HEADS], 1, 2)
    m0b = jnp.broadcast_to(m0.astype(F32)[:, :, None, None], (B, M_HEADS, SUBLANES, LANES))
    nb = MLSTM_SEQS if B % MLSTM_SEQS == 0 else 1
    colspec = lambda j: pl.BlockSpec((nb, L, M_WIDTH), lambda b, c, j=j: (b, c, j))
    st = lambda shp: pl.BlockSpec((nb,) + shp, lambda b, c: (b,) + (0,) * len(shp))
    kern = functools.partial(_mlstm_kernel, L=L, t_valid=t_valid, nb=nb)
    h, ct, nt, mt = pl.pallas_call(
        kern,
        grid=(B // nb, nc),
        in_specs=[colspec(0), colspec(1), colspec(2), colspec(3),
                  pl.BlockSpec((nb, L, GATE_COLS), lambda b, c: (b, c, 0)),
                  pl.BlockSpec((nb, 2 * M_HEADS, L), lambda b, c: (b, 0, c)),
                  st((M_HEADS, M_DH, M_DH)), st((M_HEADS, M_DH)), st((M_HEADS, SUBLANES, LANES)),
                  pl.BlockSpec((1, M_WIDTH), lambda b, c: (0, 0))],
        out_specs=[pl.BlockSpec((nb, L, M_WIDTH), lambda b, c: (b, c, 0)),
                   st((M_HEADS, M_DH, M_DH)), st((M_HEADS, M_DH)), st((M_HEADS, SUBLANES, LANES))],
        out_shape=[jax.ShapeDtypeStruct((B, T, M_WIDTH), BF16),
                   jax.ShapeDtypeStruct((B, M_HEADS, M_DH, M_DH), F32),
                   jax.ShapeDtypeStruct((B, M_HEADS, M_DH), F32),
                   jax.ShapeDtypeStruct((B, M_HEADS, SUBLANES, LANES), F32)],
        scratch_shapes=[pltpu.VMEM((nb, M_HEADS, M_DH, M_DH), F32), pltpu.VMEM((nb, M_HEADS, M_DH), F32),
                        pltpu.VMEM((nb, M_HEADS, SUBLANES, LANES), F32)],
        compiler_params=_params(("parallel", "arbitrary")),
        name="mlstm",
    )(za, za, za, za, zd, gr, c0.astype(F32), n0.astype(F32), m0b, norm_g.reshape(1, M_WIDTH).astype(F32))
    return h, ct, nt, mt[:, :, 0, 0]


def _mem_attn_kernel(q_ref, k_ref, v_ref, o_ref):
    scale = X_DH ** -0.5
    for h in range(X_HEADS):
        sl = slice(h * X_DH, (h + 1) * X_DH)
        k = k_ref[:, sl] if len(k_ref.shape) == 2 else k_ref[:, h, :]
        v = v_ref[:, sl] if len(v_ref.shape) == 2 else v_ref[:, h, :]
        s = _dot_nt(q_ref[0, :, sl], k.astype(BF16)) * scale
        mx = jnp.max(s, axis=-1, keepdims=True)
        p = jnp.exp(s - mx)
        p = p / jnp.sum(p, axis=-1, keepdims=True)
        o_ref[0, :, sl] = _dot(p.astype(BF16), v.astype(BF16)).astype(o_ref.dtype)


def mem_attn(zb, kv, layer, tq):
    B, T, _ = zb.shape
    if kv.ndim == 3:
        kv_specs = [pl.BlockSpec((None, MEM_LEN, X_WIDTH), lambda b, i, j=j: (b, 0, j)) for j in range(2)]
    else:
        kv_blk = (None, MEM_LEN, None, None, X_HEADS, X_DH)
        kv_specs = [pl.BlockSpec(kv_blk, lambda b, i, j=j: (b, 0, layer, j, 0, 0)) for j in range(2)]
    return pl.pallas_call(
        _mem_attn_kernel,
        grid=(B, T // tq),
        in_specs=[pl.BlockSpec((1, tq, X_WIDTH), lambda b, i: (b, i, NSA_QPAD // X_WIDTH))] + kv_specs,
        out_specs=pl.BlockSpec((1, tq, X_WIDTH), lambda b, i: (b, i, 0)),
        out_shape=jax.ShapeDtypeStruct((B, T, X_WIDTH), BF16),
        compiler_params=_params(("parallel", "parallel")),
        name="mem_attn",
    )(zb, kv, kv)


def _head_rows(q_ref, nq):
    parts = [q_ref[0, :, hd * LANES:(hd + 1) * LANES] for hd in range(NSA_HEADS)]
    if nq % 16:
        return jnp.concatenate([p.astype(F32) for p in parts], axis=0).astype(BF16)
    return jnp.concatenate(parts, axis=0)


def _store_heads(o_ref, gate, o_c, o_s, o_w, nq):
    lane = lax.broadcasted_iota(jnp.int32, (nq, LANES), 1)
    for hd in range(NSA_HEADS):
        r = slice(hd * nq, (hd + 1) * nq)
        gcol = lambda br: gate[:, NG_OFF + br * NSA_HEADS + hd:NG_OFF + br * NSA_HEADS + hd + 1]
        val = gcol(0) * o_c[r] + gcol(1) * o_s[r] + gcol(2) * o_w[r]
        keep = (lane >= NSA_DH) if hd // NSA_HPG == 1 else (lane < NSA_DH)
        o_ref[0, :, hd * LANES:(hd + 1) * LANES] = jnp.where(keep, val, 0.0).astype(o_ref.dtype)


def _overlap(tok, blk):
    c_start = tok * CMP_STRIDE
    s_start = blk * SEL_LEN
    return ((c_start < s_start + SEL_LEN) & (c_start + CMP_LEN > s_start)).astype(F32)


def _select_blocks(imp, tpos, n_sel, n_top):
    nq, W = imp.shape
    blk = lax.broadcasted_iota(jnp.int32, (nq, W), 1)
    cur = tpos // SEL_LEN
    forced = (blk == 0) | (blk == cur) | (blk == cur - 1)
    v = jnp.where(forced, jnp.inf, jnp.where(blk <= cur, imp, NEG_INF))
    v = jnp.where(blk < n_sel, v, NEG_INF)
    ahead = jnp.zeros((nq, W), F32)
    for j in range(n_sel):
        vj = v[:, j:j + 1]
        ahead = ahead + jnp.where(vj > v, 1.0, jnp.where(vj == v, jnp.where(blk > j, 1.0, 0.0), 0.0))
    return jnp.where((ahead < n_top) & (blk < n_sel), 1.0, 0.0)


def _select_blocks_t(imp_t, tpos_row, n_sel, n_top, ahead_ref=None, t_last=None):
    n_blk, nq = imp_t.shape
    blk = lax.broadcasted_iota(jnp.int32, (n_blk, nq), 0)
    cur = tpos_row // SEL_LEN
    forced = (blk == 0) | (blk == cur) | (blk == cur - 1)
    v = jnp.where(forced, jnp.inf, jnp.where(blk <= cur, imp_t, NEG_INF))
    v = jnp.where(blk < n_sel, v, NEG_INF)
    n_rg = n_blk // SUBLANES
    vg = [v[rg * SUBLANES:(rg + 1) * SUBLANES] for rg in range(n_rg)]
    bg = [rg * SUBLANES + lax.broadcasted_iota(jnp.int32, (SUBLANES, nq), 0) for rg in range(n_rg)]

    def count(j_lo, j_hi):
        ahead = [jnp.zeros((SUBLANES, nq), F32) for _ in range(n_rg)]
        for j in range(j_lo, j_hi):
            vj = v[j:j + 1, :]
            for rg in range(n_rg):
                if rg * SUBLANES > j:
                    inc = jnp.where(vj >= vg[rg], 1.0, 0.0)
                elif (rg + 1) * SUBLANES - 1 <= j:
                    inc = jnp.where(vj > vg[rg], 1.0, 0.0)
                else:
                    inc = jnp.where(vj > vg[rg], 1.0, jnp.where(vj == vg[rg], jnp.where(bg[rg] > j, 1.0, 0.0), 0.0))
                ahead[rg] = ahead[rg] + inc
        return jnp.concatenate(ahead, axis=0)

    if ahead_ref is None:
        ahead = count(0, n_sel)
    else:
        ahead_ref[...] = jnp.zeros((n_blk, nq), F32)
        for j_lo in range(0, n_sel, SUBLANES):
            def add(j_lo=j_lo):
                ahead_ref[...] += count(j_lo, min(j_lo + SUBLANES, n_sel))
            pl.when(j_lo * SEL_LEN <= t_last)(add)
        ahead = ahead_ref[...]
    return jnp.where((ahead < n_top) & (blk < n_sel), 1.0, 0.0)


def _online_update(s, m_old, l_old):
    m_new = jnp.maximum(m_old, jnp.max(s, axis=-1, keepdims=True))
    alpha = jnp.exp2(m_old - m_new)
    p = jnp.exp2(s - m_new)
    return p, m_new, alpha, alpha * l_old + jnp.sum(p, axis=-1, keepdims=True)


def _cmp_tokens_kernel(xk_ref, xv_ref, bd_ref, b_ref, kc_ref, vc_ref, *, nsub):
    for c, (x_ref, o_ref) in enumerate(((xk_ref, kc_ref), (xv_ref, vc_ref))):
        a0 = jnp.zeros((nsub, LANES), F32)
        a1 = jnp.zeros((nsub, LANES), F32)
        for j in range(CMP_STRIDE):
            xj = x_ref[0, :, j, :].astype(BF16)
            a0 = a0 + _dot(xj, bd_ref[c, 0, j])
            a1 = a1 + _dot(xj, bd_ref[c, 1, j])
        tok = a0 + pltpu.roll(a1, nsub - 1, axis=0) + b_ref[c]
        o_ref[0] = tok.astype(o_ref.dtype)


def _cmp_blockdiag(cmp_w_l):
    R = CMP_LEN // CMP_STRIDE
    w = cmp_w_l.astype(F32).reshape(2, R, CMP_STRIDE, NSA_DH, NSA_DH)
    eye = jnp.eye(NSA_GROUPS, dtype=F32)
    return jnp.einsum("ab,crjde->crjadbe", eye, w).reshape(2, R, CMP_STRIDE, LANES, LANES).astype(BF16)


def _cmp_bias(cmp_b_l):
    return jnp.tile(cmp_b_l.astype(F32), (1, NSA_GROUPS)).reshape(2, 1, LANES)


def cmp_tokens(zc, cmp_w_l, cmp_b_l):
    B, T, W = zc.shape
    nsub = T // CMP_STRIDE
    x4 = zc.reshape(B, nsub, CMP_STRIDE, W)
    bd = _cmp_blockdiag(cmp_w_l)
    bias = _cmp_bias(cmp_b_l)
    spec = lambda j: pl.BlockSpec((1, nsub, CMP_STRIDE, LANES), lambda b, j=j: (b, 0, 0, j))
    return pl.pallas_call(
        functools.partial(_cmp_tokens_kernel, nsub=nsub),
        grid=(B,),
        in_specs=[spec(0), spec(1), pl.BlockSpec(bd.shape, lambda b: (0,) * 5), pl.BlockSpec(bias.shape, lambda b: (0, 0, 0))],
        out_specs=[pl.BlockSpec((1, nsub, LANES), lambda b: (b, 0, 0))] * 2,
        out_shape=[jax.ShapeDtypeStruct((B, nsub, LANES), BF16)] * 2,
        compiler_params=_params(("parallel",)),
        name="nsa_cmp_tokens",
    )(x4, x4, bd, bias)


def _nsa_prompt_kernel(q_ref, g_ref, kca_ref, vc_ref, ka_ref, vst_ref, kwa_ref, vw_ref, qf_ref, o_ref,
                       qaug_s, s_buf, p_buf, w_buf, pw_buf, pc_s, m_s, l_s, a_s, acc_s, oc_s, ow_s, ahead_s,
                       *, tq, T, CH):
    start = pl.program_id(1) * tq
    n_cmp_rows = kca_ref.shape[1]
    n_sel = T // SEL_LEN
    n_top = min(SEL_TOPN, n_sel)
    n_blk = -(-n_sel // SUBLANES) * SUBLANES
    nq_all = NSA_HEADS * tq
    qw = nq_all // SEL_QUARTERS
    tpos_row = start + lax.broadcasted_iota(jnp.int32, (1, tq), 1)
    tpos_all = jnp.concatenate([tpos_row] * NSA_HEADS, axis=1)
    qcols = lambda qi: slice(qi * qw, (qi + 1) * qw)
    tiles = [(qi, ct, slice(qi * qw + ct * LANES, qi * qw + (ct + 1) * LANES), slice(ct * LANES, (ct + 1) * LANES))
             for qi in range(SEL_QUARTERS) for ct in range(qw // LANES)]

    qaug_s[:, 0:LANES] = _head_rows(q_ref, tq)
    for hd in range(NSA_HEADS):
        qaug_s[hd * tq:(hd + 1) * tq, LANES:2 * LANES] = jnp.broadcast_to(qf_ref[hd:hd + 1, :], (tq, LANES)).astype(BF16)

    def softmax_tile(s_t):
        mx = jnp.max(s_t, axis=0, keepdims=True)
        mx = jnp.where(mx > NEG_INF, mx, 0.0)
        p_t = jnp.exp2(s_t - mx)
        return p_t, jnp.maximum(jnp.sum(p_t, axis=0, keepdims=True), 1e-30)

    c_end = lax.broadcasted_iota(jnp.int32, (n_cmp_rows, 1), 0) * CMP_STRIDE + CMP_LEN - 1
    for qi in range(SEL_QUARTERS):
        s_buf[qi, 0:n_cmp_rows, :] = _dot_nt(kca_ref[0], qaug_s[qcols(qi), :])
    for qi, ct, cols, tc in tiles:
        p_t, den = softmax_tile(jnp.where(c_end <= tpos_all[:, cols], s_buf[qi, 0:n_cmp_rows, tc], NEG_INF))
        pc_s[:, cols] = p_t / den
    oc_s[...] = _dot_tn(vc_ref[0], pc_s[...].astype(BF16))
    ov_t = _overlap(lax.broadcasted_iota(jnp.int32, (n_blk, n_cmp_rows), 1),
                    lax.broadcasted_iota(jnp.int32, (n_blk, n_cmp_rows), 0))
    lane = lax.broadcasted_iota(jnp.int32, (tq, LANES), 1)
    feat_sel = [None] * NSA_HEADS
    for g in range(NSA_GROUPS):
        pg = pc_s[:, g * NSA_HPG * tq:(g * NSA_HPG + 1) * tq]
        for p in range(1, NSA_HPG):
            pg = pg + pc_s[:, (g * NSA_HPG + p) * tq:(g * NSA_HPG + p + 1) * tq]
        sel_t = _select_blocks_t(_dot(ov_t, pg, HIGHEST), tpos_row, n_sel, n_top, ahead_s, start + tq - 1)
        sel = jnp.concatenate([sel_t, jnp.zeros((LANES - n_blk, tq), F32)], axis=0).T
        unsel = jnp.where(lane < SEL_FEAT, (1.0 - sel) * MASK_BIAS, 0.0)
        for p in range(NSA_HPG):
            hd = g * NSA_HPG + p
            feat_sel[hd] = (unsel + qf_ref[hd:hd + 1, :]).astype(BF16)

    wk = WINDOW + tq
    base_w = pl.multiple_of(start, tq)
    wpos = start - WINDOW + lax.broadcasted_iota(jnp.int32, (wk, 1), 0)
    for qi in range(SEL_QUARTERS):
        w_buf[qi] = _dot_nt(kwa_ref[0, pl.ds(base_w, wk), :], qaug_s[qcols(qi), :])
    for qi, ct, cols, tc in tiles:
        tp = tpos_all[:, cols]
        first = jnp.maximum(tp - (WINDOW - 1), 0)
        s_t = jnp.where(wpos >= first, jnp.where(wpos <= tp, w_buf[qi, :, tc], NEG_INF), NEG_INF)
        p_t, den = softmax_tile(s_t)
        pw_buf[qi, :, tc] = p_t.astype(BF16)
        a_s[:, cols] = den
    for qi in range(SEL_QUARTERS):
        ow_s[:, qcols(qi)] = _dot_tn(vw_ref[0, pl.ds(base_w, wk), :], pw_buf[qi]) / a_s[:, qcols(qi)]

    for hd in range(NSA_HEADS):
        qaug_s[hd * tq:(hd + 1) * tq, LANES:2 * LANES] = feat_sel[hd]

    m_s[...] = jnp.full(m_s.shape, NEG_INF, F32)
    l_s[...] = jnp.zeros(l_s.shape, F32)
    acc_s[...] = jnp.zeros(acc_s.shape, F32)

    def issue(c, qi):
        base = pl.multiple_of(c * CH, CH)
        s_buf[qi] = _dot_nt(ka_ref[0, pl.ds(base, CH), :], qaug_s[qi * qw:(qi + 1) * qw, :])

    def absorb(c, qi, diag, rows=CH):
        for ct in range(qw // LANES):
            cols = slice(qi * qw + ct * LANES, qi * qw + (ct + 1) * LANES)
            s_t = s_buf[qi, 0:rows, ct * LANES:(ct + 1) * LANES]
            if diag:
                kpos = c * CH + lax.broadcasted_iota(jnp.int32, (rows, 1), 0)
                s_t = jnp.where(kpos <= tpos_all[:, cols], s_t, NEG_INF)
            m_old = m_s[:, cols]
            m_new = jnp.maximum(m_old, jnp.max(s_t, axis=0, keepdims=True))
            alpha = jnp.exp2(m_old - m_new)
            p_t = jnp.exp2(s_t - m_new)
            m_s[:, cols] = m_new
            a_s[:, cols] = alpha
            l_s[:, cols] = alpha * l_s[:, cols] + jnp.sum(p_t, axis=0, keepdims=True)
            p_buf[qi, 0:rows, ct * LANES:(ct + 1) * LANES] = p_t.astype(BF16)
        cols = slice(qi * qw, (qi + 1) * qw)
        pv = _dot(vst_ref[0, c, :, 0:rows], p_buf[qi, 0:rows, :])
        acc_s[:, cols] = a_s[:, cols] * acc_s[:, cols] + pv

    def full_chunk(c, carry):
        for qi in range(SEL_QUARTERS):
            absorb(c, qi, False)
            issue(c + 1, qi)
        return carry

    n_full = start // CH
    for qi in range(SEL_QUARTERS):
        issue(0, qi)
    lax.fori_loop(0, n_full, full_chunk, 0)
    tile_first = start - n_full * CH == 0
    for qi in range(SEL_QUARTERS):
        if tq < CH:
            pl.when(tile_first)(functools.partial(absorb, n_full, qi, True, tq))
            pl.when(jnp.logical_not(tile_first))(functools.partial(absorb, n_full, qi, True))
        else:
            absorb(n_full, qi, True)

    gate_t = jax.nn.sigmoid(g_ref[0]).T
    for hd in range(NSA_HEADS):
        cols = slice(hd * tq, (hd + 1) * tq)
        grow = lambda br: gate_t[NG_OFF + br * NSA_HEADS + hd:NG_OFF + br * NSA_HEADS + hd + 1, :]
        o_s = acc_s[:, cols] / jnp.maximum(l_s[:, cols], 1e-30)
        val = (grow(0) * oc_s[:, cols] + grow(1) * o_s + grow(2) * ow_s[:, cols]).T
        keep = (lane >= NSA_DH) if hd // NSA_HPG == 1 else (lane < NSA_DH)
        o_ref[0, :, hd * LANES:(hd + 1) * LANES] = jnp.where(keep, val, 0.0).astype(o_ref.dtype)


def _bf16_parts(x, n):
    parts = []
    for _ in range(n):
        p = float(np.asarray(x, np.float32).astype(jnp.bfloat16).astype(np.float32))
        parts.append(p)
        x = x - p
    return parts


def _slope_features():
    qf = np.zeros((NSA_HEADS, LANES), np.float32)
    for hd in range(NSA_HEADS):
        for p, s_p in enumerate(_bf16_parts(NSA_SLOPES[hd], SLOPE_PARTS)):
            qf[hd, SEL_FEAT + 2 * p] = s_p * SEL_LEN
            qf[hd, SEL_FEAT + 2 * p + 1] = s_p
    return jnp.asarray(qf)


def _key_features(pos, block_onehot):
    pos = np.asarray(pos)
    ok = pos >= 0
    kf = np.zeros((pos.shape[0], LANES), np.float32)
    if block_onehot:
        kf[np.arange(pos.shape[0])[ok], pos[ok] // SEL_LEN] = 1.0
    for p in range(SLOPE_PARTS):
        kf[ok, SEL_FEAT + 2 * p] = pos[ok] // SEL_LEN
        kf[ok, SEL_FEAT + 2 * p + 1] = pos[ok] % SEL_LEN
    return jnp.asarray(kf, BF16)


def nsa_prompt(zb, zd, zcb, kc, vc, tq, CH):
    B, T, _ = zb.shape
    nsub = kc.shape[1]
    assert T % CH == 0 and CH % tq == 0 and T // SEL_LEN <= SEL_FEAT and nsub <= CH
    qf = _slope_features()
    with_feats = lambda k, feats: jnp.concatenate([k, jnp.broadcast_to(feats[None], (B,) + feats.shape)], axis=2)
    k_aug = with_feats(zcb[:, :, 2 * NSA_KV:3 * NSA_KV], _key_features(np.arange(T), True))
    kc_aug = with_feats(kc, _key_features(np.arange(nsub) * CMP_STRIDE + CMP_LEN - 1, False))
    kw = jnp.pad(zcb[:, :, 4 * NSA_KV:5 * NSA_KV], ((0, 0), (WINDOW, 0), (0, 0)))
    kw_aug = with_feats(kw, _key_features(np.arange(T + WINDOW) - WINDOW, False))
    vs_t = jnp.swapaxes(zcb[:, :, 3 * NSA_KV:4 * NSA_KV].reshape(B, T // CH, CH, NSA_KV), 2, 3)
    vw = jnp.pad(zcb[:, :, 5 * NSA_KV:6 * NSA_KV], ((0, 0), (WINDOW, 0), (0, 0)))
    per_b = lambda rows_, w: pl.BlockSpec((1, rows_, w), lambda b, i: (b, 0, 0))
    kern = functools.partial(_nsa_prompt_kernel, tq=tq, T=T, CH=CH)
    nq_all = NSA_HEADS * tq
    qw = nq_all // SEL_QUARTERS
    wk = WINDOW + tq
    assert qw % LANES == 0
    return pl.pallas_call(
        kern,
        grid=(B, T // tq),
        in_specs=[pl.BlockSpec((1, tq, NSA_QPAD), lambda b, i: (b, i, 0)),
                  pl.BlockSpec((1, tq, GATE_COLS), lambda b, i: (b, i, 0)),
                  per_b(nsub, 2 * LANES), per_b(nsub, LANES),
                  per_b(T, 2 * LANES),
                  pl.BlockSpec((1, T // CH, NSA_KV, CH), lambda b, i: (b, 0, 0, 0)),
                  per_b(T + WINDOW, 2 * LANES), per_b(T + WINDOW, LANES),
                  pl.BlockSpec(qf.shape, lambda b, i: (0, 0))],
        out_specs=pl.BlockSpec((1, tq, NSA_QPAD), lambda b, i: (b, i, 0)),
        out_shape=jax.ShapeDtypeStruct((B, T, NSA_QPAD), BF16),
        scratch_shapes=[pltpu.VMEM((nq_all, 2 * LANES), BF16),
                        pltpu.VMEM((SEL_QUARTERS, CH, qw), F32), pltpu.VMEM((SEL_QUARTERS, CH, qw), BF16),
                        pltpu.VMEM((SEL_QUARTERS, wk, qw), F32), pltpu.VMEM((SEL_QUARTERS, wk, qw), BF16),
                        pltpu.VMEM((nsub, nq_all), F32),
                        pltpu.VMEM((1, nq_all), F32), pltpu.VMEM((1, nq_all), F32), pltpu.VMEM((1, nq_all), F32),
                        pltpu.VMEM((NSA_KV, nq_all), F32), pltpu.VMEM((NSA_KV, nq_all), F32),
                        pltpu.VMEM((NSA_KV, nq_all), F32),
                        pltpu.VMEM((-(-(T // SEL_LEN) // SUBLANES) * SUBLANES, tq), F32)],
        compiler_params=_params(("parallel", "arbitrary")),
        name="nsa_prompt",
    )(zb, zd, kc_aug, vc, k_aug, vs_t, kw_aug, vw, qf)


def _page_specs(shape, slot_blk, layer, n_pages):
    def mk(i):
        def imap(b, s, pt):
            return (pt[b * n_pages + s * PAGES_PER_STEP + i], layer, slot_blk, 0, 0)
        return pl.BlockSpec(shape, imap)
    return [mk(i) for i in range(PAGES_PER_STEP)]


def _nsa_dec_cmp_kernel(pt_ref, *refs, T, past_len, n_rows):
    pages = refs[:PAGES_PER_STEP]
    perm_ref, bd_ref, b_ref, q_ref, oc_ref, sel_ref, tok_s, pend_s = refs[PAGES_PER_STEP:]
    s = pl.program_id(1)
    n_steps = pl.num_programs(1)
    page_rows = pages[0].shape[-1]
    sub = page_rows // CMP_STRIDE
    R = PAGES_PER_STEP * sub

    @pl.when(s == 0)
    def _():
        pend_s[...] = jnp.zeros(pend_s.shape, F32)

    rid = lax.broadcasted_iota(jnp.int32, (R, LANES), 0)
    for c in range(2):
        xs = [_dot_nt(perm_ref[...], pg[0, 0, c].astype(BF16)) for pg in pages]
        a0 = jnp.zeros((R, LANES), F32)
        a1 = jnp.zeros((R, LANES), F32)
        for j in range(CMP_STRIDE):
            xj = jnp.concatenate([x[j * sub:(j + 1) * sub] for x in xs], axis=0).astype(BF16)
            a0 = a0 + _dot(xj, bd_ref[c, 0, j])
            a1 = a1 + _dot(xj, bd_ref[c, 1, j])
        a0 = a0 + b_ref[c]
        tok = jnp.where(rid == 0, pend_s[c], pltpu.roll(a0, 1, axis=0)) + a1
        tok_s[c, pl.ds(pl.multiple_of(s * R, R), R), :] = tok
        pend_s[c] = a0[R - 1:R, :]

    @pl.when(s == n_steps - 1)
    def _():
        qp = _head_rows(q_ref, T)
        kc = tok_s[0].astype(BF16)
        vc = tok_s[1].astype(BF16)
        sc_all = _dot_nt(qp, kc)
        tpos = past_len + lax.broadcasted_iota(jnp.int32, (T, 1), 0)
        r = lax.broadcasted_iota(jnp.int32, (T, n_rows), 1)
        dc = tpos - ((r - 1) * CMP_STRIDE + CMP_LEN - 1)
        mask_c = (dc >= 0) & (r >= 1)
        dcf = dc.astype(F32)
        pcs = [_masked_softmax2(sc_all[hd * T:(hd + 1) * T] - NSA_SLOPES[hd] * dcf, mask_c) for hd in range(NSA_HEADS)]
        oc_ref[0] = _dot(jnp.concatenate(pcs, axis=0).astype(BF16), vc)
        W = sel_ref.shape[2]
        n_sel = -(-(past_len + T) // SEL_LEN)
        ov = _overlap(lax.broadcasted_iota(jnp.int32, (n_rows, W), 0) - 1, lax.broadcasted_iota(jnp.int32, (n_rows, W), 1))
        for g in range(NSA_GROUPS):
            pg = pcs[g * NSA_HPG]
            for p in range(1, NSA_HPG):
                pg = pg + pcs[g * NSA_HPG + p]
            imp = _dot(pg, ov, HIGHEST)
            sel_ref[0, g * T:(g + 1) * T, :] = _select_blocks(imp, tpos, n_sel, min(SEL_TOPN, n_sel))


def _nsa_dec_sel_kernel(pt_ref, *refs, T, past_len):
    pages = refs[:PAGES_PER_STEP]
    q_ref, sel_ref, sel_last_ref, ex_ref, kn_ref, vn_ref, os_ref, m_s, l_s, acc_s = refs[PAGES_PER_STEP:]
    s = pl.program_id(1)
    n_steps = pl.num_programs(1)
    page_rows = pages[0].shape[-1]
    CH = PAGES_PER_STEP * page_rows

    @pl.when(s == 0)
    def _():
        m_s[...] = jnp.full(m_s.shape, NEG_INF, F32)
        l_s[...] = jnp.zeros(l_s.shape, F32)
        acc_s[...] = jnp.zeros(acc_s.shape, F32)

    qp = _head_rows(q_ref, T)
    tpos = past_len + lax.broadcasted_iota(jnp.int32, (T, 1), 0)

    def update(s_all, mask_of_group, kpos, pv):
        ds = tpos - kpos
        dsf = ds.astype(F32)
        ps = []
        for g in range(NSA_GROUPS):
            mk = mask_of_group(g) & (ds >= 0)
            for p in range(NSA_HPG):
                hd = g * NSA_HPG + p
                r = slice(hd * T, (hd + 1) * T)
                sc = jnp.where(mk, s_all[r] - NSA_SLOPES[hd] * dsf, NEG_INF)
                pr, m_new, alpha, l_new = _online_update(sc, m_s[r], l_s[r])
                m_s[r] = m_new
                l_s[r] = l_new
                acc_s[r] = alpha * acc_s[r]
                ps.append(pr)
        acc_s[...] = acc_s[...] + pv(jnp.concatenate(ps, axis=0).astype(BF16))

    selk = _dot(sel_ref[0, 0].astype(BF16), ex_ref[...])
    s_all = jnp.concatenate([_dot(qp, pg[0, 0, 0].astype(BF16)) for pg in pages], axis=1)
    kpos = s * CH + lax.broadcasted_iota(jnp.int32, (T, CH), 1)

    def pv_pages(pmat):
        out = jnp.zeros((NSA_HEADS * T, LANES), F32)
        for i, pg in enumerate(pages):
            out = out + _dot_nt(pmat[:, i * page_rows:(i + 1) * page_rows], pg[0, 0, 1].astype(BF16))
        return out

    update(s_all, lambda g: selk[g * T:(g + 1) * T] > 0.5, kpos, pv_pages)

    @pl.when(s == n_steps - 1)
    def _():
        nk = kn_ref.shape[1]
        sn = _dot_nt(qp, kn_ref[0])
        lane = lax.broadcasted_iota(jnp.int32, (T, nk), 1)
        sl = sel_last_ref[0, 0]
        update(sn, lambda g: (sl[g * T:(g + 1) * T, 0:1] > 0.5) & (lane < T), past_len + lane,
               lambda pmat: _dot(pmat, vn_ref[0]))
        os_ref[0] = acc_s[...] / jnp.maximum(l_s[...], 1e-30)


def _nsa_dec_win_kernel(q_ref, g_ref, oc_ref, os_ref, wp_ref, kwn_ref, vwn_ref, o_ref, *, T, past_len):
    qp = _head_rows(q_ref, T)
    w_src = wp_ref.shape[-1]
    nk = kwn_ref.shape[1]
    sw_all = jnp.concatenate([_dot(qp, wp_ref[0, 0, 0].astype(BF16)), _dot_nt(qp, kwn_ref[0])], axis=1)
    j = lax.broadcasted_iota(jnp.int32, (T, w_src + nk), 1)
    tpos = past_len + lax.broadcasted_iota(jnp.int32, (T, 1), 0)
    wpos = past_len - w_src + j
    dw = tpos - wpos
    mask_w = (dw >= 0) & (dw < WINDOW) & (wpos >= 0) & (j < w_src + T)
    dwf = dw.astype(F32)
    pws = [_masked_softmax2(sw_all[hd * T:(hd + 1) * T] - NSA_SLOPES[hd] * dwf, mask_w) for hd in range(NSA_HEADS)]
    pw = jnp.concatenate(pws, axis=0).astype(BF16)
    o_w = _dot_nt(pw[:, :w_src], wp_ref[0, 0, 1].astype(BF16)) + _dot(pw[:, w_src:], vwn_ref[0])
    _store_heads(o_ref, jax.nn.sigmoid(g_ref[0]), oc_ref[0], os_ref[0], o_w, T)


def nsa_decode(zb, zd, zcb, cache_nsa_kv, cache_win_kv, page_table, cmp_w_l, cmp_b_l, layer):
    DB, T, _ = zb.shape
    n_pool, page_rows = cache_nsa_kv.shape[:2]
    n_pages = page_table.shape[1]
    past_len = n_pages * page_rows
    n_steps = n_pages // PAGES_PER_STEP
    sub_per_page = page_rows // CMP_STRIDE
    n_rows = n_pages * sub_per_page
    assert (n_rows - 1) * CMP_STRIDE + CMP_LEN - 1 > past_len + T - 1
    assert T <= CMP_STRIDE and n_pages % PAGES_PER_STEP == 0 and page_rows == LANES
    pt = page_table.reshape(-1).astype(jnp.int32)
    cache_t = jnp.transpose(cache_nsa_kv, (0, 2, 3, 4, 5, 1)).reshape(n_pool, DEPTH, 4, NSA_KV, page_rows)
    page_blk = (1, 1, 2, NSA_KV, page_rows)

    rr = np.arange(page_rows)
    perm = jnp.asarray(rr[None, :] == ((rr % sub_per_page) * CMP_STRIDE + rr // sub_per_page)[:, None], BF16)
    bd = _cmp_blockdiag(cmp_w_l)
    bias = _cmp_bias(cmp_b_l)

    n_sel = -(-(past_len + T) // SEL_LEN)
    blocks_per_step = PAGES_PER_STEP * page_rows // SEL_LEN
    sel_used = (n_steps + 1) * blocks_per_step
    sel_w = -(-sel_used // LANES) * LANES
    assert sel_used >= n_sel and blocks_per_step <= LANES
    q_spec = pl.BlockSpec((1, T, NSA_QPAD), lambda b, s, pt: (b, 0, 0))
    const = lambda a: pl.BlockSpec(a.shape, lambda b, s, pt: (0,) * a.ndim)

    o_c, sel = pl.pallas_call(
        functools.partial(_nsa_dec_cmp_kernel, T=T, past_len=past_len, n_rows=n_rows),
        grid_spec=pltpu.PrefetchScalarGridSpec(
            num_scalar_prefetch=1, grid=(DB, n_steps),
            in_specs=_page_specs(page_blk, 0, layer, n_pages) + [const(perm), const(bd), const(bias), q_spec],
            out_specs=[pl.BlockSpec((1, NSA_HEADS * T, LANES), lambda b, s, pt: (b, 0, 0)),
                       pl.BlockSpec((1, NSA_GROUPS * T, sel_w), lambda b, s, pt: (b, 0, 0))],
            scratch_shapes=[pltpu.VMEM((2, n_rows, LANES), F32), pltpu.VMEM((2, 1, LANES), F32)]),
        out_shape=[jax.ShapeDtypeStruct((DB, NSA_HEADS * T, LANES), F32),
                   jax.ShapeDtypeStruct((DB, NSA_GROUPS * T, sel_w), F32)],
        compiler_params=_params(("parallel", "arbitrary")),
        name="nsa_dec_cmp",
    )(pt, *([cache_t] * PAGES_PER_STEP), perm, bd, bias, zb)

    sel_steps = sel[:, :, :sel_used].reshape(DB, NSA_GROUPS * T, n_steps + 1, blocks_per_step).transpose(0, 2, 1, 3)
    sel_steps = jnp.pad(sel_steps, ((0, 0), (0, 0), (0, 0), (0, LANES - blocks_per_step)))
    kk = np.arange(PAGES_PER_STEP * page_rows) // SEL_LEN
    expand = jnp.asarray(kk[None, :] == np.arange(LANES)[:, None], BF16)
    pad_rows = LANES - T
    new_rows = lambda slot: jnp.pad(zcb[:, :, slot * NSA_KV:(slot + 1) * NSA_KV], ((0, 0), (0, pad_rows), (0, 0)))
    new_spec = pl.BlockSpec((1, LANES, LANES), lambda b, s, pt: (b, 0, 0))
    sel_blk = (1, 1, NSA_GROUPS * T, LANES)

    o_s = pl.pallas_call(
        functools.partial(_nsa_dec_sel_kernel, T=T, past_len=past_len),
        grid_spec=pltpu.PrefetchScalarGridSpec(
            num_scalar_prefetch=1, grid=(DB, n_steps),
            in_specs=_page_specs(page_blk, 1, layer, n_pages)
            + [q_spec, pl.BlockSpec(sel_blk, lambda b, s, pt: (b, s, 0, 0)),
               pl.BlockSpec(sel_blk, lambda b, s, pt: (b, n_steps, 0, 0)), const(expand), new_spec, new_spec],
            out_specs=pl.BlockSpec((1, NSA_HEADS * T, LANES), lambda b, s, pt: (b, 0, 0)),
            scratch_shapes=[pltpu.VMEM((NSA_HEADS * T, 1), F32), pltpu.VMEM((NSA_HEADS * T, 1), F32),
                            pltpu.VMEM((NSA_HEADS * T, LANES), F32)]),
        out_shape=jax.ShapeDtypeStruct((DB, NSA_HEADS * T, LANES), F32),
        compiler_params=_params(("parallel", "arbitrary")),
        name="nsa_dec_sel",
    )(pt, *([cache_t] * PAGES_PER_STEP), zb, sel_steps, sel_steps, expand, new_rows(2), new_rows(3))

    w_src = cache_win_kv.shape[1]
    win_t = jnp.transpose(cache_win_kv, (0, 2, 3, 4, 5, 1)).reshape(DB, DEPTH, 2, NSA_KV, w_src)
    b3 = lambda shp: pl.BlockSpec(shp, lambda b: (b, 0, 0))
    return pl.pallas_call(
        functools.partial(_nsa_dec_win_kernel, T=T, past_len=past_len),
        grid=(DB,),
        in_specs=[b3((1, T, NSA_QPAD)), b3((1, T, GATE_COLS)), b3((1, NSA_HEADS * T, LANES)), b3((1, NSA_HEADS * T, LANES)),
                  pl.BlockSpec((1, 1, 2, NSA_KV, w_src), lambda b: (b, layer, 0, 0, 0)),
                  b3((1, LANES, LANES)), b3((1, LANES, LANES))],
        out_specs=b3((1, T, NSA_QPAD)),
        out_shape=jax.ShapeDtypeStruct((DB, T, NSA_QPAD), BF16),
        compiler_params=_params(("parallel",)),
        name="nsa_dec_win",
    )(zb, zd, o_c, o_s, win_t, new_rows(4), new_rows(5))


def _route_top2(logits):
    lane = lax.broadcasted_iota(jnp.int32, logits.shape, 1)
    W = logits.shape[1]
    logits = jnp.where(lane < N_EXPERTS, logits, NEG_INF)
    m1 = jnp.max(logits, axis=-1, keepdims=True)
    i1 = jnp.min(jnp.where(logits == m1, lane, W), axis=-1, keepdims=True)
    rest = jnp.where(lane == i1, NEG_INF, logits)
    m2 = jnp.max(rest, axis=-1, keepdims=True)
    i2 = jnp.min(jnp.where(rest == m2, lane, W), axis=-1, keepdims=True)
    e2 = jnp.exp(m2 - m1)
    den = 1.0 + e2
    return jnp.where(lane == i1, 1.0 / den, 0.0) + jnp.where(lane == i2, e2 / den, 0.0)


def _router_kernel(x_ref, w_ref, b_ref, o_ref):
    x = x_ref[...]
    w = w_ref[...]
    x_hi = x.astype(BF16)
    x_lo = (x - x_hi.astype(F32)).astype(BF16)
    w_hi = w.astype(BF16)
    w_lo = (w - w_hi.astype(F32)).astype(BF16)
    logits = _dot(x_hi, w_hi) + _dot(x_hi, w_lo) + _dot(x_lo, w_hi)
    o_ref[...] = _route_top2(logits + b_ref[...])


def router(x2d, w_router, b_router, tm):
    n = x2d.shape[0]
    w = jnp.pad(w_router.astype(F32), ((0, 0), (0, LANES - N_EXPERTS)))
    bb = jnp.pad(b_router.astype(F32), (0, LANES - N_EXPERTS)).reshape(1, LANES)
    return pl.pallas_call(
        _router_kernel,
        grid=(n // tm,),
        in_specs=[pl.BlockSpec((tm, D_MODEL), lambda i: (i, 0)), pl.BlockSpec(w.shape, lambda i: (0, 0)),
                  pl.BlockSpec(bb.shape, lambda i: (0, 0))],
        out_specs=pl.BlockSpec((tm, LANES), lambda i: (i, 0)),
        out_shape=jax.ShapeDtypeStruct((n, LANES), F32),
        compiler_params=_params(("parallel",)),
        name="router",
    )(x2d, w, bb)


def _merge_kernel(x_ref, hm_ref, hn_ref, hx_ref, wg_ref, bg_ref, wbm_ref, wbn_ref, wbx_ref, wo_ref, g_ref, b_ref,
                  o_ref):
    x = x_ref[...]
    xb = x.astype(BF16)
    merged = None
    for c, (h_ref, w_ref) in enumerate(((hm_ref, wbm_ref), (hn_ref, wbn_ref), (hx_ref, wbx_ref))):
        sl = slice(c * D_MODEL, (c + 1) * D_MODEL)
        gate = jax.nn.sigmoid(_dot(xb, wg_ref[:, sl]) + bg_ref[:, sl])
        term = gate * _dot(h_ref[...], w_ref[...])
        merged = term if merged is None else merged + term
    mix = _dot(merged.astype(BF16), wo_ref[...])
    o_ref[...] = _layer_norm(ALPHA * x + mix, g_ref[...], b_ref[...])


def merge(x2d, hm, hn, hx, wg, bg, wbr, wo, g, b, tm):
    n = x2d.shape[0]
    row = lambda w: pl.BlockSpec((tm, w), lambda i: (i, 0))
    full = lambda a: pl.BlockSpec(a.shape, lambda i: (0,) * a.ndim, pipeline_mode=pl.Buffered(1))
    args = [x2d, hm, hn, hx, wg, bg, wbr[0], wbr[1], wbr[2], wo, g, b]
    return pl.pallas_call(
        _merge_kernel,
        grid=(n // tm,),
        in_specs=[row(a.shape[1]) for a in args[:4]] + [full(a) for a in args[4:]],
        out_specs=row(D_MODEL),
        out_shape=jax.ShapeDtypeStruct((n, D_MODEL), F32),
        compiler_params=_params(("parallel",)),
        name="merge",
    )(*args)


def _ffn_kernel(x_ref, wg_ref, wu_ref, wd_ref, g_ref, b_ref, o_ref, acc_s):
    f = pl.program_id(1)

    @pl.when(f == 0)
    def _():
        acc_s[...] = jnp.zeros(acc_s.shape, F32)

    xb = x_ref[...].astype(BF16)
    hcur = jax.nn.silu(_dot(xb, wg_ref[...])) * _dot(xb, wu_ref[...])
    acc_s[...] += _dot(hcur.astype(BF16), wd_ref[...])

    @pl.when(f == pl.num_programs(1) - 1)
    def _():
        o_ref[...] = _layer_norm(ALPHA * x_ref[...] + acc_s[...], g_ref[...], b_ref[...])


def ffn(x2d, w_up, w_down, g, b, tm, tf):
    n = x2d.shape[0]
    nf = (w_up.shape[1] // 2) // tf
    wmode = dict(pipeline_mode=pl.Buffered(1)) if nf == 1 else {}
    return pl.pallas_call(
        _ffn_kernel,
        grid=(n // tm, nf),
        in_specs=[pl.BlockSpec((tm, D_MODEL), lambda i, f: (i, 0)),
                  pl.BlockSpec((D_MODEL, tf), lambda i, f: (0, f), **wmode),
                  pl.BlockSpec((D_MODEL, tf), lambda i, f: (0, nf + f), **wmode),
                  pl.BlockSpec((tf, D_MODEL), lambda i, f: (f, 0), **wmode),
                  pl.BlockSpec((1, D_MODEL), lambda i, f: (0, 0)),
                  pl.BlockSpec((1, D_MODEL), lambda i, f: (0, 0))],
        out_specs=pl.BlockSpec((tm, D_MODEL), lambda i, f: (i, 0)),
        out_shape=jax.ShapeDtypeStruct((n, D_MODEL), F32),
        scratch_shapes=[pltpu.VMEM((tm, D_MODEL), F32)],
        compiler_params=_params(("parallel", "arbitrary")),
        name="dense_ffn",
    )(x2d, w_up, w_up, w_down, g, b)


def _moe_kernel(np_ref, x_ref, cw_ref, cwt_ref, wg_ref, wu_ref, wd_ref, g_ref, b_ref, o_ref,
                xb_s, y_s, pos_s, post_s, xg_s, ws_s, acc_s, *, blk, cap, cap_pad):
    j = pl.program_id(0)
    e = pl.program_id(1)
    f = pl.program_id(2)
    n_e = pl.num_programs(1)
    n_f = pl.num_programs(2)
    n_pass = np_ref[j * n_e + e]
    RC = min(256, blk)

    @pl.when((e == 0) & (f == 0))
    def _():
        xb_s[...] = x_ref[...].astype(BF16)
        y_s[...] = jnp.zeros(y_s.shape, F32)
        cw = cw_ref[...]
        cwt = cwt_ref[...]
        routed = jnp.where(cw != 0.0, 1.0, 0.0).astype(BF16)
        routed_t = jnp.where(cwt != 0.0, 1.0, 0.0).astype(BF16)
        for rc in range(blk // RC):
            rows_i = rc * RC + lax.broadcasted_iota(jnp.int32, (RC, blk), 0)
            cols_i = lax.broadcasted_iota(jnp.int32, (RC, blk), 1)
            before = jnp.where(cols_i < rows_i, 1.0, 0.0).astype(BF16)
            cnt = _dot(before, routed)
            pos_s[rc * RC:(rc + 1) * RC, :] = jnp.where(cw[rc * RC:(rc + 1) * RC] != 0.0, cnt, -1.0)
            rows_j = lax.broadcasted_iota(jnp.int32, (blk, RC), 0)
            cols_j = rc * RC + lax.broadcasted_iota(jnp.int32, (blk, RC), 1)
            before_t = jnp.where(rows_j < cols_j, 1.0, 0.0).astype(BF16)
            cnt_t = _dot(routed_t, before_t)
            post_s[:, rc * RC:(rc + 1) * RC] = jnp.where(cwt[:, rc * RC:(rc + 1) * RC] != 0.0, cnt_t, -1.0)

    @pl.when(f == 0)
    def _():
        prow = post_s[pl.ds(e, 1), :]
        wrow = cwt_ref[pl.ds(e, 1), :]

        def gather(u, carry):
            slot = (u * cap + lax.broadcasted_iota(jnp.int32, (cap, 1), 0)).astype(F32)
            hit = prow == slot
            xg_s[u, 0:cap, :] = _dot(jnp.where(hit, 1.0, 0.0).astype(BF16), xb_s[...]).astype(BF16)
            ws_s[u, 0:cap, :] = jnp.sum(jnp.where(hit, wrow, 0.0), axis=-1, keepdims=True)
            acc_s[u] = jnp.zeros(acc_s.shape[1:], F32)
            return carry

        lax.fori_loop(0, n_pass, gather, 0)

    def expert(u, carry):
        xg = xg_s[u, 0:cap, :]
        hcur = jax.nn.silu(_dot(xg, wg_ref[0])) * _dot(xg, wu_ref[0]) * ws_s[u, 0:cap, :]
        acc_s[u, 0:cap, :] += _dot(hcur.astype(BF16), wd_ref[0])
        return carry

    lax.fori_loop(0, n_pass, expert, 0)

    @pl.when(f == n_f - 1)
    def _():
        lane = lax.broadcasted_iota(jnp.int32, pos_s.shape, 1)
        pcol = jnp.sum(jnp.where(lane == e, pos_s[...], 0.0), axis=-1, keepdims=True)

        def scatter(u, carry):
            cc = lax.broadcasted_iota(jnp.int32, (1, cap_pad), 1)
            slot = jnp.where(cc < cap, u * cap + cc, NO_SLOT).astype(F32)
            hit = jnp.where(pcol == slot, 1.0, 0.0).astype(BF16)
            y_s[...] += _dot(hit, acc_s[u].astype(BF16))
            return carry

        lax.fori_loop(0, n_pass, scatter, 0)

    @pl.when((e == n_e - 1) & (f == n_f - 1))
    def _():
        o_ref[...] = _layer_norm(ALPHA * x_ref[...] + y_s[...], g_ref[...], b_ref[...])


def moe(x2d, cw, w_up, w_down, g, b, tf):
    n = x2d.shape[0]
    E, _, F2 = w_up.shape
    nf = (F2 // 2) // tf
    blk = _row_tile(n, MOE_BLOCK)
    cap = min(MOE_CAP, blk)
    cap_pad = -(-cap // LANES) * LANES
    max_pass = -(-blk // cap)
    nblk = n // blk
    cwt = jnp.swapaxes(cw[:, :ROUTE_ROWS], 0, 1)
    counts = jnp.sum((cw[:, :E] != 0.0).reshape(nblk, blk, E), axis=1)
    n_pass = ((counts + cap - 1) // cap).astype(jnp.int32).reshape(-1)
    return pl.pallas_call(
        functools.partial(_moe_kernel, blk=blk, cap=cap, cap_pad=cap_pad),
        grid_spec=pltpu.PrefetchScalarGridSpec(
            num_scalar_prefetch=1, grid=(nblk, E, nf),
            in_specs=[pl.BlockSpec((blk, D_MODEL), lambda j, e, f, npr: (j, 0), pipeline_mode=pl.Buffered(1)),
                      pl.BlockSpec((blk, LANES), lambda j, e, f, npr: (j, 0)),
                      pl.BlockSpec((ROUTE_ROWS, blk), lambda j, e, f, npr: (0, j)),
                      pl.BlockSpec((1, D_MODEL, tf), lambda j, e, f, npr: (e, 0, f)),
                      pl.BlockSpec((1, D_MODEL, tf), lambda j, e, f, npr: (e, 0, nf + f)),
                      pl.BlockSpec((1, tf, D_MODEL), lambda j, e, f, npr: (e, f, 0)),
                      pl.BlockSpec((1, D_MODEL), lambda j, e, f, npr: (0, 0)),
                      pl.BlockSpec((1, D_MODEL), lambda j, e, f, npr: (0, 0))],
            out_specs=pl.BlockSpec((blk, D_MODEL), lambda j, e, f, npr: (j, 0)),
            scratch_shapes=[pltpu.VMEM((blk, D_MODEL), BF16), pltpu.VMEM((blk, D_MODEL), F32),
                            pltpu.VMEM((blk, LANES), F32), pltpu.VMEM((ROUTE_ROWS, blk), F32),
                            pltpu.VMEM((max_pass, cap_pad, D_MODEL), BF16), pltpu.VMEM((max_pass, cap_pad, 1), F32),
                            pltpu.VMEM((max_pass, cap_pad, D_MODEL), F32)]),
        out_shape=jax.ShapeDtypeStruct((n, D_MODEL), F32),
        compiler_params=_params(("parallel", "arbitrary", "arbitrary")),
        name="moe_ffn",
    )(n_pass, x2d, cw, cwt, w_up, w_up, w_down, g, b)


def _layer(x, lw, l, *, mem_kv, mem_layer, mstate, decode):
    B, T, _ = x.shape
    n = B * T
    x2d = x.reshape(n, D_MODEL)
    tm = _row_tile(n, 1024)
    za, zb, zc, zcb, zd = in_proj(x2d, lw["in"], tm)
    za, zb, zc, zcb, zd = (a.reshape(B, T, -1) for a in (za, zb, zc, zcb, zd))

    if T % MLSTM_CHUNK == 0:
        h_m, ct, nt, mt = mlstm(za, zd, *mstate, lw["norm_g"], L=MLSTM_CHUNK, t_valid=MLSTM_CHUNK)
    else:
        padt = lambda a: jnp.pad(a, ((0, 0), (0, LANES - T), (0, 0)))
        h_m, ct, nt, mt = mlstm(padt(za), padt(zd), *mstate, lw["norm_g"], L=LANES, t_valid=T)
        h_m = h_m[:, :T]

    if decode is None:
        kc, vc = cmp_tokens(zc, lw["cmp_w"], lw["cmp_b"])
        h_n = nsa_prompt(zb, zd, zcb, kc, vc, tq=min(256, T), CH=min(512, T))
    else:
        h_n = nsa_decode(zb, zd, zcb, decode["nsa"], decode["win"], decode["pt"], lw["cmp_w"], lw["cmp_b"], l)

    h_x = mem_attn(zb, mem_kv, mem_layer, tq=_row_tile(T, 512))

    wg, bg = lw["in"]["mg"]
    flat = lambda a: a.reshape(n, -1)
    x1 = merge(x2d, flat(h_m), flat(h_n), flat(h_x), wg, bg, lw["w_branch"], lw["w_out"], lw["ln1_g"], lw["ln1_b"], tm)

    if l % 2 == 0:
        x2 = ffn(x1, lw["ffn_up"], lw["ffn_down"], lw["ln2_g"], lw["ln2_b"], _row_tile(n, 512), D_FF)
    else:
        cw = router(x1, lw["w_router"], lw["b_router"], _row_tile(n, 1024))
        x2 = moe(x1, cw, lw["ffn_up"], lw["ffn_down"], lw["ln2_g"], lw["ln2_b"], D_FF_EXPERT // 2)
    return x2.reshape(B, T, D_MODEL), zc, (ct, nt, mt)


def kernel(x_prompt, x_sample, mem_prompt, cache_nsa_kv, cache_win_kv, state_mlstm_C, state_mlstm_n,
           state_mlstm_m, cache_mem_kv, page_table, w_in, b_in, mlstm_norm_g, cmp_w, cmp_b, w_mem_kv,
           w_branch, w_out, ln1_g, ln1_b, ln2_g, ln2_b, ffn_w_up, ffn_w_down, moe_w_router, moe_b_router,
           moe_w_up, moe_w_down):
    B, T, _ = x_prompt.shape
    DB, TS, _ = x_sample.shape
    xp, xs = x_prompt, x_sample
    nsa_p, nsa_s, win_p, win_s = [], [], [], []
    Cp, np_, mp, Cs, ns, ms, memkv_p = [], [], [], [], [], [], []
    row1 = lambda a: a.reshape(1, -1).astype(F32)
    for l in range(DEPTH):
        wbr = (w_branch[l, 0].astype(BF16), _pad_branch_rows(w_branch[l, 1]).astype(BF16), w_branch[l, 2].astype(BF16))
        lw = dict(norm_g=mlstm_norm_g[l], cmp_w=cmp_w[l], cmp_b=cmp_b[l], w_branch=wbr,
                  w_out=w_out[l].astype(BF16), ln1_g=row1(ln1_g[l]), ln1_b=row1(ln1_b[l]),
                  ln2_g=row1(ln2_g[l]), ln2_b=row1(ln2_b[l]))
        lw["in"] = _split_in_proj(w_in[l], b_in[l])
        if l % 2 == 0:
            lw["ffn_up"] = ffn_w_up[l // 2].astype(BF16)
            lw["ffn_down"] = ffn_w_down[l // 2].astype(BF16)
        else:
            lw["ffn_up"] = moe_w_up[l // 2].astype(BF16)
            lw["ffn_down"] = moe_w_down[l // 2].astype(BF16)
            lw["w_router"] = moe_w_router[l // 2]
            lw["b_router"] = moe_b_router[l // 2]

        mkv = proj(mem_prompt.reshape(B * MEM_LEN, D_MODEL), w_mem_kv[l].astype(BF16), _row_tile(B * MEM_LEN, 512))
        mkv = mkv.reshape(B, MEM_LEN, 2 * X_WIDTH)
        st0 = (jnp.zeros((B, M_HEADS, M_DH, M_DH), F32), jnp.zeros((B, M_HEADS, M_DH), F32), jnp.zeros((B, M_HEADS), F32))
        xp, zc, st = _layer(xp, lw, l, mem_kv=mkv, mem_layer=0, mstate=st0, decode=None)
        kvn = zc.reshape(B, T, 6, NSA_GROUPS, NSA_DH)
        nsa_p.append(kvn[:, :, :4])
        win_p.append(kvn[:, T - min(WINDOW, T):, 4:])
        Cp.append(st[0]); np_.append(st[1]); mp.append(st[2])
        memkv_p.append(mkv.reshape(B, MEM_LEN, 2, X_HEADS, X_DH))

        sts = (state_mlstm_C[:, l], state_mlstm_n[:, l], state_mlstm_m[:, l])
        dec = dict(nsa=cache_nsa_kv, win=cache_win_kv, pt=page_table)
        xs, zc, st = _layer(xs, lw, l, mem_kv=cache_mem_kv, mem_layer=l, mstate=sts, decode=dec)
        kvn = zc.reshape(DB, TS, 6, NSA_GROUPS, NSA_DH)
        nsa_s.append(kvn[:, :, :4])
        win_s.append(jnp.concatenate([cache_win_kv[:, :, l].astype(F32), kvn[:, :, 4:]], axis=1)[:, TS:])
        Cs.append(st[0]); ns.append(st[1]); ms.append(st[2])
    return (xp, xs,
            jnp.stack(nsa_p, axis=2), jnp.stack(nsa_s, axis=2),
            jnp.stack(win_p, axis=2), jnp.stack(win_s, axis=2),
            jnp.stack(Cp, axis=1), jnp.stack(np_, axis=1), jnp.stack(mp, axis=1),
            jnp.stack(Cs, axis=1), jnp.stack(ns, axis=1), jnp.stack(ms, axis=1),
            jnp.stack(memkv_p, axis=2))
```

```python
import functools

import numpy as np
import jax
import jax.numpy as jnp
from jax import lax
from jax.experimental import pallas as pl
from jax.experimental.pallas import tpu as pltpu

D_MODEL = 1024
DEPTH = 2
BRANCH_WIDTH = 512
N_BRANCH = 3
M_HEADS = 4
M_DH = BRANCH_WIDTH // M_HEADS
M_WIDTH = M_HEADS * M_DH
NSA_HEADS = 8
NSA_DH = BRANCH_WIDTH // NSA_HEADS
NSA_WIDTH = NSA_HEADS * NSA_DH
NSA_GROUPS = 2
NSA_HPG = NSA_HEADS // NSA_GROUPS
NSA_KV = NSA_GROUPS * NSA_DH
CMP_LEN = 32
CMP_STRIDE = 16
SEL_LEN = 64
SEL_TOPN = 16
WINDOW = 512
MEM_LEN = 256
X_HEADS = 4
X_DH = BRANCH_WIDTH // X_HEADS
X_WIDTH = X_HEADS * X_DH
D_FF = 2816
N_EXPERTS = 8
TOP_K = 2
D_FF_EXPERT = 3584
ALPHA = (2.0 * DEPTH) ** 0.25
LN_EPS = 1e-5
IN_SPLITS = (M_WIDTH, M_WIDTH, M_WIDTH, M_HEADS, M_HEADS, M_WIDTH,
             NSA_WIDTH, 6 * NSA_KV, 3 * NSA_HEADS, X_WIDTH, N_BRANCH * D_MODEL)
LOG2E = 1.4426950408889634
NSA_SLOPES = tuple(LOG2E * 2.0 ** (-8.0 * (h + 1) / NSA_HEADS) for h in range(NSA_HEADS))
MASK_BIAS = -(2.0 ** 30)
SEL_FEAT = 64
SEL_GROUPS = 8
SLOPE_PARTS = 3

LANES = 128
SUBLANES = 8
VMEM_LIMIT = 56 * 1024 * 1024
PAGES_PER_STEP = 64
MLSTM_SEQS = 1
MLSTM_CHUNK = 256
GATE_COLS = LANES
NG_OFF = 2 * M_HEADS
NSA_QPAD = NSA_HEADS * LANES
MOE_BLOCK = 1024
MOE_CAP = 288
NO_SLOT = -2
ROUTE_ROWS = 16

F32 = jnp.float32
BF16 = jnp.bfloat16
NEG_INF = float("-inf")
HIGHEST = lax.Precision.HIGHEST


def _dot(a, b, precision=None):
    return jnp.dot(a, b, preferred_element_type=F32, precision=precision)


def _dot_nt(a, b, precision=None):
    return lax.dot_general(a, b, (((1,), (1,)), ((), ())), preferred_element_type=F32, precision=precision)


def _dot_tn(a, b):
    return lax.dot_general(a, b, (((0,), (0,)), ((), ())), preferred_element_type=F32)


def _params(sem):
    return pltpu.CompilerParams(dimension_semantics=sem, vmem_limit_bytes=VMEM_LIMIT)


def _masked_softmax2(s, mask):
    s = jnp.where(mask, s, NEG_INF)
    mx = jnp.max(s, axis=-1, keepdims=True)
    mx = jnp.where(mx > NEG_INF, mx, 0.0)
    p = jnp.where(mask, jnp.exp2(s - mx), 0.0)
    return p / jnp.maximum(jnp.sum(p, axis=-1, keepdims=True), 1e-30)


def _layer_norm(xf, g, b):
    mu = jnp.mean(xf, axis=-1, keepdims=True)
    var = jnp.mean(jnp.square(xf - mu), axis=-1, keepdims=True)
    return (xf - mu) * lax.rsqrt(var + LN_EPS) * g + b


def _row_tile(n, pref):
    return pref if n % pref == 0 else n


def _in_proj_kernel(x_ref, wa_ref, wb_ref, wc_ref, wd_ref, ba_ref, bb_ref, bc_ref, bd_ref,
                    oa_ref, ob_ref, oc_ref, ocb_ref, od_ref):
    x = x_ref[...].astype(BF16)
    oa_ref[...] = (_dot(x, wa_ref[...]) + ba_ref[...]).astype(BF16)
    ob_ref[...] = (_dot(x, wb_ref[...]) + bb_ref[...]).astype(BF16)
    c = _dot(x, wc_ref[...]) + bc_ref[...]
    oc_ref[...] = c
    ocb_ref[...] = c.astype(BF16)
    od_ref[...] = _dot(x, wd_ref[...]) + bd_ref[...]


def _group_select():
    return jax.nn.one_hot(np.arange(NSA_HEADS) // NSA_HPG, NSA_GROUPS, dtype=F32)


def _pad_branch_rows(w_nsa):
    return jnp.einsum("hem,hg->hgem", w_nsa.reshape(NSA_HEADS, NSA_DH, -1), _group_select()).reshape(NSA_QPAD, -1)


def _split_in_proj(w_in_l, b_in_l):
    offs = np.cumsum((0,) + IN_SPLITS)
    w = [w_in_l[:, offs[i]:offs[i + 1]] for i in range(len(IN_SPLITS))]
    b = [b_in_l[offs[i]:offs[i + 1]] for i in range(len(IN_SPLITS))]
    mq, mk, mv, mi, mf, mo, nq, nkv, ng, xq, mg = range(11)
    pad = GATE_COLS - 2 * M_HEADS - 3 * NSA_HEADS

    def cat(ids, zpad=0):
        ww = jnp.concatenate([w[i] for i in ids], axis=1)
        bb = jnp.concatenate([b[i] for i in ids])
        if zpad:
            ww = jnp.pad(ww, ((0, 0), (0, zpad)))
            bb = jnp.pad(bb, (0, zpad))
        return ww.astype(BF16), bb.reshape(1, -1).astype(F32)

    scale = NSA_DH ** -0.5 * LOG2E
    w[nq] = jnp.einsum("dhe,hg->dhge", w[nq].reshape(D_MODEL, NSA_HEADS, NSA_DH) * scale, _group_select()).reshape(D_MODEL, NSA_QPAD)
    b[nq] = jnp.einsum("he,hg->hge", b[nq].reshape(NSA_HEADS, NSA_DH) * scale, _group_select()).reshape(NSA_QPAD)
    return dict(a=cat([mq, mk, mv, mo]), b=cat([nq, xq]), c=cat([nkv]), d=cat([mi, mf, ng], pad), mg=cat([mg]))


def in_proj(x2d, wp, tm):
    n = x2d.shape[0]
    (wa, ba), (wb, bb), (wc, bc), (wd, bd) = wp["a"], wp["b"], wp["c"], wp["d"]
    full = lambda arr: pl.BlockSpec(arr.shape, lambda i: (0, 0), pipeline_mode=pl.Buffered(1))
    row = lambda w: pl.BlockSpec((tm, w), lambda i: (i, 0))
    return pl.pallas_call(
        _in_proj_kernel,
        grid=(n // tm,),
        in_specs=[row(D_MODEL), full(wa), full(wb), full(wc), full(wd), full(ba), full(bb), full(bc), full(bd)],
        out_specs=[row(wa.shape[1]), row(wb.shape[1]), row(wc.shape[1]), row(wc.shape[1]), row(wd.shape[1])],
        out_shape=[jax.ShapeDtypeStruct((n, wa.shape[1]), BF16), jax.ShapeDtypeStruct((n, wb.shape[1]), BF16),
                   jax.ShapeDtypeStruct((n, wc.shape[1]), F32), jax.ShapeDtypeStruct((n, wc.shape[1]), BF16),
                   jax.ShapeDtypeStruct((n, wd.shape[1]), F32)],
        compiler_params=_params(("parallel",)),
        name="in_proj",
    )(x2d, wa, wb, wc, wd, ba, bb, bc, bd)


def _proj_kernel(x_ref, w_ref, o_ref):
    o_ref[...] = _dot(x_ref[...].astype(BF16), w_ref[...])


def proj(x2d, w_bf16, tm):
    n, k = x2d.shape
    m = w_bf16.shape[1]
    return pl.pallas_call(
        _proj_kernel,
        grid=(n // tm,),
        in_specs=[pl.BlockSpec((tm, k), lambda i: (i, 0)), pl.BlockSpec((k, m), lambda i: (0, 0))],
        out_specs=pl.BlockSpec((tm, m), lambda i: (i, 0)),
        out_shape=jax.ShapeDtypeStruct((n, m), F32),
        compiler_params=_params(("parallel",)),
        name="mem_proj",
    )(x2d, w_bf16)


def _log_sigmoid(x):
    return jnp.minimum(x, 0.0) - jnp.log1p(jnp.exp(-jnp.abs(x)))


def _mlstm_kernel(q_ref, k_ref, v_ref, og_ref, gc_ref, gr_ref, c0_ref, n0_ref, m0_ref, ng_ref,
                  h_ref, ct_ref, nt_ref, mt_ref, c_s, n_s, m_s, *, L, t_valid, nb):
    ci = pl.program_id(1)

    @pl.when(ci == 0)
    def _():
        c_s[...] = c0_ref[...]
        n_s[...] = n0_ref[...]
        m_s[...] = m0_ref[...]

    row = lax.broadcasted_iota(jnp.int32, (L, L), 0)
    col = lax.broadcasted_iota(jnp.int32, (L, L), 1)
    causal = row >= col
    tri = causal.astype(F32)
    tri_t = (row <= col).astype(F32)
    rvalid = lax.broadcasted_iota(jnp.int32, (L, GATE_COLS), 0) < t_valid
    cvalid = lax.broadcasted_iota(jnp.int32, (2 * M_HEADS, L), 1) < t_valid
    scale = M_DH ** -0.5

    for bb, h in [(bb, h) for bb in range(nb) for h in range(M_HEADS)]:
        if h == 0:
            gc = gc_ref[bb]
            gr = gr_ref[bb]
            lf_c = jnp.where(rvalid, _log_sigmoid(gc), 0.0)
            lf_r = jnp.where(cvalid, _log_sigmoid(gr), 0.0)
            ig_c = jnp.where(rvalid, gc, NEG_INF)
            ig_r = jnp.where(cvalid, gr, NEG_INF)
            b_c = _dot(tri, lf_c, HIGHEST)
            b_r = _dot(lf_r, tri_t, HIGHEST)
        sl = slice(h * M_DH, (h + 1) * M_DH)
        q = q_ref[bb, :, sl]
        k = k_ref[bb, :, sl]
        v = v_ref[bb, :, sl]
        bc = b_c[:, M_HEADS + h:M_HEADS + h + 1]
        br = b_r[M_HEADS + h:M_HEADS + h + 1, :]
        igc = ig_c[:, h:h + 1]
        igr = ig_r[h:h + 1, :]
        m_prev = m_s[bb, h, 0:1, 0:1]
        c_prev = c_s[bb, h]
        n_prev = n_s[bb, h:h + 1, :]

        dmat = jnp.where(causal, bc - br + igr, NEG_INF)
        inter = bc + m_prev
        m_t = jnp.maximum(inter, jnp.max(dmat, axis=-1, keepdims=True))
        s = _dot_nt(q, k) * scale * jnp.exp(dmat - m_t)
        sc_in = jnp.exp(inter - m_t)
        qf = q.astype(F32)
        num = sc_in * _dot_nt(q, c_prev.astype(BF16)) + _dot(s.astype(BF16), v)
        den = sc_in * jnp.sum(qf * n_prev, axis=-1, keepdims=True) + jnp.sum(s, axis=-1, keepdims=True)
        hh = num / jnp.maximum(jnp.abs(den), jnp.exp(-m_t))

        b_last = bc[L - 1:L, :]
        dec_c = b_last - bc + igc
        dec_r = b_last - br + igr
        m_new = jnp.maximum(b_last + m_prev, jnp.max(dec_r, axis=-1, keepdims=True))
        ws_c = jnp.exp(dec_c - m_new) * scale
        sc = jnp.exp(b_last + m_prev - m_new)
        vf = v.astype(F32)
        kf = k.astype(F32)
        c_new = sc * c_prev + _dot_tn((vf * ws_c).astype(BF16), k)
        n_new = sc * n_prev + jnp.sum(kf * ws_c, axis=0, keepdims=True)
        c_s[bb, h] = c_new
        n_s[bb, h:h + 1, :] = n_new
        m_s[bb, h] = jnp.broadcast_to(m_new, m_s.shape[2:])

        og = og_ref[bb, :, sl].astype(F32)
        hh = hh * jax.nn.sigmoid(og)
        mu = jnp.mean(hh, axis=-1, keepdims=True)
        var = jnp.mean(jnp.square(hh - mu), axis=-1, keepdims=True)
        hn = (hh - mu) * lax.rsqrt(var + LN_EPS) * ng_ref[:, sl]
        h_ref[bb, :, sl] = hn.astype(h_ref.dtype)

    ct_ref[...] = c_s[...]
    nt_ref[...] = n_s[...]
    mt_ref[...] = m_s[...]


def mlstm(za, zd, c0, n0, m0, norm_g, *, L, t_valid):
    B, T, _ = za.shape
    nc = T // L
    gr = jnp.swapaxes(zd[:, :, :2 * M_HEADS], 1, 2)
    m0b = jnp.broadcast_to(m0.astype(F32)[:, :, None, None], (B, M_HEADS, SUBLANES, LANES))
    nb = MLSTM_SEQS if B % MLSTM_SEQS == 0 else 1
    colspec = lambda j: pl.BlockSpec((nb, L, M_WIDTH), lambda b, c, j=j: (b, c, j))
    st = lambda shp: pl.BlockSpec((nb,) + shp, lambda b, c: (b,) + (0,) * len(shp))
    kern = functools.partial(_mlstm_kernel, L=L, t_valid=t_valid, nb=nb)
    h, ct, nt, mt = pl.pallas_call(
        kern,
        grid=(B // nb, nc),
        in_specs=[colspec(0), colspec(1), colspec(2), colspec(3),
                  pl.BlockSpec((nb, L, GATE_COLS), lambda b, c: (b, c, 0)),
                  pl.BlockSpec((nb, 2 * M_HEADS, L), lambda b, c: (b, 0, c)),
                  st((M_HEADS, M_DH, M_DH)), st((M_HEADS, M_DH)), st((M_HEADS, SUBLANES, LANES)),
                  pl.BlockSpec((1, M_WIDTH), lambda b, c: (0, 0))],
        out_specs=[pl.BlockSpec((nb, L, M_WIDTH), lambda b, c: (b, c, 0)),
                   st((M_HEADS, M_DH, M_DH)), st((M_HEADS, M_DH)), st((M_HEADS, SUBLANES, LANES))],
        out_shape=[jax.ShapeDtypeStruct((B, T, M_WIDTH), BF16),
                   jax.ShapeDtypeStruct((B, M_HEADS, M_DH, M_DH), F32),
                   jax.ShapeDtypeStruct((B, M_HEADS, M_DH), F32),
                   jax.ShapeDtypeStruct((B, M_HEADS, SUBLANES, LANES), F32)],
        scratch_shapes=[pltpu.VMEM((nb, M_HEADS, M_DH, M_DH), F32), pltpu.VMEM((nb, M_HEADS, M_DH), F32),
                        pltpu.VMEM((nb, M_HEADS, SUBLANES, LANES), F32)],
        compiler_params=_params(("parallel", "arbitrary")),
        name="mlstm",
    )(za, za, za, za, zd, gr, c0.astype(F32), n0.astype(F32), m0b, norm_g.reshape(1, M_WIDTH).astype(F32))
    return h, ct, nt, mt[:, :, 0, 0]


def _mem_attn_kernel(q_ref, k_ref, v_ref, o_ref):
    scale = X_DH ** -0.5
    for h in range(X_HEADS):
        sl = slice(h * X_DH, (h + 1) * X_DH)
        k = k_ref[:, sl] if len(k_ref.shape) == 2 else k_ref[:, h, :]
        v = v_ref[:, sl] if len(v_ref.shape) == 2 else v_ref[:, h, :]
        s = _dot_nt(q_ref[0, :, sl], k.astype(BF16)) * scale
        mx = jnp.max(s, axis=-1, keepdims=True)
        p = jnp.exp(s - mx)
        p = p / jnp.sum(p, axis=-1, keepdims=True)
        o_ref[0, :, sl] = _dot(p.astype(BF16), v.astype(BF16)).astype(o_ref.dtype)


def mem_attn(zb, kv, layer, tq):
    B, T, _ = zb.shape
    if kv.ndim == 3:
        kv_specs = [pl.BlockSpec((None, MEM_LEN, X_WIDTH), lambda b, i, j=j: (b, 0, j)) for j in range(2)]
    else:
        kv_blk = (None, MEM_LEN, None, None, X_HEADS, X_DH)
        kv_specs = [pl.BlockSpec(kv_blk, lambda b, i, j=j: (b, 0, layer, j, 0, 0)) for j in range(2)]
    return pl.pallas_call(
        _mem_attn_kernel,
        grid=(B, T // tq),
        in_specs=[pl.BlockSpec((1, tq, X_WIDTH), lambda b, i: (b, i, NSA_QPAD // X_WIDTH))] + kv_specs,
        out_specs=pl.BlockSpec((1, tq, X_WIDTH), lambda b, i: (b, i, 0)),
        out_shape=jax.ShapeDtypeStruct((B, T, X_WIDTH), BF16),
        compiler_params=_params(("parallel", "parallel")),
        name="mem_attn",
    )(zb, kv, kv)


def _head_rows(q_ref, nq):
    parts = [q_ref[0, :, hd * LANES:(hd + 1) * LANES] for hd in range(NSA_HEADS)]
    if nq % 16:
        return jnp.concatenate([p.astype(F32) for p in parts], axis=0).astype(BF16)
    return jnp.concatenate(parts, axis=0)


def _store_heads(o_ref, gate, o_c, o_s, o_w, nq):
    lane = lax.broadcasted_iota(jnp.int32, (nq, LANES), 1)
    for hd in range(NSA_HEADS):
        r = slice(hd * nq, (hd + 1) * nq)
        gcol = lambda br: gate[:, NG_OFF + br * NSA_HEADS + hd:NG_OFF + br * NSA_HEADS + hd + 1]
        val = gcol(0) * o_c[r] + gcol(1) * o_s[r] + gcol(2) * o_w[r]
        keep = (lane >= NSA_DH) if hd // NSA_HPG == 1 else (lane < NSA_DH)
        o_ref[0, :, hd * LANES:(hd + 1) * LANES] = jnp.where(keep, val, 0.0).astype(o_ref.dtype)


def _overlap(tok, blk):
    c_start = tok * CMP_STRIDE
    s_start = blk * SEL_LEN
    return ((c_start < s_start + SEL_LEN) & (c_start + CMP_LEN > s_start)).astype(F32)


def _select_blocks(imp, tpos, n_sel, n_top):
    nq, W = imp.shape
    blk = lax.broadcasted_iota(jnp.int32, (nq, W), 1)
    cur = tpos // SEL_LEN
    forced = (blk == 0) | (blk == cur) | (blk == cur - 1)
    v = jnp.where(forced, jnp.inf, jnp.where(blk <= cur, imp, NEG_INF))
    v = jnp.where(blk < n_sel, v, NEG_INF)
    ahead = jnp.zeros((nq, W), F32)
    for j in range(n_sel):
        vj = v[:, j:j + 1]
        ahead = ahead + jnp.where(vj > v, 1.0, jnp.where(vj == v, jnp.where(blk > j, 1.0, 0.0), 0.0))
    return jnp.where((ahead < n_top) & (blk < n_sel), 1.0, 0.0)


def _select_blocks_t(imp_t, tpos_row, n_sel, n_top):
    n_blk, nq = imp_t.shape
    blk = lax.broadcasted_iota(jnp.int32, (n_blk, nq), 0)
    cur = tpos_row // SEL_LEN
    forced = (blk == 0) | (blk == cur) | (blk == cur - 1)
    v = jnp.where(forced, jnp.inf, jnp.where(blk <= cur, imp_t, NEG_INF))
    v = jnp.where(blk < n_sel, v, NEG_INF)
    n_rg = n_blk // SUBLANES
    vg = [v[rg * SUBLANES:(rg + 1) * SUBLANES] for rg in range(n_rg)]
    bg = [rg * SUBLANES + lax.broadcasted_iota(jnp.int32, (SUBLANES, nq), 0) for rg in range(n_rg)]

    ahead = [jnp.zeros((SUBLANES, nq), F32) for _ in range(n_rg)]
    for j in range(n_sel):
        vj = v[j:j + 1, :]
        for rg in range(n_rg):
            if rg * SUBLANES > j:
                inc = jnp.where(vj >= vg[rg], 1.0, 0.0)
            elif (rg + 1) * SUBLANES - 1 <= j:
                inc = jnp.where(vj > vg[rg], 1.0, 0.0)
            else:
                inc = jnp.where(vj > vg[rg], 1.0, jnp.where(vj == vg[rg], jnp.where(bg[rg] > j, 1.0, 0.0), 0.0))
            ahead[rg] = ahead[rg] + inc
    ahead = jnp.concatenate(ahead, axis=0)
    return jnp.where((ahead < n_top) & (blk < n_sel), 1.0, 0.0)


def _online_update(s, m_old, l_old):
    m_new = jnp.maximum(m_old, jnp.max(s, axis=-1, keepdims=True))
    alpha = jnp.exp2(m_old - m_new)
    p = jnp.exp2(s - m_new)
    return p, m_new, alpha, alpha * l_old + jnp.sum(p, axis=-1, keepdims=True)


def _cmp_tokens_kernel(xk_ref, xv_ref, bd_ref, b_ref, kc_ref, vc_ref, *, nsub):
    for c, (x_ref, o_ref) in enumerate(((xk_ref, kc_ref), (xv_ref, vc_ref))):
        a0 = jnp.zeros((nsub, LANES), F32)
        a1 = jnp.zeros((nsub, LANES), F32)
        for j in range(CMP_STRIDE):
            xj = x_ref[0, :, j, :].astype(BF16)
            a0 = a0 + _dot(xj, bd_ref[c, 0, j])
            a1 = a1 + _dot(xj, bd_ref[c, 1, j])
        tok = a0 + pltpu.roll(a1, nsub - 1, axis=0) + b_ref[c]
        o_ref[0] = tok.astype(o_ref.dtype)


def _cmp_blockdiag(cmp_w_l):
    R = CMP_LEN // CMP_STRIDE
    w = cmp_w_l.astype(F32).reshape(2, R, CMP_STRIDE, NSA_DH, NSA_DH)
    eye = jnp.eye(NSA_GROUPS, dtype=F32)
    return jnp.einsum("ab,crjde->crjadbe", eye, w).reshape(2, R, CMP_STRIDE, LANES, LANES).astype(BF16)


def _cmp_bias(cmp_b_l):
    return jnp.tile(cmp_b_l.astype(F32), (1, NSA_GROUPS)).reshape(2, 1, LANES)


def cmp_tokens(zc, cmp_w_l, cmp_b_l):
    B, T, W = zc.shape
    nsub = T // CMP_STRIDE
    x4 = zc.reshape(B, nsub, CMP_STRIDE, W)
    bd = _cmp_blockdiag(cmp_w_l)
    bias = _cmp_bias(cmp_b_l)
    spec = lambda j: pl.BlockSpec((1, nsub, CMP_STRIDE, LANES), lambda b, j=j: (b, 0, 0, j))
    return pl.pallas_call(
        functools.partial(_cmp_tokens_kernel, nsub=nsub),
        grid=(B,),
        in_specs=[spec(0), spec(1), pl.BlockSpec(bd.shape, lambda b: (0,) * 5), pl.BlockSpec(bias.shape, lambda b: (0, 0, 0))],
        out_specs=[pl.BlockSpec((1, nsub, LANES), lambda b: (b, 0, 0))] * 2,
        out_shape=[jax.ShapeDtypeStruct((B, nsub, LANES), BF16)] * 2,
        compiler_params=_params(("parallel",)),
        name="nsa_cmp_tokens",
    )(x4, x4, bd, bias)


def _nsa_prompt_kernel(q_ref, g_ref, kca_ref, vc_ref, ka_ref, vst_ref, kwa_ref, vw_ref, qf_ref, o_ref,
                       qaug_s, s_buf, p_buf, w_buf, pw_buf, pc_s, m_s, l_s, a_s, acc_s, oc_s, ow_s, *, tq, T, CH):
    start = pl.program_id(1) * tq
    n_cmp_rows = kca_ref.shape[1]
    n_sel = T // SEL_LEN
    n_top = min(SEL_TOPN, n_sel)
    n_blk = -(-n_sel // SUBLANES) * SUBLANES
    nq_all = NSA_HEADS * tq
    qw = nq_all // SEL_GROUPS
    tpos_row = start + lax.broadcasted_iota(jnp.int32, (1, tq), 1)
    tpos_all = jnp.concatenate([tpos_row] * NSA_HEADS, axis=1)
    qcols = lambda qi: slice(qi * qw, (qi + 1) * qw)
    tiles = [(qi, ct, slice(qi * qw + ct * LANES, qi * qw + (ct + 1) * LANES), slice(ct * LANES, (ct + 1) * LANES))
             for qi in range(SEL_GROUPS) for ct in range(qw // LANES)]

    qaug_s[:, 0:LANES] = _head_rows(q_ref, tq)
    for hd in range(NSA_HEADS):
        qaug_s[hd * tq:(hd + 1) * tq, LANES:2 * LANES] = jnp.broadcast_to(qf_ref[hd:hd + 1, :], (tq, LANES)).astype(BF16)

    def softmax_tile(s_t):
        mx = jnp.max(s_t, axis=0, keepdims=True)
        mx = jnp.where(mx > NEG_INF, mx, 0.0)
        p_t = jnp.exp2(s_t - mx)
        return p_t, jnp.maximum(jnp.sum(p_t, axis=0, keepdims=True), 1e-30)

    c_end = lax.broadcasted_iota(jnp.int32, (n_cmp_rows, 1), 0) * CMP_STRIDE + CMP_LEN - 1
    for qi in range(SEL_GROUPS):
        s_buf[qi, 0:n_cmp_rows, :] = _dot_nt(kca_ref[0], qaug_s[qcols(qi), :])
    for qi, ct, cols, tc in tiles:
        p_t, den = softmax_tile(jnp.where(c_end <= tpos_all[:, cols], s_buf[qi, 0:n_cmp_rows, tc], NEG_INF))
        pc_s[:, cols] = p_t / den
    oc_s[...] = _dot_tn(vc_ref[0], pc_s[...].astype(BF16))
    ov_t = _overlap(lax.broadcasted_iota(jnp.int32, (n_blk, n_cmp_rows), 1),
                    lax.broadcasted_iota(jnp.int32, (n_blk, n_cmp_rows), 0))
    lane = lax.broadcasted_iota(jnp.int32, (tq, LANES), 1)
    feat_sel = [None] * NSA_HEADS
    for g in range(NSA_GROUPS):
        pg = pc_s[:, g * NSA_HPG * tq:(g * NSA_HPG + 1) * tq]
        for p in range(1, NSA_HPG):
            pg = pg + pc_s[:, (g * NSA_HPG + p) * tq:(g * NSA_HPG + p + 1) * tq]
        sel_t = _select_blocks_t(_dot(ov_t, pg, HIGHEST), tpos_row, n_sel, n_top)
        sel = jnp.concatenate([sel_t, jnp.zeros((LANES - n_blk, tq), F32)], axis=0).T
        unsel = jnp.where(lane < SEL_FEAT, (1.0 - sel) * MASK_BIAS, 0.0)
        for p in range(NSA_HPG):
            hd = g * NSA_HPG + p
            feat_sel[hd] = (unsel + qf_ref[hd:hd + 1, :]).astype(BF16)

    wk = WINDOW + tq
    base_w = pl.multiple_of(start, tq)
    wpos = start - WINDOW + lax.broadcasted_iota(jnp.int32, (wk, 1), 0)
    for qi in range(SEL_GROUPS):
        w_buf[qi] = _dot_nt(kwa_ref[0, pl.ds(base_w, wk), :], qaug_s[qcols(qi), :])
    for qi, ct, cols, tc in tiles:
        tp = tpos_all[:, cols]
        first = jnp.maximum(tp - (WINDOW - 1), 0)
        s_t = jnp.where(wpos >= first, jnp.where(wpos <= tp, w_buf[qi, :, tc], NEG_INF), NEG_INF)
        p_t, den = softmax_tile(s_t)
        pw_buf[qi, :, tc] = p_t.astype(BF16)
        a_s[:, cols] = den
    for qi in range(SEL_GROUPS):
        ow_s[:, qcols(qi)] = _dot_tn(vw_ref[0, pl.ds(base_w, wk), :], pw_buf[qi]) / a_s[:, qcols(qi)]

    for hd in range(NSA_HEADS):
        qaug_s[hd * tq:(hd + 1) * tq, LANES:2 * LANES] = feat_sel[hd]

    m_s[...] = jnp.full(m_s.shape, NEG_INF, F32)
    l_s[...] = jnp.zeros(l_s.shape, F32)
    acc_s[...] = jnp.zeros(acc_s.shape, F32)

    def issue(c, qi):
        base = pl.multiple_of(c * CH, CH)
        s_buf[qi] = _dot_nt(ka_ref[0, pl.ds(base, CH), :], qaug_s[qi * qw:(qi + 1) * qw, :])

    def absorb(c, qi, diag):
        for ct in range(qw // LANES):
            cols = slice(qi * qw + ct * LANES, qi * qw + (ct + 1) * LANES)
            s_t = s_buf[qi, :, ct * LANES:(ct + 1) * LANES]
            if diag:
                kpos = c * CH + lax.broadcasted_iota(jnp.int32, (CH, 1), 0)
                s_t = jnp.where(kpos <= tpos_all[:, cols], s_t, NEG_INF)
            m_old = m_s[:, cols]
            m_new = jnp.maximum(m_old, jnp.max(s_t, axis=0, keepdims=True))
            alpha = jnp.exp2(m_old - m_new)
            p_t = jnp.exp2(s_t - m_new)
            m_s[:, cols] = m_new
            a_s[:, cols] = alpha
            l_s[:, cols] = alpha * l_s[:, cols] + jnp.sum(p_t, axis=0, keepdims=True)
            p_buf[qi, :, ct * LANES:(ct + 1) * LANES] = p_t.astype(BF16)
        cols = slice(qi * qw, (qi + 1) * qw)
        acc_s[:, cols] = a_s[:, cols] * acc_s[:, cols] + _dot(vst_ref[0, c], p_buf[qi])

    def full_chunk(c, carry):
        for qi in range(SEL_GROUPS):
            absorb(c, qi, False)
            issue(c + 1, qi)
        return carry

    n_full = start // CH
    for qi in range(SEL_GROUPS):
        issue(0, qi)
    lax.fori_loop(0, n_full, full_chunk, 0)
    for qi in range(SEL_GROUPS):
        absorb(n_full, qi, True)

    gate_t = jax.nn.sigmoid(g_ref[0]).T
    for hd in range(NSA_HEADS):
        cols = slice(hd * tq, (hd + 1) * tq)
        grow = lambda br: gate_t[NG_OFF + br * NSA_HEADS + hd:NG_OFF + br * NSA_HEADS + hd + 1, :]
        o_s = acc_s[:, cols] / jnp.maximum(l_s[:, cols], 1e-30)
        val = (grow(0) * oc_s[:, cols] + grow(1) * o_s + grow(2) * ow_s[:, cols]).T
        keep = (lane >= NSA_DH) if hd // NSA_HPG == 1 else (lane < NSA_DH)
        o_ref[0, :, hd * LANES:(hd + 1) * LANES] = jnp.where(keep, val, 0.0).astype(o_ref.dtype)


def _bf16_parts(x, n):
    parts = []
    for _ in range(n):
        p = float(np.asarray(x, np.float32).astype(jnp.bfloat16).astype(np.float32))
        parts.append(p)
        x = x - p
    return parts


def _slope_features():
    qf = np.zeros((NSA_HEADS, LANES), np.float32)
    for hd in range(NSA_HEADS):
        for p, s_p in enumerate(_bf16_parts(NSA_SLOPES[hd], SLOPE_PARTS)):
            qf[hd, SEL_FEAT + 2 * p] = s_p * SEL_LEN
            qf[hd, SEL_FEAT + 2 * p + 1] = s_p
    return jnp.asarray(qf)


def _key_features(pos, block_onehot):
    pos = np.asarray(pos)
    ok = pos >= 0
    kf = np.zeros((pos.shape[0], LANES), np.float32)
    if block_onehot:
        kf[np.arange(pos.shape[0])[ok], pos[ok] // SEL_LEN] = 1.0
    for p in range(SLOPE_PARTS):
        kf[ok, SEL_FEAT + 2 * p] = pos[ok] // SEL_LEN
        kf[ok, SEL_FEAT + 2 * p + 1] = pos[ok] % SEL_LEN
    return jnp.asarray(kf, BF16)


def nsa_prompt(zb, zd, zcb, kc, vc, tq, CH):
    B, T, _ = zb.shape
    nsub = kc.shape[1]
    assert T % CH == 0 and CH % tq == 0 and T // SEL_LEN <= SEL_FEAT and nsub <= CH
    qf = _slope_features()
    with_feats = lambda k, feats: jnp.concatenate([k, jnp.broadcast_to(feats[None], (B,) + feats.shape)], axis=2)
    k_aug = with_feats(zcb[:, :, 2 * NSA_KV:3 * NSA_KV], _key_features(np.arange(T), True))
    kc_aug = with_feats(kc, _key_features(np.arange(nsub) * CMP_STRIDE + CMP_LEN - 1, False))
    kw = jnp.pad(zcb[:, :, 4 * NSA_KV:5 * NSA_KV], ((0, 0), (WINDOW, 0), (0, 0)))
    kw_aug = with_feats(kw, _key_features(np.arange(T + WINDOW) - WINDOW, False))
    vs_t = jnp.swapaxes(zcb[:, :, 3 * NSA_KV:4 * NSA_KV].reshape(B, T // CH, CH, NSA_KV), 2, 3)
    vw = jnp.pad(zcb[:, :, 5 * NSA_KV:6 * NSA_KV], ((0, 0), (WINDOW, 0), (0, 0)))
    per_b = lambda rows_, w: pl.BlockSpec((1, rows_, w), lambda b, i: (b, 0, 0))
    kern = functools.partial(_nsa_prompt_kernel, tq=tq, T=T, CH=CH)
    nq_all = NSA_HEADS * tq
    qw = nq_all // SEL_GROUPS
    wk = WINDOW + tq
    assert qw % LANES == 0
    return pl.pallas_call(
        kern,
        grid=(B, T // tq),
        in_specs=[pl.BlockSpec((1, tq, NSA_QPAD), lambda b, i: (b, i, 0)),
                  pl.BlockSpec((1, tq, GATE_COLS), lambda b, i: (b, i, 0)),
                  per_b(nsub, 2 * LANES), per_b(nsub, LANES),
                  per_b(T, 2 * LANES),
                  pl.BlockSpec((1, T // CH, NSA_KV, CH), lambda b, i: (b, 0, 0, 0)),
                  per_b(T + WINDOW, 2 * LANES), per_b(T + WINDOW, LANES),
                  pl.BlockSpec(qf.shape, lambda b, i: (0, 0))],
        out_specs=pl.BlockSpec((1, tq, NSA_QPAD), lambda b, i: (b, i, 0)),
        out_shape=jax.ShapeDtypeStruct((B, T, NSA_QPAD), BF16),
        scratch_shapes=[pltpu.VMEM((nq_all, 2 * LANES), BF16),
                        pltpu.VMEM((SEL_GROUPS, CH, qw), F32), pltpu.VMEM((SEL_GROUPS, CH, qw), BF16),
                        pltpu.VMEM((SEL_GROUPS, wk, qw), F32), pltpu.VMEM((SEL_GROUPS, wk, qw), BF16),
                        pltpu.VMEM((nsub, nq_all), F32),
                        pltpu.VMEM((1, nq_all), F32), pltpu.VMEM((1, nq_all), F32), pltpu.VMEM((1, nq_all), F32),
                        pltpu.VMEM((NSA_KV, nq_all), F32), pltpu.VMEM((NSA_KV, nq_all), F32),
                        pltpu.VMEM((NSA_KV, nq_all), F32)],
        compiler_params=_params(("parallel", "arbitrary")),
        name="nsa_prompt",
    )(zb, zd, kc_aug, vc, k_aug, vs_t, kw_aug, vw, qf)


def _page_specs(shape, slot_blk, layer, n_pages):
    def mk(i):
        def imap(b, s, pt):
            return (pt[b * n_pages + s * PAGES_PER_STEP + i], layer, slot_blk, 0, 0)
        return pl.BlockSpec(shape, imap)
    return [mk(i) for i in range(PAGES_PER_STEP)]


def _nsa_dec_cmp_kernel(pt_ref, *refs, T, past_len, n_rows):
    pages = refs[:PAGES_PER_STEP]
    perm_ref, bd_ref, b_ref, q_ref, oc_ref, sel_ref, tok_s, pend_s = refs[PAGES_PER_STEP:]
    s = pl.program_id(1)
    n_steps = pl.num_programs(1)
    page_rows = pages[0].shape[-1]
    sub = page_rows // CMP_STRIDE
    R = PAGES_PER_STEP * sub

    @pl.when(s == 0)
    def _():
        pend_s[...] = jnp.zeros(pend_s.shape, F32)

    rid = lax.broadcasted_iota(jnp.int32, (R, LANES), 0)
    for c in range(2):
        xs = [_dot_nt(perm_ref[...], pg[0, 0, c].astype(BF16)) for pg in pages]
        a0 = jnp.zeros((R, LANES), F32)
        a1 = jnp.zeros((R, LANES), F32)
        for j in range(CMP_STRIDE):
            xj = jnp.concatenate([x[j * sub:(j + 1) * sub] for x in xs], axis=0).astype(BF16)
            a0 = a0 + _dot(xj, bd_ref[c, 0, j])
            a1 = a1 + _dot(xj, bd_ref[c, 1, j])
        a0 = a0 + b_ref[c]
        tok = jnp.where(rid == 0, pend_s[c], pltpu.roll(a0, 1, axis=0)) + a1
        tok_s[c, pl.ds(pl.multiple_of(s * R, R), R), :] = tok
        pend_s[c] = a0[R - 1:R, :]

    @pl.when(s == n_steps - 1)
    def _():
        qp = _head_rows(q_ref, T)
        kc = tok_s[0].astype(BF16)
        vc = tok_s[1].astype(BF16)
        sc_all = _dot_nt(qp, kc)
        tpos = past_len + lax.broadcasted_iota(jnp.int32, (T, 1), 0)
        r = lax.broadcasted_iota(jnp.int32, (T, n_rows), 1)
        dc = tpos - ((r - 1) * CMP_STRIDE + CMP_LEN - 1)
        mask_c = (dc >= 0) & (r >= 1)
        dcf = dc.astype(F32)
        pcs = [_masked_softmax2(sc_all[hd * T:(hd + 1) * T] - NSA_SLOPES[hd] * dcf, mask_c) for hd in range(NSA_HEADS)]
        oc_ref[0] = _dot(jnp.concatenate(pcs, axis=0).astype(BF16), vc)
        W = sel_ref.shape[2]
        n_sel = -(-(past_len + T) // SEL_LEN)
        ov = _overlap(lax.broadcasted_iota(jnp.int32, (n_rows, W), 0) - 1, lax.broadcasted_iota(jnp.int32, (n_rows, W), 1))
        for g in range(NSA_GROUPS):
            pg = pcs[g * NSA_HPG]
            for p in range(1, NSA_HPG):
                pg = pg + pcs[g * NSA_HPG + p]
            imp = _dot(pg, ov, HIGHEST)
            sel_ref[0, g * T:(g + 1) * T, :] = _select_blocks(imp, tpos, n_sel, min(SEL_TOPN, n_sel))


def _nsa_dec_sel_kernel(pt_ref, *refs, T, past_len):
    pages = refs[:PAGES_PER_STEP]
    q_ref, sel_ref, sel_last_ref, ex_ref, kn_ref, vn_ref, os_ref, m_s, l_s, acc_s = refs[PAGES_PER_STEP:]
    s = pl.program_id(1)
    n_steps = pl.num_programs(1)
    page_rows = pages[0].shape[-1]
    CH = PAGES_PER_STEP * page_rows

    @pl.when(s == 0)
    def _():
        m_s[...] = jnp.full(m_s.shape, NEG_INF, F32)
        l_s[...] = jnp.zeros(l_s.shape, F32)
        acc_s[...] = jnp.zeros(acc_s.shape, F32)

    qp = _head_rows(q_ref, T)
    tpos = past_len + lax.broadcasted_iota(jnp.int32, (T, 1), 0)

    def update(s_all, mask_of_group, kpos, pv):
        ds = tpos - kpos
        dsf = ds.astype(F32)
        ps = []
        for g in range(NSA_GROUPS):
            mk = mask_of_group(g) & (ds >= 0)
            for p in range(NSA_HPG):
                hd = g * NSA_HPG + p
                r = slice(hd * T, (hd + 1) * T)
                sc = jnp.where(mk, s_all[r] - NSA_SLOPES[hd] * dsf, NEG_INF)
                pr, m_new, alpha, l_new = _online_update(sc, m_s[r], l_s[r])
                m_s[r] = m_new
                l_s[r] = l_new
                acc_s[r] = alpha * acc_s[r]
                ps.append(pr)
        acc_s[...] = acc_s[...] + pv(jnp.concatenate(ps, axis=0).astype(BF16))

    selk = _dot(sel_ref[0, 0].astype(BF16), ex_ref[...])
    s_all = jnp.concatenate([_dot(qp, pg[0, 0, 0].astype(BF16)) for pg in pages], axis=1)
    kpos = s * CH + lax.broadcasted_iota(jnp.int32, (T, CH), 1)

    def pv_pages(pmat):
        out = jnp.zeros((NSA_HEADS * T, LANES), F32)
        for i, pg in enumerate(pages):
            out = out + _dot_nt(pmat[:, i * page_rows:(i + 1) * page_rows], pg[0, 0, 1].astype(BF16))
        return out

    update(s_all, lambda g: selk[g * T:(g + 1) * T] > 0.5, kpos, pv_pages)

    @pl.when(s == n_steps - 1)
    def _():
        nk = kn_ref.shape[1]
        sn = _dot_nt(qp, kn_ref[0])
        lane = lax.broadcasted_iota(jnp.int32, (T, nk), 1)
        sl = sel_last_ref[0, 0]
        update(sn, lambda g: (sl[g * T:(g + 1) * T, 0:1] > 0.5) & (lane < T), past_len + lane,
               lambda pmat: _dot(pmat, vn_ref[0]))
        os_ref[0] = acc_s[...] / jnp.maximum(l_s[...], 1e-30)


def _nsa_dec_win_kernel(q_ref, g_ref, oc_ref, os_ref, wp_ref, kwn_ref, vwn_ref, o_ref, *, T, past_len):
    qp = _head_rows(q_ref, T)
    w_src = wp_ref.shape[-1]
    nk = kwn_ref.shape[1]
    sw_all = jnp.concatenate([_dot(qp, wp_ref[0, 0, 0].astype(BF16)), _dot_nt(qp, kwn_ref[0])], axis=1)
    j = lax.broadcasted_iota(jnp.int32, (T, w_src + nk), 1)
    tpos = past_len + lax.broadcasted_iota(jnp.int32, (T, 1), 0)
    wpos = past_len - w_src + j
    dw = tpos - wpos
    mask_w = (dw >= 0) & (dw < WINDOW) & (wpos >= 0) & (j < w_src + T)
    dwf = dw.astype(F32)
    pws = [_masked_softmax2(sw_all[hd * T:(hd + 1) * T] - NSA_SLOPES[hd] * dwf, mask_w) for hd in range(NSA_HEADS)]
    pw = jnp.concatenate(pws, axis=0).astype(BF16)
    o_w = _dot_nt(pw[:, :w_src], wp_ref[0, 0, 1].astype(BF16)) + _dot(pw[:, w_src:], vwn_ref[0])
    _store_heads(o_ref, jax.nn.sigmoid(g_ref[0]), oc_ref[0], os_ref[0], o_w, T)


def nsa_decode(zb, zd, zcb, cache_nsa_kv, cache_win_kv, page_table, cmp_w_l, cmp_b_l, layer):
    DB, T, _ = zb.shape
    n_pool, page_rows = cache_nsa_kv.shape[:2]
    n_pages = page_table.shape[1]
    past_len = n_pages * page_rows
    n_steps = n_pages // PAGES_PER_STEP
    sub_per_page = page_rows // CMP_STRIDE
    n_rows = n_pages * sub_per_page
    assert (n_rows - 1) * CMP_STRIDE + CMP_LEN - 1 > past_len + T - 1
    assert T <= CMP_STRIDE and n_pages % PAGES_PER_STEP == 0 and page_rows == LANES
    pt = page_table.reshape(-1).astype(jnp.int32)
    cache_t = jnp.transpose(cache_nsa_kv, (0, 2, 3, 4, 5, 1)).reshape(n_pool, DEPTH, 4, NSA_KV, page_rows)
    page_blk = (1, 1, 2, NSA_KV, page_rows)

    rr = np.arange(page_rows)
    perm = jnp.asarray(rr[None, :] == ((rr % sub_per_page) * CMP_STRIDE + rr // sub_per_page)[:, None], BF16)
    bd = _cmp_blockdiag(cmp_w_l)
    bias = _cmp_bias(cmp_b_l)

    n_sel = -(-(past_len + T) // SEL_LEN)
    blocks_per_step = PAGES_PER_STEP * page_rows // SEL_LEN
    sel_used = (n_steps + 1) * blocks_per_step
    sel_w = -(-sel_used // LANES) * LANES
    assert sel_used >= n_sel and blocks_per_step <= LANES
    q_spec = pl.BlockSpec((1, T, NSA_QPAD), lambda b, s, pt: (b, 0, 0))
    const = lambda a: pl.BlockSpec(a.shape, lambda b, s, pt: (0,) * a.ndim)

    o_c, sel = pl.pallas_call(
        functools.partial(_nsa_dec_cmp_kernel, T=T, past_len=past_len, n_rows=n_rows),
        grid_spec=pltpu.PrefetchScalarGridSpec(
            num_scalar_prefetch=1, grid=(DB, n_steps),
            in_specs=_page_specs(page_blk, 0, layer, n_pages) + [const(perm), const(bd), const(bias), q_spec],
            out_specs=[pl.BlockSpec((1, NSA_HEADS * T, LANES), lambda b, s, pt: (b, 0, 0)),
                       pl.BlockSpec((1, NSA_GROUPS * T, sel_w), lambda b, s, pt: (b, 0, 0))],
            scratch_shapes=[pltpu.VMEM((2, n_rows, LANES), F32), pltpu.VMEM((2, 1, LANES), F32)]),
        out_shape=[jax.ShapeDtypeStruct((DB, NSA_HEADS * T, LANES), F32),
                   jax.ShapeDtypeStruct((DB, NSA_GROUPS * T, sel_w), F32)],
        compiler_params=_params(("parallel", "arbitrary")),
        name="nsa_dec_cmp",
    )(pt, *([cache_t] * PAGES_PER_STEP), perm, bd, bias, zb)

    sel_steps = sel[:, :, :sel_used].reshape(DB, NSA_GROUPS * T, n_steps + 1, blocks_per_step).transpose(0, 2, 1, 3)
    sel_steps = jnp.pad(sel_steps, ((0, 0), (0, 0), (0, 0), (0, LANES - blocks_per_step)))
    kk = np.arange(PAGES_PER_STEP * page_rows) // SEL_LEN
    expand = jnp.asarray(kk[None, :] == np.arange(LANES)[:, None], BF16)
    pad_rows = LANES - T
    new_rows = lambda slot: jnp.pad(zcb[:, :, slot * NSA_KV:(slot + 1) * NSA_KV], ((0, 0), (0, pad_rows), (0, 0)))
    new_spec = pl.BlockSpec((1, LANES, LANES), lambda b, s, pt: (b, 0, 0))
    sel_blk = (1, 1, NSA_GROUPS * T, LANES)

    o_s = pl.pallas_call(
        functools.partial(_nsa_dec_sel_kernel, T=T, past_len=past_len),
        grid_spec=pltpu.PrefetchScalarGridSpec(
            num_scalar_prefetch=1, grid=(DB, n_steps),
            in_specs=_page_specs(page_blk, 1, layer, n_pages)
            + [q_spec, pl.BlockSpec(sel_blk, lambda b, s, pt: (b, s, 0, 0)),
               pl.BlockSpec(sel_blk, lambda b, s, pt: (b, n_steps, 0, 0)), const(expand), new_spec, new_spec],
            out_specs=pl.BlockSpec((1, NSA_HEADS * T, LANES), lambda b, s, pt: (b, 0, 0)),
            scratch_shapes=[pltpu.VMEM((NSA_HEADS * T, 1), F32), pltpu.VMEM((NSA_HEADS * T, 1), F32),
                            pltpu.VMEM((NSA_HEADS * T, LANES), F32)]),
        out_shape=jax.ShapeDtypeStruct((DB, NSA_HEADS * T, LANES), F32),
        compiler_params=_params(("parallel", "arbitrary")),
        name="nsa_dec_sel",
    )(pt, *([cache_t] * PAGES_PER_STEP), zb, sel_steps, sel_steps, expand, new_rows(2), new_rows(3))

    w_src = cache_win_kv.shape[1]
    win_t = jnp.transpose(cache_win_kv, (0, 2, 3, 4, 5, 1)).reshape(DB, DEPTH, 2, NSA_KV, w_src)
    b3 = lambda shp: pl.BlockSpec(shp, lambda b: (b, 0, 0))
    return pl.pallas_call(
        functools.partial(_nsa_dec_win_kernel, T=T, past_len=past_len),
        grid=(DB,),
        in_specs=[b3((1, T, NSA_QPAD)), b3((1, T, GATE_COLS)), b3((1, NSA_HEADS * T, LANES)), b3((1, NSA_HEADS * T, LANES)),
                  pl.BlockSpec((1, 1, 2, NSA_KV, w_src), lambda b: (b, layer, 0, 0, 0)),
                  b3((1, LANES, LANES)), b3((1, LANES, LANES))],
        out_specs=b3((1, T, NSA_QPAD)),
        out_shape=jax.ShapeDtypeStruct((DB, T, NSA_QPAD), BF16),
        compiler_params=_params(("parallel",)),
        name="nsa_dec_win",
    )(zb, zd, o_c, o_s, win_t, new_rows(4), new_rows(5))


def _route_top2(logits):
    lane = lax.broadcasted_iota(jnp.int32, logits.shape, 1)
    W = logits.shape[1]
    logits = jnp.where(lane < N_EXPERTS, logits, NEG_INF)
    m1 = jnp.max(logits, axis=-1, keepdims=True)
    i1 = jnp.min(jnp.where(logits == m1, lane, W), axis=-1, keepdims=True)
    rest = jnp.where(lane == i1, NEG_INF, logits)
    m2 = jnp.max(rest, axis=-1, keepdims=True)
    i2 = jnp.min(jnp.where(rest == m2, lane, W), axis=-1, keepdims=True)
    e2 = jnp.exp(m2 - m1)
    den = 1.0 + e2
    return jnp.where(lane == i1, 1.0 / den, 0.0) + jnp.where(lane == i2, e2 / den, 0.0)


def _router_kernel(x_ref, w_ref, b_ref, o_ref):
    x = x_ref[...]
    w = w_ref[...]
    x_hi = x.astype(BF16)
    x_lo = (x - x_hi.astype(F32)).astype(BF16)
    w_hi = w.astype(BF16)
    w_lo = (w - w_hi.astype(F32)).astype(BF16)
    logits = _dot(x_hi, w_hi) + _dot(x_hi, w_lo) + _dot(x_lo, w_hi)
    o_ref[...] = _route_top2(logits + b_ref[...])


def router(x2d, w_router, b_router, tm):
    n = x2d.shape[0]
    w = jnp.pad(w_router.astype(F32), ((0, 0), (0, LANES - N_EXPERTS)))
    bb = jnp.pad(b_router.astype(F32), (0, LANES - N_EXPERTS)).reshape(1, LANES)
    return pl.pallas_call(
        _router_kernel,
        grid=(n // tm,),
        in_specs=[pl.BlockSpec((tm, D_MODEL), lambda i: (i, 0)), pl.BlockSpec(w.shape, lambda i: (0, 0)),
                  pl.BlockSpec(bb.shape, lambda i: (0, 0))],
        out_specs=pl.BlockSpec((tm, LANES), lambda i: (i, 0)),
        out_shape=jax.ShapeDtypeStruct((n, LANES), F32),
        compiler_params=_params(("parallel",)),
        name="router",
    )(x2d, w, bb)


def _merge_kernel(x_ref, hm_ref, hn_ref, hx_ref, wg_ref, bg_ref, wbm_ref, wbn_ref, wbx_ref, wo_ref, g_ref, b_ref,
                  o_ref):
    x = x_ref[...]
    xb = x.astype(BF16)
    merged = None
    for c, (h_ref, w_ref) in enumerate(((hm_ref, wbm_ref), (hn_ref, wbn_ref), (hx_ref, wbx_ref))):
        sl = slice(c * D_MODEL, (c + 1) * D_MODEL)
        gate = jax.nn.sigmoid(_dot(xb, wg_ref[:, sl]) + bg_ref[:, sl])
        term = gate * _dot(h_ref[...], w_ref[...])
        merged = term if merged is None else merged + term
    mix = _dot(merged.astype(BF16), wo_ref[...])
    o_ref[...] = _layer_norm(ALPHA * x + mix, g_ref[...], b_ref[...])


def merge(x2d, hm, hn, hx, wg, bg, wbr, wo, g, b, tm):
    n = x2d.shape[0]
    row = lambda w: pl.BlockSpec((tm, w), lambda i: (i, 0))
    full = lambda a: pl.BlockSpec(a.shape, lambda i: (0,) * a.ndim, pipeline_mode=pl.Buffered(1))
    args = [x2d, hm, hn, hx, wg, bg, wbr[0], wbr[1], wbr[2], wo, g, b]
    return pl.pallas_call(
        _merge_kernel,
        grid=(n // tm,),
        in_specs=[row(a.shape[1]) for a in args[:4]] + [full(a) for a in args[4:]],
        out_specs=row(D_MODEL),
        out_shape=jax.ShapeDtypeStruct((n, D_MODEL), F32),
        compiler_params=_params(("parallel",)),
        name="merge",
    )(*args)


def _ffn_kernel(x_ref, wg_ref, wu_ref, wd_ref, g_ref, b_ref, o_ref, acc_s):
    f = pl.program_id(1)

    @pl.when(f == 0)
    def _():
        acc_s[...] = jnp.zeros(acc_s.shape, F32)

    xb = x_ref[...].astype(BF16)
    hcur = jax.nn.silu(_dot(xb, wg_ref[...])) * _dot(xb, wu_ref[...])
    acc_s[...] += _dot(hcur.astype(BF16), wd_ref[...])

    @pl.when(f == pl.num_programs(1) - 1)
    def _():
        o_ref[...] = _layer_norm(ALPHA * x_ref[...] + acc_s[...], g_ref[...], b_ref[...])


def ffn(x2d, w_up, w_down, g, b, tm, tf):
    n = x2d.shape[0]
    nf = (w_up.shape[1] // 2) // tf
    wmode = dict(pipeline_mode=pl.Buffered(1)) if nf == 1 else {}
    return pl.pallas_call(
        _ffn_kernel,
        grid=(n // tm, nf),
        in_specs=[pl.BlockSpec((tm, D_MODEL), lambda i, f: (i, 0)),
                  pl.BlockSpec((D_MODEL, tf), lambda i, f: (0, f), **wmode),
                  pl.BlockSpec((D_MODEL, tf), lambda i, f: (0, nf + f), **wmode),
                  pl.BlockSpec((tf, D_MODEL), lambda i, f: (f, 0), **wmode),
                  pl.BlockSpec((1, D_MODEL), lambda i, f: (0, 0)),
                  pl.BlockSpec((1, D_MODEL), lambda i, f: (0, 0))],
        out_specs=pl.BlockSpec((tm, D_MODEL), lambda i, f: (i, 0)),
        out_shape=jax.ShapeDtypeStruct((n, D_MODEL), F32),
        scratch_shapes=[pltpu.VMEM((tm, D_MODEL), F32)],
        compiler_params=_params(("parallel", "arbitrary")),
        name="dense_ffn",
    )(x2d, w_up, w_up, w_down, g, b)


def _moe_kernel(np_ref, x_ref, cw_ref, cwt_ref, wg_ref, wu_ref, wd_ref, g_ref, b_ref, o_ref,
                xb_s, y_s, pos_s, post_s, xg_s, ws_s, acc_s, *, blk, cap, cap_pad):
    j = pl.program_id(0)
    e = pl.program_id(1)
    f = pl.program_id(2)
    n_e = pl.num_programs(1)
    n_f = pl.num_programs(2)
    n_pass = np_ref[j * n_e + e]
    RC = min(256, blk)

    @pl.when((e == 0) & (f == 0))
    def _():
        xb_s[...] = x_ref[...].astype(BF16)
        y_s[...] = jnp.zeros(y_s.shape, F32)
        cw = cw_ref[...]
        cwt = cwt_ref[...]
        routed = jnp.where(cw != 0.0, 1.0, 0.0).astype(BF16)
        routed_t = jnp.where(cwt != 0.0, 1.0, 0.0).astype(BF16)
        for rc in range(blk // RC):
            rows_i = rc * RC + lax.broadcasted_iota(jnp.int32, (RC, blk), 0)
            cols_i = lax.broadcasted_iota(jnp.int32, (RC, blk), 1)
            before = jnp.where(cols_i < rows_i, 1.0, 0.0).astype(BF16)
            cnt = _dot(before, routed)
            pos_s[rc * RC:(rc + 1) * RC, :] = jnp.where(cw[rc * RC:(rc + 1) * RC] != 0.0, cnt, -1.0)
            rows_j = lax.broadcasted_iota(jnp.int32, (blk, RC), 0)
            cols_j = rc * RC + lax.broadcasted_iota(jnp.int32, (blk, RC), 1)
            before_t = jnp.where(rows_j < cols_j, 1.0, 0.0).astype(BF16)
            cnt_t = _dot(routed_t, before_t)
            post_s[:, rc * RC:(rc + 1) * RC] = jnp.where(cwt[:, rc * RC:(rc + 1) * RC] != 0.0, cnt_t, -1.0)

    @pl.when(f == 0)
    def _():
        prow = post_s[pl.ds(e, 1), :]
        wrow = cwt_ref[pl.ds(e, 1), :]

        def gather(u, carry):
            slot = (u * cap + lax.broadcasted_iota(jnp.int32, (cap, 1), 0)).astype(F32)
            hit = prow == slot
            xg_s[u, 0:cap, :] = _dot(jnp.where(hit, 1.0, 0.0).astype(BF16), xb_s[...]).astype(BF16)
            ws_s[u, 0:cap, :] = jnp.sum(jnp.where(hit, wrow, 0.0), axis=-1, keepdims=True)
            acc_s[u] = jnp.zeros(acc_s.shape[1:], F32)
            return carry

        lax.fori_loop(0, n_pass, gather, 0)

    def expert(u, carry):
        xg = xg_s[u, 0:cap, :]
        hcur = jax.nn.silu(_dot(xg, wg_ref[0])) * _dot(xg, wu_ref[0]) * ws_s[u, 0:cap, :]
        acc_s[u, 0:cap, :] += _dot(hcur.astype(BF16), wd_ref[0])
        return carry

    lax.fori_loop(0, n_pass, expert, 0)

    @pl.when(f == n_f - 1)
    def _():
        lane = lax.broadcasted_iota(jnp.int32, pos_s.shape, 1)
        pcol = jnp.sum(jnp.where(lane == e, pos_s[...], 0.0), axis=-1, keepdims=True)

        def scatter(u, carry):
            cc = lax.broadcasted_iota(jnp.int32, (1, cap_pad), 1)
            slot = jnp.where(cc < cap, u * cap + cc, NO_SLOT).astype(F32)
            hit = jnp.where(pcol == slot, 1.0, 0.0).astype(BF16)
            y_s[...] += _dot(hit, acc_s[u].astype(BF16))
            return carry

        lax.fori_loop(0, n_pass, scatter, 0)

    @pl.when((e == n_e - 1) & (f == n_f - 1))
    def _():
        o_ref[...] = _layer_norm(ALPHA * x_ref[...] + y_s[...], g_ref[...], b_ref[...])


def moe(x2d, cw, w_up, w_down, g, b, tf):
    n = x2d.shape[0]
    E, _, F2 = w_up.shape
    nf = (F2 // 2) // tf
    blk = _row_tile(n, MOE_BLOCK)
    cap = min(MOE_CAP, blk)
    cap_pad = -(-cap // LANES) * LANES
    max_pass = -(-blk // cap)
    nblk = n // blk
    cwt = jnp.swapaxes(cw[:, :ROUTE_ROWS], 0, 1)
    counts = jnp.sum((cw[:, :E] != 0.0).reshape(nblk, blk, E), axis=1)
    n_pass = ((counts + cap - 1) // cap).astype(jnp.int32).reshape(-1)
    return pl.pallas_call(
        functools.partial(_moe_kernel, blk=blk, cap=cap, cap_pad=cap_pad),
        grid_spec=pltpu.PrefetchScalarGridSpec(
            num_scalar_prefetch=1, grid=(nblk, E, nf),
            in_specs=[pl.BlockSpec((blk, D_MODEL), lambda j, e, f, npr: (j, 0), pipeline_mode=pl.Buffered(1)),
                      pl.BlockSpec((blk, LANES), lambda j, e, f, npr: (j, 0)),
                      pl.BlockSpec((ROUTE_ROWS, blk), lambda j, e, f, npr: (0, j)),
                      pl.BlockSpec((1, D_MODEL, tf), lambda j, e, f, npr: (e, 0, f)),
                      pl.BlockSpec((1, D_MODEL, tf), lambda j, e, f, npr: (e, 0, nf + f)),
                      pl.BlockSpec((1, tf, D_MODEL), lambda j, e, f, npr: (e, f, 0)),
                      pl.BlockSpec((1, D_MODEL), lambda j, e, f, npr: (0, 0)),
                      pl.BlockSpec((1, D_MODEL), lambda j, e, f, npr: (0, 0))],
            out_specs=pl.BlockSpec((blk, D_MODEL), lambda j, e, f, npr: (j, 0)),
            scratch_shapes=[pltpu.VMEM((blk, D_MODEL), BF16), pltpu.VMEM((blk, D_MODEL), F32),
                            pltpu.VMEM((blk, LANES), F32), pltpu.VMEM((ROUTE_ROWS, blk), F32),
                            pltpu.VMEM((max_pass, cap_pad, D_MODEL), BF16), pltpu.VMEM((max_pass, cap_pad, 1), F32),
                            pltpu.VMEM((max_pass, cap_pad, D_MODEL), F32)]),
        out_shape=jax.ShapeDtypeStruct((n, D_MODEL), F32),
        compiler_params=_params(("parallel", "arbitrary", "arbitrary")),
        name="moe_ffn",
    )(n_pass, x2d, cw, cwt, w_up, w_up, w_down, g, b)


def _layer(x, lw, l, *, mem_kv, mem_layer, mstate, decode):
    B, T, _ = x.shape
    n = B * T
    x2d = x.reshape(n, D_MODEL)
    tm = _row_tile(n, 1024)
    za, zb, zc, zcb, zd = in_proj(x2d, lw["in"], tm)
    za, zb, zc, zcb, zd = (a.reshape(B, T, -1) for a in (za, zb, zc, zcb, zd))

    if T % MLSTM_CHUNK == 0:
        h_m, ct, nt, mt = mlstm(za, zd, *mstate, lw["norm_g"], L=MLSTM_CHUNK, t_valid=MLSTM_CHUNK)
    else:
        padt = lambda a: jnp.pad(a, ((0, 0), (0, LANES - T), (0, 0)))
        h_m, ct, nt, mt = mlstm(padt(za), padt(zd), *mstate, lw["norm_g"], L=LANES, t_valid=T)
        h_m = h_m[:, :T]

    if decode is None:
        kc, vc = cmp_tokens(zc, lw["cmp_w"], lw["cmp_b"])
        h_n = nsa_prompt(zb, zd, zcb, kc, vc, tq=min(256, T), CH=min(512, T))
    else:
        h_n = nsa_decode(zb, zd, zcb, decode["nsa"], decode["win"], decode["pt"], lw["cmp_w"], lw["cmp_b"], l)

    h_x = mem_attn(zb, mem_kv, mem_layer, tq=_row_tile(T, 512))

    wg, bg = lw["in"]["mg"]
    flat = lambda a: a.reshape(n, -1)
    x1 = merge(x2d, flat(h_m), flat(h_n), flat(h_x), wg, bg, lw["w_branch"], lw["w_out"], lw["ln1_g"], lw["ln1_b"], tm)

    if l % 2 == 0:
        x2 = ffn(x1, lw["ffn_up"], lw["ffn_down"], lw["ln2_g"], lw["ln2_b"], _row_tile(n, 512), D_FF)
    else:
        cw = router(x1, lw["w_router"], lw["b_router"], _row_tile(n, 1024))
        x2 = moe(x1, cw, lw["ffn_up"], lw["ffn_down"], lw["ln2_g"], lw["ln2_b"], D_FF_EXPERT // 2)
    return x2.reshape(B, T, D_MODEL), zc, (ct, nt, mt)


def kernel(x_prompt, x_sample, mem_prompt, cache_nsa_kv, cache_win_kv, state_mlstm_C, state_mlstm_n,
           state_mlstm_m, cache_mem_kv, page_table, w_in, b_in, mlstm_norm_g, cmp_w, cmp_b, w_mem_kv,
           w_branch, w_out, ln1_g, ln1_b, ln2_g, ln2_b, ffn_w_up, ffn_w_down, moe_w_router, moe_b_router,
           moe_w_up, moe_w_down):
    B, T, _ = x_prompt.shape
    DB, TS, _ = x_sample.shape
    xp, xs = x_prompt, x_sample
    nsa_p, nsa_s, win_p, win_s = [], [], [], []
    Cp, np_, mp, Cs, ns, ms, memkv_p = [], [], [], [], [], [], []
    row1 = lambda a: a.reshape(1, -1).astype(F32)
    for l in range(DEPTH):
        wbr = (w_branch[l, 0].astype(BF16), _pad_branch_rows(w_branch[l, 1]).astype(BF16), w_branch[l, 2].astype(BF16))
        lw = dict(norm_g=mlstm_norm_g[l], cmp_w=cmp_w[l], cmp_b=cmp_b[l], w_branch=wbr,
                  w_out=w_out[l].astype(BF16), ln1_g=row1(ln1_g[l]), ln1_b=row1(ln1_b[l]),
                  ln2_g=row1(ln2_g[l]), ln2_b=row1(ln2_b[l]))
        lw["in"] = _split_in_proj(w_in[l], b_in[l])
        if l % 2 == 0:
            lw["ffn_up"] = ffn_w_up[l // 2].astype(BF16)
            lw["ffn_down"] = ffn_w_down[l // 2].astype(BF16)
        else:
            lw["ffn_up"] = moe_w_up[l // 2].astype(BF16)
            lw["ffn_down"] = moe_w_down[l // 2].astype(BF16)
            lw["w_router"] = moe_w_router[l // 2]
            lw["b_router"] = moe_b_router[l // 2]

        mkv = proj(mem_prompt.reshape(B * MEM_LEN, D_MODEL), w_mem_kv[l].astype(BF16), _row_tile(B * MEM_LEN, 512))
        mkv = mkv.reshape(B, MEM_LEN, 2 * X_WIDTH)
        st0 = (jnp.zeros((B, M_HEADS, M_DH, M_DH), F32), jnp.zeros((B, M_HEADS, M_DH), F32), jnp.zeros((B, M_HEADS), F32))
        xp, zc, st = _layer(xp, lw, l, mem_kv=mkv, mem_layer=0, mstate=st0, decode=None)
        kvn = zc.reshape(B, T, 6, NSA_GROUPS, NSA_DH)
        nsa_p.append(kvn[:, :, :4])
        win_p.append(kvn[:, T - min(WINDOW, T):, 4:])
        Cp.append(st[0]); np_.append(st[1]); mp.append(st[2])
        memkv_p.append(mkv.reshape(B, MEM_LEN, 2, X_HEADS, X_DH))

        sts = (state_mlstm_C[:, l], state_mlstm_n[:, l], state_mlstm_m[:, l])
        dec = dict(nsa=cache_nsa_kv, win=cache_win_kv, pt=page_table)
        xs, zc, st = _layer(xs, lw, l, mem_kv=cache_mem_kv, mem_layer=l, mstate=sts, decode=dec)
        kvn = zc.reshape(DB, TS, 6, NSA_GROUPS, NSA_DH)
        nsa_s.append(kvn[:, :, :4])
        win_s.append(jnp.concatenate([cache_win_kv[:, :, l].astype(F32), kvn[:, :, 4:]], axis=1)[:, TS:])
        Cs.append(st[0]); ns.append(st[1]); ms.append(st[2])
    return (xp, xs,
            jnp.stack(nsa_p, axis=2), jnp.stack(nsa_s, axis=2),
            jnp.stack(win_p, axis=2), jnp.stack(win_s, axis=2),
            jnp.stack(Cp, axis=1), jnp.stack(np_, axis=1), jnp.stack(mp, axis=1),
            jnp.stack(Cs, axis=1), jnp.stack(ns, axis=1), jnp.stack(ms, axis=1),
            jnp.stack(memkv_p, axis=2))
```

```python
import functools

import numpy as np
import jax
import jax.numpy as jnp
from jax import lax
from jax.experimental import pallas as pl
from jax.experimental.pallas import tpu as pltpu

D_MODEL = 1024
DEPTH = 2
BRANCH_WIDTH = 512
N_BRANCH = 3
M_HEADS = 4
M_DH = BRANCH_WIDTH // M_HEADS
M_WIDTH = M_HEADS * M_DH
NSA_HEADS = 8
NSA_DH = BRANCH_WIDTH // NSA_HEADS
NSA_WIDTH = NSA_HEADS * NSA_DH
NSA_GROUPS = 2
NSA_HPG = NSA_HEADS // NSA_GROUPS
NSA_KV = NSA_GROUPS * NSA_DH
CMP_LEN = 32
CMP_STRIDE = 16
SEL_LEN = 64
SEL_TOPN = 16
WINDOW = 512
MEM_LEN = 256
X_HEADS = 4
X_DH = BRANCH_WIDTH // X_HEADS
X_WIDTH = X_HEADS * X_DH
D_FF = 2816
N_EXPERTS = 8
TOP_K = 2
D_FF_EXPERT = 3584
ALPHA = (2.0 * DEPTH) ** 0.25
LN_EPS = 1e-5
IN_SPLITS = (M_WIDTH, M_WIDTH, M_WIDTH, M_HEADS, M_HEADS, M_WIDTH,
             NSA_WIDTH, 6 * NSA_KV, 3 * NSA_HEADS, X_WIDTH, N_BRANCH * D_MODEL)
LOG2E = 1.4426950408889634
NSA_SLOPES = tuple(LOG2E * 2.0 ** (-8.0 * (h + 1) / NSA_HEADS) for h in range(NSA_HEADS))
MASK_BIAS = -(2.0 ** 30)
SEL_FEAT = 64
SEL_GROUPS = 8
SLOPE_PARTS = 3

LANES = 128
SUBLANES = 8
VMEM_LIMIT = 56 * 1024 * 1024
PAGES_PER_STEP = 64
MLSTM_SEQS = 1
MLSTM_CHUNK = 256
GATE_COLS = LANES
NG_OFF = 2 * M_HEADS
NSA_QPAD = NSA_HEADS * LANES
MOE_BLOCK = 1024
MOE_CAP = 288
NO_SLOT = -2
ROUTE_ROWS = 16

F32 = jnp.float32
BF16 = jnp.bfloat16
NEG_INF = float("-inf")
HIGHEST = lax.Precision.HIGHEST


def _dot(a, b, precision=None):
    return jnp.dot(a, b, preferred_element_type=F32, precision=precision)


def _dot_nt(a, b, precision=None):
    return lax.dot_general(a, b, (((1,), (1,)), ((), ())), preferred_element_type=F32, precision=precision)


def _dot_tn(a, b):
    return lax.dot_general(a, b, (((0,), (0,)), ((), ())), preferred_element_type=F32)


def _params(sem):
    return pltpu.CompilerParams(dimension_semantics=sem, vmem_limit_bytes=VMEM_LIMIT)


def _masked_softmax2(s, mask):
    s = jnp.where(mask, s, NEG_INF)
    mx = jnp.max(s, axis=-1, keepdims=True)
    mx = jnp.where(mx > NEG_INF, mx, 0.0)
    p = jnp.where(mask, jnp.exp2(s - mx), 0.0)
    return p / jnp.maximum(jnp.sum(p, axis=-1, keepdims=True), 1e-30)


def _layer_norm(xf, g, b):
    mu = jnp.mean(xf, axis=-1, keepdims=True)
    var = jnp.mean(jnp.square(xf - mu), axis=-1, keepdims=True)
    return (xf - mu) * lax.rsqrt(var + LN_EPS) * g + b


def _row_tile(n, pref):
    return pref if n % pref == 0 else n


def _in_proj_kernel(x_ref, wa_ref, wb_ref, wc_ref, wd_ref, ba_ref, bb_ref, bc_ref, bd_ref,
                    oa_ref, ob_ref, oc_ref, ocb_ref, od_ref):
    x = x_ref[...].astype(BF16)
    oa_ref[...] = (_dot(x, wa_ref[...]) + ba_ref[...]).astype(BF16)
    ob_ref[...] = (_dot(x, wb_ref[...]) + bb_ref[...]).astype(BF16)
    c = _dot(x, wc_ref[...]) + bc_ref[...]
    oc_ref[...] = c
    ocb_ref[...] = c.astype(BF16)
    od_ref[...] = _dot(x, wd_ref[...]) + bd_ref[...]


def _group_select():
    return jax.nn.one_hot(np.arange(NSA_HEADS) // NSA_HPG, NSA_GROUPS, dtype=F32)


def _pad_branch_rows(w_nsa):
    return jnp.einsum("hem,hg->hgem", w_nsa.reshape(NSA_HEADS, NSA_DH, -1), _group_select()).reshape(NSA_QPAD, -1)


def _split_in_proj(w_in_l, b_in_l):
    offs = np.cumsum((0,) + IN_SPLITS)
    w = [w_in_l[:, offs[i]:offs[i + 1]] for i in range(len(IN_SPLITS))]
    b = [b_in_l[offs[i]:offs[i + 1]] for i in range(len(IN_SPLITS))]
    mq, mk, mv, mi, mf, mo, nq, nkv, ng, xq, mg = range(11)
    pad = GATE_COLS - 2 * M_HEADS - 3 * NSA_HEADS

    def cat(ids, zpad=0):
        ww = jnp.concatenate([w[i] for i in ids], axis=1)
        bb = jnp.concatenate([b[i] for i in ids])
        if zpad:
            ww = jnp.pad(ww, ((0, 0), (0, zpad)))
            bb = jnp.pad(bb, (0, zpad))
        return ww.astype(BF16), bb.reshape(1, -1).astype(F32)

    scale = NSA_DH ** -0.5 * LOG2E
    w[nq] = jnp.einsum("dhe,hg->dhge", w[nq].reshape(D_MODEL, NSA_HEADS, NSA_DH) * scale, _group_select()).reshape(D_MODEL, NSA_QPAD)
    b[nq] = jnp.einsum("he,hg->hge", b[nq].reshape(NSA_HEADS, NSA_DH) * scale, _group_select()).reshape(NSA_QPAD)
    return dict(a=cat([mq, mk, mv, mo]), b=cat([nq, xq]), c=cat([nkv]), d=cat([mi, mf, ng], pad), mg=cat([mg]))


def in_proj(x2d, wp, tm):
    n = x2d.shape[0]
    (wa, ba), (wb, bb), (wc, bc), (wd, bd) = wp["a"], wp["b"], wp["c"], wp["d"]
    full = lambda arr: pl.BlockSpec(arr.shape, lambda i: (0, 0), pipeline_mode=pl.Buffered(1))
    row = lambda w: pl.BlockSpec((tm, w), lambda i: (i, 0))
    return pl.pallas_call(
        _in_proj_kernel,
        grid=(n // tm,),
        in_specs=[row(D_MODEL), full(wa), full(wb), full(wc), full(wd), full(ba), full(bb), full(bc), full(bd)],
        out_specs=[row(wa.shape[1]), row(wb.shape[1]), row(wc.shape[1]), row(wc.shape[1]), row(wd.shape[1])],
        out_shape=[jax.ShapeDtypeStruct((n, wa.shape[1]), BF16), jax.ShapeDtypeStruct((n, wb.shape[1]), BF16),
                   jax.ShapeDtypeStruct((n, wc.shape[1]), F32), jax.ShapeDtypeStruct((n, wc.shape[1]), BF16),
                   jax.ShapeDtypeStruct((n, wd.shape[1]), F32)],
        compiler_params=_params(("parallel",)),
        name="in_proj",
    )(x2d, wa, wb, wc, wd, ba, bb, bc, bd)


def _proj_kernel(x_ref, w_ref, o_ref):
    o_ref[...] = _dot(x_ref[...].astype(BF16), w_ref[...])


def proj(x2d, w_bf16, tm):
    n, k = x2d.shape
    m = w_bf16.shape[1]
    return pl.pallas_call(
        _proj_kernel,
        grid=(n // tm,),
        in_specs=[pl.BlockSpec((tm, k), lambda i: (i, 0)), pl.BlockSpec((k, m), lambda i: (0, 0))],
        out_specs=pl.BlockSpec((tm, m), lambda i: (i, 0)),
        out_shape=jax.ShapeDtypeStruct((n, m), F32),
        compiler_params=_params(("parallel",)),
        name="mem_proj",
    )(x2d, w_bf16)


def _log_sigmoid(x):
    return jnp.minimum(x, 0.0) - jnp.log1p(jnp.exp(-jnp.abs(x)))


def _mlstm_kernel(q_ref, k_ref, v_ref, og_ref, gc_ref, gr_ref, c0_ref, n0_ref, m0_ref, ng_ref,
                  h_ref, ct_ref, nt_ref, mt_ref, c_s, n_s, m_s, *, L, t_valid, nb):
    ci = pl.program_id(1)

    @pl.when(ci == 0)
    def _():
        c_s[...] = c0_ref[...]
        n_s[...] = n0_ref[...]
        m_s[...] = m0_ref[...]

    row = lax.broadcasted_iota(jnp.int32, (L, L), 0)
    col = lax.broadcasted_iota(jnp.int32, (L, L), 1)
    causal = row >= col
    tri = causal.astype(F32)
    tri_t = (row <= col).astype(F32)
    rvalid = lax.broadcasted_iota(jnp.int32, (L, GATE_COLS), 0) < t_valid
    cvalid = lax.broadcasted_iota(jnp.int32, (2 * M_HEADS, L), 1) < t_valid
    scale = M_DH ** -0.5

    for bb, h in [(bb, h) for bb in range(nb) for h in range(M_HEADS)]:
        if h == 0:
            gc = gc_ref[bb]
            gr = gr_ref[bb]
            lf_c = jnp.where(rvalid, _log_sigmoid(gc), 0.0)
            lf_r = jnp.where(cvalid, _log_sigmoid(gr), 0.0)
            ig_c = jnp.where(rvalid, gc, NEG_INF)
            ig_r = jnp.where(cvalid, gr, NEG_INF)
            b_c = _dot(tri, lf_c, HIGHEST)
            b_r = _dot(lf_r, tri_t, HIGHEST)
        sl = slice(h * M_DH, (h + 1) * M_DH)
        q = q_ref[bb, :, sl]
        k = k_ref[bb, :, sl]
        v = v_ref[bb, :, sl]
        bc = b_c[:, M_HEADS + h:M_HEADS + h + 1]
        br = b_r[M_HEADS + h:M_HEADS + h + 1, :]
        igc = ig_c[:, h:h + 1]
        igr = ig_r[h:h + 1, :]
        m_prev = m_s[bb, h, 0:1, 0:1]
        c_prev = c_s[bb, h]
        n_prev = n_s[bb, h:h + 1, :]

        dmat = jnp.where(causal, bc - br + igr, NEG_INF)
        inter = bc + m_prev
        m_t = jnp.maximum(inter, jnp.max(dmat, axis=-1, keepdims=True))
        s = _dot_nt(q, k) * scale * jnp.exp(dmat - m_t)
        sc_in = jnp.exp(inter - m_t)
        qf = q.astype(F32)
        num = sc_in * _dot_nt(q, c_prev.astype(BF16)) + _dot(s.astype(BF16), v)
        den = sc_in * jnp.sum(qf * n_prev, axis=-1, keepdims=True) + jnp.sum(s, axis=-1, keepdims=True)
        hh = num / jnp.maximum(jnp.abs(den), jnp.exp(-m_t))

        b_last = bc[L - 1:L, :]
        dec_c = b_last - bc + igc
        dec_r = b_last - br + igr
        m_new = jnp.maximum(b_last + m_prev, jnp.max(dec_r, axis=-1, keepdims=True))
        ws_c = jnp.exp(dec_c - m_new) * scale
        sc = jnp.exp(b_last + m_prev - m_new)
        vf = v.astype(F32)
        kf = k.astype(F32)
        c_new = sc * c_prev + _dot_tn((vf * ws_c).astype(BF16), k)
        n_new = sc * n_prev + jnp.sum(kf * ws_c, axis=0, keepdims=True)
        c_s[bb, h] = c_new
        n_s[bb, h:h + 1, :] = n_new
        m_s[bb, h] = jnp.broadcast_to(m_new, m_s.shape[2:])

        og = og_ref[bb, :, sl].astype(F32)
        hh = hh * jax.nn.sigmoid(og)
        mu = jnp.mean(hh, axis=-1, keepdims=True)
        var = jnp.mean(jnp.square(hh - mu), axis=-1, keepdims=True)
        hn = (hh - mu) * lax.rsqrt(var + LN_EPS) * ng_ref[:, sl]
        h_ref[bb, :, sl] = hn.astype(h_ref.dtype)

    ct_ref[...] = c_s[...]
    nt_ref[...] = n_s[...]
    mt_ref[...] = m_s[...]


def mlstm(za, zd, c0, n0, m0, norm_g, *, L, t_valid):
    B, T, _ = za.shape
    nc = T // L
    gr = jnp.swapaxes(zd[:, :, :2 * M_HEADS], 1, 2)
    m0b = jnp.broadcast_to(m0.astype(F32)[:, :, None, None], (B, M_HEADS, SUBLANES, LANES))
    nb = MLSTM_SEQS if B % MLSTM_SEQS == 0 else 1
    colspec = lambda j: pl.BlockSpec((nb, L, M_WIDTH), lambda b, c, j=j: (b, c, j))
    st = lambda shp: pl.BlockSpec((nb,) + shp, lambda b, c: (b,) + (0,) * len(shp))
    kern = functools.partial(_mlstm_kernel, L=L, t_valid=t_valid, nb=nb)
    h, ct, nt, mt = pl.pallas_call(
        kern,
        grid=(B // nb, nc),
        in_specs=[colspec(0), colspec(1), colspec(2), colspec(3),
                  pl.BlockSpec((nb, L, GATE_COLS), lambda b, c: (b, c, 0)),
                  pl.BlockSpec((nb, 2 * M_HEADS, L), lambda b, c: (b, 0, c)),
                  st((M_HEADS, M_DH, M_DH)), st((M_HEADS, M_DH)), st((M_HEADS, SUBLANES, LANES)),
                  pl.BlockSpec((1, M_WIDTH), lambda b, c: (0, 0))],
        out_specs=[pl.BlockSpec((nb, L, M_WIDTH), lambda b, c: (b, c, 0)),
                   st((M_HEADS, M_DH, M_DH)), st((M_HEADS, M_DH)), st((M_HEADS, SUBLANES, LANES))],
        out_shape=[jax.ShapeDtypeStruct((B, T, M_WIDTH), BF16),
                   jax.ShapeDtypeStruct((B, M_HEADS, M_DH, M_DH), F32),
                   jax.ShapeDtypeStruct((B, M_HEADS, M_DH), F32),
                   jax.ShapeDtypeStruct((B, M_HEADS, SUBLANES, LANES), F32)],
        scratch_shapes=[pltpu.VMEM((nb, M_HEADS, M_DH, M_DH), F32), pltpu.VMEM((nb, M_HEADS, M_DH), F32),
                        pltpu.VMEM((nb, M_HEADS, SUBLANES, LANES), F32)],
        compiler_params=_params(("parallel", "arbitrary")),
        name="mlstm",
    )(za, za, za, za, zd, gr, c0.astype(F32), n0.astype(F32), m0b, norm_g.reshape(1, M_WIDTH).astype(F32))
    return h, ct, nt, mt[:, :, 0, 0]


def _mem_attn_kernel(q_ref, k_ref, v_ref, o_ref):
    scale = X_DH ** -0.5
    for h in range(X_HEADS):
        sl = slice(h * X_DH, (h + 1) * X_DH)
        k = k_ref[:, sl] if len(k_ref.shape) == 2 else k_ref[:, h, :]
        v = v_ref[:, sl] if len(v_ref.shape) == 2 else v_ref[:, h, :]
        s = _dot_nt(q_ref[0, :, sl], k.astype(BF16)) * scale
        mx = jnp.max(s, axis=-1, keepdims=True)
        p = jnp.exp(s - mx)
        p = p / jnp.sum(p, axis=-1, keepdims=True)
        o_ref[0, :, sl] = _dot(p.astype(BF16), v.astype(BF16)).astype(o_ref.dtype)


def mem_attn(zb, kv, layer, tq):
    B, T, _ = zb.shape
    if kv.ndim == 3:
        kv_specs = [pl.BlockSpec((None, MEM_LEN, X_WIDTH), lambda b, i, j=j: (b, 0, j)) for j in range(2)]
    else:
        kv_blk = (None, MEM_LEN, None, None, X_HEADS, X_DH)
        kv_specs = [pl.BlockSpec(kv_blk, lambda b, i, j=j: (b, 0, layer, j, 0, 0)) for j in range(2)]
    return pl.pallas_call(
        _mem_attn_kernel,
        grid=(B, T // tq),
        in_specs=[pl.BlockSpec((1, tq, X_WIDTH), lambda b, i: (b, i, NSA_QPAD // X_WIDTH))] + kv_specs,
        out_specs=pl.BlockSpec((1, tq, X_WIDTH), lambda b, i: (b, i, 0)),
        out_shape=jax.ShapeDtypeStruct((B, T, X_WIDTH), BF16),
        compiler_params=_params(("parallel", "parallel")),
        name="mem_attn",
    )(zb, kv, kv)


def _head_rows(q_ref, nq):
    parts = [q_ref[0, :, hd * LANES:(hd + 1) * LANES] for hd in range(NSA_HEADS)]
    if nq % 16:
        return jnp.concatenate([p.astype(F32) for p in parts], axis=0).astype(BF16)
    return jnp.concatenate(parts, axis=0)


def _store_heads(o_ref, gate, o_c, o_s, o_w, nq):
    lane = lax.broadcasted_iota(jnp.int32, (nq, LANES), 1)
    for hd in range(NSA_HEADS):
        r = slice(hd * nq, (hd + 1) * nq)
        gcol = lambda br: gate[:, NG_OFF + br * NSA_HEADS + hd:NG_OFF + br * NSA_HEADS + hd + 1]
        val = gcol(0) * o_c[r] + gcol(1) * o_s[r] + gcol(2) * o_w[r]
        keep = (lane >= NSA_DH) if hd // NSA_HPG == 1 else (lane < NSA_DH)
        o_ref[0, :, hd * LANES:(hd + 1) * LANES] = jnp.where(keep, val, 0.0).astype(o_ref.dtype)


def _overlap(tok, blk):
    c_start = tok * CMP_STRIDE
    s_start = blk * SEL_LEN
    return ((c_start < s_start + SEL_LEN) & (c_start + CMP_LEN > s_start)).astype(F32)


def _select_blocks(imp, tpos, n_sel, n_top):
    nq, W = imp.shape
    blk = lax.broadcasted_iota(jnp.int32, (nq, W), 1)
    cur = tpos // SEL_LEN
    forced = (blk == 0) | (blk == cur) | (blk == cur - 1)
    v = jnp.where(forced, jnp.inf, jnp.where(blk <= cur, imp, NEG_INF))
    v = jnp.where(blk < n_sel, v, NEG_INF)
    ahead = jnp.zeros((nq, W), F32)
    for j in range(n_sel):
        vj = v[:, j:j + 1]
        ahead = ahead + jnp.where(vj > v, 1.0, jnp.where(vj == v, jnp.where(blk > j, 1.0, 0.0), 0.0))
    return jnp.where((ahead < n_top) & (blk < n_sel), 1.0, 0.0)


def _select_blocks_t(imp_t, tpos_row, n_sel, n_top):
    n_blk, nq = imp_t.shape
    blk = lax.broadcasted_iota(jnp.int32, (n_blk, nq), 0)
    cur = tpos_row // SEL_LEN
    forced = (blk == 0) | (blk == cur) | (blk == cur - 1)
    v = jnp.where(forced, jnp.inf, jnp.where(blk <= cur, imp_t, NEG_INF))
    v = jnp.where(blk < n_sel, v, NEG_INF)
    n_rg = n_blk // SUBLANES
    vg = [v[rg * SUBLANES:(rg + 1) * SUBLANES] for rg in range(n_rg)]
    bg = [rg * SUBLANES + lax.broadcasted_iota(jnp.int32, (SUBLANES, nq), 0) for rg in range(n_rg)]

    ahead = [jnp.zeros((SUBLANES, nq), F32) for _ in range(n_rg)]
    for j in range(n_sel):
        vj = v[j:j + 1, :]
        for rg in range(n_rg):
            if rg * SUBLANES > j:
                inc = jnp.where(vj >= vg[rg], 1.0, 0.0)
            elif (rg + 1) * SUBLANES - 1 <= j:
                inc = jnp.where(vj > vg[rg], 1.0, 0.0)
            else:
                inc = jnp.where(vj > vg[rg], 1.0, jnp.where(vj == vg[rg], jnp.where(bg[rg] > j, 1.0, 0.0), 0.0))
            ahead[rg] = ahead[rg] + inc
    ahead = jnp.concatenate(ahead, axis=0)
    return jnp.where((ahead < n_top) & (blk < n_sel), 1.0, 0.0)


def _online_update(s, m_old, l_old):
    m_new = jnp.maximum(m_old, jnp.max(s, axis=-1, keepdims=True))
    alpha = jnp.exp2(m_old - m_new)
    p = jnp.exp2(s - m_new)
    return p, m_new, alpha, alpha * l_old + jnp.sum(p, axis=-1, keepdims=True)


def _cmp_tokens_kernel(xk_ref, xv_ref, bd_ref, b_ref, kc_ref, vc_ref, *, nsub):
    for c, (x_ref, o_ref) in enumerate(((xk_ref, kc_ref), (xv_ref, vc_ref))):
        a0 = jnp.zeros((nsub, LANES), F32)
        a1 = jnp.zeros((nsub, LANES), F32)
        for j in range(CMP_STRIDE):
            xj = x_ref[0, :, j, :].astype(BF16)
            a0 = a0 + _dot(xj, bd_ref[c, 0, j])
            a1 = a1 + _dot(xj, bd_ref[c, 1, j])
        tok = a0 + pltpu.roll(a1, nsub - 1, axis=0) + b_ref[c]
        o_ref[0] = tok.astype(o_ref.dtype)


def _cmp_blockdiag(cmp_w_l):
    R = CMP_LEN // CMP_STRIDE
    w = cmp_w_l.astype(F32).reshape(2, R, CMP_STRIDE, NSA_DH, NSA_DH)
    eye = jnp.eye(NSA_GROUPS, dtype=F32)
    return jnp.einsum("ab,crjde->crjadbe", eye, w).reshape(2, R, CMP_STRIDE, LANES, LANES).astype(BF16)


def _cmp_bias(cmp_b_l):
    return jnp.tile(cmp_b_l.astype(F32), (1, NSA_GROUPS)).reshape(2, 1, LANES)


def cmp_tokens(zc, cmp_w_l, cmp_b_l):
    B, T, W = zc.shape
    nsub = T // CMP_STRIDE
    x4 = zc.reshape(B, nsub, CMP_STRIDE, W)
    bd = _cmp_blockdiag(cmp_w_l)
    bias = _cmp_bias(cmp_b_l)
    spec = lambda j: pl.BlockSpec((1, nsub, CMP_STRIDE, LANES), lambda b, j=j: (b, 0, 0, j))
    return pl.pallas_call(
        functools.partial(_cmp_tokens_kernel, nsub=nsub),
        grid=(B,),
        in_specs=[spec(0), spec(1), pl.BlockSpec(bd.shape, lambda b: (0,) * 5), pl.BlockSpec(bias.shape, lambda b: (0, 0, 0))],
        out_specs=[pl.BlockSpec((1, nsub, LANES), lambda b: (b, 0, 0))] * 2,
        out_shape=[jax.ShapeDtypeStruct((B, nsub, LANES), BF16)] * 2,
        compiler_params=_params(("parallel",)),
        name="nsa_cmp_tokens",
    )(x4, x4, bd, bias)


def _nsa_prompt_kernel(q_ref, g_ref, kca_ref, vc_ref, ka_ref, vst_ref, kwa_ref, vw_ref, qf_ref, o_ref,
                       qaug_s, s_buf, p_buf, w_buf, pw_buf, pc_s, m_s, l_s, a_s, acc_s, oc_s, ow_s, *, tq, T, CH):
    start = pl.program_id(1) * tq
    n_cmp_rows = kca_ref.shape[1]
    n_sel = T // SEL_LEN
    n_top = min(SEL_TOPN, n_sel)
    n_blk = -(-n_sel // SUBLANES) * SUBLANES
    nq_all = NSA_HEADS * tq
    qw = nq_all // SEL_GROUPS
    tpos_row = start + lax.broadcasted_iota(jnp.int32, (1, tq), 1)
    tpos_all = jnp.concatenate([tpos_row] * NSA_HEADS, axis=1)
    qcols = lambda qi: slice(qi * qw, (qi + 1) * qw)
    tiles = [(qi, ct, slice(qi * qw + ct * LANES, qi * qw + (ct + 1) * LANES), slice(ct * LANES, (ct + 1) * LANES))
             for qi in range(SEL_GROUPS) for ct in range(qw // LANES)]

    qaug_s[:, 0:LANES] = _head_rows(q_ref, tq)
    for hd in range(NSA_HEADS):
        qaug_s[hd * tq:(hd + 1) * tq, LANES:2 * LANES] = jnp.broadcast_to(qf_ref[hd:hd + 1, :], (tq, LANES)).astype(BF16)

    def softmax_tile(s_t):
        mx = jnp.max(s_t, axis=0, keepdims=True)
        mx = jnp.where(mx > NEG_INF, mx, 0.0)
        p_t = jnp.exp2(s_t - mx)
        return p_t, jnp.maximum(jnp.sum(p_t, axis=0, keepdims=True), 1e-30)

    c_end = lax.broadcasted_iota(jnp.int32, (n_cmp_rows, 1), 0) * CMP_STRIDE + CMP_LEN - 1
    for qi in range(SEL_GROUPS):
        s_buf[qi, 0:n_cmp_rows, :] = _dot_nt(kca_ref[0], qaug_s[qcols(qi), :])
    for qi, ct, cols, tc in tiles:
        p_t, den = softmax_tile(jnp.where(c_end <= tpos_all[:, cols], s_buf[qi, 0:n_cmp_rows, tc], NEG_INF))
        pc_s[:, cols] = p_t / den
    oc_s[...] = _dot_tn(vc_ref[0], pc_s[...].astype(BF16))
    ov_t = _overlap(lax.broadcasted_iota(jnp.int32, (n_blk, n_cmp_rows), 1),
                    lax.broadcasted_iota(jnp.int32, (n_blk, n_cmp_rows), 0))
    lane = lax.broadcasted_iota(jnp.int32, (tq, LANES), 1)
    feat_sel = [None] * NSA_HEADS
    for g in range(NSA_GROUPS):
        pg = pc_s[:, g * NSA_HPG * tq:(g * NSA_HPG + 1) * tq]
        for p in range(1, NSA_HPG):
            pg = pg + pc_s[:, (g * NSA_HPG + p) * tq:(g * NSA_HPG + p + 1) * tq]
        sel_t = _select_blocks_t(_dot(ov_t, pg, HIGHEST), tpos_row, n_sel, n_top)
        sel = jnp.concatenate([sel_t, jnp.zeros((LANES - n_blk, tq), F32)], axis=0).T
        unsel = jnp.where(lane < SEL_FEAT, (1.0 - sel) * MASK_BIAS, 0.0)
        for p in range(NSA_HPG):
            hd = g * NSA_HPG + p
            feat_sel[hd] = (unsel + qf_ref[hd:hd + 1, :]).astype(BF16)

    wk = WINDOW + tq
    base_w = pl.multiple_of(start, tq)
    wpos = start - WINDOW + lax.broadcasted_iota(jnp.int32, (wk, 1), 0)
    for qi in range(SEL_GROUPS):
        w_buf[qi] = _dot_nt(kwa_ref[0, pl.ds(base_w, wk), :], qaug_s[qcols(qi), :])
    for qi, ct, cols, tc in tiles:
        tp = tpos_all[:, cols]
        first = jnp.maximum(tp - (WINDOW - 1), 0)
        s_t = jnp.where(wpos >= first, jnp.where(wpos <= tp, w_buf[qi, :, tc], NEG_INF), NEG_INF)
        p_t, den = softmax_tile(s_t)
        pw_buf[qi, :, tc] = p_t.astype(BF16)
        a_s[:, cols] = den
    for qi in range(SEL_GROUPS):
        ow_s[:, qcols(qi)] = _dot_tn(vw_ref[0, pl.ds(base_w, wk), :], pw_buf[qi]) / a_s[:, qcols(qi)]

    for hd in range(NSA_HEADS):
        qaug_s[hd * tq:(hd + 1) * tq, LANES:2 * LANES] = feat_sel[hd]

    m_s[...] = jnp.full(m_s.shape, NEG_INF, F32)
    l_s[...] = jnp.zeros(l_s.shape, F32)
    acc_s[...] = jnp.zeros(acc_s.shape, F32)

    def issue(c, qi):
        base = pl.multiple_of(c * CH, CH)
        s_buf[qi] = _dot_nt(ka_ref[0, pl.ds(base, CH), :], qaug_s[qi * qw:(qi + 1) * qw, :])

    def absorb(c, qi, diag):
        for ct in range(qw // LANES):
            cols = slice(qi * qw + ct * LANES, qi * qw + (ct + 1) * LANES)
            s_t = s_buf[qi, :, ct * LANES:(ct + 1) * LANES]
            if diag:
                kpos = c * CH + lax.broadcasted_iota(jnp.int32, (CH, 1), 0)
                s_t = jnp.where(kpos <= tpos_all[:, cols], s_t, NEG_INF)
            m_old = m_s[:, cols]
            m_new = jnp.maximum(m_old, jnp.max(s_t, axis=0, keepdims=True))
            alpha = jnp.exp2(m_old - m_new)
            p_t = jnp.exp2(s_t - m_new)
            m_s[:, cols] = m_new
            a_s[:, cols] = alpha
            l_s[:, cols] = alpha * l_s[:, cols] + jnp.sum(p_t, axis=0, keepdims=True)
            p_buf[qi, :, ct * LANES:(ct + 1) * LANES] = p_t.astype(BF16)
        cols = slice(qi * qw, (qi + 1) * qw)
        acc_s[:, cols] = a_s[:, cols] * acc_s[:, cols] + _dot(vst_ref[0, c], p_buf[qi])

    def full_chunk(c, carry):
        for qi in range(SEL_GROUPS):
            absorb(c, qi, False)
            issue(c + 1, qi)
        return carry

    n_full = start // CH
    for qi in range(SEL_GROUPS):
        issue(0, qi)
    lax.fori_loop(0, n_full, full_chunk, 0)
    for qi in range(SEL_GROUPS):
        absorb(n_full, qi, True)

    gate_t = jax.nn.sigmoid(g_ref[0]).T
    for hd in range(NSA_HEADS):
        cols = slice(hd * tq, (hd + 1) * tq)
        grow = lambda br: gate_t[NG_OFF + br * NSA_HEADS + hd:NG_OFF + br * NSA_HEADS + hd + 1, :]
        o_s = acc_s[:, cols] / jnp.maximum(l_s[:, cols], 1e-30)
        val = (grow(0) * oc_s[:, cols] + grow(1) * o_s + grow(2) * ow_s[:, cols]).T
        keep = (lane >= NSA_DH) if hd // NSA_HPG == 1 else (lane < NSA_DH)
        o_ref[0, :, hd * LANES:(hd + 1) * LANES] = jnp.where(keep, val, 0.0).astype(o_ref.dtype)


def _bf16_parts(x, n):
    parts = []
    for _ in range(n):
        p = float(np.asarray(x, np.float32).astype(jnp.bfloat16).astype(np.float32))
        parts.append(p)
        x = x - p
    return parts


def _slope_features():
    qf = np.zeros((NSA_HEADS, LANES), np.float32)
    for hd in range(NSA_HEADS):
        for p, s_p in enumerate(_bf16_parts(NSA_SLOPES[hd], SLOPE_PARTS)):
            qf[hd, SEL_FEAT + 2 * p] = s_p * SEL_LEN
            qf[hd, SEL_FEAT + 2 * p + 1] = s_p
    return jnp.asarray(qf)


def _key_features(pos, block_onehot):
    pos = np.asarray(pos)
    ok = pos >= 0
    kf = np.zeros((pos.shape[0], LANES), np.float32)
    if block_onehot:
        kf[np.arange(pos.shape[0])[ok], pos[ok] // SEL_LEN] = 1.0
    for p in range(SLOPE_PARTS):
        kf[ok, SEL_FEAT + 2 * p] = pos[ok] // SEL_LEN
        kf[ok, SEL_FEAT + 2 * p + 1] = pos[ok] % SEL_LEN
    return jnp.asarray(kf, BF16)


def nsa_prompt(zb, zd, zcb, kc, vc, tq, CH):
    B, T, _ = zb.shape
    nsub = kc.shape[1]
    assert T % CH == 0 and CH % tq == 0 and T // SEL_LEN <= SEL_FEAT and nsub <= CH
    qf = _slope_features()
    with_feats = lambda k, feats: jnp.concatenate([k, jnp.broadcast_to(feats[None], (B,) + feats.shape)], axis=2)
    k_aug = with_feats(zcb[:, :, 2 * NSA_KV:3 * NSA_KV], _key_features(np.arange(T), True))
    kc_aug = with_feats(kc, _key_features(np.arange(nsub) * CMP_STRIDE + CMP_LEN - 1, False))
    kw = jnp.pad(zcb[:, :, 4 * NSA_KV:5 * NSA_KV], ((0, 0), (WINDOW, 0), (0, 0)))
    kw_aug = with_feats(kw, _key_features(np.arange(T + WINDOW) - WINDOW, False))
    vs_t = jnp.swapaxes(zcb[:, :, 3 * NSA_KV:4 * NSA_KV].reshape(B, T // CH, CH, NSA_KV), 2, 3)
    vw = jnp.pad(zcb[:, :, 5 * NSA_KV:6 * NSA_KV], ((0, 0), (WINDOW, 0), (0, 0)))
    per_b = lambda rows_, w: pl.BlockSpec((1, rows_, w), lambda b, i: (b, 0, 0))
    kern = functools.partial(_nsa_prompt_kernel, tq=tq, T=T, CH=CH)
    nq_all = NSA_HEADS * tq
    qw = nq_all // SEL_GROUPS
    wk = WINDOW + tq
    assert qw % LANES == 0
    return pl.pallas_call(
        kern,
        grid=(B, T // tq),
        in_specs=[pl.BlockSpec((1, tq, NSA_QPAD), lambda b, i: (b, i, 0)),
                  pl.BlockSpec((1, tq, GATE_COLS), lambda b, i: (b, i, 0)),
                  per_b(nsub, 2 * LANES), per_b(nsub, LANES),
                  per_b(T, 2 * LANES),
                  pl.BlockSpec((1, T // CH, NSA_KV, CH), lambda b, i: (b, 0, 0, 0)),
                  per_b(T + WINDOW, 2 * LANES), per_b(T + WINDOW, LANES),
                  pl.BlockSpec(qf.shape, lambda b, i: (0, 0))],
        out_specs=pl.BlockSpec((1, tq, NSA_QPAD), lambda b, i: (b, i, 0)),
        out_shape=jax.ShapeDtypeStruct((B, T, NSA_QPAD), BF16),
        scratch_shapes=[pltpu.VMEM((nq_all, 2 * LANES), BF16),
                        pltpu.VMEM((SEL_GROUPS, CH, qw), F32), pltpu.VMEM((SEL_GROUPS, CH, qw), BF16),
                        pltpu.VMEM((SEL_GROUPS, wk, qw), F32), pltpu.VMEM((SEL_GROUPS, wk, qw), BF16),
                        pltpu.VMEM((nsub, nq_all), F32),
                        pltpu.VMEM((1, nq_all), F32), pltpu.VMEM((1, nq_all), F32), pltpu.VMEM((1, nq_all), F32),
                        pltpu.VMEM((NSA_KV, nq_all), F32), pltpu.VMEM((NSA_KV, nq_all), F32),
                        pltpu.VMEM((NSA_KV, nq_all), F32)],
        compiler_params=_params(("parallel", "arbitrary")),
        name="nsa_prompt",
    )(zb, zd, kc_aug, vc, k_aug, vs_t, kw_aug, vw, qf)


def _page_specs(shape, slot_blk, layer, n_pages):
    def mk(i):
        def imap(b, s, pt):
            return (pt[b * n_pages + s * PAGES_PER_STEP + i], layer, slot_blk, 0, 0)
        return pl.BlockSpec(shape, imap)
    return [mk(i) for i in range(PAGES_PER_STEP)]


def _nsa_dec_cmp_kernel(pt_ref, *refs, T, past_len, n_rows):
    pages = refs[:PAGES_PER_STEP]
    perm_ref, bd_ref, b_ref, q_ref, oc_ref, sel_ref, tok_s, pend_s = refs[PAGES_PER_STEP:]
    s = pl.program_id(1)
    n_steps = pl.num_programs(1)
    page_rows = pages[0].shape[-1]
    sub = page_rows // CMP_STRIDE
    R = PAGES_PER_STEP * sub

    @pl.when(s == 0)
    def _():
        pend_s[...] = jnp.zeros(pend_s.shape, F32)

    W2 = 2 * LANES
    rid = lax.broadcasted_iota(jnp.int32, (R, W2), 0)
    xs = [_dot_nt(perm_ref[...], pg[0, 0].reshape(W2, page_rows).astype(BF16)) for pg in pages]
    a0 = jnp.zeros((R, W2), F32)
    a1 = jnp.zeros((R, W2), F32)
    for j in range(CMP_STRIDE):
        xj = jnp.concatenate([x[j * sub:(j + 1) * sub] for x in xs], axis=0).astype(BF16)
        a0 = a0 + _dot(xj, bd_ref[0, j])
        a1 = a1 + _dot(xj, bd_ref[1, j])
    a0 = a0 + b_ref[...]
    tok_s[pl.ds(pl.multiple_of(s * R, R), R), :] = jnp.where(rid == 0, pend_s[...], pltpu.roll(a0, 1, axis=0)) + a1
    pend_s[...] = a0[R - 1:R, :]

    @pl.when(s == n_steps - 1)
    def _():
        qp = _head_rows(q_ref, T)
        kc = tok_s[:, 0:LANES].astype(BF16)
        vc = tok_s[:, LANES:W2].astype(BF16)
        sc_all = _dot_nt(qp, kc)
        tpos = past_len + lax.broadcasted_iota(jnp.int32, (T, 1), 0)
        r = lax.broadcasted_iota(jnp.int32, (T, n_rows), 1)
        dc = tpos - ((r - 1) * CMP_STRIDE + CMP_LEN - 1)
        mask_c = (dc >= 0) & (r >= 1)
        dcf = dc.astype(F32)
        pcs = [_masked_softmax2(sc_all[hd * T:(hd + 1) * T] - NSA_SLOPES[hd] * dcf, mask_c) for hd in range(NSA_HEADS)]
        oc_ref[0] = _dot(jnp.concatenate(pcs, axis=0).astype(BF16), vc)
        W = sel_ref.shape[2]
        n_sel = -(-(past_len + T) // SEL_LEN)
        ov = _overlap(lax.broadcasted_iota(jnp.int32, (n_rows, W), 0) - 1, lax.broadcasted_iota(jnp.int32, (n_rows, W), 1))
        for g in range(NSA_GROUPS):
            pg = pcs[g * NSA_HPG]
            for p in range(1, NSA_HPG):
                pg = pg + pcs[g * NSA_HPG + p]
            imp = _dot(pg, ov, HIGHEST)
            sel_ref[0, g * T:(g + 1) * T, :] = _select_blocks(imp, tpos, n_sel, min(SEL_TOPN, n_sel))


def _nsa_dec_sel_kernel(pt_ref, *refs, T, past_len):
    pages = refs[:PAGES_PER_STEP]
    q_ref, sel_ref, sel_last_ref, ex_ref, kn_ref, vn_ref, os_ref, m_s, l_s, acc_s = refs[PAGES_PER_STEP:]
    s = pl.program_id(1)
    n_steps = pl.num_programs(1)
    page_rows = pages[0].shape[-1]
    CH = PAGES_PER_STEP * page_rows

    @pl.when(s == 0)
    def _():
        m_s[...] = jnp.full(m_s.shape, NEG_INF, F32)
        l_s[...] = jnp.zeros(l_s.shape, F32)
        acc_s[...] = jnp.zeros(acc_s.shape, F32)

    qp = _head_rows(q_ref, T)
    tpos = past_len + lax.broadcasted_iota(jnp.int32, (T, 1), 0)

    def update(s_all, mask_of_group, kpos, pv):
        ds = tpos - kpos
        dsf = ds.astype(F32)
        ps = []
        for g in range(NSA_GROUPS):
            mk = mask_of_group(g) & (ds >= 0)
            for p in range(NSA_HPG):
                hd = g * NSA_HPG + p
                r = slice(hd * T, (hd + 1) * T)
                sc = jnp.where(mk, s_all[r] - NSA_SLOPES[hd] * dsf, NEG_INF)
                pr, m_new, alpha, l_new = _online_update(sc, m_s[r], l_s[r])
                m_s[r] = m_new
                l_s[r] = l_new
                acc_s[r] = alpha * acc_s[r]
                ps.append(pr)
        acc_s[...] = acc_s[...] + pv(jnp.concatenate(ps, axis=0).astype(BF16))

    selk = _dot(sel_ref[0, 0].astype(BF16), ex_ref[...])
    pairs = [(pages[i], pages[i + 1]) for i in range(0, PAGES_PER_STEP, 2)]
    side_by_side = lambda a, b, slot: jnp.concatenate([a[0, 0, slot], b[0, 0, slot]], axis=1).astype(BF16)
    s_all = jnp.concatenate([_dot(qp, side_by_side(a, b, 0)) for a, b in pairs], axis=1)
    kpos = s * CH + lax.broadcasted_iota(jnp.int32, (T, CH), 1)

    def pv_pages(pmat):
        out = jnp.zeros((NSA_HEADS * T, LANES), F32)
        for i, (a, b) in enumerate(pairs):
            out = out + _dot_nt(pmat[:, 2 * i * page_rows:2 * (i + 1) * page_rows], side_by_side(a, b, 1))
        return out

    update(s_all, lambda g: selk[g * T:(g + 1) * T] > 0.5, kpos, pv_pages)

    @pl.when(s == n_steps - 1)
    def _():
        nk = kn_ref.shape[1]
        sn = _dot_nt(qp, kn_ref[0])
        lane = lax.broadcasted_iota(jnp.int32, (T, nk), 1)
        sl = sel_last_ref[0, 0]
        update(sn, lambda g: (sl[g * T:(g + 1) * T, 0:1] > 0.5) & (lane < T), past_len + lane,
               lambda pmat: _dot(pmat, vn_ref[0]))
        os_ref[0] = acc_s[...] / jnp.maximum(l_s[...], 1e-30)


def _nsa_dec_win_kernel(q_ref, g_ref, oc_ref, os_ref, wp_ref, kwn_ref, vwn_ref, o_ref, *, T, past_len):
    qp = _head_rows(q_ref, T)
    w_src = wp_ref.shape[-1]
    nk = kwn_ref.shape[1]
    sw_all = jnp.concatenate([_dot(qp, wp_ref[0, 0, 0].astype(BF16)), _dot_nt(qp, kwn_ref[0])], axis=1)
    j = lax.broadcasted_iota(jnp.int32, (T, w_src + nk), 1)
    tpos = past_len + lax.broadcasted_iota(jnp.int32, (T, 1), 0)
    wpos = past_len - w_src + j
    dw = tpos - wpos
    mask_w = (dw >= 0) & (dw < WINDOW) & (wpos >= 0) & (j < w_src + T)
    dwf = dw.astype(F32)
    pws = [_masked_softmax2(sw_all[hd * T:(hd + 1) * T] - NSA_SLOPES[hd] * dwf, mask_w) for hd in range(NSA_HEADS)]
    pw = jnp.concatenate(pws, axis=0).astype(BF16)
    o_w = _dot_nt(pw[:, :w_src], wp_ref[0, 0, 1].astype(BF16)) + _dot(pw[:, w_src:], vwn_ref[0])
    _store_heads(o_ref, jax.nn.sigmoid(g_ref[0]), oc_ref[0], os_ref[0], o_w, T)


def nsa_decode(zb, zd, zcb, cache_nsa_kv, cache_win_kv, page_table, cmp_w_l, cmp_b_l, layer):
    DB, T, _ = zb.shape
    n_pool, page_rows = cache_nsa_kv.shape[:2]
    n_pages = page_table.shape[1]
    past_len = n_pages * page_rows
    n_steps = n_pages // PAGES_PER_STEP
    sub_per_page = page_rows // CMP_STRIDE
    n_rows = n_pages * sub_per_page
    assert (n_rows - 1) * CMP_STRIDE + CMP_LEN - 1 > past_len + T - 1
    assert T <= CMP_STRIDE and n_pages % PAGES_PER_STEP == 0 and page_rows == LANES
    pt = page_table.reshape(-1).astype(jnp.int32)
    cache_t = jnp.transpose(cache_nsa_kv, (0, 2, 3, 4, 5, 1)).reshape(n_pool, DEPTH, 4, NSA_KV, page_rows)
    page_blk = (1, 1, 2, NSA_KV, page_rows)

    rr = np.arange(page_rows)
    perm = jnp.asarray(rr[None, :] == ((rr % sub_per_page) * CMP_STRIDE + rr // sub_per_page)[:, None], BF16)
    bd_kv = _cmp_blockdiag(cmp_w_l)
    zero = jnp.zeros_like(bd_kv[0])
    bd = jnp.concatenate([jnp.concatenate([bd_kv[0], zero], axis=-1), jnp.concatenate([zero, bd_kv[1]], axis=-1)], axis=-2)
    bias = _cmp_bias(cmp_b_l).reshape(1, 2 * LANES)

    n_sel = -(-(past_len + T) // SEL_LEN)
    blocks_per_step = PAGES_PER_STEP * page_rows // SEL_LEN
    sel_used = (n_steps + 1) * blocks_per_step
    sel_w = -(-sel_used // LANES) * LANES
    assert sel_used >= n_sel and blocks_per_step <= LANES
    q_spec = pl.BlockSpec((1, T, NSA_QPAD), lambda b, s, pt: (b, 0, 0))
    const = lambda a: pl.BlockSpec(a.shape, lambda b, s, pt: (0,) * a.ndim)

    o_c, sel = pl.pallas_call(
        functools.partial(_nsa_dec_cmp_kernel, T=T, past_len=past_len, n_rows=n_rows),
        grid_spec=pltpu.PrefetchScalarGridSpec(
            num_scalar_prefetch=1, grid=(DB, n_steps),
            in_specs=_page_specs(page_blk, 0, layer, n_pages) + [const(perm), const(bd), const(bias), q_spec],
            out_specs=[pl.BlockSpec((1, NSA_HEADS * T, LANES), lambda b, s, pt: (b, 0, 0)),
                       pl.BlockSpec((1, NSA_GROUPS * T, sel_w), lambda b, s, pt: (b, 0, 0))],
            scratch_shapes=[pltpu.VMEM((n_rows, 2 * LANES), F32), pltpu.VMEM((1, 2 * LANES), F32)]),
        out_shape=[jax.ShapeDtypeStruct((DB, NSA_HEADS * T, LANES), F32),
                   jax.ShapeDtypeStruct((DB, NSA_GROUPS * T, sel_w), F32)],
        compiler_params=_params(("parallel", "arbitrary")),
        name="nsa_dec_cmp",
    )(pt, *([cache_t] * PAGES_PER_STEP), perm, bd, bias, zb)

    sel_steps = sel[:, :, :sel_used].reshape(DB, NSA_GROUPS * T, n_steps + 1, blocks_per_step).transpose(0, 2, 1, 3)
    sel_steps = jnp.pad(sel_steps, ((0, 0), (0, 0), (0, 0), (0, LANES - blocks_per_step)))
    kk = np.arange(PAGES_PER_STEP * page_rows) // SEL_LEN
    expand = jnp.asarray(kk[None, :] == np.arange(LANES)[:, None], BF16)
    pad_rows = LANES - T
    new_rows = lambda slot: jnp.pad(zcb[:, :, slot * NSA_KV:(slot + 1) * NSA_KV], ((0, 0), (0, pad_rows), (0, 0)))
    new_spec = pl.BlockSpec((1, LANES, LANES), lambda b, s, pt: (b, 0, 0))
    sel_blk = (1, 1, NSA_GROUPS * T, LANES)

    o_s = pl.pallas_call(
        functools.partial(_nsa_dec_sel_kernel, T=T, past_len=past_len),
        grid_spec=pltpu.PrefetchScalarGridSpec(
            num_scalar_prefetch=1, grid=(DB, n_steps),
            in_specs=_page_specs(page_blk, 1, layer, n_pages)
            + [q_spec, pl.BlockSpec(sel_blk, lambda b, s, pt: (b, s, 0, 0)),
               pl.BlockSpec(sel_blk, lambda b, s, pt: (b, n_steps, 0, 0)), const(expand), new_spec, new_spec],
            out_specs=pl.BlockSpec((1, NSA_HEADS * T, LANES), lambda b, s, pt: (b, 0, 0)),
            scratch_shapes=[pltpu.VMEM((NSA_HEADS * T, 1), F32), pltpu.VMEM((NSA_HEADS * T, 1), F32),
                            pltpu.VMEM((NSA_HEADS * T, LANES), F32)]),
        out_shape=jax.ShapeDtypeStruct((DB, NSA_HEADS * T, LANES), F32),
        compiler_params=_params(("parallel", "arbitrary")),
        name="nsa_dec_sel",
    )(pt, *([cache_t] * PAGES_PER_STEP), zb, sel_steps, sel_steps, expand, new_rows(2), new_rows(3))

    w_src = cache_win_kv.shape[1]
    win_t = jnp.transpose(cache_win_kv, (0, 2, 3, 4, 5, 1)).reshape(DB, DEPTH, 2, NSA_KV, w_src)
    b3 = lambda shp: pl.BlockSpec(shp, lambda b: (b, 0, 0))
    return pl.pallas_call(
        functools.partial(_nsa_dec_win_kernel, T=T, past_len=past_len),
        grid=(DB,),
        in_specs=[b3((1, T, NSA_QPAD)), b3((1, T, GATE_COLS)), b3((1, NSA_HEADS * T, LANES)), b3((1, NSA_HEADS * T, LANES)),
                  pl.BlockSpec((1, 1, 2, NSA_KV, w_src), lambda b: (b, layer, 0, 0, 0)),
                  b3((1, LANES, LANES)), b3((1, LANES, LANES))],
        out_specs=b3((1, T, NSA_QPAD)),
        out_shape=jax.ShapeDtypeStruct((DB, T, NSA_QPAD), BF16),
        compiler_params=_params(("parallel",)),
        name="nsa_dec_win",
    )(zb, zd, o_c, o_s, win_t, new_rows(4), new_rows(5))


def _route_top2(logits):
    lane = lax.broadcasted_iota(jnp.int32, logits.shape, 1)
    W = logits.shape[1]
    logits = jnp.where(lane < N_EXPERTS, logits, NEG_INF)
    m1 = jnp.max(logits, axis=-1, keepdims=True)
    i1 = jnp.min(jnp.where(logits == m1, lane, W), axis=-1, keepdims=True)
    rest = jnp.where(lane == i1, NEG_INF, logits)
    m2 = jnp.max(rest, axis=-1, keepdims=True)
    i2 = jnp.min(jnp.where(rest == m2, lane, W), axis=-1, keepdims=True)
    e2 = jnp.exp(m2 - m1)
    den = 1.0 + e2
    return jnp.where(lane == i1, 1.0 / den, 0.0) + jnp.where(lane == i2, e2 / den, 0.0)


def _router_kernel(x_ref, w_ref, b_ref, o_ref):
    x = x_ref[...]
    w = w_ref[...]
    x_hi = x.astype(BF16)
    x_lo = (x - x_hi.astype(F32)).astype(BF16)
    w_hi = w.astype(BF16)
    w_lo = (w - w_hi.astype(F32)).astype(BF16)
    logits = _dot(x_hi, w_hi) + _dot(x_hi, w_lo) + _dot(x_lo, w_hi)
    o_ref[...] = _route_top2(logits + b_ref[...])


def router(x2d, w_router, b_router, tm):
    n = x2d.shape[0]
    w = jnp.pad(w_router.astype(F32), ((0, 0), (0, LANES - N_EXPERTS)))
    bb = jnp.pad(b_router.astype(F32), (0, LANES - N_EXPERTS)).reshape(1, LANES)
    return pl.pallas_call(
        _router_kernel,
        grid=(n // tm,),
        in_specs=[pl.BlockSpec((tm, D_MODEL), lambda i: (i, 0)), pl.BlockSpec(w.shape, lambda i: (0, 0)),
                  pl.BlockSpec(bb.shape, lambda i: (0, 0))],
        out_specs=pl.BlockSpec((tm, LANES), lambda i: (i, 0)),
        out_shape=jax.ShapeDtypeStruct((n, LANES), F32),
        compiler_params=_params(("parallel",)),
        name="router",
    )(x2d, w, bb)


def _merge_kernel(x_ref, hm_ref, hn_ref, hx_ref, wg_ref, bg_ref, wbm_ref, wbn_ref, wbx_ref, wo_ref, g_ref, b_ref,
                  o_ref):
    x = x_ref[...]
    xb = x.astype(BF16)
    merged = None
    for c, (h_ref, w_ref) in enumerate(((hm_ref, wbm_ref), (hn_ref, wbn_ref), (hx_ref, wbx_ref))):
        sl = slice(c * D_MODEL, (c + 1) * D_MODEL)
        gate = jax.nn.sigmoid(_dot(xb, wg_ref[:, sl]) + bg_ref[:, sl])
        term = gate * _dot(h_ref[...], w_ref[...])
        merged = term if merged is None else merged + term
    mix = _dot(merged.astype(BF16), wo_ref[...])
    o_ref[...] = _layer_norm(ALPHA * x + mix, g_ref[...], b_ref[...])


def merge(x2d, hm, hn, hx, wg, bg, wbr, wo, g, b, tm):
    n = x2d.shape[0]
    row = lambda w: pl.BlockSpec((tm, w), lambda i: (i, 0))
    full = lambda a: pl.BlockSpec(a.shape, lambda i: (0,) * a.ndim, pipeline_mode=pl.Buffered(1))
    args = [x2d, hm, hn, hx, wg, bg, wbr[0], wbr[1], wbr[2], wo, g, b]
    return pl.pallas_call(
        _merge_kernel,
        grid=(n // tm,),
        in_specs=[row(a.shape[1]) for a in args[:4]] + [full(a) for a in args[4:]],
        out_specs=row(D_MODEL),
        out_shape=jax.ShapeDtypeStruct((n, D_MODEL), F32),
        compiler_params=_params(("parallel",)),
        name="merge",
    )(*args)


def _ffn_kernel(x_ref, wg_ref, wu_ref, wd_ref, g_ref, b_ref, o_ref, acc_s):
    f = pl.program_id(1)

    @pl.when(f == 0)
    def _():
        acc_s[...] = jnp.zeros(acc_s.shape, F32)

    xb = x_ref[...].astype(BF16)
    hcur = jax.nn.silu(_dot(xb, wg_ref[...])) * _dot(xb, wu_ref[...])
    acc_s[...] += _dot(hcur.astype(BF16), wd_ref[...])

    @pl.when(f == pl.num_programs(1) - 1)
    def _():
        o_ref[...] = _layer_norm(ALPHA * x_ref[...] + acc_s[...], g_ref[...], b_ref[...])


def ffn(x2d, w_up, w_down, g, b, tm, tf):
    n = x2d.shape[0]
    nf = (w_up.shape[1] // 2) // tf
    wmode = dict(pipeline_mode=pl.Buffered(1)) if nf == 1 else {}
    return pl.pallas_call(
        _ffn_kernel,
        grid=(n // tm, nf),
        in_specs=[pl.BlockSpec((tm, D_MODEL), lambda i, f: (i, 0)),
                  pl.BlockSpec((D_MODEL, tf), lambda i, f: (0, f), **wmode),
                  pl.BlockSpec((D_MODEL, tf), lambda i, f: (0, nf + f), **wmode),
                  pl.BlockSpec((tf, D_MODEL), lambda i, f: (f, 0), **wmode),
                  pl.BlockSpec((1, D_MODEL), lambda i, f: (0, 0)),
                  pl.BlockSpec((1, D_MODEL), lambda i, f: (0, 0))],
        out_specs=pl.BlockSpec((tm, D_MODEL), lambda i, f: (i, 0)),
        out_shape=jax.ShapeDtypeStruct((n, D_MODEL), F32),
        scratch_shapes=[pltpu.VMEM((tm, D_MODEL), F32)],
        compiler_params=_params(("parallel", "arbitrary")),
        name="dense_ffn",
    )(x2d, w_up, w_up, w_down, g, b)


def _moe_kernel(np_ref, x_ref, cw_ref, cwt_ref, wg_ref, wu_ref, wd_ref, g_ref, b_ref, o_ref,
                xb_s, y_s, pos_s, post_s, xg_s, ws_s, acc_s, *, blk, cap, cap_pad):
    j = pl.program_id(0)
    e = pl.program_id(1)
    f = pl.program_id(2)
    n_e = pl.num_programs(1)
    n_f = pl.num_programs(2)
    n_pass = np_ref[j * n_e + e]
    RC = min(256, blk)

    @pl.when((e == 0) & (f == 0))
    def _():
        xb_s[...] = x_ref[...].astype(BF16)
        y_s[...] = jnp.zeros(y_s.shape, F32)
        cw = cw_ref[...]
        cwt = cwt_ref[...]
        routed = jnp.where(cw != 0.0, 1.0, 0.0).astype(BF16)
        routed_t = jnp.where(cwt != 0.0, 1.0, 0.0).astype(BF16)
        for rc in range(blk // RC):
            rows_i = rc * RC + lax.broadcasted_iota(jnp.int32, (RC, blk), 0)
            cols_i = lax.broadcasted_iota(jnp.int32, (RC, blk), 1)
            before = jnp.where(cols_i < rows_i, 1.0, 0.0).astype(BF16)
            cnt = _dot(before, routed)
            pos_s[rc * RC:(rc + 1) * RC, :] = jnp.where(cw[rc * RC:(rc + 1) * RC] != 0.0, cnt, -1.0)
            rows_j = lax.broadcasted_iota(jnp.int32, (blk, RC), 0)
            cols_j = rc * RC + lax.broadcasted_iota(jnp.int32, (blk, RC), 1)
            before_t = jnp.where(rows_j < cols_j, 1.0, 0.0).astype(BF16)
            cnt_t = _dot(routed_t, before_t)
            post_s[:, rc * RC:(rc + 1) * RC] = jnp.where(cwt[:, rc * RC:(rc + 1) * RC] != 0.0, cnt_t, -1.0)

    @pl.when(f == 0)
    def _():
        prow = post_s[pl.ds(e, 1), :]
        wrow = cwt_ref[pl.ds(e, 1), :]

        def gather(u, carry):
            slot = (u * cap + lax.broadcasted_iota(jnp.int32, (cap, 1), 0)).astype(F32)
            hit = prow == slot
            xg_s[u, 0:cap, :] = _dot(jnp.where(hit, 1.0, 0.0).astype(BF16), xb_s[...]).astype(BF16)
            ws_s[u, 0:cap, :] = jnp.sum(jnp.where(hit, wrow, 0.0), axis=-1, keepdims=True)
            acc_s[u] = jnp.zeros(acc_s.shape[1:], F32)
            return carry

        lax.fori_loop(0, n_pass, gather, 0)

    def expert(u, carry):
        xg = xg_s[u, 0:cap, :]
        hcur = jax.nn.silu(_dot(xg, wg_ref[0])) * _dot(xg, wu_ref[0]) * ws_s[u, 0:cap, :]
        acc_s[u, 0:cap, :] += _dot(hcur.astype(BF16), wd_ref[0])
        return carry

    lax.fori_loop(0, n_pass, expert, 0)

    @pl.when(f == n_f - 1)
    def _():
        lane = lax.broadcasted_iota(jnp.int32, pos_s.shape, 1)
        pcol = jnp.sum(jnp.where(lane == e, pos_s[...], 0.0), axis=-1, keepdims=True)

        def scatter(u, carry):
            cc = lax.broadcasted_iota(jnp.int32, (1, cap_pad), 1)
            slot = jnp.where(cc < cap, u * cap + cc, NO_SLOT).astype(F32)
            hit = jnp.where(pcol == slot, 1.0, 0.0).astype(BF16)
            y_s[...] += _dot(hit, acc_s[u].astype(BF16))
            return carry

        lax.fori_loop(0, n_pass, scatter, 0)

    @pl.when((e == n_e - 1) & (f == n_f - 1))
    def _():
        o_ref[...] = _layer_norm(ALPHA * x_ref[...] + y_s[...], g_ref[...], b_ref[...])


def moe(x2d, cw, w_up, w_down, g, b, tf):
    n = x2d.shape[0]
    E, _, F2 = w_up.shape
    nf = (F2 // 2) // tf
    blk = _row_tile(n, MOE_BLOCK)
    cap = min(MOE_CAP, blk)
    cap_pad = -(-cap // LANES) * LANES
    max_pass = -(-blk // cap)
    nblk = n // blk
    cwt = jnp.swapaxes(cw[:, :ROUTE_ROWS], 0, 1)
    counts = jnp.sum((cw[:, :E] != 0.0).reshape(nblk, blk, E), axis=1)
    n_pass = ((counts + cap - 1) // cap).astype(jnp.int32).reshape(-1)
    return pl.pallas_call(
        functools.partial(_moe_kernel, blk=blk, cap=cap, cap_pad=cap_pad),
        grid_spec=pltpu.PrefetchScalarGridSpec(
            num_scalar_prefetch=1, grid=(nblk, E, nf),
            in_specs=[pl.BlockSpec((blk, D_MODEL), lambda j, e, f, npr: (j, 0), pipeline_mode=pl.Buffered(1)),
                      pl.BlockSpec((blk, LANES), lambda j, e, f, npr: (j, 0)),
                      pl.BlockSpec((ROUTE_ROWS, blk), lambda j, e, f, npr: (0, j)),
                      pl.BlockSpec((1, D_MODEL, tf), lambda j, e, f, npr: (e, 0, f)),
                      pl.BlockSpec((1, D_MODEL, tf), lambda j, e, f, npr: (e, 0, nf + f)),
                      pl.BlockSpec((1, tf, D_MODEL), lambda j, e, f, npr: (e, f, 0)),
                      pl.BlockSpec((1, D_MODEL), lambda j, e, f, npr: (0, 0)),
                      pl.BlockSpec((1, D_MODEL), lambda j, e, f, npr: (0, 0))],
            out_specs=pl.BlockSpec((blk, D_MODEL), lambda j, e, f, npr: (j, 0)),
            scratch_shapes=[pltpu.VMEM((blk, D_MODEL), BF16), pltpu.VMEM((blk, D_MODEL), F32),
                            pltpu.VMEM((blk, LANES), F32), pltpu.VMEM((ROUTE_ROWS, blk), F32),
                            pltpu.VMEM((max_pass, cap_pad, D_MODEL), BF16), pltpu.VMEM((max_pass, cap_pad, 1), F32),
                            pltpu.VMEM((max_pass, cap_pad, D_MODEL), F32)]),
        out_shape=jax.ShapeDtypeStruct((n, D_MODEL), F32),
        compiler_params=_params(("parallel", "arbitrary", "arbitrary")),
        name="moe_ffn",
    )(n_pass, x2d, cw, cwt, w_up, w_up, w_down, g, b)


def _layer(x, lw, l, *, mem_kv, mem_layer, mstate, decode):
    B, T, _ = x.shape
    n = B * T
    x2d = x.reshape(n, D_MODEL)
    tm = _row_tile(n, 1024)
    za, zb, zc, zcb, zd = in_proj(x2d, lw["in"], tm)
    za, zb, zc, zcb, zd = (a.reshape(B, T, -1) for a in (za, zb, zc, zcb, zd))

    if T % MLSTM_CHUNK == 0:
        h_m, ct, nt, mt = mlstm(za, zd, *mstate, lw["norm_g"], L=MLSTM_CHUNK, t_valid=MLSTM_CHUNK)
    else:
        padt = lambda a: jnp.pad(a, ((0, 0), (0, LANES - T), (0, 0)))
        h_m, ct, nt, mt = mlstm(padt(za), padt(zd), *mstate, lw["norm_g"], L=LANES, t_valid=T)
        h_m = h_m[:, :T]

    if decode is None:
        kc, vc = cmp_tokens(zc, lw["cmp_w"], lw["cmp_b"])
        h_n = nsa_prompt(zb, zd, zcb, kc, vc, tq=min(256, T), CH=min(512, T))
    else:
        h_n = nsa_decode(zb, zd, zcb, decode["nsa"], decode["win"], decode["pt"], lw["cmp_w"], lw["cmp_b"], l)

    h_x = mem_attn(zb, mem_kv, mem_layer, tq=_row_tile(T, 512))

    wg, bg = lw["in"]["mg"]
    flat = lambda a: a.reshape(n, -1)
    x1 = merge(x2d, flat(h_m), flat(h_n), flat(h_x), wg, bg, lw["w_branch"], lw["w_out"], lw["ln1_g"], lw["ln1_b"], tm)

    if l % 2 == 0:
        x2 = ffn(x1, lw["ffn_up"], lw["ffn_down"], lw["ln2_g"], lw["ln2_b"], _row_tile(n, 512), D_FF)
    else:
        cw = router(x1, lw["w_router"], lw["b_router"], _row_tile(n, 1024))
        x2 = moe(x1, cw, lw["ffn_up"], lw["ffn_down"], lw["ln2_g"], lw["ln2_b"], D_FF_EXPERT // 2)
    return x2.reshape(B, T, D_MODEL), zc, (ct, nt, mt)


def kernel(x_prompt, x_sample, mem_prompt, cache_nsa_kv, cache_win_kv, state_mlstm_C, state_mlstm_n,
           state_mlstm_m, cache_mem_kv, page_table, w_in, b_in, mlstm_norm_g, cmp_w, cmp_b, w_mem_kv,
           w_branch, w_out, ln1_g, ln1_b, ln2_g, ln2_b, ffn_w_up, ffn_w_down, moe_w_router, moe_b_router,
           moe_w_up, moe_w_down):
    B, T, _ = x_prompt.shape
    DB, TS, _ = x_sample.shape
    xp, xs = x_prompt, x_sample
    nsa_p, nsa_s, win_p, win_s = [], [], [], []
    Cp, np_, mp, Cs, ns, ms, memkv_p = [], [], [], [], [], [], []
    row1 = lambda a: a.reshape(1, -1).astype(F32)
    for l in range(DEPTH):
        wbr = (w_branch[l, 0].astype(BF16), _pad_branch_rows(w_branch[l, 1]).astype(BF16), w_branch[l, 2].astype(BF16))
        lw = dict(norm_g=mlstm_norm_g[l], cmp_w=cmp_w[l], cmp_b=cmp_b[l], w_branch=wbr,
                  w_out=w_out[l].astype(BF16), ln1_g=row1(ln1_g[l]), ln1_b=row1(ln1_b[l]),
                  ln2_g=row1(ln2_g[l]), ln2_b=row1(ln2_b[l]))
        lw["in"] = _split_in_proj(w_in[l], b_in[l])
        if l % 2 == 0:
            lw["ffn_up"] = ffn_w_up[l // 2].astype(BF16)
            lw["ffn_down"] = ffn_w_down[l // 2].astype(BF16)
        else:
            lw["ffn_up"] = moe_w_up[l // 2].astype(BF16)
            lw["ffn_down"] = moe_w_down[l // 2].astype(BF16)
            lw["w_router"] = moe_w_router[l // 2]
            lw["b_router"] = moe_b_router[l // 2]

        mkv = proj(mem_prompt.reshape(B * MEM_LEN, D_MODEL), w_mem_kv[l].astype(BF16), _row_tile(B * MEM_LEN, 512))
        mkv = mkv.reshape(B, MEM_LEN, 2 * X_WIDTH)
        st0 = (jnp.zeros((B, M_HEADS, M_DH, M_DH), F32), jnp.zeros((B, M_HEADS, M_DH), F32), jnp.zeros((B, M_HEADS), F32))
        xp, zc, st = _layer(xp, lw, l, mem_kv=mkv, mem_layer=0, mstate=st0, decode=None)
        kvn = zc.reshape(B, T, 6, NSA_GROUPS, NSA_DH)
        nsa_p.append(kvn[:, :, :4])
        win_p.append(kvn[:, T - min(WINDOW, T):, 4:])
        Cp.append(st[0]); np_.append(st[1]); mp.append(st[2])
        memkv_p.append(mkv.reshape(B, MEM_LEN, 2, X_HEADS, X_DH))

        sts = (state_mlstm_C[:, l], state_mlstm_n[:, l], state_mlstm_m[:, l])
        dec = dict(nsa=cache_nsa_kv, win=cache_win_kv, pt=page_table)
        xs, zc, st = _layer(xs, lw, l, mem_kv=cache_mem_kv, mem_layer=l, mstate=sts, decode=dec)
        kvn = zc.reshape(DB, TS, 6, NSA_GROUPS, NSA_DH)
        nsa_s.append(kvn[:, :, :4])
        win_s.append(jnp.concatenate([cache_win_kv[:, :, l].astype(F32), kvn[:, :, 4:]], axis=1)[:, TS:])
        Cs.append(st[0]); ns.append(st[1]); ms.append(st[2])
    return (xp, xs,
            jnp.stack(nsa_p, axis=2), jnp.stack(nsa_s, axis=2),
            jnp.stack(win_p, axis=2), jnp.stack(win_s, axis=2),
            jnp.stack(Cp, axis=1), jnp.stack(np_, axis=1), jnp.stack(mp, axis=1),
            jnp.stack(Cs, axis=1), jnp.stack(ns, axis=1), jnp.stack(ms, axis=1),
            jnp.stack(memkv_p, axis=2))
```

```python
import functools

import numpy as np
import jax
import jax.numpy as jnp
from jax import lax
from jax.experimental import pallas as pl
from jax.experimental.pallas import tpu as pltpu

D_MODEL = 1024
DEPTH = 2
BRANCH_WIDTH = 512
N_BRANCH = 3
M_HEADS = 4
M_DH = BRANCH_WIDTH // M_HEADS
M_WIDTH = M_HEADS * M_DH
NSA_HEADS = 8
NSA_DH = BRANCH_WIDTH // NSA_HEADS
NSA_WIDTH = NSA_HEADS * NSA_DH
NSA_GROUPS = 2
NSA_HPG = NSA_HEADS // NSA_GROUPS
NSA_KV = NSA_GROUPS * NSA_DH
CMP_LEN = 32
CMP_STRIDE = 16
SEL_LEN = 64
SEL_TOPN = 16
WINDOW = 512
MEM_LEN = 256
X_HEADS = 4
X_DH = BRANCH_WIDTH // X_HEADS
X_WIDTH = X_HEADS * X_DH
D_FF = 2816
N_EXPERTS = 8
TOP_K = 2
D_FF_EXPERT = 3584
ALPHA = (2.0 * DEPTH) ** 0.25
LN_EPS = 1e-5
IN_SPLITS = (M_WIDTH, M_WIDTH, M_WIDTH, M_HEADS, M_HEADS, M_WIDTH,
             NSA_WIDTH, 6 * NSA_KV, 3 * NSA_HEADS, X_WIDTH, N_BRANCH * D_MODEL)
LOG2E = 1.4426950408889634
NSA_SLOPES = tuple(LOG2E * 2.0 ** (-8.0 * (h + 1) / NSA_HEADS) for h in range(NSA_HEADS))
MASK_BIAS = -(2.0 ** 30)
SEL_FEAT = 64
SEL_GROUPS = 8
SLOPE_PARTS = 3

LANES = 128
SUBLANES = 8
VMEM_LIMIT = 56 * 1024 * 1024
PAGES_PER_STEP = 64
MLSTM_SEQS = 1
MLSTM_CHUNK = 256
GATE_COLS = LANES
NG_OFF = 2 * M_HEADS
NSA_QPAD = NSA_HEADS * LANES
MOE_BLOCK = 1024
MOE_CAP = 288
NO_SLOT = -2
ROUTE_ROWS = 16

F32 = jnp.float32
BF16 = jnp.bfloat16
NEG_INF = float("-inf")
HIGHEST = lax.Precision.HIGHEST


def _dot(a, b, precision=None):
    return jnp.dot(a, b, preferred_element_type=F32, precision=precision)


def _dot_nt(a, b, precision=None):
    return lax.dot_general(a, b, (((1,), (1,)), ((), ())), preferred_element_type=F32, precision=precision)


def _dot_tn(a, b):
    return lax.dot_general(a, b, (((0,), (0,)), ((), ())), preferred_element_type=F32)


def _params(sem):
    return pltpu.CompilerParams(dimension_semantics=sem, vmem_limit_bytes=VMEM_LIMIT)


def _masked_softmax2(s, mask):
    s = jnp.where(mask, s, NEG_INF)
    mx = jnp.max(s, axis=-1, keepdims=True)
    mx = jnp.where(mx > NEG_INF, mx, 0.0)
    p = jnp.where(mask, jnp.exp2(s - mx), 0.0)
    return p / jnp.maximum(jnp.sum(p, axis=-1, keepdims=True), 1e-30)


def _layer_norm(xf, g, b):
    mu = jnp.mean(xf, axis=-1, keepdims=True)
    var = jnp.mean(jnp.square(xf - mu), axis=-1, keepdims=True)
    return (xf - mu) * lax.rsqrt(var + LN_EPS) * g + b


def _row_tile(n, pref):
    return pref if n % pref == 0 else n


def _in_proj_kernel(x_ref, wa_ref, wb_ref, wc_ref, wd_ref, ba_ref, bb_ref, bc_ref, bd_ref,
                    oa_ref, ob_ref, oc_ref, ocb_ref, od_ref):
    x = x_ref[...].astype(BF16)
    oa_ref[...] = (_dot(x, wa_ref[...]) + ba_ref[...]).astype(BF16)
    ob_ref[...] = (_dot(x, wb_ref[...]) + bb_ref[...]).astype(BF16)
    c = _dot(x, wc_ref[...]) + bc_ref[...]
    oc_ref[...] = c
    ocb_ref[...] = c.astype(BF16)
    od_ref[...] = _dot(x, wd_ref[...]) + bd_ref[...]


def _group_select():
    return jax.nn.one_hot(np.arange(NSA_HEADS) // NSA_HPG, NSA_GROUPS, dtype=F32)


def _pad_branch_rows(w_nsa):
    return jnp.einsum("hem,hg->hgem", w_nsa.reshape(NSA_HEADS, NSA_DH, -1), _group_select()).reshape(NSA_QPAD, -1)


def _split_in_proj(w_in_l, b_in_l):
    offs = np.cumsum((0,) + IN_SPLITS)
    w = [w_in_l[:, offs[i]:offs[i + 1]] for i in range(len(IN_SPLITS))]
    b = [b_in_l[offs[i]:offs[i + 1]] for i in range(len(IN_SPLITS))]
    mq, mk, mv, mi, mf, mo, nq, nkv, ng, xq, mg = range(11)
    pad = GATE_COLS - 2 * M_HEADS - 3 * NSA_HEADS

    def cat(ids, zpad=0):
        ww = jnp.concatenate([w[i] for i in ids], axis=1)
        bb = jnp.concatenate([b[i] for i in ids])
        if zpad:
            ww = jnp.pad(ww, ((0, 0), (0, zpad)))
            bb = jnp.pad(bb, (0, zpad))
        return ww.astype(BF16), bb.reshape(1, -1).astype(F32)

    scale = NSA_DH ** -0.5 * LOG2E
    w[nq] = jnp.einsum("dhe,hg->dhge", w[nq].reshape(D_MODEL, NSA_HEADS, NSA_DH) * scale, _group_select()).reshape(D_MODEL, NSA_QPAD)
    b[nq] = jnp.einsum("he,hg->hge", b[nq].reshape(NSA_HEADS, NSA_DH) * scale, _group_select()).reshape(NSA_QPAD)
    return dict(a=cat([mq, mk, mv, mo]), b=cat([nq, xq]), c=cat([nkv]), d=cat([mi, mf, ng], pad), mg=cat([mg]))


def in_proj(x2d, wp, tm):
    n = x2d.shape[0]
    (wa, ba), (wb, bb), (wc, bc), (wd, bd) = wp["a"], wp["b"], wp["c"], wp["d"]
    full = lambda arr: pl.BlockSpec(arr.shape, lambda i: (0, 0), pipeline_mode=pl.Buffered(1))
    row = lambda w: pl.BlockSpec((tm, w), lambda i: (i, 0))
    return pl.pallas_call(
        _in_proj_kernel,
        grid=(n // tm,),
        in_specs=[row(D_MODEL), full(wa), full(wb), full(wc), full(wd), full(ba), full(bb), full(bc), full(bd)],
        out_specs=[row(wa.shape[1]), row(wb.shape[1]), row(wc.shape[1]), row(wc.shape[1]), row(wd.shape[1])],
        out_shape=[jax.ShapeDtypeStruct((n, wa.shape[1]), BF16), jax.ShapeDtypeStruct((n, wb.shape[1]), BF16),
                   jax.ShapeDtypeStruct((n, wc.shape[1]), F32), jax.ShapeDtypeStruct((n, wc.shape[1]), BF16),
                   jax.ShapeDtypeStruct((n, wd.shape[1]), F32)],
        compiler_params=_params(("parallel",)),
        name="in_proj",
    )(x2d, wa, wb, wc, wd, ba, bb, bc, bd)


def _proj_kernel(x_ref, w_ref, o_ref):
    o_ref[...] = _dot(x_ref[...].astype(BF16), w_ref[...])


def proj(x2d, w_bf16, tm):
    n, k = x2d.shape
    m = w_bf16.shape[1]
    return pl.pallas_call(
        _proj_kernel,
        grid=(n // tm,),
        in_specs=[pl.BlockSpec((tm, k), lambda i: (i, 0)), pl.BlockSpec((k, m), lambda i: (0, 0))],
        out_specs=pl.BlockSpec((tm, m), lambda i: (i, 0)),
        out_shape=jax.ShapeDtypeStruct((n, m), F32),
        compiler_params=_params(("parallel",)),
        name="mem_proj",
    )(x2d, w_bf16)


def _log_sigmoid(x):
    return jnp.minimum(x, 0.0) - jnp.log1p(jnp.exp(-jnp.abs(x)))


def _mlstm_kernel(q_ref, k_ref, v_ref, og_ref, gc_ref, gr_ref, c0_ref, n0_ref, m0_ref, ng_ref,
                  h_ref, ct_ref, nt_ref, mt_ref, c_s, n_s, m_s, *, L, t_valid, nb):
    ci = pl.program_id(1)

    @pl.when(ci == 0)
    def _():
        c_s[...] = c0_ref[...]
        n_s[...] = n0_ref[...]
        m_s[...] = m0_ref[...]

    row = lax.broadcasted_iota(jnp.int32, (L, L), 0)
    col = lax.broadcasted_iota(jnp.int32, (L, L), 1)
    causal = row >= col
    tri = causal.astype(F32)
    tri_t = (row <= col).astype(F32)
    rvalid = lax.broadcasted_iota(jnp.int32, (L, GATE_COLS), 0) < t_valid
    cvalid = lax.broadcasted_iota(jnp.int32, (2 * M_HEADS, L), 1) < t_valid
    scale = M_DH ** -0.5

    for bb, h in [(bb, h) for bb in range(nb) for h in range(M_HEADS)]:
        if h == 0:
            gc = gc_ref[bb]
            gr = gr_ref[bb]
            lf_c = jnp.where(rvalid, _log_sigmoid(gc), 0.0)
            lf_r = jnp.where(cvalid, _log_sigmoid(gr), 0.0)
            ig_c = jnp.where(rvalid, gc, NEG_INF)
            ig_r = jnp.where(cvalid, gr, NEG_INF)
            b_c = _dot(tri, lf_c, HIGHEST)
            b_r = _dot(lf_r, tri_t, HIGHEST)
        sl = slice(h * M_DH, (h + 1) * M_DH)
        q = q_ref[bb, :, sl]
        k = k_ref[bb, :, sl]
        v = v_ref[bb, :, sl]
        bc = b_c[:, M_HEADS + h:M_HEADS + h + 1]
        br = b_r[M_HEADS + h:M_HEADS + h + 1, :]
        igc = ig_c[:, h:h + 1]
        igr = ig_r[h:h + 1, :]
        m_prev = m_s[bb, h, 0:1, 0:1]
        c_prev = c_s[bb, h]
        n_prev = n_s[bb, h:h + 1, :]

        dmat = jnp.where(causal, bc - br + igr, NEG_INF)
        inter = bc + m_prev
        m_t = jnp.maximum(inter, jnp.max(dmat, axis=-1, keepdims=True))
        s = _dot_nt(q, k) * scale * jnp.exp(dmat - m_t)
        sc_in = jnp.exp(inter - m_t)
        qf = q.astype(F32)
        num = sc_in * _dot_nt(q, c_prev.astype(BF16)) + _dot(s.astype(BF16), v)
        den = sc_in * jnp.sum(qf * n_prev, axis=-1, keepdims=True) + jnp.sum(s, axis=-1, keepdims=True)
        hh = num / jnp.maximum(jnp.abs(den), jnp.exp(-m_t))

        b_last = bc[L - 1:L, :]
        dec_c = b_last - bc + igc
        dec_r = b_last - br + igr
        m_new = jnp.maximum(b_last + m_prev, jnp.max(dec_r, axis=-1, keepdims=True))
        ws_c = jnp.exp(dec_c - m_new) * scale
        sc = jnp.exp(b_last + m_prev - m_new)
        vf = v.astype(F32)
        kf = k.astype(F32)
        c_new = sc * c_prev + _dot_tn((vf * ws_c).astype(BF16), k)
        n_new = sc * n_prev + jnp.sum(kf * ws_c, axis=0, keepdims=True)
        c_s[bb, h] = c_new
        n_s[bb, h:h + 1, :] = n_new
        m_s[bb, h] = jnp.broadcast_to(m_new, m_s.shape[2:])

        og = og_ref[bb, :, sl].astype(F32)
        hh = hh * jax.nn.sigmoid(og)
        mu = jnp.mean(hh, axis=-1, keepdims=True)
        var = jnp.mean(jnp.square(hh - mu), axis=-1, keepdims=True)
        hn = (hh - mu) * lax.rsqrt(var + LN_EPS) * ng_ref[:, sl]
        h_ref[bb, :, sl] = hn.astype(h_ref.dtype)

    ct_ref[...] = c_s[...]
    nt_ref[...] = n_s[...]
    mt_ref[...] = m_s[...]


def mlstm(za, zd, c0, n0, m0, norm_g, *, L, t_valid):
    B, T, _ = za.shape
    nc = T // L
    gr = jnp.swapaxes(zd[:, :, :2 * M_HEADS], 1, 2)
    m0b = jnp.broadcast_to(m0.astype(F32)[:, :, None, None], (B, M_HEADS, SUBLANES, LANES))
    nb = MLSTM_SEQS if B % MLSTM_SEQS == 0 else 1
    colspec = lambda j: pl.BlockSpec((nb, L, M_WIDTH), lambda b, c, j=j: (b, c, j))
    st = lambda shp: pl.BlockSpec((nb,) + shp, lambda b, c: (b,) + (0,) * len(shp))
    kern = functools.partial(_mlstm_kernel, L=L, t_valid=t_valid, nb=nb)
    h, ct, nt, mt = pl.pallas_call(
        kern,
        grid=(B // nb, nc),
        in_specs=[colspec(0), colspec(1), colspec(2), colspec(3),
                  pl.BlockSpec((nb, L, GATE_COLS), lambda b, c: (b, c, 0)),
                  pl.BlockSpec((nb, 2 * M_HEADS, L), lambda b, c: (b, 0, c)),
                  st((M_HEADS, M_DH, M_DH)), st((M_HEADS, M_DH)), st((M_HEADS, SUBLANES, LANES)),
                  pl.BlockSpec((1, M_WIDTH), lambda b, c: (0, 0))],
        out_specs=[pl.BlockSpec((nb, L, M_WIDTH), lambda b, c: (b, c, 0)),
                   st((M_HEADS, M_DH, M_DH)), st((M_HEADS, M_DH)), st((M_HEADS, SUBLANES, LANES))],
        out_shape=[jax.ShapeDtypeStruct((B, T, M_WIDTH), BF16),
                   jax.ShapeDtypeStruct((B, M_HEADS, M_DH, M_DH), F32),
                   jax.ShapeDtypeStruct((B, M_HEADS, M_DH), F32),
                   jax.ShapeDtypeStruct((B, M_HEADS, SUBLANES, LANES), F32)],
        scratch_shapes=[pltpu.VMEM((nb, M_HEADS, M_DH, M_DH), F32), pltpu.VMEM((nb, M_HEADS, M_DH), F32),
                        pltpu.VMEM((nb, M_HEADS, SUBLANES, LANES), F32)],
        compiler_params=_params(("parallel", "arbitrary")),
        name="mlstm",
    )(za, za, za, za, zd, gr, c0.astype(F32), n0.astype(F32), m0b, norm_g.reshape(1, M_WIDTH).astype(F32))
    return h, ct, nt, mt[:, :, 0, 0]


def _mem_attn_kernel(q_ref, k_ref, v_ref, o_ref):
    scale = X_DH ** -0.5
    for h in range(X_HEADS):
        sl = slice(h * X_DH, (h + 1) * X_DH)
        k = k_ref[:, sl] if len(k_ref.shape) == 2 else k_ref[:, h, :]
        v = v_ref[:, sl] if len(v_ref.shape) == 2 else v_ref[:, h, :]
        s = _dot_nt(q_ref[0, :, sl], k.astype(BF16)) * scale
        mx = jnp.max(s, axis=-1, keepdims=True)
        p = jnp.exp(s - mx)
        p = p / jnp.sum(p, axis=-1, keepdims=True)
        o_ref[0, :, sl] = _dot(p.astype(BF16), v.astype(BF16)).astype(o_ref.dtype)


def mem_attn(zb, kv, layer, tq):
    B, T, _ = zb.shape
    if kv.ndim == 3:
        kv_specs = [pl.BlockSpec((None, MEM_LEN, X_WIDTH), lambda b, i, j=j: (b, 0, j)) for j in range(2)]
    else:
        kv_blk = (None, MEM_LEN, None, None, X_HEADS, X_DH)
        kv_specs = [pl.BlockSpec(kv_blk, lambda b, i, j=j: (b, 0, layer, j, 0, 0)) for j in range(2)]
    return pl.pallas_call(
        _mem_attn_kernel,
        grid=(B, T // tq),
        in_specs=[pl.BlockSpec((1, tq, X_WIDTH), lambda b, i: (b, i, NSA_QPAD // X_WIDTH))] + kv_specs,
        out_specs=pl.BlockSpec((1, tq, X_WIDTH), lambda b, i: (b, i, 0)),
        out_shape=jax.ShapeDtypeStruct((B, T, X_WIDTH), BF16),
        compiler_params=_params(("parallel", "parallel")),
        name="mem_attn",
    )(zb, kv, kv)


def _head_rows(q_ref, nq):
    parts = [q_ref[0, :, hd * LANES:(hd + 1) * LANES] for hd in range(NSA_HEADS)]
    if nq % 16:
        return jnp.concatenate([p.astype(F32) for p in parts], axis=0).astype(BF16)
    return jnp.concatenate(parts, axis=0)


def _store_heads(o_ref, gate, o_c, o_s, o_w, nq):
    lane = lax.broadcasted_iota(jnp.int32, (nq, LANES), 1)
    for hd in range(NSA_HEADS):
        r = slice(hd * nq, (hd + 1) * nq)
        gcol = lambda br: gate[:, NG_OFF + br * NSA_HEADS + hd:NG_OFF + br * NSA_HEADS + hd + 1]
        val = gcol(0) * o_c[r] + gcol(1) * o_s[r] + gcol(2) * o_w[r]
        keep = (lane >= NSA_DH) if hd // NSA_HPG == 1 else (lane < NSA_DH)
        o_ref[0, :, hd * LANES:(hd + 1) * LANES] = jnp.where(keep, val, 0.0).astype(o_ref.dtype)


def _overlap(tok, blk):
    c_start = tok * CMP_STRIDE
    s_start = blk * SEL_LEN
    return ((c_start < s_start + SEL_LEN) & (c_start + CMP_LEN > s_start)).astype(F32)


def _select_blocks(imp, tpos, n_sel, n_top):
    nq, W = imp.shape
    blk = lax.broadcasted_iota(jnp.int32, (nq, W), 1)
    cur = tpos // SEL_LEN
    forced = (blk == 0) | (blk == cur) | (blk == cur - 1)
    v = jnp.where(forced, jnp.inf, jnp.where(blk <= cur, imp, NEG_INF))
    v = jnp.where(blk < n_sel, v, NEG_INF)
    ahead = jnp.zeros((nq, W), F32)
    for j in range(n_sel):
        vj = v[:, j:j + 1]
        ahead = ahead + jnp.where(vj > v, 1.0, jnp.where(vj == v, jnp.where(blk > j, 1.0, 0.0), 0.0))
    return jnp.where((ahead < n_top) & (blk < n_sel), 1.0, 0.0)


def _select_blocks_t(imp_t, tpos_row, n_sel, n_top):
    n_blk, nq = imp_t.shape
    blk = lax.broadcasted_iota(jnp.int32, (n_blk, nq), 0)
    cur = tpos_row // SEL_LEN
    forced = (blk == 0) | (blk == cur) | (blk == cur - 1)
    v = jnp.where(forced, jnp.inf, jnp.where(blk <= cur, imp_t, NEG_INF))
    v = jnp.where(blk < n_sel, v, NEG_INF)
    n_rg = n_blk // SUBLANES
    vg = [v[rg * SUBLANES:(rg + 1) * SUBLANES] for rg in range(n_rg)]
    bg = [rg * SUBLANES + lax.broadcasted_iota(jnp.int32, (SUBLANES, nq), 0) for rg in range(n_rg)]

    ahead = [jnp.zeros((SUBLANES, nq), F32) for _ in range(n_rg)]
    for j in range(n_sel):
        vj = v[j:j + 1, :]
        for rg in range(n_rg):
            if rg * SUBLANES > j:
                inc = jnp.where(vj >= vg[rg], 1.0, 0.0)
            elif (rg + 1) * SUBLANES - 1 <= j:
                inc = jnp.where(vj > vg[rg], 1.0, 0.0)
            else:
                inc = jnp.where(vj > vg[rg], 1.0, jnp.where(vj == vg[rg], jnp.where(bg[rg] > j, 1.0, 0.0), 0.0))
            ahead[rg] = ahead[rg] + inc
    ahead = jnp.concatenate(ahead, axis=0)
    return jnp.where((ahead < n_top) & (blk < n_sel), 1.0, 0.0)


def _online_update(s, m_old, l_old):
    m_new = jnp.maximum(m_old, jnp.max(s, axis=-1, keepdims=True))
    alpha = jnp.exp2(m_old - m_new)
    p = jnp.exp2(s - m_new)
    return p, m_new, alpha, alpha * l_old + jnp.sum(p, axis=-1, keepdims=True)


def _cmp_tokens_kernel(xk_ref, xv_ref, bd_ref, b_ref, kc_ref, vc_ref, *, nsub):
    for c, (x_ref, o_ref) in enumerate(((xk_ref, kc_ref), (xv_ref, vc_ref))):
        a0 = jnp.zeros((nsub, LANES), F32)
        a1 = jnp.zeros((nsub, LANES), F32)
        for j in range(CMP_STRIDE):
            xj = x_ref[0, :, j, :].astype(BF16)
            a0 = a0 + _dot(xj, bd_ref[c, 0, j])
            a1 = a1 + _dot(xj, bd_ref[c, 1, j])
        tok = a0 + pltpu.roll(a1, nsub - 1, axis=0) + b_ref[c]
        o_ref[0] = tok.astype(o_ref.dtype)


def _cmp_blockdiag(cmp_w_l):
    R = CMP_LEN // CMP_STRIDE
    w = cmp_w_l.astype(F32).reshape(2, R, CMP_STRIDE, NSA_DH, NSA_DH)
    eye = jnp.eye(NSA_GROUPS, dtype=F32)
    return jnp.einsum("ab,crjde->crjadbe", eye, w).reshape(2, R, CMP_STRIDE, LANES, LANES).astype(BF16)


def _cmp_bias(cmp_b_l):
    return jnp.tile(cmp_b_l.astype(F32), (1, NSA_GROUPS)).reshape(2, 1, LANES)


def cmp_tokens(zc, cmp_w_l, cmp_b_l):
    B, T, W = zc.shape
    nsub = T // CMP_STRIDE
    x4 = zc.reshape(B, nsub, CMP_STRIDE, W)
    bd = _cmp_blockdiag(cmp_w_l)
    bias = _cmp_bias(cmp_b_l)
    spec = lambda j: pl.BlockSpec((1, nsub, CMP_STRIDE, LANES), lambda b, j=j: (b, 0, 0, j))
    return pl.pallas_call(
        functools.partial(_cmp_tokens_kernel, nsub=nsub),
        grid=(B,),
        in_specs=[spec(0), spec(1), pl.BlockSpec(bd.shape, lambda b: (0,) * 5), pl.BlockSpec(bias.shape, lambda b: (0, 0, 0))],
        out_specs=[pl.BlockSpec((1, nsub, LANES), lambda b: (b, 0, 0))] * 2,
        out_shape=[jax.ShapeDtypeStruct((B, nsub, LANES), BF16)] * 2,
        compiler_params=_params(("parallel",)),
        name="nsa_cmp_tokens",
    )(x4, x4, bd, bias)


def _nsa_prompt_kernel(q_ref, g_ref, kca_ref, vc_ref, ka_ref, vst_ref, kwa_ref, vw_ref, qf_ref, o_ref,
                       qaug_s, s_buf, p_buf, w_buf, pw_buf, pc_s, m_s, l_s, a_s, acc_s, oc_s, ow_s, *, tq, T, CH):
    start = pl.program_id(1) * tq
    n_cmp_rows = kca_ref.shape[1]
    n_sel = T // SEL_LEN
    n_top = min(SEL_TOPN, n_sel)
    n_blk = -(-n_sel // SUBLANES) * SUBLANES
    nq_all = NSA_HEADS * tq
    qw = nq_all // SEL_GROUPS
    tpos_row = start + lax.broadcasted_iota(jnp.int32, (1, tq), 1)
    tpos_all = jnp.concatenate([tpos_row] * NSA_HEADS, axis=1)
    qcols = lambda qi: slice(qi * qw, (qi + 1) * qw)
    tiles = [(qi, ct, slice(qi * qw + ct * LANES, qi * qw + (ct + 1) * LANES), slice(ct * LANES, (ct + 1) * LANES))
             for qi in range(SEL_GROUPS) for ct in range(qw // LANES)]

    qaug_s[:, 0:LANES] = _head_rows(q_ref, tq)
    for hd in range(NSA_HEADS):
        qaug_s[hd * tq:(hd + 1) * tq, LANES:2 * LANES] = jnp.broadcast_to(qf_ref[hd:hd + 1, :], (tq, LANES)).astype(BF16)

    def softmax_tile(s_t):
        mx = jnp.max(s_t, axis=0, keepdims=True)
        mx = jnp.where(mx > NEG_INF, mx, 0.0)
        p_t = jnp.exp2(s_t - mx)
        return p_t, jnp.maximum(jnp.sum(p_t, axis=0, keepdims=True), 1e-30)

    c_end = lax.broadcasted_iota(jnp.int32, (n_cmp_rows, 1), 0) * CMP_STRIDE + CMP_LEN - 1
    for qi in range(SEL_GROUPS):
        s_buf[qi, 0:n_cmp_rows, :] = _dot_nt(kca_ref[0], qaug_s[qcols(qi), :])
    for qi, ct, cols, tc in tiles:
        p_t, den = softmax_tile(jnp.where(c_end <= tpos_all[:, cols], s_buf[qi, 0:n_cmp_rows, tc], NEG_INF))
        pc_s[:, cols] = p_t / den
    oc_s[...] = _dot_tn(vc_ref[0], pc_s[...].astype(BF16))
    ov_t = _overlap(lax.broadcasted_iota(jnp.int32, (n_blk, n_cmp_rows), 1),
                    lax.broadcasted_iota(jnp.int32, (n_blk, n_cmp_rows), 0))
    lane = lax.broadcasted_iota(jnp.int32, (tq, LANES), 1)
    feat_sel = [None] * NSA_HEADS
    for g in range(NSA_GROUPS):
        pg = pc_s[:, g * NSA_HPG * tq:(g * NSA_HPG + 1) * tq]
        for p in range(1, NSA_HPG):
            pg = pg + pc_s[:, (g * NSA_HPG + p) * tq:(g * NSA_HPG + p + 1) * tq]
        sel_t = _select_blocks_t(_dot(ov_t, pg, HIGHEST), tpos_row, n_sel, n_top)
        sel = jnp.concatenate([sel_t, jnp.zeros((LANES - n_blk, tq), F32)], axis=0).T
        unsel = jnp.where(lane < SEL_FEAT, (1.0 - sel) * MASK_BIAS, 0.0)
        for p in range(NSA_HPG):
            hd = g * NSA_HPG + p
            feat_sel[hd] = (unsel + qf_ref[hd:hd + 1, :]).astype(BF16)

    wk = WINDOW + tq
    base_w = pl.multiple_of(start, tq)
    wpos = start - WINDOW + lax.broadcasted_iota(jnp.int32, (wk, 1), 0)
    for qi in range(SEL_GROUPS):
        w_buf[qi] = _dot_nt(kwa_ref[0, pl.ds(base_w, wk), :], qaug_s[qcols(qi), :])
    for qi, ct, cols, tc in tiles:
        tp = tpos_all[:, cols]
        first = jnp.maximum(tp - (WINDOW - 1), 0)
        s_t = jnp.where(wpos >= first, jnp.where(wpos <= tp, w_buf[qi, :, tc], NEG_INF), NEG_INF)
        p_t, den = softmax_tile(s_t)
        pw_buf[qi, :, tc] = p_t.astype(BF16)
        a_s[:, cols] = den
    for qi in range(SEL_GROUPS):
        ow_s[:, qcols(qi)] = _dot_tn(vw_ref[0, pl.ds(base_w, wk), :], pw_buf[qi]) / a_s[:, qcols(qi)]

    for hd in range(NSA_HEADS):
        qaug_s[hd * tq:(hd + 1) * tq, LANES:2 * LANES] = feat_sel[hd]

    m_s[...] = jnp.full(m_s.shape, NEG_INF, F32)
    l_s[...] = jnp.zeros(l_s.shape, F32)
    acc_s[...] = jnp.zeros(acc_s.shape, F32)

    def issue(c, qi):
        base = pl.multiple_of(c * CH, CH)
        s_buf[qi] = _dot_nt(ka_ref[0, pl.ds(base, CH), :], qaug_s[qi * qw:(qi + 1) * qw, :])

    def absorb(c, qi, diag):
        for ct in range(qw // LANES):
            cols = slice(qi * qw + ct * LANES, qi * qw + (ct + 1) * LANES)
            s_t = s_buf[qi, :, ct * LANES:(ct + 1) * LANES]
            if diag:
                kpos = c * CH + lax.broadcasted_iota(jnp.int32, (CH, 1), 0)
                s_t = jnp.where(kpos <= tpos_all[:, cols], s_t, NEG_INF)
            m_old = m_s[:, cols]
            m_new = jnp.maximum(m_old, jnp.max(s_t, axis=0, keepdims=True))
            alpha = jnp.exp2(m_old - m_new)
            p_t = jnp.exp2(s_t - m_new)
            m_s[:, cols] = m_new
            a_s[:, cols] = alpha
            l_s[:, cols] = alpha * l_s[:, cols] + jnp.sum(p_t, axis=0, keepdims=True)
            p_buf[qi, :, ct * LANES:(ct + 1) * LANES] = p_t.astype(BF16)
        cols = slice(qi * qw, (qi + 1) * qw)
        acc_s[:, cols] = a_s[:, cols] * acc_s[:, cols] + _dot(vst_ref[0, c], p_buf[qi])

    def full_chunk(c, carry):
        for qi in range(SEL_GROUPS):
            absorb(c, qi, False)
            issue(c + 1, qi)
        return carry

    n_full = start // CH
    for qi in range(SEL_GROUPS):
        issue(0, qi)
    lax.fori_loop(0, n_full, full_chunk, 0)
    for qi in range(SEL_GROUPS):
        absorb(n_full, qi, True)

    gate_t = jax.nn.sigmoid(g_ref[0]).T
    for hd in range(NSA_HEADS):
        cols = slice(hd * tq, (hd + 1) * tq)
        grow = lambda br: gate_t[NG_OFF + br * NSA_HEADS + hd:NG_OFF + br * NSA_HEADS + hd + 1, :]
        o_s = acc_s[:, cols] / jnp.maximum(l_s[:, cols], 1e-30)
        val = (grow(0) * oc_s[:, cols] + grow(1) * o_s + grow(2) * ow_s[:, cols]).T
        keep = (lane >= NSA_DH) if hd // NSA_HPG == 1 else (lane < NSA_DH)
        o_ref[0, :, hd * LANES:(hd + 1) * LANES] = jnp.where(keep, val, 0.0).astype(o_ref.dtype)


def _bf16_parts(x, n):
    parts = []
    for _ in range(n):
        p = float(np.asarray(x, np.float32).astype(jnp.bfloat16).astype(np.float32))
        parts.append(p)
        x = x - p
    return parts


def _slope_features():
    qf = np.zeros((NSA_HEADS, LANES), np.float32)
    for hd in range(NSA_HEADS):
        for p, s_p in enumerate(_bf16_parts(NSA_SLOPES[hd], SLOPE_PARTS)):
            qf[hd, SEL_FEAT + 2 * p] = s_p * SEL_LEN
            qf[hd, SEL_FEAT + 2 * p + 1] = s_p
    return jnp.asarray(qf)


def _key_features(pos, block_onehot):
    pos = np.asarray(pos)
    ok = pos >= 0
    kf = np.zeros((pos.shape[0], LANES), np.float32)
    if block_onehot:
        kf[np.arange(pos.shape[0])[ok], pos[ok] // SEL_LEN] = 1.0
    for p in range(SLOPE_PARTS):
        kf[ok, SEL_FEAT + 2 * p] = pos[ok] // SEL_LEN
        kf[ok, SEL_FEAT + 2 * p + 1] = pos[ok] % SEL_LEN
    return jnp.asarray(kf, BF16)


def nsa_prompt(zb, zd, zcb, kc, vc, tq, CH):
    B, T, _ = zb.shape
    nsub = kc.shape[1]
    assert T % CH == 0 and CH % tq == 0 and T // SEL_LEN <= SEL_FEAT and nsub <= CH
    qf = _slope_features()
    with_feats = lambda k, feats: jnp.concatenate([k, jnp.broadcast_to(feats[None], (B,) + feats.shape)], axis=2)
    k_aug = with_feats(zcb[:, :, 2 * NSA_KV:3 * NSA_KV], _key_features(np.arange(T), True))
    kc_aug = with_feats(kc, _key_features(np.arange(nsub) * CMP_STRIDE + CMP_LEN - 1, False))
    kw = jnp.pad(zcb[:, :, 4 * NSA_KV:5 * NSA_KV], ((0, 0), (WINDOW, 0), (0, 0)))
    kw_aug = with_feats(kw, _key_features(np.arange(T + WINDOW) - WINDOW, False))
    vs_t = jnp.swapaxes(zcb[:, :, 3 * NSA_KV:4 * NSA_KV].reshape(B, T // CH, CH, NSA_KV), 2, 3)
    vw = jnp.pad(zcb[:, :, 5 * NSA_KV:6 * NSA_KV], ((0, 0), (WINDOW, 0), (0, 0)))
    per_b = lambda rows_, w: pl.BlockSpec((1, rows_, w), lambda b, i: (b, 0, 0))
    kern = functools.partial(_nsa_prompt_kernel, tq=tq, T=T, CH=CH)
    nq_all = NSA_HEADS * tq
    qw = nq_all // SEL_GROUPS
    wk = WINDOW + tq
    assert qw % LANES == 0
    return pl.pallas_call(
        kern,
        grid=(B, T // tq),
        in_specs=[pl.BlockSpec((1, tq, NSA_QPAD), lambda b, i: (b, i, 0)),
                  pl.BlockSpec((1, tq, GATE_COLS), lambda b, i: (b, i, 0)),
                  per_b(nsub, 2 * LANES), per_b(nsub, LANES),
                  per_b(T, 2 * LANES),
                  pl.BlockSpec((1, T // CH, NSA_KV, CH), lambda b, i: (b, 0, 0, 0)),
                  per_b(T + WINDOW, 2 * LANES), per_b(T + WINDOW, LANES),
                  pl.BlockSpec(qf.shape, lambda b, i: (0, 0))],
        out_specs=pl.BlockSpec((1, tq, NSA_QPAD), lambda b, i: (b, i, 0)),
        out_shape=jax.ShapeDtypeStruct((B, T, NSA_QPAD), BF16),
        scratch_shapes=[pltpu.VMEM((nq_all, 2 * LANES), BF16),
                        pltpu.VMEM((SEL_GROUPS, CH, qw), F32), pltpu.VMEM((SEL_GROUPS, CH, qw), BF16),
                        pltpu.VMEM((SEL_GROUPS, wk, qw), F32), pltpu.VMEM((SEL_GROUPS, wk, qw), BF16),
                        pltpu.VMEM((nsub, nq_all), F32),
                        pltpu.VMEM((1, nq_all), F32), pltpu.VMEM((1, nq_all), F32), pltpu.VMEM((1, nq_all), F32),
                        pltpu.VMEM((NSA_KV, nq_all), F32), pltpu.VMEM((NSA_KV, nq_all), F32),
                        pltpu.VMEM((NSA_KV, nq_all), F32)],
        compiler_params=_params(("parallel", "arbitrary")),
        name="nsa_prompt",
    )(zb, zd, kc_aug, vc, k_aug, vs_t, kw_aug, vw, qf)


def _page_specs(shape, slot_blk, layer, n_pages):
    def mk(i):
        def imap(b, s, pt):
            return (pt[b * n_pages + s * PAGES_PER_STEP + i], layer, slot_blk, 0, 0)
        return pl.BlockSpec(shape, imap)
    return [mk(i) for i in range(PAGES_PER_STEP)]


def _nsa_dec_cmp_kernel(pt_ref, *refs, T, past_len, n_rows):
    pages = refs[:PAGES_PER_STEP]
    perm_ref, bd_ref, b_ref, q_ref, oc_ref, sel_ref, tok_s, pend_s = refs[PAGES_PER_STEP:]
    s = pl.program_id(1)
    n_steps = pl.num_programs(1)
    page_rows = pages[0].shape[-1]
    sub = page_rows // CMP_STRIDE
    R = PAGES_PER_STEP * sub

    @pl.when(s == 0)
    def _():
        pend_s[...] = jnp.zeros(pend_s.shape, F32)

    W2 = 2 * LANES
    rid = lax.broadcasted_iota(jnp.int32, (R, W2), 0)
    xs = [_dot_nt(perm_ref[...], pg[0, 0].reshape(W2, page_rows).astype(BF16)) for pg in pages]
    a0 = jnp.zeros((R, W2), F32)
    a1 = jnp.zeros((R, W2), F32)
    for j in range(CMP_STRIDE):
        xj = jnp.concatenate([x[j * sub:(j + 1) * sub] for x in xs], axis=0).astype(BF16)
        a0 = a0 + _dot(xj, bd_ref[0, j])
        a1 = a1 + _dot(xj, bd_ref[1, j])
    a0 = a0 + b_ref[...]
    tok_s[pl.ds(pl.multiple_of(s * R, R), R), :] = jnp.where(rid == 0, pend_s[...], pltpu.roll(a0, 1, axis=0)) + a1
    pend_s[...] = a0[R - 1:R, :]

    @pl.when(s == n_steps - 1)
    def _():
        qp = _head_rows(q_ref, T)
        kc = tok_s[:, 0:LANES].astype(BF16)
        vc = tok_s[:, LANES:W2].astype(BF16)
        sc_all = _dot_nt(qp, kc)
        tpos = past_len + lax.broadcasted_iota(jnp.int32, (T, 1), 0)
        r = lax.broadcasted_iota(jnp.int32, (T, n_rows), 1)
        dc = tpos - ((r - 1) * CMP_STRIDE + CMP_LEN - 1)
        mask_c = (dc >= 0) & (r >= 1)
        dcf = dc.astype(F32)
        pcs = [_masked_softmax2(sc_all[hd * T:(hd + 1) * T] - NSA_SLOPES[hd] * dcf, mask_c) for hd in range(NSA_HEADS)]
        oc_ref[0] = _dot(jnp.concatenate(pcs, axis=0).astype(BF16), vc)
        W = sel_ref.shape[2]
        n_sel = -(-(past_len + T) // SEL_LEN)
        ov = _overlap(lax.broadcasted_iota(jnp.int32, (n_rows, W), 0) - 1, lax.broadcasted_iota(jnp.int32, (n_rows, W), 1))
        for g in range(NSA_GROUPS):
            pg = pcs[g * NSA_HPG]
            for p in range(1, NSA_HPG):
                pg = pg + pcs[g * NSA_HPG + p]
            imp = _dot(pg, ov, HIGHEST)
            sel_ref[0, g * T:(g + 1) * T, :] = _select_blocks(imp, tpos, n_sel, min(SEL_TOPN, n_sel))


def _nsa_dec_sel_kernel(pt_ref, *refs, T, past_len):
    pages = refs[:PAGES_PER_STEP]
    q_ref, sel_ref, sel_last_ref, ex_ref, kn_ref, vn_ref, os_ref, m_s, l_s, acc_s = refs[PAGES_PER_STEP:]
    s = pl.program_id(1)
    n_steps = pl.num_programs(1)
    page_rows = pages[0].shape[-1]
    CH = PAGES_PER_STEP * page_rows

    @pl.when(s == 0)
    def _():
        m_s[...] = jnp.full(m_s.shape, NEG_INF, F32)
        l_s[...] = jnp.zeros(l_s.shape, F32)
        acc_s[...] = jnp.zeros(acc_s.shape, F32)

    qp = _head_rows(q_ref, T)
    tpos = past_len + lax.broadcasted_iota(jnp.int32, (T, 1), 0)

    def update(s_all, mask_of_group, kpos, pv):
        ds = tpos - kpos
        dsf = ds.astype(F32)
        ps = []
        for g in range(NSA_GROUPS):
            mk = mask_of_group(g) & (ds >= 0)
            for p in range(NSA_HPG):
                hd = g * NSA_HPG + p
                r = slice(hd * T, (hd + 1) * T)
                sc = jnp.where(mk, s_all[r] - NSA_SLOPES[hd] * dsf, NEG_INF)
                pr, m_new, alpha, l_new = _online_update(sc, m_s[r], l_s[r])
                m_s[r] = m_new
                l_s[r] = l_new
                acc_s[r] = alpha * acc_s[r]
                ps.append(pr)
        acc_s[...] = acc_s[...] + pv(jnp.concatenate(ps, axis=0).astype(BF16))

    selk = _dot(sel_ref[0, 0].astype(BF16), ex_ref[...])
    pairs = [(pages[i], pages[i + 1]) for i in range(0, PAGES_PER_STEP, 2)]
    side_by_side = lambda a, b, slot: jnp.concatenate([a[0, 0, slot], b[0, 0, slot]], axis=1).astype(BF16)
    s_all = jnp.concatenate([_dot(qp, side_by_side(a, b, 0)) for a, b in pairs], axis=1)
    kpos = s * CH + lax.broadcasted_iota(jnp.int32, (T, CH), 1)

    def pv_pages(pmat):
        out = jnp.zeros((NSA_HEADS * T, LANES), F32)
        for i, (a, b) in enumerate(pairs):
            out = out + _dot_nt(pmat[:, 2 * i * page_rows:2 * (i + 1) * page_rows], side_by_side(a, b, 1))
        return out

    update(s_all, lambda g: selk[g * T:(g + 1) * T] > 0.5, kpos, pv_pages)

    @pl.when(s == n_steps - 1)
    def _():
        nk = kn_ref.shape[1]
        sn = _dot_nt(qp, kn_ref[0])
        lane = lax.broadcasted_iota(jnp.int32, (T, nk), 1)
        sl = sel_last_ref[0, 0]
        update(sn, lambda g: (sl[g * T:(g + 1) * T, 0:1] > 0.5) & (lane < T), past_len + lane,
               lambda pmat: _dot(pmat, vn_ref[0]))
        os_ref[0] = acc_s[...] / jnp.maximum(l_s[...], 1e-30)


def _nsa_dec_win_kernel(q_ref, g_ref, oc_ref, os_ref, wp_ref, kwn_ref, vwn_ref, o_ref, *, T, past_len):
    qp = _head_rows(q_ref, T)
    w_src = wp_ref.shape[-1]
    nk = kwn_ref.shape[1]
    sw_all = jnp.concatenate([_dot(qp, wp_ref[0, 0, 0].astype(BF16)), _dot_nt(qp, kwn_ref[0])], axis=1)
    j = lax.broadcasted_iota(jnp.int32, (T, w_src + nk), 1)
    tpos = past_len + lax.broadcasted_iota(jnp.int32, (T, 1), 0)
    wpos = past_len - w_src + j
    dw = tpos - wpos
    mask_w = (dw >= 0) & (dw < WINDOW) & (wpos >= 0) & (j < w_src + T)
    dwf = dw.astype(F32)
    pws = [_masked_softmax2(sw_all[hd * T:(hd + 1) * T] - NSA_SLOPES[hd] * dwf, mask_w) for hd in range(NSA_HEADS)]
    pw = jnp.concatenate(pws, axis=0).astype(BF16)
    o_w = _dot_nt(pw[:, :w_src], wp_ref[0, 0, 1].astype(BF16)) + _dot(pw[:, w_src:], vwn_ref[0])
    _store_heads(o_ref, jax.nn.sigmoid(g_ref[0]), oc_ref[0], os_ref[0], o_w, T)


def nsa_decode(zb, zd, zcb, cache_nsa_kv, cache_win_kv, page_table, cmp_w_l, cmp_b_l, layer):
    DB, T, _ = zb.shape
    n_pool, page_rows = cache_nsa_kv.shape[:2]
    n_pages = page_table.shape[1]
    past_len = n_pages * page_rows
    n_steps = n_pages // PAGES_PER_STEP
    sub_per_page = page_rows // CMP_STRIDE
    n_rows = n_pages * sub_per_page
    assert (n_rows - 1) * CMP_STRIDE + CMP_LEN - 1 > past_len + T - 1
    assert T <= CMP_STRIDE and n_pages % PAGES_PER_STEP == 0 and page_rows == LANES
    pt = page_table.reshape(-1).astype(jnp.int32)
    cache_t = jnp.transpose(cache_nsa_kv, (0, 2, 3, 4, 5, 1)).reshape(n_pool, DEPTH, 4, NSA_KV, page_rows)
    page_blk = (1, 1, 2, NSA_KV, page_rows)

    rr = np.arange(page_rows)
    perm = jnp.asarray(rr[None, :] == ((rr % sub_per_page) * CMP_STRIDE + rr // sub_per_page)[:, None], BF16)
    bd_kv = _cmp_blockdiag(cmp_w_l)
    zero = jnp.zeros_like(bd_kv[0])
    bd = jnp.concatenate([jnp.concatenate([bd_kv[0], zero], axis=-1), jnp.concatenate([zero, bd_kv[1]], axis=-1)], axis=-2)
    bias = _cmp_bias(cmp_b_l).reshape(1, 2 * LANES)

    n_sel = -(-(past_len + T) // SEL_LEN)
    blocks_per_step = PAGES_PER_STEP * page_rows // SEL_LEN
    sel_used = (n_steps + 1) * blocks_per_step
    sel_w = -(-sel_used // LANES) * LANES
    assert sel_used >= n_sel and blocks_per_step <= LANES
    q_spec = pl.BlockSpec((1, T, NSA_QPAD), lambda b, s, pt: (b, 0, 0))
    const = lambda a: pl.BlockSpec(a.shape, lambda b, s, pt: (0,) * a.ndim)

    o_c, sel = pl.pallas_call(
        functools.partial(_nsa_dec_cmp_kernel, T=T, past_len=past_len, n_rows=n_rows),
        grid_spec=pltpu.PrefetchScalarGridSpec(
            num_scalar_prefetch=1, grid=(DB, n_steps),
            in_specs=_page_specs(page_blk, 0, layer, n_pages) + [const(perm), const(bd), const(bias), q_spec],
            out_specs=[pl.BlockSpec((1, NSA_HEADS * T, LANES), lambda b, s, pt: (b, 0, 0)),
                       pl.BlockSpec((1, NSA_GROUPS * T, sel_w), lambda b, s, pt: (b, 0, 0))],
            scratch_shapes=[pltpu.VMEM((n_rows, 2 * LANES), F32), pltpu.VMEM((1, 2 * LANES), F32)]),
        out_shape=[jax.ShapeDtypeStruct((DB, NSA_HEADS * T, LANES), F32),
                   jax.ShapeDtypeStruct((DB, NSA_GROUPS * T, sel_w), F32)],
        compiler_params=_params(("parallel", "arbitrary")),
        name="nsa_dec_cmp",
    )(pt, *([cache_t] * PAGES_PER_STEP), perm, bd, bias, zb)

    sel_steps = sel[:, :, :sel_used].reshape(DB, NSA_GROUPS * T, n_steps + 1, blocks_per_step).transpose(0, 2, 1, 3)
    sel_steps = jnp.pad(sel_steps, ((0, 0), (0, 0), (0, 0), (0, LANES - blocks_per_step)))
    kk = np.arange(PAGES_PER_STEP * page_rows) // SEL_LEN
    expand = jnp.asarray(kk[None, :] == np.arange(LANES)[:, None], BF16)
    pad_rows = LANES - T
    new_rows = lambda slot: jnp.pad(zcb[:, :, slot * NSA_KV:(slot + 1) * NSA_KV], ((0, 0), (0, pad_rows), (0, 0)))
    new_spec = pl.BlockSpec((1, LANES, LANES), lambda b, s, pt: (b, 0, 0))
    sel_blk = (1, 1, NSA_GROUPS * T, LANES)

    o_s = pl.pallas_call(
        functools.partial(_nsa_dec_sel_kernel, T=T, past_len=past_len),
        grid_spec=pltpu.PrefetchScalarGridSpec(
            num_scalar_prefetch=1, grid=(DB, n_steps),
            in_specs=_page_specs(page_blk, 1, layer, n_pages)
            + [q_spec, pl.BlockSpec(sel_blk, lambda b, s, pt: (b, s, 0, 0)),
               pl.BlockSpec(sel_blk, lambda b, s, pt: (b, n_steps, 0, 0)), const(expand), new_spec, new_spec],
            out_specs=pl.BlockSpec((1, NSA_HEADS * T, LANES), lambda b, s, pt: (b, 0, 0)),
            scratch_shapes=[pltpu.VMEM((NSA_HEADS * T, 1), F32), pltpu.VMEM((NSA_HEADS * T, 1), F32),
                            pltpu.VMEM((NSA_HEADS * T, LANES), F32)]),
        out_shape=jax.ShapeDtypeStruct((DB, NSA_HEADS * T, LANES), F32),
        compiler_params=_params(("parallel", "arbitrary")),
        name="nsa_dec_sel",
    )(pt, *([cache_t] * PAGES_PER_STEP), zb, sel_steps, sel_steps, expand, new_rows(2), new_rows(3))

    w_src = cache_win_kv.shape[1]
    win_t = jnp.transpose(cache_win_kv, (0, 2, 3, 4, 5, 1)).reshape(DB, DEPTH, 2, NSA_KV, w_src)
    b3 = lambda shp: pl.BlockSpec(shp, lambda b: (b, 0, 0))
    return pl.pallas_call(
        functools.partial(_nsa_dec_win_kernel, T=T, past_len=past_len),
        grid=(DB,),
        in_specs=[b3((1, T, NSA_QPAD)), b3((1, T, GATE_COLS)), b3((1, NSA_HEADS * T, LANES)), b3((1, NSA_HEADS * T, LANES)),
                  pl.BlockSpec((1, 1, 2, NSA_KV, w_src), lambda b: (b, layer, 0, 0, 0)),
                  b3((1, LANES, LANES)), b3((1, LANES, LANES))],
        out_specs=b3((1, T, NSA_QPAD)),
        out_shape=jax.ShapeDtypeStruct((DB, T, NSA_QPAD), BF16),
        compiler_params=_params(("parallel",)),
        name="nsa_dec_win",
    )(zb, zd, o_c, o_s, win_t, new_rows(4), new_rows(5))


def _route_top2(logits):
    lane = lax.broadcasted_iota(jnp.int32, logits.shape, 1)
    W = logits.shape[1]
    logits = jnp.where(lane < N_EXPERTS, logits, NEG_INF)
    m1 = jnp.max(logits, axis=-1, keepdims=True)
    i1 = jnp.min(jnp.where(logits == m1, lane, W), axis=-1, keepdims=True)
    rest = jnp.where(lane == i1, NEG_INF, logits)
    m2 = jnp.max(rest, axis=-1, keepdims=True)
    i2 = jnp.min(jnp.where(rest == m2, lane, W), axis=-1, keepdims=True)
    e2 = jnp.exp(m2 - m1)
    den = 1.0 + e2
    return jnp.where(lane == i1, 1.0 / den, 0.0) + jnp.where(lane == i2, e2 / den, 0.0)


def _router_kernel(x_ref, w_ref, b_ref, o_ref):
    x = x_ref[...]
    w = w_ref[...]
    x_hi = x.astype(BF16)
    x_lo = (x - x_hi.astype(F32)).astype(BF16)
    w_hi = w.astype(BF16)
    w_lo = (w - w_hi.astype(F32)).astype(BF16)
    logits = _dot(x_hi, w_hi) + _dot(x_hi, w_lo) + _dot(x_lo, w_hi)
    o_ref[...] = _route_top2(logits + b_ref[...])


def router(x2d, w_router, b_router, tm):
    n = x2d.shape[0]
    w = jnp.pad(w_router.astype(F32), ((0, 0), (0, LANES - N_EXPERTS)))
    bb = jnp.pad(b_router.astype(F32), (0, LANES - N_EXPERTS)).reshape(1, LANES)
    return pl.pallas_call(
        _router_kernel,
        grid=(n // tm,),
        in_specs=[pl.BlockSpec((tm, D_MODEL), lambda i: (i, 0)), pl.BlockSpec(w.shape, lambda i: (0, 0)),
                  pl.BlockSpec(bb.shape, lambda i: (0, 0))],
        out_specs=pl.BlockSpec((tm, LANES), lambda i: (i, 0)),
        out_shape=jax.ShapeDtypeStruct((n, LANES), F32),
        compiler_params=_params(("parallel",)),
        name="router",
    )(x2d, w, bb)


def _merge_kernel(x_ref, hm_ref, hn_ref, hx_ref, wg_ref, bg_ref, wbm_ref, wbn_ref, wbx_ref, wo_ref, g_ref, b_ref,
                  o_ref):
    x = x_ref[...]
    xb = x.astype(BF16)
    merged = None
    for c, (h_ref, w_ref) in enumerate(((hm_ref, wbm_ref), (hn_ref, wbn_ref), (hx_ref, wbx_ref))):
        sl = slice(c * D_MODEL, (c + 1) * D_MODEL)
        gate = jax.nn.sigmoid(_dot(xb, wg_ref[:, sl]) + bg_ref[:, sl])
        term = gate * _dot(h_ref[...], w_ref[...])
        merged = term if merged is None else merged + term
    mix = _dot(merged.astype(BF16), wo_ref[...])
    o_ref[...] = _layer_norm(ALPHA * x + mix, g_ref[...], b_ref[...])


def merge(x2d, hm, hn, hx, wg, bg, wbr, wo, g, b, tm):
    n = x2d.shape[0]
    row = lambda w: pl.BlockSpec((tm, w), lambda i: (i, 0))
    full = lambda a: pl.BlockSpec(a.shape, lambda i: (0,) * a.ndim, pipeline_mode=pl.Buffered(1))
    args = [x2d, hm, hn, hx, wg, bg, wbr[0], wbr[1], wbr[2], wo, g, b]
    return pl.pallas_call(
        _merge_kernel,
        grid=(n // tm,),
        in_specs=[row(a.shape[1]) for a in args[:4]] + [full(a) for a in args[4:]],
        out_specs=row(D_MODEL),
        out_shape=jax.ShapeDtypeStruct((n, D_MODEL), F32),
        compiler_params=_params(("parallel",)),
        name="merge",
    )(*args)


def _ffn_kernel(x_ref, wg_ref, wu_ref, wd_ref, g_ref, b_ref, o_ref, acc_s):
    f = pl.program_id(1)

    @pl.when(f == 0)
    def _():
        acc_s[...] = jnp.zeros(acc_s.shape, F32)

    xb = x_ref[...].astype(BF16)
    hcur = jax.nn.silu(_dot(xb, wg_ref[...])) * _dot(xb, wu_ref[...])
    acc_s[...] += _dot(hcur.astype(BF16), wd_ref[...])

    @pl.when(f == pl.num_programs(1) - 1)
    def _():
        o_ref[...] = _layer_norm(ALPHA * x_ref[...] + acc_s[...], g_ref[...], b_ref[...])


def ffn(x2d, w_up, w_down, g, b, tm, tf):
    n = x2d.shape[0]
    nf = (w_up.shape[1] // 2) // tf
    wmode = dict(pipeline_mode=pl.Buffered(1)) if nf == 1 else {}
    return pl.pallas_call(
        _ffn_kernel,
        grid=(n // tm, nf),
        in_specs=[pl.BlockSpec((tm, D_MODEL), lambda i, f: (i, 0)),
                  pl.BlockSpec((D_MODEL, tf), lambda i, f: (0, f), **wmode),
                  pl.BlockSpec((D_MODEL, tf), lambda i, f: (0, nf + f), **wmode),
                  pl.BlockSpec((tf, D_MODEL), lambda i, f: (f, 0), **wmode),
                  pl.BlockSpec((1, D_MODEL), lambda i, f: (0, 0)),
                  pl.BlockSpec((1, D_MODEL), lambda i, f: (0, 0))],
        out_specs=pl.BlockSpec((tm, D_MODEL), lambda i, f: (i, 0)),
        out_shape=jax.ShapeDtypeStruct((n, D_MODEL), F32),
        scratch_shapes=[pltpu.VMEM((tm, D_MODEL), F32)],
        compiler_params=_params(("parallel", "arbitrary")),
        name="dense_ffn",
    )(x2d, w_up, w_up, w_down, g, b)


def _moe_kernel(np_ref, x_ref, cw_ref, cwt_ref, wg_ref, wu_ref, wd_ref, g_ref, b_ref, o_ref,
                xb_s, y_s, pos_s, post_s, xg_s, ws_s, acc_s, *, blk, cap, cap_pad):
    j = pl.program_id(0)
    e = pl.program_id(1)
    f = pl.program_id(2)
    n_e = pl.num_programs(1)
    n_f = pl.num_programs(2)
    n_pass = np_ref[j * n_e + e]
    RC = min(256, blk)

    @pl.when((e == 0) & (f == 0))
    def _():
        xb_s[...] = x_ref[...].astype(BF16)
        y_s[...] = jnp.zeros(y_s.shape, F32)
        cw = cw_ref[...]
        cwt = cwt_ref[...]
        routed = jnp.where(cw != 0.0, 1.0, 0.0).astype(BF16)
        routed_t = jnp.where(cwt != 0.0, 1.0, 0.0).astype(BF16)
        for rc in range(blk // RC):
            rows_i = rc * RC + lax.broadcasted_iota(jnp.int32, (RC, blk), 0)
            cols_i = lax.broadcasted_iota(jnp.int32, (RC, blk), 1)
            before = jnp.where(cols_i < rows_i, 1.0, 0.0).astype(BF16)
            cnt = _dot(before, routed)
            pos_s[rc * RC:(rc + 1) * RC, :] = jnp.where(cw[rc * RC:(rc + 1) * RC] != 0.0, cnt, -1.0)
            rows_j = lax.broadcasted_iota(jnp.int32, (blk, RC), 0)
            cols_j = rc * RC + lax.broadcasted_iota(jnp.int32, (blk, RC), 1)
            before_t = jnp.where(rows_j < cols_j, 1.0, 0.0).astype(BF16)
            cnt_t = _dot(routed_t, before_t)
            post_s[:, rc * RC:(rc + 1) * RC] = jnp.where(cwt[:, rc * RC:(rc + 1) * RC] != 0.0, cnt_t, -1.0)

    @pl.when(f == 0)
    def _():
        prow = post_s[pl.ds(e, 1), :]
        wrow = cwt_ref[pl.ds(e, 1), :]

        def gather(u, carry):
            slot = (u * cap + lax.broadcasted_iota(jnp.int32, (cap, 1), 0)).astype(F32)
            hit = prow == slot
            xg_s[u, 0:cap, :] = _dot(jnp.where(hit, 1.0, 0.0).astype(BF16), xb_s[...]).astype(BF16)
            ws_s[u, 0:cap, :] = jnp.sum(jnp.where(hit, wrow, 0.0), axis=-1, keepdims=True)
            acc_s[u] = jnp.zeros(acc_s.shape[1:], F32)
            return carry

        lax.fori_loop(0, n_pass, gather, 0)

    def expert(u, carry):
        xg = xg_s[u, 0:cap, :]
        hcur = jax.nn.silu(_dot(xg, wg_ref[...])) * _dot(xg, wu_ref[...]) * ws_s[u, 0:cap, :]
        acc_s[u, 0:cap, :] += _dot(hcur.astype(BF16), wd_ref[...])
        return carry

    lax.fori_loop(0, n_pass, expert, 0)

    @pl.when(f == n_f - 1)
    def _():
        lane = lax.broadcasted_iota(jnp.int32, pos_s.shape, 1)
        pcol = jnp.sum(jnp.where(lane == e, pos_s[...], 0.0), axis=-1, keepdims=True)

        def scatter(u, carry):
            cc = lax.broadcasted_iota(jnp.int32, (1, cap_pad), 1)
            slot = jnp.where(cc < cap, u * cap + cc, NO_SLOT).astype(F32)
            hit = jnp.where(pcol == slot, 1.0, 0.0).astype(BF16)
            y_s[...] += _dot(hit, acc_s[u].astype(BF16))
            return carry

        lax.fori_loop(0, n_pass, scatter, 0)

    @pl.when((e == n_e - 1) & (f == n_f - 1))
    def _():
        o_ref[...] = _layer_norm(ALPHA * x_ref[...] + y_s[...], g_ref[...], b_ref[...])


def moe(x2d, cw, w_up, w_down, g, b, tf):
    n = x2d.shape[0]
    E, _, F2 = w_up.shape
    nf = (F2 // 2) // tf
    blk = _row_tile(n, MOE_BLOCK)
    cap = min(MOE_CAP, blk)
    cap_pad = -(-cap // LANES) * LANES
    max_pass = -(-blk // cap)
    nblk = n // blk
    cwt = jnp.swapaxes(cw[:, :ROUTE_ROWS], 0, 1)
    counts = jnp.sum((cw[:, :E] != 0.0).reshape(nblk, blk, E), axis=1)
    n_pass = ((counts + cap - 1) // cap).astype(jnp.int32).reshape(-1)
    w_up = jnp.transpose(w_up.reshape(E, D_MODEL, 2, nf, tf), (0, 2, 3, 1, 4))
    up_blk = (None, None, None, D_MODEL, tf)
    return pl.pallas_call(
        functools.partial(_moe_kernel, blk=blk, cap=cap, cap_pad=cap_pad),
        grid_spec=pltpu.PrefetchScalarGridSpec(
            num_scalar_prefetch=1, grid=(nblk, E, nf),
            in_specs=[pl.BlockSpec((blk, D_MODEL), lambda j, e, f, npr: (j, 0), pipeline_mode=pl.Buffered(1)),
                      pl.BlockSpec((blk, LANES), lambda j, e, f, npr: (j, 0)),
                      pl.BlockSpec((ROUTE_ROWS, blk), lambda j, e, f, npr: (0, j)),
                      pl.BlockSpec(up_blk, lambda j, e, f, npr: (e, 0, f, 0, 0)),
                      pl.BlockSpec(up_blk, lambda j, e, f, npr: (e, 1, f, 0, 0)),
                      pl.BlockSpec((None, tf, D_MODEL), lambda j, e, f, npr: (e, f, 0)),
                      pl.BlockSpec((1, D_MODEL), lambda j, e, f, npr: (0, 0)),
                      pl.BlockSpec((1, D_MODEL), lambda j, e, f, npr: (0, 0))],
            out_specs=pl.BlockSpec((blk, D_MODEL), lambda j, e, f, npr: (j, 0)),
            scratch_shapes=[pltpu.VMEM((blk, D_MODEL), BF16), pltpu.VMEM((blk, D_MODEL), F32),
                            pltpu.VMEM((blk, LANES), F32), pltpu.VMEM((ROUTE_ROWS, blk), F32),
                            pltpu.VMEM((max_pass, cap_pad, D_MODEL), BF16), pltpu.VMEM((max_pass, cap_pad, 1), F32),
                            pltpu.VMEM((max_pass, cap_pad, D_MODEL), F32)]),
        out_shape=jax.ShapeDtypeStruct((n, D_MODEL), F32),
        compiler_params=_params(("parallel", "arbitrary", "arbitrary")),
        name="moe_ffn",
    )(n_pass, x2d, cw, cwt, w_up, w_up, w_down, g, b)


def _layer(x, lw, l, *, mem_kv, mem_layer, mstate, decode):
    B, T, _ = x.shape
    n = B * T
    x2d = x.reshape(n, D_MODEL)
    tm = _row_tile(n, 1024)
    za, zb, zc, zcb, zd = in_proj(x2d, lw["in"], tm)
    za, zb, zc, zcb, zd = (a.reshape(B, T, -1) for a in (za, zb, zc, zcb, zd))

    if T % MLSTM_CHUNK == 0:
        h_m, ct, nt, mt = mlstm(za, zd, *mstate, lw["norm_g"], L=MLSTM_CHUNK, t_valid=MLSTM_CHUNK)
    else:
        padt = lambda a: jnp.pad(a, ((0, 0), (0, LANES - T), (0, 0)))
        h_m, ct, nt, mt = mlstm(padt(za), padt(zd), *mstate, lw["norm_g"], L=LANES, t_valid=T)
        h_m = h_m[:, :T]

    if decode is None:
        kc, vc = cmp_tokens(zc, lw["cmp_w"], lw["cmp_b"])
        h_n = nsa_prompt(zb, zd, zcb, kc, vc, tq=min(256, T), CH=min(512, T))
    else:
        h_n = nsa_decode(zb, zd, zcb, decode["nsa"], decode["win"], decode["pt"], lw["cmp_w"], lw["cmp_b"], l)

    h_x = mem_attn(zb, mem_kv, mem_layer, tq=_row_tile(T, 512))

    wg, bg = lw["in"]["mg"]
    flat = lambda a: a.reshape(n, -1)
    x1 = merge(x2d, flat(h_m), flat(h_n), flat(h_x), wg, bg, lw["w_branch"], lw["w_out"], lw["ln1_g"], lw["ln1_b"], tm)

    if l % 2 == 0:
        x2 = ffn(x1, lw["ffn_up"], lw["ffn_down"], lw["ln2_g"], lw["ln2_b"], _row_tile(n, 512), D_FF)
    else:
        cw = router(x1, lw["w_router"], lw["b_router"], _row_tile(n, 1024))
        x2 = moe(x1, cw, lw["ffn_up"], lw["ffn_down"], lw["ln2_g"], lw["ln2_b"], D_FF_EXPERT // 2)
    return x2.reshape(B, T, D_MODEL), zc, (ct, nt, mt)


def kernel(x_prompt, x_sample, mem_prompt, cache_nsa_kv, cache_win_kv, state_mlstm_C, state_mlstm_n,
           state_mlstm_m, cache_mem_kv, page_table, w_in, b_in, mlstm_norm_g, cmp_w, cmp_b, w_mem_kv,
           w_branch, w_out, ln1_g, ln1_b, ln2_g, ln2_b, ffn_w_up, ffn_w_down, moe_w_router, moe_b_router,
           moe_w_up, moe_w_down):
    B, T, _ = x_prompt.shape
    DB, TS, _ = x_sample.shape
    xp, xs = x_prompt, x_sample
    nsa_p, nsa_s, win_p, win_s = [], [], [], []
    Cp, np_, mp, Cs, ns, ms, memkv_p = [], [], [], [], [], [], []
    row1 = lambda a: a.reshape(1, -1).astype(F32)
    for l in range(DEPTH):
        wbr = (w_branch[l, 0].astype(BF16), _pad_branch_rows(w_branch[l, 1]).astype(BF16), w_branch[l, 2].astype(BF16))
        lw = dict(norm_g=mlstm_norm_g[l], cmp_w=cmp_w[l], cmp_b=cmp_b[l], w_branch=wbr,
                  w_out=w_out[l].astype(BF16), ln1_g=row1(ln1_g[l]), ln1_b=row1(ln1_b[l]),
                  ln2_g=row1(ln2_g[l]), ln2_b=row1(ln2_b[l]))
        lw["in"] = _split_in_proj(w_in[l], b_in[l])
        if l % 2 == 0:
            lw["ffn_up"] = ffn_w_up[l // 2].astype(BF16)
            lw["ffn_down"] = ffn_w_down[l // 2].astype(BF16)
        else:
            lw["ffn_up"] = moe_w_up[l // 2].astype(BF16)
            lw["ffn_down"] = moe_w_down[l // 2].astype(BF16)
            lw["w_router"] = moe_w_router[l // 2]
            lw["b_router"] = moe_b_router[l // 2]

        mkv = proj(mem_prompt.reshape(B * MEM_LEN, D_MODEL), w_mem_kv[l].astype(BF16), _row_tile(B * MEM_LEN, 512))
        mkv = mkv.reshape(B, MEM_LEN, 2 * X_WIDTH)
        st0 = (jnp.zeros((B, M_HEADS, M_DH, M_DH), F32), jnp.zeros((B, M_HEADS, M_DH), F32), jnp.zeros((B, M_HEADS), F32))
        xp, zc, st = _layer(xp, lw, l, mem_kv=mkv, mem_layer=0, mstate=st0, decode=None)
        kvn = zc.reshape(B, T, 6, NSA_GROUPS, NSA_DH)
        nsa_p.append(kvn[:, :, :4])
        win_p.append(kvn[:, T - min(WINDOW, T):, 4:])
        Cp.append(st[0]); np_.append(st[1]); mp.append(st[2])
        memkv_p.append(mkv.reshape(B, MEM_LEN, 2, X_HEADS, X_DH))

        sts = (state_mlstm_C[:, l], state_mlstm_n[:, l], state_mlstm_m[:, l])
        dec = dict(nsa=cache_nsa_kv, win=cache_win_kv, pt=page_table)
        xs, zc, st = _layer(xs, lw, l, mem_kv=cache_mem_kv, mem_layer=l, mstate=sts, decode=dec)
        kvn = zc.reshape(DB, TS, 6, NSA_GROUPS, NSA_DH)
        nsa_s.append(kvn[:, :, :4])
        win_s.append(jnp.concatenate([cache_win_kv[:, :, l].astype(F32), kvn[:, :, 4:]], axis=1)[:, TS:])
        Cs.append(st[0]); ns.append(st[1]); ms.append(st[2])
    return (xp, xs,
            jnp.stack(nsa_p, axis=2), jnp.stack(nsa_s, axis=2),
            jnp.stack(win_p, axis=2), jnp.stack(win_s, axis=2),
            jnp.stack(Cp, axis=1), jnp.stack(np_, axis=1), jnp.stack(mp, axis=1),
            jnp.stack(Cs, axis=1), jnp.stack(ns, axis=1), jnp.stack(ms, axis=1),
            jnp.stack(memkv_p, axis=2))
```

```python
import functools

import numpy as np
import jax
import jax.numpy as jnp
from jax import lax
from jax.experimental import pallas as pl
from jax.experimental.pallas import tpu as pltpu

D_MODEL = 1024
DEPTH = 2
BRANCH_WIDTH = 512
N_BRANCH = 3
M_HEADS = 4
M_DH = BRANCH_WIDTH // M_HEADS
M_WIDTH = M_HEADS * M_DH
NSA_HEADS = 8
NSA_DH = BRANCH_WIDTH // NSA_HEADS
NSA_WIDTH = NSA_HEADS * NSA_DH
NSA_GROUPS = 2
NSA_HPG = NSA_HEADS // NSA_GROUPS
NSA_KV = NSA_GROUPS * NSA_DH
CMP_LEN = 32
CMP_STRIDE = 16
SEL_LEN = 64
SEL_TOPN = 16
WINDOW = 512
MEM_LEN = 256
X_HEADS = 4
X_DH = BRANCH_WIDTH // X_HEADS
X_WIDTH = X_HEADS * X_DH
D_FF = 2816
N_EXPERTS = 8
TOP_K = 2
D_FF_EXPERT = 3584
ALPHA = (2.0 * DEPTH) ** 0.25
LN_EPS = 1e-5
IN_SPLITS = (M_WIDTH, M_WIDTH, M_WIDTH, M_HEADS, M_HEADS, M_WIDTH,
             NSA_WIDTH, 6 * NSA_KV, 3 * NSA_HEADS, X_WIDTH, N_BRANCH * D_MODEL)
LOG2E = 1.4426950408889634
NSA_SLOPES = tuple(LOG2E * 2.0 ** (-8.0 * (h + 1) / NSA_HEADS) for h in range(NSA_HEADS))
MASK_BIAS = -(2.0 ** 30)
SEL_FEAT = 64
SEL_GROUPS = 8
SLOPE_PARTS = 3

LANES = 128
SUBLANES = 8
VMEM_LIMIT = 56 * 1024 * 1024
PAGES_PER_STEP = 64
MLSTM_SEQS = 1
MLSTM_CHUNK = 256
GATE_COLS = LANES
NG_OFF = 2 * M_HEADS
NSA_QPAD = NSA_HEADS * LANES
MOE_BLOCK = 1024
MOE_CAP = 288
NO_SLOT = -2
ROUTE_ROWS = 16

F32 = jnp.float32
BF16 = jnp.bfloat16
NEG_INF = float("-inf")
HIGHEST = lax.Precision.HIGHEST


def _dot(a, b, precision=None):
    return jnp.dot(a, b, preferred_element_type=F32, precision=precision)


def _dot_nt(a, b, precision=None):
    return lax.dot_general(a, b, (((1,), (1,)), ((), ())), preferred_element_type=F32, precision=precision)


def _dot_tn(a, b):
    return lax.dot_general(a, b, (((0,), (0,)), ((), ())), preferred_element_type=F32)


def _params(sem):
    return pltpu.CompilerParams(dimension_semantics=sem, vmem_limit_bytes=VMEM_LIMIT)


def _masked_softmax2(s, mask):
    s = jnp.where(mask, s, NEG_INF)
    mx = jnp.max(s, axis=-1, keepdims=True)
    mx = jnp.where(mx > NEG_INF, mx, 0.0)
    p = jnp.where(mask, jnp.exp2(s - mx), 0.0)
    return p / jnp.maximum(jnp.sum(p, axis=-1, keepdims=True), 1e-30)


def _layer_norm(xf, g, b):
    mu = jnp.mean(xf, axis=-1, keepdims=True)
    var = jnp.mean(jnp.square(xf - mu), axis=-1, keepdims=True)
    return (xf - mu) * lax.rsqrt(var + LN_EPS) * g + b


def _row_tile(n, pref):
    return pref if n % pref == 0 else n


def _in_proj_kernel(x_ref, wa_ref, wb_ref, wc_ref, wd_ref, ba_ref, bb_ref, bc_ref, bd_ref,
                    oa_ref, ob_ref, oc_ref, ocb_ref, od_ref):
    x = x_ref[...].astype(BF16)
    oa_ref[...] = (_dot(x, wa_ref[...]) + ba_ref[...]).astype(BF16)
    ob_ref[...] = (_dot(x, wb_ref[...]) + bb_ref[...]).astype(BF16)
    c = _dot(x, wc_ref[...]) + bc_ref[...]
    oc_ref[...] = c
    ocb_ref[...] = c.astype(BF16)
    od_ref[...] = _dot(x, wd_ref[...]) + bd_ref[...]


def _group_select():
    return jax.nn.one_hot(np.arange(NSA_HEADS) // NSA_HPG, NSA_GROUPS, dtype=F32)


def _pad_branch_rows(w_nsa):
    return jnp.einsum("hem,hg->hgem", w_nsa.reshape(NSA_HEADS, NSA_DH, -1), _group_select()).reshape(NSA_QPAD, -1)


def _split_in_proj(w_in_l, b_in_l):
    offs = np.cumsum((0,) + IN_SPLITS)
    w = [w_in_l[:, offs[i]:offs[i + 1]] for i in range(len(IN_SPLITS))]
    b = [b_in_l[offs[i]:offs[i + 1]] for i in range(len(IN_SPLITS))]
    mq, mk, mv, mi, mf, mo, nq, nkv, ng, xq, mg = range(11)
    pad = GATE_COLS - 2 * M_HEADS - 3 * NSA_HEADS

    def cat(ids, zpad=0):
        ww = jnp.concatenate([w[i] for i in ids], axis=1)
        bb = jnp.concatenate([b[i] for i in ids])
        if zpad:
            ww = jnp.pad(ww, ((0, 0), (0, zpad)))
            bb = jnp.pad(bb, (0, zpad))
        return ww.astype(BF16), bb.reshape(1, -1).astype(F32)

    scale = NSA_DH ** -0.5 * LOG2E
    w[nq] = jnp.einsum("dhe,hg->dhge", w[nq].reshape(D_MODEL, NSA_HEADS, NSA_DH) * scale, _group_select()).reshape(D_MODEL, NSA_QPAD)
    b[nq] = jnp.einsum("he,hg->hge", b[nq].reshape(NSA_HEADS, NSA_DH) * scale, _group_select()).reshape(NSA_QPAD)
    return dict(a=cat([mq, mk, mv, mo]), b=cat([nq, xq]), c=cat([nkv]), d=cat([mi, mf, ng], pad), mg=cat([mg]))


def in_proj(x2d, wp, tm):
    n = x2d.shape[0]
    (wa, ba), (wb, bb), (wc, bc), (wd, bd) = wp["a"], wp["b"], wp["c"], wp["d"]
    full = lambda arr: pl.BlockSpec(arr.shape, lambda i: (0, 0), pipeline_mode=pl.Buffered(1))
    row = lambda w: pl.BlockSpec((tm, w), lambda i: (i, 0))
    return pl.pallas_call(
        _in_proj_kernel,
        grid=(n // tm,),
        in_specs=[row(D_MODEL), full(wa), full(wb), full(wc), full(wd), full(ba), full(bb), full(bc), full(bd)],
        out_specs=[row(wa.shape[1]), row(wb.shape[1]), row(wc.shape[1]), row(wc.shape[1]), row(wd.shape[1])],
        out_shape=[jax.ShapeDtypeStruct((n, wa.shape[1]), BF16), jax.ShapeDtypeStruct((n, wb.shape[1]), BF16),
                   jax.ShapeDtypeStruct((n, wc.shape[1]), F32), jax.ShapeDtypeStruct((n, wc.shape[1]), BF16),
                   jax.ShapeDtypeStruct((n, wd.shape[1]), F32)],
        compiler_params=_params(("parallel",)),
        name="in_proj",
    )(x2d, wa, wb, wc, wd, ba, bb, bc, bd)


def _proj_kernel(x_ref, w_ref, o_ref):
    o_ref[...] = _dot(x_ref[...].astype(BF16), w_ref[...])


def proj(x2d, w_bf16, tm):
    n, k = x2d.shape
    m = w_bf16.shape[1]
    return pl.pallas_call(
        _proj_kernel,
        grid=(n // tm,),
        in_specs=[pl.BlockSpec((tm, k), lambda i: (i, 0)), pl.BlockSpec((k, m), lambda i: (0, 0))],
        out_specs=pl.BlockSpec((tm, m), lambda i: (i, 0)),
        out_shape=jax.ShapeDtypeStruct((n, m), F32),
        compiler_params=_params(("parallel",)),
        name="mem_proj",
    )(x2d, w_bf16)


def _log_sigmoid(x):
    return jnp.minimum(x, 0.0) - jnp.log1p(jnp.exp(-jnp.abs(x)))


def _mlstm_kernel(q_ref, k_ref, v_ref, og_ref, gc_ref, gr_ref, c0_ref, n0_ref, m0_ref, ng_ref,
                  h_ref, ct_ref, nt_ref, mt_ref, c_s, n_s, m_s, *, L, t_valid, nb):
    ci = pl.program_id(1)

    @pl.when(ci == 0)
    def _():
        c_s[...] = c0_ref[...]
        n_s[...] = n0_ref[...]
        m_s[...] = m0_ref[...]

    row = lax.broadcasted_iota(jnp.int32, (L, L), 0)
    col = lax.broadcasted_iota(jnp.int32, (L, L), 1)
    causal = row >= col
    tri = causal.astype(F32)
    tri_t = (row <= col).astype(F32)
    rvalid = lax.broadcasted_iota(jnp.int32, (L, GATE_COLS), 0) < t_valid
    cvalid = lax.broadcasted_iota(jnp.int32, (2 * M_HEADS, L), 1) < t_valid
    scale = M_DH ** -0.5

    for bb, h in [(bb, h) for bb in range(nb) for h in range(M_HEADS)]:
        if h == 0:
            gc = gc_ref[bb]
            gr = gr_ref[bb]
            lf_c = jnp.where(rvalid, _log_sigmoid(gc), 0.0)
            lf_r = jnp.where(cvalid, _log_sigmoid(gr), 0.0)
            ig_c = jnp.where(rvalid, gc, NEG_INF)
            ig_r = jnp.where(cvalid, gr, NEG_INF)
            b_c = _dot(tri, lf_c, HIGHEST)
            b_r = _dot(lf_r, tri_t, HIGHEST)
        sl = slice(h * M_DH, (h + 1) * M_DH)
        q = q_ref[bb, :, sl]
        k = k_ref[bb, :, sl]
        v = v_ref[bb, :, sl]
        bc = b_c[:, M_HEADS + h:M_HEADS + h + 1]
        br = b_r[M_HEADS + h:M_HEADS + h + 1, :]
        igc = ig_c[:, h:h + 1]
        igr = ig_r[h:h + 1, :]
        m_prev = m_s[bb, h, 0:1, 0:1]
        c_prev = c_s[bb, h]
        n_prev = n_s[bb, h:h + 1, :]

        dmat = jnp.where(causal, bc - br + igr, NEG_INF)
        inter = bc + m_prev
        m_t = jnp.maximum(inter, jnp.max(dmat, axis=-1, keepdims=True))
        s = _dot_nt(q, k) * scale * jnp.exp(dmat - m_t)
        sc_in = jnp.exp(inter - m_t)
        qf = q.astype(F32)
        num = sc_in * _dot_nt(q, c_prev.astype(BF16)) + _dot(s.astype(BF16), v)
        den = sc_in * jnp.sum(qf * n_prev, axis=-1, keepdims=True) + jnp.sum(s, axis=-1, keepdims=True)
        hh = num / jnp.maximum(jnp.abs(den), jnp.exp(-m_t))

        b_last = bc[L - 1:L, :]
        dec_c = b_last - bc + igc
        dec_r = b_last - br + igr
        m_new = jnp.maximum(b_last + m_prev, jnp.max(dec_r, axis=-1, keepdims=True))
        ws_c = jnp.exp(dec_c - m_new) * scale
        sc = jnp.exp(b_last + m_prev - m_new)
        vf = v.astype(F32)
        kf = k.astype(F32)
        c_new = sc * c_prev + _dot_tn((vf * ws_c).astype(BF16), k)
        n_new = sc * n_prev + jnp.sum(kf * ws_c, axis=0, keepdims=True)
        c_s[bb, h] = c_new
        n_s[bb, h:h + 1, :] = n_new
        m_s[bb, h] = jnp.broadcast_to(m_new, m_s.shape[2:])

        og = og_ref[bb, :, sl].astype(F32)
        hh = hh * jax.nn.sigmoid(og)
        mu = jnp.mean(hh, axis=-1, keepdims=True)
        var = jnp.mean(jnp.square(hh - mu), axis=-1, keepdims=True)
        hn = (hh - mu) * lax.rsqrt(var + LN_EPS) * ng_ref[:, sl]
        h_ref[bb, :, sl] = hn.astype(h_ref.dtype)

    ct_ref[...] = c_s[...]
    nt_ref[...] = n_s[...]
    mt_ref[...] = m_s[...]


def mlstm(za, zd, c0, n0, m0, norm_g, *, L, t_valid):
    B, T, _ = za.shape
    nc = T // L
    gr = jnp.swapaxes(zd[:, :, :2 * M_HEADS], 1, 2)
    m0b = jnp.broadcast_to(m0.astype(F32)[:, :, None, None], (B, M_HEADS, SUBLANES, LANES))
    nb = MLSTM_SEQS if B % MLSTM_SEQS == 0 else 1
    colspec = lambda j: pl.BlockSpec((nb, L, M_WIDTH), lambda b, c, j=j: (b, c, j))
    st = lambda shp: pl.BlockSpec((nb,) + shp, lambda b, c: (b,) + (0,) * len(shp))
    kern = functools.partial(_mlstm_kernel, L=L, t_valid=t_valid, nb=nb)
    h, ct, nt, mt = pl.pallas_call(
        kern,
        grid=(B // nb, nc),
        in_specs=[colspec(0), colspec(1), colspec(2), colspec(3),
                  pl.BlockSpec((nb, L, GATE_COLS), lambda b, c: (b, c, 0)),
                  pl.BlockSpec((nb, 2 * M_HEADS, L), lambda b, c: (b, 0, c)),
                  st((M_HEADS, M_DH, M_DH)), st((M_HEADS, M_DH)), st((M_HEADS, SUBLANES, LANES)),
                  pl.BlockSpec((1, M_WIDTH), lambda b, c: (0, 0))],
        out_specs=[pl.BlockSpec((nb, L, M_WIDTH), lambda b, c: (b, c, 0)),
                   st((M_HEADS, M_DH, M_DH)), st((M_HEADS, M_DH)), st((M_HEADS, SUBLANES, LANES))],
        out_shape=[jax.ShapeDtypeStruct((B, T, M_WIDTH), BF16),
                   jax.ShapeDtypeStruct((B, M_HEADS, M_DH, M_DH), F32),
                   jax.ShapeDtypeStruct((B, M_HEADS, M_DH), F32),
                   jax.ShapeDtypeStruct((B, M_HEADS, SUBLANES, LANES), F32)],
        scratch_shapes=[pltpu.VMEM((nb, M_HEADS, M_DH, M_DH), F32), pltpu.VMEM((nb, M_HEADS, M_DH), F32),
                        pltpu.VMEM((nb, M_HEADS, SUBLANES, LANES), F32)],
        compiler_params=_params(("parallel", "arbitrary")),
        name="mlstm",
    )(za, za, za, za, zd, gr, c0.astype(F32), n0.astype(F32), m0b, norm_g.reshape(1, M_WIDTH).astype(F32))
    return h, ct, nt, mt[:, :, 0, 0]


def _mem_attn_kernel(q_ref, k_ref, v_ref, o_ref):
    scale = X_DH ** -0.5
    for h in range(X_HEADS):
        sl = slice(h * X_DH, (h + 1) * X_DH)
        k = k_ref[:, sl] if len(k_ref.shape) == 2 else k_ref[:, h, :]
        v = v_ref[:, sl] if len(v_ref.shape) == 2 else v_ref[:, h, :]
        s = _dot_nt(q_ref[0, :, sl], k.astype(BF16)) * scale
        mx = jnp.max(s, axis=-1, keepdims=True)
        p = jnp.exp(s - mx)
        p = p / jnp.sum(p, axis=-1, keepdims=True)
        o_ref[0, :, sl] = _dot(p.astype(BF16), v.astype(BF16)).astype(o_ref.dtype)


def mem_attn(zb, kv, layer, tq):
    B, T, _ = zb.shape
    if kv.ndim == 3:
        kv_specs = [pl.BlockSpec((None, MEM_LEN, X_WIDTH), lambda b, i, j=j: (b, 0, j)) for j in range(2)]
    else:
        kv_blk = (None, MEM_LEN, None, None, X_HEADS, X_DH)
        kv_specs = [pl.BlockSpec(kv_blk, lambda b, i, j=j: (b, 0, layer, j, 0, 0)) for j in range(2)]
    return pl.pallas_call(
        _mem_attn_kernel,
        grid=(B, T // tq),
        in_specs=[pl.BlockSpec((1, tq, X_WIDTH), lambda b, i: (b, i, NSA_QPAD // X_WIDTH))] + kv_specs,
        out_specs=pl.BlockSpec((1, tq, X_WIDTH), lambda b, i: (b, i, 0)),
        out_shape=jax.ShapeDtypeStruct((B, T, X_WIDTH), BF16),
        compiler_params=_params(("parallel", "parallel")),
        name="mem_attn",
    )(zb, kv, kv)


def _head_rows(q_ref, nq):
    parts = [q_ref[0, :, hd * LANES:(hd + 1) * LANES] for hd in range(NSA_HEADS)]
    if nq % 16:
        return jnp.concatenate([p.astype(F32) for p in parts], axis=0).astype(BF16)
    return jnp.concatenate(parts, axis=0)


def _store_heads(o_ref, gate, o_c, o_s, o_w, nq):
    lane = lax.broadcasted_iota(jnp.int32, (nq, LANES), 1)
    for hd in range(NSA_HEADS):
        r = slice(hd * nq, (hd + 1) * nq)
        gcol = lambda br: gate[:, NG_OFF + br * NSA_HEADS + hd:NG_OFF + br * NSA_HEADS + hd + 1]
        val = gcol(0) * o_c[r] + gcol(1) * o_s[r] + gcol(2) * o_w[r]
        keep = (lane >= NSA_DH) if hd // NSA_HPG == 1 else (lane < NSA_DH)
        o_ref[0, :, hd * LANES:(hd + 1) * LANES] = jnp.where(keep, val, 0.0).astype(o_ref.dtype)


def _overlap(tok, blk):
    c_start = tok * CMP_STRIDE
    s_start = blk * SEL_LEN
    return ((c_start < s_start + SEL_LEN) & (c_start + CMP_LEN > s_start)).astype(F32)


def _select_blocks(imp, tpos, n_sel, n_top):
    nq, W = imp.shape
    blk = lax.broadcasted_iota(jnp.int32, (nq, W), 1)
    cur = tpos // SEL_LEN
    forced = (blk == 0) | (blk == cur) | (blk == cur - 1)
    v = jnp.where(forced, jnp.inf, jnp.where(blk <= cur, imp, NEG_INF))
    v = jnp.where(blk < n_sel, v, NEG_INF)
    ahead = jnp.zeros((nq, W), F32)
    for j in range(n_sel):
        vj = v[:, j:j + 1]
        ahead = ahead + jnp.where(vj > v, 1.0, jnp.where(vj == v, jnp.where(blk > j, 1.0, 0.0), 0.0))
    return jnp.where((ahead < n_top) & (blk < n_sel), 1.0, 0.0)


def _select_blocks_t(imp_t, tpos_row, n_sel, n_top):
    n_blk, nq = imp_t.shape
    blk = lax.broadcasted_iota(jnp.int32, (n_blk, nq), 0)
    cur = tpos_row // SEL_LEN
    forced = (blk == 0) | (blk == cur) | (blk == cur - 1)
    v = jnp.where(forced, jnp.inf, jnp.where(blk <= cur, imp_t, NEG_INF))
    v = jnp.where(blk < n_sel, v, NEG_INF)
    n_rg = n_blk // SUBLANES
    vg = [v[rg * SUBLANES:(rg + 1) * SUBLANES] for rg in range(n_rg)]
    bg = [rg * SUBLANES + lax.broadcasted_iota(jnp.int32, (SUBLANES, nq), 0) for rg in range(n_rg)]

    ahead = [jnp.zeros((SUBLANES, nq), F32) for _ in range(n_rg)]
    for j in range(n_sel):
        vj = v[j:j + 1, :]
        for rg in range(n_rg):
            if rg * SUBLANES > j:
                inc = jnp.where(vj >= vg[rg], 1.0, 0.0)
            elif (rg + 1) * SUBLANES - 1 <= j:
                inc = jnp.where(vj > vg[rg], 1.0, 0.0)
            else:
                inc = jnp.where(vj > vg[rg], 1.0, jnp.where(vj == vg[rg], jnp.where(bg[rg] > j, 1.0, 0.0), 0.0))
            ahead[rg] = ahead[rg] + inc
    ahead = jnp.concatenate(ahead, axis=0)
    return jnp.where((ahead < n_top) & (blk < n_sel), 1.0, 0.0)


def _online_update(s, m_old, l_old):
    m_new = jnp.maximum(m_old, jnp.max(s, axis=-1, keepdims=True))
    alpha = jnp.exp2(m_old - m_new)
    p = jnp.exp2(s - m_new)
    return p, m_new, alpha, alpha * l_old + jnp.sum(p, axis=-1, keepdims=True)


def _cmp_tokens_kernel(xk_ref, xv_ref, bd_ref, b_ref, kc_ref, vc_ref, *, nsub):
    for c, (x_ref, o_ref) in enumerate(((xk_ref, kc_ref), (xv_ref, vc_ref))):
        a0 = jnp.zeros((nsub, LANES), F32)
        a1 = jnp.zeros((nsub, LANES), F32)
        for j in range(CMP_STRIDE):
            xj = x_ref[0, :, j, :].astype(BF16)
            a0 = a0 + _dot(xj, bd_ref[c, 0, j])
            a1 = a1 + _dot(xj, bd_ref[c, 1, j])
        tok = a0 + pltpu.roll(a1, nsub - 1, axis=0) + b_ref[c]
        o_ref[0] = tok.astype(o_ref.dtype)


def _cmp_blockdiag(cmp_w_l):
    R = CMP_LEN // CMP_STRIDE
    w = cmp_w_l.astype(F32).reshape(2, R, CMP_STRIDE, NSA_DH, NSA_DH)
    eye = jnp.eye(NSA_GROUPS, dtype=F32)
    return jnp.einsum("ab,crjde->crjadbe", eye, w).reshape(2, R, CMP_STRIDE, LANES, LANES).astype(BF16)


def _cmp_bias(cmp_b_l):
    return jnp.tile(cmp_b_l.astype(F32), (1, NSA_GROUPS)).reshape(2, 1, LANES)


def cmp_tokens(zc, cmp_w_l, cmp_b_l):
    B, T, W = zc.shape
    nsub = T // CMP_STRIDE
    x4 = zc.reshape(B, nsub, CMP_STRIDE, W)
    bd = _cmp_blockdiag(cmp_w_l)
    bias = _cmp_bias(cmp_b_l)
    spec = lambda j: pl.BlockSpec((1, nsub, CMP_STRIDE, LANES), lambda b, j=j: (b, 0, 0, j))
    return pl.pallas_call(
        functools.partial(_cmp_tokens_kernel, nsub=nsub),
        grid=(B,),
        in_specs=[spec(0), spec(1), pl.BlockSpec(bd.shape, lambda b: (0,) * 5), pl.BlockSpec(bias.shape, lambda b: (0, 0, 0))],
        out_specs=[pl.BlockSpec((1, nsub, LANES), lambda b: (b, 0, 0))] * 2,
        out_shape=[jax.ShapeDtypeStruct((B, nsub, LANES), BF16)] * 2,
        compiler_params=_params(("parallel",)),
        name="nsa_cmp_tokens",
    )(x4, x4, bd, bias)


def _nsa_prompt_kernel(q_ref, g_ref, kca_ref, vc_ref, ka_ref, vst_ref, kwa_ref, vw_ref, qf_ref, o_ref,
                       qaug_s, s_buf, p_buf, w_buf, pw_buf, pc_s, m_s, l_s, a_s, acc_s, oc_s, ow_s, *, tq, T, CH):
    start = pl.program_id(1) * tq
    n_cmp_rows = kca_ref.shape[1]
    n_sel = T // SEL_LEN
    n_top = min(SEL_TOPN, n_sel)
    n_blk = -(-n_sel // SUBLANES) * SUBLANES
    nq_all = NSA_HEADS * tq
    qw = nq_all // SEL_GROUPS
    tpos_row = start + lax.broadcasted_iota(jnp.int32, (1, tq), 1)
    tpos_all = jnp.concatenate([tpos_row] * NSA_HEADS, axis=1)
    qcols = lambda qi: slice(qi * qw, (qi + 1) * qw)
    tiles = [(qi, ct, slice(qi * qw + ct * LANES, qi * qw + (ct + 1) * LANES), slice(ct * LANES, (ct + 1) * LANES))
             for qi in range(SEL_GROUPS) for ct in range(qw // LANES)]

    qaug_s[:, 0:LANES] = _head_rows(q_ref, tq)
    for hd in range(NSA_HEADS):
        qaug_s[hd * tq:(hd + 1) * tq, LANES:2 * LANES] = jnp.broadcast_to(qf_ref[hd:hd + 1, :], (tq, LANES)).astype(BF16)

    def softmax_tile(s_t):
        mx = jnp.max(s_t, axis=0, keepdims=True)
        mx = jnp.where(mx > NEG_INF, mx, 0.0)
        p_t = jnp.exp2(s_t - mx)
        return p_t, jnp.maximum(jnp.sum(p_t, axis=0, keepdims=True), 1e-30)

    c_end = lax.broadcasted_iota(jnp.int32, (n_cmp_rows, 1), 0) * CMP_STRIDE + CMP_LEN - 1
    for qi in range(SEL_GROUPS):
        s_buf[qi, 0:n_cmp_rows, :] = _dot_nt(kca_ref[0], qaug_s[qcols(qi), :])
    for qi, ct, cols, tc in tiles:
        p_t, den = softmax_tile(jnp.where(c_end <= tpos_all[:, cols], s_buf[qi, 0:n_cmp_rows, tc], NEG_INF))
        pc_s[:, cols] = p_t / den
    oc_s[...] = _dot_tn(vc_ref[0], pc_s[...].astype(BF16))
    ov_t = _overlap(lax.broadcasted_iota(jnp.int32, (n_blk, n_cmp_rows), 1),
                    lax.broadcasted_iota(jnp.int32, (n_blk, n_cmp_rows), 0))
    lane = lax.broadcasted_iota(jnp.int32, (tq, LANES), 1)
    feat_sel = [None] * NSA_HEADS
    for g in range(NSA_GROUPS):
        pg = pc_s[:, g * NSA_HPG * tq:(g * NSA_HPG + 1) * tq]
        for p in range(1, NSA_HPG):
            pg = pg + pc_s[:, (g * NSA_HPG + p) * tq:(g * NSA_HPG + p + 1) * tq]
        sel_t = _select_blocks_t(_dot(ov_t, pg, HIGHEST), tpos_row, n_sel, n_top)
        sel = jnp.concatenate([sel_t, jnp.zeros((LANES - n_blk, tq), F32)], axis=0).T
        unsel = jnp.where(lane < SEL_FEAT, (1.0 - sel) * MASK_BIAS, 0.0)
        for p in range(NSA_HPG):
            hd = g * NSA_HPG + p
            feat_sel[hd] = (unsel + qf_ref[hd:hd + 1, :]).astype(BF16)

    wk = WINDOW + tq
    base_w = pl.multiple_of(start, tq)
    wrows = WINDOW + LANES
    for qi in range(SEL_GROUPS):
        w_buf[qi] = _dot_nt(kwa_ref[0, pl.ds(base_w, wk), :], qaug_s[qcols(qi), :])
    for qi, ct, cols, tc in tiles:
        off = cols.start % tq
        band = slice(off, off + wrows)
        wpos = start - WINDOW + off + lax.broadcasted_iota(jnp.int32, (wrows, 1), 0)
        tp = tpos_all[:, cols]
        first = jnp.maximum(tp - (WINDOW - 1), 0)
        s_t = jnp.where(wpos >= first, jnp.where(wpos <= tp, w_buf[qi, band, tc], NEG_INF), NEG_INF)
        p_t, den = softmax_tile(s_t)
        pw_buf[qi, band, tc] = p_t.astype(BF16)
        if off:
            pw_buf[qi, 0:off, tc] = jnp.zeros((off, LANES), BF16)
        if off + wrows < wk:
            pw_buf[qi, off + wrows:wk, tc] = jnp.zeros((wk - off - wrows, LANES), BF16)
        a_s[:, cols] = den
    for qi in range(SEL_GROUPS):
        ow_s[:, qcols(qi)] = _dot_tn(vw_ref[0, pl.ds(base_w, wk), :], pw_buf[qi]) / a_s[:, qcols(qi)]

    for hd in range(NSA_HEADS):
        qaug_s[hd * tq:(hd + 1) * tq, LANES:2 * LANES] = feat_sel[hd]

    m_s[...] = jnp.full(m_s.shape, NEG_INF, F32)
    l_s[...] = jnp.zeros(l_s.shape, F32)
    acc_s[...] = jnp.zeros(acc_s.shape, F32)

    def issue(c, qi):
        base = pl.multiple_of(c * CH, CH)
        s_buf[qi] = _dot_nt(ka_ref[0, pl.ds(base, CH), :], qaug_s[qi * qw:(qi + 1) * qw, :])

    def absorb(c, qi, diag):
        for ct in range(qw // LANES):
            cols = slice(qi * qw + ct * LANES, qi * qw + (ct + 1) * LANES)
            s_t = s_buf[qi, :, ct * LANES:(ct + 1) * LANES]
            if diag:
                kpos = c * CH + lax.broadcasted_iota(jnp.int32, (CH, 1), 0)
                s_t = jnp.where(kpos <= tpos_all[:, cols], s_t, NEG_INF)
            m_old = m_s[:, cols]
            m_new = jnp.maximum(m_old, jnp.max(s_t, axis=0, keepdims=True))
            alpha = jnp.exp2(m_old - m_new)
            p_t = jnp.exp2(s_t - m_new)
            m_s[:, cols] = m_new
            a_s[:, cols] = alpha
            l_s[:, cols] = alpha * l_s[:, cols] + jnp.sum(p_t, axis=0, keepdims=True)
            p_buf[qi, :, ct * LANES:(ct + 1) * LANES] = p_t.astype(BF16)
        cols = slice(qi * qw, (qi + 1) * qw)
        acc_s[:, cols] = a_s[:, cols] * acc_s[:, cols] + _dot(vst_ref[0, c], p_buf[qi])

    def full_chunk(c, carry):
        for qi in range(SEL_GROUPS):
            absorb(c, qi, False)
            issue(c + 1, qi)
        return carry

    n_full = start // CH
    for qi in range(SEL_GROUPS):
        issue(0, qi)
    lax.fori_loop(0, n_full, full_chunk, 0)
    for qi in range(SEL_GROUPS):
        absorb(n_full, qi, True)

    gate_t = jax.nn.sigmoid(g_ref[0]).T
    for hd in range(NSA_HEADS):
        cols = slice(hd * tq, (hd + 1) * tq)
        grow = lambda br: gate_t[NG_OFF + br * NSA_HEADS + hd:NG_OFF + br * NSA_HEADS + hd + 1, :]
        o_s = acc_s[:, cols] / jnp.maximum(l_s[:, cols], 1e-30)
        val = (grow(0) * oc_s[:, cols] + grow(1) * o_s + grow(2) * ow_s[:, cols]).T
        keep = (lane >= NSA_DH) if hd // NSA_HPG == 1 else (lane < NSA_DH)
        o_ref[0, :, hd * LANES:(hd + 1) * LANES] = jnp.where(keep, val, 0.0).astype(o_ref.dtype)


def _bf16_parts(x, n):
    parts = []
    for _ in range(n):
        p = float(np.asarray(x, np.float32).astype(jnp.bfloat16).astype(np.float32))
        parts.append(p)
        x = x - p
    return parts


def _slope_features():
    qf = np.zeros((NSA_HEADS, LANES), np.float32)
    for hd in range(NSA_HEADS):
        for p, s_p in enumerate(_bf16_parts(NSA_SLOPES[hd], SLOPE_PARTS)):
            qf[hd, SEL_FEAT + 2 * p] = s_p * SEL_LEN
            qf[hd, SEL_FEAT + 2 * p + 1] = s_p
    return jnp.asarray(qf)


def _key_features(pos, block_onehot):
    pos = np.asarray(pos)
    ok = pos >= 0
    kf = np.zeros((pos.shape[0], LANES), np.float32)
    if block_onehot:
        kf[np.arange(pos.shape[0])[ok], pos[ok] // SEL_LEN] = 1.0
    for p in range(SLOPE_PARTS):
        kf[ok, SEL_FEAT + 2 * p] = pos[ok] // SEL_LEN
        kf[ok, SEL_FEAT + 2 * p + 1] = pos[ok] % SEL_LEN
    return jnp.asarray(kf, BF16)


def nsa_prompt(zb, zd, zcb, kc, vc, tq, CH):
    B, T, _ = zb.shape
    nsub = kc.shape[1]
    assert T % CH == 0 and CH % tq == 0 and T // SEL_LEN <= SEL_FEAT and nsub <= CH
    qf = _slope_features()
    with_feats = lambda k, feats: jnp.concatenate([k, jnp.broadcast_to(feats[None], (B,) + feats.shape)], axis=2)
    k_aug = with_feats(zcb[:, :, 2 * NSA_KV:3 * NSA_KV], _key_features(np.arange(T), True))
    kc_aug = with_feats(kc, _key_features(np.arange(nsub) * CMP_STRIDE + CMP_LEN - 1, False))
    kw = jnp.pad(zcb[:, :, 4 * NSA_KV:5 * NSA_KV], ((0, 0), (WINDOW, 0), (0, 0)))
    kw_aug = with_feats(kw, _key_features(np.arange(T + WINDOW) - WINDOW, False))
    vs_t = jnp.swapaxes(zcb[:, :, 3 * NSA_KV:4 * NSA_KV].reshape(B, T // CH, CH, NSA_KV), 2, 3)
    vw = jnp.pad(zcb[:, :, 5 * NSA_KV:6 * NSA_KV], ((0, 0), (WINDOW, 0), (0, 0)))
    per_b = lambda rows_, w: pl.BlockSpec((1, rows_, w), lambda b, i: (b, 0, 0))
    kern = functools.partial(_nsa_prompt_kernel, tq=tq, T=T, CH=CH)
    nq_all = NSA_HEADS * tq
    qw = nq_all // SEL_GROUPS
    wk = WINDOW + tq
    assert qw % LANES == 0
    return pl.pallas_call(
        kern,
        grid=(B, T // tq),
        in_specs=[pl.BlockSpec((1, tq, NSA_QPAD), lambda b, i: (b, i, 0)),
                  pl.BlockSpec((1, tq, GATE_COLS), lambda b, i: (b, i, 0)),
                  per_b(nsub, 2 * LANES), per_b(nsub, LANES),
                  per_b(T, 2 * LANES),
                  pl.BlockSpec((1, T // CH, NSA_KV, CH), lambda b, i: (b, 0, 0, 0)),
                  per_b(T + WINDOW, 2 * LANES), per_b(T + WINDOW, LANES),
                  pl.BlockSpec(qf.shape, lambda b, i: (0, 0))],
        out_specs=pl.BlockSpec((1, tq, NSA_QPAD), lambda b, i: (b, i, 0)),
        out_shape=jax.ShapeDtypeStruct((B, T, NSA_QPAD), BF16),
        scratch_shapes=[pltpu.VMEM((nq_all, 2 * LANES), BF16),
                        pltpu.VMEM((SEL_GROUPS, CH, qw), F32), pltpu.VMEM((SEL_GROUPS, CH, qw), BF16),
                        pltpu.VMEM((SEL_GROUPS, wk, qw), F32), pltpu.VMEM((SEL_GROUPS, wk, qw), BF16),
                        pltpu.VMEM((nsub, nq_all), F32),
                        pltpu.VMEM((1, nq_all), F32), pltpu.VMEM((1, nq_all), F32), pltpu.VMEM((1, nq_all), F32),
                        pltpu.VMEM((NSA_KV, nq_all), F32), pltpu.VMEM((NSA_KV, nq_all), F32),
                        pltpu.VMEM((NSA_KV, nq_all), F32)],
        compiler_params=_params(("parallel", "arbitrary")),
        name="nsa_prompt",
    )(zb, zd, kc_aug, vc, k_aug, vs_t, kw_aug, vw, qf)


def _page_specs(shape, slot_blk, layer, n_pages):
    def mk(i):
        def imap(b, s, pt):
            return (pt[b * n_pages + s * PAGES_PER_STEP + i], layer, slot_blk, 0, 0)
        return pl.BlockSpec(shape, imap)
    return [mk(i) for i in range(PAGES_PER_STEP)]


def _nsa_dec_cmp_kernel(pt_ref, *refs, T, past_len, n_rows):
    pages = refs[:PAGES_PER_STEP]
    perm_ref, bd_ref, b_ref, q_ref, oc_ref, sel_ref, tok_s, pend_s = refs[PAGES_PER_STEP:]
    s = pl.program_id(1)
    n_steps = pl.num_programs(1)
    page_rows = pages[0].shape[-1]
    sub = page_rows // CMP_STRIDE
    R = PAGES_PER_STEP * sub

    @pl.when(s == 0)
    def _():
        pend_s[...] = jnp.zeros(pend_s.shape, F32)

    W2 = 2 * LANES
    rid = lax.broadcasted_iota(jnp.int32, (R, W2), 0)
    xs = [_dot_nt(perm_ref[...], pg[0, 0].reshape(W2, page_rows).astype(BF16)) for pg in pages]
    a0 = jnp.zeros((R, W2), F32)
    a1 = jnp.zeros((R, W2), F32)
    for j in range(CMP_STRIDE):
        xj = jnp.concatenate([x[j * sub:(j + 1) * sub] for x in xs], axis=0).astype(BF16)
        a0 = a0 + _dot(xj, bd_ref[0, j])
        a1 = a1 + _dot(xj, bd_ref[1, j])
    a0 = a0 + b_ref[...]
    tok_s[pl.ds(pl.multiple_of(s * R, R), R), :] = jnp.where(rid == 0, pend_s[...], pltpu.roll(a0, 1, axis=0)) + a1
    pend_s[...] = a0[R - 1:R, :]

    @pl.when(s == n_steps - 1)
    def _():
        qp = _head_rows(q_ref, T)
        kc = tok_s[:, 0:LANES].astype(BF16)
        vc = tok_s[:, LANES:W2].astype(BF16)
        sc_all = _dot_nt(qp, kc)
        tpos = past_len + lax.broadcasted_iota(jnp.int32, (T, 1), 0)
        r = lax.broadcasted_iota(jnp.int32, (T, n_rows), 1)
        dc = tpos - ((r - 1) * CMP_STRIDE + CMP_LEN - 1)
        mask_c = (dc >= 0) & (r >= 1)
        dcf = dc.astype(F32)
        pcs = [_masked_softmax2(sc_all[hd * T:(hd + 1) * T] - NSA_SLOPES[hd] * dcf, mask_c) for hd in range(NSA_HEADS)]
        oc_ref[0] = _dot(jnp.concatenate(pcs, axis=0).astype(BF16), vc)
        W = sel_ref.shape[2]
        n_sel = -(-(past_len + T) // SEL_LEN)
        ov = _overlap(lax.broadcasted_iota(jnp.int32, (n_rows, W), 0) - 1, lax.broadcasted_iota(jnp.int32, (n_rows, W), 1))
        for g in range(NSA_GROUPS):
            pg = pcs[g * NSA_HPG]
            for p in range(1, NSA_HPG):
                pg = pg + pcs[g * NSA_HPG + p]
            imp = _dot(pg, ov, HIGHEST)
            sel_ref[0, g * T:(g + 1) * T, :] = _select_blocks(imp, tpos, n_sel, min(SEL_TOPN, n_sel))


def _nsa_dec_sel_kernel(pt_ref, *refs, T, past_len):
    pages = refs[:PAGES_PER_STEP]
    q_ref, sel_ref, sel_last_ref, ex_ref, kn_ref, vn_ref, os_ref, m_s, l_s, acc_s = refs[PAGES_PER_STEP:]
    s = pl.program_id(1)
    n_steps = pl.num_programs(1)
    page_rows = pages[0].shape[-1]
    CH = PAGES_PER_STEP * page_rows

    @pl.when(s == 0)
    def _():
        m_s[...] = jnp.full(m_s.shape, NEG_INF, F32)
        l_s[...] = jnp.zeros(l_s.shape, F32)
        acc_s[...] = jnp.zeros(acc_s.shape, F32)

    qp = _head_rows(q_ref, T)
    tpos = past_len + lax.broadcasted_iota(jnp.int32, (T, 1), 0)

    def update(s_all, mask_of_group, kpos, pv):
        ds = tpos - kpos
        dsf = ds.astype(F32)
        ps = []
        for g in range(NSA_GROUPS):
            mk = mask_of_group(g) & (ds >= 0)
            for p in range(NSA_HPG):
                hd = g * NSA_HPG + p
                r = slice(hd * T, (hd + 1) * T)
                sc = jnp.where(mk, s_all[r] - NSA_SLOPES[hd] * dsf, NEG_INF)
                pr, m_new, alpha, l_new = _online_update(sc, m_s[r], l_s[r])
                m_s[r] = m_new
                l_s[r] = l_new
                acc_s[r] = alpha * acc_s[r]
                ps.append(pr)
        acc_s[...] = acc_s[...] + pv(jnp.concatenate(ps, axis=0).astype(BF16))

    selk = _dot(sel_ref[0, 0].astype(BF16), ex_ref[...])
    pairs = [(pages[i], pages[i + 1]) for i in range(0, PAGES_PER_STEP, 2)]
    side_by_side = lambda a, b, slot: jnp.concatenate([a[0, 0, slot], b[0, 0, slot]], axis=1).astype(BF16)
    s_all = jnp.concatenate([_dot(qp, side_by_side(a, b, 0)) for a, b in pairs], axis=1)
    kpos = s * CH + lax.broadcasted_iota(jnp.int32, (T, CH), 1)

    def pv_pages(pmat):
        out = jnp.zeros((NSA_HEADS * T, LANES), F32)
        for i, (a, b) in enumerate(pairs):
            out = out + _dot_nt(pmat[:, 2 * i * page_rows:2 * (i + 1) * page_rows], side_by_side(a, b, 1))
        return out

    update(s_all, lambda g: selk[g * T:(g + 1) * T] > 0.5, kpos, pv_pages)

    @pl.when(s == n_steps - 1)
    def _():
        nk = kn_ref.shape[1]
        sn = _dot_nt(qp, kn_ref[0])
        lane = lax.broadcasted_iota(jnp.int32, (T, nk), 1)
        sl = sel_last_ref[0, 0]
        update(sn, lambda g: (sl[g * T:(g + 1) * T, 0:1] > 0.5) & (lane < T), past_len + lane,
               lambda pmat: _dot(pmat, vn_ref[0]))
        os_ref[0] = acc_s[...] / jnp.maximum(l_s[...], 1e-30)


def _nsa_dec_win_kernel(q_ref, g_ref, oc_ref, os_ref, wp_ref, kwn_ref, vwn_ref, o_ref, *, T, past_len):
    qp = _head_rows(q_ref, T)
    w_src = wp_ref.shape[-1]
    nk = kwn_ref.shape[1]
    sw_all = jnp.concatenate([_dot(qp, wp_ref[0, 0, 0].astype(BF16)), _dot_nt(qp, kwn_ref[0])], axis=1)
    j = lax.broadcasted_iota(jnp.int32, (T, w_src + nk), 1)
    tpos = past_len + lax.broadcasted_iota(jnp.int32, (T, 1), 0)
    wpos = past_len - w_src + j
    dw = tpos - wpos
    mask_w = (dw >= 0) & (dw < WINDOW) & (wpos >= 0) & (j < w_src + T)
    dwf = dw.astype(F32)
    pws = [_masked_softmax2(sw_all[hd * T:(hd + 1) * T] - NSA_SLOPES[hd] * dwf, mask_w) for hd in range(NSA_HEADS)]
    pw = jnp.concatenate(pws, axis=0).astype(BF16)
    o_w = _dot_nt(pw[:, :w_src], wp_ref[0, 0, 1].astype(BF16)) + _dot(pw[:, w_src:], vwn_ref[0])
    _store_heads(o_ref, jax.nn.sigmoid(g_ref[0]), oc_ref[0], os_ref[0], o_w, T)


def nsa_decode(zb, zd, zcb, cache_nsa_kv, cache_win_kv, page_table, cmp_w_l, cmp_b_l, layer):
    DB, T, _ = zb.shape
    n_pool, page_rows = cache_nsa_kv.shape[:2]
    n_pages = page_table.shape[1]
    past_len = n_pages * page_rows
    n_steps = n_pages // PAGES_PER_STEP
    sub_per_page = page_rows // CMP_STRIDE
    n_rows = n_pages * sub_per_page
    assert (n_rows - 1) * CMP_STRIDE + CMP_LEN - 1 > past_len + T - 1
    assert T <= CMP_STRIDE and n_pages % PAGES_PER_STEP == 0 and page_rows == LANES
    pt = page_table.reshape(-1).astype(jnp.int32)
    cache_t = jnp.transpose(cache_nsa_kv, (0, 2, 3, 4, 5, 1)).reshape(n_pool, DEPTH, 4, NSA_KV, page_rows)
    page_blk = (1, 1, 2, NSA_KV, page_rows)

    rr = np.arange(page_rows)
    perm = jnp.asarray(rr[None, :] == ((rr % sub_per_page) * CMP_STRIDE + rr // sub_per_page)[:, None], BF16)
    bd_kv = _cmp_blockdiag(cmp_w_l)
    zero = jnp.zeros_like(bd_kv[0])
    bd = jnp.concatenate([jnp.concatenate([bd_kv[0], zero], axis=-1), jnp.concatenate([zero, bd_kv[1]], axis=-1)], axis=-2)
    bias = _cmp_bias(cmp_b_l).reshape(1, 2 * LANES)

    n_sel = -(-(past_len + T) // SEL_LEN)
    blocks_per_step = PAGES_PER_STEP * page_rows // SEL_LEN
    sel_used = (n_steps + 1) * blocks_per_step
    sel_w = -(-sel_used // LANES) * LANES
    assert sel_used >= n_sel and blocks_per_step <= LANES
    q_spec = pl.BlockSpec((1, T, NSA_QPAD), lambda b, s, pt: (b, 0, 0))
    const = lambda a: pl.BlockSpec(a.shape, lambda b, s, pt: (0,) * a.ndim)

    o_c, sel = pl.pallas_call(
        functools.partial(_nsa_dec_cmp_kernel, T=T, past_len=past_len, n_rows=n_rows),
        grid_spec=pltpu.PrefetchScalarGridSpec(
            num_scalar_prefetch=1, grid=(DB, n_steps),
            in_specs=_page_specs(page_blk, 0, layer, n_pages) + [const(perm), const(bd), const(bias), q_spec],
            out_specs=[pl.BlockSpec((1, NSA_HEADS * T, LANES), lambda b, s, pt: (b, 0, 0)),
                       pl.BlockSpec((1, NSA_GROUPS * T, sel_w), lambda b, s, pt: (b, 0, 0))],
            scratch_shapes=[pltpu.VMEM((n_rows, 2 * LANES), F32), pltpu.VMEM((1, 2 * LANES), F32)]),
        out_shape=[jax.ShapeDtypeStruct((DB, NSA_HEADS * T, LANES), F32),
                   jax.ShapeDtypeStruct((DB, NSA_GROUPS * T, sel_w), F32)],
        compiler_params=_params(("parallel", "arbitrary")),
        name="nsa_dec_cmp",
    )(pt, *([cache_t] * PAGES_PER_STEP), perm, bd, bias, zb)

    sel_steps = sel[:, :, :sel_used].reshape(DB, NSA_GROUPS * T, n_steps + 1, blocks_per_step).transpose(0, 2, 1, 3)
    sel_steps = jnp.pad(sel_steps, ((0, 0), (0, 0), (0, 0), (0, LANES - blocks_per_step)))
    kk = np.arange(PAGES_PER_STEP * page_rows) // SEL_LEN
    expand = jnp.asarray(kk[None, :] == np.arange(LANES)[:, None], BF16)
    pad_rows = LANES - T
    new_rows = lambda slot: jnp.pad(zcb[:, :, slot * NSA_KV:(slot + 1) * NSA_KV], ((0, 0), (0, pad_rows), (0, 0)))
    new_spec = pl.BlockSpec((1, LANES, LANES), lambda b, s, pt: (b, 0, 0))
    sel_blk = (1, 1, NSA_GROUPS * T, LANES)

    o_s = pl.pallas_call(
        functools.partial(_nsa_dec_sel_kernel, T=T, past_len=past_len),
        grid_spec=pltpu.PrefetchScalarGridSpec(
            num_scalar_prefetch=1, grid=(DB, n_steps),
            in_specs=_page_specs(page_blk, 1, layer, n_pages)
            + [q_spec, pl.BlockSpec(sel_blk, lambda b, s, pt: (b, s, 0, 0)),
               pl.BlockSpec(sel_blk, lambda b, s, pt: (b, n_steps, 0, 0)), const(expand), new_spec, new_spec],
            out_specs=pl.BlockSpec((1, NSA_HEADS * T, LANES), lambda b, s, pt: (b, 0, 0)),
            scratch_shapes=[pltpu.VMEM((NSA_HEADS * T, 1), F32), pltpu.VMEM((NSA_HEADS * T, 1), F32),
                            pltpu.VMEM((NSA_HEADS * T, LANES), F32)]),
        out_shape=jax.ShapeDtypeStruct((DB, NSA_HEADS * T, LANES), F32),
        compiler_params=_params(("parallel", "arbitrary")),
        name="nsa_dec_sel",
    )(pt, *([cache_t] * PAGES_PER_STEP), zb, sel_steps, sel_steps, expand, new_rows(2), new_rows(3))

    w_src = cache_win_kv.shape[1]
    win_t = jnp.transpose(cache_win_kv, (0, 2, 3, 4, 5, 1)).reshape(DB, DEPTH, 2, NSA_KV, w_src)
    b3 = lambda shp: pl.BlockSpec(shp, lambda b: (b, 0, 0))
    return pl.pallas_call(
        functools.partial(_nsa_dec_win_kernel, T=T, past_len=past_len),
        grid=(DB,),
        in_specs=[b3((1, T, NSA_QPAD)), b3((1, T, GATE_COLS)), b3((1, NSA_HEADS * T, LANES)), b3((1, NSA_HEADS * T, LANES)),
                  pl.BlockSpec((1, 1, 2, NSA_KV, w_src), lambda b: (b, layer, 0, 0, 0)),
                  b3((1, LANES, LANES)), b3((1, LANES, LANES))],
        out_specs=b3((1, T, NSA_QPAD)),
        out_shape=jax.ShapeDtypeStruct((DB, T, NSA_QPAD), BF16),
        compiler_params=_params(("parallel",)),
        name="nsa_dec_win",
    )(zb, zd, o_c, o_s, win_t, new_rows(4), new_rows(5))


def _route_top2(logits):
    lane = lax.broadcasted_iota(jnp.int32, logits.shape, 1)
    W = logits.shape[1]
    logits = jnp.where(lane < N_EXPERTS, logits, NEG_INF)
    m1 = jnp.max(logits, axis=-1, keepdims=True)
    i1 = jnp.min(jnp.where(logits == m1, lane, W), axis=-1, keepdims=True)
    rest = jnp.where(lane == i1, NEG_INF, logits)
    m2 = jnp.max(rest, axis=-1, keepdims=True)
    i2 = jnp.min(jnp.where(rest == m2, lane, W), axis=-1, keepdims=True)
    e2 = jnp.exp(m2 - m1)
    den = 1.0 + e2
    return jnp.where(lane == i1, 1.0 / den, 0.0) + jnp.where(lane == i2, e2 / den, 0.0)


def _router_kernel(x_ref, w_ref, b_ref, o_ref):
    x = x_ref[...]
    w = w_ref[...]
    x_hi = x.astype(BF16)
    x_lo = (x - x_hi.astype(F32)).astype(BF16)
    w_hi = w.astype(BF16)
    w_lo = (w - w_hi.astype(F32)).astype(BF16)
    logits = _dot(x_hi, w_hi) + _dot(x_hi, w_lo) + _dot(x_lo, w_hi)
    o_ref[...] = _route_top2(logits + b_ref[...])


def router(x2d, w_router, b_router, tm):
    n = x2d.shape[0]
    w = jnp.pad(w_router.astype(F32), ((0, 0), (0, LANES - N_EXPERTS)))
    bb = jnp.pad(b_router.astype(F32), (0, LANES - N_EXPERTS)).reshape(1, LANES)
    return pl.pallas_call(
        _router_kernel,
        grid=(n // tm,),
        in_specs=[pl.BlockSpec((tm, D_MODEL), lambda i: (i, 0)), pl.BlockSpec(w.shape, lambda i: (0, 0)),
                  pl.BlockSpec(bb.shape, lambda i: (0, 0))],
        out_specs=pl.BlockSpec((tm, LANES), lambda i: (i, 0)),
        out_shape=jax.ShapeDtypeStruct((n, LANES), F32),
        compiler_params=_params(("parallel",)),
        name="router",
    )(x2d, w, bb)


def _merge_kernel(x_ref, hm_ref, hn_ref, hx_ref, wg_ref, bg_ref, wbm_ref, wbn_ref, wbx_ref, wo_ref, g_ref, b_ref,
                  o_ref):
    x = x_ref[...]
    xb = x.astype(BF16)
    merged = None
    for c, (h_ref, w_ref) in enumerate(((hm_ref, wbm_ref), (hn_ref, wbn_ref), (hx_ref, wbx_ref))):
        sl = slice(c * D_MODEL, (c + 1) * D_MODEL)
        gate = jax.nn.sigmoid(_dot(xb, wg_ref[:, sl]) + bg_ref[:, sl])
        term = gate * _dot(h_ref[...], w_ref[...])
        merged = term if merged is None else merged + term
    mix = _dot(merged.astype(BF16), wo_ref[...])
    o_ref[...] = _layer_norm(ALPHA * x + mix, g_ref[...], b_ref[...])


def merge(x2d, hm, hn, hx, wg, bg, wbr, wo, g, b, tm):
    n = x2d.shape[0]
    row = lambda w: pl.BlockSpec((tm, w), lambda i: (i, 0))
    full = lambda a: pl.BlockSpec(a.shape, lambda i: (0,) * a.ndim, pipeline_mode=pl.Buffered(1))
    args = [x2d, hm, hn, hx, wg, bg, wbr[0], wbr[1], wbr[2], wo, g, b]
    return pl.pallas_call(
        _merge_kernel,
        grid=(n // tm,),
        in_specs=[row(a.shape[1]) for a in args[:4]] + [full(a) for a in args[4:]],
        out_specs=row(D_MODEL),
        out_shape=jax.ShapeDtypeStruct((n, D_MODEL), F32),
        compiler_params=_params(("parallel",)),
        name="merge",
    )(*args)


def _ffn_kernel(x_ref, wg_ref, wu_ref, wd_ref, g_ref, b_ref, o_ref, acc_s):
    f = pl.program_id(1)

    @pl.when(f == 0)
    def _():
        acc_s[...] = jnp.zeros(acc_s.shape, F32)

    xb = x_ref[...].astype(BF16)
    hcur = jax.nn.silu(_dot(xb, wg_ref[...])) * _dot(xb, wu_ref[...])
    acc_s[...] += _dot(hcur.astype(BF16), wd_ref[...])

    @pl.when(f == pl.num_programs(1) - 1)
    def _():
        o_ref[...] = _layer_norm(ALPHA * x_ref[...] + acc_s[...], g_ref[...], b_ref[...])


def ffn(x2d, w_up, w_down, g, b, tm, tf):
    n = x2d.shape[0]
    nf = (w_up.shape[1] // 2) // tf
    wmode = dict(pipeline_mode=pl.Buffered(1)) if nf == 1 else {}
    return pl.pallas_call(
        _ffn_kernel,
        grid=(n // tm, nf),
        in_specs=[pl.BlockSpec((tm, D_MODEL), lambda i, f: (i, 0)),
                  pl.BlockSpec((D_MODEL, tf), lambda i, f: (0, f), **wmode),
                  pl.BlockSpec((D_MODEL, tf), lambda i, f: (0, nf + f), **wmode),
                  pl.BlockSpec((tf, D_MODEL), lambda i, f: (f, 0), **wmode),
                  pl.BlockSpec((1, D_MODEL), lambda i, f: (0, 0)),
                  pl.BlockSpec((1, D_MODEL), lambda i, f: (0, 0))],
        out_specs=pl.BlockSpec((tm, D_MODEL), lambda i, f: (i, 0)),
        out_shape=jax.ShapeDtypeStruct((n, D_MODEL), F32),
        scratch_shapes=[pltpu.VMEM((tm, D_MODEL), F32)],
        compiler_params=_params(("parallel", "arbitrary")),
        name="dense_ffn",
    )(x2d, w_up, w_up, w_down, g, b)


def _moe_kernel(np_ref, x_ref, cw_ref, cwt_ref, wg_ref, wu_ref, wd_ref, g_ref, b_ref, o_ref,
                xb_s, y_s, pos_s, post_s, xg_s, ws_s, acc_s, *, blk, cap, cap_pad):
    j = pl.program_id(0)
    e = pl.program_id(1)
    f = pl.program_id(2)
    n_e = pl.num_programs(1)
    n_f = pl.num_programs(2)
    n_pass = np_ref[j * n_e + e]
    RC = min(256, blk)

    @pl.when((e == 0) & (f == 0))
    def _():
        xb_s[...] = x_ref[...].astype(BF16)
        y_s[...] = jnp.zeros(y_s.shape, F32)
        cw = cw_ref[...]
        cwt = cwt_ref[...]
        routed = jnp.where(cw != 0.0, 1.0, 0.0).astype(BF16)
        routed_t = jnp.where(cwt != 0.0, 1.0, 0.0).astype(BF16)
        for rc in range(blk // RC):
            rows_i = rc * RC + lax.broadcasted_iota(jnp.int32, (RC, blk), 0)
            cols_i = lax.broadcasted_iota(jnp.int32, (RC, blk), 1)
            before = jnp.where(cols_i < rows_i, 1.0, 0.0).astype(BF16)
            cnt = _dot(before, routed)
            pos_s[rc * RC:(rc + 1) * RC, :] = jnp.where(cw[rc * RC:(rc + 1) * RC] != 0.0, cnt, -1.0)
            rows_j = lax.broadcasted_iota(jnp.int32, (blk, RC), 0)
            cols_j = rc * RC + lax.broadcasted_iota(jnp.int32, (blk, RC), 1)
            before_t = jnp.where(rows_j < cols_j, 1.0, 0.0).astype(BF16)
            cnt_t = _dot(routed_t, before_t)
            post_s[:, rc * RC:(rc + 1) * RC] = jnp.where(cwt[:, rc * RC:(rc + 1) * RC] != 0.0, cnt_t, -1.0)

    @pl.when(f == 0)
    def _():
        prow = post_s[pl.ds(e, 1), :]
        wrow = cwt_ref[pl.ds(e, 1), :]

        def gather(u, carry):
            slot = (u * cap + lax.broadcasted_iota(jnp.int32, (cap, 1), 0)).astype(F32)
            hit = prow == slot
            xg_s[u, 0:cap, :] = _dot(jnp.where(hit, 1.0, 0.0).astype(BF16), xb_s[...]).astype(BF16)
            ws_s[u, 0:cap, :] = jnp.sum(jnp.where(hit, wrow, 0.0), axis=-1, keepdims=True)
            acc_s[u] = jnp.zeros(acc_s.shape[1:], F32)
            return carry

        lax.fori_loop(0, n_pass, gather, 0)

    def expert(u, carry):
        xg = xg_s[u, 0:cap, :]
        hcur = jax.nn.silu(_dot(xg, wg_ref[0])) * _dot(xg, wu_ref[0]) * ws_s[u, 0:cap, :]
        acc_s[u, 0:cap, :] += _dot(hcur.astype(BF16), wd_ref[0])
        return carry

    lax.fori_loop(0, n_pass, expert, 0)

    @pl.when(f == n_f - 1)
    def _():
        lane = lax.broadcasted_iota(jnp.int32, pos_s.shape, 1)
        pcol = jnp.sum(jnp.where(lane == e, pos_s[...], 0.0), axis=-1, keepdims=True)

        def scatter(u, carry):
            cc = lax.broadcasted_iota(jnp.int32, (1, cap_pad), 1)
            slot = jnp.where(cc < cap, u * cap + cc, NO_SLOT).astype(F32)
            hit = jnp.where(pcol == slot, 1.0, 0.0).astype(BF16)
            y_s[...] += _dot(hit, acc_s[u].astype(BF16))
            return carry

        lax.fori_loop(0, n_pass, scatter, 0)

    @pl.when((e == n_e - 1) & (f == n_f - 1))
    def _():
        o_ref[...] = _layer_norm(ALPHA * x_ref[...] + y_s[...], g_ref[...], b_ref[...])


def moe(x2d, cw, w_up, w_down, g, b, tf):
    n = x2d.shape[0]
    E, _, F2 = w_up.shape
    nf = (F2 // 2) // tf
    blk = _row_tile(n, MOE_BLOCK)
    cap = min(MOE_CAP, blk)
    cap_pad = -(-cap // LANES) * LANES
    max_pass = -(-blk // cap)
    nblk = n // blk
    cwt = jnp.swapaxes(cw[:, :ROUTE_ROWS], 0, 1)
    counts = jnp.sum((cw[:, :E] != 0.0).reshape(nblk, blk, E), axis=1)
    n_pass = ((counts + cap - 1) // cap).astype(jnp.int32).reshape(-1)
    return pl.pallas_call(
        functools.partial(_moe_kernel, blk=blk, cap=cap, cap_pad=cap_pad),
        grid_spec=pltpu.PrefetchScalarGridSpec(
            num_scalar_prefetch=1, grid=(nblk, E, nf),
            in_specs=[pl.BlockSpec((blk, D_MODEL), lambda j, e, f, npr: (j, 0), pipeline_mode=pl.Buffered(1)),
                      pl.BlockSpec((blk, LANES), lambda j, e, f, npr: (j, 0)),
                      pl.BlockSpec((ROUTE_ROWS, blk), lambda j, e, f, npr: (0, j)),
                      pl.BlockSpec((1, D_MODEL, tf), lambda j, e, f, npr: (e, 0, f)),
                      pl.BlockSpec((1, D_MODEL, tf), lambda j, e, f, npr: (e, 0, nf + f)),
                      pl.BlockSpec((1, tf, D_MODEL), lambda j, e, f, npr: (e, f, 0)),
                      pl.BlockSpec((1, D_MODEL), lambda j, e, f, npr: (0, 0)),
                      pl.BlockSpec((1, D_MODEL), lambda j, e, f, npr: (0, 0))],
            out_specs=pl.BlockSpec((blk, D_MODEL), lambda j, e, f, npr: (j, 0)),
            scratch_shapes=[pltpu.VMEM((blk, D_MODEL), BF16), pltpu.VMEM((blk, D_MODEL), F32),
                            pltpu.VMEM((blk, LANES), F32), pltpu.VMEM((ROUTE_ROWS, blk), F32),
                            pltpu.VMEM((max_pass, cap_pad, D_MODEL), BF16), pltpu.VMEM((max_pass, cap_pad, 1), F32),
                            pltpu.VMEM((max_pass, cap_pad, D_MODEL), F32)]),
        out_shape=jax.ShapeDtypeStruct((n, D_MODEL), F32),
        compiler_params=_params(("parallel", "arbitrary", "arbitrary")),
        name="moe_ffn",
    )(n_pass, x2d, cw, cwt, w_up, w_up, w_down, g, b)


def _layer(x, lw, l, *, mem_kv, mem_layer, mstate, decode):
    B, T, _ = x.shape
    n = B * T
    x2d = x.reshape(n, D_MODEL)
    tm = _row_tile(n, 1024)
    za, zb, zc, zcb, zd = in_proj(x2d, lw["in"], tm)
    za, zb, zc, zcb, zd = (a.reshape(B, T, -1) for a in (za, zb, zc, zcb, zd))

    if T % MLSTM_CHUNK == 0:
        h_m, ct, nt, mt = mlstm(za, zd, *mstate, lw["norm_g"], L=MLSTM_CHUNK, t_valid=MLSTM_CHUNK)
    else:
        padt = lambda a: jnp.pad(a, ((0, 0), (0, LANES - T), (0, 0)))
        h_m, ct, nt, mt = mlstm(padt(za), padt(zd), *mstate, lw["norm_g"], L=LANES, t_valid=T)
        h_m = h_m[:, :T]

    if decode is None:
        kc, vc = cmp_tokens(zc, lw["cmp_w"], lw["cmp_b"])
        h_n = nsa_prompt(zb, zd, zcb, kc, vc, tq=min(256, T), CH=min(512, T))
    else:
        h_n = nsa_decode(zb, zd, zcb, decode["nsa"], decode["win"], decode["pt"], lw["cmp_w"], lw["cmp_b"], l)

    h_x = mem_attn(zb, mem_kv, mem_layer, tq=_row_tile(T, 512))

    wg, bg = lw["in"]["mg"]
    flat = lambda a: a.reshape(n, -1)
    x1 = merge(x2d, flat(h_m), flat(h_n), flat(h_x), wg, bg, lw["w_branch"], lw["w_out"], lw["ln1_g"], lw["ln1_b"], tm)

    if l % 2 == 0:
        x2 = ffn(x1, lw["ffn_up"], lw["ffn_down"], lw["ln2_g"], lw["ln2_b"], _row_tile(n, 512), D_FF)
    else:
        cw = router(x1, lw["w_router"], lw["b_router"], _row_tile(n, 1024))
        x2 = moe(x1, cw, lw["ffn_up"], lw["ffn_down"], lw["ln2_g"], lw["ln2_b"], D_FF_EXPERT // 2)
    return x2.reshape(B, T, D_MODEL), zc, (ct, nt, mt)


def kernel(x_prompt, x_sample, mem_prompt, cache_nsa_kv, cache_win_kv, state_mlstm_C, state_mlstm_n,
           state_mlstm_m, cache_mem_kv, page_table, w_in, b_in, mlstm_norm_g, cmp_w, cmp_b, w_mem_kv,
           w_branch, w_out, ln1_g, ln1_b, ln2_g, ln2_b, ffn_w_up, ffn_w_down, moe_w_router, moe_b_router,
           moe_w_up, moe_w_down):
    B, T, _ = x_prompt.shape
    DB, TS, _ = x_sample.shape
    xp, xs = x_prompt, x_sample
    nsa_p, nsa_s, win_p, win_s = [], [], [], []
    Cp, np_, mp, Cs, ns, ms, memkv_p = [], [], [], [], [], [], []
    row1 = lambda a: a.reshape(1, -1).astype(F32)
    for l in range(DEPTH):
        wbr = (w_branch[l, 0].astype(BF16), _pad_branch_rows(w_branch[l, 1]).astype(BF16), w_branch[l, 2].astype(BF16))
        lw = dict(norm_g=mlstm_norm_g[l], cmp_w=cmp_w[l], cmp_b=cmp_b[l], w_branch=wbr,
                  w_out=w_out[l].astype(BF16), ln1_g=row1(ln1_g[l]), ln1_b=row1(ln1_b[l]),
                  ln2_g=row1(ln2_g[l]), ln2_b=row1(ln2_b[l]))
        lw["in"] = _split_in_proj(w_in[l], b_in[l])
        if l % 2 == 0:
            lw["ffn_up"] = ffn_w_up[l // 2].astype(BF16)
            lw["ffn_down"] = ffn_w_down[l // 2].astype(BF16)
        else:
            lw["ffn_up"] = moe_w_up[l // 2].astype(BF16)
            lw["ffn_down"] = moe_w_down[l // 2].astype(BF16)
            lw["w_router"] = moe_w_router[l // 2]
            lw["b_router"] = moe_b_router[l // 2]

        mkv = proj(mem_prompt.reshape(B * MEM_LEN, D_MODEL), w_mem_kv[l].astype(BF16), _row_tile(B * MEM_LEN, 512))
        mkv = mkv.reshape(B, MEM_LEN, 2 * X_WIDTH)
        st0 = (jnp.zeros((B, M_HEADS, M_DH, M_DH), F32), jnp.zeros((B, M_HEADS, M_DH), F32), jnp.zeros((B, M_HEADS), F32))
        xp, zc, st = _layer(xp, lw, l, mem_kv=mkv, mem_layer=0, mstate=st0, decode=None)
        kvn = zc.reshape(B, T, 6, NSA_GROUPS, NSA_DH)
        nsa_p.append(kvn[:, :, :4])
        win_p.append(kvn[:, T - min(WINDOW, T):, 4:])
        Cp.append(st[0]); np_.append(st[1]); mp.append(st[2])
        memkv_p.append(mkv.reshape(B, MEM_LEN, 2, X_HEADS, X_DH))

        sts = (state_mlstm_C[:, l], state_mlstm_n[:, l], state_mlstm_m[:, l])
        dec = dict(nsa=cache_nsa_kv, win=cache_win_kv, pt=page_table)
        xs, zc, st = _layer(xs, lw, l, mem_kv=cache_mem_kv, mem_layer=l, mstate=sts, decode=dec)
        kvn = zc.reshape(DB, TS, 6, NSA_GROUPS, NSA_DH)
        nsa_s.append(kvn[:, :, :4])
        win_s.append(jnp.concatenate([cache_win_kv[:, :, l].astype(F32), kvn[:, :, 4:]], axis=1)[:, TS:])
        Cs.append(st[0]); ns.append(st[1]); ms.append(st[2])
    return (xp, xs,
            jnp.stack(nsa_p, axis=2), jnp.stack(nsa_s, axis=2),
            jnp.stack(win_p, axis=2), jnp.stack(win_s, axis=2),
            jnp.stack(Cp, axis=1), jnp.stack(np_, axis=1), jnp.stack(mp, axis=1),
            jnp.stack(Cs, axis=1), jnp.stack(ns, axis=1), jnp.stack(ms, axis=1),
            jnp.stack(memkv_p, axis=2))
```

```python
import functools

import numpy as np
import jax
import jax.numpy as jnp
from jax import lax
from jax.experimental import pallas as pl
from jax.experimental.pallas import tpu as pltpu

D_MODEL = 1024
DEPTH = 2
BRANCH_WIDTH = 512
N_BRANCH = 3
M_HEADS = 4
M_DH = BRANCH_WIDTH // M_HEADS
M_WIDTH = M_HEADS * M_DH
NSA_HEADS = 8
NSA_DH = BRANCH_WIDTH // NSA_HEADS
NSA_WIDTH = NSA_HEADS * NSA_DH
NSA_GROUPS = 2
NSA_HPG = NSA_HEADS // NSA_GROUPS
NSA_KV = NSA_GROUPS * NSA_DH
CMP_LEN = 32
CMP_STRIDE = 16
SEL_LEN = 64
SEL_TOPN = 16
WINDOW = 512
MEM_LEN = 256
X_HEADS = 4
X_DH = BRANCH_WIDTH // X_HEADS
X_WIDTH = X_HEADS * X_DH
D_FF = 2816
N_EXPERTS = 8
TOP_K = 2
D_FF_EXPERT = 3584
ALPHA = (2.0 * DEPTH) ** 0.25
LN_EPS = 1e-5
IN_SPLITS = (M_WIDTH, M_WIDTH, M_WIDTH, M_HEADS, M_HEADS, M_WIDTH,
             NSA_WIDTH, 6 * NSA_KV, 3 * NSA_HEADS, X_WIDTH, N_BRANCH * D_MODEL)
LOG2E = 1.4426950408889634
NSA_SLOPES = tuple(LOG2E * 2.0 ** (-8.0 * (h + 1) / NSA_HEADS) for h in range(NSA_HEADS))
MASK_BIAS = -(2.0 ** 30)
SEL_FEAT = 64
SEL_GROUPS = 8
SLOPE_PARTS = 3

LANES = 128
SUBLANES = 8
VMEM_LIMIT = 56 * 1024 * 1024
PAGES_PER_STEP = 64
MLSTM_SEQS = 1
MLSTM_CHUNK = 256
GATE_COLS = LANES
NG_OFF = 2 * M_HEADS
NSA_QPAD = NSA_HEADS * LANES
MOE_BLOCK = 1024
MOE_CAP = 288
MOE_WEIGHT_SLOTS = 3
NO_SLOT = -2
ROUTE_ROWS = 16

F32 = jnp.float32
BF16 = jnp.bfloat16
NEG_INF = float("-inf")
HIGHEST = lax.Precision.HIGHEST


def _dot(a, b, precision=None):
    return jnp.dot(a, b, preferred_element_type=F32, precision=precision)


def _dot_nt(a, b, precision=None):
    return lax.dot_general(a, b, (((1,), (1,)), ((), ())), preferred_element_type=F32, precision=precision)


def _dot_tn(a, b):
    return lax.dot_general(a, b, (((0,), (0,)), ((), ())), preferred_element_type=F32)


def _params(sem):
    return pltpu.CompilerParams(dimension_semantics=sem, vmem_limit_bytes=VMEM_LIMIT)


def _masked_softmax2(s, mask):
    s = jnp.where(mask, s, NEG_INF)
    mx = jnp.max(s, axis=-1, keepdims=True)
    mx = jnp.where(mx > NEG_INF, mx, 0.0)
    p = jnp.where(mask, jnp.exp2(s - mx), 0.0)
    return p / jnp.maximum(jnp.sum(p, axis=-1, keepdims=True), 1e-30)


def _layer_norm(xf, g, b):
    mu = jnp.mean(xf, axis=-1, keepdims=True)
    var = jnp.mean(jnp.square(xf - mu), axis=-1, keepdims=True)
    return (xf - mu) * lax.rsqrt(var + LN_EPS) * g + b


def _row_tile(n, pref):
    return pref if n % pref == 0 else n


def _in_proj_kernel(x_ref, wa_ref, wb_ref, wc_ref, wd_ref, ba_ref, bb_ref, bc_ref, bd_ref,
                    oa_ref, ob_ref, oc_ref, ocb_ref, od_ref):
    x = x_ref[...].astype(BF16)
    oa_ref[...] = (_dot(x, wa_ref[...]) + ba_ref[...]).astype(BF16)
    ob_ref[...] = (_dot(x, wb_ref[...]) + bb_ref[...]).astype(BF16)
    c = _dot(x, wc_ref[...]) + bc_ref[...]
    oc_ref[...] = c
    ocb_ref[...] = c.astype(BF16)
    od_ref[...] = _dot(x, wd_ref[...]) + bd_ref[...]


def _group_select():
    return jax.nn.one_hot(np.arange(NSA_HEADS) // NSA_HPG, NSA_GROUPS, dtype=F32)


def _pad_branch_rows(w_nsa):
    return jnp.einsum("hem,hg->hgem", w_nsa.reshape(NSA_HEADS, NSA_DH, -1), _group_select()).reshape(NSA_QPAD, -1)


def _split_in_proj(w_in_l, b_in_l):
    offs = np.cumsum((0,) + IN_SPLITS)
    w = [w_in_l[:, offs[i]:offs[i + 1]] for i in range(len(IN_SPLITS))]
    b = [b_in_l[offs[i]:offs[i + 1]] for i in range(len(IN_SPLITS))]
    mq, mk, mv, mi, mf, mo, nq, nkv, ng, xq, mg = range(11)
    pad = GATE_COLS - 2 * M_HEADS - 3 * NSA_HEADS

    def cat(ids, zpad=0):
        ww = jnp.concatenate([w[i] for i in ids], axis=1)
        bb = jnp.concatenate([b[i] for i in ids])
        if zpad:
            ww = jnp.pad(ww, ((0, 0), (0, zpad)))
            bb = jnp.pad(bb, (0, zpad))
        return ww.astype(BF16), bb.reshape(1, -1).astype(F32)

    scale = NSA_DH ** -0.5 * LOG2E
    w[nq] = jnp.einsum("dhe,hg->dhge", w[nq].reshape(D_MODEL, NSA_HEADS, NSA_DH) * scale, _group_select()).reshape(D_MODEL, NSA_QPAD)
    b[nq] = jnp.einsum("he,hg->hge", b[nq].reshape(NSA_HEADS, NSA_DH) * scale, _group_select()).reshape(NSA_QPAD)
    return dict(a=cat([mq, mk, mv, mo]), b=cat([nq, xq]), c=cat([nkv]), d=cat([mi, mf, ng], pad), mg=cat([mg]))


def in_proj(x2d, wp, tm):
    n = x2d.shape[0]
    (wa, ba), (wb, bb), (wc, bc), (wd, bd) = wp["a"], wp["b"], wp["c"], wp["d"]
    full = lambda arr: pl.BlockSpec(arr.shape, lambda i: (0, 0), pipeline_mode=pl.Buffered(1))
    row = lambda w: pl.BlockSpec((tm, w), lambda i: (i, 0))
    return pl.pallas_call(
        _in_proj_kernel,
        grid=(n // tm,),
        in_specs=[row(D_MODEL), full(wa), full(wb), full(wc), full(wd), full(ba), full(bb), full(bc), full(bd)],
        out_specs=[row(wa.shape[1]), row(wb.shape[1]), row(wc.shape[1]), row(wc.shape[1]), row(wd.shape[1])],
        out_shape=[jax.ShapeDtypeStruct((n, wa.shape[1]), BF16), jax.ShapeDtypeStruct((n, wb.shape[1]), BF16),
                   jax.ShapeDtypeStruct((n, wc.shape[1]), F32), jax.ShapeDtypeStruct((n, wc.shape[1]), BF16),
                   jax.ShapeDtypeStruct((n, wd.shape[1]), F32)],
        compiler_params=_params(("parallel",)),
        name="in_proj",
    )(x2d, wa, wb, wc, wd, ba, bb, bc, bd)


def _proj_kernel(x_ref, w_ref, o_ref):
    o_ref[...] = _dot(x_ref[...].astype(BF16), w_ref[...])


def proj(x2d, w_bf16, tm):
    n, k = x2d.shape
    m = w_bf16.shape[1]
    return pl.pallas_call(
        _proj_kernel,
        grid=(n // tm,),
        in_specs=[pl.BlockSpec((tm, k), lambda i: (i, 0)), pl.BlockSpec((k, m), lambda i: (0, 0))],
        out_specs=pl.BlockSpec((tm, m), lambda i: (i, 0)),
        out_shape=jax.ShapeDtypeStruct((n, m), F32),
        compiler_params=_params(("parallel",)),
        name="mem_proj",
    )(x2d, w_bf16)


def _log_sigmoid(x):
    return jnp.minimum(x, 0.0) - jnp.log1p(jnp.exp(-jnp.abs(x)))


def _mlstm_kernel(q_ref, k_ref, v_ref, og_ref, gc_ref, gr_ref, c0_ref, n0_ref, m0_ref, ng_ref,
                  h_ref, ct_ref, nt_ref, mt_ref, c_s, n_s, m_s, *, L, t_valid, nb):
    ci = pl.program_id(1)

    @pl.when(ci == 0)
    def _():
        c_s[...] = c0_ref[...]
        n_s[...] = n0_ref[...]
        m_s[...] = m0_ref[...]

    row = lax.broadcasted_iota(jnp.int32, (L, L), 0)
    col = lax.broadcasted_iota(jnp.int32, (L, L), 1)
    causal = row >= col
    tri = causal.astype(F32)
    tri_t = (row <= col).astype(F32)
    rvalid = lax.broadcasted_iota(jnp.int32, (L, GATE_COLS), 0) < t_valid
    cvalid = lax.broadcasted_iota(jnp.int32, (2 * M_HEADS, L), 1) < t_valid
    scale = M_DH ** -0.5

    for bb, h in [(bb, h) for bb in range(nb) for h in range(M_HEADS)]:
        if h == 0:
            gc = gc_ref[bb]
            gr = gr_ref[bb]
            lf_c = jnp.where(rvalid, _log_sigmoid(gc), 0.0)
            lf_r = jnp.where(cvalid, _log_sigmoid(gr), 0.0)
            ig_c = jnp.where(rvalid, gc, NEG_INF)
            ig_r = jnp.where(cvalid, gr, NEG_INF)
            b_c = _dot(tri, lf_c, HIGHEST)
            b_r = _dot(lf_r, tri_t, HIGHEST)
        sl = slice(h * M_DH, (h + 1) * M_DH)
        q = q_ref[bb, :, sl]
        k = k_ref[bb, :, sl]
        v = v_ref[bb, :, sl]
        bc = b_c[:, M_HEADS + h:M_HEADS + h + 1]
        br = b_r[M_HEADS + h:M_HEADS + h + 1, :]
        igc = ig_c[:, h:h + 1]
        igr = ig_r[h:h + 1, :]
        m_prev = m_s[bb, h, 0:1, 0:1]
        c_prev = c_s[bb, h]
        n_prev = n_s[bb, h:h + 1, :]

        dmat = jnp.where(causal, bc - br + igr, NEG_INF)
        inter = bc + m_prev
        m_t = jnp.maximum(inter, jnp.max(dmat, axis=-1, keepdims=True))
        s = _dot_nt(q, k) * scale * jnp.exp(dmat - m_t)
        sc_in = jnp.exp(inter - m_t)
        qf = q.astype(F32)
        num = sc_in * _dot_nt(q, c_prev.astype(BF16)) + _dot(s.astype(BF16), v)
        den = sc_in * jnp.sum(qf * n_prev, axis=-1, keepdims=True) + jnp.sum(s, axis=-1, keepdims=True)
        hh = num / jnp.maximum(jnp.abs(den), jnp.exp(-m_t))

        b_last = bc[L - 1:L, :]
        dec_c = b_last - bc + igc
        dec_r = b_last - br + igr
        m_new = jnp.maximum(b_last + m_prev, jnp.max(dec_r, axis=-1, keepdims=True))
        ws_c = jnp.exp(dec_c - m_new) * scale
        sc = jnp.exp(b_last + m_prev - m_new)
        vf = v.astype(F32)
        kf = k.astype(F32)
        c_new = sc * c_prev + _dot_tn((vf * ws_c).astype(BF16), k)
        n_new = sc * n_prev + jnp.sum(kf * ws_c, axis=0, keepdims=True)
        c_s[bb, h] = c_new
        n_s[bb, h:h + 1, :] = n_new
        m_s[bb, h] = jnp.broadcast_to(m_new, m_s.shape[2:])

        og = og_ref[bb, :, sl].astype(F32)
        hh = hh * jax.nn.sigmoid(og)
        mu = jnp.mean(hh, axis=-1, keepdims=True)
        var = jnp.mean(jnp.square(hh - mu), axis=-1, keepdims=True)
        hn = (hh - mu) * lax.rsqrt(var + LN_EPS) * ng_ref[:, sl]
        h_ref[bb, :, sl] = hn.astype(h_ref.dtype)

    ct_ref[...] = c_s[...]
    nt_ref[...] = n_s[...]
    mt_ref[...] = m_s[...]


def mlstm(za, zd, c0, n0, m0, norm_g, *, L, t_valid):
    B, T, _ = za.shape
    nc = T // L
    gr = jnp.swapaxes(zd[:, :, :2 * M_HEADS], 1, 2)
    m0b = jnp.broadcast_to(m0.astype(F32)[:, :, None, None], (B, M_HEADS, SUBLANES, LANES))
    nb = MLSTM_SEQS if B % MLSTM_SEQS == 0 else 1
    colspec = lambda j: pl.BlockSpec((nb, L, M_WIDTH), lambda b, c, j=j: (b, c, j))
    st = lambda shp: pl.BlockSpec((nb,) + shp, lambda b, c: (b,) + (0,) * len(shp))
    kern = functools.partial(_mlstm_kernel, L=L, t_valid=t_valid, nb=nb)
    h, ct, nt, mt = pl.pallas_call(
        kern,
        grid=(B // nb, nc),
        in_specs=[colspec(0), colspec(1), colspec(2), colspec(3),
                  pl.BlockSpec((nb, L, GATE_COLS), lambda b, c: (b, c, 0)),
                  pl.BlockSpec((nb, 2 * M_HEADS, L), lambda b, c: (b, 0, c)),
                  st((M_HEADS, M_DH, M_DH)), st((M_HEADS, M_DH)), st((M_HEADS, SUBLANES, LANES)),
                  pl.BlockSpec((1, M_WIDTH), lambda b, c: (0, 0))],
        out_specs=[pl.BlockSpec((nb, L, M_WIDTH), lambda b, c: (b, c, 0)),
                   st((M_HEADS, M_DH, M_DH)), st((M_HEADS, M_DH)), st((M_HEADS, SUBLANES, LANES))],
        out_shape=[jax.ShapeDtypeStruct((B, T, M_WIDTH), BF16),
                   jax.ShapeDtypeStruct((B, M_HEADS, M_DH, M_DH), F32),
                   jax.ShapeDtypeStruct((B, M_HEADS, M_DH), F32),
                   jax.ShapeDtypeStruct((B, M_HEADS, SUBLANES, LANES), F32)],
        scratch_shapes=[pltpu.VMEM((nb, M_HEADS, M_DH, M_DH), F32), pltpu.VMEM((nb, M_HEADS, M_DH), F32),
                        pltpu.VMEM((nb, M_HEADS, SUBLANES, LANES), F32)],
        compiler_params=_params(("parallel", "arbitrary")),
        name="mlstm",
    )(za, za, za, za, zd, gr, c0.astype(F32), n0.astype(F32), m0b, norm_g.reshape(1, M_WIDTH).astype(F32))
    return h, ct, nt, mt[:, :, 0, 0]


def _mem_attn_kernel(q_ref, k_ref, v_ref, o_ref):
    scale = X_DH ** -0.5
    for h in range(X_HEADS):
        sl = slice(h * X_DH, (h + 1) * X_DH)
        k = k_ref[:, sl] if len(k_ref.shape) == 2 else k_ref[:, h, :]
        v = v_ref[:, sl] if len(v_ref.shape) == 2 else v_ref[:, h, :]
        s = _dot_nt(q_ref[0, :, sl], k.astype(BF16)) * scale
        mx = jnp.max(s, axis=-1, keepdims=True)
        p = jnp.exp(s - mx)
        p = p / jnp.sum(p, axis=-1, keepdims=True)
        o_ref[0, :, sl] = _dot(p.astype(BF16), v.astype(BF16)).astype(o_ref.dtype)


def mem_attn(zb, kv, layer, tq):
    B, T, _ = zb.shape
    if kv.ndim == 3:
        kv_specs = [pl.BlockSpec((None, MEM_LEN, X_WIDTH), lambda b, i, j=j: (b, 0, j)) for j in range(2)]
    else:
        kv_blk = (None, MEM_LEN, None, None, X_HEADS, X_DH)
        kv_specs = [pl.BlockSpec(kv_blk, lambda b, i, j=j: (b, 0, layer, j, 0, 0)) for j in range(2)]
    return pl.pallas_call(
        _mem_attn_kernel,
        grid=(B, T // tq),
        in_specs=[pl.BlockSpec((1, tq, X_WIDTH), lambda b, i: (b, i, NSA_QPAD // X_WIDTH))] + kv_specs,
        out_specs=pl.BlockSpec((1, tq, X_WIDTH), lambda b, i: (b, i, 0)),
        out_shape=jax.ShapeDtypeStruct((B, T, X_WIDTH), BF16),
        compiler_params=_params(("parallel", "parallel")),
        name="mem_attn",
    )(zb, kv, kv)


def _head_rows(q_ref, nq):
    parts = [q_ref[0, :, hd * LANES:(hd + 1) * LANES] for hd in range(NSA_HEADS)]
    if nq % 16:
        return jnp.concatenate([p.astype(F32) for p in parts], axis=0).astype(BF16)
    return jnp.concatenate(parts, axis=0)


def _store_heads(o_ref, gate, o_c, o_s, o_w, nq):
    lane = lax.broadcasted_iota(jnp.int32, (nq, LANES), 1)
    for hd in range(NSA_HEADS):
        r = slice(hd * nq, (hd + 1) * nq)
        gcol = lambda br: gate[:, NG_OFF + br * NSA_HEADS + hd:NG_OFF + br * NSA_HEADS + hd + 1]
        val = gcol(0) * o_c[r] + gcol(1) * o_s[r] + gcol(2) * o_w[r]
        keep = (lane >= NSA_DH) if hd // NSA_HPG == 1 else (lane < NSA_DH)
        o_ref[0, :, hd * LANES:(hd + 1) * LANES] = jnp.where(keep, val, 0.0).astype(o_ref.dtype)


def _overlap(tok, blk):
    c_start = tok * CMP_STRIDE
    s_start = blk * SEL_LEN
    return ((c_start < s_start + SEL_LEN) & (c_start + CMP_LEN > s_start)).astype(F32)


def _select_blocks(imp, tpos, n_sel, n_top):
    nq, W = imp.shape
    blk = lax.broadcasted_iota(jnp.int32, (nq, W), 1)
    cur = tpos // SEL_LEN
    forced = (blk == 0) | (blk == cur) | (blk == cur - 1)
    v = jnp.where(forced, jnp.inf, jnp.where(blk <= cur, imp, NEG_INF))
    v = jnp.where(blk < n_sel, v, NEG_INF)
    ahead = jnp.zeros((nq, W), F32)
    for j in range(n_sel):
        vj = v[:, j:j + 1]
        ahead = ahead + jnp.where(vj > v, 1.0, jnp.where(vj == v, jnp.where(blk > j, 1.0, 0.0), 0.0))
    return jnp.where((ahead < n_top) & (blk < n_sel), 1.0, 0.0)


def _select_blocks_t(imp_t, tpos_row, n_sel, n_top):
    n_blk, nq = imp_t.shape
    blk = lax.broadcasted_iota(jnp.int32, (n_blk, nq), 0)
    cur = tpos_row // SEL_LEN
    forced = (blk == 0) | (blk == cur) | (blk == cur - 1)
    v = jnp.where(forced, jnp.inf, jnp.where(blk <= cur, imp_t, NEG_INF))
    v = jnp.where(blk < n_sel, v, NEG_INF)
    n_rg = n_blk // SUBLANES
    vg = [v[rg * SUBLANES:(rg + 1) * SUBLANES] for rg in range(n_rg)]
    bg = [rg * SUBLANES + lax.broadcasted_iota(jnp.int32, (SUBLANES, nq), 0) for rg in range(n_rg)]

    ahead = [jnp.zeros((SUBLANES, nq), F32) for _ in range(n_rg)]
    for j in range(n_sel):
        vj = v[j:j + 1, :]
        for rg in range(n_rg):
            if rg * SUBLANES > j:
                inc = jnp.where(vj >= vg[rg], 1.0, 0.0)
            elif (rg + 1) * SUBLANES - 1 <= j:
                inc = jnp.where(vj > vg[rg], 1.0, 0.0)
            else:
                inc = jnp.where(vj > vg[rg], 1.0, jnp.where(vj == vg[rg], jnp.where(bg[rg] > j, 1.0, 0.0), 0.0))
            ahead[rg] = ahead[rg] + inc
    ahead = jnp.concatenate(ahead, axis=0)
    return jnp.where((ahead < n_top) & (blk < n_sel), 1.0, 0.0)


def _online_update(s, m_old, l_old):
    m_new = jnp.maximum(m_old, jnp.max(s, axis=-1, keepdims=True))
    alpha = jnp.exp2(m_old - m_new)
    p = jnp.exp2(s - m_new)
    return p, m_new, alpha, alpha * l_old + jnp.sum(p, axis=-1, keepdims=True)


def _cmp_tokens_kernel(xk_ref, xv_ref, bd_ref, b_ref, kc_ref, vc_ref, *, nsub):
    for c, (x_ref, o_ref) in enumerate(((xk_ref, kc_ref), (xv_ref, vc_ref))):
        a0 = jnp.zeros((nsub, LANES), F32)
        a1 = jnp.zeros((nsub, LANES), F32)
        for j in range(CMP_STRIDE):
            xj = x_ref[0, :, j, :].astype(BF16)
            a0 = a0 + _dot(xj, bd_ref[c, 0, j])
            a1 = a1 + _dot(xj, bd_ref[c, 1, j])
        tok = a0 + pltpu.roll(a1, nsub - 1, axis=0) + b_ref[c]
        o_ref[0] = tok.astype(o_ref.dtype)


def _cmp_blockdiag(cmp_w_l):
    R = CMP_LEN // CMP_STRIDE
    w = cmp_w_l.astype(F32).reshape(2, R, CMP_STRIDE, NSA_DH, NSA_DH)
    eye = jnp.eye(NSA_GROUPS, dtype=F32)
    return jnp.einsum("ab,crjde->crjadbe", eye, w).reshape(2, R, CMP_STRIDE, LANES, LANES).astype(BF16)


def _cmp_bias(cmp_b_l):
    return jnp.tile(cmp_b_l.astype(F32), (1, NSA_GROUPS)).reshape(2, 1, LANES)


def cmp_tokens(zc, cmp_w_l, cmp_b_l):
    B, T, W = zc.shape
    nsub = T // CMP_STRIDE
    x4 = zc.reshape(B, nsub, CMP_STRIDE, W)
    bd = _cmp_blockdiag(cmp_w_l)
    bias = _cmp_bias(cmp_b_l)
    spec = lambda j: pl.BlockSpec((1, nsub, CMP_STRIDE, LANES), lambda b, j=j: (b, 0, 0, j))
    return pl.pallas_call(
        functools.partial(_cmp_tokens_kernel, nsub=nsub),
        grid=(B,),
        in_specs=[spec(0), spec(1), pl.BlockSpec(bd.shape, lambda b: (0,) * 5), pl.BlockSpec(bias.shape, lambda b: (0, 0, 0))],
        out_specs=[pl.BlockSpec((1, nsub, LANES), lambda b: (b, 0, 0))] * 2,
        out_shape=[jax.ShapeDtypeStruct((B, nsub, LANES), BF16)] * 2,
        compiler_params=_params(("parallel",)),
        name="nsa_cmp_tokens",
    )(x4, x4, bd, bias)


def _nsa_prompt_kernel(q_ref, g_ref, kca_ref, vc_ref, ka_ref, vst_ref, kwa_ref, vw_ref, qf_ref, o_ref,
                       qaug_s, s_buf, p_buf, w_buf, pw_buf, pc_s, m_s, l_s, a_s, acc_s, oc_s, ow_s, *, tq, T, CH):
    start = pl.program_id(1) * tq
    n_cmp_rows = kca_ref.shape[1]
    n_sel = T // SEL_LEN
    n_top = min(SEL_TOPN, n_sel)
    n_blk = -(-n_sel // SUBLANES) * SUBLANES
    nq_all = NSA_HEADS * tq
    qw = nq_all // SEL_GROUPS
    tpos_row = start + lax.broadcasted_iota(jnp.int32, (1, tq), 1)
    tpos_all = jnp.concatenate([tpos_row] * NSA_HEADS, axis=1)
    qcols = lambda qi: slice(qi * qw, (qi + 1) * qw)
    tiles = [(qi, ct, slice(qi * qw + ct * LANES, qi * qw + (ct + 1) * LANES), slice(ct * LANES, (ct + 1) * LANES))
             for qi in range(SEL_GROUPS) for ct in range(qw // LANES)]

    qaug_s[:, 0:LANES] = _head_rows(q_ref, tq)
    for hd in range(NSA_HEADS):
        qaug_s[hd * tq:(hd + 1) * tq, LANES:2 * LANES] = jnp.broadcast_to(qf_ref[hd:hd + 1, :], (tq, LANES)).astype(BF16)

    def softmax_tile(s_t):
        mx = jnp.max(s_t, axis=0, keepdims=True)
        mx = jnp.where(mx > NEG_INF, mx, 0.0)
        p_t = jnp.exp2(s_t - mx)
        return p_t, jnp.maximum(jnp.sum(p_t, axis=0, keepdims=True), 1e-30)

    c_end = lax.broadcasted_iota(jnp.int32, (n_cmp_rows, 1), 0) * CMP_STRIDE + CMP_LEN - 1
    for qi in range(SEL_GROUPS):
        s_buf[qi, 0:n_cmp_rows, :] = _dot_nt(kca_ref[0], qaug_s[qcols(qi), :])
    for qi, ct, cols, tc in tiles:
        p_t, den = softmax_tile(jnp.where(c_end <= tpos_all[:, cols], s_buf[qi, 0:n_cmp_rows, tc], NEG_INF))
        pc_s[:, cols] = p_t / den
    oc_s[...] = _dot_tn(vc_ref[0], pc_s[...].astype(BF16))
    ov_t = _overlap(lax.broadcasted_iota(jnp.int32, (n_blk, n_cmp_rows), 1),
                    lax.broadcasted_iota(jnp.int32, (n_blk, n_cmp_rows), 0))
    lane = lax.broadcasted_iota(jnp.int32, (tq, LANES), 1)
    feat_sel = [None] * NSA_HEADS
    for g in range(NSA_GROUPS):
        pg = pc_s[:, g * NSA_HPG * tq:(g * NSA_HPG + 1) * tq]
        for p in range(1, NSA_HPG):
            pg = pg + pc_s[:, (g * NSA_HPG + p) * tq:(g * NSA_HPG + p + 1) * tq]
        sel_t = _select_blocks_t(_dot(ov_t, pg, HIGHEST), tpos_row, n_sel, n_top)
        sel = jnp.concatenate([sel_t, jnp.zeros((LANES - n_blk, tq), F32)], axis=0).T
        unsel = jnp.where(lane < SEL_FEAT, (1.0 - sel) * MASK_BIAS, 0.0)
        for p in range(NSA_HPG):
            hd = g * NSA_HPG + p
            feat_sel[hd] = (unsel + qf_ref[hd:hd + 1, :]).astype(BF16)

    wk = WINDOW + tq
    base_w = pl.multiple_of(start, tq)
    wrows = WINDOW + LANES
    for qi in range(SEL_GROUPS):
        w_buf[qi] = _dot_nt(kwa_ref[0, pl.ds(base_w, wk), :], qaug_s[qcols(qi), :])
    for qi, ct, cols, tc in tiles:
        off = cols.start % tq
        band = slice(off, off + wrows)
        wpos = start - WINDOW + off + lax.broadcasted_iota(jnp.int32, (wrows, 1), 0)
        tp = tpos_all[:, cols]
        first = jnp.maximum(tp - (WINDOW - 1), 0)
        s_t = jnp.where(wpos >= first, jnp.where(wpos <= tp, w_buf[qi, band, tc], NEG_INF), NEG_INF)
        p_t, den = softmax_tile(s_t)
        pw_buf[qi, band, tc] = p_t.astype(BF16)
        if off:
            pw_buf[qi, 0:off, tc] = jnp.zeros((off, LANES), BF16)
        if off + wrows < wk:
            pw_buf[qi, off + wrows:wk, tc] = jnp.zeros((wk - off - wrows, LANES), BF16)
        a_s[:, cols] = den
    for qi in range(SEL_GROUPS):
        ow_s[:, qcols(qi)] = _dot_tn(vw_ref[0, pl.ds(base_w, wk), :], pw_buf[qi]) / a_s[:, qcols(qi)]

    for hd in range(NSA_HEADS):
        qaug_s[hd * tq:(hd + 1) * tq, LANES:2 * LANES] = feat_sel[hd]

    m_s[...] = jnp.full(m_s.shape, NEG_INF, F32)
    l_s[...] = jnp.zeros(l_s.shape, F32)
    acc_s[...] = jnp.zeros(acc_s.shape, F32)

    def issue(c, qi):
        base = pl.multiple_of(c * CH, CH)
        s_buf[qi] = _dot_nt(ka_ref[0, pl.ds(base, CH), :], qaug_s[qi * qw:(qi + 1) * qw, :])

    def absorb(c, qi, diag):
        for ct in range(qw // LANES):
            cols = slice(qi * qw + ct * LANES, qi * qw + (ct + 1) * LANES)
            s_t = s_buf[qi, :, ct * LANES:(ct + 1) * LANES]
            if diag:
                kpos = c * CH + lax.broadcasted_iota(jnp.int32, (CH, 1), 0)
                s_t = jnp.where(kpos <= tpos_all[:, cols], s_t, NEG_INF)
            m_old = m_s[:, cols]
            m_new = jnp.maximum(m_old, jnp.max(s_t, axis=0, keepdims=True))
            alpha = jnp.exp2(m_old - m_new)
            p_t = jnp.exp2(s_t - m_new)
            m_s[:, cols] = m_new
            a_s[:, cols] = alpha
            l_s[:, cols] = alpha * l_s[:, cols] + jnp.sum(p_t, axis=0, keepdims=True)
            p_buf[qi, :, ct * LANES:(ct + 1) * LANES] = p_t.astype(BF16)
        cols = slice(qi * qw, (qi + 1) * qw)
        acc_s[:, cols] = a_s[:, cols] * acc_s[:, cols] + _dot(vst_ref[0, c], p_buf[qi])

    def full_chunk(c, carry):
        for qi in range(SEL_GROUPS):
            absorb(c, qi, False)
            issue(c + 1, qi)
        return carry

    n_full = start // CH
    for qi in range(SEL_GROUPS):
        issue(0, qi)
    lax.fori_loop(0, n_full, full_chunk, 0)
    for qi in range(SEL_GROUPS):
        absorb(n_full, qi, True)

    gate_t = jax.nn.sigmoid(g_ref[0]).T
    for hd in range(NSA_HEADS):
        cols = slice(hd * tq, (hd + 1) * tq)
        grow = lambda br: gate_t[NG_OFF + br * NSA_HEADS + hd:NG_OFF + br * NSA_HEADS + hd + 1, :]
        o_s = acc_s[:, cols] / jnp.maximum(l_s[:, cols], 1e-30)
        val = (grow(0) * oc_s[:, cols] + grow(1) * o_s + grow(2) * ow_s[:, cols]).T
        keep = (lane >= NSA_DH) if hd // NSA_HPG == 1 else (lane < NSA_DH)
        o_ref[0, :, hd * LANES:(hd + 1) * LANES] = jnp.where(keep, val, 0.0).astype(o_ref.dtype)


def _bf16_parts(x, n):
    parts = []
    for _ in range(n):
        p = float(np.asarray(x, np.float32).astype(jnp.bfloat16).astype(np.float32))
        parts.append(p)
        x = x - p
    return parts


def _slope_features():
    qf = np.zeros((NSA_HEADS, LANES), np.float32)
    for hd in range(NSA_HEADS):
        for p, s_p in enumerate(_bf16_parts(NSA_SLOPES[hd], SLOPE_PARTS)):
            qf[hd, SEL_FEAT + 2 * p] = s_p * SEL_LEN
            qf[hd, SEL_FEAT + 2 * p + 1] = s_p
    return jnp.asarray(qf)


def _key_features(pos, block_onehot):
    pos = np.asarray(pos)
    ok = pos >= 0
    kf = np.zeros((pos.shape[0], LANES), np.float32)
    if block_onehot:
        kf[np.arange(pos.shape[0])[ok], pos[ok] // SEL_LEN] = 1.0
    for p in range(SLOPE_PARTS):
        kf[ok, SEL_FEAT + 2 * p] = pos[ok] // SEL_LEN
        kf[ok, SEL_FEAT + 2 * p + 1] = pos[ok] % SEL_LEN
    return jnp.asarray(kf, BF16)


def nsa_prompt(zb, zd, zcb, kc, vc, tq, CH):
    B, T, _ = zb.shape
    nsub = kc.shape[1]
    assert T % CH == 0 and CH % tq == 0 and T // SEL_LEN <= SEL_FEAT and nsub <= CH
    qf = _slope_features()
    with_feats = lambda k, feats: jnp.concatenate([k, jnp.broadcast_to(feats[None], (B,) + feats.shape)], axis=2)
    k_aug = with_feats(zcb[:, :, 2 * NSA_KV:3 * NSA_KV], _key_features(np.arange(T), True))
    kc_aug = with_feats(kc, _key_features(np.arange(nsub) * CMP_STRIDE + CMP_LEN - 1, False))
    kw = jnp.pad(zcb[:, :, 4 * NSA_KV:5 * NSA_KV], ((0, 0), (WINDOW, 0), (0, 0)))
    kw_aug = with_feats(kw, _key_features(np.arange(T + WINDOW) - WINDOW, False))
    vs_t = jnp.swapaxes(zcb[:, :, 3 * NSA_KV:4 * NSA_KV].reshape(B, T // CH, CH, NSA_KV), 2, 3)
    vw = jnp.pad(zcb[:, :, 5 * NSA_KV:6 * NSA_KV], ((0, 0), (WINDOW, 0), (0, 0)))
    per_b = lambda rows_, w: pl.BlockSpec((1, rows_, w), lambda b, i: (b, 0, 0))
    kern = functools.partial(_nsa_prompt_kernel, tq=tq, T=T, CH=CH)
    nq_all = NSA_HEADS * tq
    qw = nq_all // SEL_GROUPS
    wk = WINDOW + tq
    assert qw % LANES == 0
    return pl.pallas_call(
        kern,
        grid=(B, T // tq),
        in_specs=[pl.BlockSpec((1, tq, NSA_QPAD), lambda b, i: (b, i, 0)),
                  pl.BlockSpec((1, tq, GATE_COLS), lambda b, i: (b, i, 0)),
                  per_b(nsub, 2 * LANES), per_b(nsub, LANES),
                  per_b(T, 2 * LANES),
                  pl.BlockSpec((1, T // CH, NSA_KV, CH), lambda b, i: (b, 0, 0, 0)),
                  per_b(T + WINDOW, 2 * LANES), per_b(T + WINDOW, LANES),
                  pl.BlockSpec(qf.shape, lambda b, i: (0, 0))],
        out_specs=pl.BlockSpec((1, tq, NSA_QPAD), lambda b, i: (b, i, 0)),
        out_shape=jax.ShapeDtypeStruct((B, T, NSA_QPAD), BF16),
        scratch_shapes=[pltpu.VMEM((nq_all, 2 * LANES), BF16),
                        pltpu.VMEM((SEL_GROUPS, CH, qw), F32), pltpu.VMEM((SEL_GROUPS, CH, qw), BF16),
                        pltpu.VMEM((SEL_GROUPS, wk, qw), F32), pltpu.VMEM((SEL_GROUPS, wk, qw), BF16),
                        pltpu.VMEM((nsub, nq_all), F32),
                        pltpu.VMEM((1, nq_all), F32), pltpu.VMEM((1, nq_all), F32), pltpu.VMEM((1, nq_all), F32),
                        pltpu.VMEM((NSA_KV, nq_all), F32), pltpu.VMEM((NSA_KV, nq_all), F32),
                        pltpu.VMEM((NSA_KV, nq_all), F32)],
        compiler_params=_params(("parallel", "arbitrary")),
        name="nsa_prompt",
    )(zb, zd, kc_aug, vc, k_aug, vs_t, kw_aug, vw, qf)


def _page_specs(shape, slot_blk, layer, n_pages):
    def mk(i):
        def imap(b, s, pt):
            return (pt[b * n_pages + s * PAGES_PER_STEP + i], layer, slot_blk, 0, 0)
        return pl.BlockSpec(shape, imap)
    return [mk(i) for i in range(PAGES_PER_STEP)]


def _nsa_dec_cmp_kernel(pt_ref, *refs, T, past_len, n_rows):
    pages = refs[:PAGES_PER_STEP]
    perm_ref, bd_ref, b_ref, q_ref, oc_ref, sel_ref, tok_s, pend_s = refs[PAGES_PER_STEP:]
    s = pl.program_id(1)
    n_steps = pl.num_programs(1)
    page_rows = pages[0].shape[-1]
    sub = page_rows // CMP_STRIDE
    R = PAGES_PER_STEP * sub

    @pl.when(s == 0)
    def _():
        pend_s[...] = jnp.zeros(pend_s.shape, F32)

    W2 = 2 * LANES
    rid = lax.broadcasted_iota(jnp.int32, (R, W2), 0)
    xs = [_dot_nt(perm_ref[...], pg[0, 0].reshape(W2, page_rows).astype(BF16)) for pg in pages]
    a0 = jnp.zeros((R, W2), F32)
    a1 = jnp.zeros((R, W2), F32)
    for j in range(CMP_STRIDE):
        xj = jnp.concatenate([x[j * sub:(j + 1) * sub] for x in xs], axis=0).astype(BF16)
        a0 = a0 + _dot(xj, bd_ref[0, j])
        a1 = a1 + _dot(xj, bd_ref[1, j])
    a0 = a0 + b_ref[...]
    tok_s[pl.ds(pl.multiple_of(s * R, R), R), :] = jnp.where(rid == 0, pend_s[...], pltpu.roll(a0, 1, axis=0)) + a1
    pend_s[...] = a0[R - 1:R, :]

    @pl.when(s == n_steps - 1)
    def _():
        qp = _head_rows(q_ref, T)
        kc = tok_s[:, 0:LANES].astype(BF16)
        vc = tok_s[:, LANES:W2].astype(BF16)
        sc_all = _dot_nt(qp, kc)
        tpos = past_len + lax.broadcasted_iota(jnp.int32, (T, 1), 0)
        r = lax.broadcasted_iota(jnp.int32, (T, n_rows), 1)
        dc = tpos - ((r - 1) * CMP_STRIDE + CMP_LEN - 1)
        mask_c = (dc >= 0) & (r >= 1)
        dcf = dc.astype(F32)
        pcs = [_masked_softmax2(sc_all[hd * T:(hd + 1) * T] - NSA_SLOPES[hd] * dcf, mask_c) for hd in range(NSA_HEADS)]
        oc_ref[0] = _dot(jnp.concatenate(pcs, axis=0).astype(BF16), vc)
        W = sel_ref.shape[2]
        n_sel = -(-(past_len + T) // SEL_LEN)
        ov = _overlap(lax.broadcasted_iota(jnp.int32, (n_rows, W), 0) - 1, lax.broadcasted_iota(jnp.int32, (n_rows, W), 1))
        for g in range(NSA_GROUPS):
            pg = pcs[g * NSA_HPG]
            for p in range(1, NSA_HPG):
                pg = pg + pcs[g * NSA_HPG + p]
            imp = _dot(pg, ov, HIGHEST)
            sel_ref[0, g * T:(g + 1) * T, :] = _select_blocks(imp, tpos, n_sel, min(SEL_TOPN, n_sel))


def _nsa_dec_sel_kernel(pt_ref, *refs, T, past_len):
    pages = refs[:PAGES_PER_STEP]
    q_ref, sel_ref, sel_last_ref, ex_ref, kn_ref, vn_ref, os_ref, m_s, l_s, acc_s = refs[PAGES_PER_STEP:]
    s = pl.program_id(1)
    n_steps = pl.num_programs(1)
    page_rows = pages[0].shape[-1]
    CH = PAGES_PER_STEP * page_rows

    @pl.when(s == 0)
    def _():
        m_s[...] = jnp.full(m_s.shape, NEG_INF, F32)
        l_s[...] = jnp.zeros(l_s.shape, F32)
        acc_s[...] = jnp.zeros(acc_s.shape, F32)

    qp = _head_rows(q_ref, T)
    tpos = past_len + lax.broadcasted_iota(jnp.int32, (T, 1), 0)

    def update(s_all, mask_of_group, kpos, pv):
        ds = tpos - kpos
        dsf = ds.astype(F32)
        ps = []
        for g in range(NSA_GROUPS):
            mk = mask_of_group(g) & (ds >= 0)
            for p in range(NSA_HPG):
                hd = g * NSA_HPG + p
                r = slice(hd * T, (hd + 1) * T)
                sc = jnp.where(mk, s_all[r] - NSA_SLOPES[hd] * dsf, NEG_INF)
                pr, m_new, alpha, l_new = _online_update(sc, m_s[r], l_s[r])
                m_s[r] = m_new
                l_s[r] = l_new
                acc_s[r] = alpha * acc_s[r]
                ps.append(pr)
        acc_s[...] = acc_s[...] + pv(jnp.concatenate(ps, axis=0).astype(BF16))

    selk = _dot(sel_ref[0, 0].astype(BF16), ex_ref[...])
    pairs = [(pages[i], pages[i + 1]) for i in range(0, PAGES_PER_STEP, 2)]
    side_by_side = lambda a, b, slot: jnp.concatenate([a[0, 0, slot], b[0, 0, slot]], axis=1).astype(BF16)
    s_all = jnp.concatenate([_dot(qp, side_by_side(a, b, 0)) for a, b in pairs], axis=1)
    kpos = s * CH + lax.broadcasted_iota(jnp.int32, (T, CH), 1)

    def pv_pages(pmat):
        out = jnp.zeros((NSA_HEADS * T, LANES), F32)
        for i, (a, b) in enumerate(pairs):
            out = out + _dot_nt(pmat[:, 2 * i * page_rows:2 * (i + 1) * page_rows], side_by_side(a, b, 1))
        return out

    update(s_all, lambda g: selk[g * T:(g + 1) * T] > 0.5, kpos, pv_pages)

    @pl.when(s == n_steps - 1)
    def _():
        nk = kn_ref.shape[1]
        sn = _dot_nt(qp, kn_ref[0])
        lane = lax.broadcasted_iota(jnp.int32, (T, nk), 1)
        sl = sel_last_ref[0, 0]
        update(sn, lambda g: (sl[g * T:(g + 1) * T, 0:1] > 0.5) & (lane < T), past_len + lane,
               lambda pmat: _dot(pmat, vn_ref[0]))
        os_ref[0] = acc_s[...] / jnp.maximum(l_s[...], 1e-30)


def _nsa_dec_win_kernel(q_ref, g_ref, oc_ref, os_ref, wp_ref, kwn_ref, vwn_ref, o_ref, *, T, past_len):
    qp = _head_rows(q_ref, T)
    w_src = wp_ref.shape[-1]
    nk = kwn_ref.shape[1]
    sw_all = jnp.concatenate([_dot(qp, wp_ref[0, 0, 0].astype(BF16)), _dot_nt(qp, kwn_ref[0])], axis=1)
    j = lax.broadcasted_iota(jnp.int32, (T, w_src + nk), 1)
    tpos = past_len + lax.broadcasted_iota(jnp.int32, (T, 1), 0)
    wpos = past_len - w_src + j
    dw = tpos - wpos
    mask_w = (dw >= 0) & (dw < WINDOW) & (wpos >= 0) & (j < w_src + T)
    dwf = dw.astype(F32)
    pws = [_masked_softmax2(sw_all[hd * T:(hd + 1) * T] - NSA_SLOPES[hd] * dwf, mask_w) for hd in range(NSA_HEADS)]
    pw = jnp.concatenate(pws, axis=0).astype(BF16)
    o_w = _dot_nt(pw[:, :w_src], wp_ref[0, 0, 1].astype(BF16)) + _dot(pw[:, w_src:], vwn_ref[0])
    _store_heads(o_ref, jax.nn.sigmoid(g_ref[0]), oc_ref[0], os_ref[0], o_w, T)


def nsa_decode(zb, zd, zcb, cache_nsa_kv, cache_win_kv, page_table, cmp_w_l, cmp_b_l, layer):
    DB, T, _ = zb.shape
    n_pool, page_rows = cache_nsa_kv.shape[:2]
    n_pages = page_table.shape[1]
    past_len = n_pages * page_rows
    n_steps = n_pages // PAGES_PER_STEP
    sub_per_page = page_rows // CMP_STRIDE
    n_rows = n_pages * sub_per_page
    assert (n_rows - 1) * CMP_STRIDE + CMP_LEN - 1 > past_len + T - 1
    assert T <= CMP_STRIDE and n_pages % PAGES_PER_STEP == 0 and page_rows == LANES
    pt = page_table.reshape(-1).astype(jnp.int32)
    cache_t = jnp.transpose(cache_nsa_kv, (0, 2, 3, 4, 5, 1)).reshape(n_pool, DEPTH, 4, NSA_KV, page_rows)
    page_blk = (1, 1, 2, NSA_KV, page_rows)

    rr = np.arange(page_rows)
    perm = jnp.asarray(rr[None, :] == ((rr % sub_per_page) * CMP_STRIDE + rr // sub_per_page)[:, None], BF16)
    bd_kv = _cmp_blockdiag(cmp_w_l)
    zero = jnp.zeros_like(bd_kv[0])
    bd = jnp.concatenate([jnp.concatenate([bd_kv[0], zero], axis=-1), jnp.concatenate([zero, bd_kv[1]], axis=-1)], axis=-2)
    bias = _cmp_bias(cmp_b_l).reshape(1, 2 * LANES)

    n_sel = -(-(past_len + T) // SEL_LEN)
    blocks_per_step = PAGES_PER_STEP * page_rows // SEL_LEN
    sel_used = (n_steps + 1) * blocks_per_step
    sel_w = -(-sel_used // LANES) * LANES
    assert sel_used >= n_sel and blocks_per_step <= LANES
    q_spec = pl.BlockSpec((1, T, NSA_QPAD), lambda b, s, pt: (b, 0, 0))
    const = lambda a: pl.BlockSpec(a.shape, lambda b, s, pt: (0,) * a.ndim)

    o_c, sel = pl.pallas_call(
        functools.partial(_nsa_dec_cmp_kernel, T=T, past_len=past_len, n_rows=n_rows),
        grid_spec=pltpu.PrefetchScalarGridSpec(
            num_scalar_prefetch=1, grid=(DB, n_steps),
            in_specs=_page_specs(page_blk, 0, layer, n_pages) + [const(perm), const(bd), const(bias), q_spec],
            out_specs=[pl.BlockSpec((1, NSA_HEADS * T, LANES), lambda b, s, pt: (b, 0, 0)),
                       pl.BlockSpec((1, NSA_GROUPS * T, sel_w), lambda b, s, pt: (b, 0, 0))],
            scratch_shapes=[pltpu.VMEM((n_rows, 2 * LANES), F32), pltpu.VMEM((1, 2 * LANES), F32)]),
        out_shape=[jax.ShapeDtypeStruct((DB, NSA_HEADS * T, LANES), F32),
                   jax.ShapeDtypeStruct((DB, NSA_GROUPS * T, sel_w), F32)],
        compiler_params=_params(("parallel", "arbitrary")),
        name="nsa_dec_cmp",
    )(pt, *([cache_t] * PAGES_PER_STEP), perm, bd, bias, zb)

    sel_steps = sel[:, :, :sel_used].reshape(DB, NSA_GROUPS * T, n_steps + 1, blocks_per_step).transpose(0, 2, 1, 3)
    sel_steps = jnp.pad(sel_steps, ((0, 0), (0, 0), (0, 0), (0, LANES - blocks_per_step)))
    kk = np.arange(PAGES_PER_STEP * page_rows) // SEL_LEN
    expand = jnp.asarray(kk[None, :] == np.arange(LANES)[:, None], BF16)
    pad_rows = LANES - T
    new_rows = lambda slot: jnp.pad(zcb[:, :, slot * NSA_KV:(slot + 1) * NSA_KV], ((0, 0), (0, pad_rows), (0, 0)))
    new_spec = pl.BlockSpec((1, LANES, LANES), lambda b, s, pt: (b, 0, 0))
    sel_blk = (1, 1, NSA_GROUPS * T, LANES)

    o_s = pl.pallas_call(
        functools.partial(_nsa_dec_sel_kernel, T=T, past_len=past_len),
        grid_spec=pltpu.PrefetchScalarGridSpec(
            num_scalar_prefetch=1, grid=(DB, n_steps),
            in_specs=_page_specs(page_blk, 1, layer, n_pages)
            + [q_spec, pl.BlockSpec(sel_blk, lambda b, s, pt: (b, s, 0, 0)),
               pl.BlockSpec(sel_blk, lambda b, s, pt: (b, n_steps, 0, 0)), const(expand), new_spec, new_spec],
            out_specs=pl.BlockSpec((1, NSA_HEADS * T, LANES), lambda b, s, pt: (b, 0, 0)),
            scratch_shapes=[pltpu.VMEM((NSA_HEADS * T, 1), F32), pltpu.VMEM((NSA_HEADS * T, 1), F32),
                            pltpu.VMEM((NSA_HEADS * T, LANES), F32)]),
        out_shape=jax.ShapeDtypeStruct((DB, NSA_HEADS * T, LANES), F32),
        compiler_params=_params(("parallel", "arbitrary")),
        name="nsa_dec_sel",
    )(pt, *([cache_t] * PAGES_PER_STEP), zb, sel_steps, sel_steps, expand, new_rows(2), new_rows(3))

    w_src = cache_win_kv.shape[1]
    win_t = jnp.transpose(cache_win_kv, (0, 2, 3, 4, 5, 1)).reshape(DB, DEPTH, 2, NSA_KV, w_src)
    b3 = lambda shp: pl.BlockSpec(shp, lambda b: (b, 0, 0))
    return pl.pallas_call(
        functools.partial(_nsa_dec_win_kernel, T=T, past_len=past_len),
        grid=(DB,),
        in_specs=[b3((1, T, NSA_QPAD)), b3((1, T, GATE_COLS)), b3((1, NSA_HEADS * T, LANES)), b3((1, NSA_HEADS * T, LANES)),
                  pl.BlockSpec((1, 1, 2, NSA_KV, w_src), lambda b: (b, layer, 0, 0, 0)),
                  b3((1, LANES, LANES)), b3((1, LANES, LANES))],
        out_specs=b3((1, T, NSA_QPAD)),
        out_shape=jax.ShapeDtypeStruct((DB, T, NSA_QPAD), BF16),
        compiler_params=_params(("parallel",)),
        name="nsa_dec_win",
    )(zb, zd, o_c, o_s, win_t, new_rows(4), new_rows(5))


def _route_top2(logits):
    lane = lax.broadcasted_iota(jnp.int32, logits.shape, 1)
    W = logits.shape[1]
    logits = jnp.where(lane < N_EXPERTS, logits, NEG_INF)
    m1 = jnp.max(logits, axis=-1, keepdims=True)
    i1 = jnp.min(jnp.where(logits == m1, lane, W), axis=-1, keepdims=True)
    rest = jnp.where(lane == i1, NEG_INF, logits)
    m2 = jnp.max(rest, axis=-1, keepdims=True)
    i2 = jnp.min(jnp.where(rest == m2, lane, W), axis=-1, keepdims=True)
    e2 = jnp.exp(m2 - m1)
    den = 1.0 + e2
    return jnp.where(lane == i1, 1.0 / den, 0.0) + jnp.where(lane == i2, e2 / den, 0.0)


def _router_kernel(x_ref, w_ref, b_ref, o_ref):
    x = x_ref[...]
    w = w_ref[...]
    x_hi = x.astype(BF16)
    x_lo = (x - x_hi.astype(F32)).astype(BF16)
    w_hi = w.astype(BF16)
    w_lo = (w - w_hi.astype(F32)).astype(BF16)
    logits = _dot(x_hi, w_hi) + _dot(x_hi, w_lo) + _dot(x_lo, w_hi)
    o_ref[...] = _route_top2(logits + b_ref[...])


def router(x2d, w_router, b_router, tm):
    n = x2d.shape[0]
    w = jnp.pad(w_router.astype(F32), ((0, 0), (0, LANES - N_EXPERTS)))
    bb = jnp.pad(b_router.astype(F32), (0, LANES - N_EXPERTS)).reshape(1, LANES)
    return pl.pallas_call(
        _router_kernel,
        grid=(n // tm,),
        in_specs=[pl.BlockSpec((tm, D_MODEL), lambda i: (i, 0)), pl.BlockSpec(w.shape, lambda i: (0, 0)),
                  pl.BlockSpec(bb.shape, lambda i: (0, 0))],
        out_specs=pl.BlockSpec((tm, LANES), lambda i: (i, 0)),
        out_shape=jax.ShapeDtypeStruct((n, LANES), F32),
        compiler_params=_params(("parallel",)),
        name="router",
    )(x2d, w, bb)


def _merge_kernel(x_ref, hm_ref, hn_ref, hx_ref, wg_ref, bg_ref, wbm_ref, wbn_ref, wbx_ref, wo_ref, g_ref, b_ref,
                  o_ref):
    x = x_ref[...]
    xb = x.astype(BF16)
    merged = None
    for c, (h_ref, w_ref) in enumerate(((hm_ref, wbm_ref), (hn_ref, wbn_ref), (hx_ref, wbx_ref))):
        sl = slice(c * D_MODEL, (c + 1) * D_MODEL)
        gate = jax.nn.sigmoid(_dot(xb, wg_ref[:, sl]) + bg_ref[:, sl])
        term = gate * _dot(h_ref[...], w_ref[...])
        merged = term if merged is None else merged + term
    mix = _dot(merged.astype(BF16), wo_ref[...])
    o_ref[...] = _layer_norm(ALPHA * x + mix, g_ref[...], b_ref[...])


def merge(x2d, hm, hn, hx, wg, bg, wbr, wo, g, b, tm):
    n = x2d.shape[0]
    row = lambda w: pl.BlockSpec((tm, w), lambda i: (i, 0))
    full = lambda a: pl.BlockSpec(a.shape, lambda i: (0,) * a.ndim, pipeline_mode=pl.Buffered(1))
    args = [x2d, hm, hn, hx, wg, bg, wbr[0], wbr[1], wbr[2], wo, g, b]
    return pl.pallas_call(
        _merge_kernel,
        grid=(n // tm,),
        in_specs=[row(a.shape[1]) for a in args[:4]] + [full(a) for a in args[4:]],
        out_specs=row(D_MODEL),
        out_shape=jax.ShapeDtypeStruct((n, D_MODEL), F32),
        compiler_params=_params(("parallel",)),
        name="merge",
    )(*args)


def _ffn_kernel(x_ref, wg_ref, wu_ref, wd_ref, g_ref, b_ref, o_ref, acc_s):
    f = pl.program_id(1)

    @pl.when(f == 0)
    def _():
        acc_s[...] = jnp.zeros(acc_s.shape, F32)

    xb = x_ref[...].astype(BF16)
    hcur = jax.nn.silu(_dot(xb, wg_ref[...])) * _dot(xb, wu_ref[...])
    acc_s[...] += _dot(hcur.astype(BF16), wd_ref[...])

    @pl.when(f == pl.num_programs(1) - 1)
    def _():
        o_ref[...] = _layer_norm(ALPHA * x_ref[...] + acc_s[...], g_ref[...], b_ref[...])


def ffn(x2d, w_up, w_down, g, b, tm, tf):
    n = x2d.shape[0]
    nf = (w_up.shape[1] // 2) // tf
    wmode = dict(pipeline_mode=pl.Buffered(1)) if nf == 1 else {}
    return pl.pallas_call(
        _ffn_kernel,
        grid=(n // tm, nf),
        in_specs=[pl.BlockSpec((tm, D_MODEL), lambda i, f: (i, 0)),
                  pl.BlockSpec((D_MODEL, tf), lambda i, f: (0, f), **wmode),
                  pl.BlockSpec((D_MODEL, tf), lambda i, f: (0, nf + f), **wmode),
                  pl.BlockSpec((tf, D_MODEL), lambda i, f: (f, 0), **wmode),
                  pl.BlockSpec((1, D_MODEL), lambda i, f: (0, 0)),
                  pl.BlockSpec((1, D_MODEL), lambda i, f: (0, 0))],
        out_specs=pl.BlockSpec((tm, D_MODEL), lambda i, f: (i, 0)),
        out_shape=jax.ShapeDtypeStruct((n, D_MODEL), F32),
        scratch_shapes=[pltpu.VMEM((tm, D_MODEL), F32)],
        compiler_params=_params(("parallel", "arbitrary")),
        name="dense_ffn",
    )(x2d, w_up, w_up, w_down, g, b)


def _moe_kernel(np_ref, x_ref, cw_ref, cwt_ref, wup_hbm, wdn_hbm, g_ref, b_ref, o_ref,
                xb_s, y_s, pos_s, post_s, xg_s, ws_s, acc_s, wg_b, wu_b, wd_b, w_sem, *, blk, cap, cap_pad, tf):
    j = pl.program_id(0)
    e = pl.program_id(1)
    f = pl.program_id(2)
    n_e = pl.num_programs(1)
    n_f = pl.num_programs(2)
    n_pass = np_ref[j * n_e + e]
    RC = min(256, blk)

    step = (j * n_e + e) * n_f + f
    n_steps = pl.num_programs(0) * n_e * n_f
    ff = n_f * tf

    def weight_copies(t):
        slot = t % MOE_WEIGHT_SLOTS
        et = (t // n_f) % n_e
        col = pl.multiple_of((t % n_f) * tf, LANES)
        return (pltpu.make_async_copy(wup_hbm.at[et, :, pl.ds(col, tf)], wg_b.at[slot], w_sem.at[slot, 0]),
                pltpu.make_async_copy(wup_hbm.at[et, :, pl.ds(ff + col, tf)], wu_b.at[slot], w_sem.at[slot, 1]),
                pltpu.make_async_copy(wdn_hbm.at[et, pl.ds(col, tf), :], wd_b.at[slot], w_sem.at[slot, 2]))

    @pl.when(step == 0)
    def _():
        for t0 in range(MOE_WEIGHT_SLOTS - 1):
            @pl.when(t0 < n_steps)
            def _():
                for cp in weight_copies(jnp.int32(t0)):
                    cp.start()

    @pl.when(step + MOE_WEIGHT_SLOTS - 1 < n_steps)
    def _():
        for cp in weight_copies(step + MOE_WEIGHT_SLOTS - 1):
            cp.start()

    for cp in weight_copies(step):
        cp.wait()
    slot = step % MOE_WEIGHT_SLOTS

    @pl.when((e == 0) & (f == 0))
    def _():
        xb_s[...] = x_ref[...].astype(BF16)
        y_s[...] = jnp.zeros(y_s.shape, F32)
        cw = cw_ref[...]
        cwt = cwt_ref[...]
        routed = jnp.where(cw != 0.0, 1.0, 0.0).astype(BF16)
        routed_t = jnp.where(cwt != 0.0, 1.0, 0.0).astype(BF16)
        for rc in range(blk // RC):
            rows_i = rc * RC + lax.broadcasted_iota(jnp.int32, (RC, blk), 0)
            cols_i = lax.broadcasted_iota(jnp.int32, (RC, blk), 1)
            before = jnp.where(cols_i < rows_i, 1.0, 0.0).astype(BF16)
            cnt = _dot(before, routed)
            pos_s[rc * RC:(rc + 1) * RC, :] = jnp.where(cw[rc * RC:(rc + 1) * RC] != 0.0, cnt, -1.0)
            rows_j = lax.broadcasted_iota(jnp.int32, (blk, RC), 0)
            cols_j = rc * RC + lax.broadcasted_iota(jnp.int32, (blk, RC), 1)
            before_t = jnp.where(rows_j < cols_j, 1.0, 0.0).astype(BF16)
            cnt_t = _dot(routed_t, before_t)
            post_s[:, rc * RC:(rc + 1) * RC] = jnp.where(cwt[:, rc * RC:(rc + 1) * RC] != 0.0, cnt_t, -1.0)

    @pl.when(f == 0)
    def _():
        prow = post_s[pl.ds(e, 1), :]
        wrow = cwt_ref[pl.ds(e, 1), :]

        def gather(u, carry):
            slot = (u * cap + lax.broadcasted_iota(jnp.int32, (cap, 1), 0)).astype(F32)
            hit = prow == slot
            xg_s[u, 0:cap, :] = _dot(jnp.where(hit, 1.0, 0.0).astype(BF16), xb_s[...]).astype(BF16)
            ws_s[u, 0:cap, :] = jnp.sum(jnp.where(hit, wrow, 0.0), axis=-1, keepdims=True)
            acc_s[u] = jnp.zeros(acc_s.shape[1:], F32)
            return carry

        lax.fori_loop(0, n_pass, gather, 0)

    def expert(u, carry):
        xg = xg_s[u, 0:cap, :]
        hcur = jax.nn.silu(_dot(xg, wg_b[slot])) * _dot(xg, wu_b[slot]) * ws_s[u, 0:cap, :]
        acc_s[u, 0:cap, :] += _dot(hcur.astype(BF16), wd_b[slot])
        return carry

    lax.fori_loop(0, n_pass, expert, 0)

    @pl.when(f == n_f - 1)
    def _():
        lane = lax.broadcasted_iota(jnp.int32, pos_s.shape, 1)
        pcol = jnp.sum(jnp.where(lane == e, pos_s[...], 0.0), axis=-1, keepdims=True)

        def scatter(u, carry):
            cc = lax.broadcasted_iota(jnp.int32, (1, cap_pad), 1)
            slot = jnp.where(cc < cap, u * cap + cc, NO_SLOT).astype(F32)
            hit = jnp.where(pcol == slot, 1.0, 0.0).astype(BF16)
            y_s[...] += _dot(hit, acc_s[u].astype(BF16))
            return carry

        lax.fori_loop(0, n_pass, scatter, 0)

    @pl.when((e == n_e - 1) & (f == n_f - 1))
    def _():
        o_ref[...] = _layer_norm(ALPHA * x_ref[...] + y_s[...], g_ref[...], b_ref[...])


def moe(x2d, cw, w_up, w_down, g, b, tf):
    n = x2d.shape[0]
    E, _, F2 = w_up.shape
    nf = (F2 // 2) // tf
    blk = _row_tile(n, MOE_BLOCK)
    cap = min(MOE_CAP, blk)
    cap_pad = -(-cap // LANES) * LANES
    max_pass = -(-blk // cap)
    nblk = n // blk
    cwt = jnp.swapaxes(cw[:, :ROUTE_ROWS], 0, 1)
    counts = jnp.sum((cw[:, :E] != 0.0).reshape(nblk, blk, E), axis=1)
    n_pass = ((counts + cap - 1) // cap).astype(jnp.int32).reshape(-1)
    return pl.pallas_call(
        functools.partial(_moe_kernel, blk=blk, cap=cap, cap_pad=cap_pad, tf=tf),
        grid_spec=pltpu.PrefetchScalarGridSpec(
            num_scalar_prefetch=1, grid=(nblk, E, nf),
            in_specs=[pl.BlockSpec((blk, D_MODEL), lambda j, e, f, npr: (j, 0), pipeline_mode=pl.Buffered(1)),
                      pl.BlockSpec((blk, LANES), lambda j, e, f, npr: (j, 0)),
                      pl.BlockSpec((ROUTE_ROWS, blk), lambda j, e, f, npr: (0, j)),
                      pl.BlockSpec(memory_space=pl.ANY), pl.BlockSpec(memory_space=pl.ANY),
                      pl.BlockSpec((1, D_MODEL), lambda j, e, f, npr: (0, 0)),
                      pl.BlockSpec((1, D_MODEL), lambda j, e, f, npr: (0, 0))],
            out_specs=pl.BlockSpec((blk, D_MODEL), lambda j, e, f, npr: (j, 0)),
            scratch_shapes=[pltpu.VMEM((blk, D_MODEL), BF16), pltpu.VMEM((blk, D_MODEL), F32),
                            pltpu.VMEM((blk, LANES), F32), pltpu.VMEM((ROUTE_ROWS, blk), F32),
                            pltpu.VMEM((max_pass, cap_pad, D_MODEL), BF16), pltpu.VMEM((max_pass, cap_pad, 1), F32),
                            pltpu.VMEM((max_pass, cap_pad, D_MODEL), F32),
                            pltpu.VMEM((MOE_WEIGHT_SLOTS, D_MODEL, tf), BF16),
                            pltpu.VMEM((MOE_WEIGHT_SLOTS, D_MODEL, tf), BF16),
                            pltpu.VMEM((MOE_WEIGHT_SLOTS, tf, D_MODEL), BF16),
                            pltpu.SemaphoreType.DMA((MOE_WEIGHT_SLOTS, 3))]),
        out_shape=jax.ShapeDtypeStruct((n, D_MODEL), F32),
        compiler_params=_params(("arbitrary", "arbitrary", "arbitrary")),
        name="moe_ffn",
    )(n_pass, x2d, cw, cwt, w_up, w_down, g, b)


def _layer(x, lw, l, *, mem_kv, mem_layer, mstate, decode):
    B, T, _ = x.shape
    n = B * T
    x2d = x.reshape(n, D_MODEL)
    tm = _row_tile(n, 1024)
    za, zb, zc, zcb, zd = in_proj(x2d, lw["in"], tm)
    za, zb, zc, zcb, zd = (a.reshape(B, T, -1) for a in (za, zb, zc, zcb, zd))

    if T % MLSTM_CHUNK == 0:
        h_m, ct, nt, mt = mlstm(za, zd, *mstate, lw["norm_g"], L=MLSTM_CHUNK, t_valid=MLSTM_CHUNK)
    else:
        padt = lambda a: jnp.pad(a, ((0, 0), (0, LANES - T), (0, 0)))
        h_m, ct, nt, mt = mlstm(padt(za), padt(zd), *mstate, lw["norm_g"], L=LANES, t_valid=T)
        h_m = h_m[:, :T]

    if decode is None:
        kc, vc = cmp_tokens(zc, lw["cmp_w"], lw["cmp_b"])
        h_n = nsa_prompt(zb, zd, zcb, kc, vc, tq=min(256, T), CH=min(512, T))
    else:
        h_n = nsa_decode(zb, zd, zcb, decode["nsa"], decode["win"], decode["pt"], lw["cmp_w"], lw["cmp_b"], l)

    h_x = mem_attn(zb, mem_kv, mem_layer, tq=_row_tile(T, 512))

    wg, bg = lw["in"]["mg"]
    flat = lambda a: a.reshape(n, -1)
    x1 = merge(x2d, flat(h_m), flat(h_n), flat(h_x), wg, bg, lw["w_branch"], lw["w_out"], lw["ln1_g"], lw["ln1_b"], tm)

    if l % 2 == 0:
        x2 = ffn(x1, lw["ffn_up"], lw["ffn_down"], lw["ln2_g"], lw["ln2_b"], _row_tile(n, 512), D_FF)
    else:
        cw = router(x1, lw["w_router"], lw["b_router"], _row_tile(n, 1024))
        x2 = moe(x1, cw, lw["ffn_up"], lw["ffn_down"], lw["ln2_g"], lw["ln2_b"], D_FF_EXPERT // 4)
    return x2.reshape(B, T, D_MODEL), zc, (ct, nt, mt)


def kernel(x_prompt, x_sample, mem_prompt, cache_nsa_kv, cache_win_kv, state_mlstm_C, state_mlstm_n,
           state_mlstm_m, cache_mem_kv, page_table, w_in, b_in, mlstm_norm_g, cmp_w, cmp_b, w_mem_kv,
           w_branch, w_out, ln1_g, ln1_b, ln2_g, ln2_b, ffn_w_up, ffn_w_down, moe_w_router, moe_b_router,
           moe_w_up, moe_w_down):
    B, T, _ = x_prompt.shape
    DB, TS, _ = x_sample.shape
    xp, xs = x_prompt, x_sample
    nsa_p, nsa_s, win_p, win_s = [], [], [], []
    Cp, np_, mp, Cs, ns, ms, memkv_p = [], [], [], [], [], [], []
    row1 = lambda a: a.reshape(1, -1).astype(F32)
    for l in range(DEPTH):
        wbr = (w_branch[l, 0].astype(BF16), _pad_branch_rows(w_branch[l, 1]).astype(BF16), w_branch[l, 2].astype(BF16))
        lw = dict(norm_g=mlstm_norm_g[l], cmp_w=cmp_w[l], cmp_b=cmp_b[l], w_branch=wbr,
                  w_out=w_out[l].astype(BF16), ln1_g=row1(ln1_g[l]), ln1_b=row1(ln1_b[l]),
                  ln2_g=row1(ln2_g[l]), ln2_b=row1(ln2_b[l]))
        lw["in"] = _split_in_proj(w_in[l], b_in[l])
        if l % 2 == 0:
            lw["ffn_up"] = ffn_w_up[l // 2].astype(BF16)
            lw["ffn_down"] = ffn_w_down[l // 2].astype(BF16)
        else:
            lw["ffn_up"] = moe_w_up[l // 2].astype(BF16)
            lw["ffn_down"] = moe_w_down[l // 2].astype(BF16)
            lw["w_router"] = moe_w_router[l // 2]
            lw["b_router"] = moe_b_router[l // 2]

        mkv = proj(mem_prompt.reshape(B * MEM_LEN, D_MODEL), w_mem_kv[l].astype(BF16), _row_tile(B * MEM_LEN, 512))
        mkv = mkv.reshape(B, MEM_LEN, 2 * X_WIDTH)
        st0 = (jnp.zeros((B, M_HEADS, M_DH, M_DH), F32), jnp.zeros((B, M_HEADS, M_DH), F32), jnp.zeros((B, M_HEADS), F32))
        xp, zc, st = _layer(xp, lw, l, mem_kv=mkv, mem_layer=0, mstate=st0, decode=None)
        kvn = zc.reshape(B, T, 6, NSA_GROUPS, NSA_DH)
        nsa_p.append(kvn[:, :, :4])
        win_p.append(kvn[:, T - min(WINDOW, T):, 4:])
        Cp.append(st[0]); np_.append(st[1]); mp.append(st[2])
        memkv_p.append(mkv.reshape(B, MEM_LEN, 2, X_HEADS, X_DH))

        sts = (state_mlstm_C[:, l], state_mlstm_n[:, l], state_mlstm_m[:, l])
        dec = dict(nsa=cache_nsa_kv, win=cache_win_kv, pt=page_table)
        xs, zc, st = _layer(xs, lw, l, mem_kv=cache_mem_kv, mem_layer=l, mstate=sts, decode=dec)
        kvn = zc.reshape(DB, TS, 6, NSA_GROUPS, NSA_DH)
        nsa_s.append(kvn[:, :, :4])
        win_s.append(jnp.concatenate([cache_win_kv[:, :, l].astype(F32), kvn[:, :, 4:]], axis=1)[:, TS:])
        Cs.append(st[0]); ns.append(st[1]); ms.append(st[2])
    return (xp, xs,
            jnp.stack(nsa_p, axis=2), jnp.stack(nsa_s, axis=2),
            jnp.stack(win_p, axis=2), jnp.stack(win_s, axis=2),
            jnp.stack(Cp, axis=1), jnp.stack(np_, axis=1), jnp.stack(mp, axis=1),
            jnp.stack(Cs, axis=1), jnp.stack(ns, axis=1), jnp.stack(ms, axis=1),
            jnp.stack(memkv_p, axis=2))
```
